```python
import jax, jax.numpy as jnp
from jax import lax
import numpy as np

D_MODEL = 1024
BATCH = 8
SEQ = 4096
DEPTH = 2

N_EVEN = (DEPTH + 1) // 2
N_ODD = DEPTH // 2
RG_WIDTH = D_MODEL // 2
RG_BLOCKS = 8
RG_BLOCK_DIM = RG_WIDTH // RG_BLOCKS
CONV_WIDTH = 4
CONV_LEFT = 2
CONV_RIGHT = 1
RG_C = 8.0
HG_WIDTH = D_MODEL // 2
HG_HEADS = 4
HG_HEAD_DIM = HG_WIDTH // HG_HEADS
GLA_HEADS = 4
GLA_KEY_DIM = D_MODEL // 2
GLA_VALUE_DIM = D_MODEL
GLA_HEAD_K = GLA_KEY_DIM // GLA_HEADS
GLA_HEAD_V = GLA_VALUE_DIM // GLA_HEADS
GLA_LOWRANK = 16
GLA_GATE_NORMALIZER = 16.0
D_FF = 4 * D_MODEL
CHUNK = 64
EPS = 1e-6
AB_IN_SIZES = (RG_WIDTH, RG_WIDTH, HG_WIDTH, HG_WIDTH, HG_WIDTH, HG_WIDTH, HG_WIDTH)
AB_IN_WIDTH = 2 * RG_WIDTH + 5 * HG_WIDTH
AB_OUT_WIDTH = RG_WIDTH + HG_WIDTH
GLA_IN_SIZES = (GLA_KEY_DIM, GLA_KEY_DIM, GLA_VALUE_DIM, GLA_VALUE_DIM, GLA_LOWRANK, GLA_LOWRANK)
GLA_IN_WIDTH = 2 * GLA_KEY_DIM + 2 * GLA_VALUE_DIM + 2 * GLA_LOWRANK

kernel_name = "bidir_hybrid_rglru_hgrn2_gla_trunk"


def split_cols(t, sizes):
    offsets = []
    acc = 0
    for s in sizes[:-1]:
        acc += s
        offsets.append(acc)
    return jnp.split(t, offsets, axis=-1)


def rmsnorm(x, gain):
    x32 = x.astype(jnp.float32)
    y = x32 * lax.rsqrt(jnp.mean(x32 * x32, axis=-1, keepdims=True) + EPS)
    return (y * gain.astype(jnp.float32)).astype(x.dtype)


def to_heads(t, n_heads):
    b, s, w = t.shape
    return t.reshape(b, s, n_heads, w // n_heads).transpose(0, 2, 1, 3)


def head_rmsnorm(o, gain):
    b, h, s, d = o.shape
    o32 = o.astype(jnp.float32)
    o32 = o32 * lax.rsqrt(jnp.mean(o32 * o32, axis=-1, keepdims=True) + EPS)
    o32 = o32.transpose(0, 2, 1, 3).reshape(b, s, h * d)
    return (o32 * gain.astype(jnp.float32)).astype(o.dtype)


def chunk_gated_linear_attn(q, k, v, logf):
    b, h, s, dk = q.shape
    dv = v.shape[-1]
    n = s // CHUNK
    qc = q.reshape(b, h, n, CHUNK, dk)
    kc = k.reshape(b, h, n, CHUNK, dk)
    vc = v.reshape(b, h, n, CHUNK, dv)
    cum = jnp.cumsum(logf.astype(jnp.float32).reshape(b, h, n, CHUNK, dk), axis=3)
    ref = cum[:, :, :, CHUNK // 2:CHUNK // 2 + 1, :]
    last = cum[:, :, :, CHUNK - 1:, :]
    q_in = qc * jnp.exp(cum - ref)
    k_in = kc * jnp.exp(ref - cum)
    scores = jnp.einsum('bhnld,bhnmd->bhnlm', q_in, k_in)
    mask = jnp.tril(jnp.ones((CHUNK, CHUNK), dtype=bool))
    scores = jnp.where(mask, scores, 0.0)
    o_intra = jnp.einsum('bhnlm,bhnmv->bhnlv', scores, vc)
    k_state = kc * jnp.exp(last - cum)
    upd = jnp.einsum('bhnld,bhnlv->bhndv', k_state, vc)
    decay = jnp.exp(last[:, :, :, 0, :])

    def step(state, inp):
        d, u = inp
        return d[..., None] * state + u, state

    init = jnp.zeros((b, h, dk, dv), dtype=upd.dtype)
    _, s_prev = lax.scan(step, init, (jnp.moveaxis(decay, 2, 0), jnp.moveaxis(upd, 2, 0)))
    o_inter = jnp.einsum('bhnld,nbhdv->bhnlv', qc * jnp.exp(cum), s_prev)
    return (o_intra + o_inter).reshape(b, h, s, dv)


def bidir_gated_linear_attn(q, k_f, k_b, v, logf_f, logf_b):
    rev = lambda t: jnp.flip(t, axis=2)
    fwd = chunk_gated_linear_attn(q, k_f, v, logf_f)
    bwd = rev(chunk_gated_linear_attn(rev(q), rev(k_b), rev(v), rev(logf_b)))
    return fwd + bwd


def linear_scan_combine(left, right):
    a1, b1 = left
    a2, b2 = right
    return a1 * a2, a2 * b1 + b2


def rglru_branch(xa, ga, conv_w, conv_b, w_a, b_a, w_x, b_x, lam):
    bsz, s, _ = xa.shape
    xc = lax.conv_general_dilated(
        xa, conv_w[:, None, :], window_strides=(1,), padding=[(CONV_LEFT, CONV_RIGHT)],
        dimension_numbers=('NWC', 'WIO', 'NWC'), feature_group_count=RG_WIDTH) + conv_b
    xb = xc.reshape(bsz, s, RG_BLOCKS, RG_BLOCK_DIM)
    r = jax.nn.sigmoid(jnp.einsum('bsgi,dgij->dbsgj', xb, w_a).reshape(2, bsz, s, RG_WIDTH)
                       + b_a[:, None, None, :])
    i = jax.nn.sigmoid(jnp.einsum('bsgi,dgij->dbsgj', xb, w_x).reshape(2, bsz, s, RG_WIDTH)
                       + b_x[:, None, None, :])
    log_a = -RG_C * r * jax.nn.softplus(-lam)[:, None, None, :]
    a = jnp.exp(log_a)
    u = jnp.sqrt(-jnp.expm1(2.0 * log_a)) * (i * xc[None])
    _, h_f = lax.associative_scan(linear_scan_combine, (a[0], u[0]), axis=1)
    _, h_b = lax.associative_scan(linear_scan_combine, (a[1], u[1]), axis=1, reverse=True)
    return (h_f + h_b) * jax.nn.gelu(ga)


def hgrn2_branch(q, f_f, f_b, iv, g, lb, norm_gain):
    qh = to_heads(jax.nn.silu(q), HG_HEADS)
    vh = to_heads(iv, HG_HEADS)

    def gates(fpre):
        fpre32 = fpre.astype(jnp.float32)
        logf = jnp.log(lb + (1.0 - lb) * jax.nn.sigmoid(fpre32))
        k = (1.0 - lb) * jax.nn.sigmoid(-fpre32)
        return to_heads(k, HG_HEADS), to_heads(logf, HG_HEADS)

    k_f, logf_f = gates(f_f)
    k_b, logf_b = gates(f_b)
    o = bidir_gated_linear_attn(qh, k_f, k_b, vh, logf_f, logf_b)
    return head_rmsnorm(o, norm_gain) * jax.nn.silu(g)


def gla_mixer(y, w_in, w_gate_up, b_gate, norm_gain):
    proj = y @ w_in
    q, k, v, r, lr_f, lr_b = split_cols(proj, GLA_IN_SIZES)
    qh = to_heads(q, GLA_HEADS) * (GLA_HEAD_K ** -0.5)
    kh = to_heads(k, GLA_HEADS)
    vh = to_heads(v, GLA_HEADS)

    def log_gate(lr, w_up, bias):
        z = jnp.einsum('bsr,rk->bsk', lr, w_up) + bias
        return to_heads(jax.nn.log_sigmoid(z.astype(jnp.float32)) / GLA_GATE_NORMALIZER, GLA_HEADS)

    logf_f = log_gate(lr_f, w_gate_up[0], b_gate[0])
    logf_b = log_gate(lr_b, w_gate_up[1], b_gate[1])
    o = bidir_gated_linear_attn(qh, kh, kh, vh, logf_f, logf_b)
    return head_rmsnorm(o, norm_gain) * jax.nn.silu(r)


def _fwd_setup_inputs(seed: int = 0) -> dict:
    key = jax.random.key(seed)
    ks = jax.random.split(key, 24)
    nrm = lambda k, shape, scale: jax.random.normal(k, shape, jnp.float32) * scale
    u = jax.random.uniform(ks[10], (N_EVEN, 2, RG_WIDTH), jnp.float32, minval=0.9, maxval=0.999)
    p = u ** (1.0 / RG_C)
    rg_lambda = jnp.log(p) - jnp.log1p(-p)
    return {
        "x": nrm(ks[0], (BATCH, SEQ, D_MODEL), 1.0),
        "norm_mix": 1.0 + nrm(ks[1], (DEPTH, D_MODEL), 0.02),
        "norm_mlp": 1.0 + nrm(ks[2], (DEPTH, D_MODEL), 0.02),
        "norm_final": 1.0 + nrm(ks[3], (D_MODEL,), 0.02),
        "mlp_w1": nrm(ks[4], (DEPTH, D_MODEL, D_FF), D_MODEL ** -0.5),
        "mlp_w2": nrm(ks[5], (DEPTH, D_FF, D_MODEL), D_FF ** -0.5),
        "ab_w_in": nrm(ks[6], (N_EVEN, D_MODEL, AB_IN_WIDTH), D_MODEL ** -0.5),
        "ab_w_out": nrm(ks[7], (N_EVEN, AB_OUT_WIDTH, D_MODEL), AB_OUT_WIDTH ** -0.5),
        "rg_conv_w": nrm(ks[8], (N_EVEN, CONV_WIDTH, RG_WIDTH), CONV_WIDTH ** -0.5),
        "rg_conv_b": nrm(ks[9], (N_EVEN, RG_WIDTH), 0.01),
        "rg_w_a": nrm(ks[11], (N_EVEN, 2, RG_BLOCKS, RG_BLOCK_DIM, RG_BLOCK_DIM), RG_BLOCK_DIM ** -0.5),
        "rg_b_a": nrm(ks[12], (N_EVEN, 2, RG_WIDTH), 0.01),
        "rg_w_x": nrm(ks[13], (N_EVEN, 2, RG_BLOCKS, RG_BLOCK_DIM, RG_BLOCK_DIM), RG_BLOCK_DIM ** -0.5),
        "rg_b_x": nrm(ks[14], (N_EVEN, 2, RG_WIDTH), 0.01),
        "rg_lambda": rg_lambda,
        "hg_lb_logits": nrm(ks[15], (N_EVEN + 1, HG_WIDTH), 0.1),
        "hg_norm": 1.0 + nrm(ks[16], (N_EVEN, HG_WIDTH), 0.02),
        "gla_w_in": nrm(ks[17], (N_ODD, D_MODEL, GLA_IN_WIDTH), D_MODEL ** -0.5),
        "gla_w_out": nrm(ks[18], (N_ODD, GLA_VALUE_DIM, D_MODEL), GLA_VALUE_DIM ** -0.5),
        "gla_w_gate_up": nrm(ks[19], (N_ODD, 2, GLA_LOWRANK, GLA_KEY_DIM), GLA_LOWRANK ** -0.5),
        "gla_b_gate": nrm(ks[20], (N_ODD, 2, GLA_KEY_DIM), 0.01),
        "gla_norm": 1.0 + nrm(ks[21], (N_ODD, GLA_VALUE_DIM), 0.02),
    }


def _fwd_reference(x, norm_mix, norm_mlp, norm_final, mlp_w1, mlp_w2, ab_w_in, ab_w_out,
              rg_conv_w, rg_conv_b, rg_w_a, rg_b_a, rg_w_x, rg_b_x, rg_lambda,
              hg_lb_logits, hg_norm, gla_w_in, gla_w_out, gla_w_gate_up, gla_b_gate, gla_norm):
    lbs = jnp.cumsum(jax.nn.softmax(hg_lb_logits.astype(jnp.float32), axis=0), axis=0)
    h = x
    for layer in range(DEPTH):
        j = layer // 2
        y = rmsnorm(h, norm_mix[layer])
        if layer % 2 == 0:
            proj = y @ ab_w_in[j]
            xa, ga, q, f_f, f_b, iv, g = split_cols(proj, AB_IN_SIZES)
            ya = rglru_branch(xa, ga, rg_conv_w[j], rg_conv_b[j], rg_w_a[j], rg_b_a[j],
                              rg_w_x[j], rg_b_x[j], rg_lambda[j])
            yb = hgrn2_branch(q, f_f, f_b, iv, g, lbs[j], hg_norm[j])
            mix = jnp.concatenate([ya, yb], axis=-1) @ ab_w_out[j]
        else:
            mix = gla_mixer(y, gla_w_in[j], gla_w_gate_up[j], gla_b_gate[j], gla_norm[j]) @ gla_w_out[j]
        h = h + mix
        y = rmsnorm(h, norm_mlp[layer])
        h = h + jnp.square(jax.nn.relu(y @ mlp_w1[layer])) @ mlp_w2[layer]
    return rmsnorm(h, norm_final)


import jax as _jax
import jax.numpy as _jnp

TWIN_FORMAT = 'train_step'
FWD_PARAMS = ['x', 'norm_mix', 'norm_mlp', 'norm_final', 'mlp_w1', 'mlp_w2', 'ab_w_in', 'ab_w_out', 'rg_conv_w', 'rg_conv_b', 'rg_w_a', 'rg_b_a', 'rg_w_x', 'rg_b_x', 'rg_lambda', 'hg_lb_logits', 'hg_norm', 'gla_w_in', 'gla_w_out', 'gla_w_gate_up', 'gla_b_gate', 'gla_norm']
TWIN_WEIGHTS = ['norm_mix', 'norm_mlp', 'norm_final', 'mlp_w1', 'mlp_w2', 'ab_w_in', 'ab_w_out', 'rg_conv_w', 'rg_conv_b', 'rg_w_a', 'rg_b_a', 'rg_w_x', 'rg_b_x', 'rg_lambda', 'hg_lb_logits', 'hg_norm', 'gla_w_in', 'gla_w_out', 'gla_w_gate_up', 'gla_b_gate', 'gla_norm']
TWIN_DIFF_INPUT = 'x'
TWIN_INPUTS = ['x', 'norm_mix', 'norm_mlp', 'norm_final', 'mlp_w1', 'mlp_w2', 'ab_w_in', 'ab_w_out', 'rg_conv_w', 'rg_conv_b', 'rg_w_a', 'rg_b_a', 'rg_w_x', 'rg_b_x', 'rg_lambda', 'hg_lb_logits', 'hg_norm', 'gla_w_in', 'gla_w_out', 'gla_w_gate_up', 'gla_b_gate', 'gla_norm', 'loss_target', 'm_norm_mix', 'm_norm_mlp', 'm_norm_final', 'm_mlp_w1', 'm_mlp_w2', 'm_ab_w_in', 'm_ab_w_out', 'm_rg_conv_w', 'm_rg_conv_b', 'm_rg_w_a', 'm_rg_b_a', 'm_rg_w_x', 'm_rg_b_x', 'm_rg_lambda', 'm_hg_lb_logits', 'm_hg_norm', 'm_gla_w_in', 'm_gla_w_out', 'm_gla_w_gate_up', 'm_gla_b_gate', 'm_gla_norm', 'v_norm_mix', 'v_norm_mlp', 'v_norm_final', 'v_mlp_w1', 'v_mlp_w2', 'v_ab_w_in', 'v_ab_w_out', 'v_rg_conv_w', 'v_rg_conv_b', 'v_rg_w_a', 'v_rg_b_a', 'v_rg_w_x', 'v_rg_b_x', 'v_rg_lambda', 'v_hg_lb_logits', 'v_hg_norm', 'v_gla_w_in', 'v_gla_w_out', 'v_gla_w_gate_up', 'v_gla_b_gate', 'v_gla_norm']
TWIN_OUTPUTS = ['loss', 'grad_x', 'grad_norm_mix', 'grad_norm_mlp', 'grad_norm_final', 'grad_mlp_w1', 'grad_mlp_w2', 'grad_ab_w_in', 'grad_ab_w_out', 'grad_rg_conv_w', 'grad_rg_conv_b', 'grad_rg_w_a', 'grad_rg_b_a', 'grad_rg_w_x', 'grad_rg_b_x', 'grad_rg_lambda', 'grad_hg_lb_logits', 'grad_hg_norm', 'grad_gla_w_in', 'grad_gla_w_out', 'grad_gla_w_gate_up', 'grad_gla_b_gate', 'grad_gla_norm', 'delta_norm_mix', 'delta_norm_mlp', 'delta_norm_final', 'delta_mlp_w1', 'delta_mlp_w2', 'delta_ab_w_in', 'delta_ab_w_out', 'delta_rg_conv_w', 'delta_rg_conv_b', 'delta_rg_w_a', 'delta_rg_b_a', 'delta_rg_w_x', 'delta_rg_b_x', 'delta_rg_lambda', 'delta_hg_lb_logits', 'delta_hg_norm', 'delta_gla_w_in', 'delta_gla_w_out', 'delta_gla_w_gate_up', 'delta_gla_b_gate', 'delta_gla_norm', 'new_m_norm_mix', 'new_m_norm_mlp', 'new_m_norm_final', 'new_m_mlp_w1', 'new_m_mlp_w2', 'new_m_ab_w_in', 'new_m_ab_w_out', 'new_m_rg_conv_w', 'new_m_rg_conv_b', 'new_m_rg_w_a', 'new_m_rg_b_a', 'new_m_rg_w_x', 'new_m_rg_b_x', 'new_m_rg_lambda', 'new_m_hg_lb_logits', 'new_m_hg_norm', 'new_m_gla_w_in', 'new_m_gla_w_out', 'new_m_gla_w_gate_up', 'new_m_gla_b_gate', 'new_m_gla_norm', 'new_v_norm_mix', 'new_v_norm_mlp', 'new_v_norm_final', 'new_v_mlp_w1', 'new_v_mlp_w2', 'new_v_ab_w_in', 'new_v_ab_w_out', 'new_v_rg_conv_w', 'new_v_rg_conv_b', 'new_v_rg_w_a', 'new_v_rg_b_a', 'new_v_rg_w_x', 'new_v_rg_b_x', 'new_v_rg_lambda', 'new_v_hg_lb_logits', 'new_v_hg_norm', 'new_v_gla_w_in', 'new_v_gla_w_out', 'new_v_gla_w_gate_up', 'new_v_gla_b_gate', 'new_v_gla_norm']
TWIN_LEAF_KINDS = {'loss': 'loss', 'grad_x': 'grad_x', 'grad_norm_mix': 'grad_w', 'grad_norm_mlp': 'grad_w', 'grad_norm_final': 'grad_w', 'grad_mlp_w1': 'grad_w', 'grad_mlp_w2': 'grad_w', 'grad_ab_w_in': 'grad_w', 'grad_ab_w_out': 'grad_w', 'grad_rg_conv_w': 'grad_w', 'grad_rg_conv_b': 'grad_w', 'grad_rg_w_a': 'grad_w', 'grad_rg_b_a': 'grad_w', 'grad_rg_w_x': 'grad_w', 'grad_rg_b_x': 'grad_w', 'grad_rg_lambda': 'grad_w', 'grad_hg_lb_logits': 'grad_w', 'grad_hg_norm': 'grad_w', 'grad_gla_w_in': 'grad_w', 'grad_gla_w_out': 'grad_w', 'grad_gla_w_gate_up': 'grad_w', 'grad_gla_b_gate': 'grad_w', 'grad_gla_norm': 'grad_w', 'delta_norm_mix': 'delta_w', 'delta_norm_mlp': 'delta_w', 'delta_norm_final': 'delta_w', 'delta_mlp_w1': 'delta_w', 'delta_mlp_w2': 'delta_w', 'delta_ab_w_in': 'delta_w', 'delta_ab_w_out': 'delta_w', 'delta_rg_conv_w': 'delta_w', 'delta_rg_conv_b': 'delta_w', 'delta_rg_w_a': 'delta_w', 'delta_rg_b_a': 'delta_w', 'delta_rg_w_x': 'delta_w', 'delta_rg_b_x': 'delta_w', 'delta_rg_lambda': 'delta_w', 'delta_hg_lb_logits': 'delta_w', 'delta_hg_norm': 'delta_w', 'delta_gla_w_in': 'delta_w', 'delta_gla_w_out': 'delta_w', 'delta_gla_w_gate_up': 'delta_w', 'delta_gla_b_gate': 'delta_w', 'delta_gla_norm': 'delta_w', 'new_m_norm_mix': 'new_m', 'new_m_norm_mlp': 'new_m', 'new_m_norm_final': 'new_m', 'new_m_mlp_w1': 'new_m', 'new_m_mlp_w2': 'new_m', 'new_m_ab_w_in': 'new_m', 'new_m_ab_w_out': 'new_m', 'new_m_rg_conv_w': 'new_m', 'new_m_rg_conv_b': 'new_m', 'new_m_rg_w_a': 'new_m', 'new_m_rg_b_a': 'new_m', 'new_m_rg_w_x': 'new_m', 'new_m_rg_b_x': 'new_m', 'new_m_rg_lambda': 'new_m', 'new_m_hg_lb_logits': 'new_m', 'new_m_hg_norm': 'new_m', 'new_m_gla_w_in': 'new_m', 'new_m_gla_w_out': 'new_m', 'new_m_gla_w_gate_up': 'new_m', 'new_m_gla_b_gate': 'new_m', 'new_m_gla_norm': 'new_m', 'new_v_norm_mix': 'new_v', 'new_v_norm_mlp': 'new_v', 'new_v_norm_final': 'new_v', 'new_v_mlp_w1': 'new_v', 'new_v_mlp_w2': 'new_v', 'new_v_ab_w_in': 'new_v', 'new_v_ab_w_out': 'new_v', 'new_v_rg_conv_w': 'new_v', 'new_v_rg_conv_b': 'new_v', 'new_v_rg_w_a': 'new_v', 'new_v_rg_b_a': 'new_v', 'new_v_rg_w_x': 'new_v', 'new_v_rg_b_x': 'new_v', 'new_v_rg_lambda': 'new_v', 'new_v_hg_lb_logits': 'new_v', 'new_v_hg_norm': 'new_v', 'new_v_gla_w_in': 'new_v', 'new_v_gla_w_out': 'new_v', 'new_v_gla_w_gate_up': 'new_v', 'new_v_gla_b_gate': 'new_v', 'new_v_gla_norm': 'new_v'}


def _forward(args):
    return _fwd_reference(*[args[k] for k in FWD_PARAMS])


def _output_shape():
    out = _jax.eval_shape(lambda: _forward(_fwd_setup_inputs(0)))
    return out.shape, out.dtype

N_MICROBATCH = 1
ADAM_LR = 0.001
ADAM_B1 = 0.9
ADAM_B2 = 0.999
ADAM_EPS = 1e-08
ADAM_WD = 0.01
ADAM_STEP = 10
PER_EXAMPLE_BATCH_AXIS = {'x': 0, 'loss_target': 0}
SHARED_INPUTS = []
_WEIGHT_DTYPES = {'norm_mix': _jnp.float32, 'norm_mlp': _jnp.float32, 'norm_final': _jnp.float32, 'mlp_w1': _jnp.float32, 'mlp_w2': _jnp.float32, 'ab_w_in': _jnp.float32, 'ab_w_out': _jnp.float32, 'rg_conv_w': _jnp.float32, 'rg_conv_b': _jnp.float32, 'rg_w_a': _jnp.float32, 'rg_b_a': _jnp.float32, 'rg_w_x': _jnp.float32, 'rg_b_x': _jnp.float32, 'rg_lambda': _jnp.float32, 'hg_lb_logits': _jnp.float32, 'hg_norm': _jnp.float32, 'gla_w_in': _jnp.float32, 'gla_w_out': _jnp.float32, 'gla_w_gate_up': _jnp.float32, 'gla_b_gate': _jnp.float32, 'gla_norm': _jnp.float32}
MOMENT_SCALE = {'norm_mix': 1.723874e-01, 'norm_mlp': 1.541888e-01, 'norm_final': 3.241887e+01, 'mlp_w1': 7.591523e-02, 'mlp_w2': 1.467467e-01, 'ab_w_in': 9.247644e-02, 'ab_w_out': 1.263222e-01, 'rg_conv_w': 1.512815e-01, 'rg_conv_b': 2.274538e+00, 'rg_w_a': 3.666935e-02, 'rg_b_a': 3.637787e-02, 'rg_w_x': 6.841646e-02, 'rg_b_x': 3.464598e-02, 'rg_lambda': 4.662301e-02, 'hg_lb_logits': 1.012799e-02, 'hg_norm': 1.256180e-01, 'gla_w_in': 7.843209e-02, 'gla_w_out': 6.260772e-02, 'gla_w_gate_up': 1.125682e-02, 'gla_b_gate': 3.117248e-02, 'gla_norm': 6.862004e-02}


def _to_microbatches(a, axis):
    t = _jnp.moveaxis(a, axis, 0)
    t = t.reshape((N_MICROBATCH, t.shape[0] // N_MICROBATCH) + t.shape[1:])
    return _jnp.moveaxis(t, 1, axis + 1)


def setup_inputs(seed: int = 0) -> dict:
    inp = _fwd_setup_inputs(seed)
    key = _jax.random.fold_in(_jax.random.key(seed), 7919)
    shape, _ = _output_shape()
    out = dict(inp)
    out["loss_target"] = _jax.random.normal(_jax.random.fold_in(key, 0), shape, _jnp.float32)
    for i, name in enumerate(TWIN_WEIGHTS):
        w = inp[name].astype(_jnp.float32)
        if MOMENT_SCALE is None:
            s = _jnp.sqrt(_jnp.mean(_jnp.square(w)) + 1e-30)
        else:
            s = MOMENT_SCALE[name]
        km, kv = _jax.random.split(_jax.random.fold_in(key, i + 1))
        out[name] = w
        out["m_" + name] = s * _jax.random.normal(km, w.shape, _jnp.float32)
        out["v_" + name] = (s * s) * _jax.random.uniform(kv, w.shape, _jnp.float32, 0.5, 1.5)
    if N_MICROBATCH > 1:
        for name, axis in PER_EXAMPLE_BATCH_AXIS.items():
            out[name] = _to_microbatches(out[name], axis)
    return {'x': out['x'], 'norm_mix': out['norm_mix'], 'norm_mlp': out['norm_mlp'], 'norm_final': out['norm_final'], 'mlp_w1': out['mlp_w1'], 'mlp_w2': out['mlp_w2'], 'ab_w_in': out['ab_w_in'], 'ab_w_out': out['ab_w_out'], 'rg_conv_w': out['rg_conv_w'], 'rg_conv_b': out['rg_conv_b'], 'rg_w_a': out['rg_w_a'], 'rg_b_a': out['rg_b_a'], 'rg_w_x': out['rg_w_x'], 'rg_b_x': out['rg_b_x'], 'rg_lambda': out['rg_lambda'], 'hg_lb_logits': out['hg_lb_logits'], 'hg_norm': out['hg_norm'], 'gla_w_in': out['gla_w_in'], 'gla_w_out': out['gla_w_out'], 'gla_w_gate_up': out['gla_w_gate_up'], 'gla_b_gate': out['gla_b_gate'], 'gla_norm': out['gla_norm'], 'loss_target': out['loss_target'], 'm_norm_mix': out['m_norm_mix'], 'm_norm_mlp': out['m_norm_mlp'], 'm_norm_final': out['m_norm_final'], 'm_mlp_w1': out['m_mlp_w1'], 'm_mlp_w2': out['m_mlp_w2'], 'm_ab_w_in': out['m_ab_w_in'], 'm_ab_w_out': out['m_ab_w_out'], 'm_rg_conv_w': out['m_rg_conv_w'], 'm_rg_conv_b': out['m_rg_conv_b'], 'm_rg_w_a': out['m_rg_w_a'], 'm_rg_b_a': out['m_rg_b_a'], 'm_rg_w_x': out['m_rg_w_x'], 'm_rg_b_x': out['m_rg_b_x'], 'm_rg_lambda': out['m_rg_lambda'], 'm_hg_lb_logits': out['m_hg_lb_logits'], 'm_hg_norm': out['m_hg_norm'], 'm_gla_w_in': out['m_gla_w_in'], 'm_gla_w_out': out['m_gla_w_out'], 'm_gla_w_gate_up': out['m_gla_w_gate_up'], 'm_gla_b_gate': out['m_gla_b_gate'], 'm_gla_norm': out['m_gla_norm'], 'v_norm_mix': out['v_norm_mix'], 'v_norm_mlp': out['v_norm_mlp'], 'v_norm_final': out['v_norm_final'], 'v_mlp_w1': out['v_mlp_w1'], 'v_mlp_w2': out['v_mlp_w2'], 'v_ab_w_in': out['v_ab_w_in'], 'v_ab_w_out': out['v_ab_w_out'], 'v_rg_conv_w': out['v_rg_conv_w'], 'v_rg_conv_b': out['v_rg_conv_b'], 'v_rg_w_a': out['v_rg_w_a'], 'v_rg_b_a': out['v_rg_b_a'], 'v_rg_w_x': out['v_rg_w_x'], 'v_rg_b_x': out['v_rg_b_x'], 'v_rg_lambda': out['v_rg_lambda'], 'v_hg_lb_logits': out['v_hg_lb_logits'], 'v_hg_norm': out['v_hg_norm'], 'v_gla_w_in': out['v_gla_w_in'], 'v_gla_w_out': out['v_gla_w_out'], 'v_gla_w_gate_up': out['v_gla_w_gate_up'], 'v_gla_b_gate': out['v_gla_b_gate'], 'v_gla_norm': out['v_gla_norm']}


def _loss(weights, diff, rest, loss_target):
    with _jax.named_scope("forward"):
        args = {**rest, TWIN_DIFF_INPUT: diff, **{k: w.astype(_WEIGHT_DTYPES[k]) for k, w in weights.items()}}
        y = _forward(args)
    with _jax.named_scope("loss_head"):
        err = _jnp.square(y.astype(_jnp.float32) - loss_target)
        return 0.5 * _jnp.sum(_jnp.mean(err, axis=-1)) if err.ndim else 0.5 * err


def _adamw(w, g, m, v):
    m = ADAM_B1 * m + (1.0 - ADAM_B1) * g
    v = ADAM_B2 * v + (1.0 - ADAM_B2) * _jnp.square(g)
    m_hat = m / (1.0 - ADAM_B1 ** ADAM_STEP)
    v_hat = v / (1.0 - ADAM_B2 ** ADAM_STEP)
    delta = -ADAM_LR * (m_hat / (_jnp.sqrt(v_hat) + ADAM_EPS) + ADAM_WD * w)
    return delta, m, v


def reference(x, norm_mix, norm_mlp, norm_final, mlp_w1, mlp_w2, ab_w_in, ab_w_out, rg_conv_w, rg_conv_b, rg_w_a, rg_b_a, rg_w_x, rg_b_x, rg_lambda, hg_lb_logits, hg_norm, gla_w_in, gla_w_out, gla_w_gate_up, gla_b_gate, gla_norm, loss_target, m_norm_mix, m_norm_mlp, m_norm_final, m_mlp_w1, m_mlp_w2, m_ab_w_in, m_ab_w_out, m_rg_conv_w, m_rg_conv_b, m_rg_w_a, m_rg_b_a, m_rg_w_x, m_rg_b_x, m_rg_lambda, m_hg_lb_logits, m_hg_norm, m_gla_w_in, m_gla_w_out, m_gla_w_gate_up, m_gla_b_gate, m_gla_norm, v_norm_mix, v_norm_mlp, v_norm_final, v_mlp_w1, v_mlp_w2, v_ab_w_in, v_ab_w_out, v_rg_conv_w, v_rg_conv_b, v_rg_w_a, v_rg_b_a, v_rg_w_x, v_rg_b_x, v_rg_lambda, v_hg_lb_logits, v_hg_norm, v_gla_w_in, v_gla_w_out, v_gla_w_gate_up, v_gla_b_gate, v_gla_norm):
    given = dict(x=x, norm_mix=norm_mix, norm_mlp=norm_mlp, norm_final=norm_final, mlp_w1=mlp_w1, mlp_w2=mlp_w2, ab_w_in=ab_w_in, ab_w_out=ab_w_out, rg_conv_w=rg_conv_w, rg_conv_b=rg_conv_b, rg_w_a=rg_w_a, rg_b_a=rg_b_a, rg_w_x=rg_w_x, rg_b_x=rg_b_x, rg_lambda=rg_lambda, hg_lb_logits=hg_lb_logits, hg_norm=hg_norm, gla_w_in=gla_w_in, gla_w_out=gla_w_out, gla_w_gate_up=gla_w_gate_up, gla_b_gate=gla_b_gate, gla_norm=gla_norm, loss_target=loss_target, m_norm_mix=m_norm_mix, m_norm_mlp=m_norm_mlp, m_norm_final=m_norm_final, m_mlp_w1=m_mlp_w1, m_mlp_w2=m_mlp_w2, m_ab_w_in=m_ab_w_in, m_ab_w_out=m_ab_w_out, m_rg_conv_w=m_rg_conv_w, m_rg_conv_b=m_rg_conv_b, m_rg_w_a=m_rg_w_a, m_rg_b_a=m_rg_b_a, m_rg_w_x=m_rg_w_x, m_rg_b_x=m_rg_b_x, m_rg_lambda=m_rg_lambda, m_hg_lb_logits=m_hg_lb_logits, m_hg_norm=m_hg_norm, m_gla_w_in=m_gla_w_in, m_gla_w_out=m_gla_w_out, m_gla_w_gate_up=m_gla_w_gate_up, m_gla_b_gate=m_gla_b_gate, m_gla_norm=m_gla_norm, v_norm_mix=v_norm_mix, v_norm_mlp=v_norm_mlp, v_norm_final=v_norm_final, v_mlp_w1=v_mlp_w1, v_mlp_w2=v_mlp_w2, v_ab_w_in=v_ab_w_in, v_ab_w_out=v_ab_w_out, v_rg_conv_w=v_rg_conv_w, v_rg_conv_b=v_rg_conv_b, v_rg_w_a=v_rg_w_a, v_rg_b_a=v_rg_b_a, v_rg_w_x=v_rg_w_x, v_rg_b_x=v_rg_b_x, v_rg_lambda=v_rg_lambda, v_hg_lb_logits=v_hg_lb_logits, v_hg_norm=v_hg_norm, v_gla_w_in=v_gla_w_in, v_gla_w_out=v_gla_w_out, v_gla_w_gate_up=v_gla_w_gate_up, v_gla_b_gate=v_gla_b_gate, v_gla_norm=v_gla_norm)
    weights = {n: given[n] for n in TWIN_WEIGHTS}
    shared = {n: given[n] for n in SHARED_INPUTS}
    per_example = {n: given[n] for n in ['x']}
    grad_fn = _jax.value_and_grad(_loss, argnums=(0, 1))

    def one_microbatch(ex, loss_target):
        ex = dict(ex)
        diff = ex.pop(TWIN_DIFF_INPUT)
        return grad_fn(weights, diff, {**shared, **ex}, loss_target)

    if N_MICROBATCH == 1:
        loss, (grad_w, grad_x) = one_microbatch(per_example, given["loss_target"])
    else:
        def body(carry, xs):
            loss_sum, grad_sum = carry
            l_k, (gw_k, gx_k) = one_microbatch(xs[0], xs[1])
            with _jax.named_scope("update"):
                return (loss_sum + l_k, _jax.tree.map(_jnp.add, grad_sum, gw_k)), gx_k

        init = (_jnp.zeros((), _jnp.float32), _jax.tree.map(_jnp.zeros_like, weights))
        (loss, grad_w), grad_x = _jax.lax.scan(body, init, (per_example, given["loss_target"]))
    with _jax.named_scope("update"):
        delta_w, new_m, new_v = {}, {}, {}
        for n in TWIN_WEIGHTS:
            delta_w[n], new_m[n], new_v[n] = _adamw(weights[n], grad_w[n], given["m_" + n], given["v_" + n])
    return (loss, grad_x, *[grad_w[n] for n in TWIN_WEIGHTS], *[delta_w[n] for n in TWIN_WEIGHTS],
            *[new_m[n] for n in TWIN_WEIGHTS], *[new_v[n] for n in TWIN_WEIGHTS])
```

```python
import functools

import jax
import jax.numpy as jnp
from jax import lax
from jax.experimental import pallas as pl
from jax.experimental.pallas import tpu as pltpu

F32, BF16 = jnp.float32, jnp.bfloat16
HI = lax.Precision.HIGHEST
MESH = pl.DeviceIdType.MESH

D_MODEL = 1024
D_FF = 4096
RG_W = 512
HG_W = 512
CHUNK = 64
EPS = 1e-6
RG_C = 8.0
AB_IN = 3584
GLA_IN = 3104
GLA_IN_PAD = 3200
N_DEV = 8
LANES = 128
SUBLANES = 8
VMEM_LIMIT = 48 * 1024 * 1024

ADAM_LR, ADAM_B1, ADAM_B2, ADAM_EPS, ADAM_WD, ADAM_STEP = 0.001, 0.9, 0.999, 1e-08, 0.01, 10


def _params(*sem):
    return pltpu.CompilerParams(dimension_semantics=sem, vmem_limit_bytes=VMEM_LIMIT)


def _dg(a, b, ca, cb):
    return lax.dot_general(a.astype(BF16), b.astype(BF16), (((ca,), (cb,)), ((), ())),
                           preferred_element_type=F32)


@jax.custom_vjp
def _mm_nn(a, b):
    return _dg(a, b, 1, 0)


_mm_nn.defvjp(lambda a, b: (_dg(a, b, 1, 0), (a, b)),
              lambda res, g: (_dg(g, res[1], 1, 1), _dg(res[0], g, 0, 0)))


@jax.custom_vjp
def _mm_nt(a, b):
    return _dg(a, b, 1, 1)


_mm_nt.defvjp(lambda a, b: (_dg(a, b, 1, 1), (a, b)),
              lambda res, g: (_dg(g, res[1], 1, 0), _dg(g, res[0], 0, 0)))


@jax.custom_vjp
def _mm_tn(a, b):
    return _dg(a, b, 0, 0)


_mm_tn.defvjp(lambda a, b: (_dg(a, b, 0, 0), (a, b)),
              lambda res, g: (_dg(res[1], g, 1, 1), _dg(res[0], g, 1, 0)))


@jax.custom_vjp
def _cum(tri, tri_t, x):
    return jnp.dot(tri, x, precision=HI, preferred_element_type=F32)


_cum.defvjp(lambda tri, tri_t, x: (jnp.dot(tri, x, precision=HI, preferred_element_type=F32), (tri, tri_t)),
            lambda res, g: (jnp.zeros_like(res[0]), jnp.zeros_like(res[1]),
                            jnp.dot(res[1], g, precision=HI, preferred_element_type=F32)))


def _sig(x):
    return 1.0 / (1.0 + jnp.exp(-x))


def _gelu(x):
    return 0.5 * x * (1.0 + jnp.tanh(0.7978845608028654 * (x + 0.044715 * (x * x * x))))


def _softplus(z):
    return jnp.maximum(z, 0.0) + jnp.log(1.0 + jnp.exp(-jnp.abs(z)))


def _rms(x):
    return lax.rsqrt(jnp.mean(x * x, axis=-1, keepdims=True) + EPS)


def _rmsnorm_bwd(x, gain, dy):
    r = _rms(x)
    xh = x * r
    dgain = jnp.sum(dy * xh, axis=0, keepdims=True)
    dxh = dy * gain
    dx = r * (dxh - xh * jnp.mean(dxh * xh, axis=-1, keepdims=True))
    return dx, dgain


def _headnorm(o, gain, n_heads, hd):
    parts = []
    for h in range(n_heads):
        oh = o[:, h * hd:(h + 1) * hd]
        parts.append(oh * _rms(oh))
    return jnp.concatenate(parts, axis=1) * gain


def _tri_consts(d):
    row = lax.broadcasted_iota(jnp.int32, (CHUNK, CHUNK), 0)
    col = lax.broadcasted_iota(jnp.int32, (CHUNK, CHUNK), 1)
    ge = (row >= col).astype(F32)
    le = (row <= col).astype(F32)
    tri = jnp.where(d == 0, ge, le)
    tri_t = jnp.where(d == 0, le, ge)
    r1 = lax.broadcasted_iota(jnp.int32, (CHUNK, 1), 0)
    mref = jnp.where(d == 0, (r1 <= CHUNK // 2).astype(F32), (r1 >= CHUNK // 2 - 1).astype(F32))
    return tri, tri_t, mref


def _chunk_core(qh, k, v, logf, st_prev, tri, tri_t, mref, n_heads, dk, dv):
    cum = _cum(tri, tri_t, logf)
    ref = jnp.sum(logf * mref, axis=0, keepdims=True)
    last = jnp.sum(logf, axis=0, keepdims=True)
    q_in = qh * jnp.exp(cum - ref)
    k_in = k * jnp.exp(ref - cum)
    k_st = k * jnp.exp(last - cum)
    q_dec = qh * jnp.exp(cum)
    decay = jnp.exp(last)
    outs, sts = [], []
    for h in range(n_heads):
        sk = slice(h * dk, (h + 1) * dk)
        sv = slice(h * dv, (h + 1) * dv)
        sc = _mm_nt(q_in[:, sk], k_in[:, sk]) * tri
        o = _mm_nn(sc, v[:, sv]) + _mm_nt(q_dec[:, sk], st_prev[h])
        sts.append(st_prev[h] * decay[:, sk] + _mm_tn(v[:, sv], k_st[:, sk]))
        outs.append(o)
    return jnp.concatenate(outs, axis=1), tuple(sts)


def _hg_chunk(q, f, v, l0, l1, st_prev, tri, tri_t, mref):
    lb = _sig(l0 - l1)
    sg = _sig(f)
    qh = q * _sig(q)
    logf = jnp.log(lb + (1.0 - lb) * sg)
    k = (1.0 - lb) * (1.0 - sg)
    return _chunk_core(qh, k, v, logf, st_prev, tri, tri_t, mref, 4, 128, 128)


def _gla_chunk(q, k, v, z, st_prev, tri, tri_t, mref):
    logf = (jnp.minimum(z, 0.0) - jnp.log(1.0 + jnp.exp(-jnp.abs(z)))) * (1.0 / 16.0)
    qh = q * (128.0 ** -0.5)
    return _chunk_core(qh, k, v, logf, st_prev, tri, tri_t, mref, 4, 128, 256)


def _rg_gates(xc, wbd, bias, lam):
    z = _mm_nn(xc, wbd) + bias
    r = _sig(z[:, :RG_W])
    i = _sig(z[:, RG_W:])
    log_a = -RG_C * r * _softplus(-lam)
    a = jnp.exp(log_a)
    x2 = 2.0 * log_a
    neg_expm1 = jnp.where(x2 > -1e-2, -(x2 + 0.5 * x2 * x2 + x2 * x2 * x2 * (1.0 / 6.0)), 1.0 - jnp.exp(x2))
    u = jnp.sqrt(neg_expm1) * (i * xc)
    return a, u


def _l0_combine(hf, hb, ga, of, ob, g, gain):
    ya = (hf + hb) * _gelu(ga)
    yb = _headnorm(of + ob, gain, 4, 128) * (g * _sig(g))
    return jnp.concatenate([ya, yb], axis=1)


def _l1_combine(of, ob, r, gain):
    return _headnorm(of + ob, gain, 4, 256) * (r * _sig(r))


def _norm_matmul(h, gain, w, tn, name):
    T, D = h.shape
    N = w.shape[1]
    tm = min(512, T)

    def body(h_ref, g_ref, w_ref, o_ref, y_ref, ysc):
        @pl.when(pl.program_id(1) == 0)
        def _():
            x = h_ref[...]
            y = (x * _rms(x) * g_ref[...]).astype(BF16)
            ysc[...] = y
            y_ref[...] = y

        o_ref[...] = jnp.dot(ysc[...], w_ref[...], preferred_element_type=F32)

    return pl.pallas_call(
        body, name=name, grid=(T // tm, N // tn),
        in_specs=[pl.BlockSpec((tm, D), lambda i, j: (i, 0)), pl.BlockSpec((1, D), lambda i, j: (0, 0)),
                  pl.BlockSpec((D, tn), lambda i, j: (0, j))],
        out_specs=[pl.BlockSpec((tm, tn), lambda i, j: (i, j)), pl.BlockSpec((tm, D), lambda i, j: (i, 0))],
        out_shape=[jax.ShapeDtypeStruct((T, N), F32), jax.ShapeDtypeStruct((T, D), BF16)],
        scratch_shapes=[pltpu.VMEM((tm, D), BF16)],
        compiler_params=_params("parallel", "arbitrary"))(h, gain, w)


def _matmul_res(a, w, res, name):
    T, K = a.shape
    N = w.shape[1]
    tm = min(512, T)

    def body(a_ref, w_ref, r_ref, o_ref):
        o_ref[...] = r_ref[...] + jnp.dot(a_ref[...], w_ref[...], preferred_element_type=F32)

    return pl.pallas_call(
        body, name=name, grid=(T // tm,),
        in_specs=[pl.BlockSpec((tm, K), lambda i: (i, 0)), pl.BlockSpec((K, N), lambda i: (0, 0)),
                  pl.BlockSpec((tm, N), lambda i: (i, 0))],
        out_specs=pl.BlockSpec((tm, N), lambda i: (i, 0)),
        out_shape=jax.ShapeDtypeStruct((T, N), F32),
        compiler_params=_params("parallel"))(a, w, res)


def _dgrad(dc, w, name):
    T, N = dc.shape
    K = w.shape[0]
    tm = min(512, T)

    def body(d_ref, w_ref, o_ref):
        o_ref[...] = _dg(d_ref[...], w_ref[...], 1, 1)

    return pl.pallas_call(
        body, name=name, grid=(T // tm,),
        in_specs=[pl.BlockSpec((tm, N), lambda i: (i, 0)), pl.BlockSpec((K, N), lambda i: (0, 0))],
        out_specs=pl.BlockSpec((tm, K), lambda i: (i, 0)),
        out_shape=jax.ShapeDtypeStruct((T, K), F32),
        compiler_params=_params("parallel"))(dc, w)


def _dgrad_norm(dproj, w, h, gain, dres, tn, name):
    T, N = dproj.shape
    D = w.shape[0]
    tm = min(512, T)
    nj = N // tn

    def body(dp_ref, w_ref, h_ref, g_ref, dr_ref, dh_ref, dg_ref, acc):
        i, j = pl.program_id(0), pl.program_id(1)

        @pl.when(j == 0)
        def _():
            acc[...] = jnp.zeros_like(acc)

        @pl.when((i == 0) & (j == 0))
        def _():
            dg_ref[...] = jnp.zeros_like(dg_ref)

        acc[...] += _dg(dp_ref[...], w_ref[...], 1, 1)

        @pl.when(j == nj - 1)
        def _():
            dx, dgain = _rmsnorm_bwd(h_ref[...], g_ref[...], acc[...])
            dh_ref[...] = dr_ref[...] + dx
            dg_ref[...] += dgain

    return pl.pallas_call(
        body, name=name, grid=(T // tm, nj),
        in_specs=[pl.BlockSpec((tm, tn), lambda i, j: (i, j)), pl.BlockSpec((D, tn), lambda i, j: (0, j)),
                  pl.BlockSpec((tm, D), lambda i, j: (i, 0)), pl.BlockSpec((1, D), lambda i, j: (0, 0)),
                  pl.BlockSpec((tm, D), lambda i, j: (i, 0))],
        out_specs=[pl.BlockSpec((tm, D), lambda i, j: (i, 0)), pl.BlockSpec((1, D), lambda i, j: (0, 0))],
        out_shape=[jax.ShapeDtypeStruct((T, D), F32), jax.ShapeDtypeStruct((1, D), F32)],
        scratch_shapes=[pltpu.VMEM((tm, D), F32)],
        compiler_params=_params("arbitrary", "arbitrary"))(dproj, w, h, gain, dres)


def _wgrad(a, b, tn, name, sharded_cols=False):
    T, K = a.shape
    N = b.shape[1]
    tk = min(512, K)
    tt = min(512, T)
    nt = T // tt

    def body(a_ref, b_ref, o_ref):
        @pl.when(pl.program_id(2) == 0)
        def _():
            o_ref[...] = jnp.zeros_like(o_ref)

        o_ref[...] += _dg(a_ref[...], b_ref[...], 0, 0)

    if sharded_cols:
        out_spec = pl.BlockSpec((None, tk, tn), lambda k, n, t: (n, k, 0))
        out_shape = jax.ShapeDtypeStruct((N // tn, K, tn), F32)
    else:
        out_spec = pl.BlockSpec((tk, tn), lambda k, n, t: (k, n))
        out_shape = jax.ShapeDtypeStruct((K, N), F32)
    return pl.pallas_call(
        body, name=name, grid=(K // tk, N // tn, nt),
        in_specs=[pl.BlockSpec((tt, tk), lambda k, n, t: (t, k)), pl.BlockSpec((tt, tn), lambda k, n, t: (t, n))],
        out_specs=out_spec, out_shape=out_shape,
        compiler_params=_params("parallel", "parallel", "arbitrary"))(a, b)


def _mlp_fwd(h, gain, w1g, w2, name):
    T, D = h.shape
    nf, _, tf = w1g.shape
    tm = min(512, T)

    def body(h_ref, g_ref, w1_ref, w2_ref, o_ref, pre_ref, y_ref, ysc, acc):
        j = pl.program_id(1)

        @pl.when(j == 0)
        def _():
            x = h_ref[...]
            y = (x * _rms(x) * g_ref[...]).astype(BF16)
            ysc[...] = y
            y_ref[...] = y
            acc[...] = jnp.zeros_like(acc)

        pre = jnp.dot(ysc[...], w1_ref[...], preferred_element_type=F32)
        pre_ref[...] = pre.astype(BF16)
        act = jnp.square(jnp.maximum(pre, 0.0))
        acc[...] += jnp.dot(act.astype(BF16), w2_ref[...], preferred_element_type=F32)

        @pl.when(j == nf - 1)
        def _():
            o_ref[...] = h_ref[...] + acc[...]

    return pl.pallas_call(
        body, name=name, grid=(T // tm, nf),
        in_specs=[pl.BlockSpec((tm, D), lambda i, j: (i, 0)), pl.BlockSpec((1, D), lambda i, j: (0, 0)),
                  pl.BlockSpec((None, D, tf), lambda i, j: (j, 0, 0)), pl.BlockSpec((tf, D), lambda i, j: (j, 0))],
        out_specs=[pl.BlockSpec((tm, D), lambda i, j: (i, 0)), pl.BlockSpec((tm, tf), lambda i, j: (i, j)),
                   pl.BlockSpec((tm, D), lambda i, j: (i, 0))],
        out_shape=[jax.ShapeDtypeStruct((T, D), F32), jax.ShapeDtypeStruct((T, nf * tf), BF16),
                   jax.ShapeDtypeStruct((T, D), BF16)],
        scratch_shapes=[pltpu.VMEM((tm, D), BF16), pltpu.VMEM((tm, D), F32)],
        compiler_params=_params("parallel", "arbitrary"))(h, gain, w1g, w2)


def _mlp_bwd(dout, h, gain, pre, w1g, w2, name):
    T, D = h.shape
    nf, _, tf = w1g.shape
    tm = min(512, T)

    def body(do_ref, h_ref, g_ref, pre_ref, w1_ref, w2_ref, dh_ref, dpre_ref, act_ref, dg_ref, dy):
        i, j = pl.program_id(0), pl.program_id(1)

        @pl.when(j == 0)
        def _():
            dy[...] = jnp.zeros_like(dy)

        @pl.when((i == 0) & (j == 0))
        def _():
            dg_ref[...] = jnp.zeros_like(dg_ref)

        rp = jnp.maximum(pre_ref[...].astype(F32), 0.0)
        dact = _dg(do_ref[...], w2_ref[...], 1, 1)
        dpre = (dact * (2.0 * rp)).astype(BF16)
        dpre_ref[...] = dpre
        act_ref[...] = (rp * rp).astype(BF16)
        dy[...] += _dg(dpre, w1_ref[...], 1, 1)

        @pl.when(j == nf - 1)
        def _():
            dx, dgain = _rmsnorm_bwd(h_ref[...], g_ref[...], dy[...])
            dh_ref[...] = do_ref[...] + dx
            dg_ref[...] += dgain

    return pl.pallas_call(
        body, name=name, grid=(T // tm, nf),
        in_specs=[pl.BlockSpec((tm, D), lambda i, j: (i, 0)), pl.BlockSpec((tm, D), lambda i, j: (i, 0)),
                  pl.BlockSpec((1, D), lambda i, j: (0, 0)), pl.BlockSpec((tm, tf), lambda i, j: (i, j)),
                  pl.BlockSpec((None, D, tf), lambda i, j: (j, 0, 0)), pl.BlockSpec((tf, D), lambda i, j: (j, 0))],
        out_specs=[pl.BlockSpec((tm, D), lambda i, j: (i, 0)), pl.BlockSpec((tm, tf), lambda i, j: (i, j)),
                   pl.BlockSpec((tm, tf), lambda i, j: (i, j)), pl.BlockSpec((1, D), lambda i, j: (0, 0))],
        out_shape=[jax.ShapeDtypeStruct((T, D), F32), jax.ShapeDtypeStruct((T, nf * tf), BF16),
                   jax.ShapeDtypeStruct((T, nf * tf), BF16), jax.ShapeDtypeStruct((1, D), F32)],
        scratch_shapes=[pltpu.VMEM((tm, D), F32)],
        compiler_params=_params("arbitrary", "arbitrary"))(dout, h, gain, pre, w1g, w2)


def _final_loss(h, gain, target, name):
    T, D = h.shape
    tm = min(512, T)

    def body(h_ref, g_ref, t_ref, l_ref, dh_ref, dg_ref):
        @pl.when(pl.program_id(0) == 0)
        def _():
            l_ref[...] = jnp.zeros_like(l_ref)
            dg_ref[...] = jnp.zeros_like(dg_ref)

        x = h_ref[...]
        err = x * _rms(x) * g_ref[...] - t_ref[...]
        l_ref[...] += 0.5 * jnp.sum(jnp.mean(err * err, axis=-1, keepdims=True), axis=0, keepdims=True)
        dx, dgain = _rmsnorm_bwd(x, g_ref[...], err * (1.0 / D))
        dh_ref[...] = dx
        dg_ref[...] += dgain

    return pl.pallas_call(
        body, name=name, grid=(T // tm,),
        in_specs=[pl.BlockSpec((tm, D), lambda i: (i, 0)), pl.BlockSpec((1, D), lambda i: (0, 0)),
                  pl.BlockSpec((tm, D), lambda i: (i, 0))],
        out_specs=[pl.BlockSpec((SUBLANES, LANES), lambda i: (0, 0)), pl.BlockSpec((tm, D), lambda i: (i, 0)),
                   pl.BlockSpec((1, D), lambda i: (0, 0))],
        out_shape=[jax.ShapeDtypeStruct((SUBLANES, LANES), F32), jax.ShapeDtypeStruct((T, D), F32),
                   jax.ShapeDtypeStruct((1, D), F32)],
        compiler_params=_params("arbitrary"))(h, gain, target)


def _halo_specs(tm, T, width, col, lead=None):
    r8 = tm // SUBLANES
    nb8 = T // SUBLANES
    if lead is None:
        return [pl.BlockSpec((tm, width), lambda i: (i, col)),
                pl.BlockSpec((SUBLANES, width), lambda i: (jnp.maximum(i * r8 - 1, 0), col)),
                pl.BlockSpec((SUBLANES, width), lambda i: (jnp.minimum((i + 1) * r8, nb8 - 1), col))]
    return [pl.BlockSpec((None, tm, width), lambda i: (lead, i, col)),
            pl.BlockSpec((None, SUBLANES, width), lambda i: (lead, jnp.maximum(i * r8 - 1, 0), col)),
            pl.BlockSpec((None, SUBLANES, width), lambda i: (lead, jnp.minimum((i + 1) * r8, nb8 - 1), col))]


def _ext(cur, prev, nxt, has_prev, has_next):
    return jnp.concatenate([jnp.where(has_prev, prev, 0.0), cur, jnp.where(has_next, nxt, 0.0)], axis=0)


def _shifted(ext, offset, tm):
    n = ext.shape[0]
    sh = (-offset) % n
    r = ext if sh == 0 else pltpu.roll(ext, sh, 0)
    return r[SUBLANES:SUBLANES + tm]


def _rg_conv_fwd(proj, cw8, cb, name):
    T = proj.shape[0]
    tm = min(512, T)
    nT = T // tm

    def body(cur_ref, prev_ref, next_ref, w_ref, b_ref, o_ref):
        i = pl.program_id(0)
        ext = _ext(cur_ref[...], prev_ref[...], next_ref[...], i > 0, i < nT - 1)
        acc = jnp.broadcast_to(b_ref[...], (tm, RG_W))
        for k in range(4):
            acc = acc + w_ref[k:k + 1, :] * _shifted(ext, k - 2, tm)
        o_ref[...] = acc

    return pl.pallas_call(
        body, name=name, grid=(nT,),
        in_specs=_halo_specs(tm, T, RG_W, 0) + [pl.BlockSpec((SUBLANES, RG_W), lambda i: (0, 0)),
                                                pl.BlockSpec((1, RG_W), lambda i: (0, 0))],
        out_specs=pl.BlockSpec((tm, RG_W), lambda i: (i, 0)),
        out_shape=jax.ShapeDtypeStruct((T, RG_W), F32),
        compiler_params=_params("parallel"))(proj, proj, proj, cw8, cb)


def _rg_conv_bwd(dxc, proj, cw8, name):
    T = proj.shape[0]
    tm = min(512, T)
    nT = T // tm

    def body(a0, p0, n0, a1, p1, n1, xa, xp, xn, w_ref, dxa_ref, dw_ref, db_ref):
        i = pl.program_id(0)

        @pl.when(i == 0)
        def _():
            dw_ref[...] = jnp.zeros_like(dw_ref)
            db_ref[...] = jnp.zeros_like(db_ref)

        has_p, has_n = i > 0, i < nT - 1
        cur = a0[...] + a1[...]
        dext = _ext(cur, p0[...] + p1[...], n0[...] + n1[...], has_p, has_n)
        xext = _ext(xa[...], xp[...], xn[...], has_p, has_n)
        acc = jnp.zeros((tm, RG_W), F32)
        rows = []
        for k in range(4):
            acc = acc + w_ref[k:k + 1, :] * _shifted(dext, 2 - k, tm)
            rows.append(jnp.sum(cur * _shifted(xext, k - 2, tm), axis=0, keepdims=True))
        dxa_ref[...] = acc
        dw_ref[...] += jnp.concatenate(rows + [jnp.zeros((4, RG_W), F32)], axis=0)
        db_ref[...] += jnp.sum(cur, axis=0, keepdims=True)

    return pl.pallas_call(
        body, name=name, grid=(nT,),
        in_specs=(_halo_specs(tm, T, RG_W, 0, lead=0) + _halo_specs(tm, T, RG_W, 0, lead=1)
                  + _halo_specs(tm, T, RG_W, 0) + [pl.BlockSpec((SUBLANES, RG_W), lambda i: (0, 0))]),
        out_specs=[pl.BlockSpec((tm, RG_W), lambda i: (i, 0)), pl.BlockSpec((SUBLANES, RG_W), lambda i: (0, 0)),
                   pl.BlockSpec((1, RG_W), lambda i: (0, 0))],
        out_shape=[jax.ShapeDtypeStruct((T, RG_W), F32), jax.ShapeDtypeStruct((SUBLANES, RG_W), F32),
                   jax.ShapeDtypeStruct((1, RG_W), F32)],
        compiler_params=_params("arbitrary"))(dxc, dxc, dxc, dxc, dxc, dxc, proj, proj, proj, cw8)


def _rg_scan_fwd(xc, wbd, bias, lam, name):
    T = xc.shape[0]
    tm = min(512, T)
    nT = T // tm

    def tile(d, i):
        return i + d * (nT - 1 - 2 * i)

    def body(xc_ref, w_ref, b_ref, lam_ref, h_ref, a_sc, u_sc, carry):
        d, i = pl.program_id(0), pl.program_id(1)

        @pl.when(i == 0)
        def _():
            carry[...] = jnp.zeros_like(carry)

        a, u = _rg_gates(xc_ref[...], w_ref[...], b_ref[...], lam_ref[...])
        a_sc[...] = a
        u_sc[...] = u

        def step(t, h):
            tt = t + d * (tm - 1 - 2 * t)
            h = a_sc[pl.ds(tt, 1), :] * h + u_sc[pl.ds(tt, 1), :]
            h_ref[pl.ds(tt, 1), :] = h
            return h

        carry[0:1, :] = lax.fori_loop(0, tm, step, carry[0:1, :])

    return pl.pallas_call(
        body, name=name, grid=(2, nT),
        in_specs=[pl.BlockSpec((tm, RG_W), lambda d, i: (tile(d, i), 0)),
                  pl.BlockSpec((None, RG_W, 2 * RG_W), lambda d, i: (d, 0, 0)),
                  pl.BlockSpec((None, 1, 2 * RG_W), lambda d, i: (d, 0, 0)),
                  pl.BlockSpec((None, 1, RG_W), lambda d, i: (d, 0, 0))],
        out_specs=pl.BlockSpec((None, tm, RG_W), lambda d, i: (d, tile(d, i), 0)),
        out_shape=jax.ShapeDtypeStruct((2, T, RG_W), F32),
        scratch_shapes=[pltpu.VMEM((tm, RG_W), F32), pltpu.VMEM((tm, RG_W), F32), pltpu.VMEM((SUBLANES, RG_W), F32)],
        compiler_params=_params("arbitrary", "arbitrary"))(xc, wbd, bias, lam)


def _rg_scan_bwd(xc, wbd, bias, lam, hs, dho, name):
    T = xc.shape[0]
    tm = min(512, T)
    nT = T // tm
    r8 = tm // SUBLANES
    nb8 = T // SUBLANES

    def tile(d, i):
        return (nT - 1 - i) + d * (2 * i - (nT - 1))

    def body(xc_ref, w_ref, b_ref, lam_ref, hc_ref, hp_ref, hn_ref, dho_ref,
             dxc_ref, dw_ref, db_ref, dlam_ref, a_sc, dt_sc, carry):
        d, i = pl.program_id(0), pl.program_id(1)
        ti = tile(d, i)

        @pl.when(i == 0)
        def _():
            carry[...] = jnp.zeros_like(carry)
            dw_ref[...] = jnp.zeros_like(dw_ref)
            db_ref[...] = jnp.zeros_like(db_ref)
            dlam_ref[...] = jnp.zeros_like(dlam_ref)

        (a, _), vjp = jax.vjp(_rg_gates, xc_ref[...], w_ref[...].astype(F32), b_ref[...], lam_ref[...])
        a_sc[...] = a

        def step(t, c):
            tt = (tm - 1 - t) + d * (2 * t - (tm - 1))
            dt = dho_ref[pl.ds(tt, 1), :] + c
            dt_sc[pl.ds(tt, 1), :] = dt
            return a_sc[pl.ds(tt, 1), :] * dt

        carry[0:1, :] = lax.fori_loop(0, tm, step, carry[0:1, :])
        dtot = dt_sc[...]
        ext = _ext(hc_ref[...], hp_ref[...], hn_ref[...], ti > 0, ti < nT - 1)
        hprev = jnp.where(d == 0, _shifted(ext, -1, tm), _shifted(ext, 1, tm))
        dxc, dw, db, dlam = vjp((dtot * hprev, dtot))
        dxc_ref[...] = dxc
        dw_ref[...] += dw
        db_ref[...] += db
        dlam_ref[...] += dlam

    return pl.pallas_call(
        body, name=name, grid=(2, nT),
        in_specs=[pl.BlockSpec((tm, RG_W), lambda d, i: (tile(d, i), 0)),
                  pl.BlockSpec((None, RG_W, 2 * RG_W), lambda d, i: (d, 0, 0)),
                  pl.BlockSpec((None, 1, 2 * RG_W), lambda d, i: (d, 0, 0)),
                  pl.BlockSpec((None, 1, RG_W), lambda d, i: (d, 0, 0)),
                  pl.BlockSpec((None, tm, RG_W), lambda d, i: (d, tile(d, i), 0)),
                  pl.BlockSpec((None, SUBLANES, RG_W), lambda d, i: (d, jnp.maximum(tile(d, i) * r8 - 1, 0), 0)),
                  pl.BlockSpec((None, SUBLANES, RG_W),
                               lambda d, i: (d, jnp.minimum((tile(d, i) + 1) * r8, nb8 - 1), 0)),
                  pl.BlockSpec((tm, RG_W), lambda d, i: (tile(d, i), 0))],
        out_specs=[pl.BlockSpec((None, tm, RG_W), lambda d, i: (d, tile(d, i), 0)),
                   pl.BlockSpec((None, RG_W, 2 * RG_W), lambda d, i: (d, 0, 0)),
                   pl.BlockSpec((None, 1, 2 * RG_W), lambda d, i: (d, 0, 0)),
                   pl.BlockSpec((None, 1, RG_W), lambda d, i: (d, 0, 0))],
        out_shape=[jax.ShapeDtypeStruct((2, T, RG_W), F32), jax.ShapeDtypeStruct((2, RG_W, 2 * RG_W), F32),
                   jax.ShapeDtypeStruct((2, 1, 2 * RG_W), F32), jax.ShapeDtypeStruct((2, 1, RG_W), F32)],
        scratch_shapes=[pltpu.VMEM((tm, RG_W), F32), pltpu.VMEM((tm, RG_W), F32), pltpu.VMEM((SUBLANES, RG_W), F32)],
        compiler_params=_params("arbitrary", "arbitrary"))(xc, wbd, bias, lam, hs, hs, hs, dho)


def _chunk_idx(d, c, n_chunks):
    return c + d * (n_chunks - 1 - 2 * c)


def _chunk_idx_rev(d, c, n_chunks):
    return (n_chunks - 1 - c) + d * (2 * c - (n_chunks - 1))


def _hg_fwd(proj, l0, l1, name):
    T = proj.shape[0]
    nC = T // CHUNK
    H, dk, dv = 4, 128, 128

    def body(q_ref, f_ref, v_ref, l0_ref, l1_ref, o_ref, sp_ref, st):
        d, c = pl.program_id(0), pl.program_id(1)

        @pl.when(c == 0)
        def _():
            st[...] = jnp.zeros_like(st)

        tri, tri_t, mref = _tri_consts(d)
        stp = tuple(st[h] for h in range(H))
        sp_ref[...] = st[...]
        o, stn = _hg_chunk(q_ref[...], f_ref[...], v_ref[...], l0_ref[...], l1_ref[...], stp, tri, tri_t, mref)
        o_ref[...] = o
        for h in range(H):
            st[h] = stn[h]

    row = lambda d, c: _chunk_idx(d, c, nC)
    return pl.pallas_call(
        body, name=name, grid=(2, nC),
        in_specs=[pl.BlockSpec((CHUNK, HG_W), lambda d, c: (row(d, c), 2)),
                  pl.BlockSpec((CHUNK, HG_W), lambda d, c: (row(d, c), 3 + d)),
                  pl.BlockSpec((CHUNK, HG_W), lambda d, c: (row(d, c), 5)),
                  pl.BlockSpec((1, HG_W), lambda d, c: (0, 0)), pl.BlockSpec((1, HG_W), lambda d, c: (0, 0))],
        out_specs=[pl.BlockSpec((None, CHUNK, H * dv), lambda d, c: (d, row(d, c), 0)),
                   pl.BlockSpec((None, None, H, dv, dk), lambda d, c: (d, row(d, c), 0, 0, 0))],
        out_shape=[jax.ShapeDtypeStruct((2, T, H * dv), F32), jax.ShapeDtypeStruct((2, nC, H, dv, dk), F32)],
        scratch_shapes=[pltpu.VMEM((H, dv, dk), F32)],
        compiler_params=_params("arbitrary", "arbitrary"))(proj, proj, proj, l0, l1)


def _hg_bwd(proj, l0, l1, sprev, do, name):
    T = proj.shape[0]
    nC = T // CHUNK
    H, dk, dv = 4, 128, 128

    def body(q_ref, f_ref, v_ref, l0_ref, l1_ref, sp_ref, do_ref, dq_ref, df_ref, dv_ref, dl0_ref, dl1_ref, dst):
        d, c = pl.program_id(0), pl.program_id(1)

        @pl.when(c == 0)
        def _():
            dst[...] = jnp.zeros_like(dst)
            dl0_ref[...] = jnp.zeros_like(dl0_ref)
            dl1_ref[...] = jnp.zeros_like(dl1_ref)

        tri, tri_t, mref = _tri_consts(d)
        fn = lambda q, f, v, a0, a1, stp: _hg_chunk(q, f, v, a0, a1, stp, tri, tri_t, mref)
        stp = tuple(sp_ref[h] for h in range(H))
        _, vjp = jax.vjp(fn, q_ref[...], f_ref[...], v_ref[...], l0_ref[...], l1_ref[...], stp)
        dq, df, dvv, dl0, dl1, dstp = vjp((do_ref[...], tuple(dst[h] for h in range(H))))
        dq_ref[...] = dq
        df_ref[...] = df
        dv_ref[...] = dvv
        dl0_ref[...] += dl0
        dl1_ref[...] += dl1
        for h in range(H):
            dst[h] = dstp[h]

    row = lambda d, c: _chunk_idx_rev(d, c, nC)
    tok = lambda: pl.BlockSpec((None, CHUNK, HG_W), lambda d, c: (d, row(d, c), 0))
    par = lambda: pl.BlockSpec((None, 1, HG_W), lambda d, c: (d, 0, 0))
    return pl.pallas_call(
        body, name=name, grid=(2, nC),
        in_specs=[pl.BlockSpec((CHUNK, HG_W), lambda d, c: (row(d, c), 2)),
                  pl.BlockSpec((CHUNK, HG_W), lambda d, c: (row(d, c), 3 + d)),
                  pl.BlockSpec((CHUNK, HG_W), lambda d, c: (row(d, c), 5)),
                  pl.BlockSpec((1, HG_W), lambda d, c: (0, 0)), pl.BlockSpec((1, HG_W), lambda d, c: (0, 0)),
                  pl.BlockSpec((None, None, H, dv, dk), lambda d, c: (d, row(d, c), 0, 0, 0)),
                  pl.BlockSpec((CHUNK, H * dv), lambda d, c: (row(d, c), 0))],
        out_specs=[tok(), tok(), tok(), par(), par()],
        out_shape=[jax.ShapeDtypeStruct((2, T, HG_W), F32)] * 3 + [jax.ShapeDtypeStruct((2, 1, HG_W), F32)] * 2,
        scratch_shapes=[pltpu.VMEM((H, dv, dk), F32)],
        compiler_params=_params("arbitrary", "arbitrary"))(proj, proj, proj, l0, l1, sprev, do)


def _gate_logits(proj, wup, bg, name):
    T = proj.shape[0]
    tm = min(512, T)

    def body(lr_ref, w_ref, b_ref, z_ref, lrb_ref):
        lr = lr_ref[...].astype(BF16)
        lrb_ref[...] = lr
        for d in range(2):
            z_ref[d] = _dg(lr, w_ref[d], 1, 0) + b_ref[d]

    return pl.pallas_call(
        body, name=name, grid=(T // tm,),
        in_specs=[pl.BlockSpec((tm, LANES), lambda i: (i, 24)), pl.BlockSpec((2, LANES, 512), lambda i: (0, 0, 0)),
                  pl.BlockSpec((2, 1, 512), lambda i: (0, 0, 0))],
        out_specs=[pl.BlockSpec((2, tm, 512), lambda i: (0, i, 0)), pl.BlockSpec((tm, LANES), lambda i: (i, 0))],
        out_shape=[jax.ShapeDtypeStruct((2, T, 512), F32), jax.ShapeDtypeStruct((T, LANES), BF16)],
        compiler_params=_params("parallel"))(proj, wup, bg)


def _gate_logits_bwd(dz, wup, name):
    T = dz.shape[1]
    tm = min(512, T)

    def body(dz_ref, w_ref, dlr_ref, db_ref):
        @pl.when(pl.program_id(0) == 0)
        def _():
            db_ref[...] = jnp.zeros_like(db_ref)

        acc = jnp.zeros((tm, LANES), F32)
        for d in range(2):
            g = dz_ref[d]
            acc = acc + _dg(g, w_ref[d], 1, 1)
            db_ref[d] += jnp.sum(g, axis=0, keepdims=True)
        dlr_ref[...] = acc

    return pl.pallas_call(
        body, name=name, grid=(T // tm,),
        in_specs=[pl.BlockSpec((2, tm, 512), lambda i: (0, i, 0)), pl.BlockSpec((2, LANES, 512), lambda i: (0, 0, 0))],
        out_specs=[pl.BlockSpec((tm, LANES), lambda i: (i, 0)), pl.BlockSpec((2, 1, 512), lambda i: (0, 0, 0))],
        out_shape=[jax.ShapeDtypeStruct((T, LANES), F32), jax.ShapeDtypeStruct((2, 1, 512), F32)],
        compiler_params=_params("arbitrary"))(dz, wup)


def _gla_fwd(proj, z, name):
    T = proj.shape[0]
    nC = T // CHUNK
    H, dk, dv = 4, 128, 256

    def body(q_ref, k_ref, v_ref, z_ref, o_ref, sp_ref, st):
        d, c = pl.program_id(0), pl.program_id(1)

        @pl.when(c == 0)
        def _():
            st[...] = jnp.zeros_like(st)

        tri, tri_t, mref = _tri_consts(d)
        stp = tuple(st[h] for h in range(H))
        sp_ref[...] = st[...]
        o, stn = _gla_chunk(q_ref[...], k_ref[...], v_ref[...], z_ref[...], stp, tri, tri_t, mref)
        o_ref[...] = o
        for h in range(H):
            st[h] = stn[h]

    row = lambda d, c: _chunk_idx(d, c, nC)
    return pl.pallas_call(
        body, name=name, grid=(2, nC),
        in_specs=[pl.BlockSpec((CHUNK, 512), lambda d, c: (row(d, c), 0)),
                  pl.BlockSpec((CHUNK, 512), lambda d, c: (row(d, c), 1)),
                  pl.BlockSpec((CHUNK, 1024), lambda d, c: (row(d, c), 1)),
                  pl.BlockSpec((None, CHUNK, 512), lambda d, c: (d, row(d, c), 0))],
        out_specs=[pl.BlockSpec((None, CHUNK, H * dv), lambda d, c: (d, row(d, c), 0)),
                   pl.BlockSpec((None, None, H, dv, dk), lambda d, c: (d, row(d, c), 0, 0, 0))],
        out_shape=[jax.ShapeDtypeStruct((2, T, H * dv), F32), jax.ShapeDtypeStruct((2, nC, H, dv, dk), F32)],
        scratch_shapes=[pltpu.VMEM((H, dv, dk), F32)],
        compiler_params=_params("arbitrary", "arbitrary"))(proj, proj, proj, z)


def _gla_bwd(proj, z, sprev, do, name):
    T = proj.shape[0]
    nC = T // CHUNK
    H, dk, dv = 4, 128, 256

    def body(q_ref, k_ref, v_ref, z_ref, sp_ref, do_ref, dq_ref, dk_ref, dv_ref, dz_ref, dst):
        d, c = pl.program_id(0), pl.program_id(1)

        @pl.when(c == 0)
        def _():
            dst[...] = jnp.zeros_like(dst)

        tri, tri_t, mref = _tri_consts(d)
        fn = lambda q, k, v, zz, stp: _gla_chunk(q, k, v, zz, stp, tri, tri_t, mref)
        stp = tuple(sp_ref[h] for h in range(H))
        _, vjp = jax.vjp(fn, q_ref[...], k_ref[...], v_ref[...], z_ref[...], stp)
        dq, dkk, dvv, dzz, dstp = vjp((do_ref[...], tuple(dst[h] for h in range(H))))
        dq_ref[...] = dq
        dk_ref[...] = dkk
        dv_ref[...] = dvv
        dz_ref[...] = dzz
        for h in range(H):
            dst[h] = dstp[h]

    row = lambda d, c: _chunk_idx_rev(d, c, nC)
    tok = lambda w: pl.BlockSpec((None, CHUNK, w), lambda d, c: (d, row(d, c), 0))
    return pl.pallas_call(
        body, name=name, grid=(2, nC),
        in_specs=[pl.BlockSpec((CHUNK, 512), lambda d, c: (row(d, c), 0)),
                  pl.BlockSpec((CHUNK, 512), lambda d, c: (row(d, c), 1)),
                  pl.BlockSpec((CHUNK, 1024), lambda d, c: (row(d, c), 1)),
                  tok(512),
                  pl.BlockSpec((None, None, H, dv, dk), lambda d, c: (d, row(d, c), 0, 0, 0)),
                  pl.BlockSpec((CHUNK, H * dv), lambda d, c: (row(d, c), 0))],
        out_specs=[tok(512), tok(512), tok(1024), tok(512)],
        out_shape=[jax.ShapeDtypeStruct((2, T, 512), F32), jax.ShapeDtypeStruct((2, T, 512), F32),
                   jax.ShapeDtypeStruct((2, T, 1024), F32), jax.ShapeDtypeStruct((2, T, 512), F32)],
        scratch_shapes=[pltpu.VMEM((H, dv, dk), F32)],
        compiler_params=_params("arbitrary", "arbitrary"))(proj, proj, proj, z, sprev, do)


def _l0_combine_fwd(hs, proj, o, gain, name):
    T = proj.shape[0]
    tm = min(512, T)

    def body(hf, hb, ga, of, ob, g, gn, out):
        out[...] = _l0_combine(hf[...], hb[...], ga[...], of[...], ob[...], g[...], gn[...]).astype(BF16)

    two = lambda lead: pl.BlockSpec((None, tm, 512), lambda i: (lead, i, 0))
    return pl.pallas_call(
        body, name=name, grid=(T // tm,),
        in_specs=[two(0), two(1), pl.BlockSpec((tm, 512), lambda i: (i, 1)), two(0), two(1),
                  pl.BlockSpec((tm, 512), lambda i: (i, 6)), pl.BlockSpec((1, 512), lambda i: (0, 0))],
        out_specs=pl.BlockSpec((tm, 1024), lambda i: (i, 0)),
        out_shape=jax.ShapeDtypeStruct((T, 1024), BF16),
        compiler_params=_params("parallel"))(hs, hs, proj, o, o, proj, gain)


def _l0_combine_bwd(hs, proj, o, gain, dmix, name):
    T = proj.shape[0]
    tm = min(512, T)

    def body(hf, hb, ga, of, ob, g, gn, dm, dho_ref, dga_ref, do_ref, dg_ref, dgn_ref):
        @pl.when(pl.program_id(0) == 0)
        def _():
            dgn_ref[...] = jnp.zeros_like(dgn_ref)

        _, vjp = jax.vjp(_l0_combine, hf[...], hb[...], ga[...], of[...], ob[...], g[...], gn[...])
        dhf, _, dga, dof, _, dg, dgn = vjp(dm[...])
        dho_ref[...] = dhf
        dga_ref[...] = dga
        do_ref[...] = dof
        dg_ref[...] = dg
        dgn_ref[...] += dgn

    two = lambda lead: pl.BlockSpec((None, tm, 512), lambda i: (lead, i, 0))
    tok = lambda: pl.BlockSpec((tm, 512), lambda i: (i, 0))
    return pl.pallas_call(
        body, name=name, grid=(T // tm,),
        in_specs=[two(0), two(1), pl.BlockSpec((tm, 512), lambda i: (i, 1)), two(0), two(1),
                  pl.BlockSpec((tm, 512), lambda i: (i, 6)), pl.BlockSpec((1, 512), lambda i: (0, 0)),
                  pl.BlockSpec((tm, 1024), lambda i: (i, 0))],
        out_specs=[tok(), tok(), tok(), tok(), pl.BlockSpec((1, 512), lambda i: (0, 0))],
        out_shape=[jax.ShapeDtypeStruct((T, 512), F32)] * 4 + [jax.ShapeDtypeStruct((1, 512), F32)],
        compiler_params=_params("arbitrary"))(hs, hs, proj, o, o, proj, gain, dmix)


def _l0_assemble(dxa, dga, dq, df, dv, dg, name):
    T = dxa.shape[0]
    tm = min(512, T)

    def body(xa, ga, q0, q1, f0, f1, v0, v1, g, out):
        out[...] = jnp.concatenate([xa[...], ga[...], q0[...] + q1[...], f0[...], f1[...], v0[...] + v1[...],
                                    g[...]], axis=1).astype(BF16)

    two = lambda lead: pl.BlockSpec((None, tm, 512), lambda i: (lead, i, 0))
    tok = lambda: pl.BlockSpec((tm, 512), lambda i: (i, 0))
    return pl.pallas_call(
        body, name=name, grid=(T // tm,),
        in_specs=[tok(), tok(), two(0), two(1), two(0), two(1), two(0), two(1), tok()],
        out_specs=pl.BlockSpec((tm, AB_IN), lambda i: (i, 0)),
        out_shape=jax.ShapeDtypeStruct((T, AB_IN), BF16),
        compiler_params=_params("parallel"))(dxa, dga, dq, dq, df, df, dv, dv, dg)


def _l1_combine_fwd(o, proj, gain, name):
    T = proj.shape[0]
    tm = min(512, T)

    def body(of, ob, r, gn, out):
        out[...] = _l1_combine(of[...], ob[...], r[...], gn[...]).astype(BF16)

    two = lambda lead: pl.BlockSpec((None, tm, 1024), lambda i: (lead, i, 0))
    return pl.pallas_call(
        body, name=name, grid=(T // tm,),
        in_specs=[two(0), two(1), pl.BlockSpec((tm, 1024), lambda i: (i, 2)), pl.BlockSpec((1, 1024), lambda i: (0, 0))],
        out_specs=pl.BlockSpec((tm, 1024), lambda i: (i, 0)),
        out_shape=jax.ShapeDtypeStruct((T, 1024), BF16),
        compiler_params=_params("parallel"))(o, o, proj, gain)


def _l1_combine_bwd(o, proj, gain, dmix, name):
    T = proj.shape[0]
    tm = min(512, T)

    def body(of, ob, r, gn, dm, do_ref, dr_ref, dgn_ref):
        @pl.when(pl.program_id(0) == 0)
        def _():
            dgn_ref[...] = jnp.zeros_like(dgn_ref)

        _, vjp = jax.vjp(_l1_combine, of[...], ob[...], r[...], gn[...])
        dof, _, dr, dgn = vjp(dm[...])
        do_ref[...] = dof
        dr_ref[...] = dr
        dgn_ref[...] += dgn

    two = lambda lead: pl.BlockSpec((None, tm, 1024), lambda i: (lead, i, 0))
    tok = lambda: pl.BlockSpec((tm, 1024), lambda i: (i, 0))
    return pl.pallas_call(
        body, name=name, grid=(T // tm,),
        in_specs=[two(0), two(1), pl.BlockSpec((tm, 1024), lambda i: (i, 2)),
                  pl.BlockSpec((1, 1024), lambda i: (0, 0)), tok()],
        out_specs=[tok(), tok(), pl.BlockSpec((1, 1024), lambda i: (0, 0))],
        out_shape=[jax.ShapeDtypeStruct((T, 1024), F32)] * 2 + [jax.ShapeDtypeStruct((1, 1024), F32)],
        compiler_params=_params("arbitrary"))(o, o, proj, gain, dmix)


def _l1_assemble(dq, dk, dv, dr, dlr, name):
    T = dr.shape[0]
    tm = min(512, T)

    def body(q0, q1, k0, k1, v0, v1, r, a, out):
        out[...] = jnp.concatenate([q0[...] + q1[...], k0[...] + k1[...], v0[...] + v1[...], r[...], a[...]],
                                   axis=1).astype(BF16)

    two = lambda lead, w: pl.BlockSpec((None, tm, w), lambda i: (lead, i, 0))
    return pl.pallas_call(
        body, name=name, grid=(T // tm,),
        in_specs=[two(0, 512), two(1, 512), two(0, 512), two(1, 512), two(0, 1024), two(1, 1024),
                  pl.BlockSpec((tm, 1024), lambda i: (i, 0)), pl.BlockSpec((tm, LANES), lambda i: (i, 0))],
        out_specs=pl.BlockSpec((tm, GLA_IN_PAD), lambda i: (i, 0)),
        out_shape=jax.ShapeDtypeStruct((T, GLA_IN_PAD), BF16),
        compiler_params=_params("parallel"))(dq, dq, dk, dk, dv, dv, dr, dlr)


HBM_SPEC = pl.BlockSpec(memory_space=pltpu.HBM)


def _place():
    x, y, c = lax.axis_index("x"), lax.axis_index("y"), lax.axis_index("c")
    return x, y, c


def _allgather_hbm(shards, name):
    n = len(shards)

    def body(*refs):
        ins, outs = refs[:n], refs[n:2 * n]
        send_sems, recv_sems, local_sems = refs[2 * n:]
        x, y, c = _place()
        me, sibling = (x, y, c), (x, y, 1 - c)
        chips = [(1 - x, y), (x, 1 - y), (1 - x, 1 - y)]

        def slot(a, p):
            return outs[a].at[4 * p[0] + 2 * p[1] + p[2]]

        def copy(a, k, block, to, src=None):
            return pltpu.make_async_remote_copy(
                src_ref=slot(a, block) if src is None else src, dst_ref=slot(a, block),
                send_sem=send_sems.at[a * 7 + k], recv_sem=recv_sems.at[a * 7 + k],
                device_id=to, device_id_type=MESH)

        mine = [pltpu.make_async_copy(ins[a], slot(a, me), local_sems.at[a]) for a in range(n)]
        for cp in mine:
            cp.start()
        first = []
        for a in range(n):
            first.append(copy(a, 0, me, sibling, src=ins[a]))
            first += [copy(a, 1 + j, me, (*chip, c), src=ins[a]) for j, chip in enumerate(chips)]
        for cp in first:
            cp.start()
        passed = []
        for j, chip in enumerate(chips):
            for a in range(n):
                copy(a, 1 + j, (*chip, c), me).wait_recv()
                cp = copy(a, 4 + j, (*chip, c), sibling)
                cp.start()
                passed.append(cp)
        for a in range(n):
            copy(a, 0, sibling, me).wait_recv()
            for j, chip in enumerate(chips):
                copy(a, 4 + j, (*chip, 1 - c), me).wait_recv()
        for cp in first + passed:
            cp.wait_send()
        for cp in mine:
            cp.wait()

    return pl.pallas_call(
        body, name=name,
        in_specs=[HBM_SPEC] * n, out_specs=[HBM_SPEC] * n,
        out_shape=[jax.ShapeDtypeStruct((N_DEV,) + s.shape, s.dtype) for s in shards],
        scratch_shapes=[pltpu.SemaphoreType.DMA((7 * n,)), pltpu.SemaphoreType.DMA((7 * n,)),
                        pltpu.SemaphoreType.DMA((n,))],
        compiler_params=pltpu.CompilerParams(has_side_effects=True))(*shards)


def _allgather_vmem(x_shard, name, reduce=False):
    m_per, n = x_shard.shape

    def body(x_ref, out_ref, *rest):
        if reduce:
            sum_ref, send_sems, recv_sems, local_sem = rest
        else:
            send_sems, recv_sems, local_sem = rest
        x, y, c = _place()
        me, sibling = (x, y, c), (x, y, 1 - c)
        chips = [(1 - x, y), (x, 1 - y), (1 - x, 1 - y)]

        def rows(px, py, pc):
            return out_ref.at[pl.ds((4 * px + 2 * py + pc) * m_per, m_per), :]

        def copy(k, block, to, src=None):
            return pltpu.make_async_remote_copy(
                src_ref=rows(*block) if src is None else src, dst_ref=rows(*block),
                send_sem=send_sems.at[k], recv_sem=recv_sems.at[k], device_id=to, device_id_type=MESH)

        mine = pltpu.make_async_copy(x_ref, rows(*me), local_sem)
        mine.start()
        first = [copy(0, me, sibling, src=x_ref)]
        first += [copy(1 + j, me, (*chip, c), src=x_ref) for j, chip in enumerate(chips)]
        for cp in first:
            cp.start()
        passed = [copy(4 + j, (*chip, c), sibling) for j, chip in enumerate(chips)]
        for j, chip in enumerate(chips):
            copy(1 + j, (*chip, c), me).wait_recv()
            passed[j].start()
        copy(0, sibling, me).wait_recv()
        for j, chip in enumerate(chips):
            copy(4 + j, (*chip, 1 - c), me).wait_recv()
        for cp in first + passed:
            cp.wait_send()
        mine.wait()
        if reduce:
            acc = out_ref[pl.ds(0, m_per), :]
            for j in range(1, N_DEV):
                acc = acc + out_ref[pl.ds(j * m_per, m_per), :]
            sum_ref[...] = acc

    vm = pl.BlockSpec(memory_space=pltpu.VMEM)
    out_shape = [jax.ShapeDtypeStruct((N_DEV * m_per, n), x_shard.dtype)]
    if reduce:
        out_shape.append(jax.ShapeDtypeStruct((m_per, n), x_shard.dtype))
    res = pl.pallas_call(
        body, name=name, in_specs=[vm], out_specs=[vm] * len(out_shape), out_shape=out_shape,
        scratch_shapes=[pltpu.SemaphoreType.DMA((7,)), pltpu.SemaphoreType.DMA((7,)), pltpu.SemaphoreType.DMA],
        compiler_params=pltpu.CompilerParams(has_side_effects=True, vmem_limit_bytes=VMEM_LIMIT))(x_shard)
    return res[1] if reduce else res[0]


def _exchange_sibling(grads, name):
    n = len(grads)

    def body(*refs):
        ins, outs = refs[:n], refs[n:2 * n]
        send_sems, recv_sems = refs[2 * n:]
        x, y, c = _place()
        sibling = (x, y, 1 - c)
        copies = []
        for a in range(n):
            for q in range(4):
                copies.append(pltpu.make_async_remote_copy(
                    src_ref=ins[a].at[2 * q + (1 - c)], dst_ref=outs[a].at[q],
                    send_sem=send_sems.at[a * 4 + q], recv_sem=recv_sems.at[a * 4 + q],
                    device_id=sibling, device_id_type=MESH))
        for cp in copies:
            cp.start()
        for cp in copies:
            cp.wait()

    return pl.pallas_call(
        body, name=name, in_specs=[HBM_SPEC] * n, out_specs=[HBM_SPEC] * n,
        out_shape=[jax.ShapeDtypeStruct((4,) + g.shape[1:], g.dtype) for g in grads],
        scratch_shapes=[pltpu.SemaphoreType.DMA((4 * n,)), pltpu.SemaphoreType.DMA((4 * n,))],
        compiler_params=pltpu.CompilerParams(has_side_effects=True))(*grads)


def _exchange_chips(parts, name):
    n = len(parts)

    def body(*refs):
        ins, outs = refs[:n], refs[n:2 * n]
        send_sems, recv_sems = refs[2 * n:]
        x, y, c = _place()
        chips = [(1 - x, y), (x, 1 - y), (1 - x, 1 - y)]
        copies = []
        for a in range(n):
            for k, chip in enumerate(chips):
                copies.append(pltpu.make_async_remote_copy(
                    src_ref=ins[a].at[2 * chip[0] + chip[1]], dst_ref=outs[a].at[k],
                    send_sem=send_sems.at[a * 3 + k], recv_sem=recv_sems.at[a * 3 + k],
                    device_id=(*chip, c), device_id_type=MESH))
        for cp in copies:
            cp.start()
        for cp in copies:
            cp.wait()

    return pl.pallas_call(
        body, name=name, in_specs=[HBM_SPEC] * n, out_specs=[HBM_SPEC] * n,
        out_shape=[jax.ShapeDtypeStruct((3,) + p.shape[1:], p.dtype) for p in parts],
        scratch_shapes=[pltpu.SemaphoreType.DMA((3 * n,)), pltpu.SemaphoreType.DMA((3 * n,))],
        compiler_params=pltpu.CompilerParams(has_side_effects=True))(*parts)


def _chip_partial(g, r1, place, name):
    _, R, C = g.shape
    tr = min(256, R)
    assert R % tr == 0

    def body(pl_ref, g_ref, r_ref, pb_ref, pm_ref):
        q = pl.program_id(1)
        s = g_ref[...] + r_ref[...]
        pb_ref[...] = s.astype(BF16)

        @pl.when(q == pl_ref[1])
        def _():
            pm_ref[...] = s

    grid_spec = pltpu.PrefetchScalarGridSpec(
        num_scalar_prefetch=1, grid=(R // tr, 4),
        in_specs=[pl.BlockSpec((None, tr, C), lambda r, q, p: (2 * q + p[0], r, 0)),
                  pl.BlockSpec((None, tr, C), lambda r, q, p: (q, r, 0))],
        out_specs=[pl.BlockSpec((None, tr, C), lambda r, q, p: (q, r, 0)),
                   pl.BlockSpec((tr, C), lambda r, q, p: (r, 0))])
    return pl.pallas_call(
        body, name=name, grid_spec=grid_spec,
        out_shape=[jax.ShapeDtypeStruct((4, R, C), BF16), jax.ShapeDtypeStruct((R, C), F32)],
        compiler_params=_params("parallel", "arbitrary"))(place, g, r1)


def _adamw(w, gparts, m, v, name):
    R, C = w.shape
    tr = min(256, R)
    assert R % tr == 0
    g0, g3 = gparts

    def body(w_ref, g0_ref, *rest):
        if g3 is not None:
            g3_ref, m_ref, v_ref, go, do, mo, vo = rest
        else:
            m_ref, v_ref, go, do, mo, vo = rest
        g = g0_ref[...]
        if g3 is not None:
            for k in range(3):
                g = g + g3_ref[k].astype(F32)
        wv = w_ref[...]
        mn = ADAM_B1 * m_ref[...] + (1.0 - ADAM_B1) * g
        vn = ADAM_B2 * v_ref[...] + (1.0 - ADAM_B2) * jnp.square(g)
        m_hat = mn / (1.0 - ADAM_B1 ** ADAM_STEP)
        v_hat = vn / (1.0 - ADAM_B2 ** ADAM_STEP)
        go[...] = g
        do[...] = -ADAM_LR * (m_hat / (jnp.sqrt(v_hat) + ADAM_EPS) + ADAM_WD * wv)
        mo[...] = mn
        vo[...] = vn

    blk = pl.BlockSpec((tr, C), lambda i: (i, 0))
    in_specs = [blk, blk] + ([pl.BlockSpec((3, tr, C), lambda i: (0, i, 0))] if g3 is not None else []) + [blk, blk]
    args = [w, g0] + ([g3] if g3 is not None else []) + [m, v]
    return pl.pallas_call(
        body, name=name, grid=(R // tr,), in_specs=in_specs, out_specs=[blk] * 4,
        out_shape=[jax.ShapeDtypeStruct((R, C), F32)] * 4,
        compiler_params=_params("parallel"))(*args)


SMALL_SHARDED = ("rg_conv_w", "rg_b_a", "rg_b_x", "rg_lambda", "gla_w_gate_up", "gla_b_gate", "gla_norm")
SMALL_REPLICATED = ("norm_mix", "norm_mlp", "norm_final", "rg_conv_b", "rg_w_a", "rg_w_x", "hg_lb_logits", "hg_norm")
WEIGHT_NAMES = ("norm_mix", "norm_mlp", "norm_final", "mlp_w1", "mlp_w2", "ab_w_in", "ab_w_out", "rg_conv_w",
                "rg_conv_b", "rg_w_a", "rg_b_a", "rg_w_x", "rg_b_x", "rg_lambda", "hg_lb_logits", "hg_norm",
                "gla_w_in", "gla_w_out", "gla_w_gate_up", "gla_b_gate", "gla_norm")


def _rows128(a):
    return a.reshape(-1, LANES)


def _part_rows(a):
    return -(-(a.size // LANES) // SUBLANES) * SUBLANES


def _pack_rows(arrays, pad_to=SUBLANES):
    parts = [jnp.pad(_rows128(a), ((0, _part_rows(a) - a.size // LANES), (0, 0))) for a in arrays]
    total = sum(p.shape[0] for p in parts)
    extra = (-total) % pad_to
    if extra:
        parts.append(jnp.zeros((extra, LANES), parts[0].dtype))
    return jnp.concatenate(parts, axis=0)


def _unshard_last(g, shape_local):
    nd = len(shape_local)
    t = g.reshape((N_DEV,) + tuple(shape_local))
    t = jnp.moveaxis(t, 0, nd - 1)
    return t.reshape(tuple(shape_local[:-1]) + (N_DEV * shape_local[-1],))


def _block_diag(w):
    eye = jnp.eye(8, dtype=w.dtype)
    return (w[:, :, :, None, :] * eye[None, :, None, :, None]).reshape(2, RG_W, RG_W)


def _block_diag_extract(dw):
    t = dw.reshape(2, 8, 64, 8, 64)
    return jnp.moveaxis(jnp.diagonal(t, axis1=1, axis2=3), -1, 1)


def kernel(x, norm_mix, norm_mlp, norm_final, mlp_w1, mlp_w2, ab_w_in, ab_w_out, rg_conv_w, rg_conv_b, rg_w_a, rg_b_a, rg_w_x, rg_b_x, rg_lambda, hg_lb_logits, hg_norm, gla_w_in, gla_w_out, gla_w_gate_up, gla_b_gate, gla_norm, loss_target, m_norm_mix, m_norm_mlp, m_norm_final, m_mlp_w1, m_mlp_w2, m_ab_w_in, m_ab_w_out, m_rg_conv_w, m_rg_conv_b, m_rg_w_a, m_rg_b_a, m_rg_w_x, m_rg_b_x, m_rg_lambda, m_hg_lb_logits, m_hg_norm, m_gla_w_in, m_gla_w_out, m_gla_w_gate_up, m_gla_b_gate, m_gla_norm, v_norm_mix, v_norm_mlp, v_norm_final, v_mlp_w1, v_mlp_w2, v_ab_w_in, v_ab_w_out, v_rg_conv_w, v_rg_conv_b, v_rg_w_a, v_rg_b_a, v_rg_w_x, v_rg_b_x, v_rg_lambda, v_hg_lb_logits, v_hg_norm, v_gla_w_in, v_gla_w_out, v_gla_w_gate_up, v_gla_b_gate, v_gla_norm):
    w_loc = dict(norm_mix=norm_mix, norm_mlp=norm_mlp, norm_final=norm_final, mlp_w1=mlp_w1, mlp_w2=mlp_w2,
                 ab_w_in=ab_w_in, ab_w_out=ab_w_out, rg_conv_w=rg_conv_w, rg_conv_b=rg_conv_b, rg_w_a=rg_w_a,
                 rg_b_a=rg_b_a, rg_w_x=rg_w_x, rg_b_x=rg_b_x, rg_lambda=rg_lambda, hg_lb_logits=hg_lb_logits,
                 hg_norm=hg_norm, gla_w_in=gla_w_in, gla_w_out=gla_w_out, gla_w_gate_up=gla_w_gate_up,
                 gla_b_gate=gla_b_gate, gla_norm=gla_norm)
    m_loc = dict(norm_mix=m_norm_mix, norm_mlp=m_norm_mlp, norm_final=m_norm_final, mlp_w1=m_mlp_w1,
                 mlp_w2=m_mlp_w2, ab_w_in=m_ab_w_in, ab_w_out=m_ab_w_out, rg_conv_w=m_rg_conv_w,
                 rg_conv_b=m_rg_conv_b, rg_w_a=m_rg_w_a, rg_b_a=m_rg_b_a, rg_w_x=m_rg_w_x, rg_b_x=m_rg_b_x,
                 rg_lambda=m_rg_lambda, hg_lb_logits=m_hg_lb_logits, hg_norm=m_hg_norm, gla_w_in=m_gla_w_in,
                 gla_w_out=m_gla_w_out, gla_w_gate_up=m_gla_w_gate_up, gla_b_gate=m_gla_b_gate,
                 gla_norm=m_gla_norm)
    v_loc = dict(norm_mix=v_norm_mix, norm_mlp=v_norm_mlp, norm_final=v_norm_final, mlp_w1=v_mlp_w1,
                 mlp_w2=v_mlp_w2, ab_w_in=v_ab_w_in, ab_w_out=v_ab_w_out, rg_conv_w=v_rg_conv_w,
                 rg_conv_b=v_rg_conv_b, rg_w_a=v_rg_w_a, rg_b_a=v_rg_b_a, rg_w_x=v_rg_w_x, rg_b_x=v_rg_b_x,
                 rg_lambda=v_rg_lambda, hg_lb_logits=v_hg_lb_logits, hg_norm=v_hg_norm, gla_w_in=v_gla_w_in,
                 gla_w_out=v_gla_w_out, gla_w_gate_up=v_gla_w_gate_up, gla_b_gate=v_gla_b_gate,
                 gla_norm=v_gla_norm)

    T = x.shape[1]
    h0 = x.reshape(T, D_MODEL)
    target = loss_target.reshape(T, D_MODEL)
    ax, ay, ac = lax.axis_index("x"), lax.axis_index("y"), lax.axis_index("c")
    dev = 4 * ax + 2 * ay + ac
    place = jnp.stack([ac, 2 * ax + ay]).astype(jnp.int32)

    big_shards = [mlp_w1[0].astype(BF16), mlp_w1[1].astype(BF16), mlp_w2[0].astype(BF16), mlp_w2[1].astype(BF16),
                  ab_w_in[0].astype(BF16), ab_w_out[0].astype(BF16), gla_w_in[0].astype(BF16),
                  gla_w_out[0].astype(BF16)]
    w1g0, w1g1, w2g0, w2g1, abin_g, about_g, glain_g, glaout_g = _allgather_hbm(big_shards, "ag_weights")
    w1g = (w1g0, w1g1)
    w2f = (w2g0.reshape(D_FF, D_MODEL), w2g1.reshape(D_FF, D_MODEL))
    wab_in = jnp.transpose(abin_g, (1, 0, 2)).reshape(D_MODEL, AB_IN)
    wab_out = about_g.reshape(D_MODEL, D_MODEL)
    wgla_in = jnp.pad(jnp.transpose(glain_g, (1, 0, 2)).reshape(D_MODEL, GLA_IN), ((0, 0), (0, GLA_IN_PAD - GLA_IN)))
    wgla_out = glaout_g.reshape(D_MODEL, D_MODEL)

    small_local = [w_loc[n] for n in SMALL_SHARDED]
    small_g = _allgather_vmem(_pack_rows(small_local, 8), "ag_small")
    small_g = small_g.reshape(N_DEV, -1, LANES)
    full = {}
    off = 0
    for n, a in zip(SMALL_SHARDED, small_local):
        full[n] = _unshard_last(small_g[:, off:off + a.size // LANES].reshape(N_DEV, a.size), a.shape)
        off += _part_rows(a)
    conv_w = full["rg_conv_w"][0]
    b_a, b_x, lam = full["rg_b_a"][0], full["rg_b_x"][0], full["rg_lambda"][0]
    w_up, b_gate, g_norm = full["gla_w_gate_up"][0], full["gla_b_gate"][0], full["gla_norm"]

    cw8 = jnp.pad(conv_w, ((0, 4), (0, 0)))
    wbd = jnp.concatenate([_block_diag(rg_w_a[0]), _block_diag(rg_w_x[0])], axis=2).astype(BF16)
    rg_bias = jnp.concatenate([b_a, b_x], axis=1).reshape(2, 1, 2 * RG_W)
    lam3 = lam.reshape(2, 1, RG_W)
    l0, l1 = hg_lb_logits[0:1], hg_lb_logits[1:2]
    wup_pad = jnp.zeros((2, LANES, 512), F32).at[0, 0:16].set(w_up[0]).at[1, 16:32].set(w_up[1])
    bg3 = b_gate.reshape(2, 1, 512)
    nmix0, nmix1 = norm_mix[0:1], norm_mix[1:2]
    nmlp0, nmlp1 = norm_mlp[0:1], norm_mlp[1:2]
    nfin = norm_final.reshape(1, D_MODEL)

    proj0, y0 = _norm_matmul(h0, nmix0, wab_in, 512, "l0_in_proj")
    xc = _rg_conv_fwd(proj0, cw8, rg_conv_b, "rg_conv")
    hs = _rg_scan_fwd(xc, wbd, rg_bias, lam3, "rg_scan")
    o_hg, s_hg = _hg_fwd(proj0, l0, l1, "hg_chunks")
    mixin0 = _l0_combine_fwd(hs, proj0, o_hg, hg_norm, "l0_combine")
    h1 = _matmul_res(mixin0, wab_out, h0, "l0_out_proj")
    h2, pre0, ym0 = _mlp_fwd(h1, nmlp0, w1g[0], w2f[0], "mlp0")
    proj1, y1 = _norm_matmul(h2, nmix1, wgla_in, 640, "l1_in_proj")
    z_gate, lr_b = _gate_logits(proj1, wup_pad, bg3, "gla_gate_logits")
    o_gla, s_gla = _gla_fwd(proj1, z_gate, "gla_chunks")
    mixin1 = _l1_combine_fwd(o_gla, proj1, g_norm, "l1_combine")
    h3 = _matmul_res(mixin1, wgla_out, h2, "l1_out_proj")
    h4, pre1, ym1 = _mlp_fwd(h3, nmlp1, w1g[1], w2f[1], "mlp1")
    loss_blk, dh4, d_nfin = _final_loss(h4, nfin, target, "final_loss")
    loss = lax.psum(loss_blk[0, 0], ("x", "y", "c"))

    dh3, dpre1, act1, d_nmlp1 = _mlp_bwd(dh4, h3, nmlp1, pre1, w1g[1], w2f[1], "mlp1_bwd")
    g_w1_1 = _wgrad(ym1, dpre1, 512, "mlp1_dw1", sharded_cols=True)
    g_w2_1 = _wgrad(act1, dh4.astype(BF16), 512, "mlp1_dw2")
    dh3b = dh3.astype(BF16)
    dmixin1 = _dgrad(dh3, wgla_out, "l1_out_dgrad")
    g_gla_out = _wgrad(mixin1, dh3b, 512, "l1_out_dw")
    do_gla, dr, d_gnorm = _l1_combine_bwd(o_gla, proj1, g_norm, dmixin1, "l1_combine_bwd")
    dq1, dk1, dv1, dz_gate = _gla_bwd(proj1, z_gate, s_gla, do_gla, "gla_chunks_bwd")
    dlr1, d_bg = _gate_logits_bwd(dz_gate, wup_pad, "gla_gate_logits_bwd")
    dz_b = dz_gate.astype(BF16)
    d_wup = [_wgrad(lr_b, dz_b[d], 512, "gla_gate_dw%d" % d) for d in range(2)]
    dproj1 = _l1_assemble(dq1, dk1, dv1, dr, dlr1, "l1_assemble")
    dh2, d_nmix1 = _dgrad_norm(dproj1, wgla_in, h2, nmix1, dh3, 640, "l1_in_dgrad")
    g_gla_in = _wgrad(y1, dproj1, 640, "l1_in_dw")

    dh1, dpre0, act0, d_nmlp0 = _mlp_bwd(dh2, h1, nmlp0, pre0, w1g[0], w2f[0], "mlp0_bwd")
    g_w1_0 = _wgrad(ym0, dpre0, 512, "mlp0_dw1", sharded_cols=True)
    g_w2_0 = _wgrad(act0, dh2.astype(BF16), 512, "mlp0_dw2")
    dmixin0 = _dgrad(dh1, wab_out, "l0_out_dgrad")
    g_ab_out = _wgrad(mixin0, dh1.astype(BF16), 512, "l0_out_dw")
    dho, dga, do_hg, dg_gate, d_hgnorm = _l0_combine_bwd(hs, proj0, o_hg, hg_norm, dmixin0, "l0_combine_bwd")
    dxc, d_wbd, d_rgb, d_lam = _rg_scan_bwd(xc, wbd, rg_bias, lam3, hs, dho, "rg_scan_bwd")
    dxa, d_cw8, d_cb = _rg_conv_bwd(dxc, proj0, cw8, "rg_conv_bwd")
    dq0, df0, dv0, d_l0, d_l1 = _hg_bwd(proj0, l0, l1, s_hg, do_hg, "hg_chunks_bwd")
    dproj0 = _l0_assemble(dxa, dga, dq0, df0, dv0, dg_gate, "l0_assemble")
    dx, d_nmix0 = _dgrad_norm(dproj0, wab_in, h0, nmix0, dh1, 512, "l0_in_dgrad")
    g_ab_in = _wgrad(y0, dproj0, 512, "l0_in_dw")

    big_grads = [g_w1_0, g_w1_1, g_w2_0.reshape(N_DEV, 512, D_MODEL), g_w2_1.reshape(N_DEV, 512, D_MODEL),
                 jnp.transpose(g_ab_in.reshape(D_MODEL, N_DEV, AB_IN // N_DEV), (1, 0, 2)),
                 g_ab_out.reshape(N_DEV, 128, D_MODEL),
                 jnp.transpose(g_gla_in[:, :GLA_IN].reshape(D_MODEL, N_DEV, GLA_IN // N_DEV), (1, 0, 2)),
                 g_gla_out.reshape(N_DEV, 128, D_MODEL)]
    from_sibling = _exchange_sibling(big_grads, "rs_sibling")
    partials = [_chip_partial(g, r, place, "rs_partial_%d" % a) for a, (g, r) in enumerate(zip(big_grads, from_sibling))]
    from_chips = _exchange_chips([p[0] for p in partials], "rs_chips")
    big_w = [mlp_w1[0], mlp_w1[1], mlp_w2[0], mlp_w2[1], ab_w_in[0], ab_w_out[0], gla_w_in[0], gla_w_out[0]]
    big_m = [m_mlp_w1[0], m_mlp_w1[1], m_mlp_w2[0], m_mlp_w2[1], m_ab_w_in[0], m_ab_w_out[0], m_gla_w_in[0], m_gla_w_out[0]]
    big_v = [v_mlp_w1[0], v_mlp_w1[1], v_mlp_w2[0], v_mlp_w2[1], v_ab_w_in[0], v_ab_w_out[0], v_gla_w_in[0], v_gla_w_out[0]]
    big_res = [_adamw(w, (p[1], r), m, v, "adamw_big_%d" % a)
               for a, (w, p, r, m, v) in enumerate(zip(big_w, partials, from_chips, big_m, big_v))]

    def stacked(i, j):
        return tuple(jnp.stack([big_res[i][k], big_res[j][k]]) for k in range(4))

    res = {"mlp_w1": stacked(0, 1), "mlp_w2": stacked(2, 3),
           "ab_w_in": tuple(big_res[4][k][None] for k in range(4)),
           "ab_w_out": tuple(big_res[5][k][None] for k in range(4)),
           "gla_w_in": tuple(big_res[6][k][None] for k in range(4)),
           "gla_w_out": tuple(big_res[7][k][None] for k in range(4))}

    d_wa = _block_diag_extract(d_wbd[:, :, :RG_W])[None]
    d_wx = _block_diag_extract(d_wbd[:, :, RG_W:])[None]
    small_full = {
        "norm_mix": jnp.concatenate([d_nmix0, d_nmix1], axis=0), "norm_mlp": jnp.concatenate([d_nmlp0, d_nmlp1], axis=0),
        "norm_final": d_nfin.reshape(D_MODEL), "rg_conv_b": d_cb, "rg_w_a": d_wa, "rg_w_x": d_wx,
        "hg_lb_logits": jnp.concatenate([d_l0[0] + d_l0[1], d_l1[0] + d_l1[1]], axis=0), "hg_norm": d_hgnorm,
        "rg_conv_w": d_cw8[0:4][None], "rg_b_a": d_rgb[:, 0, :RG_W][None], "rg_b_x": d_rgb[:, 0, RG_W:][None],
        "rg_lambda": d_lam[:, 0, :][None],
        "gla_w_gate_up": jnp.stack([d_wup[0][0:16], d_wup[1][16:32]])[None], "gla_b_gate": d_bg[:, 0, :][None],
        "gla_norm": d_gnorm}
    small_names = SMALL_REPLICATED + SMALL_SHARDED
    packed = _pack_rows([small_full[n] for n in small_names], 8)
    summed = _allgather_vmem(packed, "ar_small", reduce=True)
    g_small = {}
    off = 0
    for n in small_names:
        a = small_full[n]
        gfull = summed[off:off + a.size // LANES].reshape(a.shape)
        off += _part_rows(a)
        if n in SMALL_SHARDED:
            loc = w_loc[n].shape[-1]
            gfull = lax.dynamic_slice_in_dim(gfull, dev * loc, loc, axis=gfull.ndim - 1)
        g_small[n] = gfull
    sw = _pack_rows([w_loc[n] for n in small_names], 256)
    sg = _pack_rows([g_small[n] for n in small_names], 256)
    sm = _pack_rows([m_loc[n] for n in small_names], 256)
    sv = _pack_rows([v_loc[n] for n in small_names], 256)
    small_res = _adamw(sw, (sg, None), sm, sv, "adamw_small")
    off = 0
    for n in small_names:
        a = w_loc[n]
        nr = a.size // LANES
        res[n] = tuple(small_res[k][off:off + nr].reshape(a.shape) for k in range(4))
        off += _part_rows(a)

    grad_x = dx.reshape(1, T, D_MODEL)
    out = [loss, grad_x]
    for k in range(4):
        out += [res[n][k] for n in WEIGHT_NAMES]
    return tuple(out)
```

```python
import jax
import jax.numpy as jnp
from jax import lax
from jax.experimental import pallas as pl
from jax.experimental.pallas import tpu as pltpu

F32, BF16 = jnp.float32, jnp.bfloat16
HI = lax.Precision.HIGHEST
MESH = pl.DeviceIdType.MESH

D_MODEL = 1024
D_FF = 4096
RG_W = 512
HG_W = 512
CHUNK = 64
EPS = 1e-6
RG_C = 8.0
AB_IN = 3584
GLA_IN = 3104
GLA_IN_PAD = 3200
N_DEV = 8
LANES = 128
SUBLANES = 8
VMEM_LIMIT = 48 * 1024 * 1024

ADAM_LR, ADAM_B1, ADAM_B2, ADAM_EPS, ADAM_WD, ADAM_STEP = 0.001, 0.9, 0.999, 1e-08, 0.01, 10


def _params(*sem):
    return pltpu.CompilerParams(dimension_semantics=sem, vmem_limit_bytes=VMEM_LIMIT)


def _dg(a, b, ca, cb):
    return lax.dot_general(a.astype(BF16), b.astype(BF16), (((ca,), (cb,)), ((), ())),
                           preferred_element_type=F32)


@jax.custom_vjp
def _mm_nn(a, b):
    return _dg(a, b, 1, 0)


_mm_nn.defvjp(lambda a, b: (_dg(a, b, 1, 0), (a, b)),
              lambda res, g: (_dg(g, res[1], 1, 1), _dg(res[0], g, 0, 0)))


@jax.custom_vjp
def _mm_nt(a, b):
    return _dg(a, b, 1, 1)


_mm_nt.defvjp(lambda a, b: (_dg(a, b, 1, 1), (a, b)),
              lambda res, g: (_dg(g, res[1], 1, 0), _dg(g, res[0], 0, 0)))


@jax.custom_vjp
def _mm_tn(a, b):
    return _dg(a, b, 0, 0)


_mm_tn.defvjp(lambda a, b: (_dg(a, b, 0, 0), (a, b)),
              lambda res, g: (_dg(res[1], g, 1, 1), _dg(res[0], g, 1, 0)))


@jax.custom_vjp
def _cum(tri, tri_t, x):
    return jnp.dot(tri, x, precision=HI, preferred_element_type=F32)


_cum.defvjp(lambda tri, tri_t, x: (jnp.dot(tri, x, precision=HI, preferred_element_type=F32), (tri, tri_t)),
            lambda res, g: (jnp.zeros_like(res[0]), jnp.zeros_like(res[1]),
                            jnp.dot(res[1], g, precision=HI, preferred_element_type=F32)))


def _sig(x):
    return 1.0 / (1.0 + jnp.exp(-x))


def _gelu(x):
    return 0.5 * x * (1.0 + jnp.tanh(0.7978845608028654 * (x + 0.044715 * (x * x * x))))


def _softplus(z):
    return jnp.maximum(z, 0.0) + jnp.log(1.0 + jnp.exp(-jnp.abs(z)))


def _rms(x):
    return lax.rsqrt(jnp.mean(x * x, axis=-1, keepdims=True) + EPS)


def _rmsnorm_bwd(x, gain, dy):
    r = _rms(x)
    xh = x * r
    dgain = jnp.sum(dy * xh, axis=0, keepdims=True)
    dxh = dy * gain
    dx = r * (dxh - xh * jnp.mean(dxh * xh, axis=-1, keepdims=True))
    return dx, dgain


def _headnorm(o, gain, n_heads, hd):
    parts = []
    for h in range(n_heads):
        oh = o[:, h * hd:(h + 1) * hd]
        parts.append(oh * _rms(oh))
    return jnp.concatenate(parts, axis=1) * gain


def _tri_consts(d):
    row = lax.broadcasted_iota(jnp.int32, (CHUNK, CHUNK), 0)
    col = lax.broadcasted_iota(jnp.int32, (CHUNK, CHUNK), 1)
    ge = (row >= col).astype(F32)
    le = (row <= col).astype(F32)
    tri = jnp.where(d == 0, ge, le)
    tri_t = jnp.where(d == 0, le, ge)
    r1 = lax.broadcasted_iota(jnp.int32, (CHUNK, 1), 0)
    mref = jnp.where(d == 0, (r1 <= CHUNK // 2).astype(F32), (r1 >= CHUNK // 2 - 1).astype(F32))
    return tri, tri_t, mref


def _chunk_core(qh, k, v, logf, st_prev, tri, tri_t, mref, n_heads, dk, dv):
    cum = _cum(tri, tri_t, logf)
    ref = jnp.sum(logf * mref, axis=0, keepdims=True)
    last = jnp.sum(logf, axis=0, keepdims=True)
    q_in = qh * jnp.exp(cum - ref)
    k_in = k * jnp.exp(ref - cum)
    k_st = k * jnp.exp(last - cum)
    q_dec = qh * jnp.exp(cum)
    decay = jnp.exp(last)
    outs, sts = [], []
    for h in range(n_heads):
        sk = slice(h * dk, (h + 1) * dk)
        sv = slice(h * dv, (h + 1) * dv)
        sc = _mm_nt(q_in[:, sk], k_in[:, sk]) * tri
        o = _mm_nn(sc, v[:, sv]) + _mm_nt(q_dec[:, sk], st_prev[h])
        sts.append(st_prev[h] * decay[:, sk] + _mm_tn(v[:, sv], k_st[:, sk]))
        outs.append(o)
    return jnp.concatenate(outs, axis=1), tuple(sts)


def _hg_chunk(q, f, v, l0, l1, st_prev, tri, tri_t, mref):
    lb = _sig(l0 - l1)
    sg = _sig(f)
    qh = q * _sig(q)
    logf = jnp.log(lb + (1.0 - lb) * sg)
    k = (1.0 - lb) * (1.0 - sg)
    return _chunk_core(qh, k, v, logf, st_prev, tri, tri_t, mref, 4, 128, 128)


def _gla_chunk(q, k, v, z, st_prev, tri, tri_t, mref):
    logf = (jnp.minimum(z, 0.0) - jnp.log(1.0 + jnp.exp(-jnp.abs(z)))) * (1.0 / 16.0)
    qh = q * (128.0 ** -0.5)
    return _chunk_core(qh, k, v, logf, st_prev, tri, tri_t, mref, 4, 128, 256)


def _rg_gates(xc, wbd, bias, lam):
    z = _mm_nn(xc, wbd) + bias
    r = _sig(z[:, :RG_W])
    i = _sig(z[:, RG_W:])
    log_a = -RG_C * r * _softplus(-lam)
    a = jnp.exp(log_a)
    x2 = 2.0 * log_a
    neg_expm1 = jnp.where(x2 > -1e-2, -(x2 + 0.5 * x2 * x2 + x2 * x2 * x2 * (1.0 / 6.0)), 1.0 - jnp.exp(x2))
    u = jnp.sqrt(neg_expm1) * (i * xc)
    return a, u


def _l0_combine(hf, hb, ga, of, ob, g, gain):
    ya = (hf + hb) * _gelu(ga)
    yb = _headnorm(of + ob, gain, 4, 128) * (g * _sig(g))
    return jnp.concatenate([ya, yb], axis=1)


def _l1_combine(of, ob, r, gain):
    return _headnorm(of + ob, gain, 4, 256) * (r * _sig(r))


def _norm_matmul(h, gain, w, name):
    T, D = h.shape
    N = w.shape[1]
    tm = min(512, T)

    def body(h_ref, g_ref, w_ref, o_ref, y_ref):
        x = h_ref[...]
        y = (x * _rms(x) * g_ref[...]).astype(BF16)
        y_ref[...] = y
        o_ref[...] = jnp.dot(y, w_ref[...], preferred_element_type=F32)

    return pl.pallas_call(
        body, name=name, grid=(T // tm,),
        in_specs=[pl.BlockSpec((tm, D), lambda i: (i, 0)), pl.BlockSpec((1, D), lambda i: (0, 0)),
                  pl.BlockSpec((D, N), lambda i: (0, 0))],
        out_specs=[pl.BlockSpec((tm, N), lambda i: (i, 0)), pl.BlockSpec((tm, D), lambda i: (i, 0))],
        out_shape=[jax.ShapeDtypeStruct((T, N), F32), jax.ShapeDtypeStruct((T, D), BF16)],
        compiler_params=_params("parallel"))(h, gain, w)


def _matmul_res(a, w, res, name):
    T, K = a.shape
    N = w.shape[1]
    tm = min(512, T)

    def body(a_ref, w_ref, r_ref, o_ref):
        o_ref[...] = r_ref[...] + jnp.dot(a_ref[...], w_ref[...], preferred_element_type=F32)

    return pl.pallas_call(
        body, name=name, grid=(T // tm,),
        in_specs=[pl.BlockSpec((tm, K), lambda i: (i, 0)), pl.BlockSpec((K, N), lambda i: (0, 0)),
                  pl.BlockSpec((tm, N), lambda i: (i, 0))],
        out_specs=pl.BlockSpec((tm, N), lambda i: (i, 0)),
        out_shape=jax.ShapeDtypeStruct((T, N), F32),
        compiler_params=_params("parallel"))(a, w, res)


def _dgrad(dc, w, name):
    T, N = dc.shape
    K = w.shape[0]
    tm = min(512, T)

    def body(d_ref, w_ref, o_ref):
        o_ref[...] = _dg(d_ref[...], w_ref[...], 1, 1)

    return pl.pallas_call(
        body, name=name, grid=(T // tm,),
        in_specs=[pl.BlockSpec((tm, N), lambda i: (i, 0)), pl.BlockSpec((K, N), lambda i: (0, 0))],
        out_specs=pl.BlockSpec((tm, K), lambda i: (i, 0)),
        out_shape=jax.ShapeDtypeStruct((T, K), F32),
        compiler_params=_params("parallel"))(dc, w)


def _dgrad_norm(dproj, w, h, gain, dres, name):
    T, N = dproj.shape
    D = w.shape[0]
    tm = min(512, T)

    def body(dp_ref, w_ref, h_ref, g_ref, dr_ref, dh_ref, dhb_ref, dg_ref):
        @pl.when(pl.program_id(0) == 0)
        def _():
            dg_ref[...] = jnp.zeros_like(dg_ref)

        dy = _dg(dp_ref[...], w_ref[...], 1, 1)
        dx, dgain = _rmsnorm_bwd(h_ref[...], g_ref[...], dy)
        dh = dr_ref[...] + dx
        dh_ref[...] = dh
        dhb_ref[...] = dh.astype(BF16)
        dg_ref[...] += dgain

    return pl.pallas_call(
        body, name=name, grid=(T // tm,),
        in_specs=[pl.BlockSpec((tm, N), lambda i: (i, 0)), pl.BlockSpec((D, N), lambda i: (0, 0)),
                  pl.BlockSpec((tm, D), lambda i: (i, 0)), pl.BlockSpec((1, D), lambda i: (0, 0)),
                  pl.BlockSpec((tm, D), lambda i: (i, 0))],
        out_specs=[pl.BlockSpec((tm, D), lambda i: (i, 0)), pl.BlockSpec((tm, D), lambda i: (i, 0)),
                   pl.BlockSpec((1, D), lambda i: (0, 0))],
        out_shape=[jax.ShapeDtypeStruct((T, D), F32), jax.ShapeDtypeStruct((T, D), BF16),
                   jax.ShapeDtypeStruct((1, D), F32)],
        compiler_params=_params("arbitrary"))(dproj, w, h, gain, dres)


def _wgrad(a, b, tn, name, sharded_cols=False):
    T, K = a.shape
    N = b.shape[1]
    tk = min(1024, K)
    tt = min(1024, T)
    nt = T // tt

    def body(a_ref, b_ref, o_ref):
        @pl.when(pl.program_id(2) == 0)
        def _():
            o_ref[...] = jnp.zeros_like(o_ref)

        o_ref[...] += _dg(a_ref[...], b_ref[...], 0, 0)

    if sharded_cols:
        out_spec = pl.BlockSpec((None, tk, tn), lambda k, n, t: (n, k, 0))
        out_shape = jax.ShapeDtypeStruct((N // tn, K, tn), F32)
    else:
        out_spec = pl.BlockSpec((tk, tn), lambda k, n, t: (k, n))
        out_shape = jax.ShapeDtypeStruct((K, N), F32)
    return pl.pallas_call(
        body, name=name, grid=(K // tk, N // tn, nt),
        in_specs=[pl.BlockSpec((tt, tk), lambda k, n, t: (t, k)), pl.BlockSpec((tt, tn), lambda k, n, t: (t, n))],
        out_specs=out_spec, out_shape=out_shape,
        compiler_params=_params("parallel", "parallel", "arbitrary"))(a, b)


def _mlp_fwd(h, gain, w1g, w2, name):
    T, D = h.shape
    nf, _, tf = w1g.shape
    tm = min(1024, T)

    def body(h_ref, g_ref, w1_ref, w2_ref, o_ref, pre_ref, y_ref, ysc, acc):
        j = pl.program_id(1)

        @pl.when(j == 0)
        def _():
            x = h_ref[...]
            y = (x * _rms(x) * g_ref[...]).astype(BF16)
            ysc[...] = y
            y_ref[...] = y
            acc[...] = jnp.zeros_like(acc)

        pre = jnp.dot(ysc[...], w1_ref[...], preferred_element_type=F32)
        pre_ref[...] = pre.astype(BF16)
        act = jnp.square(jnp.maximum(pre, 0.0))
        acc[...] += jnp.dot(act.astype(BF16), w2_ref[...], preferred_element_type=F32)

        @pl.when(j == nf - 1)
        def _():
            o_ref[...] = h_ref[...] + acc[...]

    return pl.pallas_call(
        body, name=name, grid=(T // tm, nf),
        in_specs=[pl.BlockSpec((tm, D), lambda i, j: (i, 0)), pl.BlockSpec((1, D), lambda i, j: (0, 0)),
                  pl.BlockSpec((None, D, tf), lambda i, j: (j, 0, 0)), pl.BlockSpec((tf, D), lambda i, j: (j, 0))],
        out_specs=[pl.BlockSpec((tm, D), lambda i, j: (i, 0)), pl.BlockSpec((tm, tf), lambda i, j: (i, j)),
                   pl.BlockSpec((tm, D), lambda i, j: (i, 0))],
        out_shape=[jax.ShapeDtypeStruct((T, D), F32), jax.ShapeDtypeStruct((T, nf * tf), BF16),
                   jax.ShapeDtypeStruct((T, D), BF16)],
        scratch_shapes=[pltpu.VMEM((tm, D), BF16), pltpu.VMEM((tm, D), F32)],
        compiler_params=_params("parallel", "arbitrary"))(h, gain, w1g, w2)


def _mlp_bwd(dout, h, gain, pre, w1g, w2, name):
    T, D = h.shape
    nf, _, tf = w1g.shape
    tm = min(512, T)

    def body(do_ref, h_ref, g_ref, pre_ref, w1_ref, w2_ref, dh_ref, dhb_ref, dpre_ref, act_ref, dg_ref, dy):
        i, j = pl.program_id(0), pl.program_id(1)

        @pl.when(j == 0)
        def _():
            dy[...] = jnp.zeros_like(dy)

        @pl.when((i == 0) & (j == 0))
        def _():
            dg_ref[...] = jnp.zeros_like(dg_ref)

        rp = jnp.maximum(pre_ref[...].astype(F32), 0.0)
        dact = _dg(do_ref[...], w2_ref[...], 1, 1)
        dpre = (dact * (2.0 * rp)).astype(BF16)
        dpre_ref[...] = dpre
        act_ref[...] = (rp * rp).astype(BF16)
        dy[...] += _dg(dpre, w1_ref[...], 1, 1)

        @pl.when(j == nf - 1)
        def _():
            dx, dgain = _rmsnorm_bwd(h_ref[...], g_ref[...], dy[...])
            dh = do_ref[...] + dx
            dh_ref[...] = dh
            dhb_ref[...] = dh.astype(BF16)
            dg_ref[...] += dgain

    return pl.pallas_call(
        body, name=name, grid=(T // tm, nf),
        in_specs=[pl.BlockSpec((tm, D), lambda i, j: (i, 0)), pl.BlockSpec((tm, D), lambda i, j: (i, 0)),
                  pl.BlockSpec((1, D), lambda i, j: (0, 0)), pl.BlockSpec((tm, tf), lambda i, j: (i, j)),
                  pl.BlockSpec((None, D, tf), lambda i, j: (j, 0, 0)), pl.BlockSpec((tf, D), lambda i, j: (j, 0))],
        out_specs=[pl.BlockSpec((tm, D), lambda i, j: (i, 0)), pl.BlockSpec((tm, D), lambda i, j: (i, 0)),
                   pl.BlockSpec((tm, tf), lambda i, j: (i, j)),
                   pl.BlockSpec((tm, tf), lambda i, j: (i, j)), pl.BlockSpec((1, D), lambda i, j: (0, 0))],
        out_shape=[jax.ShapeDtypeStruct((T, D), F32), jax.ShapeDtypeStruct((T, D), BF16),
                   jax.ShapeDtypeStruct((T, nf * tf), BF16),
                   jax.ShapeDtypeStruct((T, nf * tf), BF16), jax.ShapeDtypeStruct((1, D), F32)],
        scratch_shapes=[pltpu.VMEM((tm, D), F32)],
        compiler_params=_params("arbitrary", "arbitrary"))(dout, h, gain, pre, w1g, w2)


def _final_loss(h, gain, target, name):
    T, D = h.shape
    tm = min(512, T)

    def body(h_ref, g_ref, t_ref, l_ref, dh_ref, dhb_ref, dg_ref):
        @pl.when(pl.program_id(0) == 0)
        def _():
            l_ref[...] = jnp.zeros_like(l_ref)
            dg_ref[...] = jnp.zeros_like(dg_ref)

        x = h_ref[...]
        err = x * _rms(x) * g_ref[...] - t_ref[...]
        l_ref[...] += 0.5 * jnp.sum(jnp.mean(err * err, axis=-1, keepdims=True), axis=0, keepdims=True)
        dx, dgain = _rmsnorm_bwd(x, g_ref[...], err * (1.0 / D))
        dh_ref[...] = dx
        dhb_ref[...] = dx.astype(BF16)
        dg_ref[...] += dgain

    return pl.pallas_call(
        body, name=name, grid=(T // tm,),
        in_specs=[pl.BlockSpec((tm, D), lambda i: (i, 0)), pl.BlockSpec((1, D), lambda i: (0, 0)),
                  pl.BlockSpec((tm, D), lambda i: (i, 0))],
        out_specs=[pl.BlockSpec((SUBLANES, LANES), lambda i: (0, 0)), pl.BlockSpec((tm, D), lambda i: (i, 0)),
                   pl.BlockSpec((tm, D), lambda i: (i, 0)), pl.BlockSpec((1, D), lambda i: (0, 0))],
        out_shape=[jax.ShapeDtypeStruct((SUBLANES, LANES), F32), jax.ShapeDtypeStruct((T, D), F32),
                   jax.ShapeDtypeStruct((T, D), BF16), jax.ShapeDtypeStruct((1, D), F32)],
        compiler_params=_params("arbitrary"))(h, gain, target)


def _halo_specs(tm, T, width, col, lead=None):
    r8 = tm // SUBLANES
    nb8 = T // SUBLANES
    if lead is None:
        return [pl.BlockSpec((tm, width), lambda i: (i, col)),
                pl.BlockSpec((SUBLANES, width), lambda i: (jnp.maximum(i * r8 - 1, 0), col)),
                pl.BlockSpec((SUBLANES, width), lambda i: (jnp.minimum((i + 1) * r8, nb8 - 1), col))]
    return [pl.BlockSpec((None, tm, width), lambda i: (lead, i, col)),
            pl.BlockSpec((None, SUBLANES, width), lambda i: (lead, jnp.maximum(i * r8 - 1, 0), col)),
            pl.BlockSpec((None, SUBLANES, width), lambda i: (lead, jnp.minimum((i + 1) * r8, nb8 - 1), col))]


def _ext(cur, prev, nxt, has_prev, has_next):
    return jnp.concatenate([jnp.where(has_prev, prev, 0.0), cur, jnp.where(has_next, nxt, 0.0)], axis=0)


def _shifted(ext, offset, tm):
    n = ext.shape[0]
    sh = (-offset) % n
    r = ext if sh == 0 else pltpu.roll(ext, sh, 0)
    return r[SUBLANES:SUBLANES + tm]


def _rg_conv_fwd(proj, cw8, cb, name):
    T = proj.shape[0]
    tm = min(512, T)
    nT = T // tm

    def body(cur_ref, prev_ref, next_ref, w_ref, b_ref, o_ref):
        i = pl.program_id(0)
        ext = _ext(cur_ref[...], prev_ref[...], next_ref[...], i > 0, i < nT - 1)
        acc = jnp.broadcast_to(b_ref[...], (tm, RG_W))
        for k in range(4):
            acc = acc + w_ref[k:k + 1, :] * _shifted(ext, k - 2, tm)
        o_ref[...] = acc

    return pl.pallas_call(
        body, name=name, grid=(nT,),
        in_specs=_halo_specs(tm, T, RG_W, 0) + [pl.BlockSpec((SUBLANES, RG_W), lambda i: (0, 0)),
                                                pl.BlockSpec((1, RG_W), lambda i: (0, 0))],
        out_specs=pl.BlockSpec((tm, RG_W), lambda i: (i, 0)),
        out_shape=jax.ShapeDtypeStruct((T, RG_W), F32),
        compiler_params=_params("parallel"))(proj, proj, proj, cw8, cb)


def _rg_conv_bwd(dxc, proj, cw8, name):
    T = proj.shape[0]
    tm = min(512, T)
    nT = T // tm

    def body(a0, p0, n0, a1, p1, n1, xa, xp, xn, w_ref, dxa_ref, dw_ref, db_ref):
        i = pl.program_id(0)

        @pl.when(i == 0)
        def _():
            dw_ref[...] = jnp.zeros_like(dw_ref)
            db_ref[...] = jnp.zeros_like(db_ref)

        has_p, has_n = i > 0, i < nT - 1
        cur = a0[...] + a1[...]
        dext = _ext(cur, p0[...] + p1[...], n0[...] + n1[...], has_p, has_n)
        xext = _ext(xa[...], xp[...], xn[...], has_p, has_n)
        acc = jnp.zeros((tm, RG_W), F32)
        rows = []
        for k in range(4):
            acc = acc + w_ref[k:k + 1, :] * _shifted(dext, 2 - k, tm)
            rows.append(jnp.sum(cur * _shifted(xext, k - 2, tm), axis=0, keepdims=True))
        dxa_ref[...] = acc
        dw_ref[...] += jnp.concatenate(rows + [jnp.zeros((4, RG_W), F32)], axis=0)
        db_ref[...] += jnp.sum(cur, axis=0, keepdims=True)

    return pl.pallas_call(
        body, name=name, grid=(nT,),
        in_specs=(_halo_specs(tm, T, RG_W, 0, lead=0) + _halo_specs(tm, T, RG_W, 0, lead=1)
                  + _halo_specs(tm, T, RG_W, 0) + [pl.BlockSpec((SUBLANES, RG_W), lambda i: (0, 0))]),
        out_specs=[pl.BlockSpec((tm, RG_W), lambda i: (i, 0)), pl.BlockSpec((SUBLANES, RG_W), lambda i: (0, 0)),
                   pl.BlockSpec((1, RG_W), lambda i: (0, 0))],
        out_shape=[jax.ShapeDtypeStruct((T, RG_W), F32), jax.ShapeDtypeStruct((SUBLANES, RG_W), F32),
                   jax.ShapeDtypeStruct((1, RG_W), F32)],
        compiler_params=_params("arbitrary"))(dxc, dxc, dxc, dxc, dxc, dxc, proj, proj, proj, cw8)


def _rg_scan_fwd(xc, wbd, bias, lam, name):
    T = xc.shape[0]
    tm = min(512, T)
    nT = T // tm

    def tile(d, i):
        return i + d * (nT - 1 - 2 * i)

    def body(xc_ref, w_ref, b_ref, lam_ref, h_ref, a_sc, u_sc, carry):
        d, i = pl.program_id(0), pl.program_id(1)

        @pl.when(i == 0)
        def _():
            carry[...] = jnp.zeros_like(carry)

        a, u = _rg_gates(xc_ref[...], w_ref[...], b_ref[...], lam_ref[...])
        a_sc[...] = a
        u_sc[...] = u

        def step(t, h):
            tt = t + d * (tm - 1 - 2 * t)
            h = a_sc[pl.ds(tt, 1), :] * h + u_sc[pl.ds(tt, 1), :]
            h_ref[pl.ds(tt, 1), :] = h
            return h

        carry[0:1, :] = lax.fori_loop(0, tm, step, carry[0:1, :])

    return pl.pallas_call(
        body, name=name, grid=(2, nT),
        in_specs=[pl.BlockSpec((tm, RG_W), lambda d, i: (tile(d, i), 0)),
                  pl.BlockSpec((None, RG_W, 2 * RG_W), lambda d, i: (d, 0, 0)),
                  pl.BlockSpec((None, 1, 2 * RG_W), lambda d, i: (d, 0, 0)),
                  pl.BlockSpec((None, 1, RG_W), lambda d, i: (d, 0, 0))],
        out_specs=pl.BlockSpec((None, tm, RG_W), lambda d, i: (d, tile(d, i), 0)),
        out_shape=jax.ShapeDtypeStruct((2, T, RG_W), F32),
        scratch_shapes=[pltpu.VMEM((tm, RG_W), F32), pltpu.VMEM((tm, RG_W), F32), pltpu.VMEM((SUBLANES, RG_W), F32)],
        compiler_params=_params("arbitrary", "arbitrary"))(xc, wbd, bias, lam)


def _rg_scan_bwd(xc, wbd, bias, lam, hs, dho, name):
    T = xc.shape[0]
    tm = min(512, T)
    nT = T // tm
    r8 = tm // SUBLANES
    nb8 = T // SUBLANES

    def tile(d, i):
        return (nT - 1 - i) + d * (2 * i - (nT - 1))

    def body(xc_ref, w_ref, b_ref, lam_ref, hc_ref, hp_ref, hn_ref, dho_ref,
             dxc_ref, dw_ref, db_ref, dlam_ref, a_sc, dt_sc, carry):
        d, i = pl.program_id(0), pl.program_id(1)
        ti = tile(d, i)

        @pl.when(i == 0)
        def _():
            carry[...] = jnp.zeros_like(carry)
            dw_ref[...] = jnp.zeros_like(dw_ref)
            db_ref[...] = jnp.zeros_like(db_ref)
            dlam_ref[...] = jnp.zeros_like(dlam_ref)

        (a, _), vjp = jax.vjp(_rg_gates, xc_ref[...], w_ref[...].astype(F32), b_ref[...], lam_ref[...])
        a_sc[...] = a

        def step(t, c):
            tt = (tm - 1 - t) + d * (2 * t - (tm - 1))
            dt = dho_ref[pl.ds(tt, 1), :] + c
            dt_sc[pl.ds(tt, 1), :] = dt
            return a_sc[pl.ds(tt, 1), :] * dt

        carry[0:1, :] = lax.fori_loop(0, tm, step, carry[0:1, :])
        dtot = dt_sc[...]
        ext = _ext(hc_ref[...], hp_ref[...], hn_ref[...], ti > 0, ti < nT - 1)
        hprev = jnp.where(d == 0, _shifted(ext, -1, tm), _shifted(ext, 1, tm))
        dxc, dw, db, dlam = vjp((dtot * hprev, dtot))
        dxc_ref[...] = dxc
        dw_ref[...] += dw
        db_ref[...] += db
        dlam_ref[...] += dlam

    return pl.pallas_call(
        body, name=name, grid=(2, nT),
        in_specs=[pl.BlockSpec((tm, RG_W), lambda d, i: (tile(d, i), 0)),
                  pl.BlockSpec((None, RG_W, 2 * RG_W), lambda d, i: (d, 0, 0)),
                  pl.BlockSpec((None, 1, 2 * RG_W), lambda d, i: (d, 0, 0)),
                  pl.BlockSpec((None, 1, RG_W), lambda d, i: (d, 0, 0)),
                  pl.BlockSpec((None, tm, RG_W), lambda d, i: (d, tile(d, i), 0)),
                  pl.BlockSpec((None, SUBLANES, RG_W), lambda d, i: (d, jnp.maximum(tile(d, i) * r8 - 1, 0), 0)),
                  pl.BlockSpec((None, SUBLANES, RG_W),
                               lambda d, i: (d, jnp.minimum((tile(d, i) + 1) * r8, nb8 - 1), 0)),
                  pl.BlockSpec((tm, RG_W), lambda d, i: (tile(d, i), 0))],
        out_specs=[pl.BlockSpec((None, tm, RG_W), lambda d, i: (d, tile(d, i), 0)),
                   pl.BlockSpec((None, RG_W, 2 * RG_W), lambda d, i: (d, 0, 0)),
                   pl.BlockSpec((None, 1, 2 * RG_W), lambda d, i: (d, 0, 0)),
                   pl.BlockSpec((None, 1, RG_W), lambda d, i: (d, 0, 0))],
        out_shape=[jax.ShapeDtypeStruct((2, T, RG_W), F32), jax.ShapeDtypeStruct((2, RG_W, 2 * RG_W), F32),
                   jax.ShapeDtypeStruct((2, 1, 2 * RG_W), F32), jax.ShapeDtypeStruct((2, 1, RG_W), F32)],
        scratch_shapes=[pltpu.VMEM((tm, RG_W), F32), pltpu.VMEM((tm, RG_W), F32), pltpu.VMEM((SUBLANES, RG_W), F32)],
        compiler_params=_params("arbitrary", "arbitrary"))(xc, wbd, bias, lam, hs, hs, hs, dho)


def _chunk_idx(d, c, n_chunks):
    return c + d * (n_chunks - 1 - 2 * c)


def _chunk_idx_rev(d, c, n_chunks):
    return (n_chunks - 1 - c) + d * (2 * c - (n_chunks - 1))


def _hg_fwd(proj, l0, l1, name):
    T = proj.shape[0]
    nC = T // CHUNK
    H, dk, dv = 4, 128, 128

    def body(q_ref, f_ref, v_ref, l0_ref, l1_ref, o_ref, sp_ref, st):
        d, c = pl.program_id(0), pl.program_id(1)

        @pl.when(c == 0)
        def _():
            st[...] = jnp.zeros_like(st)

        tri, tri_t, mref = _tri_consts(d)
        stp = tuple(st[h] for h in range(H))
        sp_ref[...] = st[...]
        o, stn = _hg_chunk(q_ref[...], f_ref[...], v_ref[...], l0_ref[...], l1_ref[...], stp, tri, tri_t, mref)
        o_ref[...] = o
        for h in range(H):
            st[h] = stn[h]

    row = lambda d, c: _chunk_idx(d, c, nC)
    return pl.pallas_call(
        body, name=name, grid=(2, nC),
        in_specs=[pl.BlockSpec((CHUNK, HG_W), lambda d, c: (row(d, c), 2)),
                  pl.BlockSpec((CHUNK, HG_W), lambda d, c: (row(d, c), 3 + d)),
                  pl.BlockSpec((CHUNK, HG_W), lambda d, c: (row(d, c), 5)),
                  pl.BlockSpec((1, HG_W), lambda d, c: (0, 0)), pl.BlockSpec((1, HG_W), lambda d, c: (0, 0))],
        out_specs=[pl.BlockSpec((None, CHUNK, H * dv), lambda d, c: (d, row(d, c), 0)),
                   pl.BlockSpec((None, None, H, dv, dk), lambda d, c: (d, row(d, c), 0, 0, 0))],
        out_shape=[jax.ShapeDtypeStruct((2, T, H * dv), F32), jax.ShapeDtypeStruct((2, nC, H, dv, dk), F32)],
        scratch_shapes=[pltpu.VMEM((H, dv, dk), F32)],
        compiler_params=_params("arbitrary", "arbitrary"))(proj, proj, proj, l0, l1)


def _hg_bwd(proj, l0, l1, sprev, do, name):
    T = proj.shape[0]
    nC = T // CHUNK
    H, dk, dv = 4, 128, 128

    def body(q_ref, f_ref, v_ref, l0_ref, l1_ref, sp_ref, do_ref, dq_ref, df_ref, dv_ref, dl0_ref, dl1_ref, dst):
        d, c = pl.program_id(0), pl.program_id(1)

        @pl.when(c == 0)
        def _():
            dst[...] = jnp.zeros_like(dst)
            dl0_ref[...] = jnp.zeros_like(dl0_ref)
            dl1_ref[...] = jnp.zeros_like(dl1_ref)

        tri, tri_t, mref = _tri_consts(d)
        fn = lambda q, f, v, a0, a1, stp: _hg_chunk(q, f, v, a0, a1, stp, tri, tri_t, mref)
        stp = tuple(sp_ref[h] for h in range(H))
        _, vjp = jax.vjp(fn, q_ref[...], f_ref[...], v_ref[...], l0_ref[...], l1_ref[...], stp)
        dq, df, dvv, dl0, dl1, dstp = vjp((do_ref[...], tuple(dst[h] for h in range(H))))
        dq_ref[...] = dq
        df_ref[...] = df
        dv_ref[...] = dvv
        dl0_ref[...] += dl0
        dl1_ref[...] += dl1
        for h in range(H):
            dst[h] = dstp[h]

    row = lambda d, c: _chunk_idx_rev(d, c, nC)
    tok = lambda: pl.BlockSpec((None, CHUNK, HG_W), lambda d, c: (d, row(d, c), 0))
    par = lambda: pl.BlockSpec((None, 1, HG_W), lambda d, c: (d, 0, 0))
    return pl.pallas_call(
        body, name=name, grid=(2, nC),
        in_specs=[pl.BlockSpec((CHUNK, HG_W), lambda d, c: (row(d, c), 2)),
                  pl.BlockSpec((CHUNK, HG_W), lambda d, c: (row(d, c), 3 + d)),
                  pl.BlockSpec((CHUNK, HG_W), lambda d, c: (row(d, c), 5)),
                  pl.BlockSpec((1, HG_W), lambda d, c: (0, 0)), pl.BlockSpec((1, HG_W), lambda d, c: (0, 0)),
                  pl.BlockSpec((None, None, H, dv, dk), lambda d, c: (d, row(d, c), 0, 0, 0)),
                  pl.BlockSpec((CHUNK, H * dv), lambda d, c: (row(d, c), 0))],
        out_specs=[tok(), tok(), tok(), par(), par()],
        out_shape=[jax.ShapeDtypeStruct((2, T, HG_W), F32)] * 3 + [jax.ShapeDtypeStruct((2, 1, HG_W), F32)] * 2,
        scratch_shapes=[pltpu.VMEM((H, dv, dk), F32)],
        compiler_params=_params("arbitrary", "arbitrary"))(proj, proj, proj, l0, l1, sprev, do)


def _gate_logits(proj, wup, bg, name):
    T = proj.shape[0]
    tm = min(512, T)

    def body(lr_ref, w_ref, b_ref, z_ref, lrb_ref):
        lr = lr_ref[...].astype(BF16)
        lrb_ref[...] = lr
        for d in range(2):
            z_ref[d] = _dg(lr, w_ref[d], 1, 0) + b_ref[d]

    return pl.pallas_call(
        body, name=name, grid=(T // tm,),
        in_specs=[pl.BlockSpec((tm, LANES), lambda i: (i, 24)), pl.BlockSpec((2, LANES, 512), lambda i: (0, 0, 0)),
                  pl.BlockSpec((2, 1, 512), lambda i: (0, 0, 0))],
        out_specs=[pl.BlockSpec((2, tm, 512), lambda i: (0, i, 0)), pl.BlockSpec((tm, LANES), lambda i: (i, 0))],
        out_shape=[jax.ShapeDtypeStruct((2, T, 512), F32), jax.ShapeDtypeStruct((T, LANES), BF16)],
        compiler_params=_params("parallel"))(proj, wup, bg)


def _gate_logits_bwd(dz, wup, name):
    T = dz.shape[1]
    tm = min(512, T)

    def body(dz_ref, w_ref, dlr_ref, db_ref, dzb_ref):
        @pl.when(pl.program_id(0) == 0)
        def _():
            db_ref[...] = jnp.zeros_like(db_ref)

        acc = jnp.zeros((tm, LANES), F32)
        for d in range(2):
            g = dz_ref[d]
            gb = g.astype(BF16)
            dzb_ref[d] = gb
            acc = acc + _dg(gb, w_ref[d], 1, 1)
            db_ref[d] += jnp.sum(g, axis=0, keepdims=True)
        dlr_ref[...] = acc

    return pl.pallas_call(
        body, name=name, grid=(T // tm,),
        in_specs=[pl.BlockSpec((2, tm, 512), lambda i: (0, i, 0)), pl.BlockSpec((2, LANES, 512), lambda i: (0, 0, 0))],
        out_specs=[pl.BlockSpec((tm, LANES), lambda i: (i, 0)), pl.BlockSpec((2, 1, 512), lambda i: (0, 0, 0)),
                   pl.BlockSpec((2, tm, 512), lambda i: (0, i, 0))],
        out_shape=[jax.ShapeDtypeStruct((T, LANES), F32), jax.ShapeDtypeStruct((2, 1, 512), F32),
                   jax.ShapeDtypeStruct((2, T, 512), BF16)],
        compiler_params=_params("arbitrary"))(dz, wup)


def _gla_fwd(proj, z, name):
    T = proj.shape[0]
    nC = T // CHUNK
    H, dk, dv = 4, 128, 256

    def body(q_ref, k_ref, v_ref, z_ref, o_ref, sp_ref, st):
        d, c = pl.program_id(0), pl.program_id(1)

        @pl.when(c == 0)
        def _():
            st[...] = jnp.zeros_like(st)

        tri, tri_t, mref = _tri_consts(d)
        stp = tuple(st[h] for h in range(H))
        sp_ref[...] = st[...]
        o, stn = _gla_chunk(q_ref[...], k_ref[...], v_ref[...], z_ref[...], stp, tri, tri_t, mref)
        o_ref[...] = o
        for h in range(H):
            st[h] = stn[h]

    row = lambda d, c: _chunk_idx(d, c, nC)
    return pl.pallas_call(
        body, name=name, grid=(2, nC),
        in_specs=[pl.BlockSpec((CHUNK, 512), lambda d, c: (row(d, c), 0)),
                  pl.BlockSpec((CHUNK, 512), lambda d, c: (row(d, c), 1)),
                  pl.BlockSpec((CHUNK, 1024), lambda d, c: (row(d, c), 1)),
                  pl.BlockSpec((None, CHUNK, 512), lambda d, c: (d, row(d, c), 0))],
        out_specs=[pl.BlockSpec((None, CHUNK, H * dv), lambda d, c: (d, row(d, c), 0)),
                   pl.BlockSpec((None, None, H, dv, dk), lambda d, c: (d, row(d, c), 0, 0, 0))],
        out_shape=[jax.ShapeDtypeStruct((2, T, H * dv), F32), jax.ShapeDtypeStruct((2, nC, H, dv, dk), F32)],
        scratch_shapes=[pltpu.VMEM((H, dv, dk), F32)],
        compiler_params=_params("arbitrary", "arbitrary"))(proj, proj, proj, z)


def _gla_bwd(proj, z, sprev, do, name):
    T = proj.shape[0]
    nC = T // CHUNK
    H, dk, dv = 4, 128, 256

    def body(q_ref, k_ref, v_ref, z_ref, sp_ref, do_ref, dq_ref, dk_ref, dv_ref, dz_ref, dst):
        d, c = pl.program_id(0), pl.program_id(1)

        @pl.when(c == 0)
        def _():
            dst[...] = jnp.zeros_like(dst)

        tri, tri_t, mref = _tri_consts(d)
        fn = lambda q, k, v, zz, stp: _gla_chunk(q, k, v, zz, stp, tri, tri_t, mref)
        stp = tuple(sp_ref[h] for h in range(H))
        _, vjp = jax.vjp(fn, q_ref[...], k_ref[...], v_ref[...], z_ref[...], stp)
        dq, dkk, dvv, dzz, dstp = vjp((do_ref[...], tuple(dst[h] for h in range(H))))
        dq_ref[...] = dq
        dk_ref[...] = dkk
        dv_ref[...] = dvv
        dz_ref[...] = dzz
        for h in range(H):
            dst[h] = dstp[h]

    row = lambda d, c: _chunk_idx_rev(d, c, nC)
    tok = lambda w: pl.BlockSpec((None, CHUNK, w), lambda d, c: (d, row(d, c), 0))
    return pl.pallas_call(
        body, name=name, grid=(2, nC),
        in_specs=[pl.BlockSpec((CHUNK, 512), lambda d, c: (row(d, c), 0)),
                  pl.BlockSpec((CHUNK, 512), lambda d, c: (row(d, c), 1)),
                  pl.BlockSpec((CHUNK, 1024), lambda d, c: (row(d, c), 1)),
                  tok(512),
                  pl.BlockSpec((None, None, H, dv, dk), lambda d, c: (d, row(d, c), 0, 0, 0)),
                  pl.BlockSpec((CHUNK, H * dv), lambda d, c: (row(d, c), 0))],
        out_specs=[tok(512), tok(512), tok(1024), tok(512)],
        out_shape=[jax.ShapeDtypeStruct((2, T, 512), F32), jax.ShapeDtypeStruct((2, T, 512), F32),
                   jax.ShapeDtypeStruct((2, T, 1024), F32), jax.ShapeDtypeStruct((2, T, 512), F32)],
        scratch_shapes=[pltpu.VMEM((H, dv, dk), F32)],
        compiler_params=_params("arbitrary", "arbitrary"))(proj, proj, proj, z, sprev, do)


def _l0_combine_fwd(hs, proj, o, gain, name):
    T = proj.shape[0]
    tm = min(512, T)

    def body(hf, hb, ga, of, ob, g, gn, out):
        out[...] = _l0_combine(hf[...], hb[...], ga[...], of[...], ob[...], g[...], gn[...]).astype(BF16)

    two = lambda lead: pl.BlockSpec((None, tm, 512), lambda i: (lead, i, 0))
    return pl.pallas_call(
        body, name=name, grid=(T // tm,),
        in_specs=[two(0), two(1), pl.BlockSpec((tm, 512), lambda i: (i, 1)), two(0), two(1),
                  pl.BlockSpec((tm, 512), lambda i: (i, 6)), pl.BlockSpec((1, 512), lambda i: (0, 0))],
        out_specs=pl.BlockSpec((tm, 1024), lambda i: (i, 0)),
        out_shape=jax.ShapeDtypeStruct((T, 1024), BF16),
        compiler_params=_params("parallel"))(hs, hs, proj, o, o, proj, gain)


def _l0_combine_bwd(hs, proj, o, gain, dmix, name):
    T = proj.shape[0]
    tm = min(512, T)

    def body(hf, hb, ga, of, ob, g, gn, dm, dho_ref, dga_ref, do_ref, dg_ref, dgn_ref):
        @pl.when(pl.program_id(0) == 0)
        def _():
            dgn_ref[...] = jnp.zeros_like(dgn_ref)

        _, vjp = jax.vjp(_l0_combine, hf[...], hb[...], ga[...], of[...], ob[...], g[...], gn[...])
        dhf, _, dga, dof, _, dg, dgn = vjp(dm[...])
        dho_ref[...] = dhf
        dga_ref[...] = dga
        do_ref[...] = dof
        dg_ref[...] = dg
        dgn_ref[...] += dgn

    two = lambda lead: pl.BlockSpec((None, tm, 512), lambda i: (lead, i, 0))
    tok = lambda: pl.BlockSpec((tm, 512), lambda i: (i, 0))
    return pl.pallas_call(
        body, name=name, grid=(T // tm,),
        in_specs=[two(0), two(1), pl.BlockSpec((tm, 512), lambda i: (i, 1)), two(0), two(1),
                  pl.BlockSpec((tm, 512), lambda i: (i, 6)), pl.BlockSpec((1, 512), lambda i: (0, 0)),
                  pl.BlockSpec((tm, 1024), lambda i: (i, 0))],
        out_specs=[tok(), tok(), tok(), tok(), pl.BlockSpec((1, 512), lambda i: (0, 0))],
        out_shape=[jax.ShapeDtypeStruct((T, 512), F32)] * 4 + [jax.ShapeDtypeStruct((1, 512), F32)],
        compiler_params=_params("arbitrary"))(hs, hs, proj, o, o, proj, gain, dmix)


def _l0_assemble(dxa, dga, dq, df, dv, dg, name):
    T = dxa.shape[0]
    tm = min(512, T)

    def body(xa, ga, q0, q1, f0, f1, v0, v1, g, out):
        out[...] = jnp.concatenate([xa[...], ga[...], q0[...] + q1[...], f0[...], f1[...], v0[...] + v1[...],
                                    g[...]], axis=1).astype(BF16)

    two = lambda lead: pl.BlockSpec((None, tm, 512), lambda i: (lead, i, 0))
    tok = lambda: pl.BlockSpec((tm, 512), lambda i: (i, 0))
    return pl.pallas_call(
        body, name=name, grid=(T // tm,),
        in_specs=[tok(), tok(), two(0), two(1), two(0), two(1), two(0), two(1), tok()],
        out_specs=pl.BlockSpec((tm, AB_IN), lambda i: (i, 0)),
        out_shape=jax.ShapeDtypeStruct((T, AB_IN), BF16),
        compiler_params=_params("parallel"))(dxa, dga, dq, dq, df, df, dv, dv, dg)


def _l1_combine_fwd(o, proj, gain, name):
    T = proj.shape[0]
    tm = min(512, T)

    def body(of, ob, r, gn, out):
        out[...] = _l1_combine(of[...], ob[...], r[...], gn[...]).astype(BF16)

    two = lambda lead: pl.BlockSpec((None, tm, 1024), lambda i: (lead, i, 0))
    return pl.pallas_call(
        body, name=name, grid=(T // tm,),
        in_specs=[two(0), two(1), pl.BlockSpec((tm, 1024), lambda i: (i, 2)), pl.BlockSpec((1, 1024), lambda i: (0, 0))],
        out_specs=pl.BlockSpec((tm, 1024), lambda i: (i, 0)),
        out_shape=jax.ShapeDtypeStruct((T, 1024), BF16),
        compiler_params=_params("parallel"))(o, o, proj, gain)


def _l1_combine_bwd(o, proj, gain, dmix, name):
    T = proj.shape[0]
    tm = min(512, T)

    def body(of, ob, r, gn, dm, do_ref, dr_ref, dgn_ref):
        @pl.when(pl.program_id(0) == 0)
        def _():
            dgn_ref[...] = jnp.zeros_like(dgn_ref)

        _, vjp = jax.vjp(_l1_combine, of[...], ob[...], r[...], gn[...])
        dof, _, dr, dgn = vjp(dm[...])
        do_ref[...] = dof
        dr_ref[...] = dr
        dgn_ref[...] += dgn

    two = lambda lead: pl.BlockSpec((None, tm, 1024), lambda i: (lead, i, 0))
    tok = lambda: pl.BlockSpec((tm, 1024), lambda i: (i, 0))
    return pl.pallas_call(
        body, name=name, grid=(T // tm,),
        in_specs=[two(0), two(1), pl.BlockSpec((tm, 1024), lambda i: (i, 2)),
                  pl.BlockSpec((1, 1024), lambda i: (0, 0)), tok()],
        out_specs=[tok(), tok(), pl.BlockSpec((1, 1024), lambda i: (0, 0))],
        out_shape=[jax.ShapeDtypeStruct((T, 1024), F32)] * 2 + [jax.ShapeDtypeStruct((1, 1024), F32)],
        compiler_params=_params("arbitrary"))(o, o, proj, gain, dmix)


def _l1_assemble(dq, dk, dv, dr, dlr, name):
    T = dr.shape[0]
    tm = min(512, T)

    def body(q0, q1, k0, k1, v0, v1, r, a, out):
        out[...] = jnp.concatenate([q0[...] + q1[...], k0[...] + k1[...], v0[...] + v1[...], r[...], a[...]],
                                   axis=1).astype(BF16)

    two = lambda lead, w: pl.BlockSpec((None, tm, w), lambda i: (lead, i, 0))
    return pl.pallas_call(
        body, name=name, grid=(T // tm,),
        in_specs=[two(0, 512), two(1, 512), two(0, 512), two(1, 512), two(0, 1024), two(1, 1024),
                  pl.BlockSpec((tm, 1024), lambda i: (i, 0)), pl.BlockSpec((tm, LANES), lambda i: (i, 0))],
        out_specs=pl.BlockSpec((tm, GLA_IN_PAD), lambda i: (i, 0)),
        out_shape=jax.ShapeDtypeStruct((T, GLA_IN_PAD), BF16),
        compiler_params=_params("parallel"))(dq, dq, dk, dk, dv, dv, dr, dlr)


HBM_SPEC = pl.BlockSpec(memory_space=pltpu.HBM)


def _place():
    x, y, c = lax.axis_index("x"), lax.axis_index("y"), lax.axis_index("c")
    return x, y, c


def _allgather_hbm(shards, name):
    n = len(shards)

    def body(*refs):
        ins, outs = refs[:n], refs[n:2 * n]
        send_sems, recv_sems, local_sems = refs[2 * n:]
        x, y, c = _place()
        me, sibling = (x, y, c), (x, y, 1 - c)
        chips = [(1 - x, y), (x, 1 - y), (1 - x, 1 - y)]

        def slot(a, p):
            return outs[a].at[4 * p[0] + 2 * p[1] + p[2]]

        def copy(a, k, block, to, src=None):
            return pltpu.make_async_remote_copy(
                src_ref=slot(a, block) if src is None else src, dst_ref=slot(a, block),
                send_sem=send_sems.at[a * 7 + k], recv_sem=recv_sems.at[a * 7 + k],
                device_id=to, device_id_type=MESH)

        mine = [pltpu.make_async_copy(ins[a], slot(a, me), local_sems.at[a]) for a in range(n)]
        for cp in mine:
            cp.start()
        first = []
        for a in range(n):
            first.append(copy(a, 0, me, sibling, src=ins[a]))
            first += [copy(a, 1 + j, me, (*chip, c), src=ins[a]) for j, chip in enumerate(chips)]
        for cp in first:
            cp.start()
        passed = []
        for j, chip in enumerate(chips):
            for a in range(n):
                copy(a, 1 + j, (*chip, c), me).wait_recv()
                cp = copy(a, 4 + j, (*chip, c), sibling)
                cp.start()
                passed.append(cp)
        for a in range(n):
            copy(a, 0, sibling, me).wait_recv()
            for j, chip in enumerate(chips):
                copy(a, 4 + j, (*chip, 1 - c), me).wait_recv()
        for cp in first + passed:
            cp.wait_send()
        for cp in mine:
            cp.wait()

    return pl.pallas_call(
        body, name=name,
        in_specs=[HBM_SPEC] * n, out_specs=[HBM_SPEC] * n,
        out_shape=[jax.ShapeDtypeStruct((N_DEV,) + s.shape, s.dtype) for s in shards],
        scratch_shapes=[pltpu.SemaphoreType.DMA((7 * n,)), pltpu.SemaphoreType.DMA((7 * n,)),
                        pltpu.SemaphoreType.DMA((n,))],
        compiler_params=pltpu.CompilerParams(has_side_effects=True))(*shards)


def _allgather_vmem(x_shard, name, reduce=False):
    m_per, n = x_shard.shape

    def body(x_ref, out_ref, *rest):
        if reduce:
            sum_ref, send_sems, recv_sems, local_sem = rest
        else:
            send_sems, recv_sems, local_sem = rest
        x, y, c = _place()
        me, sibling = (x, y, c), (x, y, 1 - c)
        chips = [(1 - x, y), (x, 1 - y), (1 - x, 1 - y)]

        def rows(px, py, pc):
            return out_ref.at[pl.ds((4 * px + 2 * py + pc) * m_per, m_per), :]

        def copy(k, block, to, src=None):
            return pltpu.make_async_remote_copy(
                src_ref=rows(*block) if src is None else src, dst_ref=rows(*block),
                send_sem=send_sems.at[k], recv_sem=recv_sems.at[k], device_id=to, device_id_type=MESH)

        mine = pltpu.make_async_copy(x_ref, rows(*me), local_sem)
        mine.start()
        first = [copy(0, me, sibling, src=x_ref)]
        first += [copy(1 + j, me, (*chip, c), src=x_ref) for j, chip in enumerate(chips)]
        for cp in first:
            cp.start()
        passed = [copy(4 + j, (*chip, c), sibling) for j, chip in enumerate(chips)]
        for j, chip in enumerate(chips):
            copy(1 + j, (*chip, c), me).wait_recv()
            passed[j].start()
        copy(0, sibling, me).wait_recv()
        for j, chip in enumerate(chips):
            copy(4 + j, (*chip, 1 - c), me).wait_recv()
        for cp in first + passed:
            cp.wait_send()
        mine.wait()
        if reduce:
            acc = out_ref[pl.ds(0, m_per), :]
            for j in range(1, N_DEV):
                acc = acc + out_ref[pl.ds(j * m_per, m_per), :]
            sum_ref[...] = acc

    vm = pl.BlockSpec(memory_space=pltpu.VMEM)
    out_shape = [jax.ShapeDtypeStruct((N_DEV * m_per, n), x_shard.dtype)]
    if reduce:
        out_shape.append(jax.ShapeDtypeStruct((m_per, n), x_shard.dtype))
    res = pl.pallas_call(
        body, name=name, in_specs=[vm], out_specs=[vm] * len(out_shape), out_shape=out_shape,
        scratch_shapes=[pltpu.SemaphoreType.DMA((7,)), pltpu.SemaphoreType.DMA((7,)), pltpu.SemaphoreType.DMA],
        compiler_params=pltpu.CompilerParams(has_side_effects=True, vmem_limit_bytes=VMEM_LIMIT))(x_shard)
    return res[1] if reduce else res[0]


def _exchange_sibling(grads, name):
    n = len(grads)

    def body(*refs):
        ins, outs = refs[:n], refs[n:2 * n]
        send_sems, recv_sems = refs[2 * n:]
        x, y, c = _place()
        sibling = (x, y, 1 - c)
        copies = []
        for a in range(n):
            for q in range(4):
                copies.append(pltpu.make_async_remote_copy(
                    src_ref=ins[a].at[2 * q + (1 - c)], dst_ref=outs[a].at[q],
                    send_sem=send_sems.at[a * 4 + q], recv_sem=recv_sems.at[a * 4 + q],
                    device_id=sibling, device_id_type=MESH))
        for cp in copies:
            cp.start()
        for cp in copies:
            cp.wait()

    return pl.pallas_call(
        body, name=name, in_specs=[HBM_SPEC] * n, out_specs=[HBM_SPEC] * n,
        out_shape=[jax.ShapeDtypeStruct((4,) + g.shape[1:], g.dtype) for g in grads],
        scratch_shapes=[pltpu.SemaphoreType.DMA((4 * n,)), pltpu.SemaphoreType.DMA((4 * n,))],
        compiler_params=pltpu.CompilerParams(has_side_effects=True))(*grads)


def _exchange_chips(parts, name):
    n = len(parts)

    def body(*refs):
        ins, outs = refs[:n], refs[n:2 * n]
        send_sems, recv_sems = refs[2 * n:]
        x, y, c = _place()
        chips = [(1 - x, y), (x, 1 - y), (1 - x, 1 - y)]
        copies = []
        for a in range(n):
            for k, chip in enumerate(chips):
                copies.append(pltpu.make_async_remote_copy(
                    src_ref=ins[a].at[2 * chip[0] + chip[1]], dst_ref=outs[a].at[k],
                    send_sem=send_sems.at[a * 3 + k], recv_sem=recv_sems.at[a * 3 + k],
                    device_id=(*chip, c), device_id_type=MESH))
        for cp in copies:
            cp.start()
        for cp in copies:
            cp.wait()

    return pl.pallas_call(
        body, name=name, in_specs=[HBM_SPEC] * n, out_specs=[HBM_SPEC] * n,
        out_shape=[jax.ShapeDtypeStruct((3,) + p.shape[1:], p.dtype) for p in parts],
        scratch_shapes=[pltpu.SemaphoreType.DMA((3 * n,)), pltpu.SemaphoreType.DMA((3 * n,))],
        compiler_params=pltpu.CompilerParams(has_side_effects=True))(*parts)


def _chip_partial(g, r1, place, name):
    _, R, C = g.shape
    tr = min(256, R)
    assert R % tr == 0

    def body(pl_ref, g_ref, r_ref, pb_ref, pm_ref):
        q = pl.program_id(1)
        s = g_ref[...] + r_ref[...]
        pb_ref[...] = s.astype(BF16)

        @pl.when(q == pl_ref[1])
        def _():
            pm_ref[...] = s

    grid_spec = pltpu.PrefetchScalarGridSpec(
        num_scalar_prefetch=1, grid=(R // tr, 4),
        in_specs=[pl.BlockSpec((None, tr, C), lambda r, q, p: (2 * q + p[0], r, 0)),
                  pl.BlockSpec((None, tr, C), lambda r, q, p: (q, r, 0))],
        out_specs=[pl.BlockSpec((None, tr, C), lambda r, q, p: (q, r, 0)),
                   pl.BlockSpec((tr, C), lambda r, q, p: (r, 0))])
    return pl.pallas_call(
        body, name=name, grid_spec=grid_spec,
        out_shape=[jax.ShapeDtypeStruct((4, R, C), BF16), jax.ShapeDtypeStruct((R, C), F32)],
        compiler_params=_params("parallel", "arbitrary"))(place, g, r1)


def _adamw(w, gparts, m, v, name):
    R, C = w.shape
    tr = min(256, R)
    assert R % tr == 0
    g0, g3 = gparts

    def body(w_ref, g0_ref, *rest):
        if g3 is not None:
            g3_ref, m_ref, v_ref, go, do, mo, vo = rest
        else:
            m_ref, v_ref, go, do, mo, vo = rest
        g = g0_ref[...]
        if g3 is not None:
            for k in range(3):
                g = g + g3_ref[k].astype(F32)
        wv = w_ref[...]
        mn = ADAM_B1 * m_ref[...] + (1.0 - ADAM_B1) * g
        vn = ADAM_B2 * v_ref[...] + (1.0 - ADAM_B2) * jnp.square(g)
        m_hat = mn / (1.0 - ADAM_B1 ** ADAM_STEP)
        v_hat = vn / (1.0 - ADAM_B2 ** ADAM_STEP)
        go[...] = g
        do[...] = -ADAM_LR * (m_hat / (jnp.sqrt(v_hat) + ADAM_EPS) + ADAM_WD * wv)
        mo[...] = mn
        vo[...] = vn

    blk = pl.BlockSpec((tr, C), lambda i: (i, 0))
    in_specs = [blk, blk] + ([pl.BlockSpec((3, tr, C), lambda i: (0, i, 0))] if g3 is not None else []) + [blk, blk]
    args = [w, g0] + ([g3] if g3 is not None else []) + [m, v]
    return pl.pallas_call(
        body, name=name, grid=(R // tr,), in_specs=in_specs, out_specs=[blk] * 4,
        out_shape=[jax.ShapeDtypeStruct((R, C), F32)] * 4,
        compiler_params=_params("parallel"))(*args)


SMALL_SHARDED = ("rg_conv_w", "rg_b_a", "rg_b_x", "rg_lambda", "gla_w_gate_up", "gla_b_gate", "gla_norm")
SMALL_REPLICATED = ("norm_mix", "norm_mlp", "norm_final", "rg_conv_b", "rg_w_a", "rg_w_x", "hg_lb_logits", "hg_norm")
WEIGHT_NAMES = ("norm_mix", "norm_mlp", "norm_final", "mlp_w1", "mlp_w2", "ab_w_in", "ab_w_out", "rg_conv_w",
                "rg_conv_b", "rg_w_a", "rg_b_a", "rg_w_x", "rg_b_x", "rg_lambda", "hg_lb_logits", "hg_norm",
                "gla_w_in", "gla_w_out", "gla_w_gate_up", "gla_b_gate", "gla_norm")


def _rows128(a):
    return a.reshape(-1, LANES)


def _part_rows(a):
    return -(-(a.size // LANES) // SUBLANES) * SUBLANES


def _pack_rows(arrays, pad_to=SUBLANES):
    parts = [jnp.pad(_rows128(a), ((0, _part_rows(a) - a.size // LANES), (0, 0))) for a in arrays]
    total = sum(p.shape[0] for p in parts)
    extra = (-total) % pad_to
    if extra:
        parts.append(jnp.zeros((extra, LANES), parts[0].dtype))
    return jnp.concatenate(parts, axis=0)


def _unshard_last(g, shape_local):
    nd = len(shape_local)
    t = g.reshape((N_DEV,) + tuple(shape_local))
    t = jnp.moveaxis(t, 0, nd - 1)
    return t.reshape(tuple(shape_local[:-1]) + (N_DEV * shape_local[-1],))


def _block_diag(w):
    eye = jnp.eye(8, dtype=w.dtype)
    return (w[:, :, :, None, :] * eye[None, :, None, :, None]).reshape(2, RG_W, RG_W)


def _block_diag_extract(dw):
    t = dw.reshape(2, 8, 64, 8, 64)
    return jnp.moveaxis(jnp.diagonal(t, axis1=1, axis2=3), -1, 1)


def kernel(x, norm_mix, norm_mlp, norm_final, mlp_w1, mlp_w2, ab_w_in, ab_w_out, rg_conv_w, rg_conv_b, rg_w_a, rg_b_a, rg_w_x, rg_b_x, rg_lambda, hg_lb_logits, hg_norm, gla_w_in, gla_w_out, gla_w_gate_up, gla_b_gate, gla_norm, loss_target, m_norm_mix, m_norm_mlp, m_norm_final, m_mlp_w1, m_mlp_w2, m_ab_w_in, m_ab_w_out, m_rg_conv_w, m_rg_conv_b, m_rg_w_a, m_rg_b_a, m_rg_w_x, m_rg_b_x, m_rg_lambda, m_hg_lb_logits, m_hg_norm, m_gla_w_in, m_gla_w_out, m_gla_w_gate_up, m_gla_b_gate, m_gla_norm, v_norm_mix, v_norm_mlp, v_norm_final, v_mlp_w1, v_mlp_w2, v_ab_w_in, v_ab_w_out, v_rg_conv_w, v_rg_conv_b, v_rg_w_a, v_rg_b_a, v_rg_w_x, v_rg_b_x, v_rg_lambda, v_hg_lb_logits, v_hg_norm, v_gla_w_in, v_gla_w_out, v_gla_w_gate_up, v_gla_b_gate, v_gla_norm):
    w_loc = dict(norm_mix=norm_mix, norm_mlp=norm_mlp, norm_final=norm_final, mlp_w1=mlp_w1, mlp_w2=mlp_w2,
                 ab_w_in=ab_w_in, ab_w_out=ab_w_out, rg_conv_w=rg_conv_w, rg_conv_b=rg_conv_b, rg_w_a=rg_w_a,
                 rg_b_a=rg_b_a, rg_w_x=rg_w_x, rg_b_x=rg_b_x, rg_lambda=rg_lambda, hg_lb_logits=hg_lb_logits,
                 hg_norm=hg_norm, gla_w_in=gla_w_in, gla_w_out=gla_w_out, gla_w_gate_up=gla_w_gate_up,
                 gla_b_gate=gla_b_gate, gla_norm=gla_norm)
    m_loc = dict(norm_mix=m_norm_mix, norm_mlp=m_norm_mlp, norm_final=m_norm_final, mlp_w1=m_mlp_w1,
                 mlp_w2=m_mlp_w2, ab_w_in=m_ab_w_in, ab_w_out=m_ab_w_out, rg_conv_w=m_rg_conv_w,
                 rg_conv_b=m_rg_conv_b, rg_w_a=m_rg_w_a, rg_b_a=m_rg_b_a, rg_w_x=m_rg_w_x, rg_b_x=m_rg_b_x,
                 rg_lambda=m_rg_lambda, hg_lb_logits=m_hg_lb_logits, hg_norm=m_hg_norm, gla_w_in=m_gla_w_in,
                 gla_w_out=m_gla_w_out, gla_w_gate_up=m_gla_w_gate_up, gla_b_gate=m_gla_b_gate,
                 gla_norm=m_gla_norm)
    v_loc = dict(norm_mix=v_norm_mix, norm_mlp=v_norm_mlp, norm_final=v_norm_final, mlp_w1=v_mlp_w1,
                 mlp_w2=v_mlp_w2, ab_w_in=v_ab_w_in, ab_w_out=v_ab_w_out, rg_conv_w=v_rg_conv_w,
                 rg_conv_b=v_rg_conv_b, rg_w_a=v_rg_w_a, rg_b_a=v_rg_b_a, rg_w_x=v_rg_w_x, rg_b_x=v_rg_b_x,
                 rg_lambda=v_rg_lambda, hg_lb_logits=v_hg_lb_logits, hg_norm=v_hg_norm, gla_w_in=v_gla_w_in,
                 gla_w_out=v_gla_w_out, gla_w_gate_up=v_gla_w_gate_up, gla_b_gate=v_gla_b_gate,
                 gla_norm=v_gla_norm)

    T = x.shape[1]
    h0 = x.reshape(T, D_MODEL)
    target = loss_target.reshape(T, D_MODEL)
    ax, ay, ac = lax.axis_index("x"), lax.axis_index("y"), lax.axis_index("c")
    dev = 4 * ax + 2 * ay + ac
    place = jnp.stack([ac, 2 * ax + ay]).astype(jnp.int32)

    big_shards = [mlp_w1[0].astype(BF16), mlp_w1[1].astype(BF16), mlp_w2[0].astype(BF16), mlp_w2[1].astype(BF16),
                  ab_w_in[0].astype(BF16), ab_w_out[0].astype(BF16), gla_w_in[0].astype(BF16),
                  gla_w_out[0].astype(BF16)]
    w1g0, w1g1, w2g0, w2g1, abin_g, about_g, glain_g, glaout_g = _allgather_hbm(big_shards, "ag_weights")
    w1g = (w1g0, w1g1)
    w2f = (w2g0.reshape(D_FF, D_MODEL), w2g1.reshape(D_FF, D_MODEL))
    wab_in = jnp.transpose(abin_g, (1, 0, 2)).reshape(D_MODEL, AB_IN)
    wab_out = about_g.reshape(D_MODEL, D_MODEL)
    wgla_in = jnp.pad(jnp.transpose(glain_g, (1, 0, 2)).reshape(D_MODEL, GLA_IN), ((0, 0), (0, GLA_IN_PAD - GLA_IN)))
    wgla_out = glaout_g.reshape(D_MODEL, D_MODEL)

    small_local = [w_loc[n] for n in SMALL_SHARDED]
    small_g = _allgather_vmem(_pack_rows(small_local, 8), "ag_small")
    small_g = small_g.reshape(N_DEV, -1, LANES)
    full = {}
    off = 0
    for n, a in zip(SMALL_SHARDED, small_local):
        full[n] = _unshard_last(small_g[:, off:off + a.size // LANES].reshape(N_DEV, a.size), a.shape)
        off += _part_rows(a)
    conv_w = full["rg_conv_w"][0]
    b_a, b_x, lam = full["rg_b_a"][0], full["rg_b_x"][0], full["rg_lambda"][0]
    w_up, b_gate, g_norm = full["gla_w_gate_up"][0], full["gla_b_gate"][0], full["gla_norm"]

    cw8 = jnp.pad(conv_w, ((0, 4), (0, 0)))
    wbd = jnp.concatenate([_block_diag(rg_w_a[0]), _block_diag(rg_w_x[0])], axis=2).astype(BF16)
    rg_bias = jnp.concatenate([b_a, b_x], axis=1).reshape(2, 1, 2 * RG_W)
    lam3 = lam.reshape(2, 1, RG_W)
    l0, l1 = hg_lb_logits[0:1], hg_lb_logits[1:2]
    wup_pad = jnp.zeros((2, LANES, 512), F32).at[0, 0:16].set(w_up[0]).at[1, 16:32].set(w_up[1])
    bg3 = b_gate.reshape(2, 1, 512)
    nmix0, nmix1 = norm_mix[0:1], norm_mix[1:2]
    nmlp0, nmlp1 = norm_mlp[0:1], norm_mlp[1:2]
    nfin = norm_final.reshape(1, D_MODEL)

    proj0, y0 = _norm_matmul(h0, nmix0, wab_in, "l0_in_proj")
    xc = _rg_conv_fwd(proj0, cw8, rg_conv_b, "rg_conv")
    hs = _rg_scan_fwd(xc, wbd, rg_bias, lam3, "rg_scan")
    o_hg, s_hg = _hg_fwd(proj0, l0, l1, "hg_chunks")
    mixin0 = _l0_combine_fwd(hs, proj0, o_hg, hg_norm, "l0_combine")
    h1 = _matmul_res(mixin0, wab_out, h0, "l0_out_proj")
    h2, pre0, ym0 = _mlp_fwd(h1, nmlp0, w1g[0], w2f[0], "mlp0")
    proj1, y1 = _norm_matmul(h2, nmix1, wgla_in, "l1_in_proj")
    z_gate, lr_b = _gate_logits(proj1, wup_pad, bg3, "gla_gate_logits")
    o_gla, s_gla = _gla_fwd(proj1, z_gate, "gla_chunks")
    mixin1 = _l1_combine_fwd(o_gla, proj1, g_norm, "l1_combine")
    h3 = _matmul_res(mixin1, wgla_out, h2, "l1_out_proj")
    h4, pre1, ym1 = _mlp_fwd(h3, nmlp1, w1g[1], w2f[1], "mlp1")
    loss_blk, dh4, dh4b, d_nfin = _final_loss(h4, nfin, target, "final_loss")
    loss = lax.psum(loss_blk[0, 0], ("x", "y", "c"))

    dh3, dh3b, dpre1, act1, d_nmlp1 = _mlp_bwd(dh4, h3, nmlp1, pre1, w1g[1], w2f[1], "mlp1_bwd")
    g_w1_1 = _wgrad(ym1, dpre1, 512, "mlp1_dw1", sharded_cols=True)
    g_w2_1 = _wgrad(act1, dh4b, 1024, "mlp1_dw2")
    dmixin1 = _dgrad(dh3b, wgla_out, "l1_out_dgrad")
    g_gla_out = _wgrad(mixin1, dh3b, 1024, "l1_out_dw")
    do_gla, dr, d_gnorm = _l1_combine_bwd(o_gla, proj1, g_norm, dmixin1, "l1_combine_bwd")
    dq1, dk1, dv1, dz_gate = _gla_bwd(proj1, z_gate, s_gla, do_gla, "gla_chunks_bwd")
    dlr1, d_bg, dz_b = _gate_logits_bwd(dz_gate, wup_pad, "gla_gate_logits_bwd")
    d_wup = [_wgrad(lr_b, dz_b[d], 512, "gla_gate_dw%d" % d) for d in range(2)]
    dproj1 = _l1_assemble(dq1, dk1, dv1, dr, dlr1, "l1_assemble")
    dh2, dh2b, d_nmix1 = _dgrad_norm(dproj1, wgla_in, h2, nmix1, dh3, "l1_in_dgrad")
    g_gla_in = _wgrad(y1, dproj1, 640, "l1_in_dw")

    dh1, dh1b, dpre0, act0, d_nmlp0 = _mlp_bwd(dh2, h1, nmlp0, pre0, w1g[0], w2f[0], "mlp0_bwd")
    g_w1_0 = _wgrad(ym0, dpre0, 512, "mlp0_dw1", sharded_cols=True)
    g_w2_0 = _wgrad(act0, dh2b, 1024, "mlp0_dw2")
    dmixin0 = _dgrad(dh1b, wab_out, "l0_out_dgrad")
    g_ab_out = _wgrad(mixin0, dh1b, 1024, "l0_out_dw")
    dho, dga, do_hg, dg_gate, d_hgnorm = _l0_combine_bwd(hs, proj0, o_hg, hg_norm, dmixin0, "l0_combine_bwd")
    dxc, d_wbd, d_rgb, d_lam = _rg_scan_bwd(xc, wbd, rg_bias, lam3, hs, dho, "rg_scan_bwd")
    dxa, d_cw8, d_cb = _rg_conv_bwd(dxc, proj0, cw8, "rg_conv_bwd")
    dq0, df0, dv0, d_l0, d_l1 = _hg_bwd(proj0, l0, l1, s_hg, do_hg, "hg_chunks_bwd")
    dproj0 = _l0_assemble(dxa, dga, dq0, df0, dv0, dg_gate, "l0_assemble")
    dx, _, d_nmix0 = _dgrad_norm(dproj0, wab_in, h0, nmix0, dh1, "l0_in_dgrad")
    g_ab_in = _wgrad(y0, dproj0, 512, "l0_in_dw")

    big_grads = [g_w1_0, g_w1_1, g_w2_0.reshape(N_DEV, 512, D_MODEL), g_w2_1.reshape(N_DEV, 512, D_MODEL),
                 jnp.transpose(g_ab_in.reshape(D_MODEL, N_DEV, AB_IN // N_DEV), (1, 0, 2)),
                 g_ab_out.reshape(N_DEV, 128, D_MODEL),
                 jnp.transpose(g_gla_in[:, :GLA_IN].reshape(D_MODEL, N_DEV, GLA_IN // N_DEV), (1, 0, 2)),
                 g_gla_out.reshape(N_DEV, 128, D_MODEL)]
    from_sibling = _exchange_sibling(big_grads, "rs_sibling")
    partials = [_chip_partial(g, r, place, "rs_partial_%d" % a) for a, (g, r) in enumerate(zip(big_grads, from_sibling))]
    from_chips = _exchange_chips([p[0] for p in partials], "rs_chips")
    big_w = [mlp_w1[0], mlp_w1[1], mlp_w2[0], mlp_w2[1], ab_w_in[0], ab_w_out[0], gla_w_in[0], gla_w_out[0]]
    big_m = [m_mlp_w1[0], m_mlp_w1[1], m_mlp_w2[0], m_mlp_w2[1], m_ab_w_in[0], m_ab_w_out[0], m_gla_w_in[0], m_gla_w_out[0]]
    big_v = [v_mlp_w1[0], v_mlp_w1[1], v_mlp_w2[0], v_mlp_w2[1], v_ab_w_in[0], v_ab_w_out[0], v_gla_w_in[0], v_gla_w_out[0]]
    big_res = [_adamw(w, (p[1], r), m, v, "adamw_big_%d" % a)
               for a, (w, p, r, m, v) in enumerate(zip(big_w, partials, from_chips, big_m, big_v))]

    def stacked(i, j):
        return tuple(jnp.stack([big_res[i][k], big_res[j][k]]) for k in range(4))

    res = {"mlp_w1": stacked(0, 1), "mlp_w2": stacked(2, 3),
           "ab_w_in": tuple(big_res[4][k][None] for k in range(4)),
           "ab_w_out": tuple(big_res[5][k][None] for k in range(4)),
           "gla_w_in": tuple(big_res[6][k][None] for k in range(4)),
           "gla_w_out": tuple(big_res[7][k][None] for k in range(4))}

    d_wa = _block_diag_extract(d_wbd[:, :, :RG_W])[None]
    d_wx = _block_diag_extract(d_wbd[:, :, RG_W:])[None]
    small_full = {
        "norm_mix": jnp.concatenate([d_nmix0, d_nmix1], axis=0), "norm_mlp": jnp.concatenate([d_nmlp0, d_nmlp1], axis=0),
        "norm_final": d_nfin.reshape(D_MODEL), "rg_conv_b": d_cb, "rg_w_a": d_wa, "rg_w_x": d_wx,
        "hg_lb_logits": jnp.concatenate([d_l0[0] + d_l0[1], d_l1[0] + d_l1[1]], axis=0), "hg_norm": d_hgnorm,
        "rg_conv_w": d_cw8[0:4][None], "rg_b_a": d_rgb[:, 0, :RG_W][None], "rg_b_x": d_rgb[:, 0, RG_W:][None],
        "rg_lambda": d_lam[:, 0, :][None],
        "gla_w_gate_up": jnp.stack([d_wup[0][0:16], d_wup[1][16:32]])[None], "gla_b_gate": d_bg[:, 0, :][None],
        "gla_norm": d_gnorm}
    small_names = SMALL_REPLICATED + SMALL_SHARDED
    packed = _pack_rows([small_full[n] for n in small_names], 8)
    summed = _allgather_vmem(packed, "ar_small", reduce=True)
    g_small = {}
    off = 0
    for n in small_names:
        a = small_full[n]
        gfull = summed[off:off + a.size // LANES].reshape(a.shape)
        off += _part_rows(a)
        if n in SMALL_SHARDED:
            loc = w_loc[n].shape[-1]
            gfull = lax.dynamic_slice_in_dim(gfull, dev * loc, loc, axis=gfull.ndim - 1)
        g_small[n] = gfull
    sw = _pack_rows([w_loc[n] for n in small_names], 256)
    sg = _pack_rows([g_small[n] for n in small_names], 256)
    sm = _pack_rows([m_loc[n] for n in small_names], 256)
    sv = _pack_rows([v_loc[n] for n in small_names], 256)
    small_res = _adamw(sw, (sg, None), sm, sv, "adamw_small")
    off = 0
    for n in small_names:
        a = w_loc[n]
        nr = a.size // LANES
        res[n] = tuple(small_res[k][off:off + nr].reshape(a.shape) for k in range(4))
        off += _part_rows(a)

    grad_x = dx.reshape(1, T, D_MODEL)
    out = [loss, grad_x]
    for k in range(4):
        out += [res[n][k] for n in WEIGHT_NAMES]
    return tuple(out)
```

```python
import jax
import jax.numpy as jnp
from jax import lax
from jax.experimental import pallas as pl
from jax.experimental.pallas import tpu as pltpu

F32, BF16 = jnp.float32, jnp.bfloat16
HI = lax.Precision.HIGHEST
MESH = pl.DeviceIdType.MESH

D_MODEL = 1024
D_FF = 4096
RG_W = 512
HG_W = 512
CHUNK = 64
EPS = 1e-6
RG_C = 8.0
AB_IN = 3584
GLA_IN = 3104
GLA_IN_PAD = 3200
N_DEV = 8
LANES = 128
SUBLANES = 8
VMEM_LIMIT = 48 * 1024 * 1024

ADAM_LR, ADAM_B1, ADAM_B2, ADAM_EPS, ADAM_WD, ADAM_STEP = 0.001, 0.9, 0.999, 1e-08, 0.01, 10


def _params(*sem):
    return pltpu.CompilerParams(dimension_semantics=sem, vmem_limit_bytes=VMEM_LIMIT)


def _dg(a, b, ca, cb):
    return lax.dot_general(a.astype(BF16), b.astype(BF16), (((ca,), (cb,)), ((), ())),
                           preferred_element_type=F32)


@jax.custom_vjp
def _mm_nn(a, b):
    return _dg(a, b, 1, 0)


_mm_nn.defvjp(lambda a, b: (_dg(a, b, 1, 0), (a, b)),
              lambda res, g: (_dg(g, res[1], 1, 1), _dg(res[0], g, 0, 0)))


@jax.custom_vjp
def _mm_nt(a, b):
    return _dg(a, b, 1, 1)


_mm_nt.defvjp(lambda a, b: (_dg(a, b, 1, 1), (a, b)),
              lambda res, g: (_dg(g, res[1], 1, 0), _dg(g, res[0], 0, 0)))


@jax.custom_vjp
def _mm_tn(a, b):
    return _dg(a, b, 0, 0)


_mm_tn.defvjp(lambda a, b: (_dg(a, b, 0, 0), (a, b)),
              lambda res, g: (_dg(res[1], g, 1, 1), _dg(res[0], g, 1, 0)))


@jax.custom_vjp
def _cum(tri, tri_t, x):
    return jnp.dot(tri, x, precision=HI, preferred_element_type=F32)


_cum.defvjp(lambda tri, tri_t, x: (jnp.dot(tri, x, precision=HI, preferred_element_type=F32), (tri, tri_t)),
            lambda res, g: (jnp.zeros_like(res[0]), jnp.zeros_like(res[1]),
                            jnp.dot(res[1], g, precision=HI, preferred_element_type=F32)))


def _sig(x):
    return 1.0 / (1.0 + jnp.exp(-x))


def _gelu(x):
    return 0.5 * x * (1.0 + jnp.tanh(0.7978845608028654 * (x + 0.044715 * (x * x * x))))


def _softplus(z):
    return jnp.maximum(z, 0.0) + jnp.log(1.0 + jnp.exp(-jnp.abs(z)))


def _rms(x):
    return lax.rsqrt(jnp.mean(x * x, axis=-1, keepdims=True) + EPS)


def _rmsnorm_bwd(x, gain, dy):
    r = _rms(x)
    xh = x * r
    dgain = jnp.sum(dy * xh, axis=0, keepdims=True)
    dxh = dy * gain
    dx = r * (dxh - xh * jnp.mean(dxh * xh, axis=-1, keepdims=True))
    return dx, dgain


def _headnorm(o, gain, n_heads, hd):
    parts = []
    for h in range(n_heads):
        oh = o[:, h * hd:(h + 1) * hd]
        parts.append(oh * _rms(oh))
    return jnp.concatenate(parts, axis=1) * gain


def _tri_consts(d):
    row = lax.broadcasted_iota(jnp.int32, (CHUNK, CHUNK), 0)
    col = lax.broadcasted_iota(jnp.int32, (CHUNK, CHUNK), 1)
    ge = (row >= col).astype(F32)
    le = (row <= col).astype(F32)
    tri = jnp.where(d == 0, ge, le)
    tri_t = jnp.where(d == 0, le, ge)
    r1 = lax.broadcasted_iota(jnp.int32, (CHUNK, 1), 0)
    mref = jnp.where(d == 0, (r1 <= CHUNK // 2).astype(F32), (r1 >= CHUNK // 2 - 1).astype(F32))
    return tri, tri_t, mref


def _chunk_core(qh, k, v, logf, st_prev, tri, tri_t, mref, n_heads, dk, dv):
    cum = _cum(tri, tri_t, logf)
    ref = jnp.sum(logf * mref, axis=0, keepdims=True)
    last = jnp.sum(logf, axis=0, keepdims=True)
    q_in = qh * jnp.exp(cum - ref)
    k_in = k * jnp.exp(ref - cum)
    k_st = k * jnp.exp(last - cum)
    q_dec = qh * jnp.exp(cum)
    decay = jnp.exp(last)
    outs, sts = [], []
    for h in range(n_heads):
        sk = slice(h * dk, (h + 1) * dk)
        sv = slice(h * dv, (h + 1) * dv)
        sc = _mm_nt(q_in[:, sk], k_in[:, sk]) * tri
        o = _mm_nn(sc, v[:, sv]) + _mm_nt(q_dec[:, sk], st_prev[h])
        sts.append(st_prev[h] * decay[:, sk] + _mm_tn(v[:, sv], k_st[:, sk]))
        outs.append(o)
    return jnp.concatenate(outs, axis=1), tuple(sts)


def _hg_chunk(q, f, v, l0, l1, st_prev, tri, tri_t, mref):
    lb = _sig(l0 - l1)
    sg = _sig(f)
    qh = q * _sig(q)
    logf = jnp.log(lb + (1.0 - lb) * sg)
    k = (1.0 - lb) * (1.0 - sg)
    return _chunk_core(qh, k, v, logf, st_prev, tri, tri_t, mref, 4, 128, 128)


def _gla_chunk(q, k, v, z, st_prev, tri, tri_t, mref):
    logf = (jnp.minimum(z, 0.0) - jnp.log(1.0 + jnp.exp(-jnp.abs(z)))) * (1.0 / 16.0)
    qh = q * (128.0 ** -0.5)
    return _chunk_core(qh, k, v, logf, st_prev, tri, tri_t, mref, 4, 128, 256)


def _rg_gates(xc, wbd, bias, lam):
    z = _mm_nn(xc, wbd) + bias
    r = _sig(z[:, :RG_W])
    i = _sig(z[:, RG_W:])
    log_a = -RG_C * r * _softplus(-lam)
    a = jnp.exp(log_a)
    x2 = 2.0 * log_a
    neg_expm1 = jnp.where(x2 > -1e-2, -(x2 + 0.5 * x2 * x2 + x2 * x2 * x2 * (1.0 / 6.0)), 1.0 - jnp.exp(x2))
    u = jnp.sqrt(neg_expm1) * (i * xc)
    return a, u


def _l0_combine(hf, hb, ga, of, ob, g, gain):
    ya = (hf + hb) * _gelu(ga)
    yb = _headnorm(of + ob, gain, 4, 128) * (g * _sig(g))
    return jnp.concatenate([ya, yb], axis=1)


def _l1_combine(of, ob, r, gain):
    return _headnorm(of + ob, gain, 4, 256) * (r * _sig(r))


def _norm_matmul(h, gain, w, name):
    T, D = h.shape
    N = w.shape[1]
    tm = min(512, T)

    def body(h_ref, g_ref, w_ref, o_ref, y_ref):
        x = h_ref[...]
        y = (x * _rms(x) * g_ref[...]).astype(BF16)
        y_ref[...] = y
        o_ref[...] = jnp.dot(y, w_ref[...], preferred_element_type=F32)

    return pl.pallas_call(
        body, name=name, grid=(T // tm,),
        in_specs=[pl.BlockSpec((tm, D), lambda i: (i, 0)), pl.BlockSpec((1, D), lambda i: (0, 0)),
                  pl.BlockSpec((D, N), lambda i: (0, 0))],
        out_specs=[pl.BlockSpec((tm, N), lambda i: (i, 0)), pl.BlockSpec((tm, D), lambda i: (i, 0))],
        out_shape=[jax.ShapeDtypeStruct((T, N), F32), jax.ShapeDtypeStruct((T, D), BF16)],
        compiler_params=_params("parallel"))(h, gain, w)


def _matmul_res(a, w, res, name):
    T, K = a.shape
    N = w.shape[1]
    tm = min(512, T)

    def body(a_ref, w_ref, r_ref, o_ref):
        o_ref[...] = r_ref[...] + jnp.dot(a_ref[...], w_ref[...], preferred_element_type=F32)

    return pl.pallas_call(
        body, name=name, grid=(T // tm,),
        in_specs=[pl.BlockSpec((tm, K), lambda i: (i, 0)), pl.BlockSpec((K, N), lambda i: (0, 0)),
                  pl.BlockSpec((tm, N), lambda i: (i, 0))],
        out_specs=pl.BlockSpec((tm, N), lambda i: (i, 0)),
        out_shape=jax.ShapeDtypeStruct((T, N), F32),
        compiler_params=_params("parallel"))(a, w, res)


def _dgrad(dc, w, name):
    T, N = dc.shape
    K = w.shape[0]
    tm = min(512, T)

    def body(d_ref, w_ref, o_ref):
        o_ref[...] = _dg(d_ref[...], w_ref[...], 1, 1)

    return pl.pallas_call(
        body, name=name, grid=(T // tm,),
        in_specs=[pl.BlockSpec((tm, N), lambda i: (i, 0)), pl.BlockSpec((K, N), lambda i: (0, 0))],
        out_specs=pl.BlockSpec((tm, K), lambda i: (i, 0)),
        out_shape=jax.ShapeDtypeStruct((T, K), F32),
        compiler_params=_params("parallel"))(dc, w)


def _dgrad_norm(dproj, w, h, gain, dres, name):
    T, N = dproj.shape
    D = w.shape[0]
    tm = min(512, T)

    def body(dp_ref, w_ref, h_ref, g_ref, dr_ref, dh_ref, dhb_ref, dg_ref):
        @pl.when(pl.program_id(0) == 0)
        def _():
            dg_ref[...] = jnp.zeros_like(dg_ref)

        dy = _dg(dp_ref[...], w_ref[...], 1, 1)
        dx, dgain = _rmsnorm_bwd(h_ref[...], g_ref[...], dy)
        dh = dr_ref[...] + dx
        dh_ref[...] = dh
        dhb_ref[...] = dh.astype(BF16)
        dg_ref[...] += dgain

    return pl.pallas_call(
        body, name=name, grid=(T // tm,),
        in_specs=[pl.BlockSpec((tm, N), lambda i: (i, 0)), pl.BlockSpec((D, N), lambda i: (0, 0)),
                  pl.BlockSpec((tm, D), lambda i: (i, 0)), pl.BlockSpec((1, D), lambda i: (0, 0)),
                  pl.BlockSpec((tm, D), lambda i: (i, 0))],
        out_specs=[pl.BlockSpec((tm, D), lambda i: (i, 0)), pl.BlockSpec((tm, D), lambda i: (i, 0)),
                   pl.BlockSpec((1, D), lambda i: (0, 0))],
        out_shape=[jax.ShapeDtypeStruct((T, D), F32), jax.ShapeDtypeStruct((T, D), BF16),
                   jax.ShapeDtypeStruct((1, D), F32)],
        compiler_params=_params("arbitrary"))(dproj, w, h, gain, dres)


def _wgrad(a, b, tn, name, sharded_cols=False):
    T, K = a.shape
    N = b.shape[1]
    tk = min(1024, K)
    tt = min(1024, T)
    nt = T // tt

    def body(a_ref, b_ref, o_ref):
        @pl.when(pl.program_id(2) == 0)
        def _():
            o_ref[...] = jnp.zeros_like(o_ref)

        o_ref[...] += _dg(a_ref[...], b_ref[...], 0, 0)

    if sharded_cols:
        out_spec = pl.BlockSpec((None, tk, tn), lambda k, n, t: (n, k, 0))
        out_shape = jax.ShapeDtypeStruct((N // tn, K, tn), F32)
    else:
        out_spec = pl.BlockSpec((tk, tn), lambda k, n, t: (k, n))
        out_shape = jax.ShapeDtypeStruct((K, N), F32)
    return pl.pallas_call(
        body, name=name, grid=(K // tk, N // tn, nt),
        in_specs=[pl.BlockSpec((tt, tk), lambda k, n, t: (t, k)), pl.BlockSpec((tt, tn), lambda k, n, t: (t, n))],
        out_specs=out_spec, out_shape=out_shape,
        compiler_params=_params("parallel", "parallel", "arbitrary"))(a, b)


def _mlp_fwd(h, gain, w1g, w2, name):
    T, D = h.shape
    nf, _, tf = w1g.shape
    tm = min(1024, T)

    def body(h_ref, g_ref, w1_ref, w2_ref, o_ref, pre_ref, y_ref, ysc, acc):
        j = pl.program_id(1)

        @pl.when(j == 0)
        def _():
            x = h_ref[...]
            y = (x * _rms(x) * g_ref[...]).astype(BF16)
            ysc[...] = y
            y_ref[...] = y
            acc[...] = jnp.zeros_like(acc)

        pre = jnp.dot(ysc[...], w1_ref[...], preferred_element_type=F32)
        pre_ref[...] = pre.astype(BF16)
        act = jnp.square(jnp.maximum(pre, 0.0))
        acc[...] += jnp.dot(act.astype(BF16), w2_ref[...], preferred_element_type=F32)

        @pl.when(j == nf - 1)
        def _():
            o_ref[...] = h_ref[...] + acc[...]

    return pl.pallas_call(
        body, name=name, grid=(T // tm, nf),
        in_specs=[pl.BlockSpec((tm, D), lambda i, j: (i, 0)), pl.BlockSpec((1, D), lambda i, j: (0, 0)),
                  pl.BlockSpec((None, D, tf), lambda i, j: (j, 0, 0)), pl.BlockSpec((tf, D), lambda i, j: (j, 0))],
        out_specs=[pl.BlockSpec((tm, D), lambda i, j: (i, 0)), pl.BlockSpec((tm, tf), lambda i, j: (i, j)),
                   pl.BlockSpec((tm, D), lambda i, j: (i, 0))],
        out_shape=[jax.ShapeDtypeStruct((T, D), F32), jax.ShapeDtypeStruct((T, nf * tf), BF16),
                   jax.ShapeDtypeStruct((T, D), BF16)],
        scratch_shapes=[pltpu.VMEM((tm, D), BF16), pltpu.VMEM((tm, D), F32)],
        compiler_params=_params("parallel", "arbitrary"))(h, gain, w1g, w2)


def _mlp_bwd(dout, h, gain, pre, w1g, w2, name):
    T, D = h.shape
    nf, _, tf = w1g.shape
    tm = min(512, T)

    def body(do_ref, h_ref, g_ref, pre_ref, w1_ref, w2_ref, dh_ref, dhb_ref, dpre_ref, act_ref, dg_ref, dy):
        i, j = pl.program_id(0), pl.program_id(1)

        @pl.when(j == 0)
        def _():
            dy[...] = jnp.zeros_like(dy)

        @pl.when((i == 0) & (j == 0))
        def _():
            dg_ref[...] = jnp.zeros_like(dg_ref)

        rp = jnp.maximum(pre_ref[...].astype(F32), 0.0)
        dact = _dg(do_ref[...], w2_ref[...], 1, 1)
        dpre = (dact * (2.0 * rp)).astype(BF16)
        dpre_ref[...] = dpre
        act_ref[...] = (rp * rp).astype(BF16)
        dy[...] += _dg(dpre, w1_ref[...], 1, 1)

        @pl.when(j == nf - 1)
        def _():
            dx, dgain = _rmsnorm_bwd(h_ref[...], g_ref[...], dy[...])
            dh = do_ref[...] + dx
            dh_ref[...] = dh
            dhb_ref[...] = dh.astype(BF16)
            dg_ref[...] += dgain

    return pl.pallas_call(
        body, name=name, grid=(T // tm, nf),
        in_specs=[pl.BlockSpec((tm, D), lambda i, j: (i, 0)), pl.BlockSpec((tm, D), lambda i, j: (i, 0)),
                  pl.BlockSpec((1, D), lambda i, j: (0, 0)), pl.BlockSpec((tm, tf), lambda i, j: (i, j)),
                  pl.BlockSpec((None, D, tf), lambda i, j: (j, 0, 0)), pl.BlockSpec((tf, D), lambda i, j: (j, 0))],
        out_specs=[pl.BlockSpec((tm, D), lambda i, j: (i, 0)), pl.BlockSpec((tm, D), lambda i, j: (i, 0)),
                   pl.BlockSpec((tm, tf), lambda i, j: (i, j)),
                   pl.BlockSpec((tm, tf), lambda i, j: (i, j)), pl.BlockSpec((1, D), lambda i, j: (0, 0))],
        out_shape=[jax.ShapeDtypeStruct((T, D), F32), jax.ShapeDtypeStruct((T, D), BF16),
                   jax.ShapeDtypeStruct((T, nf * tf), BF16),
                   jax.ShapeDtypeStruct((T, nf * tf), BF16), jax.ShapeDtypeStruct((1, D), F32)],
        scratch_shapes=[pltpu.VMEM((tm, D), F32)],
        compiler_params=_params("arbitrary", "arbitrary"))(dout, h, gain, pre, w1g, w2)


def _final_loss(h, gain, target, name):
    T, D = h.shape
    tm = min(512, T)

    def body(h_ref, g_ref, t_ref, l_ref, dh_ref, dhb_ref, dg_ref):
        @pl.when(pl.program_id(0) == 0)
        def _():
            l_ref[...] = jnp.zeros_like(l_ref)
            dg_ref[...] = jnp.zeros_like(dg_ref)

        x = h_ref[...]
        err = x * _rms(x) * g_ref[...] - t_ref[...]
        l_ref[...] += 0.5 * jnp.sum(jnp.mean(err * err, axis=-1, keepdims=True), axis=0, keepdims=True)
        dx, dgain = _rmsnorm_bwd(x, g_ref[...], err * (1.0 / D))
        dh_ref[...] = dx
        dhb_ref[...] = dx.astype(BF16)
        dg_ref[...] += dgain

    return pl.pallas_call(
        body, name=name, grid=(T // tm,),
        in_specs=[pl.BlockSpec((tm, D), lambda i: (i, 0)), pl.BlockSpec((1, D), lambda i: (0, 0)),
                  pl.BlockSpec((tm, D), lambda i: (i, 0))],
        out_specs=[pl.BlockSpec((SUBLANES, LANES), lambda i: (0, 0)), pl.BlockSpec((tm, D), lambda i: (i, 0)),
                   pl.BlockSpec((tm, D), lambda i: (i, 0)), pl.BlockSpec((1, D), lambda i: (0, 0))],
        out_shape=[jax.ShapeDtypeStruct((SUBLANES, LANES), F32), jax.ShapeDtypeStruct((T, D), F32),
                   jax.ShapeDtypeStruct((T, D), BF16), jax.ShapeDtypeStruct((1, D), F32)],
        compiler_params=_params("arbitrary"))(h, gain, target)


def _halo_specs(tm, T, width, col, lead=None):
    r8 = tm // SUBLANES
    nb8 = T // SUBLANES
    if lead is None:
        return [pl.BlockSpec((tm, width), lambda i: (i, col)),
                pl.BlockSpec((SUBLANES, width), lambda i: (jnp.maximum(i * r8 - 1, 0), col)),
                pl.BlockSpec((SUBLANES, width), lambda i: (jnp.minimum((i + 1) * r8, nb8 - 1), col))]
    return [pl.BlockSpec((None, tm, width), lambda i: (lead, i, col)),
            pl.BlockSpec((None, SUBLANES, width), lambda i: (lead, jnp.maximum(i * r8 - 1, 0), col)),
            pl.BlockSpec((None, SUBLANES, width), lambda i: (lead, jnp.minimum((i + 1) * r8, nb8 - 1), col))]


def _ext(cur, prev, nxt, has_prev, has_next):
    return jnp.concatenate([jnp.where(has_prev, prev, 0.0), cur, jnp.where(has_next, nxt, 0.0)], axis=0)


def _shifted(ext, offset, tm):
    n = ext.shape[0]
    sh = (-offset) % n
    r = ext if sh == 0 else pltpu.roll(ext, sh, 0)
    return r[SUBLANES:SUBLANES + tm]


def _rg_conv_fwd(proj, cw8, cb, name):
    T = proj.shape[0]
    tm = min(512, T)
    nT = T // tm

    def body(cur_ref, prev_ref, next_ref, w_ref, b_ref, o_ref):
        i = pl.program_id(0)
        ext = _ext(cur_ref[...], prev_ref[...], next_ref[...], i > 0, i < nT - 1)
        acc = jnp.broadcast_to(b_ref[...], (tm, RG_W))
        for k in range(4):
            acc = acc + w_ref[k:k + 1, :] * _shifted(ext, k - 2, tm)
        o_ref[...] = acc

    return pl.pallas_call(
        body, name=name, grid=(nT,),
        in_specs=_halo_specs(tm, T, RG_W, 0) + [pl.BlockSpec((SUBLANES, RG_W), lambda i: (0, 0)),
                                                pl.BlockSpec((1, RG_W), lambda i: (0, 0))],
        out_specs=pl.BlockSpec((tm, RG_W), lambda i: (i, 0)),
        out_shape=jax.ShapeDtypeStruct((T, RG_W), F32),
        compiler_params=_params("parallel"))(proj, proj, proj, cw8, cb)


def _rg_conv_bwd(dxc, proj, cw8, name):
    T = proj.shape[0]
    tm = min(512, T)
    nT = T // tm

    def body(a0, p0, n0, a1, p1, n1, xa, xp, xn, w_ref, dxa_ref, dw_ref, db_ref):
        i = pl.program_id(0)

        @pl.when(i == 0)
        def _():
            dw_ref[...] = jnp.zeros_like(dw_ref)
            db_ref[...] = jnp.zeros_like(db_ref)

        has_p, has_n = i > 0, i < nT - 1
        cur = a0[...] + a1[...]
        dext = _ext(cur, p0[...] + p1[...], n0[...] + n1[...], has_p, has_n)
        xext = _ext(xa[...], xp[...], xn[...], has_p, has_n)
        acc = jnp.zeros((tm, RG_W), F32)
        rows = []
        for k in range(4):
            acc = acc + w_ref[k:k + 1, :] * _shifted(dext, 2 - k, tm)
            rows.append(jnp.sum(cur * _shifted(xext, k - 2, tm), axis=0, keepdims=True))
        dxa_ref[...] = acc
        dw_ref[...] += jnp.concatenate(rows + [jnp.zeros((4, RG_W), F32)], axis=0)
        db_ref[...] += jnp.sum(cur, axis=0, keepdims=True)

    return pl.pallas_call(
        body, name=name, grid=(nT,),
        in_specs=(_halo_specs(tm, T, RG_W, 0, lead=0) + _halo_specs(tm, T, RG_W, 0, lead=1)
                  + _halo_specs(tm, T, RG_W, 0) + [pl.BlockSpec((SUBLANES, RG_W), lambda i: (0, 0))]),
        out_specs=[pl.BlockSpec((tm, RG_W), lambda i: (i, 0)), pl.BlockSpec((SUBLANES, RG_W), lambda i: (0, 0)),
                   pl.BlockSpec((1, RG_W), lambda i: (0, 0))],
        out_shape=[jax.ShapeDtypeStruct((T, RG_W), F32), jax.ShapeDtypeStruct((SUBLANES, RG_W), F32),
                   jax.ShapeDtypeStruct((1, RG_W), F32)],
        compiler_params=_params("arbitrary"))(dxc, dxc, dxc, dxc, dxc, dxc, proj, proj, proj, cw8)


def _rg_scan_fwd(xc, wbd, bias, lam, name):
    T = xc.shape[0]
    tm = min(512, T)
    nT = T // tm

    def tile(d, i):
        return i + d * (nT - 1 - 2 * i)

    def body(xc_ref, w_ref, b_ref, lam_ref, h_ref, a_sc, u_sc, carry):
        d, i = pl.program_id(0), pl.program_id(1)

        @pl.when(i == 0)
        def _():
            carry[...] = jnp.zeros_like(carry)

        a, u = _rg_gates(xc_ref[...], w_ref[...], b_ref[...], lam_ref[...])
        a_sc[...] = a
        u_sc[...] = u

        def step(t, h):
            tt = t + d * (tm - 1 - 2 * t)
            h = a_sc[pl.ds(tt, 1), :] * h + u_sc[pl.ds(tt, 1), :]
            h_ref[pl.ds(tt, 1), :] = h
            return h

        carry[0:1, :] = lax.fori_loop(0, tm, step, carry[0:1, :])

    return pl.pallas_call(
        body, name=name, grid=(2, nT),
        in_specs=[pl.BlockSpec((tm, RG_W), lambda d, i: (tile(d, i), 0)),
                  pl.BlockSpec((None, RG_W, 2 * RG_W), lambda d, i: (d, 0, 0)),
                  pl.BlockSpec((None, 1, 2 * RG_W), lambda d, i: (d, 0, 0)),
                  pl.BlockSpec((None, 1, RG_W), lambda d, i: (d, 0, 0))],
        out_specs=pl.BlockSpec((None, tm, RG_W), lambda d, i: (d, tile(d, i), 0)),
        out_shape=jax.ShapeDtypeStruct((2, T, RG_W), F32),
        scratch_shapes=[pltpu.VMEM((tm, RG_W), F32), pltpu.VMEM((tm, RG_W), F32), pltpu.VMEM((SUBLANES, RG_W), F32)],
        compiler_params=_params("arbitrary", "arbitrary"))(xc, wbd, bias, lam)


def _rg_scan_bwd(xc, wbd, bias, lam, hs, dho, name):
    T = xc.shape[0]
    tm = min(512, T)
    nT = T // tm
    r8 = tm // SUBLANES
    nb8 = T // SUBLANES

    def tile(d, i):
        return (nT - 1 - i) + d * (2 * i - (nT - 1))

    def body(xc_ref, w_ref, b_ref, lam_ref, hc_ref, hp_ref, hn_ref, dho_ref,
             dxc_ref, dw_ref, db_ref, dlam_ref, a_sc, dt_sc, carry):
        d, i = pl.program_id(0), pl.program_id(1)
        ti = tile(d, i)

        @pl.when(i == 0)
        def _():
            carry[...] = jnp.zeros_like(carry)
            dw_ref[...] = jnp.zeros_like(dw_ref)
            db_ref[...] = jnp.zeros_like(db_ref)
            dlam_ref[...] = jnp.zeros_like(dlam_ref)

        (a, _), vjp = jax.vjp(_rg_gates, xc_ref[...], w_ref[...].astype(F32), b_ref[...], lam_ref[...])
        a_sc[...] = a

        def step(t, c):
            tt = (tm - 1 - t) + d * (2 * t - (tm - 1))
            dt = dho_ref[pl.ds(tt, 1), :] + c
            dt_sc[pl.ds(tt, 1), :] = dt
            return a_sc[pl.ds(tt, 1), :] * dt

        carry[0:1, :] = lax.fori_loop(0, tm, step, carry[0:1, :])
        dtot = dt_sc[...]
        ext = _ext(hc_ref[...], hp_ref[...], hn_ref[...], ti > 0, ti < nT - 1)
        hprev = jnp.where(d == 0, _shifted(ext, -1, tm), _shifted(ext, 1, tm))
        dxc, dw, db, dlam = vjp((dtot * hprev, dtot))
        dxc_ref[...] = dxc
        dw_ref[...] += dw
        db_ref[...] += db
        dlam_ref[...] += dlam

    return pl.pallas_call(
        body, name=name, grid=(2, nT),
        in_specs=[pl.BlockSpec((tm, RG_W), lambda d, i: (tile(d, i), 0)),
                  pl.BlockSpec((None, RG_W, 2 * RG_W), lambda d, i: (d, 0, 0)),
                  pl.BlockSpec((None, 1, 2 * RG_W), lambda d, i: (d, 0, 0)),
                  pl.BlockSpec((None, 1, RG_W), lambda d, i: (d, 0, 0)),
                  pl.BlockSpec((None, tm, RG_W), lambda d, i: (d, tile(d, i), 0)),
                  pl.BlockSpec((None, SUBLANES, RG_W), lambda d, i: (d, jnp.maximum(tile(d, i) * r8 - 1, 0), 0)),
                  pl.BlockSpec((None, SUBLANES, RG_W),
                               lambda d, i: (d, jnp.minimum((tile(d, i) + 1) * r8, nb8 - 1), 0)),
                  pl.BlockSpec((tm, RG_W), lambda d, i: (tile(d, i), 0))],
        out_specs=[pl.BlockSpec((None, tm, RG_W), lambda d, i: (d, tile(d, i), 0)),
                   pl.BlockSpec((None, RG_W, 2 * RG_W), lambda d, i: (d, 0, 0)),
                   pl.BlockSpec((None, 1, 2 * RG_W), lambda d, i: (d, 0, 0)),
                   pl.BlockSpec((None, 1, RG_W), lambda d, i: (d, 0, 0))],
        out_shape=[jax.ShapeDtypeStruct((2, T, RG_W), F32), jax.ShapeDtypeStruct((2, RG_W, 2 * RG_W), F32),
                   jax.ShapeDtypeStruct((2, 1, 2 * RG_W), F32), jax.ShapeDtypeStruct((2, 1, RG_W), F32)],
        scratch_shapes=[pltpu.VMEM((tm, RG_W), F32), pltpu.VMEM((tm, RG_W), F32), pltpu.VMEM((SUBLANES, RG_W), F32)],
        compiler_params=_params("arbitrary", "arbitrary"))(xc, wbd, bias, lam, hs, hs, hs, dho)


def _chunk_idx(d, c, n_chunks):
    return c + d * (n_chunks - 1 - 2 * c)


def _chunk_idx_rev(d, c, n_chunks):
    return (n_chunks - 1 - c) + d * (2 * c - (n_chunks - 1))


def _hg_fwd(proj, l0, l1, name):
    T = proj.shape[0]
    nC = T // CHUNK
    H, dk, dv = 4, 128, 128

    def body(q_ref, f_ref, v_ref, l0_ref, l1_ref, o_ref, sp_ref, st):
        d, c = pl.program_id(0), pl.program_id(1)

        @pl.when(c == 0)
        def _():
            st[...] = jnp.zeros_like(st)

        tri, tri_t, mref = _tri_consts(d)
        stp = tuple(st[h] for h in range(H))
        sp_ref[...] = st[...]
        o, stn = _hg_chunk(q_ref[...], f_ref[...], v_ref[...], l0_ref[...], l1_ref[...], stp, tri, tri_t, mref)
        o_ref[...] = o
        for h in range(H):
            st[h] = stn[h]

    row = lambda d, c: _chunk_idx(d, c, nC)
    return pl.pallas_call(
        body, name=name, grid=(2, nC),
        in_specs=[pl.BlockSpec((CHUNK, HG_W), lambda d, c: (row(d, c), 2)),
                  pl.BlockSpec((CHUNK, HG_W), lambda d, c: (row(d, c), 3 + d)),
                  pl.BlockSpec((CHUNK, HG_W), lambda d, c: (row(d, c), 5)),
                  pl.BlockSpec((1, HG_W), lambda d, c: (0, 0)), pl.BlockSpec((1, HG_W), lambda d, c: (0, 0))],
        out_specs=[pl.BlockSpec((None, CHUNK, H * dv), lambda d, c: (d, row(d, c), 0)),
                   pl.BlockSpec((None, None, H, dv, dk), lambda d, c: (d, row(d, c), 0, 0, 0))],
        out_shape=[jax.ShapeDtypeStruct((2, T, H * dv), F32), jax.ShapeDtypeStruct((2, nC, H, dv, dk), F32)],
        scratch_shapes=[pltpu.VMEM((H, dv, dk), F32)],
        compiler_params=_params("arbitrary", "arbitrary"))(proj, proj, proj, l0, l1)


def _hg_bwd(proj, l0, l1, sprev, do, name):
    T = proj.shape[0]
    nC = T // CHUNK
    H, dk, dv = 4, 128, 128

    def body(q_ref, f_ref, v_ref, l0_ref, l1_ref, sp_ref, do_ref, dq_ref, df_ref, dv_ref, dl0_ref, dl1_ref, dst):
        d, c = pl.program_id(0), pl.program_id(1)

        @pl.when(c == 0)
        def _():
            dst[...] = jnp.zeros_like(dst)
            dl0_ref[...] = jnp.zeros_like(dl0_ref)
            dl1_ref[...] = jnp.zeros_like(dl1_ref)

        tri, tri_t, mref = _tri_consts(d)
        fn = lambda q, f, v, a0, a1, stp: _hg_chunk(q, f, v, a0, a1, stp, tri, tri_t, mref)
        stp = tuple(sp_ref[h] for h in range(H))
        _, vjp = jax.vjp(fn, q_ref[...], f_ref[...], v_ref[...], l0_ref[...], l1_ref[...], stp)
        dq, df, dvv, dl0, dl1, dstp = vjp((do_ref[...], tuple(dst[h] for h in range(H))))
        dq_ref[...] = dq
        df_ref[...] = df
        dv_ref[...] = dvv
        dl0_ref[...] += dl0
        dl1_ref[...] += dl1
        for h in range(H):
            dst[h] = dstp[h]

    row = lambda d, c: _chunk_idx_rev(d, c, nC)
    tok = lambda: pl.BlockSpec((None, CHUNK, HG_W), lambda d, c: (d, row(d, c), 0))
    par = lambda: pl.BlockSpec((None, 1, HG_W), lambda d, c: (d, 0, 0))
    return pl.pallas_call(
        body, name=name, grid=(2, nC),
        in_specs=[pl.BlockSpec((CHUNK, HG_W), lambda d, c: (row(d, c), 2)),
                  pl.BlockSpec((CHUNK, HG_W), lambda d, c: (row(d, c), 3 + d)),
                  pl.BlockSpec((CHUNK, HG_W), lambda d, c: (row(d, c), 5)),
                  pl.BlockSpec((1, HG_W), lambda d, c: (0, 0)), pl.BlockSpec((1, HG_W), lambda d, c: (0, 0)),
                  pl.BlockSpec((None, None, H, dv, dk), lambda d, c: (d, row(d, c), 0, 0, 0)),
                  pl.BlockSpec((CHUNK, H * dv), lambda d, c: (row(d, c), 0))],
        out_specs=[tok(), tok(), tok(), par(), par()],
        out_shape=[jax.ShapeDtypeStruct((2, T, HG_W), F32)] * 3 + [jax.ShapeDtypeStruct((2, 1, HG_W), F32)] * 2,
        scratch_shapes=[pltpu.VMEM((H, dv, dk), F32)],
        compiler_params=_params("arbitrary", "arbitrary"))(proj, proj, proj, l0, l1, sprev, do)


def _gate_logits(proj, wup, bg, name):
    T = proj.shape[0]
    tm = min(512, T)

    def body(lr_ref, w_ref, b_ref, z_ref, lrb_ref):
        lr = lr_ref[...].astype(BF16)
        lrb_ref[...] = lr
        for d in range(2):
            z_ref[d] = _dg(lr, w_ref[d], 1, 0) + b_ref[d]

    return pl.pallas_call(
        body, name=name, grid=(T // tm,),
        in_specs=[pl.BlockSpec((tm, LANES), lambda i: (i, 24)), pl.BlockSpec((2, LANES, 512), lambda i: (0, 0, 0)),
                  pl.BlockSpec((2, 1, 512), lambda i: (0, 0, 0))],
        out_specs=[pl.BlockSpec((2, tm, 512), lambda i: (0, i, 0)), pl.BlockSpec((tm, LANES), lambda i: (i, 0))],
        out_shape=[jax.ShapeDtypeStruct((2, T, 512), F32), jax.ShapeDtypeStruct((T, LANES), BF16)],
        compiler_params=_params("parallel"))(proj, wup, bg)


def _gate_logits_bwd(dz, wup, name):
    T = dz.shape[1]
    tm = min(512, T)

    def body(dz_ref, w_ref, dlr_ref, db_ref, dzb_ref):
        @pl.when(pl.program_id(0) == 0)
        def _():
            db_ref[...] = jnp.zeros_like(db_ref)

        acc = jnp.zeros((tm, LANES), F32)
        for d in range(2):
            g = dz_ref[d]
            gb = g.astype(BF16)
            dzb_ref[d] = gb
            acc = acc + _dg(gb, w_ref[d], 1, 1)
            db_ref[d] += jnp.sum(g, axis=0, keepdims=True)
        dlr_ref[...] = acc

    return pl.pallas_call(
        body, name=name, grid=(T // tm,),
        in_specs=[pl.BlockSpec((2, tm, 512), lambda i: (0, i, 0)), pl.BlockSpec((2, LANES, 512), lambda i: (0, 0, 0))],
        out_specs=[pl.BlockSpec((tm, LANES), lambda i: (i, 0)), pl.BlockSpec((2, 1, 512), lambda i: (0, 0, 0)),
                   pl.BlockSpec((2, tm, 512), lambda i: (0, i, 0))],
        out_shape=[jax.ShapeDtypeStruct((T, LANES), F32), jax.ShapeDtypeStruct((2, 1, 512), F32),
                   jax.ShapeDtypeStruct((2, T, 512), BF16)],
        compiler_params=_params("arbitrary"))(dz, wup)


def _gla_fwd(proj, z, name):
    T = proj.shape[0]
    nC = T // CHUNK
    H, dk, dv = 4, 128, 256

    def body(q_ref, k_ref, v_ref, z_ref, o_ref, sp_ref, st):
        d, c = pl.program_id(0), pl.program_id(1)

        @pl.when(c == 0)
        def _():
            st[...] = jnp.zeros_like(st)

        tri, tri_t, mref = _tri_consts(d)
        stp = tuple(st[h] for h in range(H))
        sp_ref[...] = st[...]
        o, stn = _gla_chunk(q_ref[...], k_ref[...], v_ref[...], z_ref[...], stp, tri, tri_t, mref)
        o_ref[...] = o
        for h in range(H):
            st[h] = stn[h]

    row = lambda d, c: _chunk_idx(d, c, nC)
    return pl.pallas_call(
        body, name=name, grid=(2, nC),
        in_specs=[pl.BlockSpec((CHUNK, 512), lambda d, c: (row(d, c), 0)),
                  pl.BlockSpec((CHUNK, 512), lambda d, c: (row(d, c), 1)),
                  pl.BlockSpec((CHUNK, 1024), lambda d, c: (row(d, c), 1)),
                  pl.BlockSpec((None, CHUNK, 512), lambda d, c: (d, row(d, c), 0))],
        out_specs=[pl.BlockSpec((None, CHUNK, H * dv), lambda d, c: (d, row(d, c), 0)),
                   pl.BlockSpec((None, None, H, dv, dk), lambda d, c: (d, row(d, c), 0, 0, 0))],
        out_shape=[jax.ShapeDtypeStruct((2, T, H * dv), F32), jax.ShapeDtypeStruct((2, nC, H, dv, dk), F32)],
        scratch_shapes=[pltpu.VMEM((H, dv, dk), F32)],
        compiler_params=_params("arbitrary", "arbitrary"))(proj, proj, proj, z)


def _gla_bwd(proj, z, sprev, do, name):
    T = proj.shape[0]
    nC = T // CHUNK
    H, dk, dv = 4, 128, 256

    def body(q_ref, k_ref, v_ref, z_ref, sp_ref, do_ref, dq_ref, dk_ref, dv_ref, dz_ref, dst):
        d, c = pl.program_id(0), pl.program_id(1)

        @pl.when(c == 0)
        def _():
            dst[...] = jnp.zeros_like(dst)

        tri, tri_t, mref = _tri_consts(d)
        fn = lambda q, k, v, zz, stp: _gla_chunk(q, k, v, zz, stp, tri, tri_t, mref)
        stp = tuple(sp_ref[h] for h in range(H))
        _, vjp = jax.vjp(fn, q_ref[...], k_ref[...], v_ref[...], z_ref[...], stp)
        dq, dkk, dvv, dzz, dstp = vjp((do_ref[...], tuple(dst[h] for h in range(H))))
        dq_ref[...] = dq
        dk_ref[...] = dkk
        dv_ref[...] = dvv
        dz_ref[...] = dzz
        for h in range(H):
            dst[h] = dstp[h]

    row = lambda d, c: _chunk_idx_rev(d, c, nC)
    tok = lambda w: pl.BlockSpec((None, CHUNK, w), lambda d, c: (d, row(d, c), 0))
    return pl.pallas_call(
        body, name=name, grid=(2, nC),
        in_specs=[pl.BlockSpec((CHUNK, 512), lambda d, c: (row(d, c), 0)),
                  pl.BlockSpec((CHUNK, 512), lambda d, c: (row(d, c), 1)),
                  pl.BlockSpec((CHUNK, 1024), lambda d, c: (row(d, c), 1)),
                  tok(512),
                  pl.BlockSpec((None, None, H, dv, dk), lambda d, c: (d, row(d, c), 0, 0, 0)),
                  pl.BlockSpec((CHUNK, H * dv), lambda d, c: (row(d, c), 0))],
        out_specs=[tok(512), tok(512), tok(1024), tok(512)],
        out_shape=[jax.ShapeDtypeStruct((2, T, 512), F32), jax.ShapeDtypeStruct((2, T, 512), F32),
                   jax.ShapeDtypeStruct((2, T, 1024), F32), jax.ShapeDtypeStruct((2, T, 512), F32)],
        scratch_shapes=[pltpu.VMEM((H, dv, dk), F32)],
        compiler_params=_params("arbitrary", "arbitrary"))(proj, proj, proj, z, sprev, do)


def _l0_combine_fwd(hs, proj, o, gain, name):
    T = proj.shape[0]
    tm = min(512, T)

    def body(hf, hb, ga, of, ob, g, gn, out):
        out[...] = _l0_combine(hf[...], hb[...], ga[...], of[...], ob[...], g[...], gn[...]).astype(BF16)

    two = lambda lead: pl.BlockSpec((None, tm, 512), lambda i: (lead, i, 0))
    return pl.pallas_call(
        body, name=name, grid=(T // tm,),
        in_specs=[two(0), two(1), pl.BlockSpec((tm, 512), lambda i: (i, 1)), two(0), two(1),
                  pl.BlockSpec((tm, 512), lambda i: (i, 6)), pl.BlockSpec((1, 512), lambda i: (0, 0))],
        out_specs=pl.BlockSpec((tm, 1024), lambda i: (i, 0)),
        out_shape=jax.ShapeDtypeStruct((T, 1024), BF16),
        compiler_params=_params("parallel"))(hs, hs, proj, o, o, proj, gain)


def _l0_combine_bwd(hs, proj, o, gain, dmix, name):
    T = proj.shape[0]
    tm = min(512, T)

    def body(hf, hb, ga, of, ob, g, gn, dm, dho_ref, dga_ref, do_ref, dg_ref, dgn_ref):
        @pl.when(pl.program_id(0) == 0)
        def _():
            dgn_ref[...] = jnp.zeros_like(dgn_ref)

        _, vjp = jax.vjp(_l0_combine, hf[...], hb[...], ga[...], of[...], ob[...], g[...], gn[...])
        dhf, _, dga, dof, _, dg, dgn = vjp(dm[...])
        dho_ref[...] = dhf
        dga_ref[...] = dga
        do_ref[...] = dof
        dg_ref[...] = dg
        dgn_ref[...] += dgn

    two = lambda lead: pl.BlockSpec((None, tm, 512), lambda i: (lead, i, 0))
    tok = lambda: pl.BlockSpec((tm, 512), lambda i: (i, 0))
    return pl.pallas_call(
        body, name=name, grid=(T // tm,),
        in_specs=[two(0), two(1), pl.BlockSpec((tm, 512), lambda i: (i, 1)), two(0), two(1),
                  pl.BlockSpec((tm, 512), lambda i: (i, 6)), pl.BlockSpec((1, 512), lambda i: (0, 0)),
                  pl.BlockSpec((tm, 1024), lambda i: (i, 0))],
        out_specs=[tok(), tok(), tok(), tok(), pl.BlockSpec((1, 512), lambda i: (0, 0))],
        out_shape=[jax.ShapeDtypeStruct((T, 512), F32)] * 4 + [jax.ShapeDtypeStruct((1, 512), F32)],
        compiler_params=_params("arbitrary"))(hs, hs, proj, o, o, proj, gain, dmix)


def _l0_assemble(dxa, dga, dq, df, dv, dg, name):
    T = dxa.shape[0]
    tm = min(512, T)

    def body(xa, ga, q0, q1, f0, f1, v0, v1, g, out):
        out[...] = jnp.concatenate([xa[...], ga[...], q0[...] + q1[...], f0[...], f1[...], v0[...] + v1[...],
                                    g[...]], axis=1).astype(BF16)

    two = lambda lead: pl.BlockSpec((None, tm, 512), lambda i: (lead, i, 0))
    tok = lambda: pl.BlockSpec((tm, 512), lambda i: (i, 0))
    return pl.pallas_call(
        body, name=name, grid=(T // tm,),
        in_specs=[tok(), tok(), two(0), two(1), two(0), two(1), two(0), two(1), tok()],
        out_specs=pl.BlockSpec((tm, AB_IN), lambda i: (i, 0)),
        out_shape=jax.ShapeDtypeStruct((T, AB_IN), BF16),
        compiler_params=_params("parallel"))(dxa, dga, dq, dq, df, df, dv, dv, dg)


def _l1_combine_fwd(o, proj, gain, name):
    T = proj.shape[0]
    tm = min(512, T)

    def body(of, ob, r, gn, out):
        out[...] = _l1_combine(of[...], ob[...], r[...], gn[...]).astype(BF16)

    two = lambda lead: pl.BlockSpec((None, tm, 1024), lambda i: (lead, i, 0))
    return pl.pallas_call(
        body, name=name, grid=(T // tm,),
        in_specs=[two(0), two(1), pl.BlockSpec((tm, 1024), lambda i: (i, 2)), pl.BlockSpec((1, 1024), lambda i: (0, 0))],
        out_specs=pl.BlockSpec((tm, 1024), lambda i: (i, 0)),
        out_shape=jax.ShapeDtypeStruct((T, 1024), BF16),
        compiler_params=_params("parallel"))(o, o, proj, gain)


def _l1_combine_bwd(o, proj, gain, dmix, name):
    T = proj.shape[0]
    tm = min(512, T)

    def body(of, ob, r, gn, dm, do_ref, dr_ref, dgn_ref):
        @pl.when(pl.program_id(0) == 0)
        def _():
            dgn_ref[...] = jnp.zeros_like(dgn_ref)

        _, vjp = jax.vjp(_l1_combine, of[...], ob[...], r[...], gn[...])
        dof, _, dr, dgn = vjp(dm[...])
        do_ref[...] = dof
        dr_ref[...] = dr
        dgn_ref[...] += dgn

    two = lambda lead: pl.BlockSpec((None, tm, 1024), lambda i: (lead, i, 0))
    tok = lambda: pl.BlockSpec((tm, 1024), lambda i: (i, 0))
    return pl.pallas_call(
        body, name=name, grid=(T // tm,),
        in_specs=[two(0), two(1), pl.BlockSpec((tm, 1024), lambda i: (i, 2)),
                  pl.BlockSpec((1, 1024), lambda i: (0, 0)), tok()],
        out_specs=[tok(), tok(), pl.BlockSpec((1, 1024), lambda i: (0, 0))],
        out_shape=[jax.ShapeDtypeStruct((T, 1024), F32)] * 2 + [jax.ShapeDtypeStruct((1, 1024), F32)],
        compiler_params=_params("arbitrary"))(o, o, proj, gain, dmix)


def _l1_assemble(dq, dk, dv, dr, dlr, name):
    T = dr.shape[0]
    tm = min(512, T)

    def body(q0, q1, k0, k1, v0, v1, r, a, out):
        out[...] = jnp.concatenate([q0[...] + q1[...], k0[...] + k1[...], v0[...] + v1[...], r[...], a[...]],
                                   axis=1).astype(BF16)

    two = lambda lead, w: pl.BlockSpec((None, tm, w), lambda i: (lead, i, 0))
    return pl.pallas_call(
        body, name=name, grid=(T // tm,),
        in_specs=[two(0, 512), two(1, 512), two(0, 512), two(1, 512), two(0, 1024), two(1, 1024),
                  pl.BlockSpec((tm, 1024), lambda i: (i, 0)), pl.BlockSpec((tm, LANES), lambda i: (i, 0))],
        out_specs=pl.BlockSpec((tm, GLA_IN_PAD), lambda i: (i, 0)),
        out_shape=jax.ShapeDtypeStruct((T, GLA_IN_PAD), BF16),
        compiler_params=_params("parallel"))(dq, dq, dk, dk, dv, dv, dr, dlr)


HBM_SPEC = pl.BlockSpec(memory_space=pltpu.HBM)


def _place():
    x, y, c = lax.axis_index("x"), lax.axis_index("y"), lax.axis_index("c")
    return x, y, c


def _allgather_hbm(shards, name):
    n = len(shards)

    def body(*refs):
        ins, outs = refs[:n], refs[n:2 * n]
        send_sems, recv_sems, local_sems = refs[2 * n:]
        x, y, c = _place()
        me, sibling = (x, y, c), (x, y, 1 - c)
        chips = [(1 - x, y), (x, 1 - y), (1 - x, 1 - y)]

        def slot(a, p):
            return outs[a].at[4 * p[0] + 2 * p[1] + p[2]]

        def copy(a, k, block, to, src=None):
            return pltpu.make_async_remote_copy(
                src_ref=slot(a, block) if src is None else src, dst_ref=slot(a, block),
                send_sem=send_sems.at[a * 7 + k], recv_sem=recv_sems.at[a * 7 + k],
                device_id=to, device_id_type=MESH)

        mine = [pltpu.make_async_copy(ins[a], slot(a, me), local_sems.at[a]) for a in range(n)]
        for cp in mine:
            cp.start()
        first = []
        for a in range(n):
            first.append(copy(a, 0, me, sibling, src=ins[a]))
            first += [copy(a, 1 + j, me, (*chip, c), src=ins[a]) for j, chip in enumerate(chips)]
        for cp in first:
            cp.start()
        passed = []
        for j, chip in enumerate(chips):
            for a in range(n):
                copy(a, 1 + j, (*chip, c), me).wait_recv()
                cp = copy(a, 4 + j, (*chip, c), sibling)
                cp.start()
                passed.append(cp)
        for a in range(n):
            copy(a, 0, sibling, me).wait_recv()
            for j, chip in enumerate(chips):
                copy(a, 4 + j, (*chip, 1 - c), me).wait_recv()
        for cp in first + passed:
            cp.wait_send()
        for cp in mine:
            cp.wait()

    return pl.pallas_call(
        body, name=name,
        in_specs=[HBM_SPEC] * n, out_specs=[HBM_SPEC] * n,
        out_shape=[jax.ShapeDtypeStruct((N_DEV,) + s.shape, s.dtype) for s in shards],
        scratch_shapes=[pltpu.SemaphoreType.DMA((7 * n,)), pltpu.SemaphoreType.DMA((7 * n,)),
                        pltpu.SemaphoreType.DMA((n,))],
        compiler_params=pltpu.CompilerParams(has_side_effects=True))(*shards)


def _allgather_vmem(x_shard, name, reduce=False):
    m_per, n = x_shard.shape

    def body(x_ref, out_ref, *rest):
        if reduce:
            sum_ref, send_sems, recv_sems, local_sem = rest
        else:
            send_sems, recv_sems, local_sem = rest
        x, y, c = _place()
        me, sibling = (x, y, c), (x, y, 1 - c)
        chips = [(1 - x, y), (x, 1 - y), (1 - x, 1 - y)]

        def rows(px, py, pc):
            return out_ref.at[pl.ds((4 * px + 2 * py + pc) * m_per, m_per), :]

        def copy(k, block, to, src=None):
            return pltpu.make_async_remote_copy(
                src_ref=rows(*block) if src is None else src, dst_ref=rows(*block),
                send_sem=send_sems.at[k], recv_sem=recv_sems.at[k], device_id=to, device_id_type=MESH)

        mine = pltpu.make_async_copy(x_ref, rows(*me), local_sem)
        mine.start()
        first = [copy(0, me, sibling, src=x_ref)]
        first += [copy(1 + j, me, (*chip, c), src=x_ref) for j, chip in enumerate(chips)]
        for cp in first:
            cp.start()
        passed = [copy(4 + j, (*chip, c), sibling) for j, chip in enumerate(chips)]
        for j, chip in enumerate(chips):
            copy(1 + j, (*chip, c), me).wait_recv()
            passed[j].start()
        copy(0, sibling, me).wait_recv()
        for j, chip in enumerate(chips):
            copy(4 + j, (*chip, 1 - c), me).wait_recv()
        for cp in first + passed:
            cp.wait_send()
        mine.wait()
        if reduce:
            acc = out_ref[pl.ds(0, m_per), :]
            for j in range(1, N_DEV):
                acc = acc + out_ref[pl.ds(j * m_per, m_per), :]
            sum_ref[...] = acc

    vm = pl.BlockSpec(memory_space=pltpu.VMEM)
    out_shape = [jax.ShapeDtypeStruct((N_DEV * m_per, n), x_shard.dtype)]
    if reduce:
        out_shape.append(jax.ShapeDtypeStruct((m_per, n), x_shard.dtype))
    res = pl.pallas_call(
        body, name=name, in_specs=[vm], out_specs=[vm] * len(out_shape), out_shape=out_shape,
        scratch_shapes=[pltpu.SemaphoreType.DMA((7,)), pltpu.SemaphoreType.DMA((7,)), pltpu.SemaphoreType.DMA],
        compiler_params=pltpu.CompilerParams(has_side_effects=True, vmem_limit_bytes=VMEM_LIMIT))(x_shard)
    return res[1] if reduce else res[0]


SEM_SPEC = pl.BlockSpec(memory_space=pltpu.SEMAPHORE)
DATAFLOW_EFFECT = pltpu.SideEffectType.DATAFLOW_SIDE_EFFECTING


def _copies(plan, srcs, lands, send_sems, recv_sems):
    x, y, c = _place()
    return [pltpu.make_async_remote_copy(src_ref=s, dst_ref=d, send_sem=send_sems.at[k], recv_sem=recv_sems.at[k],
                                         device_id=dev, device_id_type=MESH)
            for k, (s, d, dev) in enumerate(plan(srcs, lands, x, y, c))]


def _copies_start(plan, n_copies, srcs, lands, name):
    ns, nl = len(srcs), len(lands)

    def body(*refs):
        send_sems, recv_sems = refs[ns + nl], refs[ns + nl + 1]
        for cp in _copies(plan, refs[:ns], refs[ns:ns + nl], send_sems, recv_sems):
            cp.start()
        refs[-1][...] = jnp.zeros_like(refs[-1])

    arrays = list(srcs) + list(lands)
    res = pl.pallas_call(
        body, name=name,
        in_specs=[HBM_SPEC] * (ns + nl),
        out_specs=tuple([SEM_SPEC, SEM_SPEC] + [HBM_SPEC] * (ns + nl) + [pl.BlockSpec(memory_space=pltpu.VMEM)]),
        out_shape=tuple([pltpu.SemaphoreType.DMA((n_copies,)), pltpu.SemaphoreType.DMA((n_copies,))]
                        + [pltpu.HBM(a.shape, a.dtype) for a in arrays]
                        + [jax.ShapeDtypeStruct((SUBLANES, LANES), F32)]),
        input_output_aliases={i: 2 + i for i in range(ns + nl)},
        compiler_params=pltpu.CompilerParams(has_side_effects=DATAFLOW_EFFECT),
    )(*[pltpu.with_memory_space_constraint(a, pltpu.HBM) for a in arrays])
    return res[0], res[1], list(res[2:2 + ns]), list(res[2 + ns:2 + ns + nl]), res[-1]


def _copies_wait(plan, started, after, name):
    send_sems, recv_sems, srcs, lands, _ = started
    ns, nl = len(srcs), len(lands)

    def body(*refs):
        for cp in _copies(plan, refs[:ns], refs[ns:ns + nl], refs[ns + nl], refs[ns + nl + 1]):
            cp.wait_send()
            cp.wait_recv()

    arrays = list(srcs) + list(lands)
    res = pl.pallas_call(
        body, name=name,
        in_specs=[HBM_SPEC] * (ns + nl) + [SEM_SPEC, SEM_SPEC, pl.BlockSpec(memory_space=pl.ANY)],
        out_specs=tuple([HBM_SPEC] * (ns + nl)),
        out_shape=tuple(pltpu.HBM(a.shape, a.dtype) for a in arrays),
        input_output_aliases={i: i for i in range(ns + nl)},
        compiler_params=pltpu.CompilerParams(has_side_effects=DATAFLOW_EFFECT),
    )(*arrays, send_sems, recv_sems, after)
    return list(res[:ns]), list(res[ns:])


def _after(token, value):
    return lax.optimization_barrier((value, token))[0]


def _chips(x, y):
    return [(1 - x, y), (x, 1 - y), (1 - x, 1 - y)]


def _plan_gather_first(srcs, lands, x, y, c):
    me = 4 * x + 2 * y + c
    out = []
    for s, l in zip(srcs, lands):
        out.append((s, l.at[me], (x, y, 1 - c)))
        out += [(s, l.at[me], (*chip, c)) for chip in _chips(x, y)]
    return out


def _plan_gather_pass(srcs, lands, x, y, c):
    out = []
    for l in lands:
        for chip in _chips(x, y):
            slot = l.at[4 * chip[0] + 2 * chip[1] + c]
            out.append((slot, slot, (x, y, 1 - c)))
    return out


def _plan_grads_sibling(srcs, lands, x, y, c):
    return [(s.at[2 * q + (1 - c)], l.at[q], (x, y, 1 - c)) for s, l in zip(srcs, lands) for q in range(4)]


def _plan_grads_chips(srcs, lands, x, y, c):
    return [(s.at[2 * chip[0] + chip[1]], l.at[k], (*chip, c))
            for s, l in zip(srcs, lands) for k, chip in enumerate(_chips(x, y))]


def _landing(n_slots, like):
    return [lax.empty((n_slots,) + a.shape[1:], a.dtype) for a in like]


def _chip_partial(g, r1, place, name):
    _, R, C = g.shape
    tr = min(256, R)
    assert R % tr == 0

    def body(pl_ref, g_ref, r_ref, pb_ref, pm_ref):
        q = pl.program_id(1)
        s = g_ref[...] + r_ref[...]
        pb_ref[...] = s.astype(BF16)

        @pl.when(q == pl_ref[1])
        def _():
            pm_ref[...] = s

    grid_spec = pltpu.PrefetchScalarGridSpec(
        num_scalar_prefetch=1, grid=(R // tr, 4),
        in_specs=[pl.BlockSpec((None, tr, C), lambda r, q, p: (2 * q + p[0], r, 0)),
                  pl.BlockSpec((None, tr, C), lambda r, q, p: (q, r, 0))],
        out_specs=[pl.BlockSpec((None, tr, C), lambda r, q, p: (q, r, 0)),
                   pl.BlockSpec((tr, C), lambda r, q, p: (r, 0))])
    return pl.pallas_call(
        body, name=name, grid_spec=grid_spec,
        out_shape=[jax.ShapeDtypeStruct((4, R, C), BF16), jax.ShapeDtypeStruct((R, C), F32)],
        compiler_params=_params("parallel", "arbitrary"))(place, g, r1)


def _adamw(w, gparts, m, v, name):
    R, C = w.shape
    tr = min(256, R)
    assert R % tr == 0
    g0, g3 = gparts

    def body(w_ref, g0_ref, *rest):
        if g3 is not None:
            g3_ref, m_ref, v_ref, go, do, mo, vo = rest
        else:
            m_ref, v_ref, go, do, mo, vo = rest
        g = g0_ref[...]
        if g3 is not None:
            for k in range(3):
                g = g + g3_ref[k].astype(F32)
        wv = w_ref[...]
        mn = ADAM_B1 * m_ref[...] + (1.0 - ADAM_B1) * g
        vn = ADAM_B2 * v_ref[...] + (1.0 - ADAM_B2) * jnp.square(g)
        m_hat = mn / (1.0 - ADAM_B1 ** ADAM_STEP)
        v_hat = vn / (1.0 - ADAM_B2 ** ADAM_STEP)
        go[...] = g
        do[...] = -ADAM_LR * (m_hat / (jnp.sqrt(v_hat) + ADAM_EPS) + ADAM_WD * wv)
        mo[...] = mn
        vo[...] = vn

    blk = pl.BlockSpec((tr, C), lambda i: (i, 0))
    in_specs = [blk, blk] + ([pl.BlockSpec((3, tr, C), lambda i: (0, i, 0))] if g3 is not None else []) + [blk, blk]
    args = [w, g0] + ([g3] if g3 is not None else []) + [m, v]
    return pl.pallas_call(
        body, name=name, grid=(R // tr,), in_specs=in_specs, out_specs=[blk] * 4,
        out_shape=[jax.ShapeDtypeStruct((R, C), F32)] * 4,
        compiler_params=_params("parallel"))(*args)


SMALL_SHARDED = ("rg_conv_w", "rg_b_a", "rg_b_x", "rg_lambda", "gla_w_gate_up", "gla_b_gate", "gla_norm")
SMALL_REPLICATED = ("norm_mix", "norm_mlp", "norm_final", "rg_conv_b", "rg_w_a", "rg_w_x", "hg_lb_logits", "hg_norm")
WEIGHT_NAMES = ("norm_mix", "norm_mlp", "norm_final", "mlp_w1", "mlp_w2", "ab_w_in", "ab_w_out", "rg_conv_w",
                "rg_conv_b", "rg_w_a", "rg_b_a", "rg_w_x", "rg_b_x", "rg_lambda", "hg_lb_logits", "hg_norm",
                "gla_w_in", "gla_w_out", "gla_w_gate_up", "gla_b_gate", "gla_norm")


def _rows128(a):
    return a.reshape(-1, LANES)


def _part_rows(a):
    return -(-(a.size // LANES) // SUBLANES) * SUBLANES


def _pack_rows(arrays, pad_to=SUBLANES):
    parts = [jnp.pad(_rows128(a), ((0, _part_rows(a) - a.size // LANES), (0, 0))) for a in arrays]
    total = sum(p.shape[0] for p in parts)
    extra = (-total) % pad_to
    if extra:
        parts.append(jnp.zeros((extra, LANES), parts[0].dtype))
    return jnp.concatenate(parts, axis=0)


def _unshard_last(g, shape_local):
    nd = len(shape_local)
    t = g.reshape((N_DEV,) + tuple(shape_local))
    t = jnp.moveaxis(t, 0, nd - 1)
    return t.reshape(tuple(shape_local[:-1]) + (N_DEV * shape_local[-1],))


def _block_diag(w):
    eye = jnp.eye(8, dtype=w.dtype)
    return (w[:, :, :, None, :] * eye[None, :, None, :, None]).reshape(2, RG_W, RG_W)


def _block_diag_extract(dw):
    t = dw.reshape(2, 8, 64, 8, 64)
    return jnp.moveaxis(jnp.diagonal(t, axis1=1, axis2=3), -1, 1)


def kernel(x, norm_mix, norm_mlp, norm_final, mlp_w1, mlp_w2, ab_w_in, ab_w_out, rg_conv_w, rg_conv_b, rg_w_a, rg_b_a, rg_w_x, rg_b_x, rg_lambda, hg_lb_logits, hg_norm, gla_w_in, gla_w_out, gla_w_gate_up, gla_b_gate, gla_norm, loss_target, m_norm_mix, m_norm_mlp, m_norm_final, m_mlp_w1, m_mlp_w2, m_ab_w_in, m_ab_w_out, m_rg_conv_w, m_rg_conv_b, m_rg_w_a, m_rg_b_a, m_rg_w_x, m_rg_b_x, m_rg_lambda, m_hg_lb_logits, m_hg_norm, m_gla_w_in, m_gla_w_out, m_gla_w_gate_up, m_gla_b_gate, m_gla_norm, v_norm_mix, v_norm_mlp, v_norm_final, v_mlp_w1, v_mlp_w2, v_ab_w_in, v_ab_w_out, v_rg_conv_w, v_rg_conv_b, v_rg_w_a, v_rg_b_a, v_rg_w_x, v_rg_b_x, v_rg_lambda, v_hg_lb_logits, v_hg_norm, v_gla_w_in, v_gla_w_out, v_gla_w_gate_up, v_gla_b_gate, v_gla_norm):
    w_loc = dict(norm_mix=norm_mix, norm_mlp=norm_mlp, norm_final=norm_final, mlp_w1=mlp_w1, mlp_w2=mlp_w2,
                 ab_w_in=ab_w_in, ab_w_out=ab_w_out, rg_conv_w=rg_conv_w, rg_conv_b=rg_conv_b, rg_w_a=rg_w_a,
                 rg_b_a=rg_b_a, rg_w_x=rg_w_x, rg_b_x=rg_b_x, rg_lambda=rg_lambda, hg_lb_logits=hg_lb_logits,
                 hg_norm=hg_norm, gla_w_in=gla_w_in, gla_w_out=gla_w_out, gla_w_gate_up=gla_w_gate_up,
                 gla_b_gate=gla_b_gate, gla_norm=gla_norm)
    m_loc = dict(norm_mix=m_norm_mix, norm_mlp=m_norm_mlp, norm_final=m_norm_final, mlp_w1=m_mlp_w1,
                 mlp_w2=m_mlp_w2, ab_w_in=m_ab_w_in, ab_w_out=m_ab_w_out, rg_conv_w=m_rg_conv_w,
                 rg_conv_b=m_rg_conv_b, rg_w_a=m_rg_w_a, rg_b_a=m_rg_b_a, rg_w_x=m_rg_w_x, rg_b_x=m_rg_b_x,
                 rg_lambda=m_rg_lambda, hg_lb_logits=m_hg_lb_logits, hg_norm=m_hg_norm, gla_w_in=m_gla_w_in,
                 gla_w_out=m_gla_w_out, gla_w_gate_up=m_gla_w_gate_up, gla_b_gate=m_gla_b_gate,
                 gla_norm=m_gla_norm)
    v_loc = dict(norm_mix=v_norm_mix, norm_mlp=v_norm_mlp, norm_final=v_norm_final, mlp_w1=v_mlp_w1,
                 mlp_w2=v_mlp_w2, ab_w_in=v_ab_w_in, ab_w_out=v_ab_w_out, rg_conv_w=v_rg_conv_w,
                 rg_conv_b=v_rg_conv_b, rg_w_a=v_rg_w_a, rg_b_a=v_rg_b_a, rg_w_x=v_rg_w_x, rg_b_x=v_rg_b_x,
                 rg_lambda=v_rg_lambda, hg_lb_logits=v_hg_lb_logits, hg_norm=v_hg_norm, gla_w_in=v_gla_w_in,
                 gla_w_out=v_gla_w_out, gla_w_gate_up=v_gla_w_gate_up, gla_b_gate=v_gla_b_gate,
                 gla_norm=v_gla_norm)

    T = x.shape[1]
    h0 = x.reshape(T, D_MODEL)
    target = loss_target.reshape(T, D_MODEL)
    ax, ay, ac = lax.axis_index("x"), lax.axis_index("y"), lax.axis_index("c")
    dev = 4 * ax + 2 * ay + ac
    place = jnp.stack([ac, 2 * ax + ay]).astype(jnp.int32)

    abin_g, about_g = _allgather_hbm([ab_w_in[0].astype(BF16), ab_w_out[0].astype(BF16)], "ag_first")
    wab_in = jnp.transpose(abin_g, (1, 0, 2)).reshape(D_MODEL, AB_IN)
    wab_out = about_g.reshape(D_MODEL, D_MODEL)
    rest_shards = [mlp_w1[0].astype(BF16), mlp_w2[0].astype(BF16), gla_w_in[0].astype(BF16),
                   gla_w_out[0].astype(BF16), mlp_w1[1].astype(BF16), mlp_w2[1].astype(BF16)]
    rest_lands = [jnp.broadcast_to(s[None], (N_DEV,) + s.shape) for s in rest_shards]
    ag_started = _copies_start(_plan_gather_first, 4 * len(rest_shards), rest_shards, rest_lands, "ag_rest_start")

    small_local = [w_loc[n] for n in SMALL_SHARDED]
    small_g = _allgather_vmem(_pack_rows(small_local, 8), "ag_small")
    small_g = small_g.reshape(N_DEV, -1, LANES)
    full = {}
    off = 0
    for n, a in zip(SMALL_SHARDED, small_local):
        full[n] = _unshard_last(small_g[:, off:off + a.size // LANES].reshape(N_DEV, a.size), a.shape)
        off += _part_rows(a)
    conv_w = full["rg_conv_w"][0]
    b_a, b_x, lam = full["rg_b_a"][0], full["rg_b_x"][0], full["rg_lambda"][0]
    w_up, b_gate, g_norm = full["gla_w_gate_up"][0], full["gla_b_gate"][0], full["gla_norm"]

    cw8 = jnp.pad(conv_w, ((0, 4), (0, 0)))
    wbd = jnp.concatenate([_block_diag(rg_w_a[0]), _block_diag(rg_w_x[0])], axis=2).astype(BF16)
    rg_bias = jnp.concatenate([b_a, b_x], axis=1).reshape(2, 1, 2 * RG_W)
    lam3 = lam.reshape(2, 1, RG_W)
    l0, l1 = hg_lb_logits[0:1], hg_lb_logits[1:2]
    wup_pad = jnp.zeros((2, LANES, 512), F32).at[0, 0:16].set(w_up[0]).at[1, 16:32].set(w_up[1])
    bg3 = b_gate.reshape(2, 1, 512)
    nmix0, nmix1 = norm_mix[0:1], norm_mix[1:2]
    nmlp0, nmlp1 = norm_mlp[0:1], norm_mlp[1:2]
    nfin = norm_final.reshape(1, D_MODEL)

    proj0, y0 = _norm_matmul(h0, _after(ag_started[4], nmix0), wab_in, "l0_in_proj")
    xc = _rg_conv_fwd(proj0, cw8, rg_conv_b, "rg_conv")
    hs = _rg_scan_fwd(xc, wbd, rg_bias, lam3, "rg_scan")
    o_hg, s_hg = _hg_fwd(proj0, l0, l1, "hg_chunks")
    mixin0 = _l0_combine_fwd(hs, proj0, o_hg, hg_norm, "l0_combine")
    _, rest_lands = _copies_wait(_plan_gather_first, ag_started, mixin0, "ag_rest_wait")
    pass_started = _copies_start(_plan_gather_pass, 3 * len(rest_lands), [], rest_lands, "ag_pass_start")
    h1 = _matmul_res(mixin0, _after(pass_started[4], wab_out), h0, "l0_out_proj")
    _, rest_g = _copies_wait(_plan_gather_pass, pass_started, h1, "ag_pass_wait")
    w1g = (rest_g[0], rest_g[4])
    w2f = (rest_g[1].reshape(D_FF, D_MODEL), rest_g[5].reshape(D_FF, D_MODEL))
    wgla_in = jnp.pad(jnp.transpose(rest_g[2], (1, 0, 2)).reshape(D_MODEL, GLA_IN),
                      ((0, 0), (0, GLA_IN_PAD - GLA_IN)))
    wgla_out = rest_g[3].reshape(D_MODEL, D_MODEL)
    h2, pre0, ym0 = _mlp_fwd(h1, nmlp0, w1g[0], w2f[0], "mlp0")
    proj1, y1 = _norm_matmul(h2, nmix1, wgla_in, "l1_in_proj")
    z_gate, lr_b = _gate_logits(proj1, wup_pad, bg3, "gla_gate_logits")
    o_gla, s_gla = _gla_fwd(proj1, z_gate, "gla_chunks")
    mixin1 = _l1_combine_fwd(o_gla, proj1, g_norm, "l1_combine")
    h3 = _matmul_res(mixin1, wgla_out, h2, "l1_out_proj")
    h4, pre1, ym1 = _mlp_fwd(h3, nmlp1, w1g[1], w2f[1], "mlp1")
    loss_blk, dh4, dh4b, d_nfin = _final_loss(h4, nfin, target, "final_loss")
    loss = lax.psum(loss_blk[0, 0], ("x", "y", "c"))

    dh3, dh3b, dpre1, act1, d_nmlp1 = _mlp_bwd(dh4, h3, nmlp1, pre1, w1g[1], w2f[1], "mlp1_bwd")
    g_w1_1 = _wgrad(ym1, dpre1, 512, "mlp1_dw1", sharded_cols=True)
    g_w2_1 = _wgrad(act1, dh4b, 1024, "mlp1_dw2")
    dmixin1 = _dgrad(dh3b, wgla_out, "l1_out_dgrad")
    g_gla_out = _wgrad(mixin1, dh3b, 1024, "l1_out_dw")
    do_gla, dr, d_gnorm = _l1_combine_bwd(o_gla, proj1, g_norm, dmixin1, "l1_combine_bwd")
    dq1, dk1, dv1, dz_gate = _gla_bwd(proj1, z_gate, s_gla, do_gla, "gla_chunks_bwd")
    dlr1, d_bg, dz_b = _gate_logits_bwd(dz_gate, wup_pad, "gla_gate_logits_bwd")
    d_wup = [_wgrad(lr_b, dz_b[d], 512, "gla_gate_dw%d" % d) for d in range(2)]
    dproj1 = _l1_assemble(dq1, dk1, dv1, dr, dlr1, "l1_assemble")
    dh2, dh2b, d_nmix1 = _dgrad_norm(dproj1, wgla_in, h2, nmix1, dh3, "l1_in_dgrad")
    g_gla_in = _wgrad(y1, dproj1, 640, "l1_in_dw")

    def reduce_start(grads, tag):
        return _copies_start(_plan_grads_sibling, 4 * len(grads), grads, _landing(4, grads), "rs_%s_d2d_start" % tag)

    def reduce_mid(started, after, tag):
        grads, got = _copies_wait(_plan_grads_sibling, started, after, "rs_%s_d2d_wait" % tag)
        parts = [_chip_partial(g, r, place, "rs_%s_partial%d" % (tag, a)) for a, (g, r) in enumerate(zip(grads, got))]
        pb = [p[0] for p in parts]
        return _copies_start(_plan_grads_chips, 3 * len(pb), pb, _landing(3, pb), "rs_%s_ici_start" % tag), \
            [p[1] for p in parts]

    def reduce_end(started, mine, after, tag, ws, ms, vs):
        _, got = _copies_wait(_plan_grads_chips, started, after, "rs_%s_ici_wait" % tag)
        return [_adamw(w, (p, r), m, v, "adamw_%s%d" % (tag, a))
                for a, (w, p, r, m, v) in enumerate(zip(ws, mine, got, ms, vs))]

    slots_l1 = [g_w1_1, g_w2_1.reshape(N_DEV, 512, D_MODEL),
                jnp.transpose(g_gla_in[:, :GLA_IN].reshape(D_MODEL, N_DEV, GLA_IN // N_DEV), (1, 0, 2)),
                g_gla_out.reshape(N_DEV, 128, D_MODEL)]
    ra_d2d = reduce_start(slots_l1, "l1")

    dh1, dh1b, dpre0, act0, d_nmlp0 = _mlp_bwd(dh2, h1, _after(ra_d2d[4], nmlp0), pre0, w1g[0], w2f[0], "mlp0_bwd")
    g_w1_0 = _wgrad(ym0, dpre0, 512, "mlp0_dw1", sharded_cols=True)
    g_w2_0 = _wgrad(act0, dh2b, 1024, "mlp0_dw2")
    ra_ici, ra_mine = reduce_mid(ra_d2d, g_w2_0, "l1")
    rb_d2d = reduce_start([g_w1_0, g_w2_0.reshape(N_DEV, 512, D_MODEL)], "mlp0")
    dmixin0 = _dgrad(dh1b, _after(rb_d2d[4], _after(ra_ici[4], wab_out)), "l0_out_dgrad")
    g_ab_out = _wgrad(mixin0, dh1b, 1024, "l0_out_dw")
    dho, dga, do_hg, dg_gate, d_hgnorm = _l0_combine_bwd(hs, proj0, o_hg, hg_norm, dmixin0, "l0_combine_bwd")
    dxc, d_wbd, d_rgb, d_lam = _rg_scan_bwd(xc, wbd, rg_bias, lam3, hs, dho, "rg_scan_bwd")
    dxa, d_cw8, d_cb = _rg_conv_bwd(dxc, proj0, cw8, "rg_conv_bwd")
    dq0, df0, dv0, d_l0, d_l1 = _hg_bwd(proj0, l0, l1, s_hg, do_hg, "hg_chunks_bwd")
    rb_ici, rb_mine = reduce_mid(rb_d2d, d_l0, "mlp0")
    dproj0 = _l0_assemble(dxa, _after(rb_ici[4], dga), dq0, df0, dv0, dg_gate, "l0_assemble")
    g_ab_in = _wgrad(y0, dproj0, 512, "l0_in_dw")
    rc_d2d = reduce_start([jnp.transpose(g_ab_in.reshape(D_MODEL, N_DEV, AB_IN // N_DEV), (1, 0, 2)),
                           g_ab_out.reshape(N_DEV, 128, D_MODEL)], "ab")
    dx, _, d_nmix0 = _dgrad_norm(dproj0, wab_in, h0, _after(rc_d2d[4], nmix0), dh1, "l0_in_dgrad")
    rc_ici, rc_mine = reduce_mid(rc_d2d, d_nmix0, "ab")

    res_l1 = reduce_end(ra_ici, ra_mine, rc_ici[4], "l1",
                        [mlp_w1[1], mlp_w2[1], gla_w_in[0], gla_w_out[0]],
                        [m_mlp_w1[1], m_mlp_w2[1], m_gla_w_in[0], m_gla_w_out[0]],
                        [v_mlp_w1[1], v_mlp_w2[1], v_gla_w_in[0], v_gla_w_out[0]])
    res_mlp0 = reduce_end(rb_ici, rb_mine, res_l1[3][0], "mlp0", [mlp_w1[0], mlp_w2[0]],
                          [m_mlp_w1[0], m_mlp_w2[0]], [v_mlp_w1[0], v_mlp_w2[0]])

    def stacked(a, b):
        return tuple(jnp.stack([a[k], b[k]]) for k in range(4))

    res = {"mlp_w1": stacked(res_mlp0[0], res_l1[0]), "mlp_w2": stacked(res_mlp0[1], res_l1[1]),
           "gla_w_in": tuple(res_l1[2][k][None] for k in range(4)),
           "gla_w_out": tuple(res_l1[3][k][None] for k in range(4))}

    d_wa = _block_diag_extract(d_wbd[:, :, :RG_W])[None]
    d_wx = _block_diag_extract(d_wbd[:, :, RG_W:])[None]
    small_full = {
        "norm_mix": jnp.concatenate([d_nmix0, d_nmix1], axis=0), "norm_mlp": jnp.concatenate([d_nmlp0, d_nmlp1], axis=0),
        "norm_final": d_nfin.reshape(D_MODEL), "rg_conv_b": d_cb, "rg_w_a": d_wa, "rg_w_x": d_wx,
        "hg_lb_logits": jnp.concatenate([d_l0[0] + d_l0[1], d_l1[0] + d_l1[1]], axis=0), "hg_norm": d_hgnorm,
        "rg_conv_w": d_cw8[0:4][None], "rg_b_a": d_rgb[:, 0, :RG_W][None], "rg_b_x": d_rgb[:, 0, RG_W:][None],
        "rg_lambda": d_lam[:, 0, :][None],
        "gla_w_gate_up": jnp.stack([d_wup[0][0:16], d_wup[1][16:32]])[None], "gla_b_gate": d_bg[:, 0, :][None],
        "gla_norm": d_gnorm}
    small_names = SMALL_REPLICATED + SMALL_SHARDED
    packed = _pack_rows([small_full[n] for n in small_names], 8)
    summed = _allgather_vmem(packed, "ar_small", reduce=True)
    g_small = {}
    off = 0
    for n in small_names:
        a = small_full[n]
        gfull = summed[off:off + a.size // LANES].reshape(a.shape)
        off += _part_rows(a)
        if n in SMALL_SHARDED:
            loc = w_loc[n].shape[-1]
            gfull = lax.dynamic_slice_in_dim(gfull, dev * loc, loc, axis=gfull.ndim - 1)
        g_small[n] = gfull
    sw = _pack_rows([w_loc[n] for n in small_names], 256)
    sg = _pack_rows([g_small[n] for n in small_names], 256)
    sm = _pack_rows([m_loc[n] for n in small_names], 256)
    sv = _pack_rows([v_loc[n] for n in small_names], 256)
    small_res = _adamw(sw, (sg, None), sm, sv, "adamw_small")
    res_ab = reduce_end(rc_ici, rc_mine, small_res[0], "ab", [ab_w_in[0], ab_w_out[0]],
                        [m_ab_w_in[0], m_ab_w_out[0]], [v_ab_w_in[0], v_ab_w_out[0]])
    res["ab_w_in"] = tuple(res_ab[0][k][None] for k in range(4))
    res["ab_w_out"] = tuple(res_ab[1][k][None] for k in range(4))
    off = 0
    for n in small_names:
        a = w_loc[n]
        nr = a.size // LANES
        res[n] = tuple(small_res[k][off:off + nr].reshape(a.shape) for k in range(4))
        off += _part_rows(a)

    grad_x = dx.reshape(1, T, D_MODEL)
    out = [loss, grad_x]
    for k in range(4):
        out += [res[n][k] for n in WEIGHT_NAMES]
    return tuple(out)
```

```python
import jax
import jax.numpy as jnp
from jax import lax
from jax.experimental import pallas as pl
from jax.experimental.pallas import tpu as pltpu

F32, BF16 = jnp.float32, jnp.bfloat16
HI = lax.Precision.HIGHEST
MESH = pl.DeviceIdType.MESH

D_MODEL = 1024
D_FF = 4096
RG_W = 512
HG_W = 512
CHUNK = 64
EPS = 1e-6
RG_C = 8.0
AB_IN = 3584
GLA_IN = 3104
GLA_IN_PAD = 3200
N_DEV = 8
LANES = 128
SUBLANES = 8
VMEM_LIMIT = 48 * 1024 * 1024

ADAM_LR, ADAM_B1, ADAM_B2, ADAM_EPS, ADAM_WD, ADAM_STEP = 0.001, 0.9, 0.999, 1e-08, 0.01, 10


def _params(*sem):
    return pltpu.CompilerParams(dimension_semantics=sem, vmem_limit_bytes=VMEM_LIMIT)


def _dg(a, b, ca, cb):
    return lax.dot_general(a.astype(BF16), b.astype(BF16), (((ca,), (cb,)), ((), ())),
                           preferred_element_type=F32)


@jax.custom_vjp
def _mm_nn(a, b):
    return _dg(a, b, 1, 0)


_mm_nn.defvjp(lambda a, b: (_dg(a, b, 1, 0), (a, b)),
              lambda res, g: (_dg(g, res[1], 1, 1), _dg(res[0], g, 0, 0)))


@jax.custom_vjp
def _mm_nt(a, b):
    return _dg(a, b, 1, 1)


_mm_nt.defvjp(lambda a, b: (_dg(a, b, 1, 1), (a, b)),
              lambda res, g: (_dg(g, res[1], 1, 0), _dg(g, res[0], 0, 0)))


@jax.custom_vjp
def _mm_tn(a, b):
    return _dg(a, b, 0, 0)


_mm_tn.defvjp(lambda a, b: (_dg(a, b, 0, 0), (a, b)),
              lambda res, g: (_dg(res[1], g, 1, 1), _dg(res[0], g, 1, 0)))


@jax.custom_vjp
def _cum(tri, tri_t, x):
    return jnp.dot(tri, x, precision=HI, preferred_element_type=F32)


_cum.defvjp(lambda tri, tri_t, x: (jnp.dot(tri, x, precision=HI, preferred_element_type=F32), (tri, tri_t)),
            lambda res, g: (jnp.zeros_like(res[0]), jnp.zeros_like(res[1]),
                            jnp.dot(res[1], g, precision=HI, preferred_element_type=F32)))


def _sig(x):
    return 1.0 / (1.0 + jnp.exp(-x))


def _gelu(x):
    return 0.5 * x * (1.0 + jnp.tanh(0.7978845608028654 * (x + 0.044715 * (x * x * x))))


def _softplus(z):
    return jnp.maximum(z, 0.0) + jnp.log(1.0 + jnp.exp(-jnp.abs(z)))


def _rms(x):
    return lax.rsqrt(jnp.mean(x * x, axis=-1, keepdims=True) + EPS)


def _rmsnorm_bwd(x, gain, dy):
    r = _rms(x)
    xh = x * r
    dgain = jnp.sum(dy * xh, axis=0, keepdims=True)
    dxh = dy * gain
    dx = r * (dxh - xh * jnp.mean(dxh * xh, axis=-1, keepdims=True))
    return dx, dgain


def _headnorm(o, gain, n_heads, hd):
    parts = []
    for h in range(n_heads):
        oh = o[:, h * hd:(h + 1) * hd]
        parts.append(oh * _rms(oh))
    return jnp.concatenate(parts, axis=1) * gain


def _tri_consts(d):
    row = lax.broadcasted_iota(jnp.int32, (CHUNK, CHUNK), 0)
    col = lax.broadcasted_iota(jnp.int32, (CHUNK, CHUNK), 1)
    ge = (row >= col).astype(F32)
    le = (row <= col).astype(F32)
    tri = jnp.where(d == 0, ge, le)
    tri_t = jnp.where(d == 0, le, ge)
    r1 = lax.broadcasted_iota(jnp.int32, (CHUNK, 1), 0)
    mref = jnp.where(d == 0, (r1 <= CHUNK // 2).astype(F32), (r1 >= CHUNK // 2 - 1).astype(F32))
    return tri, tri_t, mref


def _chunk_core(qh, k, v, logf, st_prev, tri, tri_t, mref, n_heads, dk, dv):
    cum = _cum(tri, tri_t, logf)
    ref = jnp.sum(logf * mref, axis=0, keepdims=True)
    last = jnp.sum(logf, axis=0, keepdims=True)
    q_in = qh * jnp.exp(cum - ref)
    k_in = k * jnp.exp(ref - cum)
    k_st = k * jnp.exp(last - cum)
    q_dec = qh * jnp.exp(cum)
    decay = jnp.exp(last)
    outs, sts = [], []
    for h in range(n_heads):
        sk = slice(h * dk, (h + 1) * dk)
        sv = slice(h * dv, (h + 1) * dv)
        sc = _mm_nt(q_in[:, sk], k_in[:, sk]) * tri
        o = _mm_nn(sc, v[:, sv]) + _mm_nt(q_dec[:, sk], st_prev[h])
        sts.append(st_prev[h] * decay[:, sk] + _mm_tn(v[:, sv], k_st[:, sk]))
        outs.append(o)
    return jnp.concatenate(outs, axis=1), tuple(sts)


def _hg_chunk(q, f, v, l0, l1, st_prev, tri, tri_t, mref):
    lb = _sig(l0 - l1)
    sg = _sig(f)
    qh = q * _sig(q)
    logf = jnp.log(lb + (1.0 - lb) * sg)
    k = (1.0 - lb) * (1.0 - sg)
    return _chunk_core(qh, k, v, logf, st_prev, tri, tri_t, mref, 4, 128, 128)


def _gla_chunk(q, k, v, z, st_prev, tri, tri_t, mref):
    logf = (jnp.minimum(z, 0.0) - jnp.log(1.0 + jnp.exp(-jnp.abs(z)))) * (1.0 / 16.0)
    qh = q * (128.0 ** -0.5)
    return _chunk_core(qh, k, v, logf, st_prev, tri, tri_t, mref, 4, 128, 256)


def _rg_gates(xc, wbd, bias, lam):
    z = _mm_nn(xc, wbd) + bias
    r = _sig(z[:, :RG_W])
    i = _sig(z[:, RG_W:])
    log_a = -RG_C * r * _softplus(-lam)
    a = jnp.exp(log_a)
    x2 = 2.0 * log_a
    neg_expm1 = jnp.where(x2 > -1e-2, -(x2 + 0.5 * x2 * x2 + x2 * x2 * x2 * (1.0 / 6.0)), 1.0 - jnp.exp(x2))
    u = jnp.sqrt(neg_expm1) * (i * xc)
    return a, u


def _l0_combine(hf, hb, ga, of, ob, g, gain):
    ya = (hf + hb) * _gelu(ga)
    yb = _headnorm(of + ob, gain, 4, 128) * (g * _sig(g))
    return jnp.concatenate([ya, yb], axis=1)


def _l1_combine(of, ob, r, gain):
    return _headnorm(of + ob, gain, 4, 256) * (r * _sig(r))


def _norm_matmul(h, gain, w, name):
    T, D = h.shape
    N = w.shape[1]
    tm = min(512, T)

    def body(h_ref, g_ref, w_ref, o_ref, y_ref):
        x = h_ref[...]
        y = (x * _rms(x) * g_ref[...]).astype(BF16)
        y_ref[...] = y
        o_ref[...] = jnp.dot(y, w_ref[...], preferred_element_type=F32)

    return pl.pallas_call(
        body, name=name, grid=(T // tm,),
        in_specs=[pl.BlockSpec((tm, D), lambda i: (i, 0)), pl.BlockSpec((1, D), lambda i: (0, 0)),
                  pl.BlockSpec((D, N), lambda i: (0, 0))],
        out_specs=[pl.BlockSpec((tm, N), lambda i: (i, 0)), pl.BlockSpec((tm, D), lambda i: (i, 0))],
        out_shape=[jax.ShapeDtypeStruct((T, N), F32), jax.ShapeDtypeStruct((T, D), BF16)],
        compiler_params=_params("parallel"))(h, gain, w)


def _matmul_res(a, w, res, name):
    T, K = a.shape
    N = w.shape[1]
    tm = min(512, T)

    def body(a_ref, w_ref, r_ref, o_ref):
        o_ref[...] = r_ref[...] + jnp.dot(a_ref[...], w_ref[...], preferred_element_type=F32)

    return pl.pallas_call(
        body, name=name, grid=(T // tm,),
        in_specs=[pl.BlockSpec((tm, K), lambda i: (i, 0)), pl.BlockSpec((K, N), lambda i: (0, 0)),
                  pl.BlockSpec((tm, N), lambda i: (i, 0))],
        out_specs=pl.BlockSpec((tm, N), lambda i: (i, 0)),
        out_shape=jax.ShapeDtypeStruct((T, N), F32),
        compiler_params=_params("parallel"))(a, w, res)


def _dgrad(dc, w, name):
    T, N = dc.shape
    K = w.shape[0]
    tm = min(512, T)

    def body(d_ref, w_ref, o_ref):
        o_ref[...] = _dg(d_ref[...], w_ref[...], 1, 1)

    return pl.pallas_call(
        body, name=name, grid=(T // tm,),
        in_specs=[pl.BlockSpec((tm, N), lambda i: (i, 0)), pl.BlockSpec((K, N), lambda i: (0, 0))],
        out_specs=pl.BlockSpec((tm, K), lambda i: (i, 0)),
        out_shape=jax.ShapeDtypeStruct((T, K), F32),
        compiler_params=_params("parallel"))(dc, w)


def _dgrad_norm(dproj, w, h, gain, dres, name):
    T, N = dproj.shape
    D = w.shape[0]
    tm = min(512, T)

    def body(dp_ref, w_ref, h_ref, g_ref, dr_ref, dh_ref, dhb_ref, dg_ref):
        @pl.when(pl.program_id(0) == 0)
        def _():
            dg_ref[...] = jnp.zeros_like(dg_ref)

        dy = _dg(dp_ref[...], w_ref[...], 1, 1)
        dx, dgain = _rmsnorm_bwd(h_ref[...], g_ref[...], dy)
        dh = dr_ref[...] + dx
        dh_ref[...] = dh
        dhb_ref[...] = dh.astype(BF16)
        dg_ref[...] += dgain

    return pl.pallas_call(
        body, name=name, grid=(T // tm,),
        in_specs=[pl.BlockSpec((tm, N), lambda i: (i, 0)), pl.BlockSpec((D, N), lambda i: (0, 0)),
                  pl.BlockSpec((tm, D), lambda i: (i, 0)), pl.BlockSpec((1, D), lambda i: (0, 0)),
                  pl.BlockSpec((tm, D), lambda i: (i, 0))],
        out_specs=[pl.BlockSpec((tm, D), lambda i: (i, 0)), pl.BlockSpec((tm, D), lambda i: (i, 0)),
                   pl.BlockSpec((1, D), lambda i: (0, 0))],
        out_shape=[jax.ShapeDtypeStruct((T, D), F32), jax.ShapeDtypeStruct((T, D), BF16),
                   jax.ShapeDtypeStruct((1, D), F32)],
        compiler_params=_params("arbitrary"))(dproj, w, h, gain, dres)


def _wgrad(a, b, tn, name, sharded_cols=False):
    T, K = a.shape
    N = b.shape[1]
    tk = min(1024, K)
    tt = min(1024, T)
    nt = T // tt

    def body(a_ref, b_ref, o_ref):
        @pl.when(pl.program_id(2) == 0)
        def _():
            o_ref[...] = jnp.zeros_like(o_ref)

        o_ref[...] += _dg(a_ref[...], b_ref[...], 0, 0)

    if sharded_cols:
        out_spec = pl.BlockSpec((None, tk, tn), lambda k, n, t: (n, k, 0))
        out_shape = jax.ShapeDtypeStruct((N // tn, K, tn), F32)
    else:
        out_spec = pl.BlockSpec((tk, tn), lambda k, n, t: (k, n))
        out_shape = jax.ShapeDtypeStruct((K, N), F32)
    return pl.pallas_call(
        body, name=name, grid=(K // tk, N // tn, nt),
        in_specs=[pl.BlockSpec((tt, tk), lambda k, n, t: (t, k)), pl.BlockSpec((tt, tn), lambda k, n, t: (t, n))],
        out_specs=out_spec, out_shape=out_shape,
        compiler_params=_params("parallel", "parallel", "arbitrary"))(a, b)


def _mlp_fwd(h, gain, w1g, w2, name):
    T, D = h.shape
    nf, _, tf = w1g.shape
    tm = min(1024, T)

    def body(h_ref, g_ref, w1_ref, w2_ref, o_ref, pre_ref, y_ref, ysc, acc):
        j = pl.program_id(1)

        @pl.when(j == 0)
        def _():
            x = h_ref[...]
            y = (x * _rms(x) * g_ref[...]).astype(BF16)
            ysc[...] = y
            y_ref[...] = y
            acc[...] = jnp.zeros_like(acc)

        pre = jnp.dot(ysc[...], w1_ref[...], preferred_element_type=F32)
        pre_ref[...] = pre.astype(BF16)
        act = jnp.square(jnp.maximum(pre, 0.0))
        acc[...] += jnp.dot(act.astype(BF16), w2_ref[...], preferred_element_type=F32)

        @pl.when(j == nf - 1)
        def _():
            o_ref[...] = h_ref[...] + acc[...]

    return pl.pallas_call(
        body, name=name, grid=(T // tm, nf),
        in_specs=[pl.BlockSpec((tm, D), lambda i, j: (i, 0)), pl.BlockSpec((1, D), lambda i, j: (0, 0)),
                  pl.BlockSpec((None, D, tf), lambda i, j: (j, 0, 0)), pl.BlockSpec((tf, D), lambda i, j: (j, 0))],
        out_specs=[pl.BlockSpec((tm, D), lambda i, j: (i, 0)), pl.BlockSpec((tm, tf), lambda i, j: (i, j)),
                   pl.BlockSpec((tm, D), lambda i, j: (i, 0))],
        out_shape=[jax.ShapeDtypeStruct((T, D), F32), jax.ShapeDtypeStruct((T, nf * tf), BF16),
                   jax.ShapeDtypeStruct((T, D), BF16)],
        scratch_shapes=[pltpu.VMEM((tm, D), BF16), pltpu.VMEM((tm, D), F32)],
        compiler_params=_params("parallel", "arbitrary"))(h, gain, w1g, w2)


def _mlp_bwd(dout, h, gain, pre, w1g, w2, name):
    T, D = h.shape
    nf, _, tf = w1g.shape
    tm = min(512, T)

    def body(do_ref, h_ref, g_ref, pre_ref, w1_ref, w2_ref, dh_ref, dhb_ref, dpre_ref, act_ref, dg_ref, dy):
        i, j = pl.program_id(0), pl.program_id(1)

        @pl.when(j == 0)
        def _():
            dy[...] = jnp.zeros_like(dy)

        @pl.when((i == 0) & (j == 0))
        def _():
            dg_ref[...] = jnp.zeros_like(dg_ref)

        rp = jnp.maximum(pre_ref[...].astype(F32), 0.0)
        dact = _dg(do_ref[...], w2_ref[...], 1, 1)
        dpre = (dact * (2.0 * rp)).astype(BF16)
        dpre_ref[...] = dpre
        act_ref[...] = (rp * rp).astype(BF16)
        dy[...] += _dg(dpre, w1_ref[...], 1, 1)

        @pl.when(j == nf - 1)
        def _():
            dx, dgain = _rmsnorm_bwd(h_ref[...], g_ref[...], dy[...])
            dh = do_ref[...] + dx
            dh_ref[...] = dh
            dhb_ref[...] = dh.astype(BF16)
            dg_ref[...] += dgain

    return pl.pallas_call(
        body, name=name, grid=(T // tm, nf),
        in_specs=[pl.BlockSpec((tm, D), lambda i, j: (i, 0)), pl.BlockSpec((tm, D), lambda i, j: (i, 0)),
                  pl.BlockSpec((1, D), lambda i, j: (0, 0)), pl.BlockSpec((tm, tf), lambda i, j: (i, j)),
                  pl.BlockSpec((None, D, tf), lambda i, j: (j, 0, 0)), pl.BlockSpec((tf, D), lambda i, j: (j, 0))],
        out_specs=[pl.BlockSpec((tm, D), lambda i, j: (i, 0)), pl.BlockSpec((tm, D), lambda i, j: (i, 0)),
                   pl.BlockSpec((tm, tf), lambda i, j: (i, j)),
                   pl.BlockSpec((tm, tf), lambda i, j: (i, j)), pl.BlockSpec((1, D), lambda i, j: (0, 0))],
        out_shape=[jax.ShapeDtypeStruct((T, D), F32), jax.ShapeDtypeStruct((T, D), BF16),
                   jax.ShapeDtypeStruct((T, nf * tf), BF16),
                   jax.ShapeDtypeStruct((T, nf * tf), BF16), jax.ShapeDtypeStruct((1, D), F32)],
        scratch_shapes=[pltpu.VMEM((tm, D), F32)],
        compiler_params=_params("arbitrary", "arbitrary"))(dout, h, gain, pre, w1g, w2)


def _final_loss(h, gain, target, name):
    T, D = h.shape
    tm = min(512, T)

    def body(h_ref, g_ref, t_ref, l_ref, dh_ref, dhb_ref, dg_ref):
        @pl.when(pl.program_id(0) == 0)
        def _():
            l_ref[...] = jnp.zeros_like(l_ref)
            dg_ref[...] = jnp.zeros_like(dg_ref)

        x = h_ref[...]
        err = x * _rms(x) * g_ref[...] - t_ref[...]
        l_ref[...] += 0.5 * jnp.sum(jnp.mean(err * err, axis=-1, keepdims=True), axis=0, keepdims=True)
        dx, dgain = _rmsnorm_bwd(x, g_ref[...], err * (1.0 / D))
        dh_ref[...] = dx
        dhb_ref[...] = dx.astype(BF16)
        dg_ref[...] += dgain

    return pl.pallas_call(
        body, name=name, grid=(T // tm,),
        in_specs=[pl.BlockSpec((tm, D), lambda i: (i, 0)), pl.BlockSpec((1, D), lambda i: (0, 0)),
                  pl.BlockSpec((tm, D), lambda i: (i, 0))],
        out_specs=[pl.BlockSpec((SUBLANES, LANES), lambda i: (0, 0)), pl.BlockSpec((tm, D), lambda i: (i, 0)),
                   pl.BlockSpec((tm, D), lambda i: (i, 0)), pl.BlockSpec((1, D), lambda i: (0, 0))],
        out_shape=[jax.ShapeDtypeStruct((SUBLANES, LANES), F32), jax.ShapeDtypeStruct((T, D), F32),
                   jax.ShapeDtypeStruct((T, D), BF16), jax.ShapeDtypeStruct((1, D), F32)],
        compiler_params=_params("arbitrary"))(h, gain, target)


def _halo_specs(tm, T, width, col, lead=None):
    r8 = tm // SUBLANES
    nb8 = T // SUBLANES
    if lead is None:
        return [pl.BlockSpec((tm, width), lambda i: (i, col)),
                pl.BlockSpec((SUBLANES, width), lambda i: (jnp.maximum(i * r8 - 1, 0), col)),
                pl.BlockSpec((SUBLANES, width), lambda i: (jnp.minimum((i + 1) * r8, nb8 - 1), col))]
    return [pl.BlockSpec((None, tm, width), lambda i: (lead, i, col)),
            pl.BlockSpec((None, SUBLANES, width), lambda i: (lead, jnp.maximum(i * r8 - 1, 0), col)),
            pl.BlockSpec((None, SUBLANES, width), lambda i: (lead, jnp.minimum((i + 1) * r8, nb8 - 1), col))]


def _ext(cur, prev, nxt, has_prev, has_next):
    return jnp.concatenate([jnp.where(has_prev, prev, 0.0), cur, jnp.where(has_next, nxt, 0.0)], axis=0)


def _shifted(ext, offset, tm):
    n = ext.shape[0]
    sh = (-offset) % n
    r = ext if sh == 0 else pltpu.roll(ext, sh, 0)
    return r[SUBLANES:SUBLANES + tm]


def _rg_conv_fwd(proj, cw8, cb, name):
    T = proj.shape[0]
    tm = min(512, T)
    nT = T // tm

    def body(cur_ref, prev_ref, next_ref, w_ref, b_ref, o_ref):
        i = pl.program_id(0)
        ext = _ext(cur_ref[...], prev_ref[...], next_ref[...], i > 0, i < nT - 1)
        acc = jnp.broadcast_to(b_ref[...], (tm, RG_W))
        for k in range(4):
            acc = acc + w_ref[k:k + 1, :] * _shifted(ext, k - 2, tm)
        o_ref[...] = acc

    return pl.pallas_call(
        body, name=name, grid=(nT,),
        in_specs=_halo_specs(tm, T, RG_W, 0) + [pl.BlockSpec((SUBLANES, RG_W), lambda i: (0, 0)),
                                                pl.BlockSpec((1, RG_W), lambda i: (0, 0))],
        out_specs=pl.BlockSpec((tm, RG_W), lambda i: (i, 0)),
        out_shape=jax.ShapeDtypeStruct((T, RG_W), F32),
        compiler_params=_params("parallel"))(proj, proj, proj, cw8, cb)


def _rg_conv_bwd(dxc, proj, cw8, name):
    T = proj.shape[0]
    tm = min(512, T)
    nT = T // tm

    def body(a0, p0, n0, a1, p1, n1, xa, xp, xn, w_ref, dxa_ref, dw_ref, db_ref):
        i = pl.program_id(0)

        @pl.when(i == 0)
        def _():
            dw_ref[...] = jnp.zeros_like(dw_ref)
            db_ref[...] = jnp.zeros_like(db_ref)

        has_p, has_n = i > 0, i < nT - 1
        cur = a0[...] + a1[...]
        dext = _ext(cur, p0[...] + p1[...], n0[...] + n1[...], has_p, has_n)
        xext = _ext(xa[...], xp[...], xn[...], has_p, has_n)
        acc = jnp.zeros((tm, RG_W), F32)
        rows = []
        for k in range(4):
            acc = acc + w_ref[k:k + 1, :] * _shifted(dext, 2 - k, tm)
            rows.append(jnp.sum(cur * _shifted(xext, k - 2, tm), axis=0, keepdims=True))
        dxa_ref[...] = acc
        dw_ref[...] += jnp.concatenate(rows + [jnp.zeros((4, RG_W), F32)], axis=0)
        db_ref[...] += jnp.sum(cur, axis=0, keepdims=True)

    return pl.pallas_call(
        body, name=name, grid=(nT,),
        in_specs=(_halo_specs(tm, T, RG_W, 0, lead=0) + _halo_specs(tm, T, RG_W, 0, lead=1)
                  + _halo_specs(tm, T, RG_W, 0) + [pl.BlockSpec((SUBLANES, RG_W), lambda i: (0, 0))]),
        out_specs=[pl.BlockSpec((tm, RG_W), lambda i: (i, 0)), pl.BlockSpec((SUBLANES, RG_W), lambda i: (0, 0)),
                   pl.BlockSpec((1, RG_W), lambda i: (0, 0))],
        out_shape=[jax.ShapeDtypeStruct((T, RG_W), F32), jax.ShapeDtypeStruct((SUBLANES, RG_W), F32),
                   jax.ShapeDtypeStruct((1, RG_W), F32)],
        compiler_params=_params("arbitrary"))(dxc, dxc, dxc, dxc, dxc, dxc, proj, proj, proj, cw8)


def _rg_scan_fwd(xc, wbd, bias, lam, name):
    T = xc.shape[0]
    tm = min(512, T)
    nT = T // tm

    def tile(d, i):
        return i + d * (nT - 1 - 2 * i)

    def body(xc_ref, w_ref, b_ref, lam_ref, h_ref, a_sc, u_sc, carry):
        d, i = pl.program_id(0), pl.program_id(1)

        @pl.when(i == 0)
        def _():
            carry[...] = jnp.zeros_like(carry)

        a, u = _rg_gates(xc_ref[...], w_ref[...], b_ref[...], lam_ref[...])
        a_sc[...] = a
        u_sc[...] = u

        def step(t, h):
            tt = t + d * (tm - 1 - 2 * t)
            h = a_sc[pl.ds(tt, 1), :] * h + u_sc[pl.ds(tt, 1), :]
            h_ref[pl.ds(tt, 1), :] = h
            return h

        carry[0:1, :] = lax.fori_loop(0, tm, step, carry[0:1, :])

    return pl.pallas_call(
        body, name=name, grid=(2, nT),
        in_specs=[pl.BlockSpec((tm, RG_W), lambda d, i: (tile(d, i), 0)),
                  pl.BlockSpec((None, RG_W, 2 * RG_W), lambda d, i: (d, 0, 0)),
                  pl.BlockSpec((None, 1, 2 * RG_W), lambda d, i: (d, 0, 0)),
                  pl.BlockSpec((None, 1, RG_W), lambda d, i: (d, 0, 0))],
        out_specs=pl.BlockSpec((None, tm, RG_W), lambda d, i: (d, tile(d, i), 0)),
        out_shape=jax.ShapeDtypeStruct((2, T, RG_W), F32),
        scratch_shapes=[pltpu.VMEM((tm, RG_W), F32), pltpu.VMEM((tm, RG_W), F32), pltpu.VMEM((SUBLANES, RG_W), F32)],
        compiler_params=_params("arbitrary", "arbitrary"))(xc, wbd, bias, lam)


def _rg_scan_bwd(xc, wbd, bias, lam, hs, dho, name):
    T = xc.shape[0]
    tm = min(512, T)
    nT = T // tm
    r8 = tm // SUBLANES
    nb8 = T // SUBLANES

    def tile(d, i):
        return (nT - 1 - i) + d * (2 * i - (nT - 1))

    def body(xc_ref, w_ref, b_ref, lam_ref, hc_ref, hp_ref, hn_ref, dho_ref,
             dxc_ref, dw_ref, db_ref, dlam_ref, a_sc, dt_sc, carry):
        d, i = pl.program_id(0), pl.program_id(1)
        ti = tile(d, i)

        @pl.when(i == 0)
        def _():
            carry[...] = jnp.zeros_like(carry)
            dw_ref[...] = jnp.zeros_like(dw_ref)
            db_ref[...] = jnp.zeros_like(db_ref)
            dlam_ref[...] = jnp.zeros_like(dlam_ref)

        (a, _), vjp = jax.vjp(_rg_gates, xc_ref[...], w_ref[...].astype(F32), b_ref[...], lam_ref[...])
        a_sc[...] = a

        def step(t, c):
            tt = (tm - 1 - t) + d * (2 * t - (tm - 1))
            dt = dho_ref[pl.ds(tt, 1), :] + c
            dt_sc[pl.ds(tt, 1), :] = dt
            return a_sc[pl.ds(tt, 1), :] * dt

        carry[0:1, :] = lax.fori_loop(0, tm, step, carry[0:1, :])
        dtot = dt_sc[...]
        ext = _ext(hc_ref[...], hp_ref[...], hn_ref[...], ti > 0, ti < nT - 1)
        hprev = jnp.where(d == 0, _shifted(ext, -1, tm), _shifted(ext, 1, tm))
        dxc, dw, db, dlam = vjp((dtot * hprev, dtot))
        dxc_ref[...] = dxc
        dw_ref[...] += dw
        db_ref[...] += db
        dlam_ref[...] += dlam

    return pl.pallas_call(
        body, name=name, grid=(2, nT),
        in_specs=[pl.BlockSpec((tm, RG_W), lambda d, i: (tile(d, i), 0)),
                  pl.BlockSpec((None, RG_W, 2 * RG_W), lambda d, i: (d, 0, 0)),
                  pl.BlockSpec((None, 1, 2 * RG_W), lambda d, i: (d, 0, 0)),
                  pl.BlockSpec((None, 1, RG_W), lambda d, i: (d, 0, 0)),
                  pl.BlockSpec((None, tm, RG_W), lambda d, i: (d, tile(d, i), 0)),
                  pl.BlockSpec((None, SUBLANES, RG_W), lambda d, i: (d, jnp.maximum(tile(d, i) * r8 - 1, 0), 0)),
                  pl.BlockSpec((None, SUBLANES, RG_W),
                               lambda d, i: (d, jnp.minimum((tile(d, i) + 1) * r8, nb8 - 1), 0)),
                  pl.BlockSpec((tm, RG_W), lambda d, i: (tile(d, i), 0))],
        out_specs=[pl.BlockSpec((None, tm, RG_W), lambda d, i: (d, tile(d, i), 0)),
                   pl.BlockSpec((None, RG_W, 2 * RG_W), lambda d, i: (d, 0, 0)),
                   pl.BlockSpec((None, 1, 2 * RG_W), lambda d, i: (d, 0, 0)),
                   pl.BlockSpec((None, 1, RG_W), lambda d, i: (d, 0, 0))],
        out_shape=[jax.ShapeDtypeStruct((2, T, RG_W), F32), jax.ShapeDtypeStruct((2, RG_W, 2 * RG_W), F32),
                   jax.ShapeDtypeStruct((2, 1, 2 * RG_W), F32), jax.ShapeDtypeStruct((2, 1, RG_W), F32)],
        scratch_shapes=[pltpu.VMEM((tm, RG_W), F32), pltpu.VMEM((tm, RG_W), F32), pltpu.VMEM((SUBLANES, RG_W), F32)],
        compiler_params=_params("arbitrary", "arbitrary"))(xc, wbd, bias, lam, hs, hs, hs, dho)


def _chunk_idx(d, c, n_chunks):
    return c + d * (n_chunks - 1 - 2 * c)


def _chunk_idx_rev(d, c, n_chunks):
    return (n_chunks - 1 - c) + d * (2 * c - (n_chunks - 1))


def _hg_fwd(proj, l0, l1, name):
    T = proj.shape[0]
    nC = T // CHUNK
    H, dk, dv = 4, 128, 128

    def body(q_ref, f_ref, v_ref, l0_ref, l1_ref, o_ref, sp_ref, st):
        d, c = pl.program_id(0), pl.program_id(1)

        @pl.when(c == 0)
        def _():
            st[...] = jnp.zeros_like(st)

        tri, tri_t, mref = _tri_consts(d)
        stp = tuple(st[h] for h in range(H))
        sp_ref[...] = st[...]
        o, stn = _hg_chunk(q_ref[...], f_ref[...], v_ref[...], l0_ref[...], l1_ref[...], stp, tri, tri_t, mref)
        o_ref[...] = o
        for h in range(H):
            st[h] = stn[h]

    row = lambda d, c: _chunk_idx(d, c, nC)
    return pl.pallas_call(
        body, name=name, grid=(2, nC),
        in_specs=[pl.BlockSpec((CHUNK, HG_W), lambda d, c: (row(d, c), 2)),
                  pl.BlockSpec((CHUNK, HG_W), lambda d, c: (row(d, c), 3 + d)),
                  pl.BlockSpec((CHUNK, HG_W), lambda d, c: (row(d, c), 5)),
                  pl.BlockSpec((1, HG_W), lambda d, c: (0, 0)), pl.BlockSpec((1, HG_W), lambda d, c: (0, 0))],
        out_specs=[pl.BlockSpec((None, CHUNK, H * dv), lambda d, c: (d, row(d, c), 0)),
                   pl.BlockSpec((None, None, H, dv, dk), lambda d, c: (d, row(d, c), 0, 0, 0))],
        out_shape=[jax.ShapeDtypeStruct((2, T, H * dv), F32), jax.ShapeDtypeStruct((2, nC, H, dv, dk), F32)],
        scratch_shapes=[pltpu.VMEM((H, dv, dk), F32)],
        compiler_params=_params("arbitrary", "arbitrary"))(proj, proj, proj, l0, l1)


def _hg_bwd(proj, l0, l1, sprev, do, name):
    T = proj.shape[0]
    nC = T // CHUNK
    H, dk, dv = 4, 128, 128

    def body(q_ref, f_ref, v_ref, l0_ref, l1_ref, sp_ref, do_ref, dq_ref, df_ref, dv_ref, dl0_ref, dl1_ref, dst):
        d, c = pl.program_id(0), pl.program_id(1)

        @pl.when(c == 0)
        def _():
            dst[...] = jnp.zeros_like(dst)
            dl0_ref[...] = jnp.zeros_like(dl0_ref)
            dl1_ref[...] = jnp.zeros_like(dl1_ref)

        tri, tri_t, mref = _tri_consts(d)
        fn = lambda q, f, v, a0, a1, stp: _hg_chunk(q, f, v, a0, a1, stp, tri, tri_t, mref)
        stp = tuple(sp_ref[h] for h in range(H))
        _, vjp = jax.vjp(fn, q_ref[...], f_ref[...], v_ref[...], l0_ref[...], l1_ref[...], stp)
        dq, df, dvv, dl0, dl1, dstp = vjp((do_ref[...], tuple(dst[h] for h in range(H))))
        dq_ref[...] = dq
        df_ref[...] = df
        dv_ref[...] = dvv
        dl0_ref[...] += dl0
        dl1_ref[...] += dl1
        for h in range(H):
            dst[h] = dstp[h]

    row = lambda d, c: _chunk_idx_rev(d, c, nC)
    tok = lambda: pl.BlockSpec((None, CHUNK, HG_W), lambda d, c: (d, row(d, c), 0))
    par = lambda: pl.BlockSpec((None, 1, HG_W), lambda d, c: (d, 0, 0))
    return pl.pallas_call(
        body, name=name, grid=(2, nC),
        in_specs=[pl.BlockSpec((CHUNK, HG_W), lambda d, c: (row(d, c), 2)),
                  pl.BlockSpec((CHUNK, HG_W), lambda d, c: (row(d, c), 3 + d)),
                  pl.BlockSpec((CHUNK, HG_W), lambda d, c: (row(d, c), 5)),
                  pl.BlockSpec((1, HG_W), lambda d, c: (0, 0)), pl.BlockSpec((1, HG_W), lambda d, c: (0, 0)),
                  pl.BlockSpec((None, None, H, dv, dk), lambda d, c: (d, row(d, c), 0, 0, 0)),
                  pl.BlockSpec((CHUNK, H * dv), lambda d, c: (row(d, c), 0))],
        out_specs=[tok(), tok(), tok(), par(), par()],
        out_shape=[jax.ShapeDtypeStruct((2, T, HG_W), F32)] * 3 + [jax.ShapeDtypeStruct((2, 1, HG_W), F32)] * 2,
        scratch_shapes=[pltpu.VMEM((H, dv, dk), F32)],
        compiler_params=_params("arbitrary", "arbitrary"))(proj, proj, proj, l0, l1, sprev, do)


def _gate_logits(proj, wup, bg, name):
    T = proj.shape[0]
    tm = min(512, T)

    def body(lr_ref, w_ref, b_ref, z_ref, lrb_ref):
        lr = lr_ref[...].astype(BF16)
        lrb_ref[...] = lr
        for d in range(2):
            z_ref[d] = _dg(lr, w_ref[d], 1, 0) + b_ref[d]

    return pl.pallas_call(
        body, name=name, grid=(T // tm,),
        in_specs=[pl.BlockSpec((tm, LANES), lambda i: (i, 24)), pl.BlockSpec((2, LANES, 512), lambda i: (0, 0, 0)),
                  pl.BlockSpec((2, 1, 512), lambda i: (0, 0, 0))],
        out_specs=[pl.BlockSpec((2, tm, 512), lambda i: (0, i, 0)), pl.BlockSpec((tm, LANES), lambda i: (i, 0))],
        out_shape=[jax.ShapeDtypeStruct((2, T, 512), F32), jax.ShapeDtypeStruct((T, LANES), BF16)],
        compiler_params=_params("parallel"))(proj, wup, bg)


def _gate_logits_bwd(dz, wup, name):
    T = dz.shape[1]
    tm = min(512, T)

    def body(dz_ref, w_ref, dlr_ref, db_ref, dzb_ref):
        @pl.when(pl.program_id(0) == 0)
        def _():
            db_ref[...] = jnp.zeros_like(db_ref)

        acc = jnp.zeros((tm, LANES), F32)
        for d in range(2):
            g = dz_ref[d]
            gb = g.astype(BF16)
            dzb_ref[d] = gb
            acc = acc + _dg(gb, w_ref[d], 1, 1)
            db_ref[d] += jnp.sum(g, axis=0, keepdims=True)
        dlr_ref[...] = acc

    return pl.pallas_call(
        body, name=name, grid=(T // tm,),
        in_specs=[pl.BlockSpec((2, tm, 512), lambda i: (0, i, 0)), pl.BlockSpec((2, LANES, 512), lambda i: (0, 0, 0))],
        out_specs=[pl.BlockSpec((tm, LANES), lambda i: (i, 0)), pl.BlockSpec((2, 1, 512), lambda i: (0, 0, 0)),
                   pl.BlockSpec((2, tm, 512), lambda i: (0, i, 0))],
        out_shape=[jax.ShapeDtypeStruct((T, LANES), F32), jax.ShapeDtypeStruct((2, 1, 512), F32),
                   jax.ShapeDtypeStruct((2, T, 512), BF16)],
        compiler_params=_params("arbitrary"))(dz, wup)


def _gla_fwd(proj, z, name):
    T = proj.shape[0]
    nC = T // CHUNK
    H, dk, dv = 4, 128, 256

    def body(q_ref, k_ref, v_ref, z_ref, o_ref, sp_ref, st):
        d, c = pl.program_id(0), pl.program_id(1)

        @pl.when(c == 0)
        def _():
            st[...] = jnp.zeros_like(st)

        tri, tri_t, mref = _tri_consts(d)
        stp = tuple(st[h] for h in range(H))
        sp_ref[...] = st[...]
        o, stn = _gla_chunk(q_ref[...], k_ref[...], v_ref[...], z_ref[...], stp, tri, tri_t, mref)
        o_ref[...] = o
        for h in range(H):
            st[h] = stn[h]

    row = lambda d, c: _chunk_idx(d, c, nC)
    return pl.pallas_call(
        body, name=name, grid=(2, nC),
        in_specs=[pl.BlockSpec((CHUNK, 512), lambda d, c: (row(d, c), 0)),
                  pl.BlockSpec((CHUNK, 512), lambda d, c: (row(d, c), 1)),
                  pl.BlockSpec((CHUNK, 1024), lambda d, c: (row(d, c), 1)),
                  pl.BlockSpec((None, CHUNK, 512), lambda d, c: (d, row(d, c), 0))],
        out_specs=[pl.BlockSpec((None, CHUNK, H * dv), lambda d, c: (d, row(d, c), 0)),
                   pl.BlockSpec((None, None, H, dv, dk), lambda d, c: (d, row(d, c), 0, 0, 0))],
        out_shape=[jax.ShapeDtypeStruct((2, T, H * dv), F32), jax.ShapeDtypeStruct((2, nC, H, dv, dk), F32)],
        scratch_shapes=[pltpu.VMEM((H, dv, dk), F32)],
        compiler_params=_params("arbitrary", "arbitrary"))(proj, proj, proj, z)


def _gla_bwd(proj, z, sprev, do, name):
    T = proj.shape[0]
    nC = T // CHUNK
    H, dk, dv = 4, 128, 256

    def body(q_ref, k_ref, v_ref, z_ref, sp_ref, do_ref, dq_ref, dk_ref, dv_ref, dz_ref, dst):
        d, c = pl.program_id(0), pl.program_id(1)

        @pl.when(c == 0)
        def _():
            dst[...] = jnp.zeros_like(dst)

        tri, tri_t, mref = _tri_consts(d)
        fn = lambda q, k, v, zz, stp: _gla_chunk(q, k, v, zz, stp, tri, tri_t, mref)
        stp = tuple(sp_ref[h] for h in range(H))
        _, vjp = jax.vjp(fn, q_ref[...], k_ref[...], v_ref[...], z_ref[...], stp)
        dq, dkk, dvv, dzz, dstp = vjp((do_ref[...], tuple(dst[h] for h in range(H))))
        dq_ref[...] = dq
        dk_ref[...] = dkk
        dv_ref[...] = dvv
        dz_ref[...] = dzz
        for h in range(H):
            dst[h] = dstp[h]

    row = lambda d, c: _chunk_idx_rev(d, c, nC)
    tok = lambda w: pl.BlockSpec((None, CHUNK, w), lambda d, c: (d, row(d, c), 0))
    return pl.pallas_call(
        body, name=name, grid=(2, nC),
        in_specs=[pl.BlockSpec((CHUNK, 512), lambda d, c: (row(d, c), 0)),
                  pl.BlockSpec((CHUNK, 512), lambda d, c: (row(d, c), 1)),
                  pl.BlockSpec((CHUNK, 1024), lambda d, c: (row(d, c), 1)),
                  tok(512),
                  pl.BlockSpec((None, None, H, dv, dk), lambda d, c: (d, row(d, c), 0, 0, 0)),
                  pl.BlockSpec((CHUNK, H * dv), lambda d, c: (row(d, c), 0))],
        out_specs=[tok(512), tok(512), tok(1024), tok(512)],
        out_shape=[jax.ShapeDtypeStruct((2, T, 512), F32), jax.ShapeDtypeStruct((2, T, 512), F32),
                   jax.ShapeDtypeStruct((2, T, 1024), F32), jax.ShapeDtypeStruct((2, T, 512), F32)],
        scratch_shapes=[pltpu.VMEM((H, dv, dk), F32)],
        compiler_params=_params("arbitrary", "arbitrary"))(proj, proj, proj, z, sprev, do)


def _l0_combine_fwd(hs, proj, o, gain, name):
    T = proj.shape[0]
    tm = min(512, T)

    def body(hf, hb, ga, of, ob, g, gn, out):
        out[...] = _l0_combine(hf[...], hb[...], ga[...], of[...], ob[...], g[...], gn[...]).astype(BF16)

    two = lambda lead: pl.BlockSpec((None, tm, 512), lambda i: (lead, i, 0))
    return pl.pallas_call(
        body, name=name, grid=(T // tm,),
        in_specs=[two(0), two(1), pl.BlockSpec((tm, 512), lambda i: (i, 1)), two(0), two(1),
                  pl.BlockSpec((tm, 512), lambda i: (i, 6)), pl.BlockSpec((1, 512), lambda i: (0, 0))],
        out_specs=pl.BlockSpec((tm, 1024), lambda i: (i, 0)),
        out_shape=jax.ShapeDtypeStruct((T, 1024), BF16),
        compiler_params=_params("parallel"))(hs, hs, proj, o, o, proj, gain)


def _l0_combine_bwd(hs, proj, o, gain, dmix, name):
    T = proj.shape[0]
    tm = min(512, T)

    def body(hf, hb, ga, of, ob, g, gn, dm, dho_ref, dga_ref, do_ref, dg_ref, dgn_ref):
        @pl.when(pl.program_id(0) == 0)
        def _():
            dgn_ref[...] = jnp.zeros_like(dgn_ref)

        _, vjp = jax.vjp(_l0_combine, hf[...], hb[...], ga[...], of[...], ob[...], g[...], gn[...])
        dhf, _, dga, dof, _, dg, dgn = vjp(dm[...])
        dho_ref[...] = dhf
        dga_ref[...] = dga
        do_ref[...] = dof
        dg_ref[...] = dg
        dgn_ref[...] += dgn

    two = lambda lead: pl.BlockSpec((None, tm, 512), lambda i: (lead, i, 0))
    tok = lambda: pl.BlockSpec((tm, 512), lambda i: (i, 0))
    return pl.pallas_call(
        body, name=name, grid=(T // tm,),
        in_specs=[two(0), two(1), pl.BlockSpec((tm, 512), lambda i: (i, 1)), two(0), two(1),
                  pl.BlockSpec((tm, 512), lambda i: (i, 6)), pl.BlockSpec((1, 512), lambda i: (0, 0)),
                  pl.BlockSpec((tm, 1024), lambda i: (i, 0))],
        out_specs=[tok(), tok(), tok(), tok(), pl.BlockSpec((1, 512), lambda i: (0, 0))],
        out_shape=[jax.ShapeDtypeStruct((T, 512), F32)] * 4 + [jax.ShapeDtypeStruct((1, 512), F32)],
        compiler_params=_params("arbitrary"))(hs, hs, proj, o, o, proj, gain, dmix)


def _l0_assemble(dxa, dga, dq, df, dv, dg, name):
    T = dxa.shape[0]
    tm = min(512, T)

    def body(xa, ga, q0, q1, f0, f1, v0, v1, g, out):
        out[...] = jnp.concatenate([xa[...], ga[...], q0[...] + q1[...], f0[...], f1[...], v0[...] + v1[...],
                                    g[...]], axis=1).astype(BF16)

    two = lambda lead: pl.BlockSpec((None, tm, 512), lambda i: (lead, i, 0))
    tok = lambda: pl.BlockSpec((tm, 512), lambda i: (i, 0))
    return pl.pallas_call(
        body, name=name, grid=(T // tm,),
        in_specs=[tok(), tok(), two(0), two(1), two(0), two(1), two(0), two(1), tok()],
        out_specs=pl.BlockSpec((tm, AB_IN), lambda i: (i, 0)),
        out_shape=jax.ShapeDtypeStruct((T, AB_IN), BF16),
        compiler_params=_params("parallel"))(dxa, dga, dq, dq, df, df, dv, dv, dg)


def _l1_combine_fwd(o, proj, gain, name):
    T = proj.shape[0]
    tm = min(512, T)

    def body(of, ob, r, gn, out):
        out[...] = _l1_combine(of[...], ob[...], r[...], gn[...]).astype(BF16)

    two = lambda lead: pl.BlockSpec((None, tm, 1024), lambda i: (lead, i, 0))
    return pl.pallas_call(
        body, name=name, grid=(T // tm,),
        in_specs=[two(0), two(1), pl.BlockSpec((tm, 1024), lambda i: (i, 2)), pl.BlockSpec((1, 1024), lambda i: (0, 0))],
        out_specs=pl.BlockSpec((tm, 1024), lambda i: (i, 0)),
        out_shape=jax.ShapeDtypeStruct((T, 1024), BF16),
        compiler_params=_params("parallel"))(o, o, proj, gain)


def _l1_combine_bwd(o, proj, gain, dmix, name):
    T = proj.shape[0]
    tm = min(512, T)

    def body(of, ob, r, gn, dm, do_ref, dr_ref, dgn_ref):
        @pl.when(pl.program_id(0) == 0)
        def _():
            dgn_ref[...] = jnp.zeros_like(dgn_ref)

        _, vjp = jax.vjp(_l1_combine, of[...], ob[...], r[...], gn[...])
        dof, _, dr, dgn = vjp(dm[...])
        do_ref[...] = dof
        dr_ref[...] = dr
        dgn_ref[...] += dgn

    two = lambda lead: pl.BlockSpec((None, tm, 1024), lambda i: (lead, i, 0))
    tok = lambda: pl.BlockSpec((tm, 1024), lambda i: (i, 0))
    return pl.pallas_call(
        body, name=name, grid=(T // tm,),
        in_specs=[two(0), two(1), pl.BlockSpec((tm, 1024), lambda i: (i, 2)),
                  pl.BlockSpec((1, 1024), lambda i: (0, 0)), tok()],
        out_specs=[tok(), tok(), pl.BlockSpec((1, 1024), lambda i: (0, 0))],
        out_shape=[jax.ShapeDtypeStruct((T, 1024), F32)] * 2 + [jax.ShapeDtypeStruct((1, 1024), F32)],
        compiler_params=_params("arbitrary"))(o, o, proj, gain, dmix)


def _l1_assemble(dq, dk, dv, dr, dlr, name):
    T = dr.shape[0]
    tm = min(512, T)

    def body(q0, q1, k0, k1, v0, v1, r, a, out):
        out[...] = jnp.concatenate([q0[...] + q1[...], k0[...] + k1[...], v0[...] + v1[...], r[...], a[...]],
                                   axis=1).astype(BF16)

    two = lambda lead, w: pl.BlockSpec((None, tm, w), lambda i: (lead, i, 0))
    return pl.pallas_call(
        body, name=name, grid=(T // tm,),
        in_specs=[two(0, 512), two(1, 512), two(0, 512), two(1, 512), two(0, 1024), two(1, 1024),
                  pl.BlockSpec((tm, 1024), lambda i: (i, 0)), pl.BlockSpec((tm, LANES), lambda i: (i, 0))],
        out_specs=pl.BlockSpec((tm, GLA_IN_PAD), lambda i: (i, 0)),
        out_shape=jax.ShapeDtypeStruct((T, GLA_IN_PAD), BF16),
        compiler_params=_params("parallel"))(dq, dq, dk, dk, dv, dv, dr, dlr)


HBM_SPEC = pl.BlockSpec(memory_space=pltpu.HBM)


def _place():
    x, y, c = lax.axis_index("x"), lax.axis_index("y"), lax.axis_index("c")
    return x, y, c


def _allgather_hbm(shards, name):
    n = len(shards)

    def body(*refs):
        ins, outs = refs[:n], refs[n:2 * n]
        send_sems, recv_sems, local_sems = refs[2 * n:]
        x, y, c = _place()
        me, sibling = (x, y, c), (x, y, 1 - c)
        chips = [(1 - x, y), (x, 1 - y), (1 - x, 1 - y)]

        def slot(a, p):
            return outs[a].at[4 * p[0] + 2 * p[1] + p[2]]

        def copy(a, k, block, to, src=None):
            return pltpu.make_async_remote_copy(
                src_ref=slot(a, block) if src is None else src, dst_ref=slot(a, block),
                send_sem=send_sems.at[a * 7 + k], recv_sem=recv_sems.at[a * 7 + k],
                device_id=to, device_id_type=MESH)

        mine = [pltpu.make_async_copy(ins[a], slot(a, me), local_sems.at[a]) for a in range(n)]
        for cp in mine:
            cp.start()
        first = []
        for a in range(n):
            first.append(copy(a, 0, me, sibling, src=ins[a]))
            first += [copy(a, 1 + j, me, (*chip, c), src=ins[a]) for j, chip in enumerate(chips)]
        for cp in first:
            cp.start()
        passed = []
        for j, chip in enumerate(chips):
            for a in range(n):
                copy(a, 1 + j, (*chip, c), me).wait_recv()
                cp = copy(a, 4 + j, (*chip, c), sibling)
                cp.start()
                passed.append(cp)
        for a in range(n):
            copy(a, 0, sibling, me).wait_recv()
            for j, chip in enumerate(chips):
                copy(a, 4 + j, (*chip, 1 - c), me).wait_recv()
        for cp in first + passed:
            cp.wait_send()
        for cp in mine:
            cp.wait()

    return pl.pallas_call(
        body, name=name,
        in_specs=[HBM_SPEC] * n, out_specs=[HBM_SPEC] * n,
        out_shape=[jax.ShapeDtypeStruct((N_DEV,) + s.shape, s.dtype) for s in shards],
        scratch_shapes=[pltpu.SemaphoreType.DMA((7 * n,)), pltpu.SemaphoreType.DMA((7 * n,)),
                        pltpu.SemaphoreType.DMA((n,))],
        compiler_params=pltpu.CompilerParams(has_side_effects=True))(*shards)


def _allgather_vmem(x_shard, name, reduce=False):
    m_per, n = x_shard.shape

    def body(x_ref, out_ref, *rest):
        if reduce:
            sum_ref, send_sems, recv_sems, local_sem = rest
        else:
            send_sems, recv_sems, local_sem = rest
        x, y, c = _place()
        me, sibling = (x, y, c), (x, y, 1 - c)
        chips = [(1 - x, y), (x, 1 - y), (1 - x, 1 - y)]

        def rows(px, py, pc):
            return out_ref.at[pl.ds((4 * px + 2 * py + pc) * m_per, m_per), :]

        def copy(k, block, to, src=None):
            return pltpu.make_async_remote_copy(
                src_ref=rows(*block) if src is None else src, dst_ref=rows(*block),
                send_sem=send_sems.at[k], recv_sem=recv_sems.at[k], device_id=to, device_id_type=MESH)

        mine = pltpu.make_async_copy(x_ref, rows(*me), local_sem)
        mine.start()
        first = [copy(0, me, sibling, src=x_ref)]
        first += [copy(1 + j, me, (*chip, c), src=x_ref) for j, chip in enumerate(chips)]
        for cp in first:
            cp.start()
        passed = [copy(4 + j, (*chip, c), sibling) for j, chip in enumerate(chips)]
        for j, chip in enumerate(chips):
            copy(1 + j, (*chip, c), me).wait_recv()
            passed[j].start()
        copy(0, sibling, me).wait_recv()
        for j, chip in enumerate(chips):
            copy(4 + j, (*chip, 1 - c), me).wait_recv()
        for cp in first + passed:
            cp.wait_send()
        mine.wait()
        if reduce:
            acc = out_ref[pl.ds(0, m_per), :]
            for j in range(1, N_DEV):
                acc = acc + out_ref[pl.ds(j * m_per, m_per), :]
            sum_ref[...] = acc

    vm = pl.BlockSpec(memory_space=pltpu.VMEM)
    out_shape = [jax.ShapeDtypeStruct((N_DEV * m_per, n), x_shard.dtype)]
    if reduce:
        out_shape.append(jax.ShapeDtypeStruct((m_per, n), x_shard.dtype))
    res = pl.pallas_call(
        body, name=name, in_specs=[vm], out_specs=[vm] * len(out_shape), out_shape=out_shape,
        scratch_shapes=[pltpu.SemaphoreType.DMA((7,)), pltpu.SemaphoreType.DMA((7,)), pltpu.SemaphoreType.DMA],
        compiler_params=pltpu.CompilerParams(has_side_effects=True, vmem_limit_bytes=VMEM_LIMIT))(x_shard)
    return res[1] if reduce else res[0]


SEM_SPEC = pl.BlockSpec(memory_space=pltpu.SEMAPHORE)
DATAFLOW_EFFECT = pltpu.SideEffectType.DATAFLOW_SIDE_EFFECTING


def _copies(plan, srcs, lands, send_sems, recv_sems):
    x, y, c = _place()
    return [pltpu.make_async_remote_copy(src_ref=s, dst_ref=d, send_sem=send_sems.at[k], recv_sem=recv_sems.at[k],
                                         device_id=dev, device_id_type=MESH)
            for k, (s, d, dev) in enumerate(plan(srcs, lands, x, y, c))]


def _copies_start(plan, n_copies, srcs, lands, name):
    ns, nl = len(srcs), len(lands)

    def body(*refs):
        send_sems, recv_sems = refs[ns + nl], refs[ns + nl + 1]
        for cp in _copies(plan, refs[:ns], refs[ns:ns + nl], send_sems, recv_sems):
            cp.start()
        refs[-1][...] = jnp.zeros_like(refs[-1])

    arrays = list(srcs) + list(lands)
    res = pl.pallas_call(
        body, name=name,
        in_specs=[HBM_SPEC] * (ns + nl),
        out_specs=tuple([SEM_SPEC, SEM_SPEC] + [HBM_SPEC] * (ns + nl) + [pl.BlockSpec(memory_space=pltpu.VMEM)]),
        out_shape=tuple([pltpu.SemaphoreType.DMA((n_copies,)), pltpu.SemaphoreType.DMA((n_copies,))]
                        + [pltpu.HBM(a.shape, a.dtype) for a in arrays]
                        + [jax.ShapeDtypeStruct((SUBLANES, LANES), F32)]),
        input_output_aliases={i: 2 + i for i in range(ns + nl)},
        compiler_params=pltpu.CompilerParams(has_side_effects=DATAFLOW_EFFECT),
    )(*[pltpu.with_memory_space_constraint(a, pltpu.HBM) for a in arrays])
    return res[0], res[1], list(res[2:2 + ns]), list(res[2 + ns:2 + ns + nl]), res[-1]


def _copies_wait(plan, started, after, name):
    send_sems, recv_sems, srcs, lands, _ = started
    ns, nl = len(srcs), len(lands)

    def body(*refs):
        for cp in _copies(plan, refs[:ns], refs[ns:ns + nl], refs[ns + nl], refs[ns + nl + 1]):
            cp.wait_send()
            cp.wait_recv()

    arrays = list(srcs) + list(lands)
    res = pl.pallas_call(
        body, name=name,
        in_specs=[HBM_SPEC] * (ns + nl) + [SEM_SPEC, SEM_SPEC, pl.BlockSpec(memory_space=pl.ANY)],
        out_specs=tuple([HBM_SPEC] * (ns + nl)),
        out_shape=tuple(pltpu.HBM(a.shape, a.dtype) for a in arrays),
        input_output_aliases={i: i for i in range(ns + nl)},
        compiler_params=pltpu.CompilerParams(has_side_effects=DATAFLOW_EFFECT),
    )(*arrays, send_sems, recv_sems, after)
    return list(res[:ns]), list(res[ns:])


def _after(token, value):
    return value + token[0:1, 0:1].astype(value.dtype)


def _chips(x, y):
    return [(1 - x, y), (x, 1 - y), (1 - x, 1 - y)]


def _plan_gather_first(srcs, lands, x, y, c):
    me = 4 * x + 2 * y + c
    out = []
    for s, l in zip(srcs, lands):
        out.append((s, l.at[me], (x, y, 1 - c)))
        out += [(s, l.at[me], (*chip, c)) for chip in _chips(x, y)]
    return out


def _plan_gather_pass(srcs, lands, x, y, c):
    out = []
    for l in lands:
        for chip in _chips(x, y):
            slot = l.at[4 * chip[0] + 2 * chip[1] + c]
            out.append((slot, slot, (x, y, 1 - c)))
    return out


def _plan_grads_sibling(srcs, lands, x, y, c):
    return [(s.at[2 * q + (1 - c)], l.at[q], (x, y, 1 - c)) for s, l in zip(srcs, lands) for q in range(4)]


def _plan_grads_chips(srcs, lands, x, y, c):
    return [(s.at[2 * chip[0] + chip[1]], l.at[k], (*chip, c))
            for s, l in zip(srcs, lands) for k, chip in enumerate(_chips(x, y))]


def _landing(n_slots, like):
    return [lax.empty((n_slots,) + a.shape[1:], a.dtype) for a in like]


def _chip_partial(g, r1, place, name):
    _, R, C = g.shape
    tr = min(256, R)
    assert R % tr == 0

    def body(pl_ref, g_ref, r_ref, pb_ref, pm_ref):
        q = pl.program_id(1)
        s = g_ref[...] + r_ref[...]
        pb_ref[...] = s.astype(BF16)

        @pl.when(q == pl_ref[1])
        def _():
            pm_ref[...] = s

    grid_spec = pltpu.PrefetchScalarGridSpec(
        num_scalar_prefetch=1, grid=(R // tr, 4),
        in_specs=[pl.BlockSpec((None, tr, C), lambda r, q, p: (2 * q + p[0], r, 0)),
                  pl.BlockSpec((None, tr, C), lambda r, q, p: (q, r, 0))],
        out_specs=[pl.BlockSpec((None, tr, C), lambda r, q, p: (q, r, 0)),
                   pl.BlockSpec((tr, C), lambda r, q, p: (r, 0))])
    return pl.pallas_call(
        body, name=name, grid_spec=grid_spec,
        out_shape=[jax.ShapeDtypeStruct((4, R, C), BF16), jax.ShapeDtypeStruct((R, C), F32)],
        compiler_params=_params("parallel", "arbitrary"))(place, g, r1)


def _adamw(w, gparts, m, v, name):
    R, C = w.shape
    tr = min(256, R)
    assert R % tr == 0
    g0, g3 = gparts

    def body(w_ref, g0_ref, *rest):
        if g3 is not None:
            g3_ref, m_ref, v_ref, go, do, mo, vo = rest
        else:
            m_ref, v_ref, go, do, mo, vo = rest
        g = g0_ref[...]
        if g3 is not None:
            for k in range(3):
                g = g + g3_ref[k].astype(F32)
        wv = w_ref[...]
        mn = ADAM_B1 * m_ref[...] + (1.0 - ADAM_B1) * g
        vn = ADAM_B2 * v_ref[...] + (1.0 - ADAM_B2) * jnp.square(g)
        m_hat = mn / (1.0 - ADAM_B1 ** ADAM_STEP)
        v_hat = vn / (1.0 - ADAM_B2 ** ADAM_STEP)
        go[...] = g
        do[...] = -ADAM_LR * (m_hat / (jnp.sqrt(v_hat) + ADAM_EPS) + ADAM_WD * wv)
        mo[...] = mn
        vo[...] = vn

    blk = pl.BlockSpec((tr, C), lambda i: (i, 0))
    in_specs = [blk, blk] + ([pl.BlockSpec((3, tr, C), lambda i: (0, i, 0))] if g3 is not None else []) + [blk, blk]
    args = [w, g0] + ([g3] if g3 is not None else []) + [m, v]
    return pl.pallas_call(
        body, name=name, grid=(R // tr,), in_specs=in_specs, out_specs=[blk] * 4,
        out_shape=[jax.ShapeDtypeStruct((R, C), F32)] * 4,
        compiler_params=_params("parallel"))(*args)


SMALL_SHARDED = ("rg_conv_w", "rg_b_a", "rg_b_x", "rg_lambda", "gla_w_gate_up", "gla_b_gate", "gla_norm")
SMALL_REPLICATED = ("norm_mix", "norm_mlp", "norm_final", "rg_conv_b", "rg_w_a", "rg_w_x", "hg_lb_logits", "hg_norm")
WEIGHT_NAMES = ("norm_mix", "norm_mlp", "norm_final", "mlp_w1", "mlp_w2", "ab_w_in", "ab_w_out", "rg_conv_w",
                "rg_conv_b", "rg_w_a", "rg_b_a", "rg_w_x", "rg_b_x", "rg_lambda", "hg_lb_logits", "hg_norm",
                "gla_w_in", "gla_w_out", "gla_w_gate_up", "gla_b_gate", "gla_norm")


def _rows128(a):
    return a.reshape(-1, LANES)


def _part_rows(a):
    return -(-(a.size // LANES) // SUBLANES) * SUBLANES


def _pack_rows(arrays, pad_to=SUBLANES):
    parts = [jnp.pad(_rows128(a), ((0, _part_rows(a) - a.size // LANES), (0, 0))) for a in arrays]
    total = sum(p.shape[0] for p in parts)
    extra = (-total) % pad_to
    if extra:
        parts.append(jnp.zeros((extra, LANES), parts[0].dtype))
    return jnp.concatenate(parts, axis=0)


def _unshard_last(g, shape_local):
    nd = len(shape_local)
    t = g.reshape((N_DEV,) + tuple(shape_local))
    t = jnp.moveaxis(t, 0, nd - 1)
    return t.reshape(tuple(shape_local[:-1]) + (N_DEV * shape_local[-1],))


def _block_diag(w):
    eye = jnp.eye(8, dtype=w.dtype)
    return (w[:, :, :, None, :] * eye[None, :, None, :, None]).reshape(2, RG_W, RG_W)


def _block_diag_extract(dw):
    t = dw.reshape(2, 8, 64, 8, 64)
    return jnp.moveaxis(jnp.diagonal(t, axis1=1, axis2=3), -1, 1)


def kernel(x, norm_mix, norm_mlp, norm_final, mlp_w1, mlp_w2, ab_w_in, ab_w_out, rg_conv_w, rg_conv_b, rg_w_a, rg_b_a, rg_w_x, rg_b_x, rg_lambda, hg_lb_logits, hg_norm, gla_w_in, gla_w_out, gla_w_gate_up, gla_b_gate, gla_norm, loss_target, m_norm_mix, m_norm_mlp, m_norm_final, m_mlp_w1, m_mlp_w2, m_ab_w_in, m_ab_w_out, m_rg_conv_w, m_rg_conv_b, m_rg_w_a, m_rg_b_a, m_rg_w_x, m_rg_b_x, m_rg_lambda, m_hg_lb_logits, m_hg_norm, m_gla_w_in, m_gla_w_out, m_gla_w_gate_up, m_gla_b_gate, m_gla_norm, v_norm_mix, v_norm_mlp, v_norm_final, v_mlp_w1, v_mlp_w2, v_ab_w_in, v_ab_w_out, v_rg_conv_w, v_rg_conv_b, v_rg_w_a, v_rg_b_a, v_rg_w_x, v_rg_b_x, v_rg_lambda, v_hg_lb_logits, v_hg_norm, v_gla_w_in, v_gla_w_out, v_gla_w_gate_up, v_gla_b_gate, v_gla_norm):
    w_loc = dict(norm_mix=norm_mix, norm_mlp=norm_mlp, norm_final=norm_final, mlp_w1=mlp_w1, mlp_w2=mlp_w2,
                 ab_w_in=ab_w_in, ab_w_out=ab_w_out, rg_conv_w=rg_conv_w, rg_conv_b=rg_conv_b, rg_w_a=rg_w_a,
                 rg_b_a=rg_b_a, rg_w_x=rg_w_x, rg_b_x=rg_b_x, rg_lambda=rg_lambda, hg_lb_logits=hg_lb_logits,
                 hg_norm=hg_norm, gla_w_in=gla_w_in, gla_w_out=gla_w_out, gla_w_gate_up=gla_w_gate_up,
                 gla_b_gate=gla_b_gate, gla_norm=gla_norm)
    m_loc = dict(norm_mix=m_norm_mix, norm_mlp=m_norm_mlp, norm_final=m_norm_final, mlp_w1=m_mlp_w1,
                 mlp_w2=m_mlp_w2, ab_w_in=m_ab_w_in, ab_w_out=m_ab_w_out, rg_conv_w=m_rg_conv_w,
                 rg_conv_b=m_rg_conv_b, rg_w_a=m_rg_w_a, rg_b_a=m_rg_b_a, rg_w_x=m_rg_w_x, rg_b_x=m_rg_b_x,
                 rg_lambda=m_rg_lambda, hg_lb_logits=m_hg_lb_logits, hg_norm=m_hg_norm, gla_w_in=m_gla_w_in,
                 gla_w_out=m_gla_w_out, gla_w_gate_up=m_gla_w_gate_up, gla_b_gate=m_gla_b_gate,
                 gla_norm=m_gla_norm)
    v_loc = dict(norm_mix=v_norm_mix, norm_mlp=v_norm_mlp, norm_final=v_norm_final, mlp_w1=v_mlp_w1,
                 mlp_w2=v_mlp_w2, ab_w_in=v_ab_w_in, ab_w_out=v_ab_w_out, rg_conv_w=v_rg_conv_w,
                 rg_conv_b=v_rg_conv_b, rg_w_a=v_rg_w_a, rg_b_a=v_rg_b_a, rg_w_x=v_rg_w_x, rg_b_x=v_rg_b_x,
                 rg_lambda=v_rg_lambda, hg_lb_logits=v_hg_lb_logits, hg_norm=v_hg_norm, gla_w_in=v_gla_w_in,
                 gla_w_out=v_gla_w_out, gla_w_gate_up=v_gla_w_gate_up, gla_b_gate=v_gla_b_gate,
                 gla_norm=v_gla_norm)

    T = x.shape[1]
    h0 = x.reshape(T, D_MODEL)
    target = loss_target.reshape(T, D_MODEL)
    ax, ay, ac = lax.axis_index("x"), lax.axis_index("y"), lax.axis_index("c")
    dev = 4 * ax + 2 * ay + ac
    place = jnp.stack([ac, 2 * ax + ay]).astype(jnp.int32)

    (abin_g,) = _allgather_hbm([ab_w_in[0].astype(BF16)], "ag_first")
    wab_in = jnp.transpose(abin_g, (1, 0, 2)).reshape(D_MODEL, AB_IN)
    rest_shards = [mlp_w1[0].astype(BF16), mlp_w2[0].astype(BF16), gla_w_in[0].astype(BF16),
                   gla_w_out[0].astype(BF16), mlp_w1[1].astype(BF16), mlp_w2[1].astype(BF16),
                   ab_w_out[0].astype(BF16)]
    ag_started = _copies_start(_plan_gather_first, 4 * len(rest_shards), rest_shards,
                               _landing(N_DEV, [s[None] for s in rest_shards]), "ag_rest_start")

    small_local = [w_loc[n] for n in SMALL_SHARDED]
    small_g = _allgather_vmem(_pack_rows(small_local, 8), "ag_small")
    small_g = small_g.reshape(N_DEV, -1, LANES)
    full = {}
    off = 0
    for n, a in zip(SMALL_SHARDED, small_local):
        full[n] = _unshard_last(small_g[:, off:off + a.size // LANES].reshape(N_DEV, a.size), a.shape)
        off += _part_rows(a)
    conv_w = full["rg_conv_w"][0]
    b_a, b_x, lam = full["rg_b_a"][0], full["rg_b_x"][0], full["rg_lambda"][0]
    w_up, b_gate, g_norm = full["gla_w_gate_up"][0], full["gla_b_gate"][0], full["gla_norm"]

    cw8 = jnp.pad(conv_w, ((0, 4), (0, 0)))
    wbd = jnp.concatenate([_block_diag(rg_w_a[0]), _block_diag(rg_w_x[0])], axis=2).astype(BF16)
    rg_bias = jnp.concatenate([b_a, b_x], axis=1).reshape(2, 1, 2 * RG_W)
    lam3 = lam.reshape(2, 1, RG_W)
    l0, l1 = hg_lb_logits[0:1], hg_lb_logits[1:2]
    wup_pad = jnp.zeros((2, LANES, 512), F32).at[0, 0:16].set(w_up[0]).at[1, 16:32].set(w_up[1])
    bg3 = b_gate.reshape(2, 1, 512)
    nmix0, nmix1 = norm_mix[0:1], norm_mix[1:2]
    nmlp0, nmlp1 = norm_mlp[0:1], norm_mlp[1:2]
    nfin = norm_final.reshape(1, D_MODEL)

    proj0, y0 = _norm_matmul(h0, _after(ag_started[4], nmix0), wab_in, "l0_in_proj")
    xc = _rg_conv_fwd(proj0, cw8, rg_conv_b, "rg_conv")
    hs = _rg_scan_fwd(xc, wbd, rg_bias, lam3, "rg_scan")
    o_hg, s_hg = _hg_fwd(proj0, l0, l1, "hg_chunks")
    rest_shards, rest_lands = _copies_wait(_plan_gather_first, ag_started, hs, "ag_rest_wait")
    pass_started = _copies_start(_plan_gather_pass, 3 * len(rest_lands), [], rest_lands, "ag_pass_start")
    mixin0 = _l0_combine_fwd(hs, proj0, o_hg, _after(pass_started[4], hg_norm), "l0_combine")
    _, rest_g = _copies_wait(_plan_gather_pass, pass_started, mixin0, "ag_pass_wait")
    rest_g = [lax.dynamic_update_index_in_dim(g, s, dev, 0) for g, s in zip(rest_g, rest_shards)]
    wab_out = rest_g[6].reshape(D_MODEL, D_MODEL)
    h1 = _matmul_res(mixin0, wab_out, h0, "l0_out_proj")
    w1g = (rest_g[0], rest_g[4])
    w2f = (rest_g[1].reshape(D_FF, D_MODEL), rest_g[5].reshape(D_FF, D_MODEL))
    wgla_in = jnp.pad(jnp.transpose(rest_g[2], (1, 0, 2)).reshape(D_MODEL, GLA_IN),
                      ((0, 0), (0, GLA_IN_PAD - GLA_IN)))
    wgla_out = rest_g[3].reshape(D_MODEL, D_MODEL)
    h2, pre0, ym0 = _mlp_fwd(h1, nmlp0, w1g[0], w2f[0], "mlp0")
    proj1, y1 = _norm_matmul(h2, nmix1, wgla_in, "l1_in_proj")
    z_gate, lr_b = _gate_logits(proj1, wup_pad, bg3, "gla_gate_logits")
    o_gla, s_gla = _gla_fwd(proj1, z_gate, "gla_chunks")
    mixin1 = _l1_combine_fwd(o_gla, proj1, g_norm, "l1_combine")
    h3 = _matmul_res(mixin1, wgla_out, h2, "l1_out_proj")
    h4, pre1, ym1 = _mlp_fwd(h3, nmlp1, w1g[1], w2f[1], "mlp1")
    loss_blk, dh4, dh4b, d_nfin = _final_loss(h4, nfin, target, "final_loss")
    loss = lax.psum(loss_blk[0, 0], ("x", "y", "c"))

    dh3, dh3b, dpre1, act1, d_nmlp1 = _mlp_bwd(dh4, h3, nmlp1, pre1, w1g[1], w2f[1], "mlp1_bwd")
    g_w1_1 = _wgrad(ym1, dpre1, 512, "mlp1_dw1", sharded_cols=True)
    g_w2_1 = _wgrad(act1, dh4b, 1024, "mlp1_dw2")
    dmixin1 = _dgrad(dh3b, wgla_out, "l1_out_dgrad")
    g_gla_out = _wgrad(mixin1, dh3b, 1024, "l1_out_dw")
    do_gla, dr, d_gnorm = _l1_combine_bwd(o_gla, proj1, g_norm, dmixin1, "l1_combine_bwd")
    dq1, dk1, dv1, dz_gate = _gla_bwd(proj1, z_gate, s_gla, do_gla, "gla_chunks_bwd")
    dlr1, d_bg, dz_b = _gate_logits_bwd(dz_gate, wup_pad, "gla_gate_logits_bwd")
    d_wup = [_wgrad(lr_b, dz_b[d], 512, "gla_gate_dw%d" % d) for d in range(2)]
    dproj1 = _l1_assemble(dq1, dk1, dv1, dr, dlr1, "l1_assemble")
    dh2, dh2b, d_nmix1 = _dgrad_norm(dproj1, wgla_in, h2, nmix1, dh3, "l1_in_dgrad")
    g_gla_in = _wgrad(y1, dproj1, 640, "l1_in_dw")

    def reduce_start(grads, tag):
        return _copies_start(_plan_grads_sibling, 4 * len(grads), grads, _landing(4, grads), "rs_%s_d2d_start" % tag)

    def reduce_mid(started, after, tag):
        grads, got = _copies_wait(_plan_grads_sibling, started, after, "rs_%s_d2d_wait" % tag)
        parts = [_chip_partial(g, r, place, "rs_%s_partial%d" % (tag, a)) for a, (g, r) in enumerate(zip(grads, got))]
        pb = [p[0] for p in parts]
        return _copies_start(_plan_grads_chips, 3 * len(pb), pb, _landing(3, pb), "rs_%s_ici_start" % tag), \
            [p[1] for p in parts]

    def reduce_end(started, mine, after, tag, ws, ms, vs):
        _, got = _copies_wait(_plan_grads_chips, started, after, "rs_%s_ici_wait" % tag)
        return [_adamw(w, (p, r), m, v, "adamw_%s%d" % (tag, a))
                for a, (w, p, r, m, v) in enumerate(zip(ws, mine, got, ms, vs))]

    slots_l1 = [g_w1_1, g_w2_1.reshape(N_DEV, 512, D_MODEL),
                jnp.transpose(g_gla_in[:, :GLA_IN].reshape(D_MODEL, N_DEV, GLA_IN // N_DEV), (1, 0, 2)),
                g_gla_out.reshape(N_DEV, 128, D_MODEL)]
    ra_d2d = reduce_start(slots_l1, "l1")

    dh1, dh1b, dpre0, act0, d_nmlp0 = _mlp_bwd(dh2, h1, _after(ra_d2d[4], nmlp0), pre0, w1g[0], w2f[0], "mlp0_bwd")
    g_w1_0 = _wgrad(ym0, dpre0, 512, "mlp0_dw1", sharded_cols=True)
    g_w2_0 = _wgrad(act0, dh2b, 1024, "mlp0_dw2")
    ra_ici, ra_mine = reduce_mid(ra_d2d, g_w2_0, "l1")
    rb_d2d = reduce_start([g_w1_0, g_w2_0.reshape(N_DEV, 512, D_MODEL)], "mlp0")
    dmixin0 = _dgrad(dh1b, wab_out, "l0_out_dgrad")
    g_ab_out = _wgrad(mixin0, dh1b, 1024, "l0_out_dw")
    dho, dga, do_hg, dg_gate, d_hgnorm = _l0_combine_bwd(
        hs, proj0, o_hg, _after(rb_d2d[4], _after(ra_ici[4], hg_norm)), dmixin0, "l0_combine_bwd")
    dxc, d_wbd, d_rgb, d_lam = _rg_scan_bwd(xc, wbd, rg_bias, lam3, hs, dho, "rg_scan_bwd")
    dxa, d_cw8, d_cb = _rg_conv_bwd(dxc, proj0, cw8, "rg_conv_bwd")
    dq0, df0, dv0, d_l0, d_l1 = _hg_bwd(proj0, l0, l1, s_hg, do_hg, "hg_chunks_bwd")
    rb_ici, rb_mine = reduce_mid(rb_d2d, d_l0, "mlp0")
    dproj0 = _l0_assemble(dxa, dga, dq0, df0, dv0, dg_gate, "l0_assemble")
    g_ab_in = _wgrad(y0, dproj0, 512, "l0_in_dw")
    rc_d2d = reduce_start([jnp.transpose(g_ab_in.reshape(D_MODEL, N_DEV, AB_IN // N_DEV), (1, 0, 2)),
                           g_ab_out.reshape(N_DEV, 128, D_MODEL)], "ab")
    dx, _, d_nmix0 = _dgrad_norm(dproj0, wab_in, h0, _after(rc_d2d[4], _after(rb_ici[4], nmix0)), dh1,
                                 "l0_in_dgrad")
    rc_ici, rc_mine = reduce_mid(rc_d2d, d_nmix0, "ab")

    res_l1 = reduce_end(ra_ici, ra_mine, rc_ici[4], "l1",
                        [mlp_w1[1], mlp_w2[1], gla_w_in[0], gla_w_out[0]],
                        [m_mlp_w1[1], m_mlp_w2[1], m_gla_w_in[0], m_gla_w_out[0]],
                        [v_mlp_w1[1], v_mlp_w2[1], v_gla_w_in[0], v_gla_w_out[0]])
    res_mlp0 = reduce_end(rb_ici, rb_mine, res_l1[3][0], "mlp0", [mlp_w1[0], mlp_w2[0]],
                          [m_mlp_w1[0], m_mlp_w2[0]], [v_mlp_w1[0], v_mlp_w2[0]])

    def stacked(a, b):
        return tuple(jnp.stack([a[k], b[k]]) for k in range(4))

    res = {"mlp_w1": stacked(res_mlp0[0], res_l1[0]), "mlp_w2": stacked(res_mlp0[1], res_l1[1]),
           "gla_w_in": tuple(res_l1[2][k][None] for k in range(4)),
           "gla_w_out": tuple(res_l1[3][k][None] for k in range(4))}

    d_wa = _block_diag_extract(d_wbd[:, :, :RG_W])[None]
    d_wx = _block_diag_extract(d_wbd[:, :, RG_W:])[None]
    small_full = {
        "norm_mix": jnp.concatenate([d_nmix0, d_nmix1], axis=0), "norm_mlp": jnp.concatenate([d_nmlp0, d_nmlp1], axis=0),
        "norm_final": d_nfin.reshape(D_MODEL), "rg_conv_b": d_cb, "rg_w_a": d_wa, "rg_w_x": d_wx,
        "hg_lb_logits": jnp.concatenate([d_l0[0] + d_l0[1], d_l1[0] + d_l1[1]], axis=0), "hg_norm": d_hgnorm,
        "rg_conv_w": d_cw8[0:4][None], "rg_b_a": d_rgb[:, 0, :RG_W][None], "rg_b_x": d_rgb[:, 0, RG_W:][None],
        "rg_lambda": d_lam[:, 0, :][None],
        "gla_w_gate_up": jnp.stack([d_wup[0][0:16], d_wup[1][16:32]])[None], "gla_b_gate": d_bg[:, 0, :][None],
        "gla_norm": d_gnorm}
    small_names = SMALL_REPLICATED + SMALL_SHARDED
    packed = _pack_rows([small_full[n] for n in small_names], 8)
    summed = _allgather_vmem(packed, "ar_small", reduce=True)
    g_small = {}
    off = 0
    for n in small_names:
        a = small_full[n]
        gfull = summed[off:off + a.size // LANES].reshape(a.shape)
        off += _part_rows(a)
        if n in SMALL_SHARDED:
            loc = w_loc[n].shape[-1]
            gfull = lax.dynamic_slice_in_dim(gfull, dev * loc, loc, axis=gfull.ndim - 1)
        g_small[n] = gfull
    sw = _pack_rows([w_loc[n] for n in small_names], 256)
    sg = _pack_rows([g_small[n] for n in small_names], 256)
    sm = _pack_rows([m_loc[n] for n in small_names], 256)
    sv = _pack_rows([v_loc[n] for n in small_names], 256)
    small_res = _adamw(sw, (sg, None), sm, sv, "adamw_small")
    res_ab = reduce_end(rc_ici, rc_mine, small_res[0], "ab", [ab_w_in[0], ab_w_out[0]],
                        [m_ab_w_in[0], m_ab_w_out[0]], [v_ab_w_in[0], v_ab_w_out[0]])
    res["ab_w_in"] = tuple(res_ab[0][k][None] for k in range(4))
    res["ab_w_out"] = tuple(res_ab[1][k][None] for k in range(4))
    off = 0
    for n in small_names:
        a = w_loc[n]
        nr = a.size // LANES
        res[n] = tuple(small_res[k][off:off + nr].reshape(a.shape) for k in range(4))
        off += _part_rows(a)

    grad_x = dx.reshape(1, T, D_MODEL)
    out = [loss, grad_x]
    for k in range(4):
        out += [res[n][k] for n in WEIGHT_NAMES]
    return tuple(out)
```

```python
import jax
import jax.numpy as jnp
from jax import lax
from jax.experimental import pallas as pl
from jax.experimental.pallas import tpu as pltpu

F32, BF16 = jnp.float32, jnp.bfloat16
HI = lax.Precision.HIGHEST
MESH = pl.DeviceIdType.MESH

D_MODEL = 1024
D_FF = 4096
RG_W = 512
HG_W = 512
CHUNK = 64
EPS = 1e-6
RG_C = 8.0
AB_IN = 3584
GLA_IN = 3104
GLA_IN_PAD = 3200
N_DEV = 8
LANES = 128
SUBLANES = 8
VMEM_LIMIT = 48 * 1024 * 1024

ADAM_LR, ADAM_B1, ADAM_B2, ADAM_EPS, ADAM_WD, ADAM_STEP = 0.001, 0.9, 0.999, 1e-08, 0.01, 10


def _params(*sem):
    return pltpu.CompilerParams(dimension_semantics=sem, vmem_limit_bytes=VMEM_LIMIT)


def _dg(a, b, ca, cb):
    return lax.dot_general(a.astype(BF16), b.astype(BF16), (((ca,), (cb,)), ((), ())),
                           preferred_element_type=F32)


@jax.custom_vjp
def _mm_nn(a, b):
    return _dg(a, b, 1, 0)


_mm_nn.defvjp(lambda a, b: (_dg(a, b, 1, 0), (a, b)),
              lambda res, g: (_dg(g, res[1], 1, 1), _dg(res[0], g, 0, 0)))


@jax.custom_vjp
def _mm_nt(a, b):
    return _dg(a, b, 1, 1)


_mm_nt.defvjp(lambda a, b: (_dg(a, b, 1, 1), (a, b)),
              lambda res, g: (_dg(g, res[1], 1, 0), _dg(g, res[0], 0, 0)))


@jax.custom_vjp
def _mm_tn(a, b):
    return _dg(a, b, 0, 0)


_mm_tn.defvjp(lambda a, b: (_dg(a, b, 0, 0), (a, b)),
              lambda res, g: (_dg(res[1], g, 1, 1), _dg(res[0], g, 1, 0)))


@jax.custom_vjp
def _cum(tri, tri_t, x):
    return jnp.dot(tri, x, precision=HI, preferred_element_type=F32)


_cum.defvjp(lambda tri, tri_t, x: (jnp.dot(tri, x, precision=HI, preferred_element_type=F32), (tri, tri_t)),
            lambda res, g: (jnp.zeros_like(res[0]), jnp.zeros_like(res[1]),
                            jnp.dot(res[1], g, precision=HI, preferred_element_type=F32)))


def _sig(x):
    return 1.0 / (1.0 + jnp.exp(-x))


def _gelu(x):
    return 0.5 * x * (1.0 + jnp.tanh(0.7978845608028654 * (x + 0.044715 * (x * x * x))))


def _softplus(z):
    return jnp.maximum(z, 0.0) + jnp.log(1.0 + jnp.exp(-jnp.abs(z)))


def _rms(x):
    return lax.rsqrt(jnp.mean(x * x, axis=-1, keepdims=True) + EPS)


def _rmsnorm_bwd(x, gain, dy):
    r = _rms(x)
    xh = x * r
    dgain = jnp.sum(dy * xh, axis=0, keepdims=True)
    dxh = dy * gain
    dx = r * (dxh - xh * jnp.mean(dxh * xh, axis=-1, keepdims=True))
    return dx, dgain


def _headnorm(o, gain, n_heads, hd):
    parts = []
    for h in range(n_heads):
        oh = o[:, h * hd:(h + 1) * hd]
        parts.append(oh * _rms(oh))
    return jnp.concatenate(parts, axis=1) * gain


def _tri_consts(d):
    row = lax.broadcasted_iota(jnp.int32, (CHUNK, CHUNK), 0)
    col = lax.broadcasted_iota(jnp.int32, (CHUNK, CHUNK), 1)
    ge = (row >= col).astype(F32)
    le = (row <= col).astype(F32)
    r1 = lax.broadcasted_iota(jnp.int32, (CHUNK, 1), 0)
    if d == 0:
        return ge, le, (r1 <= CHUNK // 2).astype(F32)
    return le, ge, (r1 >= CHUNK // 2 - 1).astype(F32)


def _chunk_core(qh, k, v, logf, st_prev, tri, tri_t, mref, n_heads, dk, dv):
    cum = _cum(tri, tri_t, logf)
    ref = jnp.sum(logf * mref, axis=0, keepdims=True)
    last = jnp.sum(logf, axis=0, keepdims=True)
    q_in = qh * jnp.exp(cum - ref)
    k_in = k * jnp.exp(ref - cum)
    k_st = k * jnp.exp(last - cum)
    q_dec = qh * jnp.exp(cum)
    decay = jnp.exp(last)
    outs, sts = [], []
    for h in range(n_heads):
        sk = slice(h * dk, (h + 1) * dk)
        sv = slice(h * dv, (h + 1) * dv)
        sc = _mm_nt(q_in[:, sk], k_in[:, sk]) * tri
        o = _mm_nn(sc, v[:, sv]) + _mm_nt(q_dec[:, sk], st_prev[h])
        sts.append(st_prev[h] * decay[:, sk] + _mm_tn(v[:, sv], k_st[:, sk]))
        outs.append(o)
    return jnp.concatenate(outs, axis=1), tuple(sts)


def _hg_chunk(q, f, v, l0, l1, st_prev, tri, tri_t, mref):
    lb = _sig(l0 - l1)
    sg = _sig(f)
    qh = q * _sig(q)
    logf = jnp.log(lb + (1.0 - lb) * sg)
    k = (1.0 - lb) * (1.0 - sg)
    return _chunk_core(qh, k, v, logf, st_prev, tri, tri_t, mref, 4, 128, 128)


def _gla_chunk(q, k, v, z, st_prev, tri, tri_t, mref):
    logf = (jnp.minimum(z, 0.0) - jnp.log(1.0 + jnp.exp(-jnp.abs(z)))) * (1.0 / 16.0)
    qh = q * (128.0 ** -0.5)
    return _chunk_core(qh, k, v, logf, st_prev, tri, tri_t, mref, 4, 128, 256)


def _rg_gates(xc, wbd, bias, lam):
    z = _mm_nn(xc, wbd) + bias
    r = _sig(z[:, :RG_W])
    i = _sig(z[:, RG_W:])
    log_a = -RG_C * r * _softplus(-lam)
    a = jnp.exp(log_a)
    x2 = 2.0 * log_a
    neg_expm1 = jnp.where(x2 > -1e-2, -(x2 + 0.5 * x2 * x2 + x2 * x2 * x2 * (1.0 / 6.0)), 1.0 - jnp.exp(x2))
    u = jnp.sqrt(neg_expm1) * (i * xc)
    return a, u


def _l0_combine(hf, hb, ga, of, ob, g, gain):
    ya = (hf + hb) * _gelu(ga)
    yb = _headnorm(of + ob, gain, 4, 128) * (g * _sig(g))
    return jnp.concatenate([ya, yb], axis=1)


def _l1_combine(of, ob, r, gain):
    return _headnorm(of + ob, gain, 4, 256) * (r * _sig(r))


def _norm_matmul(h, gain, w, name):
    T, D = h.shape
    N = w.shape[1]
    tm = min(512, T)

    def body(h_ref, g_ref, w_ref, o_ref, y_ref):
        x = h_ref[...]
        y = (x * _rms(x) * g_ref[...]).astype(BF16)
        y_ref[...] = y
        o_ref[...] = jnp.dot(y, w_ref[...], preferred_element_type=F32)

    return pl.pallas_call(
        body, name=name, grid=(T // tm,),
        in_specs=[pl.BlockSpec((tm, D), lambda i: (i, 0)), pl.BlockSpec((1, D), lambda i: (0, 0)),
                  pl.BlockSpec((D, N), lambda i: (0, 0))],
        out_specs=[pl.BlockSpec((tm, N), lambda i: (i, 0)), pl.BlockSpec((tm, D), lambda i: (i, 0))],
        out_shape=[jax.ShapeDtypeStruct((T, N), F32), jax.ShapeDtypeStruct((T, D), BF16)],
        compiler_params=_params("parallel"))(h, gain, w)


def _matmul_res(a, w, res, name):
    T, K = a.shape
    N = w.shape[1]
    tm = min(512, T)

    def body(a_ref, w_ref, r_ref, o_ref):
        o_ref[...] = r_ref[...] + jnp.dot(a_ref[...], w_ref[...], preferred_element_type=F32)

    return pl.pallas_call(
        body, name=name, grid=(T // tm,),
        in_specs=[pl.BlockSpec((tm, K), lambda i: (i, 0)), pl.BlockSpec((K, N), lambda i: (0, 0)),
                  pl.BlockSpec((tm, N), lambda i: (i, 0))],
        out_specs=pl.BlockSpec((tm, N), lambda i: (i, 0)),
        out_shape=jax.ShapeDtypeStruct((T, N), F32),
        compiler_params=_params("parallel"))(a, w, res)


def _dgrad(dc, w, name):
    T, N = dc.shape
    K = w.shape[0]
    tm = min(512, T)

    def body(d_ref, w_ref, o_ref):
        o_ref[...] = _dg(d_ref[...], w_ref[...], 1, 1)

    return pl.pallas_call(
        body, name=name, grid=(T // tm,),
        in_specs=[pl.BlockSpec((tm, N), lambda i: (i, 0)), pl.BlockSpec((K, N), lambda i: (0, 0))],
        out_specs=pl.BlockSpec((tm, K), lambda i: (i, 0)),
        out_shape=jax.ShapeDtypeStruct((T, K), F32),
        compiler_params=_params("parallel"))(dc, w)


def _dgrad_norm(dproj, w, h, gain, dres, name):
    T, N = dproj.shape
    D = w.shape[0]
    tm = min(512, T)

    def body(dp_ref, w_ref, h_ref, g_ref, dr_ref, dh_ref, dhb_ref, dg_ref):
        @pl.when(pl.program_id(0) == 0)
        def _():
            dg_ref[...] = jnp.zeros_like(dg_ref)

        dy = _dg(dp_ref[...], w_ref[...], 1, 1)
        dx, dgain = _rmsnorm_bwd(h_ref[...], g_ref[...], dy)
        dh = dr_ref[...] + dx
        dh_ref[...] = dh
        dhb_ref[...] = dh.astype(BF16)
        dg_ref[...] += dgain

    return pl.pallas_call(
        body, name=name, grid=(T // tm,),
        in_specs=[pl.BlockSpec((tm, N), lambda i: (i, 0)), pl.BlockSpec((D, N), lambda i: (0, 0)),
                  pl.BlockSpec((tm, D), lambda i: (i, 0)), pl.BlockSpec((1, D), lambda i: (0, 0)),
                  pl.BlockSpec((tm, D), lambda i: (i, 0))],
        out_specs=[pl.BlockSpec((tm, D), lambda i: (i, 0)), pl.BlockSpec((tm, D), lambda i: (i, 0)),
                   pl.BlockSpec((1, D), lambda i: (0, 0))],
        out_shape=[jax.ShapeDtypeStruct((T, D), F32), jax.ShapeDtypeStruct((T, D), BF16),
                   jax.ShapeDtypeStruct((1, D), F32)],
        compiler_params=_params("arbitrary"))(dproj, w, h, gain, dres)


def _wgrad(a, b, tn, name, sharded_cols=False):
    T, K = a.shape
    N = b.shape[1]
    tk = min(1024, K)
    tt = min(1024, T)
    nt = T // tt

    def body(a_ref, b_ref, o_ref):
        @pl.when(pl.program_id(2) == 0)
        def _():
            o_ref[...] = jnp.zeros_like(o_ref)

        o_ref[...] += _dg(a_ref[...], b_ref[...], 0, 0)

    if sharded_cols:
        out_spec = pl.BlockSpec((None, tk, tn), lambda k, n, t: (n, k, 0))
        out_shape = jax.ShapeDtypeStruct((N // tn, K, tn), F32)
    else:
        out_spec = pl.BlockSpec((tk, tn), lambda k, n, t: (k, n))
        out_shape = jax.ShapeDtypeStruct((K, N), F32)
    return pl.pallas_call(
        body, name=name, grid=(K // tk, N // tn, nt),
        in_specs=[pl.BlockSpec((tt, tk), lambda k, n, t: (t, k)), pl.BlockSpec((tt, tn), lambda k, n, t: (t, n))],
        out_specs=out_spec, out_shape=out_shape,
        compiler_params=_params("parallel", "parallel", "arbitrary"))(a, b)


def _mlp_fwd(h, gain, w1g, w2, name):
    T, D = h.shape
    nf, _, tf = w1g.shape
    tm = min(1024, T)

    def body(h_ref, g_ref, w1_ref, w2_ref, o_ref, pre_ref, y_ref, ysc, acc):
        j = pl.program_id(1)

        @pl.when(j == 0)
        def _():
            x = h_ref[...]
            y = (x * _rms(x) * g_ref[...]).astype(BF16)
            ysc[...] = y
            y_ref[...] = y
            acc[...] = jnp.zeros_like(acc)

        pre = jnp.dot(ysc[...], w1_ref[...], preferred_element_type=F32)
        pre_ref[...] = pre.astype(BF16)
        act = jnp.square(jnp.maximum(pre, 0.0))
        acc[...] += jnp.dot(act.astype(BF16), w2_ref[...], preferred_element_type=F32)

        @pl.when(j == nf - 1)
        def _():
            o_ref[...] = h_ref[...] + acc[...]

    return pl.pallas_call(
        body, name=name, grid=(T // tm, nf),
        in_specs=[pl.BlockSpec((tm, D), lambda i, j: (i, 0)), pl.BlockSpec((1, D), lambda i, j: (0, 0)),
                  pl.BlockSpec((None, D, tf), lambda i, j: (j, 0, 0)), pl.BlockSpec((tf, D), lambda i, j: (j, 0))],
        out_specs=[pl.BlockSpec((tm, D), lambda i, j: (i, 0)), pl.BlockSpec((tm, tf), lambda i, j: (i, j)),
                   pl.BlockSpec((tm, D), lambda i, j: (i, 0))],
        out_shape=[jax.ShapeDtypeStruct((T, D), F32), jax.ShapeDtypeStruct((T, nf * tf), BF16),
                   jax.ShapeDtypeStruct((T, D), BF16)],
        scratch_shapes=[pltpu.VMEM((tm, D), BF16), pltpu.VMEM((tm, D), F32)],
        compiler_params=_params("parallel", "arbitrary"))(h, gain, w1g, w2)


def _mlp_bwd(dout, h, gain, pre, w1g, w2, name):
    T, D = h.shape
    nf, _, tf = w1g.shape
    tm = min(512, T)

    def body(do_ref, h_ref, g_ref, pre_ref, w1_ref, w2_ref, dh_ref, dhb_ref, dpre_ref, act_ref, dg_ref, dy):
        i, j = pl.program_id(0), pl.program_id(1)

        @pl.when(j == 0)
        def _():
            dy[...] = jnp.zeros_like(dy)

        @pl.when((i == 0) & (j == 0))
        def _():
            dg_ref[...] = jnp.zeros_like(dg_ref)

        rp = jnp.maximum(pre_ref[...].astype(F32), 0.0)
        dact = _dg(do_ref[...], w2_ref[...], 1, 1)
        dpre = (dact * (2.0 * rp)).astype(BF16)
        dpre_ref[...] = dpre
        act_ref[...] = (rp * rp).astype(BF16)
        dy[...] += _dg(dpre, w1_ref[...], 1, 1)

        @pl.when(j == nf - 1)
        def _():
            dx, dgain = _rmsnorm_bwd(h_ref[...], g_ref[...], dy[...])
            dh = do_ref[...] + dx
            dh_ref[...] = dh
            dhb_ref[...] = dh.astype(BF16)
            dg_ref[...] += dgain

    return pl.pallas_call(
        body, name=name, grid=(T // tm, nf),
        in_specs=[pl.BlockSpec((tm, D), lambda i, j: (i, 0)), pl.BlockSpec((tm, D), lambda i, j: (i, 0)),
                  pl.BlockSpec((1, D), lambda i, j: (0, 0)), pl.BlockSpec((tm, tf), lambda i, j: (i, j)),
                  pl.BlockSpec((None, D, tf), lambda i, j: (j, 0, 0)), pl.BlockSpec((tf, D), lambda i, j: (j, 0))],
        out_specs=[pl.BlockSpec((tm, D), lambda i, j: (i, 0)), pl.BlockSpec((tm, D), lambda i, j: (i, 0)),
                   pl.BlockSpec((tm, tf), lambda i, j: (i, j)),
                   pl.BlockSpec((tm, tf), lambda i, j: (i, j)), pl.BlockSpec((1, D), lambda i, j: (0, 0))],
        out_shape=[jax.ShapeDtypeStruct((T, D), F32), jax.ShapeDtypeStruct((T, D), BF16),
                   jax.ShapeDtypeStruct((T, nf * tf), BF16),
                   jax.ShapeDtypeStruct((T, nf * tf), BF16), jax.ShapeDtypeStruct((1, D), F32)],
        scratch_shapes=[pltpu.VMEM((tm, D), F32)],
        compiler_params=_params("arbitrary", "arbitrary"))(dout, h, gain, pre, w1g, w2)


def _final_loss(h, gain, target, name):
    T, D = h.shape
    tm = min(512, T)

    def body(h_ref, g_ref, t_ref, l_ref, dh_ref, dhb_ref, dg_ref):
        @pl.when(pl.program_id(0) == 0)
        def _():
            l_ref[...] = jnp.zeros_like(l_ref)
            dg_ref[...] = jnp.zeros_like(dg_ref)

        x = h_ref[...]
        err = x * _rms(x) * g_ref[...] - t_ref[...]
        l_ref[...] += 0.5 * jnp.sum(jnp.mean(err * err, axis=-1, keepdims=True), axis=0, keepdims=True)
        dx, dgain = _rmsnorm_bwd(x, g_ref[...], err * (1.0 / D))
        dh_ref[...] = dx
        dhb_ref[...] = dx.astype(BF16)
        dg_ref[...] += dgain

    return pl.pallas_call(
        body, name=name, grid=(T // tm,),
        in_specs=[pl.BlockSpec((tm, D), lambda i: (i, 0)), pl.BlockSpec((1, D), lambda i: (0, 0)),
                  pl.BlockSpec((tm, D), lambda i: (i, 0))],
        out_specs=[pl.BlockSpec((SUBLANES, LANES), lambda i: (0, 0)), pl.BlockSpec((tm, D), lambda i: (i, 0)),
                   pl.BlockSpec((tm, D), lambda i: (i, 0)), pl.BlockSpec((1, D), lambda i: (0, 0))],
        out_shape=[jax.ShapeDtypeStruct((SUBLANES, LANES), F32), jax.ShapeDtypeStruct((T, D), F32),
                   jax.ShapeDtypeStruct((T, D), BF16), jax.ShapeDtypeStruct((1, D), F32)],
        compiler_params=_params("arbitrary"))(h, gain, target)


def _halo_specs(tm, T, width, col, lead=None):
    r8 = tm // SUBLANES
    nb8 = T // SUBLANES
    if lead is None:
        return [pl.BlockSpec((tm, width), lambda i: (i, col)),
                pl.BlockSpec((SUBLANES, width), lambda i: (jnp.maximum(i * r8 - 1, 0), col)),
                pl.BlockSpec((SUBLANES, width), lambda i: (jnp.minimum((i + 1) * r8, nb8 - 1), col))]
    return [pl.BlockSpec((None, tm, width), lambda i: (lead, i, col)),
            pl.BlockSpec((None, SUBLANES, width), lambda i: (lead, jnp.maximum(i * r8 - 1, 0), col)),
            pl.BlockSpec((None, SUBLANES, width), lambda i: (lead, jnp.minimum((i + 1) * r8, nb8 - 1), col))]


def _ext(cur, prev, nxt, has_prev, has_next):
    return jnp.concatenate([jnp.where(has_prev, prev, 0.0), cur, jnp.where(has_next, nxt, 0.0)], axis=0)


def _shifted(ext, offset, tm):
    n = ext.shape[0]
    sh = (-offset) % n
    r = ext if sh == 0 else pltpu.roll(ext, sh, 0)
    return r[SUBLANES:SUBLANES + tm]


def _rg_conv_fwd(proj, cw8, cb, name):
    T = proj.shape[0]
    tm = min(512, T)
    nT = T // tm

    def body(cur_ref, prev_ref, next_ref, w_ref, b_ref, o_ref):
        i = pl.program_id(0)
        ext = _ext(cur_ref[...], prev_ref[...], next_ref[...], i > 0, i < nT - 1)
        acc = jnp.broadcast_to(b_ref[...], (tm, RG_W))
        for k in range(4):
            acc = acc + w_ref[k:k + 1, :] * _shifted(ext, k - 2, tm)
        o_ref[...] = acc

    return pl.pallas_call(
        body, name=name, grid=(nT,),
        in_specs=_halo_specs(tm, T, RG_W, 0) + [pl.BlockSpec((SUBLANES, RG_W), lambda i: (0, 0)),
                                                pl.BlockSpec((1, RG_W), lambda i: (0, 0))],
        out_specs=pl.BlockSpec((tm, RG_W), lambda i: (i, 0)),
        out_shape=jax.ShapeDtypeStruct((T, RG_W), F32),
        compiler_params=_params("parallel"))(proj, proj, proj, cw8, cb)


def _rg_conv_bwd(dxc, proj, cw8, name):
    T = proj.shape[0]
    tm = min(512, T)
    nT = T // tm

    def body(a0, p0, n0, a1, p1, n1, xa, xp, xn, w_ref, dxa_ref, dw_ref, db_ref):
        i = pl.program_id(0)

        @pl.when(i == 0)
        def _():
            dw_ref[...] = jnp.zeros_like(dw_ref)
            db_ref[...] = jnp.zeros_like(db_ref)

        has_p, has_n = i > 0, i < nT - 1
        cur = a0[...] + a1[...]
        dext = _ext(cur, p0[...] + p1[...], n0[...] + n1[...], has_p, has_n)
        xext = _ext(xa[...], xp[...], xn[...], has_p, has_n)
        acc = jnp.zeros((tm, RG_W), F32)
        rows = []
        for k in range(4):
            acc = acc + w_ref[k:k + 1, :] * _shifted(dext, 2 - k, tm)
            rows.append(jnp.sum(cur * _shifted(xext, k - 2, tm), axis=0, keepdims=True))
        dxa_ref[...] = acc
        dw_ref[...] += jnp.concatenate(rows + [jnp.zeros((4, RG_W), F32)], axis=0)
        db_ref[...] += jnp.sum(cur, axis=0, keepdims=True)

    return pl.pallas_call(
        body, name=name, grid=(nT,),
        in_specs=(_halo_specs(tm, T, RG_W, 0, lead=0) + _halo_specs(tm, T, RG_W, 0, lead=1)
                  + _halo_specs(tm, T, RG_W, 0) + [pl.BlockSpec((SUBLANES, RG_W), lambda i: (0, 0))]),
        out_specs=[pl.BlockSpec((tm, RG_W), lambda i: (i, 0)), pl.BlockSpec((SUBLANES, RG_W), lambda i: (0, 0)),
                   pl.BlockSpec((1, RG_W), lambda i: (0, 0))],
        out_shape=[jax.ShapeDtypeStruct((T, RG_W), F32), jax.ShapeDtypeStruct((SUBLANES, RG_W), F32),
                   jax.ShapeDtypeStruct((1, RG_W), F32)],
        compiler_params=_params("arbitrary"))(dxc, dxc, dxc, dxc, dxc, dxc, proj, proj, proj, cw8)


def _rg_scan_fwd(xc, wbd, bias, lam, name):
    T = xc.shape[0]
    tm = min(512, T)
    nT = T // tm

    def tile(d, i):
        return i + d * (nT - 1 - 2 * i)

    def body(xc_ref, w_ref, b_ref, lam_ref, h_ref, a_sc, u_sc, carry):
        d, i = pl.program_id(0), pl.program_id(1)

        @pl.when(i == 0)
        def _():
            carry[...] = jnp.zeros_like(carry)

        a, u = _rg_gates(xc_ref[...], w_ref[...], b_ref[...], lam_ref[...])
        a_sc[...] = a
        u_sc[...] = u

        def step(t, h):
            tt = t + d * (tm - 1 - 2 * t)
            h = a_sc[pl.ds(tt, 1), :] * h + u_sc[pl.ds(tt, 1), :]
            h_ref[pl.ds(tt, 1), :] = h
            return h

        carry[0:1, :] = lax.fori_loop(0, tm, step, carry[0:1, :])

    return pl.pallas_call(
        body, name=name, grid=(2, nT),
        in_specs=[pl.BlockSpec((tm, RG_W), lambda d, i: (tile(d, i), 0)),
                  pl.BlockSpec((None, RG_W, 2 * RG_W), lambda d, i: (d, 0, 0)),
                  pl.BlockSpec((None, 1, 2 * RG_W), lambda d, i: (d, 0, 0)),
                  pl.BlockSpec((None, 1, RG_W), lambda d, i: (d, 0, 0))],
        out_specs=pl.BlockSpec((None, tm, RG_W), lambda d, i: (d, tile(d, i), 0)),
        out_shape=jax.ShapeDtypeStruct((2, T, RG_W), F32),
        scratch_shapes=[pltpu.VMEM((tm, RG_W), F32), pltpu.VMEM((tm, RG_W), F32), pltpu.VMEM((SUBLANES, RG_W), F32)],
        compiler_params=_params("arbitrary", "arbitrary"))(xc, wbd, bias, lam)


def _rg_scan_bwd(xc, wbd, bias, lam, hs, dho, name):
    T = xc.shape[0]
    tm = min(512, T)
    nT = T // tm
    r8 = tm // SUBLANES
    nb8 = T // SUBLANES

    def tile(d, i):
        return (nT - 1 - i) + d * (2 * i - (nT - 1))

    def body(xc_ref, w_ref, b_ref, lam_ref, hc_ref, hp_ref, hn_ref, dho_ref,
             dxc_ref, dw_ref, db_ref, dlam_ref, a_sc, dt_sc, carry):
        d, i = pl.program_id(0), pl.program_id(1)
        ti = tile(d, i)

        @pl.when(i == 0)
        def _():
            carry[...] = jnp.zeros_like(carry)
            dw_ref[...] = jnp.zeros_like(dw_ref)
            db_ref[...] = jnp.zeros_like(db_ref)
            dlam_ref[...] = jnp.zeros_like(dlam_ref)

        (a, _), vjp = jax.vjp(_rg_gates, xc_ref[...], w_ref[...].astype(F32), b_ref[...], lam_ref[...])
        a_sc[...] = a

        def step(t, c):
            tt = (tm - 1 - t) + d * (2 * t - (tm - 1))
            dt = dho_ref[pl.ds(tt, 1), :] + c
            dt_sc[pl.ds(tt, 1), :] = dt
            return a_sc[pl.ds(tt, 1), :] * dt

        carry[0:1, :] = lax.fori_loop(0, tm, step, carry[0:1, :])
        dtot = dt_sc[...]
        ext = _ext(hc_ref[...], hp_ref[...], hn_ref[...], ti > 0, ti < nT - 1)
        hprev = jnp.where(d == 0, _shifted(ext, -1, tm), _shifted(ext, 1, tm))
        dxc, dw, db, dlam = vjp((dtot * hprev, dtot))
        dxc_ref[...] = dxc
        dw_ref[...] += dw
        db_ref[...] += db
        dlam_ref[...] += dlam

    return pl.pallas_call(
        body, name=name, grid=(2, nT),
        in_specs=[pl.BlockSpec((tm, RG_W), lambda d, i: (tile(d, i), 0)),
                  pl.BlockSpec((None, RG_W, 2 * RG_W), lambda d, i: (d, 0, 0)),
                  pl.BlockSpec((None, 1, 2 * RG_W), lambda d, i: (d, 0, 0)),
                  pl.BlockSpec((None, 1, RG_W), lambda d, i: (d, 0, 0)),
                  pl.BlockSpec((None, tm, RG_W), lambda d, i: (d, tile(d, i), 0)),
                  pl.BlockSpec((None, SUBLANES, RG_W), lambda d, i: (d, jnp.maximum(tile(d, i) * r8 - 1, 0), 0)),
                  pl.BlockSpec((None, SUBLANES, RG_W),
                               lambda d, i: (d, jnp.minimum((tile(d, i) + 1) * r8, nb8 - 1), 0)),
                  pl.BlockSpec((tm, RG_W), lambda d, i: (tile(d, i), 0))],
        out_specs=[pl.BlockSpec((None, tm, RG_W), lambda d, i: (d, tile(d, i), 0)),
                   pl.BlockSpec((None, RG_W, 2 * RG_W), lambda d, i: (d, 0, 0)),
                   pl.BlockSpec((None, 1, 2 * RG_W), lambda d, i: (d, 0, 0)),
                   pl.BlockSpec((None, 1, RG_W), lambda d, i: (d, 0, 0))],
        out_shape=[jax.ShapeDtypeStruct((2, T, RG_W), F32), jax.ShapeDtypeStruct((2, RG_W, 2 * RG_W), F32),
                   jax.ShapeDtypeStruct((2, 1, 2 * RG_W), F32), jax.ShapeDtypeStruct((2, 1, RG_W), F32)],
        scratch_shapes=[pltpu.VMEM((tm, RG_W), F32), pltpu.VMEM((tm, RG_W), F32), pltpu.VMEM((SUBLANES, RG_W), F32)],
        compiler_params=_params("arbitrary", "arbitrary"))(xc, wbd, bias, lam, hs, hs, hs, dho)


def _chunk_rows(n_chunks, reverse):
    up, down = (lambda c: c), (lambda c: n_chunks - 1 - c)
    return (down, up) if reverse else (up, down)


def _hg_fwd(proj, l0, l1, name):
    T = proj.shape[0]
    nC = T // CHUNK
    H, dk, dv = 4, 128, 128
    rows = _chunk_rows(nC, False)

    def body(qf, ff, vf, qb, fb, vb, l0_ref, l1_ref, of, ob, spf, spb, st):
        @pl.when(pl.program_id(0) == 0)
        def _():
            st[...] = jnp.zeros_like(st)

        for d, (q, f, v, o, sp) in enumerate(((qf, ff, vf, of, spf), (qb, fb, vb, ob, spb))):
            tri, tri_t, mref = _tri_consts(d)
            stp = tuple(st[d, h] for h in range(H))
            sp[...] = st[d]
            o_val, stn = _hg_chunk(q[...], f[...], v[...], l0_ref[...], l1_ref[...], stp, tri, tri_t, mref)
            o[...] = o_val
            for h in range(H):
                st[d, h] = stn[h]

    tok = lambda d, col: pl.BlockSpec((CHUNK, HG_W), lambda c: (rows[d](c), col))
    par = pl.BlockSpec((1, HG_W), lambda c: (0, 0))
    state = lambda d: pl.BlockSpec((None, H, dv, dk), lambda c: (rows[d](c), 0, 0, 0))
    res = pl.pallas_call(
        body, name=name, grid=(nC,),
        in_specs=[tok(0, 2), tok(0, 3), tok(0, 5), tok(1, 2), tok(1, 4), tok(1, 5), par, par],
        out_specs=[tok(0, 0), tok(1, 0), state(0), state(1)],
        out_shape=[jax.ShapeDtypeStruct((T, H * dv), F32)] * 2 + [jax.ShapeDtypeStruct((nC, H, dv, dk), F32)] * 2,
        scratch_shapes=[pltpu.VMEM((2, H, dv, dk), F32)],
        compiler_params=_params("arbitrary"))(proj, proj, proj, proj, proj, proj, l0, l1)
    return (res[0], res[1]), (res[2], res[3])


def _hg_bwd(proj, l0, l1, sprev, do, name):
    T = proj.shape[0]
    nC = T // CHUNK
    H, dk, dv = 4, 128, 128
    rows = _chunk_rows(nC, True)

    def body(qf, ff, vf, qb, fb, vb, l0_ref, l1_ref, spf, spb, dof, dob,
             dqf, dff, dvf, dqb, dfb, dvb, dl0_ref, dl1_ref, dst):
        @pl.when(pl.program_id(0) == 0)
        def _():
            dst[...] = jnp.zeros_like(dst)
            dl0_ref[...] = jnp.zeros_like(dl0_ref)
            dl1_ref[...] = jnp.zeros_like(dl1_ref)

        for d, (q, f, v, sp, do_ref, dq_ref, df_ref, dv_ref) in enumerate(
                ((qf, ff, vf, spf, dof, dqf, dff, dvf), (qb, fb, vb, spb, dob, dqb, dfb, dvb))):
            tri, tri_t, mref = _tri_consts(d)
            fn = lambda q_, f_, v_, a0, a1, stp: _hg_chunk(q_, f_, v_, a0, a1, stp, tri, tri_t, mref)
            stp = tuple(sp[h] for h in range(H))
            _, vjp = jax.vjp(fn, q[...], f[...], v[...], l0_ref[...], l1_ref[...], stp)
            dq, df, dvv, dl0, dl1, dstp = vjp((do_ref[...], tuple(dst[d, h] for h in range(H))))
            dq_ref[...] = dq
            df_ref[...] = df
            dv_ref[...] = dvv
            dl0_ref[d] += dl0
            dl1_ref[d] += dl1
            for h in range(H):
                dst[d, h] = dstp[h]

    tok = lambda d, col: pl.BlockSpec((CHUNK, HG_W), lambda c: (rows[d](c), col))
    par = pl.BlockSpec((1, HG_W), lambda c: (0, 0))
    acc = pl.BlockSpec((2, 1, HG_W), lambda c: (0, 0, 0))
    state = lambda d: pl.BlockSpec((None, H, dv, dk), lambda c: (rows[d](c), 0, 0, 0))
    res = pl.pallas_call(
        body, name=name, grid=(nC,),
        in_specs=[tok(0, 2), tok(0, 3), tok(0, 5), tok(1, 2), tok(1, 4), tok(1, 5), par, par,
                  state(0), state(1), tok(0, 0), tok(1, 0)],
        out_specs=[tok(0, 0)] * 3 + [tok(1, 0)] * 3 + [acc, acc],
        out_shape=[jax.ShapeDtypeStruct((T, HG_W), F32)] * 6 + [jax.ShapeDtypeStruct((2, 1, HG_W), F32)] * 2,
        scratch_shapes=[pltpu.VMEM((2, H, dv, dk), F32)],
        compiler_params=_params("arbitrary"))(proj, proj, proj, proj, proj, proj, l0, l1, sprev[0], sprev[1], do, do)
    return (res[0], res[3]), (res[1], res[4]), (res[2], res[5]), res[6], res[7]


def _gate_logits(proj, wup, bg, name):
    T = proj.shape[0]
    tm = min(512, T)

    def body(lr_ref, w_ref, b_ref, z_ref, lrb_ref):
        lr = lr_ref[...].astype(BF16)
        lrb_ref[...] = lr
        for d in range(2):
            z_ref[d] = _dg(lr, w_ref[d], 1, 0) + b_ref[d]

    return pl.pallas_call(
        body, name=name, grid=(T // tm,),
        in_specs=[pl.BlockSpec((tm, LANES), lambda i: (i, 24)), pl.BlockSpec((2, LANES, 512), lambda i: (0, 0, 0)),
                  pl.BlockSpec((2, 1, 512), lambda i: (0, 0, 0))],
        out_specs=[pl.BlockSpec((2, tm, 512), lambda i: (0, i, 0)), pl.BlockSpec((tm, LANES), lambda i: (i, 0))],
        out_shape=[jax.ShapeDtypeStruct((2, T, 512), F32), jax.ShapeDtypeStruct((T, LANES), BF16)],
        compiler_params=_params("parallel"))(proj, wup, bg)


def _gate_logits_bwd(dz, wup, name):
    T = dz[0].shape[0]
    tm = min(512, T)

    def body(dzf_ref, dzb_ref, w_ref, dlr_ref, db_ref, dzb16_ref):
        @pl.when(pl.program_id(0) == 0)
        def _():
            db_ref[...] = jnp.zeros_like(db_ref)

        acc = jnp.zeros((tm, LANES), F32)
        for d, dz_ref in enumerate((dzf_ref, dzb_ref)):
            g = dz_ref[...]
            gb = g.astype(BF16)
            dzb16_ref[d] = gb
            acc = acc + _dg(gb, w_ref[d], 1, 1)
            db_ref[d] += jnp.sum(g, axis=0, keepdims=True)
        dlr_ref[...] = acc

    tok = pl.BlockSpec((tm, 512), lambda i: (i, 0))
    return pl.pallas_call(
        body, name=name, grid=(T // tm,),
        in_specs=[tok, tok, pl.BlockSpec((2, LANES, 512), lambda i: (0, 0, 0))],
        out_specs=[pl.BlockSpec((tm, LANES), lambda i: (i, 0)), pl.BlockSpec((2, 1, 512), lambda i: (0, 0, 0)),
                   pl.BlockSpec((2, tm, 512), lambda i: (0, i, 0))],
        out_shape=[jax.ShapeDtypeStruct((T, LANES), F32), jax.ShapeDtypeStruct((2, 1, 512), F32),
                   jax.ShapeDtypeStruct((2, T, 512), BF16)],
        compiler_params=_params("arbitrary"))(dz[0], dz[1], wup)


def _gla_fwd(proj, z, name):
    T = proj.shape[0]
    nC = T // CHUNK
    H, dk, dv = 4, 128, 256
    rows = _chunk_rows(nC, False)

    def body(qf, kf, vf, zf, qb, kb, vb, zb, of, ob, spf, spb, st):
        @pl.when(pl.program_id(0) == 0)
        def _():
            st[...] = jnp.zeros_like(st)

        for d, (q, k, v, z_ref, o, sp) in enumerate(((qf, kf, vf, zf, of, spf), (qb, kb, vb, zb, ob, spb))):
            tri, tri_t, mref = _tri_consts(d)
            stp = tuple(st[d, h] for h in range(H))
            sp[...] = st[d]
            o_val, stn = _gla_chunk(q[...], k[...], v[...], z_ref[...], stp, tri, tri_t, mref)
            o[...] = o_val
            for h in range(H):
                st[d, h] = stn[h]

    tok = lambda d, w, col: pl.BlockSpec((CHUNK, w), lambda c: (rows[d](c), col))
    gate = lambda d: pl.BlockSpec((None, CHUNK, 512), lambda c: (d, rows[d](c), 0))
    state = lambda d: pl.BlockSpec((None, H, dv, dk), lambda c: (rows[d](c), 0, 0, 0))
    res = pl.pallas_call(
        body, name=name, grid=(nC,),
        in_specs=[tok(0, 512, 0), tok(0, 512, 1), tok(0, 1024, 1), gate(0),
                  tok(1, 512, 0), tok(1, 512, 1), tok(1, 1024, 1), gate(1)],
        out_specs=[tok(0, H * dv, 0), tok(1, H * dv, 0), state(0), state(1)],
        out_shape=[jax.ShapeDtypeStruct((T, H * dv), F32)] * 2 + [jax.ShapeDtypeStruct((nC, H, dv, dk), F32)] * 2,
        scratch_shapes=[pltpu.VMEM((2, H, dv, dk), F32)],
        compiler_params=_params("arbitrary"))(proj, proj, proj, z, proj, proj, proj, z)
    return (res[0], res[1]), (res[2], res[3])


def _gla_bwd(proj, z, sprev, do, name):
    T = proj.shape[0]
    nC = T // CHUNK
    H, dk, dv = 4, 128, 256
    rows = _chunk_rows(nC, True)

    def body(qf, kf, vf, zf, qb, kb, vb, zb, spf, spb, dof, dob,
             dqf, dkf, dvf, dzf, dqb, dkb, dvb, dzb, dst):
        @pl.when(pl.program_id(0) == 0)
        def _():
            dst[...] = jnp.zeros_like(dst)

        for d, (q, k, v, z_ref, sp, do_ref, dq_ref, dk_ref, dv_ref, dz_ref) in enumerate(
                ((qf, kf, vf, zf, spf, dof, dqf, dkf, dvf, dzf), (qb, kb, vb, zb, spb, dob, dqb, dkb, dvb, dzb))):
            tri, tri_t, mref = _tri_consts(d)
            fn = lambda q_, k_, v_, z_, stp: _gla_chunk(q_, k_, v_, z_, stp, tri, tri_t, mref)
            stp = tuple(sp[h] for h in range(H))
            _, vjp = jax.vjp(fn, q[...], k[...], v[...], z_ref[...], stp)
            dq, dkk, dvv, dzz, dstp = vjp((do_ref[...], tuple(dst[d, h] for h in range(H))))
            dq_ref[...] = dq
            dk_ref[...] = dkk
            dv_ref[...] = dvv
            dz_ref[...] = dzz
            for h in range(H):
                dst[d, h] = dstp[h]

    tok = lambda d, w, col: pl.BlockSpec((CHUNK, w), lambda c: (rows[d](c), col))
    gate = lambda d: pl.BlockSpec((None, CHUNK, 512), lambda c: (d, rows[d](c), 0))
    state = lambda d: pl.BlockSpec((None, H, dv, dk), lambda c: (rows[d](c), 0, 0, 0))
    outs = lambda d: [tok(d, 512, 0), tok(d, 512, 0), tok(d, 1024, 0), tok(d, 512, 0)]
    shapes = [jax.ShapeDtypeStruct((T, 512), F32), jax.ShapeDtypeStruct((T, 512), F32),
              jax.ShapeDtypeStruct((T, 1024), F32), jax.ShapeDtypeStruct((T, 512), F32)]
    res = pl.pallas_call(
        body, name=name, grid=(nC,),
        in_specs=[tok(0, 512, 0), tok(0, 512, 1), tok(0, 1024, 1), gate(0),
                  tok(1, 512, 0), tok(1, 512, 1), tok(1, 1024, 1), gate(1),
                  state(0), state(1), tok(0, H * dv, 0), tok(1, H * dv, 0)],
        out_specs=outs(0) + outs(1), out_shape=shapes + shapes,
        scratch_shapes=[pltpu.VMEM((2, H, dv, dk), F32)],
        compiler_params=_params("arbitrary"))(proj, proj, proj, z, proj, proj, proj, z, sprev[0], sprev[1], do, do)
    return (res[0], res[4]), (res[1], res[5]), (res[2], res[6]), (res[3], res[7])


def _l0_combine_fwd(hs, proj, o, gain, name):
    T = proj.shape[0]
    tm = min(512, T)

    def body(hf, hb, ga, of, ob, g, gn, out):
        out[...] = _l0_combine(hf[...], hb[...], ga[...], of[...], ob[...], g[...], gn[...]).astype(BF16)

    two = lambda lead: pl.BlockSpec((None, tm, 512), lambda i: (lead, i, 0))
    tok = pl.BlockSpec((tm, 512), lambda i: (i, 0))
    return pl.pallas_call(
        body, name=name, grid=(T // tm,),
        in_specs=[two(0), two(1), pl.BlockSpec((tm, 512), lambda i: (i, 1)), tok, tok,
                  pl.BlockSpec((tm, 512), lambda i: (i, 6)), pl.BlockSpec((1, 512), lambda i: (0, 0))],
        out_specs=pl.BlockSpec((tm, 1024), lambda i: (i, 0)),
        out_shape=jax.ShapeDtypeStruct((T, 1024), BF16),
        compiler_params=_params("parallel"))(hs, hs, proj, o[0], o[1], proj, gain)


def _l0_combine_bwd(hs, proj, o, gain, dmix, name):
    T = proj.shape[0]
    tm = min(512, T)

    def body(hf, hb, ga, of, ob, g, gn, dm, dho_ref, dga_ref, do_ref, dg_ref, dgn_ref):
        @pl.when(pl.program_id(0) == 0)
        def _():
            dgn_ref[...] = jnp.zeros_like(dgn_ref)

        _, vjp = jax.vjp(_l0_combine, hf[...], hb[...], ga[...], of[...], ob[...], g[...], gn[...])
        dhf, _, dga, dof, _, dg, dgn = vjp(dm[...])
        dho_ref[...] = dhf
        dga_ref[...] = dga
        do_ref[...] = dof
        dg_ref[...] = dg
        dgn_ref[...] += dgn

    two = lambda lead: pl.BlockSpec((None, tm, 512), lambda i: (lead, i, 0))
    tok = lambda: pl.BlockSpec((tm, 512), lambda i: (i, 0))
    return pl.pallas_call(
        body, name=name, grid=(T // tm,),
        in_specs=[two(0), two(1), pl.BlockSpec((tm, 512), lambda i: (i, 1)), tok(), tok(),
                  pl.BlockSpec((tm, 512), lambda i: (i, 6)), pl.BlockSpec((1, 512), lambda i: (0, 0)),
                  pl.BlockSpec((tm, 1024), lambda i: (i, 0))],
        out_specs=[tok(), tok(), tok(), tok(), pl.BlockSpec((1, 512), lambda i: (0, 0))],
        out_shape=[jax.ShapeDtypeStruct((T, 512), F32)] * 4 + [jax.ShapeDtypeStruct((1, 512), F32)],
        compiler_params=_params("arbitrary"))(hs, hs, proj, o[0], o[1], proj, gain, dmix)


def _l0_assemble(dxa, dga, dq, df, dv, dg, name):
    T = dxa.shape[0]
    tm = min(512, T)

    def body(xa, ga, q0, q1, f0, f1, v0, v1, g, out):
        out[...] = jnp.concatenate([xa[...], ga[...], q0[...] + q1[...], f0[...], f1[...], v0[...] + v1[...],
                                    g[...]], axis=1).astype(BF16)

    tok = lambda: pl.BlockSpec((tm, 512), lambda i: (i, 0))
    return pl.pallas_call(
        body, name=name, grid=(T // tm,),
        in_specs=[tok() for _ in range(9)],
        out_specs=pl.BlockSpec((tm, AB_IN), lambda i: (i, 0)),
        out_shape=jax.ShapeDtypeStruct((T, AB_IN), BF16),
        compiler_params=_params("parallel"))(dxa, dga, dq[0], dq[1], df[0], df[1], dv[0], dv[1], dg)


def _l1_combine_fwd(o, proj, gain, name):
    T = proj.shape[0]
    tm = min(512, T)

    def body(of, ob, r, gn, out):
        out[...] = _l1_combine(of[...], ob[...], r[...], gn[...]).astype(BF16)

    tok = pl.BlockSpec((tm, 1024), lambda i: (i, 0))
    return pl.pallas_call(
        body, name=name, grid=(T // tm,),
        in_specs=[tok, tok, pl.BlockSpec((tm, 1024), lambda i: (i, 2)), pl.BlockSpec((1, 1024), lambda i: (0, 0))],
        out_specs=pl.BlockSpec((tm, 1024), lambda i: (i, 0)),
        out_shape=jax.ShapeDtypeStruct((T, 1024), BF16),
        compiler_params=_params("parallel"))(o[0], o[1], proj, gain)


def _l1_combine_bwd(o, proj, gain, dmix, name):
    T = proj.shape[0]
    tm = min(512, T)

    def body(of, ob, r, gn, dm, do_ref, dr_ref, dgn_ref):
        @pl.when(pl.program_id(0) == 0)
        def _():
            dgn_ref[...] = jnp.zeros_like(dgn_ref)

        _, vjp = jax.vjp(_l1_combine, of[...], ob[...], r[...], gn[...])
        dof, _, dr, dgn = vjp(dm[...])
        do_ref[...] = dof
        dr_ref[...] = dr
        dgn_ref[...] += dgn

    tok = lambda: pl.BlockSpec((tm, 1024), lambda i: (i, 0))
    return pl.pallas_call(
        body, name=name, grid=(T // tm,),
        in_specs=[tok(), tok(), pl.BlockSpec((tm, 1024), lambda i: (i, 2)),
                  pl.BlockSpec((1, 1024), lambda i: (0, 0)), tok()],
        out_specs=[tok(), tok(), pl.BlockSpec((1, 1024), lambda i: (0, 0))],
        out_shape=[jax.ShapeDtypeStruct((T, 1024), F32)] * 2 + [jax.ShapeDtypeStruct((1, 1024), F32)],
        compiler_params=_params("arbitrary"))(o[0], o[1], proj, gain, dmix)


def _l1_assemble(dq, dk, dv, dr, dlr, name):
    T = dr.shape[0]
    tm = min(512, T)

    def body(q0, q1, k0, k1, v0, v1, r, a, out):
        out[...] = jnp.concatenate([q0[...] + q1[...], k0[...] + k1[...], v0[...] + v1[...], r[...], a[...]],
                                   axis=1).astype(BF16)

    tok = lambda w: pl.BlockSpec((tm, w), lambda i: (i, 0))
    return pl.pallas_call(
        body, name=name, grid=(T // tm,),
        in_specs=[tok(512), tok(512), tok(512), tok(512), tok(1024), tok(1024), tok(1024), tok(LANES)],
        out_specs=pl.BlockSpec((tm, GLA_IN_PAD), lambda i: (i, 0)),
        out_shape=jax.ShapeDtypeStruct((T, GLA_IN_PAD), BF16),
        compiler_params=_params("parallel"))(dq[0], dq[1], dk[0], dk[1], dv[0], dv[1], dr, dlr)


HBM_SPEC = pl.BlockSpec(memory_space=pltpu.HBM)


def _place():
    x, y, c = lax.axis_index("x"), lax.axis_index("y"), lax.axis_index("c")
    return x, y, c


def _allgather_hbm(shards, name):
    n = len(shards)

    def body(*refs):
        ins, outs = refs[:n], refs[n:2 * n]
        send_sems, recv_sems, local_sems = refs[2 * n:]
        x, y, c = _place()
        me, sibling = (x, y, c), (x, y, 1 - c)
        chips = [(1 - x, y), (x, 1 - y), (1 - x, 1 - y)]

        def slot(a, p):
            return outs[a].at[4 * p[0] + 2 * p[1] + p[2]]

        def copy(a, k, block, to, src=None):
            return pltpu.make_async_remote_copy(
                src_ref=slot(a, block) if src is None else src, dst_ref=slot(a, block),
                send_sem=send_sems.at[a * 7 + k], recv_sem=recv_sems.at[a * 7 + k],
                device_id=to, device_id_type=MESH)

        mine = [pltpu.make_async_copy(ins[a], slot(a, me), local_sems.at[a]) for a in range(n)]
        for cp in mine:
            cp.start()
        first = []
        for a in range(n):
            first.append(copy(a, 0, me, sibling, src=ins[a]))
            first += [copy(a, 1 + j, me, (*chip, c), src=ins[a]) for j, chip in enumerate(chips)]
        for cp in first:
            cp.start()
        passed = []
        for j, chip in enumerate(chips):
            for a in range(n):
                copy(a, 1 + j, (*chip, c), me).wait_recv()
                cp = copy(a, 4 + j, (*chip, c), sibling)
                cp.start()
                passed.append(cp)
        for a in range(n):
            copy(a, 0, sibling, me).wait_recv()
            for j, chip in enumerate(chips):
                copy(a, 4 + j, (*chip, 1 - c), me).wait_recv()
        for cp in first + passed:
            cp.wait_send()
        for cp in mine:
            cp.wait()

    return pl.pallas_call(
        body, name=name,
        in_specs=[HBM_SPEC] * n, out_specs=[HBM_SPEC] * n,
        out_shape=[jax.ShapeDtypeStruct((N_DEV,) + s.shape, s.dtype) for s in shards],
        scratch_shapes=[pltpu.SemaphoreType.DMA((7 * n,)), pltpu.SemaphoreType.DMA((7 * n,)),
                        pltpu.SemaphoreType.DMA((n,))],
        compiler_params=pltpu.CompilerParams(has_side_effects=True))(*shards)


def _allgather_vmem(x_shard, name, reduce=False):
    m_per, n = x_shard.shape

    def body(x_ref, out_ref, *rest):
        if reduce:
            sum_ref, send_sems, recv_sems, local_sem = rest
        else:
            send_sems, recv_sems, local_sem = rest
        x, y, c = _place()
        me, sibling = (x, y, c), (x, y, 1 - c)
        chips = [(1 - x, y), (x, 1 - y), (1 - x, 1 - y)]

        def rows(px, py, pc):
            return out_ref.at[pl.ds((4 * px + 2 * py + pc) * m_per, m_per), :]

        def copy(k, block, to, src=None):
            return pltpu.make_async_remote_copy(
                src_ref=rows(*block) if src is None else src, dst_ref=rows(*block),
                send_sem=send_sems.at[k], recv_sem=recv_sems.at[k], device_id=to, device_id_type=MESH)

        mine = pltpu.make_async_copy(x_ref, rows(*me), local_sem)
        mine.start()
        first = [copy(0, me, sibling, src=x_ref)]
        first += [copy(1 + j, me, (*chip, c), src=x_ref) for j, chip in enumerate(chips)]
        for cp in first:
            cp.start()
        passed = [copy(4 + j, (*chip, c), sibling) for j, chip in enumerate(chips)]
        for j, chip in enumerate(chips):
            copy(1 + j, (*chip, c), me).wait_recv()
            passed[j].start()
        copy(0, sibling, me).wait_recv()
        for j, chip in enumerate(chips):
            copy(4 + j, (*chip, 1 - c), me).wait_recv()
        for cp in first + passed:
            cp.wait_send()
        mine.wait()
        if reduce:
            acc = out_ref[pl.ds(0, m_per), :]
            for j in range(1, N_DEV):
                acc = acc + out_ref[pl.ds(j * m_per, m_per), :]
            sum_ref[...] = acc

    vm = pl.BlockSpec(memory_space=pltpu.VMEM)
    out_shape = [jax.ShapeDtypeStruct((N_DEV * m_per, n), x_shard.dtype)]
    if reduce:
        out_shape.append(jax.ShapeDtypeStruct((m_per, n), x_shard.dtype))
    res = pl.pallas_call(
        body, name=name, in_specs=[vm], out_specs=[vm] * len(out_shape), out_shape=out_shape,
        scratch_shapes=[pltpu.SemaphoreType.DMA((7,)), pltpu.SemaphoreType.DMA((7,)), pltpu.SemaphoreType.DMA],
        compiler_params=pltpu.CompilerParams(has_side_effects=True, vmem_limit_bytes=VMEM_LIMIT))(x_shard)
    return res[1] if reduce else res[0]


SEM_SPEC = pl.BlockSpec(memory_space=pltpu.SEMAPHORE)
DATAFLOW_EFFECT = pltpu.SideEffectType.DATAFLOW_SIDE_EFFECTING


def _copies(plan, srcs, lands, send_sems, recv_sems):
    x, y, c = _place()
    return [pltpu.make_async_remote_copy(src_ref=s, dst_ref=d, send_sem=send_sems.at[k], recv_sem=recv_sems.at[k],
                                         device_id=dev, device_id_type=MESH)
            for k, (s, d, dev) in enumerate(plan(srcs, lands, x, y, c))]


def _copies_start(plan, n_copies, srcs, lands, name):
    ns, nl = len(srcs), len(lands)

    def body(*refs):
        send_sems, recv_sems = refs[ns + nl], refs[ns + nl + 1]
        for cp in _copies(plan, refs[:ns], refs[ns:ns + nl], send_sems, recv_sems):
            cp.start()
        refs[-1][...] = jnp.zeros_like(refs[-1])

    arrays = list(srcs) + list(lands)
    res = pl.pallas_call(
        body, name=name,
        in_specs=[HBM_SPEC] * (ns + nl),
        out_specs=tuple([SEM_SPEC, SEM_SPEC] + [HBM_SPEC] * (ns + nl) + [pl.BlockSpec(memory_space=pltpu.VMEM)]),
        out_shape=tuple([pltpu.SemaphoreType.DMA((n_copies,)), pltpu.SemaphoreType.DMA((n_copies,))]
                        + [pltpu.HBM(a.shape, a.dtype) for a in arrays]
                        + [jax.ShapeDtypeStruct((SUBLANES, LANES), F32)]),
        input_output_aliases={i: 2 + i for i in range(ns + nl)},
        compiler_params=pltpu.CompilerParams(has_side_effects=DATAFLOW_EFFECT),
    )(*[pltpu.with_memory_space_constraint(a, pltpu.HBM) for a in arrays])
    return res[0], res[1], list(res[2:2 + ns]), list(res[2 + ns:2 + ns + nl]), res[-1]


def _copies_wait(plan, started, after, name):
    send_sems, recv_sems, srcs, lands, _ = started
    ns, nl = len(srcs), len(lands)

    def body(*refs):
        for cp in _copies(plan, refs[:ns], refs[ns:ns + nl], refs[ns + nl], refs[ns + nl + 1]):
            cp.wait_send()
            cp.wait_recv()

    arrays = list(srcs) + list(lands)
    res = pl.pallas_call(
        body, name=name,
        in_specs=[HBM_SPEC] * (ns + nl) + [SEM_SPEC, SEM_SPEC, pl.BlockSpec(memory_space=pl.ANY)],
        out_specs=tuple([HBM_SPEC] * (ns + nl)),
        out_shape=tuple(pltpu.HBM(a.shape, a.dtype) for a in arrays),
        input_output_aliases={i: i for i in range(ns + nl)},
        compiler_params=pltpu.CompilerParams(has_side_effects=DATAFLOW_EFFECT),
    )(*arrays, send_sems, recv_sems, after)
    return list(res[:ns]), list(res[ns:])


def _after(token, value):
    return value + token[0:1, 0:1].astype(value.dtype)


def _chips(x, y):
    return [(1 - x, y), (x, 1 - y), (1 - x, 1 - y)]


def _plan_gather_first(srcs, lands, x, y, c):
    me = 4 * x + 2 * y + c
    out = []
    for s, l in zip(srcs, lands):
        out.append((s, l.at[me], (x, y, 1 - c)))
        out += [(s, l.at[me], (*chip, c)) for chip in _chips(x, y)]
    return out


def _plan_gather_pass(srcs, lands, x, y, c):
    out = []
    for l in lands:
        for chip in _chips(x, y):
            slot = l.at[4 * chip[0] + 2 * chip[1] + c]
            out.append((slot, slot, (x, y, 1 - c)))
    return out


def _plan_grads_sibling(srcs, lands, x, y, c):
    return [(s.at[2 * q + (1 - c)], l.at[q], (x, y, 1 - c)) for s, l in zip(srcs, lands) for q in range(4)]


def _plan_grads_chips(srcs, lands, x, y, c):
    return [(s.at[2 * chip[0] + chip[1]], l.at[k], (*chip, c))
            for s, l in zip(srcs, lands) for k, chip in enumerate(_chips(x, y))]


def _landing(n_slots, like):
    return [lax.empty((n_slots,) + a.shape[1:], a.dtype) for a in like]


def _chip_partial(g, r1, place, name):
    _, R, C = g.shape
    tr = min(256, R)
    assert R % tr == 0

    def body(pl_ref, g_ref, r_ref, pb_ref, pm_ref):
        q = pl.program_id(1)
        s = g_ref[...] + r_ref[...]
        pb_ref[...] = s.astype(BF16)

        @pl.when(q == pl_ref[1])
        def _():
            pm_ref[...] = s

    grid_spec = pltpu.PrefetchScalarGridSpec(
        num_scalar_prefetch=1, grid=(R // tr, 4),
        in_specs=[pl.BlockSpec((None, tr, C), lambda r, q, p: (2 * q + p[0], r, 0)),
                  pl.BlockSpec((None, tr, C), lambda r, q, p: (q, r, 0))],
        out_specs=[pl.BlockSpec((None, tr, C), lambda r, q, p: (q, r, 0)),
                   pl.BlockSpec((tr, C), lambda r, q, p: (r, 0))])
    return pl.pallas_call(
        body, name=name, grid_spec=grid_spec,
        out_shape=[jax.ShapeDtypeStruct((4, R, C), BF16), jax.ShapeDtypeStruct((R, C), F32)],
        compiler_params=_params("parallel", "arbitrary"))(place, g, r1)


def _adamw(w, gparts, m, v, name):
    R, C = w.shape
    tr = min(256, R)
    assert R % tr == 0
    g0, g3 = gparts

    def body(w_ref, g0_ref, *rest):
        if g3 is not None:
            g3_ref, m_ref, v_ref, go, do, mo, vo = rest
        else:
            m_ref, v_ref, go, do, mo, vo = rest
        g = g0_ref[...]
        if g3 is not None:
            for k in range(3):
                g = g + g3_ref[k].astype(F32)
        wv = w_ref[...]
        mn = ADAM_B1 * m_ref[...] + (1.0 - ADAM_B1) * g
        vn = ADAM_B2 * v_ref[...] + (1.0 - ADAM_B2) * jnp.square(g)
        m_hat = mn / (1.0 - ADAM_B1 ** ADAM_STEP)
        v_hat = vn / (1.0 - ADAM_B2 ** ADAM_STEP)
        go[...] = g
        do[...] = -ADAM_LR * (m_hat / (jnp.sqrt(v_hat) + ADAM_EPS) + ADAM_WD * wv)
        mo[...] = mn
        vo[...] = vn

    blk = pl.BlockSpec((tr, C), lambda i: (i, 0))
    in_specs = [blk, blk] + ([pl.BlockSpec((3, tr, C), lambda i: (0, i, 0))] if g3 is not None else []) + [blk, blk]
    args = [w, g0] + ([g3] if g3 is not None else []) + [m, v]
    return pl.pallas_call(
        body, name=name, grid=(R // tr,), in_specs=in_specs, out_specs=[blk] * 4,
        out_shape=[jax.ShapeDtypeStruct((R, C), F32)] * 4,
        compiler_params=_params("parallel"))(*args)


SMALL_SHARDED = ("rg_conv_w", "rg_b_a", "rg_b_x", "rg_lambda", "gla_w_gate_up", "gla_b_gate", "gla_norm")
SMALL_REPLICATED = ("norm_mix", "norm_mlp", "norm_final", "rg_conv_b", "rg_w_a", "rg_w_x", "hg_lb_logits", "hg_norm")
WEIGHT_NAMES = ("norm_mix", "norm_mlp", "norm_final", "mlp_w1", "mlp_w2", "ab_w_in", "ab_w_out", "rg_conv_w",
                "rg_conv_b", "rg_w_a", "rg_b_a", "rg_w_x", "rg_b_x", "rg_lambda", "hg_lb_logits", "hg_norm",
                "gla_w_in", "gla_w_out", "gla_w_gate_up", "gla_b_gate", "gla_norm")


def _rows128(a):
    return a.reshape(-1, LANES)


def _part_rows(a):
    return -(-(a.size // LANES) // SUBLANES) * SUBLANES


def _pack_rows(arrays, pad_to=SUBLANES):
    parts = [jnp.pad(_rows128(a), ((0, _part_rows(a) - a.size // LANES), (0, 0))) for a in arrays]
    total = sum(p.shape[0] for p in parts)
    extra = (-total) % pad_to
    if extra:
        parts.append(jnp.zeros((extra, LANES), parts[0].dtype))
    return jnp.concatenate(parts, axis=0)


def _unshard_last(g, shape_local):
    nd = len(shape_local)
    t = g.reshape((N_DEV,) + tuple(shape_local))
    t = jnp.moveaxis(t, 0, nd - 1)
    return t.reshape(tuple(shape_local[:-1]) + (N_DEV * shape_local[-1],))


def _block_diag(w):
    eye = jnp.eye(8, dtype=w.dtype)
    return (w[:, :, :, None, :] * eye[None, :, None, :, None]).reshape(2, RG_W, RG_W)


def _block_diag_extract(dw):
    t = dw.reshape(2, 8, 64, 8, 64)
    return jnp.moveaxis(jnp.diagonal(t, axis1=1, axis2=3), -1, 1)


def kernel(x, norm_mix, norm_mlp, norm_final, mlp_w1, mlp_w2, ab_w_in, ab_w_out, rg_conv_w, rg_conv_b, rg_w_a, rg_b_a, rg_w_x, rg_b_x, rg_lambda, hg_lb_logits, hg_norm, gla_w_in, gla_w_out, gla_w_gate_up, gla_b_gate, gla_norm, loss_target, m_norm_mix, m_norm_mlp, m_norm_final, m_mlp_w1, m_mlp_w2, m_ab_w_in, m_ab_w_out, m_rg_conv_w, m_rg_conv_b, m_rg_w_a, m_rg_b_a, m_rg_w_x, m_rg_b_x, m_rg_lambda, m_hg_lb_logits, m_hg_norm, m_gla_w_in, m_gla_w_out, m_gla_w_gate_up, m_gla_b_gate, m_gla_norm, v_norm_mix, v_norm_mlp, v_norm_final, v_mlp_w1, v_mlp_w2, v_ab_w_in, v_ab_w_out, v_rg_conv_w, v_rg_conv_b, v_rg_w_a, v_rg_b_a, v_rg_w_x, v_rg_b_x, v_rg_lambda, v_hg_lb_logits, v_hg_norm, v_gla_w_in, v_gla_w_out, v_gla_w_gate_up, v_gla_b_gate, v_gla_norm):
    w_loc = dict(norm_mix=norm_mix, norm_mlp=norm_mlp, norm_final=norm_final, mlp_w1=mlp_w1, mlp_w2=mlp_w2,
                 ab_w_in=ab_w_in, ab_w_out=ab_w_out, rg_conv_w=rg_conv_w, rg_conv_b=rg_conv_b, rg_w_a=rg_w_a,
                 rg_b_a=rg_b_a, rg_w_x=rg_w_x, rg_b_x=rg_b_x, rg_lambda=rg_lambda, hg_lb_logits=hg_lb_logits,
                 hg_norm=hg_norm, gla_w_in=gla_w_in, gla_w_out=gla_w_out, gla_w_gate_up=gla_w_gate_up,
                 gla_b_gate=gla_b_gate, gla_norm=gla_norm)
    m_loc = dict(norm_mix=m_norm_mix, norm_mlp=m_norm_mlp, norm_final=m_norm_final, mlp_w1=m_mlp_w1,
                 mlp_w2=m_mlp_w2, ab_w_in=m_ab_w_in, ab_w_out=m_ab_w_out, rg_conv_w=m_rg_conv_w,
                 rg_conv_b=m_rg_conv_b, rg_w_a=m_rg_w_a, rg_b_a=m_rg_b_a, rg_w_x=m_rg_w_x, rg_b_x=m_rg_b_x,
                 rg_lambda=m_rg_lambda, hg_lb_logits=m_hg_lb_logits, hg_norm=m_hg_norm, gla_w_in=m_gla_w_in,
                 gla_w_out=m_gla_w_out, gla_w_gate_up=m_gla_w_gate_up, gla_b_gate=m_gla_b_gate,
                 gla_norm=m_gla_norm)
    v_loc = dict(norm_mix=v_norm_mix, norm_mlp=v_norm_mlp, norm_final=v_norm_final, mlp_w1=v_mlp_w1,
                 mlp_w2=v_mlp_w2, ab_w_in=v_ab_w_in, ab_w_out=v_ab_w_out, rg_conv_w=v_rg_conv_w,
                 rg_conv_b=v_rg_conv_b, rg_w_a=v_rg_w_a, rg_b_a=v_rg_b_a, rg_w_x=v_rg_w_x, rg_b_x=v_rg_b_x,
                 rg_lambda=v_rg_lambda, hg_lb_logits=v_hg_lb_logits, hg_norm=v_hg_norm, gla_w_in=v_gla_w_in,
                 gla_w_out=v_gla_w_out, gla_w_gate_up=v_gla_w_gate_up, gla_b_gate=v_gla_b_gate,
                 gla_norm=v_gla_norm)

    T = x.shape[1]
    h0 = x.reshape(T, D_MODEL)
    target = loss_target.reshape(T, D_MODEL)
    ax, ay, ac = lax.axis_index("x"), lax.axis_index("y"), lax.axis_index("c")
    dev = 4 * ax + 2 * ay + ac
    place = jnp.stack([ac, 2 * ax + ay]).astype(jnp.int32)

    (abin_g,) = _allgather_hbm([ab_w_in[0].astype(BF16)], "ag_first")
    wab_in = jnp.transpose(abin_g, (1, 0, 2)).reshape(D_MODEL, AB_IN)
    rest_shards = [mlp_w1[0].astype(BF16), mlp_w2[0].astype(BF16), gla_w_in[0].astype(BF16),
                   gla_w_out[0].astype(BF16), mlp_w1[1].astype(BF16), mlp_w2[1].astype(BF16),
                   ab_w_out[0].astype(BF16)]
    ag_started = _copies_start(_plan_gather_first, 4 * len(rest_shards), rest_shards,
                               _landing(N_DEV, [s[None] for s in rest_shards]), "ag_rest_start")

    small_local = [w_loc[n] for n in SMALL_SHARDED]
    small_g = _allgather_vmem(_pack_rows(small_local, 8), "ag_small")
    small_g = small_g.reshape(N_DEV, -1, LANES)
    full = {}
    off = 0
    for n, a in zip(SMALL_SHARDED, small_local):
        full[n] = _unshard_last(small_g[:, off:off + a.size // LANES].reshape(N_DEV, a.size), a.shape)
        off += _part_rows(a)
    conv_w = full["rg_conv_w"][0]
    b_a, b_x, lam = full["rg_b_a"][0], full["rg_b_x"][0], full["rg_lambda"][0]
    w_up, b_gate, g_norm = full["gla_w_gate_up"][0], full["gla_b_gate"][0], full["gla_norm"]

    cw8 = jnp.pad(conv_w, ((0, 4), (0, 0)))
    wbd = jnp.concatenate([_block_diag(rg_w_a[0]), _block_diag(rg_w_x[0])], axis=2).astype(BF16)
    rg_bias = jnp.concatenate([b_a, b_x], axis=1).reshape(2, 1, 2 * RG_W)
    lam3 = lam.reshape(2, 1, RG_W)
    l0, l1 = hg_lb_logits[0:1], hg_lb_logits[1:2]
    wup_pad = jnp.zeros((2, LANES, 512), F32).at[0, 0:16].set(w_up[0]).at[1, 16:32].set(w_up[1])
    bg3 = b_gate.reshape(2, 1, 512)
    nmix0, nmix1 = norm_mix[0:1], norm_mix[1:2]
    nmlp0, nmlp1 = norm_mlp[0:1], norm_mlp[1:2]
    nfin = norm_final.reshape(1, D_MODEL)

    proj0, y0 = _norm_matmul(h0, _after(ag_started[4], nmix0), wab_in, "l0_in_proj")
    xc = _rg_conv_fwd(proj0, cw8, rg_conv_b, "rg_conv")
    hs = _rg_scan_fwd(xc, wbd, rg_bias, lam3, "rg_scan")
    o_hg, s_hg = _hg_fwd(proj0, l0, l1, "hg_chunks")
    rest_shards, rest_lands = _copies_wait(_plan_gather_first, ag_started, hs, "ag_rest_wait")
    pass_started = _copies_start(_plan_gather_pass, 3 * len(rest_lands), [], rest_lands, "ag_pass_start")
    mixin0 = _l0_combine_fwd(hs, proj0, o_hg, _after(pass_started[4], hg_norm), "l0_combine")
    _, rest_g = _copies_wait(_plan_gather_pass, pass_started, mixin0, "ag_pass_wait")
    rest_g = [lax.dynamic_update_index_in_dim(g, s, dev, 0) for g, s in zip(rest_g, rest_shards)]
    wab_out = rest_g[6].reshape(D_MODEL, D_MODEL)
    h1 = _matmul_res(mixin0, wab_out, h0, "l0_out_proj")
    w1g = (rest_g[0], rest_g[4])
    w2f = (rest_g[1].reshape(D_FF, D_MODEL), rest_g[5].reshape(D_FF, D_MODEL))
    wgla_in = jnp.pad(jnp.transpose(rest_g[2], (1, 0, 2)).reshape(D_MODEL, GLA_IN),
                      ((0, 0), (0, GLA_IN_PAD - GLA_IN)))
    wgla_out = rest_g[3].reshape(D_MODEL, D_MODEL)
    h2, pre0, ym0 = _mlp_fwd(h1, nmlp0, w1g[0], w2f[0], "mlp0")
    proj1, y1 = _norm_matmul(h2, nmix1, wgla_in, "l1_in_proj")
    z_gate, lr_b = _gate_logits(proj1, wup_pad, bg3, "gla_gate_logits")
    o_gla, s_gla = _gla_fwd(proj1, z_gate, "gla_chunks")
    mixin1 = _l1_combine_fwd(o_gla, proj1, g_norm, "l1_combine")
    h3 = _matmul_res(mixin1, wgla_out, h2, "l1_out_proj")
    h4, pre1, ym1 = _mlp_fwd(h3, nmlp1, w1g[1], w2f[1], "mlp1")
    loss_blk, dh4, dh4b, d_nfin = _final_loss(h4, nfin, target, "final_loss")
    loss = lax.psum(loss_blk[0, 0], ("x", "y", "c"))

    dh3, dh3b, dpre1, act1, d_nmlp1 = _mlp_bwd(dh4, h3, nmlp1, pre1, w1g[1], w2f[1], "mlp1_bwd")
    g_w1_1 = _wgrad(ym1, dpre1, 512, "mlp1_dw1", sharded_cols=True)
    g_w2_1 = _wgrad(act1, dh4b, 1024, "mlp1_dw2")
    dmixin1 = _dgrad(dh3b, wgla_out, "l1_out_dgrad")
    g_gla_out = _wgrad(mixin1, dh3b, 1024, "l1_out_dw")
    do_gla, dr, d_gnorm = _l1_combine_bwd(o_gla, proj1, g_norm, dmixin1, "l1_combine_bwd")
    dq1, dk1, dv1, dz_gate = _gla_bwd(proj1, z_gate, s_gla, do_gla, "gla_chunks_bwd")
    dlr1, d_bg, dz_b = _gate_logits_bwd(dz_gate, wup_pad, "gla_gate_logits_bwd")
    d_wup = [_wgrad(lr_b, dz_b[d], 512, "gla_gate_dw%d" % d) for d in range(2)]
    dproj1 = _l1_assemble(dq1, dk1, dv1, dr, dlr1, "l1_assemble")
    dh2, dh2b, d_nmix1 = _dgrad_norm(dproj1, wgla_in, h2, nmix1, dh3, "l1_in_dgrad")
    g_gla_in = _wgrad(y1, dproj1, 640, "l1_in_dw")

    def reduce_start(grads, tag):
        return _copies_start(_plan_grads_sibling, 4 * len(grads), grads, _landing(4, grads), "rs_%s_d2d_start" % tag)

    def reduce_mid(started, after, tag):
        grads, got = _copies_wait(_plan_grads_sibling, started, after, "rs_%s_d2d_wait" % tag)
        parts = [_chip_partial(g, r, place, "rs_%s_partial%d" % (tag, a)) for a, (g, r) in enumerate(zip(grads, got))]
        pb = [p[0] for p in parts]
        return _copies_start(_plan_grads_chips, 3 * len(pb), pb, _landing(3, pb), "rs_%s_ici_start" % tag), \
            [p[1] for p in parts]

    def reduce_end(started, mine, after, tag, ws, ms, vs):
        _, got = _copies_wait(_plan_grads_chips, started, after, "rs_%s_ici_wait" % tag)
        return [_adamw(w, (p, r), m, v, "adamw_%s%d" % (tag, a))
                for a, (w, p, r, m, v) in enumerate(zip(ws, mine, got, ms, vs))]

    slots_l1 = [g_w1_1, g_w2_1.reshape(N_DEV, 512, D_MODEL),
                jnp.transpose(g_gla_in[:, :GLA_IN].reshape(D_MODEL, N_DEV, GLA_IN // N_DEV), (1, 0, 2)),
                g_gla_out.reshape(N_DEV, 128, D_MODEL)]
    ra_d2d = reduce_start(slots_l1, "l1")

    dh1, dh1b, dpre0, act0, d_nmlp0 = _mlp_bwd(dh2, h1, _after(ra_d2d[4], nmlp0), pre0, w1g[0], w2f[0], "mlp0_bwd")
    g_w1_0 = _wgrad(ym0, dpre0, 512, "mlp0_dw1", sharded_cols=True)
    g_w2_0 = _wgrad(act0, dh2b, 1024, "mlp0_dw2")
    ra_ici, ra_mine = reduce_mid(ra_d2d, g_w2_0, "l1")
    rb_d2d = reduce_start([g_w1_0, g_w2_0.reshape(N_DEV, 512, D_MODEL)], "mlp0")
    dmixin0 = _dgrad(dh1b, wab_out, "l0_out_dgrad")
    g_ab_out = _wgrad(mixin0, dh1b, 1024, "l0_out_dw")
    dho, dga, do_hg, dg_gate, d_hgnorm = _l0_combine_bwd(
        hs, proj0, o_hg, _after(rb_d2d[4], _after(ra_ici[4], hg_norm)), dmixin0, "l0_combine_bwd")
    dxc, d_wbd, d_rgb, d_lam = _rg_scan_bwd(xc, wbd, rg_bias, lam3, hs, dho, "rg_scan_bwd")
    dxa, d_cw8, d_cb = _rg_conv_bwd(dxc, proj0, cw8, "rg_conv_bwd")
    dq0, df0, dv0, d_l0, d_l1 = _hg_bwd(proj0, l0, l1, s_hg, do_hg, "hg_chunks_bwd")
    rb_ici, rb_mine = reduce_mid(rb_d2d, d_l0, "mlp0")
    dproj0 = _l0_assemble(dxa, dga, dq0, df0, dv0, dg_gate, "l0_assemble")
    g_ab_in = _wgrad(y0, dproj0, 512, "l0_in_dw")
    rc_d2d = reduce_start([jnp.transpose(g_ab_in.reshape(D_MODEL, N_DEV, AB_IN // N_DEV), (1, 0, 2)),
                           g_ab_out.reshape(N_DEV, 128, D_MODEL)], "ab")
    dx, _, d_nmix0 = _dgrad_norm(dproj0, wab_in, h0, _after(rc_d2d[4], _after(rb_ici[4], nmix0)), dh1,
                                 "l0_in_dgrad")
    rc_ici, rc_mine = reduce_mid(rc_d2d, d_nmix0, "ab")

    res_l1 = reduce_end(ra_ici, ra_mine, rc_ici[4], "l1",
                        [mlp_w1[1], mlp_w2[1], gla_w_in[0], gla_w_out[0]],
                        [m_mlp_w1[1], m_mlp_w2[1], m_gla_w_in[0], m_gla_w_out[0]],
                        [v_mlp_w1[1], v_mlp_w2[1], v_gla_w_in[0], v_gla_w_out[0]])
    res_mlp0 = reduce_end(rb_ici, rb_mine, res_l1[3][0], "mlp0", [mlp_w1[0], mlp_w2[0]],
                          [m_mlp_w1[0], m_mlp_w2[0]], [v_mlp_w1[0], v_mlp_w2[0]])

    def stacked(a, b):
        return tuple(jnp.stack([a[k], b[k]]) for k in range(4))

    res = {"mlp_w1": stacked(res_mlp0[0], res_l1[0]), "mlp_w2": stacked(res_mlp0[1], res_l1[1]),
           "gla_w_in": tuple(res_l1[2][k][None] for k in range(4)),
           "gla_w_out": tuple(res_l1[3][k][None] for k in range(4))}

    d_wa = _block_diag_extract(d_wbd[:, :, :RG_W])[None]
    d_wx = _block_diag_extract(d_wbd[:, :, RG_W:])[None]
    small_full = {
        "norm_mix": jnp.concatenate([d_nmix0, d_nmix1], axis=0), "norm_mlp": jnp.concatenate([d_nmlp0, d_nmlp1], axis=0),
        "norm_final": d_nfin.reshape(D_MODEL), "rg_conv_b": d_cb, "rg_w_a": d_wa, "rg_w_x": d_wx,
        "hg_lb_logits": jnp.concatenate([d_l0[0] + d_l0[1], d_l1[0] + d_l1[1]], axis=0), "hg_norm": d_hgnorm,
        "rg_conv_w": d_cw8[0:4][None], "rg_b_a": d_rgb[:, 0, :RG_W][None], "rg_b_x": d_rgb[:, 0, RG_W:][None],
        "rg_lambda": d_lam[:, 0, :][None],
        "gla_w_gate_up": jnp.stack([d_wup[0][0:16], d_wup[1][16:32]])[None], "gla_b_gate": d_bg[:, 0, :][None],
        "gla_norm": d_gnorm}
    small_names = SMALL_REPLICATED + SMALL_SHARDED
    packed = _pack_rows([small_full[n] for n in small_names], 8)
    summed = _allgather_vmem(packed, "ar_small", reduce=True)
    g_small = {}
    off = 0
    for n in small_names:
        a = small_full[n]
        gfull = summed[off:off + a.size // LANES].reshape(a.shape)
        off += _part_rows(a)
        if n in SMALL_SHARDED:
            loc = w_loc[n].shape[-1]
            gfull = lax.dynamic_slice_in_dim(gfull, dev * loc, loc, axis=gfull.ndim - 1)
        g_small[n] = gfull
    sw = _pack_rows([w_loc[n] for n in small_names], 256)
    sg = _pack_rows([g_small[n] for n in small_names], 256)
    sm = _pack_rows([m_loc[n] for n in small_names], 256)
    sv = _pack_rows([v_loc[n] for n in small_names], 256)
    small_res = _adamw(sw, (sg, None), sm, sv, "adamw_small")
    res_ab = reduce_end(rc_ici, rc_mine, res_mlp0[1][0], "ab", [ab_w_in[0], ab_w_out[0]],
                        [m_ab_w_in[0], m_ab_w_out[0]], [v_ab_w_in[0], v_ab_w_out[0]])
    res["ab_w_in"] = tuple(res_ab[0][k][None] for k in range(4))
    res["ab_w_out"] = tuple(res_ab[1][k][None] for k in range(4))
    off = 0
    for n in small_names:
        a = w_loc[n]
        nr = a.size // LANES
        res[n] = tuple(small_res[k][off:off + nr].reshape(a.shape) for k in range(4))
        off += _part_rows(a)

    grad_x = dx.reshape(1, T, D_MODEL)
    out = [loss, grad_x]
    for k in range(4):
        out += [res[n][k] for n in WEIGHT_NAMES]
    return tuple(out)
```

```python
import jax
import jax.numpy as jnp
from jax import lax
from jax.experimental import pallas as pl
from jax.experimental.pallas import tpu as pltpu

F32, BF16 = jnp.float32, jnp.bfloat16
HI = lax.Precision.HIGHEST
MESH = pl.DeviceIdType.MESH

D_MODEL = 1024
D_FF = 4096
RG_W = 512
HG_W = 512
CHUNK = 64
EPS = 1e-6
RG_C = 8.0
AB_IN = 3584
GLA_IN = 3104
GLA_IN_PAD = 3200
N_DEV = 8
LANES = 128
SUBLANES = 8
VMEM_LIMIT = 48 * 1024 * 1024

ADAM_LR, ADAM_B1, ADAM_B2, ADAM_EPS, ADAM_WD, ADAM_STEP = 0.001, 0.9, 0.999, 1e-08, 0.01, 10


def _params(*sem):
    return pltpu.CompilerParams(dimension_semantics=sem, vmem_limit_bytes=VMEM_LIMIT)


def _dg(a, b, ca, cb):
    return lax.dot_general(a.astype(BF16), b.astype(BF16), (((ca,), (cb,)), ((), ())),
                           preferred_element_type=F32)


@jax.custom_vjp
def _mm_nn(a, b):
    return _dg(a, b, 1, 0)


_mm_nn.defvjp(lambda a, b: (_dg(a, b, 1, 0), (a, b)),
              lambda res, g: (_dg(g, res[1], 1, 1), _dg(res[0], g, 0, 0)))


@jax.custom_vjp
def _mm_nt(a, b):
    return _dg(a, b, 1, 1)


_mm_nt.defvjp(lambda a, b: (_dg(a, b, 1, 1), (a, b)),
              lambda res, g: (_dg(g, res[1], 1, 0), _dg(g, res[0], 0, 0)))


@jax.custom_vjp
def _mm_tn(a, b):
    return _dg(a, b, 0, 0)


_mm_tn.defvjp(lambda a, b: (_dg(a, b, 0, 0), (a, b)),
              lambda res, g: (_dg(res[1], g, 1, 1), _dg(res[0], g, 1, 0)))


@jax.custom_vjp
def _cum(tri, tri_t, x):
    return jnp.dot(tri, x, precision=HI, preferred_element_type=F32)


_cum.defvjp(lambda tri, tri_t, x: (jnp.dot(tri, x, precision=HI, preferred_element_type=F32), (tri, tri_t)),
            lambda res, g: (jnp.zeros_like(res[0]), jnp.zeros_like(res[1]),
                            jnp.dot(res[1], g, precision=HI, preferred_element_type=F32)))


def _sig(x):
    return 1.0 / (1.0 + jnp.exp(-x))


def _gelu(x):
    return 0.5 * x * (1.0 + jnp.tanh(0.7978845608028654 * (x + 0.044715 * (x * x * x))))


def _softplus(z):
    return jnp.maximum(z, 0.0) + jnp.log(1.0 + jnp.exp(-jnp.abs(z)))


def _rms(x):
    return lax.rsqrt(jnp.mean(x * x, axis=-1, keepdims=True) + EPS)


def _rmsnorm_bwd(x, gain, dy):
    r = _rms(x)
    xh = x * r
    dgain = jnp.sum(dy * xh, axis=0, keepdims=True)
    dxh = dy * gain
    dx = r * (dxh - xh * jnp.mean(dxh * xh, axis=-1, keepdims=True))
    return dx, dgain


def _headnorm(o, gain, n_heads, hd):
    parts = []
    for h in range(n_heads):
        oh = o[:, h * hd:(h + 1) * hd]
        parts.append(oh * _rms(oh))
    return jnp.concatenate(parts, axis=1) * gain


def _tri_consts(d):
    row = lax.broadcasted_iota(jnp.int32, (CHUNK, CHUNK), 0)
    col = lax.broadcasted_iota(jnp.int32, (CHUNK, CHUNK), 1)
    ge = (row >= col).astype(F32)
    le = (row <= col).astype(F32)
    r1 = lax.broadcasted_iota(jnp.int32, (CHUNK, 1), 0)
    if d == 0:
        return ge, le, (r1 <= CHUNK // 2).astype(F32)
    return le, ge, (r1 >= CHUNK // 2 - 1).astype(F32)


def _chunk_core(qh, k, v, logf, st_prev, tri, tri_t, mref, n_heads, dk, dv):
    cum = _cum(tri, tri_t, logf)
    ref = jnp.sum(logf * mref, axis=0, keepdims=True)
    last = jnp.sum(logf, axis=0, keepdims=True)
    q_in = qh * jnp.exp(cum - ref)
    k_in = k * jnp.exp(ref - cum)
    k_st = k * jnp.exp(last - cum)
    q_dec = qh * jnp.exp(cum)
    decay = jnp.exp(last)
    outs, sts = [], []
    for h in range(n_heads):
        sk = slice(h * dk, (h + 1) * dk)
        sv = slice(h * dv, (h + 1) * dv)
        sc = _mm_nt(q_in[:, sk], k_in[:, sk]) * tri
        o = _mm_nn(sc, v[:, sv]) + _mm_nt(q_dec[:, sk], st_prev[h])
        sts.append(st_prev[h] * decay[:, sk] + _mm_tn(v[:, sv], k_st[:, sk]))
        outs.append(o)
    return jnp.concatenate(outs, axis=1), tuple(sts)


def _hg_chunk(q, f, v, l0, l1, st_prev, tri, tri_t, mref):
    lb = _sig(l0 - l1)
    sg = _sig(f)
    qh = q * _sig(q)
    logf = jnp.log(lb + (1.0 - lb) * sg)
    k = (1.0 - lb) * (1.0 - sg)
    return _chunk_core(qh, k, v, logf, st_prev, tri, tri_t, mref, 4, 128, 128)


def _gla_chunk(q, k, v, z, st_prev, tri, tri_t, mref):
    logf = (jnp.minimum(z, 0.0) - jnp.log(1.0 + jnp.exp(-jnp.abs(z)))) * (1.0 / 16.0)
    qh = q * (128.0 ** -0.5)
    return _chunk_core(qh, k, v, logf, st_prev, tri, tri_t, mref, 4, 128, 256)


def _rg_gates(xc, wbd, bias, lam):
    z = _mm_nn(xc, wbd) + bias
    r = _sig(z[:, :RG_W])
    i = _sig(z[:, RG_W:])
    log_a = -RG_C * r * _softplus(-lam)
    a = jnp.exp(log_a)
    x2 = 2.0 * log_a
    neg_expm1 = jnp.where(x2 > -1e-2, -(x2 + 0.5 * x2 * x2 + x2 * x2 * x2 * (1.0 / 6.0)), 1.0 - jnp.exp(x2))
    u = jnp.sqrt(neg_expm1) * (i * xc)
    return a, u


def _l0_combine(hf, hb, ga, of, ob, g, gain):
    ya = (hf + hb) * _gelu(ga)
    yb = _headnorm(of + ob, gain, 4, 128) * (g * _sig(g))
    return jnp.concatenate([ya, yb], axis=1)


def _l1_combine(of, ob, r, gain):
    return _headnorm(of + ob, gain, 4, 256) * (r * _sig(r))


def _norm_matmul(h, gain, w, name):
    T, D = h.shape
    N = w.shape[1]
    tm = min(512, T)

    def body(h_ref, g_ref, w_ref, o_ref, y_ref):
        x = h_ref[...]
        y = (x * _rms(x) * g_ref[...]).astype(BF16)
        y_ref[...] = y
        o_ref[...] = jnp.dot(y, w_ref[...], preferred_element_type=F32)

    return pl.pallas_call(
        body, name=name, grid=(T // tm,),
        in_specs=[pl.BlockSpec((tm, D), lambda i: (i, 0)), pl.BlockSpec((1, D), lambda i: (0, 0)),
                  pl.BlockSpec((D, N), lambda i: (0, 0))],
        out_specs=[pl.BlockSpec((tm, N), lambda i: (i, 0)), pl.BlockSpec((tm, D), lambda i: (i, 0))],
        out_shape=[jax.ShapeDtypeStruct((T, N), F32), jax.ShapeDtypeStruct((T, D), BF16)],
        compiler_params=_params("parallel"))(h, gain, w)


def _matmul_res(a, w, res, name):
    T, K = a.shape
    N = w.shape[1]
    tm = min(512, T)

    def body(a_ref, w_ref, r_ref, o_ref):
        o_ref[...] = r_ref[...] + jnp.dot(a_ref[...], w_ref[...], preferred_element_type=F32)

    return pl.pallas_call(
        body, name=name, grid=(T // tm,),
        in_specs=[pl.BlockSpec((tm, K), lambda i: (i, 0)), pl.BlockSpec((K, N), lambda i: (0, 0)),
                  pl.BlockSpec((tm, N), lambda i: (i, 0))],
        out_specs=pl.BlockSpec((tm, N), lambda i: (i, 0)),
        out_shape=jax.ShapeDtypeStruct((T, N), F32),
        compiler_params=_params("parallel"))(a, w, res)


def _dgrad(dc, w, name):
    T, N = dc.shape
    K = w.shape[0]
    tm = min(512, T)

    def body(d_ref, w_ref, o_ref):
        o_ref[...] = _dg(d_ref[...], w_ref[...], 1, 1)

    return pl.pallas_call(
        body, name=name, grid=(T // tm,),
        in_specs=[pl.BlockSpec((tm, N), lambda i: (i, 0)), pl.BlockSpec((K, N), lambda i: (0, 0))],
        out_specs=pl.BlockSpec((tm, K), lambda i: (i, 0)),
        out_shape=jax.ShapeDtypeStruct((T, K), F32),
        compiler_params=_params("parallel"))(dc, w)


def _dgrad_norm(dproj, w, h, gain, dres, name):
    T, N = dproj.shape
    D = w.shape[0]
    tm = min(512, T)

    def body(dp_ref, w_ref, h_ref, g_ref, dr_ref, dh_ref, dhb_ref, dg_ref):
        @pl.when(pl.program_id(0) == 0)
        def _():
            dg_ref[...] = jnp.zeros_like(dg_ref)

        dy = _dg(dp_ref[...], w_ref[...], 1, 1)
        dx, dgain = _rmsnorm_bwd(h_ref[...], g_ref[...], dy)
        dh = dr_ref[...] + dx
        dh_ref[...] = dh
        dhb_ref[...] = dh.astype(BF16)
        dg_ref[...] += dgain

    return pl.pallas_call(
        body, name=name, grid=(T // tm,),
        in_specs=[pl.BlockSpec((tm, N), lambda i: (i, 0)), pl.BlockSpec((D, N), lambda i: (0, 0)),
                  pl.BlockSpec((tm, D), lambda i: (i, 0)), pl.BlockSpec((1, D), lambda i: (0, 0)),
                  pl.BlockSpec((tm, D), lambda i: (i, 0))],
        out_specs=[pl.BlockSpec((tm, D), lambda i: (i, 0)), pl.BlockSpec((tm, D), lambda i: (i, 0)),
                   pl.BlockSpec((1, D), lambda i: (0, 0))],
        out_shape=[jax.ShapeDtypeStruct((T, D), F32), jax.ShapeDtypeStruct((T, D), BF16),
                   jax.ShapeDtypeStruct((1, D), F32)],
        compiler_params=_params("arbitrary"))(dproj, w, h, gain, dres)


def _wgrad(a, b, tn, name, sharded_cols=False):
    T, K = a.shape
    N = b.shape[1]
    tk = min(1024, K)
    tt = min(1024, T)
    nt = T // tt

    def body(a_ref, b_ref, o_ref):
        @pl.when(pl.program_id(2) == 0)
        def _():
            o_ref[...] = jnp.zeros_like(o_ref)

        o_ref[...] += _dg(a_ref[...], b_ref[...], 0, 0)

    if sharded_cols:
        out_spec = pl.BlockSpec((None, tk, tn), lambda k, n, t: (n, k, 0))
        out_shape = jax.ShapeDtypeStruct((N // tn, K, tn), F32)
    else:
        out_spec = pl.BlockSpec((tk, tn), lambda k, n, t: (k, n))
        out_shape = jax.ShapeDtypeStruct((K, N), F32)
    return pl.pallas_call(
        body, name=name, grid=(K // tk, N // tn, nt),
        in_specs=[pl.BlockSpec((tt, tk), lambda k, n, t: (t, k)), pl.BlockSpec((tt, tn), lambda k, n, t: (t, n))],
        out_specs=out_spec, out_shape=out_shape,
        compiler_params=_params("parallel", "parallel", "arbitrary"))(a, b)


def _mlp_fwd(h, gain, w1g, w2, name):
    T, D = h.shape
    nf, _, tf = w1g.shape
    tm = min(1024, T)

    def body(h_ref, g_ref, w1_ref, w2_ref, o_ref, pre_ref, y_ref, ysc, acc):
        j = pl.program_id(1)

        @pl.when(j == 0)
        def _():
            x = h_ref[...]
            y = (x * _rms(x) * g_ref[...]).astype(BF16)
            ysc[...] = y
            y_ref[...] = y
            acc[...] = jnp.zeros_like(acc)

        pre = jnp.dot(ysc[...], w1_ref[...], preferred_element_type=F32)
        pre_ref[...] = pre.astype(BF16)
        act = jnp.square(jnp.maximum(pre, 0.0))
        acc[...] += jnp.dot(act.astype(BF16), w2_ref[...], preferred_element_type=F32)

        @pl.when(j == nf - 1)
        def _():
            o_ref[...] = h_ref[...] + acc[...]

    return pl.pallas_call(
        body, name=name, grid=(T // tm, nf),
        in_specs=[pl.BlockSpec((tm, D), lambda i, j: (i, 0)), pl.BlockSpec((1, D), lambda i, j: (0, 0)),
                  pl.BlockSpec((None, D, tf), lambda i, j: (j, 0, 0)), pl.BlockSpec((tf, D), lambda i, j: (j, 0))],
        out_specs=[pl.BlockSpec((tm, D), lambda i, j: (i, 0)), pl.BlockSpec((tm, tf), lambda i, j: (i, j)),
                   pl.BlockSpec((tm, D), lambda i, j: (i, 0))],
        out_shape=[jax.ShapeDtypeStruct((T, D), F32), jax.ShapeDtypeStruct((T, nf * tf), BF16),
                   jax.ShapeDtypeStruct((T, D), BF16)],
        scratch_shapes=[pltpu.VMEM((tm, D), BF16), pltpu.VMEM((tm, D), F32)],
        compiler_params=_params("parallel", "arbitrary"))(h, gain, w1g, w2)


def _mlp_bwd(dout, h, gain, pre, w1g, w2, name):
    T, D = h.shape
    nf, _, tf = w1g.shape
    tm = min(512, T)

    def body(do_ref, h_ref, g_ref, pre_ref, w1_ref, w2_ref, dh_ref, dhb_ref, dpre_ref, act_ref, dg_ref, dy):
        i, j = pl.program_id(0), pl.program_id(1)

        @pl.when(j == 0)
        def _():
            dy[...] = jnp.zeros_like(dy)

        @pl.when((i == 0) & (j == 0))
        def _():
            dg_ref[...] = jnp.zeros_like(dg_ref)

        rp = jnp.maximum(pre_ref[...].astype(F32), 0.0)
        dact = _dg(do_ref[...], w2_ref[...], 1, 1)
        dpre = (dact * (2.0 * rp)).astype(BF16)
        dpre_ref[...] = dpre
        act_ref[...] = (rp * rp).astype(BF16)
        dy[...] += _dg(dpre, w1_ref[...], 1, 1)

        @pl.when(j == nf - 1)
        def _():
            dx, dgain = _rmsnorm_bwd(h_ref[...], g_ref[...], dy[...])
            dh = do_ref[...] + dx
            dh_ref[...] = dh
            dhb_ref[...] = dh.astype(BF16)
            dg_ref[...] += dgain

    return pl.pallas_call(
        body, name=name, grid=(T // tm, nf),
        in_specs=[pl.BlockSpec((tm, D), lambda i, j: (i, 0)), pl.BlockSpec((tm, D), lambda i, j: (i, 0)),
                  pl.BlockSpec((1, D), lambda i, j: (0, 0)), pl.BlockSpec((tm, tf), lambda i, j: (i, j)),
                  pl.BlockSpec((None, D, tf), lambda i, j: (j, 0, 0)), pl.BlockSpec((tf, D), lambda i, j: (j, 0))],
        out_specs=[pl.BlockSpec((tm, D), lambda i, j: (i, 0)), pl.BlockSpec((tm, D), lambda i, j: (i, 0)),
                   pl.BlockSpec((tm, tf), lambda i, j: (i, j)),
                   pl.BlockSpec((tm, tf), lambda i, j: (i, j)), pl.BlockSpec((1, D), lambda i, j: (0, 0))],
        out_shape=[jax.ShapeDtypeStruct((T, D), F32), jax.ShapeDtypeStruct((T, D), BF16),
                   jax.ShapeDtypeStruct((T, nf * tf), BF16),
                   jax.ShapeDtypeStruct((T, nf * tf), BF16), jax.ShapeDtypeStruct((1, D), F32)],
        scratch_shapes=[pltpu.VMEM((tm, D), F32)],
        compiler_params=_params("arbitrary", "arbitrary"))(dout, h, gain, pre, w1g, w2)


def _final_loss(h, gain, target, name):
    T, D = h.shape
    tm = min(512, T)

    def body(h_ref, g_ref, t_ref, l_ref, dh_ref, dhb_ref, dg_ref):
        @pl.when(pl.program_id(0) == 0)
        def _():
            l_ref[...] = jnp.zeros_like(l_ref)
            dg_ref[...] = jnp.zeros_like(dg_ref)

        x = h_ref[...]
        err = x * _rms(x) * g_ref[...] - t_ref[...]
        l_ref[...] += 0.5 * jnp.sum(jnp.mean(err * err, axis=-1, keepdims=True), axis=0, keepdims=True)
        dx, dgain = _rmsnorm_bwd(x, g_ref[...], err * (1.0 / D))
        dh_ref[...] = dx
        dhb_ref[...] = dx.astype(BF16)
        dg_ref[...] += dgain

    return pl.pallas_call(
        body, name=name, grid=(T // tm,),
        in_specs=[pl.BlockSpec((tm, D), lambda i: (i, 0)), pl.BlockSpec((1, D), lambda i: (0, 0)),
                  pl.BlockSpec((tm, D), lambda i: (i, 0))],
        out_specs=[pl.BlockSpec((SUBLANES, LANES), lambda i: (0, 0)), pl.BlockSpec((tm, D), lambda i: (i, 0)),
                   pl.BlockSpec((tm, D), lambda i: (i, 0)), pl.BlockSpec((1, D), lambda i: (0, 0))],
        out_shape=[jax.ShapeDtypeStruct((SUBLANES, LANES), F32), jax.ShapeDtypeStruct((T, D), F32),
                   jax.ShapeDtypeStruct((T, D), BF16), jax.ShapeDtypeStruct((1, D), F32)],
        compiler_params=_params("arbitrary"))(h, gain, target)


def _halo_specs(tm, T, width, col, tile=lambda i: i):
    r8 = tm // SUBLANES
    nb8 = T // SUBLANES
    return [pl.BlockSpec((tm, width), lambda i: (tile(i), col)),
            pl.BlockSpec((SUBLANES, width), lambda i: (jnp.maximum(tile(i) * r8 - 1, 0), col)),
            pl.BlockSpec((SUBLANES, width), lambda i: (jnp.minimum((tile(i) + 1) * r8, nb8 - 1), col))]


def _ext(cur, prev, nxt, has_prev, has_next):
    return jnp.concatenate([jnp.where(has_prev, prev, 0.0), cur, jnp.where(has_next, nxt, 0.0)], axis=0)


def _shifted(ext, offset, tm):
    n = ext.shape[0]
    sh = (-offset) % n
    r = ext if sh == 0 else pltpu.roll(ext, sh, 0)
    return r[SUBLANES:SUBLANES + tm]


def _rg_conv_fwd(proj, cw8, cb, name):
    T = proj.shape[0]
    tm = min(512, T)
    nT = T // tm

    def body(cur_ref, prev_ref, next_ref, w_ref, b_ref, o_ref):
        i = pl.program_id(0)
        ext = _ext(cur_ref[...], prev_ref[...], next_ref[...], i > 0, i < nT - 1)
        acc = jnp.broadcast_to(b_ref[...], (tm, RG_W))
        for k in range(4):
            acc = acc + w_ref[k:k + 1, :] * _shifted(ext, k - 2, tm)
        o_ref[...] = acc

    return pl.pallas_call(
        body, name=name, grid=(nT,),
        in_specs=_halo_specs(tm, T, RG_W, 0) + [pl.BlockSpec((SUBLANES, RG_W), lambda i: (0, 0)),
                                                pl.BlockSpec((1, RG_W), lambda i: (0, 0))],
        out_specs=pl.BlockSpec((tm, RG_W), lambda i: (i, 0)),
        out_shape=jax.ShapeDtypeStruct((T, RG_W), F32),
        compiler_params=_params("parallel"))(proj, proj, proj, cw8, cb)


def _rg_conv_bwd(dxc, proj, cw8, name):
    T = proj.shape[0]
    tm = min(512, T)
    nT = T // tm

    def body(a0, p0, n0, a1, p1, n1, xa, xp, xn, w_ref, dxa_ref, dw_ref, db_ref):
        i = pl.program_id(0)

        @pl.when(i == 0)
        def _():
            dw_ref[...] = jnp.zeros_like(dw_ref)
            db_ref[...] = jnp.zeros_like(db_ref)

        has_p, has_n = i > 0, i < nT - 1
        cur = a0[...] + a1[...]
        dext = _ext(cur, p0[...] + p1[...], n0[...] + n1[...], has_p, has_n)
        xext = _ext(xa[...], xp[...], xn[...], has_p, has_n)
        acc = jnp.zeros((tm, RG_W), F32)
        rows = []
        for k in range(4):
            acc = acc + w_ref[k:k + 1, :] * _shifted(dext, 2 - k, tm)
            rows.append(jnp.sum(cur * _shifted(xext, k - 2, tm), axis=0, keepdims=True))
        dxa_ref[...] = acc
        dw_ref[...] += jnp.concatenate(rows + [jnp.zeros((4, RG_W), F32)], axis=0)
        db_ref[...] += jnp.sum(cur, axis=0, keepdims=True)

    return pl.pallas_call(
        body, name=name, grid=(nT,),
        in_specs=(_halo_specs(tm, T, RG_W, 0) + _halo_specs(tm, T, RG_W, 0)
                  + _halo_specs(tm, T, RG_W, 0) + [pl.BlockSpec((SUBLANES, RG_W), lambda i: (0, 0))]),
        out_specs=[pl.BlockSpec((tm, RG_W), lambda i: (i, 0)), pl.BlockSpec((SUBLANES, RG_W), lambda i: (0, 0)),
                   pl.BlockSpec((1, RG_W), lambda i: (0, 0))],
        out_shape=[jax.ShapeDtypeStruct((T, RG_W), F32), jax.ShapeDtypeStruct((SUBLANES, RG_W), F32),
                   jax.ShapeDtypeStruct((1, RG_W), F32)],
        compiler_params=_params("arbitrary"))(dxc[0], dxc[0], dxc[0], dxc[1], dxc[1], dxc[1], proj, proj, proj, cw8)


def _local_scan(a, b, ascending):
    n = a.shape[0]
    pos = jnp.bitwise_and(lax.broadcasted_iota(jnp.int32, a.shape, 0), SUBLANES - 1)
    for s in (1, 2, 4):
        sh = s if ascending else n - s
        ok = (pos >= s) if ascending else (pos < SUBLANES - s)
        a_sh, b_sh = pltpu.roll(a, sh, 0), pltpu.roll(b, sh, 0)
        b = jnp.where(ok, a * b_sh + b, b)
        a = jnp.where(ok, a * a_sh, a)
    return a, b


def _group_scan(chains, a_sc, b_sc, carry, n_groups):
    def step(g, hs):
        new = []
        for (d, out_ref, asc), h in zip(chains, hs):
            r0 = pl.multiple_of((g if asc else n_groups - 1 - g) * SUBLANES, SUBLANES)
            out_ref[pl.ds(r0, SUBLANES), :] = a_sc[d, pl.ds(r0, SUBLANES), :] * h + b_sc[d, pl.ds(r0, SUBLANES), :]
            new.append(out_ref[pl.ds(r0 + (SUBLANES - 1 if asc else 0), 1), :])
        return tuple(new)

    hs = lax.fori_loop(0, n_groups, step, tuple(carry[d, 0:1, :] for d, _, _ in chains))
    for (d, _, _), h in zip(chains, hs):
        carry[d, 0:1, :] = h


def _rg_scan_fwd(xc, wbd, bias, lam, name):
    T = xc.shape[0]
    tm = min(512, T)
    nT = T // tm

    def body(xf_ref, xb_ref, w_ref, b_ref, lam_ref, hf_ref, hb_ref, a_sc, b_sc, carry):
        @pl.when(pl.program_id(0) == 0)
        def _():
            carry[...] = jnp.zeros_like(carry)

        for d, x_ref in enumerate((xf_ref, xb_ref)):
            a, u = _rg_gates(x_ref[...], w_ref[d], b_ref[d], lam_ref[d])
            a_sc[d], b_sc[d] = _local_scan(a, u, d == 0)
        _group_scan(((0, hf_ref, True), (1, hb_ref, False)), a_sc, b_sc, carry, tm // SUBLANES)

    full = lambda a: pl.BlockSpec(a.shape, lambda i: (0,) * len(a.shape))
    res = pl.pallas_call(
        body, name=name, grid=(nT,),
        in_specs=[pl.BlockSpec((tm, RG_W), lambda i: (i, 0)), pl.BlockSpec((tm, RG_W), lambda i: (nT - 1 - i, 0)),
                  full(wbd), full(bias), full(lam)],
        out_specs=[pl.BlockSpec((tm, RG_W), lambda i: (i, 0)), pl.BlockSpec((tm, RG_W), lambda i: (nT - 1 - i, 0))],
        out_shape=[jax.ShapeDtypeStruct((T, RG_W), F32)] * 2,
        scratch_shapes=[pltpu.VMEM((2, tm, RG_W), F32), pltpu.VMEM((2, tm, RG_W), F32),
                        pltpu.VMEM((2, SUBLANES, RG_W), F32)],
        compiler_params=_params("arbitrary"))(xc, xc, wbd, bias, lam)
    return res[0], res[1]


def _rg_scan_bwd(xc, wbd, bias, lam, hs, dho, name):
    T = xc.shape[0]
    tm = min(256, T)
    nT = T // tm
    tiles = (lambda i: nT - 1 - i, lambda i: i)

    def body(xf_ref, xb_ref, w_ref, b_ref, lam_ref, hfc, hfp, hfn, hbc, hbp, hbn, dof_ref, dob_ref,
             dxf_ref, dxb_ref, dw_ref, db_ref, dlam_ref, a_sc, b_sc, y_sc, carry):
        i = pl.program_id(0)

        @pl.when(i == 0)
        def _():
            carry[...] = jnp.zeros_like(carry)
            dw_ref[...] = jnp.zeros_like(dw_ref)
            db_ref[...] = jnp.zeros_like(db_ref)
            dlam_ref[...] = jnp.zeros_like(dlam_ref)

        vjps, entering = [], []
        for d, (x_ref, do_ref) in enumerate(((xf_ref, dof_ref), (xb_ref, dob_ref))):
            (a, _), vjp = jax.vjp(_rg_gates, x_ref[...], w_ref[d].astype(F32), b_ref[d], lam_ref[d])
            vjps.append(vjp)
            entering.append(carry[d, 0:1, :])
            a_sc[d], b_sc[d] = _local_scan(a, a * do_ref[...], d == 1)
        _group_scan(((0, y_sc.at[0], False), (1, y_sc.at[1], True)), a_sc, b_sc, carry, tm // SUBLANES)

        row = lax.broadcasted_iota(jnp.int32, (tm, RG_W), 0)
        for d, (do_ref, dx_ref, hc, hp, hn, ti) in enumerate(
                ((dof_ref, dxf_ref, hfc, hfp, hfn, nT - 1 - i), (dob_ref, dxb_ref, hbc, hbp, hbn, i))):
            y = y_sc[d]
            if d == 0:
                y_next = jnp.where(row == tm - 1, entering[d], pltpu.roll(y, tm - 1, 0))
            else:
                y_next = jnp.where(row == 0, entering[d], pltpu.roll(y, 1, 0))
            dtot = do_ref[...] + y_next
            ext = _ext(hc[...], hp[...], hn[...], ti > 0, ti < nT - 1)
            hprev = _shifted(ext, -1 if d == 0 else 1, tm)
            dxc, dw, db, dlam = vjps[d]((dtot * hprev, dtot))
            dx_ref[...] = dxc
            dw_ref[d] += dw
            db_ref[d] += db
            dlam_ref[d] += dlam

    full = lambda a: pl.BlockSpec(a.shape, lambda i: (0,) * len(a.shape))
    tok = lambda d: pl.BlockSpec((tm, RG_W), lambda i: (tiles[d](i), 0))
    acc_shapes = [jax.ShapeDtypeStruct((2, RG_W, 2 * RG_W), F32), jax.ShapeDtypeStruct((2, 1, 2 * RG_W), F32),
                  jax.ShapeDtypeStruct((2, 1, RG_W), F32)]
    res = pl.pallas_call(
        body, name=name, grid=(nT,),
        in_specs=([tok(0), tok(1), full(wbd), full(bias), full(lam)]
                  + _halo_specs(tm, T, RG_W, 0, tiles[0]) + _halo_specs(tm, T, RG_W, 0, tiles[1]) + [tok(0), tok(1)]),
        out_specs=[tok(0), tok(1)] + [full(s) for s in acc_shapes],
        out_shape=[jax.ShapeDtypeStruct((T, RG_W), F32)] * 2 + acc_shapes,
        scratch_shapes=[pltpu.VMEM((2, tm, RG_W), F32), pltpu.VMEM((2, tm, RG_W), F32),
                        pltpu.VMEM((2, tm, RG_W), F32), pltpu.VMEM((2, SUBLANES, RG_W), F32)],
        compiler_params=_params("arbitrary"))(xc, xc, wbd, bias, lam, hs[0], hs[0], hs[0], hs[1], hs[1], hs[1],
                                              dho, dho)
    return (res[0], res[1]), res[2], res[3], res[4]


def _chunk_rows(n_chunks, reverse):
    up, down = (lambda c: c), (lambda c: n_chunks - 1 - c)
    return (down, up) if reverse else (up, down)


def _hg_fwd(proj, l0, l1, name):
    T = proj.shape[0]
    nC = T // CHUNK
    H, dk, dv = 4, 128, 128
    rows = _chunk_rows(nC, False)

    def body(qf, ff, vf, qb, fb, vb, l0_ref, l1_ref, of, ob, spf, spb, st):
        @pl.when(pl.program_id(0) == 0)
        def _():
            st[...] = jnp.zeros_like(st)

        for d, (q, f, v, o, sp) in enumerate(((qf, ff, vf, of, spf), (qb, fb, vb, ob, spb))):
            tri, tri_t, mref = _tri_consts(d)
            stp = tuple(st[d, h] for h in range(H))
            sp[...] = st[d]
            o_val, stn = _hg_chunk(q[...], f[...], v[...], l0_ref[...], l1_ref[...], stp, tri, tri_t, mref)
            o[...] = o_val
            for h in range(H):
                st[d, h] = stn[h]

    tok = lambda d, col: pl.BlockSpec((CHUNK, HG_W), lambda c: (rows[d](c), col))
    par = pl.BlockSpec((1, HG_W), lambda c: (0, 0))
    state = lambda d: pl.BlockSpec((None, H, dv, dk), lambda c: (rows[d](c), 0, 0, 0))
    res = pl.pallas_call(
        body, name=name, grid=(nC,),
        in_specs=[tok(0, 2), tok(0, 3), tok(0, 5), tok(1, 2), tok(1, 4), tok(1, 5), par, par],
        out_specs=[tok(0, 0), tok(1, 0), state(0), state(1)],
        out_shape=[jax.ShapeDtypeStruct((T, H * dv), F32)] * 2 + [jax.ShapeDtypeStruct((nC, H, dv, dk), F32)] * 2,
        scratch_shapes=[pltpu.VMEM((2, H, dv, dk), F32)],
        compiler_params=_params("arbitrary"))(proj, proj, proj, proj, proj, proj, l0, l1)
    return (res[0], res[1]), (res[2], res[3])


def _hg_bwd(proj, l0, l1, sprev, do, name):
    T = proj.shape[0]
    nC = T // CHUNK
    H, dk, dv = 4, 128, 128
    rows = _chunk_rows(nC, True)

    def body(qf, ff, vf, qb, fb, vb, l0_ref, l1_ref, spf, spb, dof, dob,
             dqf, dff, dvf, dqb, dfb, dvb, dl0_ref, dl1_ref, dst):
        @pl.when(pl.program_id(0) == 0)
        def _():
            dst[...] = jnp.zeros_like(dst)
            dl0_ref[...] = jnp.zeros_like(dl0_ref)
            dl1_ref[...] = jnp.zeros_like(dl1_ref)

        for d, (q, f, v, sp, do_ref, dq_ref, df_ref, dv_ref) in enumerate(
                ((qf, ff, vf, spf, dof, dqf, dff, dvf), (qb, fb, vb, spb, dob, dqb, dfb, dvb))):
            tri, tri_t, mref = _tri_consts(d)
            fn = lambda q_, f_, v_, a0, a1, stp: _hg_chunk(q_, f_, v_, a0, a1, stp, tri, tri_t, mref)
            stp = tuple(sp[h] for h in range(H))
            _, vjp = jax.vjp(fn, q[...], f[...], v[...], l0_ref[...], l1_ref[...], stp)
            dq, df, dvv, dl0, dl1, dstp = vjp((do_ref[...], tuple(dst[d, h] for h in range(H))))
            dq_ref[...] = dq
            df_ref[...] = df
            dv_ref[...] = dvv
            dl0_ref[d] += dl0
            dl1_ref[d] += dl1
            for h in range(H):
                dst[d, h] = dstp[h]

    tok = lambda d, col: pl.BlockSpec((CHUNK, HG_W), lambda c: (rows[d](c), col))
    par = pl.BlockSpec((1, HG_W), lambda c: (0, 0))
    acc = pl.BlockSpec((2, 1, HG_W), lambda c: (0, 0, 0))
    state = lambda d: pl.BlockSpec((None, H, dv, dk), lambda c: (rows[d](c), 0, 0, 0))
    res = pl.pallas_call(
        body, name=name, grid=(nC,),
        in_specs=[tok(0, 2), tok(0, 3), tok(0, 5), tok(1, 2), tok(1, 4), tok(1, 5), par, par,
                  state(0), state(1), tok(0, 0), tok(1, 0)],
        out_specs=[tok(0, 0)] * 3 + [tok(1, 0)] * 3 + [acc, acc],
        out_shape=[jax.ShapeDtypeStruct((T, HG_W), F32)] * 6 + [jax.ShapeDtypeStruct((2, 1, HG_W), F32)] * 2,
        scratch_shapes=[pltpu.VMEM((2, H, dv, dk), F32)],
        compiler_params=_params("arbitrary"))(proj, proj, proj, proj, proj, proj, l0, l1, sprev[0], sprev[1], do, do)
    return (res[0], res[3]), (res[1], res[4]), (res[2], res[5]), res[6], res[7]


def _gate_logits(proj, wup, bg, name):
    T = proj.shape[0]
    tm = min(512, T)

    def body(lr_ref, w_ref, b_ref, z_ref, lrb_ref):
        lr = lr_ref[...].astype(BF16)
        lrb_ref[...] = lr
        for d in range(2):
            z_ref[d] = _dg(lr, w_ref[d], 1, 0) + b_ref[d]

    return pl.pallas_call(
        body, name=name, grid=(T // tm,),
        in_specs=[pl.BlockSpec((tm, LANES), lambda i: (i, 24)), pl.BlockSpec((2, LANES, 512), lambda i: (0, 0, 0)),
                  pl.BlockSpec((2, 1, 512), lambda i: (0, 0, 0))],
        out_specs=[pl.BlockSpec((2, tm, 512), lambda i: (0, i, 0)), pl.BlockSpec((tm, LANES), lambda i: (i, 0))],
        out_shape=[jax.ShapeDtypeStruct((2, T, 512), F32), jax.ShapeDtypeStruct((T, LANES), BF16)],
        compiler_params=_params("parallel"))(proj, wup, bg)


def _gate_logits_bwd(dz, wup, name):
    T = dz[0].shape[0]
    tm = min(512, T)

    def body(dzf_ref, dzb_ref, w_ref, dlr_ref, db_ref, dzb16_ref):
        @pl.when(pl.program_id(0) == 0)
        def _():
            db_ref[...] = jnp.zeros_like(db_ref)

        acc = jnp.zeros((tm, LANES), F32)
        for d, dz_ref in enumerate((dzf_ref, dzb_ref)):
            g = dz_ref[...]
            gb = g.astype(BF16)
            dzb16_ref[d] = gb
            acc = acc + _dg(gb, w_ref[d], 1, 1)
            db_ref[d] += jnp.sum(g, axis=0, keepdims=True)
        dlr_ref[...] = acc

    tok = pl.BlockSpec((tm, 512), lambda i: (i, 0))
    return pl.pallas_call(
        body, name=name, grid=(T // tm,),
        in_specs=[tok, tok, pl.BlockSpec((2, LANES, 512), lambda i: (0, 0, 0))],
        out_specs=[pl.BlockSpec((tm, LANES), lambda i: (i, 0)), pl.BlockSpec((2, 1, 512), lambda i: (0, 0, 0)),
                   pl.BlockSpec((2, tm, 512), lambda i: (0, i, 0))],
        out_shape=[jax.ShapeDtypeStruct((T, LANES), F32), jax.ShapeDtypeStruct((2, 1, 512), F32),
                   jax.ShapeDtypeStruct((2, T, 512), BF16)],
        compiler_params=_params("arbitrary"))(dz[0], dz[1], wup)


def _gla_fwd(proj, z, name):
    T = proj.shape[0]
    nC = T // CHUNK
    H, dk, dv = 4, 128, 256
    rows = _chunk_rows(nC, False)

    def body(qf, kf, vf, zf, qb, kb, vb, zb, of, ob, spf, spb, st):
        @pl.when(pl.program_id(0) == 0)
        def _():
            st[...] = jnp.zeros_like(st)

        for d, (q, k, v, z_ref, o, sp) in enumerate(((qf, kf, vf, zf, of, spf), (qb, kb, vb, zb, ob, spb))):
            tri, tri_t, mref = _tri_consts(d)
            stp = tuple(st[d, h] for h in range(H))
            sp[...] = st[d]
            o_val, stn = _gla_chunk(q[...], k[...], v[...], z_ref[...], stp, tri, tri_t, mref)
            o[...] = o_val
            for h in range(H):
                st[d, h] = stn[h]

    tok = lambda d, w, col: pl.BlockSpec((CHUNK, w), lambda c: (rows[d](c), col))
    gate = lambda d: pl.BlockSpec((None, CHUNK, 512), lambda c: (d, rows[d](c), 0))
    state = lambda d: pl.BlockSpec((None, H, dv, dk), lambda c: (rows[d](c), 0, 0, 0))
    res = pl.pallas_call(
        body, name=name, grid=(nC,),
        in_specs=[tok(0, 512, 0), tok(0, 512, 1), tok(0, 1024, 1), gate(0),
                  tok(1, 512, 0), tok(1, 512, 1), tok(1, 1024, 1), gate(1)],
        out_specs=[tok(0, H * dv, 0), tok(1, H * dv, 0), state(0), state(1)],
        out_shape=[jax.ShapeDtypeStruct((T, H * dv), F32)] * 2 + [jax.ShapeDtypeStruct((nC, H, dv, dk), F32)] * 2,
        scratch_shapes=[pltpu.VMEM((2, H, dv, dk), F32)],
        compiler_params=_params("arbitrary"))(proj, proj, proj, z, proj, proj, proj, z)
    return (res[0], res[1]), (res[2], res[3])


def _gla_bwd(proj, z, sprev, do, name):
    T = proj.shape[0]
    nC = T // CHUNK
    H, dk, dv = 4, 128, 256
    rows = _chunk_rows(nC, True)

    def body(qf, kf, vf, zf, qb, kb, vb, zb, spf, spb, dof, dob,
             dqf, dkf, dvf, dzf, dqb, dkb, dvb, dzb, dst):
        @pl.when(pl.program_id(0) == 0)
        def _():
            dst[...] = jnp.zeros_like(dst)

        for d, (q, k, v, z_ref, sp, do_ref, dq_ref, dk_ref, dv_ref, dz_ref) in enumerate(
                ((qf, kf, vf, zf, spf, dof, dqf, dkf, dvf, dzf), (qb, kb, vb, zb, spb, dob, dqb, dkb, dvb, dzb))):
            tri, tri_t, mref = _tri_consts(d)
            fn = lambda q_, k_, v_, z_, stp: _gla_chunk(q_, k_, v_, z_, stp, tri, tri_t, mref)
            stp = tuple(sp[h] for h in range(H))
            _, vjp = jax.vjp(fn, q[...], k[...], v[...], z_ref[...], stp)
            dq, dkk, dvv, dzz, dstp = vjp((do_ref[...], tuple(dst[d, h] for h in range(H))))
            dq_ref[...] = dq
            dk_ref[...] = dkk
            dv_ref[...] = dvv
            dz_ref[...] = dzz
            for h in range(H):
                dst[d, h] = dstp[h]

    tok = lambda d, w, col: pl.BlockSpec((CHUNK, w), lambda c: (rows[d](c), col))
    gate = lambda d: pl.BlockSpec((None, CHUNK, 512), lambda c: (d, rows[d](c), 0))
    state = lambda d: pl.BlockSpec((None, H, dv, dk), lambda c: (rows[d](c), 0, 0, 0))
    outs = lambda d: [tok(d, 512, 0), tok(d, 512, 0), tok(d, 1024, 0), tok(d, 512, 0)]
    shapes = [jax.ShapeDtypeStruct((T, 512), F32), jax.ShapeDtypeStruct((T, 512), F32),
              jax.ShapeDtypeStruct((T, 1024), F32), jax.ShapeDtypeStruct((T, 512), F32)]
    res = pl.pallas_call(
        body, name=name, grid=(nC,),
        in_specs=[tok(0, 512, 0), tok(0, 512, 1), tok(0, 1024, 1), gate(0),
                  tok(1, 512, 0), tok(1, 512, 1), tok(1, 1024, 1), gate(1),
                  state(0), state(1), tok(0, H * dv, 0), tok(1, H * dv, 0)],
        out_specs=outs(0) + outs(1), out_shape=shapes + shapes,
        scratch_shapes=[pltpu.VMEM((2, H, dv, dk), F32)],
        compiler_params=_params("arbitrary"))(proj, proj, proj, z, proj, proj, proj, z, sprev[0], sprev[1], do, do)
    return (res[0], res[4]), (res[1], res[5]), (res[2], res[6]), (res[3], res[7])


def _l0_combine_fwd(hs, proj, o, gain, name):
    T = proj.shape[0]
    tm = min(512, T)

    def body(hf, hb, ga, of, ob, g, gn, out):
        out[...] = _l0_combine(hf[...], hb[...], ga[...], of[...], ob[...], g[...], gn[...]).astype(BF16)

    tok = pl.BlockSpec((tm, 512), lambda i: (i, 0))
    return pl.pallas_call(
        body, name=name, grid=(T // tm,),
        in_specs=[tok, tok, pl.BlockSpec((tm, 512), lambda i: (i, 1)), tok, tok,
                  pl.BlockSpec((tm, 512), lambda i: (i, 6)), pl.BlockSpec((1, 512), lambda i: (0, 0))],
        out_specs=pl.BlockSpec((tm, 1024), lambda i: (i, 0)),
        out_shape=jax.ShapeDtypeStruct((T, 1024), BF16),
        compiler_params=_params("parallel"))(hs[0], hs[1], proj, o[0], o[1], proj, gain)


def _l0_combine_bwd(hs, proj, o, gain, dmix, name):
    T = proj.shape[0]
    tm = min(512, T)

    def body(hf, hb, ga, of, ob, g, gn, dm, dho_ref, dga_ref, do_ref, dg_ref, dgn_ref):
        @pl.when(pl.program_id(0) == 0)
        def _():
            dgn_ref[...] = jnp.zeros_like(dgn_ref)

        _, vjp = jax.vjp(_l0_combine, hf[...], hb[...], ga[...], of[...], ob[...], g[...], gn[...])
        dhf, _, dga, dof, _, dg, dgn = vjp(dm[...])
        dho_ref[...] = dhf
        dga_ref[...] = dga
        do_ref[...] = dof
        dg_ref[...] = dg
        dgn_ref[...] += dgn

    tok = lambda: pl.BlockSpec((tm, 512), lambda i: (i, 0))
    return pl.pallas_call(
        body, name=name, grid=(T // tm,),
        in_specs=[tok(), tok(), pl.BlockSpec((tm, 512), lambda i: (i, 1)), tok(), tok(),
                  pl.BlockSpec((tm, 512), lambda i: (i, 6)), pl.BlockSpec((1, 512), lambda i: (0, 0)),
                  pl.BlockSpec((tm, 1024), lambda i: (i, 0))],
        out_specs=[tok(), tok(), tok(), tok(), pl.BlockSpec((1, 512), lambda i: (0, 0))],
        out_shape=[jax.ShapeDtypeStruct((T, 512), F32)] * 4 + [jax.ShapeDtypeStruct((1, 512), F32)],
        compiler_params=_params("arbitrary"))(hs[0], hs[1], proj, o[0], o[1], proj, gain, dmix)


def _l0_assemble(dxa, dga, dq, df, dv, dg, name):
    T = dxa.shape[0]
    tm = min(512, T)

    def body(xa, ga, q0, q1, f0, f1, v0, v1, g, out):
        out[...] = jnp.concatenate([xa[...], ga[...], q0[...] + q1[...], f0[...], f1[...], v0[...] + v1[...],
                                    g[...]], axis=1).astype(BF16)

    tok = lambda: pl.BlockSpec((tm, 512), lambda i: (i, 0))
    return pl.pallas_call(
        body, name=name, grid=(T // tm,),
        in_specs=[tok() for _ in range(9)],
        out_specs=pl.BlockSpec((tm, AB_IN), lambda i: (i, 0)),
        out_shape=jax.ShapeDtypeStruct((T, AB_IN), BF16),
        compiler_params=_params("parallel"))(dxa, dga, dq[0], dq[1], df[0], df[1], dv[0], dv[1], dg)


def _l1_combine_fwd(o, proj, gain, name):
    T = proj.shape[0]
    tm = min(512, T)

    def body(of, ob, r, gn, out):
        out[...] = _l1_combine(of[...], ob[...], r[...], gn[...]).astype(BF16)

    tok = pl.BlockSpec((tm, 1024), lambda i: (i, 0))
    return pl.pallas_call(
        body, name=name, grid=(T // tm,),
        in_specs=[tok, tok, pl.BlockSpec((tm, 1024), lambda i: (i, 2)), pl.BlockSpec((1, 1024), lambda i: (0, 0))],
        out_specs=pl.BlockSpec((tm, 1024), lambda i: (i, 0)),
        out_shape=jax.ShapeDtypeStruct((T, 1024), BF16),
        compiler_params=_params("parallel"))(o[0], o[1], proj, gain)


def _l1_combine_bwd(o, proj, gain, dmix, name):
    T = proj.shape[0]
    tm = min(512, T)

    def body(of, ob, r, gn, dm, do_ref, dr_ref, dgn_ref):
        @pl.when(pl.program_id(0) == 0)
        def _():
            dgn_ref[...] = jnp.zeros_like(dgn_ref)

        _, vjp = jax.vjp(_l1_combine, of[...], ob[...], r[...], gn[...])
        dof, _, dr, dgn = vjp(dm[...])
        do_ref[...] = dof
        dr_ref[...] = dr
        dgn_ref[...] += dgn

    tok = lambda: pl.BlockSpec((tm, 1024), lambda i: (i, 0))
    return pl.pallas_call(
        body, name=name, grid=(T // tm,),
        in_specs=[tok(), tok(), pl.BlockSpec((tm, 1024), lambda i: (i, 2)),
                  pl.BlockSpec((1, 1024), lambda i: (0, 0)), tok()],
        out_specs=[tok(), tok(), pl.BlockSpec((1, 1024), lambda i: (0, 0))],
        out_shape=[jax.ShapeDtypeStruct((T, 1024), F32)] * 2 + [jax.ShapeDtypeStruct((1, 1024), F32)],
        compiler_params=_params("arbitrary"))(o[0], o[1], proj, gain, dmix)


def _l1_assemble(dq, dk, dv, dr, dlr, name):
    T = dr.shape[0]
    tm = min(512, T)

    def body(q0, q1, k0, k1, v0, v1, r, a, out):
        out[...] = jnp.concatenate([q0[...] + q1[...], k0[...] + k1[...], v0[...] + v1[...], r[...], a[...]],
                                   axis=1).astype(BF16)

    tok = lambda w: pl.BlockSpec((tm, w), lambda i: (i, 0))
    return pl.pallas_call(
        body, name=name, grid=(T // tm,),
        in_specs=[tok(512), tok(512), tok(512), tok(512), tok(1024), tok(1024), tok(1024), tok(LANES)],
        out_specs=pl.BlockSpec((tm, GLA_IN_PAD), lambda i: (i, 0)),
        out_shape=jax.ShapeDtypeStruct((T, GLA_IN_PAD), BF16),
        compiler_params=_params("parallel"))(dq[0], dq[1], dk[0], dk[1], dv[0], dv[1], dr, dlr)


HBM_SPEC = pl.BlockSpec(memory_space=pltpu.HBM)


def _place():
    x, y, c = lax.axis_index("x"), lax.axis_index("y"), lax.axis_index("c")
    return x, y, c


def _allgather_hbm(shards, name):
    n = len(shards)

    def body(*refs):
        ins, outs = refs[:n], refs[n:2 * n]
        send_sems, recv_sems, local_sems = refs[2 * n:]
        x, y, c = _place()
        me, sibling = (x, y, c), (x, y, 1 - c)
        chips = [(1 - x, y), (x, 1 - y), (1 - x, 1 - y)]

        def slot(a, p):
            return outs[a].at[4 * p[0] + 2 * p[1] + p[2]]

        def copy(a, k, block, to, src=None):
            return pltpu.make_async_remote_copy(
                src_ref=slot(a, block) if src is None else src, dst_ref=slot(a, block),
                send_sem=send_sems.at[a * 7 + k], recv_sem=recv_sems.at[a * 7 + k],
                device_id=to, device_id_type=MESH)

        mine = [pltpu.make_async_copy(ins[a], slot(a, me), local_sems.at[a]) for a in range(n)]
        for cp in mine:
            cp.start()
        first = []
        for a in range(n):
            first.append(copy(a, 0, me, sibling, src=ins[a]))
            first += [copy(a, 1 + j, me, (*chip, c), src=ins[a]) for j, chip in enumerate(chips)]
        for cp in first:
            cp.start()
        passed = []
        for j, chip in enumerate(chips):
            for a in range(n):
                copy(a, 1 + j, (*chip, c), me).wait_recv()
                cp = copy(a, 4 + j, (*chip, c), sibling)
                cp.start()
                passed.append(cp)
        for a in range(n):
            copy(a, 0, sibling, me).wait_recv()
            for j, chip in enumerate(chips):
                copy(a, 4 + j, (*chip, 1 - c), me).wait_recv()
        for cp in first + passed:
            cp.wait_send()
        for cp in mine:
            cp.wait()

    return pl.pallas_call(
        body, name=name,
        in_specs=[HBM_SPEC] * n, out_specs=[HBM_SPEC] * n,
        out_shape=[jax.ShapeDtypeStruct((N_DEV,) + s.shape, s.dtype) for s in shards],
        scratch_shapes=[pltpu.SemaphoreType.DMA((7 * n,)), pltpu.SemaphoreType.DMA((7 * n,)),
                        pltpu.SemaphoreType.DMA((n,))],
        compiler_params=pltpu.CompilerParams(has_side_effects=True))(*shards)


def _allgather_vmem(x_shard, name, reduce=False):
    m_per, n = x_shard.shape

    def body(x_ref, out_ref, *rest):
        if reduce:
            sum_ref, send_sems, recv_sems, local_sem = rest
        else:
            send_sems, recv_sems, local_sem = rest
        x, y, c = _place()
        me, sibling = (x, y, c), (x, y, 1 - c)
        chips = [(1 - x, y), (x, 1 - y), (1 - x, 1 - y)]

        def rows(px, py, pc):
            return out_ref.at[pl.ds((4 * px + 2 * py + pc) * m_per, m_per), :]

        def copy(k, block, to, src=None):
            return pltpu.make_async_remote_copy(
                src_ref=rows(*block) if src is None else src, dst_ref=rows(*block),
                send_sem=send_sems.at[k], recv_sem=recv_sems.at[k], device_id=to, device_id_type=MESH)

        mine = pltpu.make_async_copy(x_ref, rows(*me), local_sem)
        mine.start()
        first = [copy(0, me, sibling, src=x_ref)]
        first += [copy(1 + j, me, (*chip, c), src=x_ref) for j, chip in enumerate(chips)]
        for cp in first:
            cp.start()
        passed = [copy(4 + j, (*chip, c), sibling) for j, chip in enumerate(chips)]
        for j, chip in enumerate(chips):
            copy(1 + j, (*chip, c), me).wait_recv()
            passed[j].start()
        copy(0, sibling, me).wait_recv()
        for j, chip in enumerate(chips):
            copy(4 + j, (*chip, 1 - c), me).wait_recv()
        for cp in first + passed:
            cp.wait_send()
        mine.wait()
        if reduce:
            acc = out_ref[pl.ds(0, m_per), :]
            for j in range(1, N_DEV):
                acc = acc + out_ref[pl.ds(j * m_per, m_per), :]
            sum_ref[...] = acc

    vm = pl.BlockSpec(memory_space=pltpu.VMEM)
    out_shape = [jax.ShapeDtypeStruct((N_DEV * m_per, n), x_shard.dtype)]
    if reduce:
        out_shape.append(jax.ShapeDtypeStruct((m_per, n), x_shard.dtype))
    res = pl.pallas_call(
        body, name=name, in_specs=[vm], out_specs=[vm] * len(out_shape), out_shape=out_shape,
        scratch_shapes=[pltpu.SemaphoreType.DMA((7,)), pltpu.SemaphoreType.DMA((7,)), pltpu.SemaphoreType.DMA],
        compiler_params=pltpu.CompilerParams(has_side_effects=True, vmem_limit_bytes=VMEM_LIMIT))(x_shard)
    return res[1] if reduce else res[0]


SEM_SPEC = pl.BlockSpec(memory_space=pltpu.SEMAPHORE)
DATAFLOW_EFFECT = pltpu.SideEffectType.DATAFLOW_SIDE_EFFECTING


def _copies(plan, srcs, lands, send_sems, recv_sems):
    x, y, c = _place()
    return [pltpu.make_async_remote_copy(src_ref=s, dst_ref=d, send_sem=send_sems.at[k], recv_sem=recv_sems.at[k],
                                         device_id=dev, device_id_type=MESH)
            for k, (s, d, dev) in enumerate(plan(srcs, lands, x, y, c))]


def _copies_start(plan, n_copies, srcs, lands, name):
    ns, nl = len(srcs), len(lands)

    def body(*refs):
        send_sems, recv_sems = refs[ns + nl], refs[ns + nl + 1]
        for cp in _copies(plan, refs[:ns], refs[ns:ns + nl], send_sems, recv_sems):
            cp.start()
        refs[-1][...] = jnp.zeros_like(refs[-1])

    arrays = list(srcs) + list(lands)
    res = pl.pallas_call(
        body, name=name,
        in_specs=[HBM_SPEC] * (ns + nl),
        out_specs=tuple([SEM_SPEC, SEM_SPEC] + [HBM_SPEC] * (ns + nl) + [pl.BlockSpec(memory_space=pltpu.VMEM)]),
        out_shape=tuple([pltpu.SemaphoreType.DMA((n_copies,)), pltpu.SemaphoreType.DMA((n_copies,))]
                        + [pltpu.HBM(a.shape, a.dtype) for a in arrays]
                        + [jax.ShapeDtypeStruct((SUBLANES, LANES), F32)]),
        input_output_aliases={i: 2 + i for i in range(ns + nl)},
        compiler_params=pltpu.CompilerParams(has_side_effects=DATAFLOW_EFFECT),
    )(*[pltpu.with_memory_space_constraint(a, pltpu.HBM) for a in arrays])
    return res[0], res[1], list(res[2:2 + ns]), list(res[2 + ns:2 + ns + nl]), res[-1]


def _copies_wait(plan, started, after, name):
    send_sems, recv_sems, srcs, lands, _ = started
    ns, nl = len(srcs), len(lands)

    def body(*refs):
        for cp in _copies(plan, refs[:ns], refs[ns:ns + nl], refs[ns + nl], refs[ns + nl + 1]):
            cp.wait_send()
            cp.wait_recv()

    arrays = list(srcs) + list(lands)
    res = pl.pallas_call(
        body, name=name,
        in_specs=[HBM_SPEC] * (ns + nl) + [SEM_SPEC, SEM_SPEC, pl.BlockSpec(memory_space=pl.ANY)],
        out_specs=tuple([HBM_SPEC] * (ns + nl)),
        out_shape=tuple(pltpu.HBM(a.shape, a.dtype) for a in arrays),
        input_output_aliases={i: i for i in range(ns + nl)},
        compiler_params=pltpu.CompilerParams(has_side_effects=DATAFLOW_EFFECT),
    )(*arrays, send_sems, recv_sems, after)
    return list(res[:ns]), list(res[ns:])


def _after(token, value):
    return value + token[0:1, 0:1].astype(value.dtype)


def _chips(x, y):
    return [(1 - x, y), (x, 1 - y), (1 - x, 1 - y)]


def _plan_gather_first(srcs, lands, x, y, c):
    me = 4 * x + 2 * y + c
    out = []
    for s, l in zip(srcs, lands):
        out.append((s, l.at[me], (x, y, 1 - c)))
        out += [(s, l.at[me], (*chip, c)) for chip in _chips(x, y)]
    return out


def _plan_gather_pass(srcs, lands, x, y, c):
    out = []
    for l in lands:
        for chip in _chips(x, y):
            slot = l.at[4 * chip[0] + 2 * chip[1] + c]
            out.append((slot, slot, (x, y, 1 - c)))
    return out


def _plan_grads_sibling(srcs, lands, x, y, c):
    return [(s.at[2 * q + (1 - c)], l.at[q], (x, y, 1 - c)) for s, l in zip(srcs, lands) for q in range(4)]


def _plan_grads_chips(srcs, lands, x, y, c):
    return [(s.at[2 * chip[0] + chip[1]], l.at[k], (*chip, c))
            for s, l in zip(srcs, lands) for k, chip in enumerate(_chips(x, y))]


def _landing(n_slots, like):
    return [lax.empty((n_slots,) + a.shape[1:], a.dtype) for a in like]


def _chip_partial(g, r1, place, name):
    _, R, C = g.shape
    tr = min(256, R)
    assert R % tr == 0

    def body(pl_ref, g_ref, r_ref, pb_ref, pm_ref):
        q = pl.program_id(1)
        s = g_ref[...] + r_ref[...]
        pb_ref[...] = s.astype(BF16)

        @pl.when(q == pl_ref[1])
        def _():
            pm_ref[...] = s

    grid_spec = pltpu.PrefetchScalarGridSpec(
        num_scalar_prefetch=1, grid=(R // tr, 4),
        in_specs=[pl.BlockSpec((None, tr, C), lambda r, q, p: (2 * q + p[0], r, 0)),
                  pl.BlockSpec((None, tr, C), lambda r, q, p: (q, r, 0))],
        out_specs=[pl.BlockSpec((None, tr, C), lambda r, q, p: (q, r, 0)),
                   pl.BlockSpec((tr, C), lambda r, q, p: (r, 0))])
    return pl.pallas_call(
        body, name=name, grid_spec=grid_spec,
        out_shape=[jax.ShapeDtypeStruct((4, R, C), BF16), jax.ShapeDtypeStruct((R, C), F32)],
        compiler_params=_params("parallel", "arbitrary"))(place, g, r1)


def _adamw(w, gparts, m, v, name):
    R, C = w.shape
    tr = min(256, R)
    assert R % tr == 0
    g0, g3 = gparts

    def body(w_ref, g0_ref, *rest):
        if g3 is not None:
            g3_ref, m_ref, v_ref, go, do, mo, vo = rest
        else:
            m_ref, v_ref, go, do, mo, vo = rest
        g = g0_ref[...]
        if g3 is not None:
            for k in range(3):
                g = g + g3_ref[k].astype(F32)
        wv = w_ref[...]
        mn = ADAM_B1 * m_ref[...] + (1.0 - ADAM_B1) * g
        vn = ADAM_B2 * v_ref[...] + (1.0 - ADAM_B2) * jnp.square(g)
        m_hat = mn / (1.0 - ADAM_B1 ** ADAM_STEP)
        v_hat = vn / (1.0 - ADAM_B2 ** ADAM_STEP)
        go[...] = g
        do[...] = -ADAM_LR * (m_hat / (jnp.sqrt(v_hat) + ADAM_EPS) + ADAM_WD * wv)
        mo[...] = mn
        vo[...] = vn

    blk = pl.BlockSpec((tr, C), lambda i: (i, 0))
    in_specs = [blk, blk] + ([pl.BlockSpec((3, tr, C), lambda i: (0, i, 0))] if g3 is not None else []) + [blk, blk]
    args = [w, g0] + ([g3] if g3 is not None else []) + [m, v]
    return pl.pallas_call(
        body, name=name, grid=(R // tr,), in_specs=in_specs, out_specs=[blk] * 4,
        out_shape=[jax.ShapeDtypeStruct((R, C), F32)] * 4,
        compiler_params=_params("parallel"))(*args)


SMALL_SHARDED = ("rg_conv_w", "rg_b_a", "rg_b_x", "rg_lambda", "gla_w_gate_up", "gla_b_gate", "gla_norm")
SMALL_REPLICATED = ("norm_mix", "norm_mlp", "norm_final", "rg_conv_b", "rg_w_a", "rg_w_x", "hg_lb_logits", "hg_norm")
WEIGHT_NAMES = ("norm_mix", "norm_mlp", "norm_final", "mlp_w1", "mlp_w2", "ab_w_in", "ab_w_out", "rg_conv_w",
                "rg_conv_b", "rg_w_a", "rg_b_a", "rg_w_x", "rg_b_x", "rg_lambda", "hg_lb_logits", "hg_norm",
                "gla_w_in", "gla_w_out", "gla_w_gate_up", "gla_b_gate", "gla_norm")


def _rows128(a):
    return a.reshape(-1, LANES)


def _part_rows(a):
    return -(-(a.size // LANES) // SUBLANES) * SUBLANES


def _pack_rows(arrays, pad_to=SUBLANES):
    parts = [jnp.pad(_rows128(a), ((0, _part_rows(a) - a.size // LANES), (0, 0))) for a in arrays]
    total = sum(p.shape[0] for p in parts)
    extra = (-total) % pad_to
    if extra:
        parts.append(jnp.zeros((extra, LANES), parts[0].dtype))
    return jnp.concatenate(parts, axis=0)


def _unshard_last(g, shape_local):
    nd = len(shape_local)
    t = g.reshape((N_DEV,) + tuple(shape_local))
    t = jnp.moveaxis(t, 0, nd - 1)
    return t.reshape(tuple(shape_local[:-1]) + (N_DEV * shape_local[-1],))


def _block_diag(w):
    eye = jnp.eye(8, dtype=w.dtype)
    return (w[:, :, :, None, :] * eye[None, :, None, :, None]).reshape(2, RG_W, RG_W)


def _block_diag_extract(dw):
    t = dw.reshape(2, 8, 64, 8, 64)
    return jnp.moveaxis(jnp.diagonal(t, axis1=1, axis2=3), -1, 1)


def kernel(x, norm_mix, norm_mlp, norm_final, mlp_w1, mlp_w2, ab_w_in, ab_w_out, rg_conv_w, rg_conv_b, rg_w_a, rg_b_a, rg_w_x, rg_b_x, rg_lambda, hg_lb_logits, hg_norm, gla_w_in, gla_w_out, gla_w_gate_up, gla_b_gate, gla_norm, loss_target, m_norm_mix, m_norm_mlp, m_norm_final, m_mlp_w1, m_mlp_w2, m_ab_w_in, m_ab_w_out, m_rg_conv_w, m_rg_conv_b, m_rg_w_a, m_rg_b_a, m_rg_w_x, m_rg_b_x, m_rg_lambda, m_hg_lb_logits, m_hg_norm, m_gla_w_in, m_gla_w_out, m_gla_w_gate_up, m_gla_b_gate, m_gla_norm, v_norm_mix, v_norm_mlp, v_norm_final, v_mlp_w1, v_mlp_w2, v_ab_w_in, v_ab_w_out, v_rg_conv_w, v_rg_conv_b, v_rg_w_a, v_rg_b_a, v_rg_w_x, v_rg_b_x, v_rg_lambda, v_hg_lb_logits, v_hg_norm, v_gla_w_in, v_gla_w_out, v_gla_w_gate_up, v_gla_b_gate, v_gla_norm):
    w_loc = dict(norm_mix=norm_mix, norm_mlp=norm_mlp, norm_final=norm_final, mlp_w1=mlp_w1, mlp_w2=mlp_w2,
                 ab_w_in=ab_w_in, ab_w_out=ab_w_out, rg_conv_w=rg_conv_w, rg_conv_b=rg_conv_b, rg_w_a=rg_w_a,
                 rg_b_a=rg_b_a, rg_w_x=rg_w_x, rg_b_x=rg_b_x, rg_lambda=rg_lambda, hg_lb_logits=hg_lb_logits,
                 hg_norm=hg_norm, gla_w_in=gla_w_in, gla_w_out=gla_w_out, gla_w_gate_up=gla_w_gate_up,
                 gla_b_gate=gla_b_gate, gla_norm=gla_norm)
    m_loc = dict(norm_mix=m_norm_mix, norm_mlp=m_norm_mlp, norm_final=m_norm_final, mlp_w1=m_mlp_w1,
                 mlp_w2=m_mlp_w2, ab_w_in=m_ab_w_in, ab_w_out=m_ab_w_out, rg_conv_w=m_rg_conv_w,
                 rg_conv_b=m_rg_conv_b, rg_w_a=m_rg_w_a, rg_b_a=m_rg_b_a, rg_w_x=m_rg_w_x, rg_b_x=m_rg_b_x,
                 rg_lambda=m_rg_lambda, hg_lb_logits=m_hg_lb_logits, hg_norm=m_hg_norm, gla_w_in=m_gla_w_in,
                 gla_w_out=m_gla_w_out, gla_w_gate_up=m_gla_w_gate_up, gla_b_gate=m_gla_b_gate,
                 gla_norm=m_gla_norm)
    v_loc = dict(norm_mix=v_norm_mix, norm_mlp=v_norm_mlp, norm_final=v_norm_final, mlp_w1=v_mlp_w1,
                 mlp_w2=v_mlp_w2, ab_w_in=v_ab_w_in, ab_w_out=v_ab_w_out, rg_conv_w=v_rg_conv_w,
                 rg_conv_b=v_rg_conv_b, rg_w_a=v_rg_w_a, rg_b_a=v_rg_b_a, rg_w_x=v_rg_w_x, rg_b_x=v_rg_b_x,
                 rg_lambda=v_rg_lambda, hg_lb_logits=v_hg_lb_logits, hg_norm=v_hg_norm, gla_w_in=v_gla_w_in,
                 gla_w_out=v_gla_w_out, gla_w_gate_up=v_gla_w_gate_up, gla_b_gate=v_gla_b_gate,
                 gla_norm=v_gla_norm)

    T = x.shape[1]
    h0 = x.reshape(T, D_MODEL)
    target = loss_target.reshape(T, D_MODEL)
    ax, ay, ac = lax.axis_index("x"), lax.axis_index("y"), lax.axis_index("c")
    dev = 4 * ax + 2 * ay + ac
    place = jnp.stack([ac, 2 * ax + ay]).astype(jnp.int32)

    (abin_g,) = _allgather_hbm([ab_w_in[0].astype(BF16)], "ag_first")
    wab_in = jnp.transpose(abin_g, (1, 0, 2)).reshape(D_MODEL, AB_IN)
    rest_shards = [mlp_w1[0].astype(BF16), mlp_w2[0].astype(BF16), gla_w_in[0].astype(BF16),
                   gla_w_out[0].astype(BF16), mlp_w1[1].astype(BF16), mlp_w2[1].astype(BF16),
                   ab_w_out[0].astype(BF16)]
    ag_started = _copies_start(_plan_gather_first, 4 * len(rest_shards), rest_shards,
                               _landing(N_DEV, [s[None] for s in rest_shards]), "ag_rest_start")

    small_local = [w_loc[n] for n in SMALL_SHARDED]
    small_g = _allgather_vmem(_pack_rows(small_local, 8), "ag_small")
    small_g = small_g.reshape(N_DEV, -1, LANES)
    full = {}
    off = 0
    for n, a in zip(SMALL_SHARDED, small_local):
        full[n] = _unshard_last(small_g[:, off:off + a.size // LANES].reshape(N_DEV, a.size), a.shape)
        off += _part_rows(a)
    conv_w = full["rg_conv_w"][0]
    b_a, b_x, lam = full["rg_b_a"][0], full["rg_b_x"][0], full["rg_lambda"][0]
    w_up, b_gate, g_norm = full["gla_w_gate_up"][0], full["gla_b_gate"][0], full["gla_norm"]

    cw8 = jnp.pad(conv_w, ((0, 4), (0, 0)))
    wbd = jnp.concatenate([_block_diag(rg_w_a[0]), _block_diag(rg_w_x[0])], axis=2).astype(BF16)
    rg_bias = jnp.concatenate([b_a, b_x], axis=1).reshape(2, 1, 2 * RG_W)
    lam3 = lam.reshape(2, 1, RG_W)
    l0, l1 = hg_lb_logits[0:1], hg_lb_logits[1:2]
    wup_pad = jnp.zeros((2, LANES, 512), F32).at[0, 0:16].set(w_up[0]).at[1, 16:32].set(w_up[1])
    bg3 = b_gate.reshape(2, 1, 512)
    nmix0, nmix1 = norm_mix[0:1], norm_mix[1:2]
    nmlp0, nmlp1 = norm_mlp[0:1], norm_mlp[1:2]
    nfin = norm_final.reshape(1, D_MODEL)

    proj0, y0 = _norm_matmul(h0, _after(ag_started[4], nmix0), wab_in, "l0_in_proj")
    xc = _rg_conv_fwd(proj0, cw8, rg_conv_b, "rg_conv")
    hs = _rg_scan_fwd(xc, wbd, rg_bias, lam3, "rg_scan")
    o_hg, s_hg = _hg_fwd(proj0, l0, l1, "hg_chunks")
    rest_shards, rest_lands = _copies_wait(_plan_gather_first, ag_started, hs[0], "ag_rest_wait")
    pass_started = _copies_start(_plan_gather_pass, 3 * len(rest_lands), [], rest_lands, "ag_pass_start")
    mixin0 = _l0_combine_fwd(hs, proj0, o_hg, _after(pass_started[4], hg_norm), "l0_combine")
    _, rest_g = _copies_wait(_plan_gather_pass, pass_started, mixin0, "ag_pass_wait")
    rest_g = [lax.dynamic_update_index_in_dim(g, s, dev, 0) for g, s in zip(rest_g, rest_shards)]
    wab_out = rest_g[6].reshape(D_MODEL, D_MODEL)
    h1 = _matmul_res(mixin0, wab_out, h0, "l0_out_proj")
    w1g = (rest_g[0], rest_g[4])
    w2f = (rest_g[1].reshape(D_FF, D_MODEL), rest_g[5].reshape(D_FF, D_MODEL))
    wgla_in = jnp.pad(jnp.transpose(rest_g[2], (1, 0, 2)).reshape(D_MODEL, GLA_IN),
                      ((0, 0), (0, GLA_IN_PAD - GLA_IN)))
    wgla_out = rest_g[3].reshape(D_MODEL, D_MODEL)
    h2, pre0, ym0 = _mlp_fwd(h1, nmlp0, w1g[0], w2f[0], "mlp0")
    proj1, y1 = _norm_matmul(h2, nmix1, wgla_in, "l1_in_proj")
    z_gate, lr_b = _gate_logits(proj1, wup_pad, bg3, "gla_gate_logits")
    o_gla, s_gla = _gla_fwd(proj1, z_gate, "gla_chunks")
    mixin1 = _l1_combine_fwd(o_gla, proj1, g_norm, "l1_combine")
    h3 = _matmul_res(mixin1, wgla_out, h2, "l1_out_proj")
    h4, pre1, ym1 = _mlp_fwd(h3, nmlp1, w1g[1], w2f[1], "mlp1")
    loss_blk, dh4, dh4b, d_nfin = _final_loss(h4, nfin, target, "final_loss")
    loss = lax.psum(loss_blk[0, 0], ("x", "y", "c"))

    dh3, dh3b, dpre1, act1, d_nmlp1 = _mlp_bwd(dh4, h3, nmlp1, pre1, w1g[1], w2f[1], "mlp1_bwd")
    g_w1_1 = _wgrad(ym1, dpre1, 512, "mlp1_dw1", sharded_cols=True)
    g_w2_1 = _wgrad(act1, dh4b, 1024, "mlp1_dw2")
    dmixin1 = _dgrad(dh3b, wgla_out, "l1_out_dgrad")
    g_gla_out = _wgrad(mixin1, dh3b, 1024, "l1_out_dw")
    do_gla, dr, d_gnorm = _l1_combine_bwd(o_gla, proj1, g_norm, dmixin1, "l1_combine_bwd")
    dq1, dk1, dv1, dz_gate = _gla_bwd(proj1, z_gate, s_gla, do_gla, "gla_chunks_bwd")
    dlr1, d_bg, dz_b = _gate_logits_bwd(dz_gate, wup_pad, "gla_gate_logits_bwd")
    d_wup = [_wgrad(lr_b, dz_b[d], 512, "gla_gate_dw%d" % d) for d in range(2)]
    dproj1 = _l1_assemble(dq1, dk1, dv1, dr, dlr1, "l1_assemble")
    dh2, dh2b, d_nmix1 = _dgrad_norm(dproj1, wgla_in, h2, nmix1, dh3, "l1_in_dgrad")
    g_gla_in = _wgrad(y1, dproj1, 640, "l1_in_dw")

    def reduce_start(grads, tag):
        return _copies_start(_plan_grads_sibling, 4 * len(grads), grads, _landing(4, grads), "rs_%s_d2d_start" % tag)

    def reduce_mid(started, after, tag):
        grads, got = _copies_wait(_plan_grads_sibling, started, after, "rs_%s_d2d_wait" % tag)
        parts = [_chip_partial(g, r, place, "rs_%s_partial%d" % (tag, a)) for a, (g, r) in enumerate(zip(grads, got))]
        pb = [p[0] for p in parts]
        return _copies_start(_plan_grads_chips, 3 * len(pb), pb, _landing(3, pb), "rs_%s_ici_start" % tag), \
            [p[1] for p in parts]

    def reduce_end(started, mine, after, tag, ws, ms, vs):
        _, got = _copies_wait(_plan_grads_chips, started, after, "rs_%s_ici_wait" % tag)
        return [_adamw(w, (p, r), m, v, "adamw_%s%d" % (tag, a))
                for a, (w, p, r, m, v) in enumerate(zip(ws, mine, got, ms, vs))]

    slots_l1 = [g_w1_1, g_w2_1.reshape(N_DEV, 512, D_MODEL),
                jnp.transpose(g_gla_in[:, :GLA_IN].reshape(D_MODEL, N_DEV, GLA_IN // N_DEV), (1, 0, 2)),
                g_gla_out.reshape(N_DEV, 128, D_MODEL)]
    ra_d2d = reduce_start(slots_l1, "l1")

    dh1, dh1b, dpre0, act0, d_nmlp0 = _mlp_bwd(dh2, h1, _after(ra_d2d[4], nmlp0), pre0, w1g[0], w2f[0], "mlp0_bwd")
    g_w1_0 = _wgrad(ym0, dpre0, 512, "mlp0_dw1", sharded_cols=True)
    g_w2_0 = _wgrad(act0, dh2b, 1024, "mlp0_dw2")
    ra_ici, ra_mine = reduce_mid(ra_d2d, g_w2_0, "l1")
    rb_d2d = reduce_start([g_w1_0, g_w2_0.reshape(N_DEV, 512, D_MODEL)], "mlp0")
    dmixin0 = _dgrad(dh1b, wab_out, "l0_out_dgrad")
    g_ab_out = _wgrad(mixin0, dh1b, 1024, "l0_out_dw")
    dho, dga, do_hg, dg_gate, d_hgnorm = _l0_combine_bwd(
        hs, proj0, o_hg, _after(rb_d2d[4], _after(ra_ici[4], hg_norm)), dmixin0, "l0_combine_bwd")
    dxc, d_wbd, d_rgb, d_lam = _rg_scan_bwd(xc, wbd, rg_bias, lam3, hs, dho, "rg_scan_bwd")
    dxa, d_cw8, d_cb = _rg_conv_bwd(dxc, proj0, cw8, "rg_conv_bwd")
    dq0, df0, dv0, d_l0, d_l1 = _hg_bwd(proj0, l0, l1, s_hg, do_hg, "hg_chunks_bwd")
    rb_ici, rb_mine = reduce_mid(rb_d2d, d_l0, "mlp0")
    dproj0 = _l0_assemble(dxa, dga, dq0, df0, dv0, dg_gate, "l0_assemble")
    g_ab_in = _wgrad(y0, dproj0, 512, "l0_in_dw")
    rc_d2d = reduce_start([jnp.transpose(g_ab_in.reshape(D_MODEL, N_DEV, AB_IN // N_DEV), (1, 0, 2)),
                           g_ab_out.reshape(N_DEV, 128, D_MODEL)], "ab")
    dx, _, d_nmix0 = _dgrad_norm(dproj0, wab_in, h0, _after(rc_d2d[4], _after(rb_ici[4], nmix0)), dh1,
                                 "l0_in_dgrad")
    rc_ici, rc_mine = reduce_mid(rc_d2d, d_nmix0, "ab")

    res_l1 = reduce_end(ra_ici, ra_mine, rc_ici[4], "l1",
                        [mlp_w1[1], mlp_w2[1], gla_w_in[0], gla_w_out[0]],
                        [m_mlp_w1[1], m_mlp_w2[1], m_gla_w_in[0], m_gla_w_out[0]],
                        [v_mlp_w1[1], v_mlp_w2[1], v_gla_w_in[0], v_gla_w_out[0]])
    res_mlp0 = reduce_end(rb_ici, rb_mine, res_l1[3][0], "mlp0", [mlp_w1[0], mlp_w2[0]],
                          [m_mlp_w1[0], m_mlp_w2[0]], [v_mlp_w1[0], v_mlp_w2[0]])

    def stacked(a, b):
        return tuple(jnp.stack([a[k], b[k]]) for k in range(4))

    res = {"mlp_w1": stacked(res_mlp0[0], res_l1[0]), "mlp_w2": stacked(res_mlp0[1], res_l1[1]),
           "gla_w_in": tuple(res_l1[2][k][None] for k in range(4)),
           "gla_w_out": tuple(res_l1[3][k][None] for k in range(4))}

    d_wa = _block_diag_extract(d_wbd[:, :, :RG_W])[None]
    d_wx = _block_diag_extract(d_wbd[:, :, RG_W:])[None]
    small_full = {
        "norm_mix": jnp.concatenate([d_nmix0, d_nmix1], axis=0), "norm_mlp": jnp.concatenate([d_nmlp0, d_nmlp1], axis=0),
        "norm_final": d_nfin.reshape(D_MODEL), "rg_conv_b": d_cb, "rg_w_a": d_wa, "rg_w_x": d_wx,
        "hg_lb_logits": jnp.concatenate([d_l0[0] + d_l0[1], d_l1[0] + d_l1[1]], axis=0), "hg_norm": d_hgnorm,
        "rg_conv_w": d_cw8[0:4][None], "rg_b_a": d_rgb[:, 0, :RG_W][None], "rg_b_x": d_rgb[:, 0, RG_W:][None],
        "rg_lambda": d_lam[:, 0, :][None],
        "gla_w_gate_up": jnp.stack([d_wup[0][0:16], d_wup[1][16:32]])[None], "gla_b_gate": d_bg[:, 0, :][None],
        "gla_norm": d_gnorm}
    small_names = SMALL_REPLICATED + SMALL_SHARDED
    packed = _pack_rows([small_full[n] for n in small_names], 8)
    summed = _allgather_vmem(packed, "ar_small", reduce=True)
    g_small = {}
    off = 0
    for n in small_names:
        a = small_full[n]
        gfull = summed[off:off + a.size // LANES].reshape(a.shape)
        off += _part_rows(a)
        if n in SMALL_SHARDED:
            loc = w_loc[n].shape[-1]
            gfull = lax.dynamic_slice_in_dim(gfull, dev * loc, loc, axis=gfull.ndim - 1)
        g_small[n] = gfull
    sw = _pack_rows([w_loc[n] for n in small_names], 256)
    sg = _pack_rows([g_small[n] for n in small_names], 256)
    sm = _pack_rows([m_loc[n] for n in small_names], 256)
    sv = _pack_rows([v_loc[n] for n in small_names], 256)
    small_res = _adamw(sw, (sg, None), sm, sv, "adamw_small")
    res_ab = reduce_end(rc_ici, rc_mine, res_mlp0[1][0], "ab", [ab_w_in[0], ab_w_out[0]],
                        [m_ab_w_in[0], m_ab_w_out[0]], [v_ab_w_in[0], v_ab_w_out[0]])
    res["ab_w_in"] = tuple(res_ab[0][k][None] for k in range(4))
    res["ab_w_out"] = tuple(res_ab[1][k][None] for k in range(4))
    off = 0
    for n in small_names:
        a = w_loc[n]
        nr = a.size // LANES
        res[n] = tuple(small_res[k][off:off + nr].reshape(a.shape) for k in range(4))
        off += _part_rows(a)

    grad_x = dx.reshape(1, T, D_MODEL)
    out = [loss, grad_x]
    for k in range(4):
        out += [res[n][k] for n in WEIGHT_NAMES]
    return tuple(out)
```

```python
import jax
import jax.numpy as jnp
from jax import lax
from jax.experimental import pallas as pl
from jax.experimental.pallas import tpu as pltpu

F32, BF16 = jnp.float32, jnp.bfloat16
HI = lax.Precision.HIGHEST
MESH = pl.DeviceIdType.MESH

D_MODEL = 1024
D_FF = 4096
RG_W = 512
HG_W = 512
CHUNK = 64
EPS = 1e-6
RG_C = 8.0
AB_IN = 3584
GLA_IN = 3104
GLA_IN_PAD = 3200
N_DEV = 8
LANES = 128
SUBLANES = 8
VMEM_LIMIT = 48 * 1024 * 1024

ADAM_LR, ADAM_B1, ADAM_B2, ADAM_EPS, ADAM_WD, ADAM_STEP = 0.001, 0.9, 0.999, 1e-08, 0.01, 10


def _params(*sem):
    return pltpu.CompilerParams(dimension_semantics=sem, vmem_limit_bytes=VMEM_LIMIT)


def _dg(a, b, ca, cb):
    return lax.dot_general(a.astype(BF16), b.astype(BF16), (((ca,), (cb,)), ((), ())),
                           preferred_element_type=F32)


@jax.custom_vjp
def _mm_nn(a, b):
    return _dg(a, b, 1, 0)


_mm_nn.defvjp(lambda a, b: (_dg(a, b, 1, 0), (a, b)),
              lambda res, g: (_dg(g, res[1], 1, 1), _dg(res[0], g, 0, 0)))


@jax.custom_vjp
def _mm_nt(a, b):
    return _dg(a, b, 1, 1)


_mm_nt.defvjp(lambda a, b: (_dg(a, b, 1, 1), (a, b)),
              lambda res, g: (_dg(g, res[1], 1, 0), _dg(g, res[0], 0, 0)))


@jax.custom_vjp
def _mm_tn(a, b):
    return _dg(a, b, 0, 0)


_mm_tn.defvjp(lambda a, b: (_dg(a, b, 0, 0), (a, b)),
              lambda res, g: (_dg(res[1], g, 1, 1), _dg(res[0], g, 1, 0)))


@jax.custom_vjp
def _cum(tri, tri_t, x):
    return jnp.dot(tri, x, precision=HI, preferred_element_type=F32)


_cum.defvjp(lambda tri, tri_t, x: (jnp.dot(tri, x, precision=HI, preferred_element_type=F32), (tri, tri_t)),
            lambda res, g: (jnp.zeros_like(res[0]), jnp.zeros_like(res[1]),
                            jnp.dot(res[1], g, precision=HI, preferred_element_type=F32)))


def _sig(x):
    return 1.0 / (1.0 + jnp.exp(-x))


def _gelu(x):
    return 0.5 * x * (1.0 + jnp.tanh(0.7978845608028654 * (x + 0.044715 * (x * x * x))))


def _softplus(z):
    return jnp.maximum(z, 0.0) + jnp.log(1.0 + jnp.exp(-jnp.abs(z)))


def _rms(x):
    return lax.rsqrt(jnp.mean(x * x, axis=-1, keepdims=True) + EPS)


def _rmsnorm_bwd(x, gain, dy):
    r = _rms(x)
    xh = x * r
    dgain = jnp.sum(dy * xh, axis=0, keepdims=True)
    dxh = dy * gain
    dx = r * (dxh - xh * jnp.mean(dxh * xh, axis=-1, keepdims=True))
    return dx, dgain


def _headnorm(o, gain, n_heads, hd):
    parts = []
    for h in range(n_heads):
        oh = o[:, h * hd:(h + 1) * hd]
        parts.append(oh * _rms(oh))
    return jnp.concatenate(parts, axis=1) * gain


def _tri_consts(d):
    row = lax.broadcasted_iota(jnp.int32, (CHUNK, CHUNK), 0)
    col = lax.broadcasted_iota(jnp.int32, (CHUNK, CHUNK), 1)
    ge = (row >= col).astype(F32)
    le = (row <= col).astype(F32)
    r1 = lax.broadcasted_iota(jnp.int32, (CHUNK, 1), 0)
    if d == 0:
        return ge, le, (r1 <= CHUNK // 2).astype(F32)
    return le, ge, (r1 >= CHUNK // 2 - 1).astype(F32)


def _chunk_core(qh, k, v, logf, st_prev, tri, tri_t, mref, n_heads, dk, dv):
    cum = _cum(tri, tri_t, logf)
    ref = jnp.sum(logf * mref, axis=0, keepdims=True)
    last = jnp.sum(logf, axis=0, keepdims=True)
    q_in = qh * jnp.exp(cum - ref)
    k_in = k * jnp.exp(ref - cum)
    k_st = k * jnp.exp(last - cum)
    q_dec = qh * jnp.exp(cum)
    decay = jnp.exp(last)
    outs, sts = [], []
    for h in range(n_heads):
        sk = slice(h * dk, (h + 1) * dk)
        sv = slice(h * dv, (h + 1) * dv)
        sc = _mm_nt(q_in[:, sk], k_in[:, sk]) * tri
        o = _mm_nn(sc, v[:, sv]) + _mm_nt(q_dec[:, sk], st_prev[h])
        sts.append(st_prev[h] * decay[:, sk] + _mm_tn(v[:, sv], k_st[:, sk]))
        outs.append(o)
    return jnp.concatenate(outs, axis=1), tuple(sts)


def _hg_chunk(q, f, v, l0, l1, st_prev, tri, tri_t, mref):
    lb = _sig(l0 - l1)
    sg = _sig(f)
    qh = q * _sig(q)
    logf = jnp.log(lb + (1.0 - lb) * sg)
    k = (1.0 - lb) * (1.0 - sg)
    return _chunk_core(qh, k, v, logf, st_prev, tri, tri_t, mref, 4, 128, 128)


def _gla_chunk(q, k, v, z, st_prev, tri, tri_t, mref):
    logf = (jnp.minimum(z, 0.0) - jnp.log(1.0 + jnp.exp(-jnp.abs(z)))) * (1.0 / 16.0)
    qh = q * (128.0 ** -0.5)
    return _chunk_core(qh, k, v, logf, st_prev, tri, tri_t, mref, 4, 128, 256)


def _rg_gates(xc, wbd, bias, lam):
    z = _mm_nn(xc, wbd) + bias
    r = _sig(z[:, :RG_W])
    i = _sig(z[:, RG_W:])
    log_a = -RG_C * r * _softplus(-lam)
    a = jnp.exp(log_a)
    x2 = 2.0 * log_a
    neg_expm1 = jnp.where(x2 > -1e-2, -(x2 + 0.5 * x2 * x2 + x2 * x2 * x2 * (1.0 / 6.0)), 1.0 - jnp.exp(x2))
    u = jnp.sqrt(neg_expm1) * (i * xc)
    return a, u


def _l0_combine(hf, hb, ga, of, ob, g, gain):
    ya = (hf + hb) * _gelu(ga)
    yb = _headnorm(of + ob, gain, 4, 128) * (g * _sig(g))
    return jnp.concatenate([ya, yb], axis=1)


def _l1_combine(of, ob, r, gain):
    return _headnorm(of + ob, gain, 4, 256) * (r * _sig(r))


def _norm_matmul(h, gain, w, name):
    T, D = h.shape
    N = w.shape[1]
    tm = min(512, T)

    def body(h_ref, g_ref, w_ref, o_ref, y_ref):
        x = h_ref[...]
        y = (x * _rms(x) * g_ref[...]).astype(BF16)
        y_ref[...] = y
        o_ref[...] = jnp.dot(y, w_ref[...], preferred_element_type=F32)

    return pl.pallas_call(
        body, name=name, grid=(T // tm,),
        in_specs=[pl.BlockSpec((tm, D), lambda i: (i, 0)), pl.BlockSpec((1, D), lambda i: (0, 0)),
                  pl.BlockSpec((D, N), lambda i: (0, 0))],
        out_specs=[pl.BlockSpec((tm, N), lambda i: (i, 0)), pl.BlockSpec((tm, D), lambda i: (i, 0))],
        out_shape=[jax.ShapeDtypeStruct((T, N), F32), jax.ShapeDtypeStruct((T, D), BF16)],
        compiler_params=_params("parallel"))(h, gain, w)


def _matmul_res(a, w, res, name):
    T, K = a.shape
    N = w.shape[1]
    tm = min(512, T)

    def body(a_ref, w_ref, r_ref, o_ref):
        o_ref[...] = r_ref[...] + jnp.dot(a_ref[...], w_ref[...], preferred_element_type=F32)

    return pl.pallas_call(
        body, name=name, grid=(T // tm,),
        in_specs=[pl.BlockSpec((tm, K), lambda i: (i, 0)), pl.BlockSpec((K, N), lambda i: (0, 0)),
                  pl.BlockSpec((tm, N), lambda i: (i, 0))],
        out_specs=pl.BlockSpec((tm, N), lambda i: (i, 0)),
        out_shape=jax.ShapeDtypeStruct((T, N), F32),
        compiler_params=_params("parallel"))(a, w, res)


def _dgrad(dc, w, name):
    T, N = dc.shape
    K = w.shape[0]
    tm = min(512, T)

    def body(d_ref, w_ref, o_ref):
        o_ref[...] = _dg(d_ref[...], w_ref[...], 1, 1)

    return pl.pallas_call(
        body, name=name, grid=(T // tm,),
        in_specs=[pl.BlockSpec((tm, N), lambda i: (i, 0)), pl.BlockSpec((K, N), lambda i: (0, 0))],
        out_specs=pl.BlockSpec((tm, K), lambda i: (i, 0)),
        out_shape=jax.ShapeDtypeStruct((T, K), F32),
        compiler_params=_params("parallel"))(dc, w)


def _dgrad_norm(dproj, w, h, gain, dres, name):
    T, N = dproj.shape
    D = w.shape[0]
    tm = min(512, T)

    def body(dp_ref, w_ref, h_ref, g_ref, dr_ref, dh_ref, dhb_ref, dg_ref):
        @pl.when(pl.program_id(0) == 0)
        def _():
            dg_ref[...] = jnp.zeros_like(dg_ref)

        dy = _dg(dp_ref[...], w_ref[...], 1, 1)
        dx, dgain = _rmsnorm_bwd(h_ref[...], g_ref[...], dy)
        dh = dr_ref[...] + dx
        dh_ref[...] = dh
        dhb_ref[...] = dh.astype(BF16)
        dg_ref[...] += dgain

    return pl.pallas_call(
        body, name=name, grid=(T // tm,),
        in_specs=[pl.BlockSpec((tm, N), lambda i: (i, 0)), pl.BlockSpec((D, N), lambda i: (0, 0)),
                  pl.BlockSpec((tm, D), lambda i: (i, 0)), pl.BlockSpec((1, D), lambda i: (0, 0)),
                  pl.BlockSpec((tm, D), lambda i: (i, 0))],
        out_specs=[pl.BlockSpec((tm, D), lambda i: (i, 0)), pl.BlockSpec((tm, D), lambda i: (i, 0)),
                   pl.BlockSpec((1, D), lambda i: (0, 0))],
        out_shape=[jax.ShapeDtypeStruct((T, D), F32), jax.ShapeDtypeStruct((T, D), BF16),
                   jax.ShapeDtypeStruct((1, D), F32)],
        compiler_params=_params("arbitrary"))(dproj, w, h, gain, dres)


def _wgrad(a, b, tn, name, sharded_cols=False):
    T, K = a.shape
    N = b.shape[1]
    tk = min(1024, K)
    tt = min(1024, T)
    nt = T // tt

    def body(a_ref, b_ref, o_ref):
        @pl.when(pl.program_id(2) == 0)
        def _():
            o_ref[...] = jnp.zeros_like(o_ref)

        o_ref[...] += _dg(a_ref[...], b_ref[...], 0, 0)

    if sharded_cols:
        out_spec = pl.BlockSpec((None, tk, tn), lambda k, n, t: (n, k, 0))
        out_shape = jax.ShapeDtypeStruct((N // tn, K, tn), F32)
    else:
        out_spec = pl.BlockSpec((tk, tn), lambda k, n, t: (k, n))
        out_shape = jax.ShapeDtypeStruct((K, N), F32)
    return pl.pallas_call(
        body, name=name, grid=(K // tk, N // tn, nt),
        in_specs=[pl.BlockSpec((tt, tk), lambda k, n, t: (t, k)), pl.BlockSpec((tt, tn), lambda k, n, t: (t, n))],
        out_specs=out_spec, out_shape=out_shape,
        compiler_params=_params("parallel", "parallel", "arbitrary"))(a, b)


def _mlp_fwd(h, gain, w1g, w2, name):
    T, D = h.shape
    nf, _, tf = w1g.shape
    tm = min(1024, T)

    def body(h_ref, g_ref, w1_ref, w2_ref, o_ref, pre_ref, y_ref, ysc, acc):
        j = pl.program_id(1)

        @pl.when(j == 0)
        def _():
            x = h_ref[...]
            y = (x * _rms(x) * g_ref[...]).astype(BF16)
            ysc[...] = y
            y_ref[...] = y
            acc[...] = jnp.zeros_like(acc)

        pre = jnp.dot(ysc[...], w1_ref[...], preferred_element_type=F32)
        pre_ref[...] = pre.astype(BF16)
        act = jnp.square(jnp.maximum(pre, 0.0))
        acc[...] += jnp.dot(act.astype(BF16), w2_ref[...], preferred_element_type=F32)

        @pl.when(j == nf - 1)
        def _():
            o_ref[...] = h_ref[...] + acc[...]

    return pl.pallas_call(
        body, name=name, grid=(T // tm, nf),
        in_specs=[pl.BlockSpec((tm, D), lambda i, j: (i, 0)), pl.BlockSpec((1, D), lambda i, j: (0, 0)),
                  pl.BlockSpec((None, D, tf), lambda i, j: (j, 0, 0)), pl.BlockSpec((tf, D), lambda i, j: (j, 0))],
        out_specs=[pl.BlockSpec((tm, D), lambda i, j: (i, 0)), pl.BlockSpec((tm, tf), lambda i, j: (i, j)),
                   pl.BlockSpec((tm, D), lambda i, j: (i, 0))],
        out_shape=[jax.ShapeDtypeStruct((T, D), F32), jax.ShapeDtypeStruct((T, nf * tf), BF16),
                   jax.ShapeDtypeStruct((T, D), BF16)],
        scratch_shapes=[pltpu.VMEM((tm, D), BF16), pltpu.VMEM((tm, D), F32)],
        compiler_params=_params("parallel", "arbitrary"))(h, gain, w1g, w2)


def _mlp_bwd(dout, dout_b, h, gain, pre, w1g, w2, name):
    T, D = h.shape
    nf, _, tf = w1g.shape
    tm = min(512, T)

    def body(do_ref, dob_ref, h_ref, g_ref, pre_ref, w1_ref, w2_ref, dh_ref, dhb_ref, dpre_ref, act_ref, dg_ref, dy):
        i, j = pl.program_id(0), pl.program_id(1)

        @pl.when(j == 0)
        def _():
            dy[...] = jnp.zeros_like(dy)

        @pl.when((i == 0) & (j == 0))
        def _():
            dg_ref[...] = jnp.zeros_like(dg_ref)

        rp = jnp.maximum(pre_ref[...].astype(F32), 0.0)
        dact = _dg(dob_ref[...], w2_ref[...], 1, 1)
        dpre = (dact * (2.0 * rp)).astype(BF16)
        dpre_ref[...] = dpre
        act_ref[...] = (rp * rp).astype(BF16)
        dy[...] += _dg(dpre, w1_ref[...], 1, 1)

        @pl.when(j == nf - 1)
        def _():
            dx, dgain = _rmsnorm_bwd(h_ref[...], g_ref[...], dy[...])
            dh = do_ref[...] + dx
            dh_ref[...] = dh
            dhb_ref[...] = dh.astype(BF16)
            dg_ref[...] += dgain

    return pl.pallas_call(
        body, name=name, grid=(T // tm, nf),
        in_specs=[pl.BlockSpec((tm, D), lambda i, j: (i, 0)), pl.BlockSpec((tm, D), lambda i, j: (i, 0)),
                  pl.BlockSpec((tm, D), lambda i, j: (i, 0)),
                  pl.BlockSpec((1, D), lambda i, j: (0, 0)), pl.BlockSpec((tm, tf), lambda i, j: (i, j)),
                  pl.BlockSpec((None, D, tf), lambda i, j: (j, 0, 0)), pl.BlockSpec((tf, D), lambda i, j: (j, 0))],
        out_specs=[pl.BlockSpec((tm, D), lambda i, j: (i, 0)), pl.BlockSpec((tm, D), lambda i, j: (i, 0)),
                   pl.BlockSpec((tm, tf), lambda i, j: (i, j)),
                   pl.BlockSpec((tm, tf), lambda i, j: (i, j)), pl.BlockSpec((1, D), lambda i, j: (0, 0))],
        out_shape=[jax.ShapeDtypeStruct((T, D), F32), jax.ShapeDtypeStruct((T, D), BF16),
                   jax.ShapeDtypeStruct((T, nf * tf), BF16),
                   jax.ShapeDtypeStruct((T, nf * tf), BF16), jax.ShapeDtypeStruct((1, D), F32)],
        scratch_shapes=[pltpu.VMEM((tm, D), F32)],
        compiler_params=_params("arbitrary", "arbitrary"))(dout, dout_b, h, gain, pre, w1g, w2)


def _final_loss(h, gain, target, name):
    T, D = h.shape
    tm = min(512, T)

    def body(h_ref, g_ref, t_ref, l_ref, dh_ref, dhb_ref, dg_ref):
        @pl.when(pl.program_id(0) == 0)
        def _():
            l_ref[...] = jnp.zeros_like(l_ref)
            dg_ref[...] = jnp.zeros_like(dg_ref)

        x = h_ref[...]
        err = x * _rms(x) * g_ref[...] - t_ref[...]
        l_ref[...] += 0.5 * jnp.sum(jnp.mean(err * err, axis=-1, keepdims=True), axis=0, keepdims=True)
        dx, dgain = _rmsnorm_bwd(x, g_ref[...], err * (1.0 / D))
        dh_ref[...] = dx
        dhb_ref[...] = dx.astype(BF16)
        dg_ref[...] += dgain

    return pl.pallas_call(
        body, name=name, grid=(T // tm,),
        in_specs=[pl.BlockSpec((tm, D), lambda i: (i, 0)), pl.BlockSpec((1, D), lambda i: (0, 0)),
                  pl.BlockSpec((tm, D), lambda i: (i, 0))],
        out_specs=[pl.BlockSpec((SUBLANES, LANES), lambda i: (0, 0)), pl.BlockSpec((tm, D), lambda i: (i, 0)),
                   pl.BlockSpec((tm, D), lambda i: (i, 0)), pl.BlockSpec((1, D), lambda i: (0, 0))],
        out_shape=[jax.ShapeDtypeStruct((SUBLANES, LANES), F32), jax.ShapeDtypeStruct((T, D), F32),
                   jax.ShapeDtypeStruct((T, D), BF16), jax.ShapeDtypeStruct((1, D), F32)],
        compiler_params=_params("arbitrary"))(h, gain, target)


def _halo_specs(tm, T, width, col, tile=lambda i: i):
    r8 = tm // SUBLANES
    nb8 = T // SUBLANES
    return [pl.BlockSpec((tm, width), lambda i: (tile(i), col)),
            pl.BlockSpec((SUBLANES, width), lambda i: (jnp.maximum(tile(i) * r8 - 1, 0), col)),
            pl.BlockSpec((SUBLANES, width), lambda i: (jnp.minimum((tile(i) + 1) * r8, nb8 - 1), col))]


def _ext(cur, prev, nxt, has_prev, has_next):
    return jnp.concatenate([jnp.where(has_prev, prev, 0.0), cur, jnp.where(has_next, nxt, 0.0)], axis=0)


def _shifted(ext, offset, tm):
    n = ext.shape[0]
    sh = (-offset) % n
    r = ext if sh == 0 else pltpu.roll(ext, sh, 0)
    return r[SUBLANES:SUBLANES + tm]


def _rg_conv_fwd(proj, cw8, cb, name):
    T = proj.shape[0]
    tm = min(512, T)
    nT = T // tm

    def body(cur_ref, prev_ref, next_ref, w_ref, b_ref, o_ref):
        i = pl.program_id(0)
        ext = _ext(cur_ref[...], prev_ref[...], next_ref[...], i > 0, i < nT - 1)
        acc = jnp.broadcast_to(b_ref[...], (tm, RG_W))
        for k in range(4):
            acc = acc + w_ref[k:k + 1, :] * _shifted(ext, k - 2, tm)
        o_ref[...] = acc

    return pl.pallas_call(
        body, name=name, grid=(nT,),
        in_specs=_halo_specs(tm, T, RG_W, 0) + [pl.BlockSpec((SUBLANES, RG_W), lambda i: (0, 0)),
                                                pl.BlockSpec((1, RG_W), lambda i: (0, 0))],
        out_specs=pl.BlockSpec((tm, RG_W), lambda i: (i, 0)),
        out_shape=jax.ShapeDtypeStruct((T, RG_W), F32),
        compiler_params=_params("parallel"))(proj, proj, proj, cw8, cb)


def _rg_conv_bwd(dxc, proj, cw8, name):
    T = proj.shape[0]
    tm = min(512, T)
    nT = T // tm

    def body(a0, p0, n0, a1, p1, n1, xa, xp, xn, w_ref, dxa_ref, dw_ref, db_ref):
        i = pl.program_id(0)

        @pl.when(i == 0)
        def _():
            dw_ref[...] = jnp.zeros_like(dw_ref)
            db_ref[...] = jnp.zeros_like(db_ref)

        has_p, has_n = i > 0, i < nT - 1
        cur = a0[...] + a1[...]
        dext = _ext(cur, p0[...] + p1[...], n0[...] + n1[...], has_p, has_n)
        xext = _ext(xa[...], xp[...], xn[...], has_p, has_n)
        acc = jnp.zeros((tm, RG_W), F32)
        rows = []
        for k in range(4):
            acc = acc + w_ref[k:k + 1, :] * _shifted(dext, 2 - k, tm)
            rows.append(jnp.sum(cur * _shifted(xext, k - 2, tm), axis=0, keepdims=True))
        dxa_ref[...] = acc
        dw_ref[...] += jnp.concatenate(rows + [jnp.zeros((4, RG_W), F32)], axis=0)
        db_ref[...] += jnp.sum(cur, axis=0, keepdims=True)

    return pl.pallas_call(
        body, name=name, grid=(nT,),
        in_specs=(_halo_specs(tm, T, RG_W, 0) + _halo_specs(tm, T, RG_W, 0)
                  + _halo_specs(tm, T, RG_W, 0) + [pl.BlockSpec((SUBLANES, RG_W), lambda i: (0, 0))]),
        out_specs=[pl.BlockSpec((tm, RG_W), lambda i: (i, 0)), pl.BlockSpec((SUBLANES, RG_W), lambda i: (0, 0)),
                   pl.BlockSpec((1, RG_W), lambda i: (0, 0))],
        out_shape=[jax.ShapeDtypeStruct((T, RG_W), F32), jax.ShapeDtypeStruct((SUBLANES, RG_W), F32),
                   jax.ShapeDtypeStruct((1, RG_W), F32)],
        compiler_params=_params("arbitrary"))(dxc[0], dxc[0], dxc[0], dxc[1], dxc[1], dxc[1], proj, proj, proj, cw8)


def _local_scan(a, b, ascending):
    n = a.shape[0]
    pos = jnp.bitwise_and(lax.broadcasted_iota(jnp.int32, a.shape, 0), SUBLANES - 1)
    for s in (1, 2, 4):
        sh = s if ascending else n - s
        ok = (pos >= s) if ascending else (pos < SUBLANES - s)
        a_sh, b_sh = pltpu.roll(a, sh, 0), pltpu.roll(b, sh, 0)
        b = jnp.where(ok, a * b_sh + b, b)
        a = jnp.where(ok, a * a_sh, a)
    return a, b


def _group_scan(chains, a_sc, b_sc, carry, n_groups):
    def step(g, hs):
        new = []
        for (d, out_ref, asc), h in zip(chains, hs):
            r0 = pl.multiple_of((g if asc else n_groups - 1 - g) * SUBLANES, SUBLANES)
            out_ref[pl.ds(r0, SUBLANES), :] = a_sc[d, pl.ds(r0, SUBLANES), :] * h + b_sc[d, pl.ds(r0, SUBLANES), :]
            new.append(out_ref[pl.ds(r0 + (SUBLANES - 1 if asc else 0), 1), :])
        return tuple(new)

    hs = lax.fori_loop(0, n_groups, step, tuple(carry[d, 0:1, :] for d, _, _ in chains))
    for (d, _, _), h in zip(chains, hs):
        carry[d, 0:1, :] = h


def _rg_scan_fwd(xc, wbd, bias, lam, name):
    T = xc.shape[0]
    tm = min(512, T)
    nT = T // tm

    def body(xf_ref, xb_ref, w_ref, b_ref, lam_ref, hf_ref, hb_ref, a_sc, b_sc, carry):
        @pl.when(pl.program_id(0) == 0)
        def _():
            carry[...] = jnp.zeros_like(carry)

        for d, x_ref in enumerate((xf_ref, xb_ref)):
            a, u = _rg_gates(x_ref[...], w_ref[d], b_ref[d], lam_ref[d])
            a_sc[d], b_sc[d] = _local_scan(a, u, d == 0)
        _group_scan(((0, hf_ref, True), (1, hb_ref, False)), a_sc, b_sc, carry, tm // SUBLANES)

    full = lambda a: pl.BlockSpec(a.shape, lambda i: (0,) * len(a.shape))
    res = pl.pallas_call(
        body, name=name, grid=(nT,),
        in_specs=[pl.BlockSpec((tm, RG_W), lambda i: (i, 0)), pl.BlockSpec((tm, RG_W), lambda i: (nT - 1 - i, 0)),
                  full(wbd), full(bias), full(lam)],
        out_specs=[pl.BlockSpec((tm, RG_W), lambda i: (i, 0)), pl.BlockSpec((tm, RG_W), lambda i: (nT - 1 - i, 0))],
        out_shape=[jax.ShapeDtypeStruct((T, RG_W), F32)] * 2,
        scratch_shapes=[pltpu.VMEM((2, tm, RG_W), F32), pltpu.VMEM((2, tm, RG_W), F32),
                        pltpu.VMEM((2, SUBLANES, RG_W), F32)],
        compiler_params=_params("arbitrary"))(xc, xc, wbd, bias, lam)
    return res[0], res[1]


def _rg_scan_bwd(xc, wbd, bias, lam, hs, dho, name):
    T = xc.shape[0]
    tm = min(256, T)
    nT = T // tm
    tiles = (lambda i: nT - 1 - i, lambda i: i)

    def body(xf_ref, xb_ref, w_ref, b_ref, lam_ref, hfc, hfp, hfn, hbc, hbp, hbn, dof_ref, dob_ref,
             dxf_ref, dxb_ref, dw_ref, db_ref, dlam_ref, a_sc, b_sc, y_sc, carry):
        i = pl.program_id(0)

        @pl.when(i == 0)
        def _():
            carry[...] = jnp.zeros_like(carry)
            dw_ref[...] = jnp.zeros_like(dw_ref)
            db_ref[...] = jnp.zeros_like(db_ref)
            dlam_ref[...] = jnp.zeros_like(dlam_ref)

        vjps, entering = [], []
        for d, (x_ref, do_ref) in enumerate(((xf_ref, dof_ref), (xb_ref, dob_ref))):
            (a, _), vjp = jax.vjp(_rg_gates, x_ref[...], w_ref[d].astype(F32), b_ref[d], lam_ref[d])
            vjps.append(vjp)
            entering.append(carry[d, 0:1, :])
            a_sc[d], b_sc[d] = _local_scan(a, a * do_ref[...], d == 1)
        _group_scan(((0, y_sc.at[0], False), (1, y_sc.at[1], True)), a_sc, b_sc, carry, tm // SUBLANES)

        row = lax.broadcasted_iota(jnp.int32, (tm, RG_W), 0)
        for d, (do_ref, dx_ref, hc, hp, hn, ti) in enumerate(
                ((dof_ref, dxf_ref, hfc, hfp, hfn, nT - 1 - i), (dob_ref, dxb_ref, hbc, hbp, hbn, i))):
            y = y_sc[d]
            if d == 0:
                y_next = jnp.where(row == tm - 1, entering[d], pltpu.roll(y, tm - 1, 0))
            else:
                y_next = jnp.where(row == 0, entering[d], pltpu.roll(y, 1, 0))
            dtot = do_ref[...] + y_next
            ext = _ext(hc[...], hp[...], hn[...], ti > 0, ti < nT - 1)
            hprev = _shifted(ext, -1 if d == 0 else 1, tm)
            dxc, dw, db, dlam = vjps[d]((dtot * hprev, dtot))
            dx_ref[...] = dxc
            dw_ref[d] += dw
            db_ref[d] += db
            dlam_ref[d] += dlam

    full = lambda a: pl.BlockSpec(a.shape, lambda i: (0,) * len(a.shape))
    tok = lambda d: pl.BlockSpec((tm, RG_W), lambda i: (tiles[d](i), 0))
    acc_shapes = [jax.ShapeDtypeStruct((2, RG_W, 2 * RG_W), F32), jax.ShapeDtypeStruct((2, 1, 2 * RG_W), F32),
                  jax.ShapeDtypeStruct((2, 1, RG_W), F32)]
    res = pl.pallas_call(
        body, name=name, grid=(nT,),
        in_specs=([tok(0), tok(1), full(wbd), full(bias), full(lam)]
                  + _halo_specs(tm, T, RG_W, 0, tiles[0]) + _halo_specs(tm, T, RG_W, 0, tiles[1]) + [tok(0), tok(1)]),
        out_specs=[tok(0), tok(1)] + [full(s) for s in acc_shapes],
        out_shape=[jax.ShapeDtypeStruct((T, RG_W), F32)] * 2 + acc_shapes,
        scratch_shapes=[pltpu.VMEM((2, tm, RG_W), F32), pltpu.VMEM((2, tm, RG_W), F32),
                        pltpu.VMEM((2, tm, RG_W), F32), pltpu.VMEM((2, SUBLANES, RG_W), F32)],
        compiler_params=_params("arbitrary"))(xc, xc, wbd, bias, lam, hs[0], hs[0], hs[0], hs[1], hs[1], hs[1],
                                              dho, dho)
    return (res[0], res[1]), res[2], res[3], res[4]


def _chunk_rows(n_chunks, reverse):
    up, down = (lambda c: c), (lambda c: n_chunks - 1 - c)
    return (down, up) if reverse else (up, down)


def _hg_fwd(proj, l0, l1, name):
    T = proj.shape[0]
    nC = T // CHUNK
    H, dk, dv = 4, 128, 128
    rows = _chunk_rows(nC, False)

    def body(qf, ff, vf, qb, fb, vb, l0_ref, l1_ref, of, ob, spf, spb, st):
        @pl.when(pl.program_id(0) == 0)
        def _():
            st[...] = jnp.zeros_like(st)

        for d, (q, f, v, o, sp) in enumerate(((qf, ff, vf, of, spf), (qb, fb, vb, ob, spb))):
            tri, tri_t, mref = _tri_consts(d)
            stp = tuple(st[d, h] for h in range(H))
            sp[...] = st[d]
            o_val, stn = _hg_chunk(q[...], f[...], v[...], l0_ref[...], l1_ref[...], stp, tri, tri_t, mref)
            o[...] = o_val
            for h in range(H):
                st[d, h] = stn[h]

    tok = lambda d, col: pl.BlockSpec((CHUNK, HG_W), lambda c: (rows[d](c), col))
    par = pl.BlockSpec((1, HG_W), lambda c: (0, 0))
    state = lambda d: pl.BlockSpec((None, H, dv, dk), lambda c: (rows[d](c), 0, 0, 0))
    res = pl.pallas_call(
        body, name=name, grid=(nC,),
        in_specs=[tok(0, 2), tok(0, 3), tok(0, 5), tok(1, 2), tok(1, 4), tok(1, 5), par, par],
        out_specs=[tok(0, 0), tok(1, 0), state(0), state(1)],
        out_shape=[jax.ShapeDtypeStruct((T, H * dv), F32)] * 2 + [jax.ShapeDtypeStruct((nC, H, dv, dk), F32)] * 2,
        scratch_shapes=[pltpu.VMEM((2, H, dv, dk), F32)],
        compiler_params=_params("arbitrary"))(proj, proj, proj, proj, proj, proj, l0, l1)
    return (res[0], res[1]), (res[2], res[3])


def _hg_bwd(proj, l0, l1, sprev, do, name):
    T = proj.shape[0]
    nC = T // CHUNK
    H, dk, dv = 4, 128, 128
    rows = _chunk_rows(nC, True)

    def body(qf, ff, vf, qb, fb, vb, l0_ref, l1_ref, spf, spb, dof, dob,
             dqf, dff, dvf, dqb, dfb, dvb, dl0_ref, dl1_ref, dst):
        @pl.when(pl.program_id(0) == 0)
        def _():
            dst[...] = jnp.zeros_like(dst)
            dl0_ref[...] = jnp.zeros_like(dl0_ref)
            dl1_ref[...] = jnp.zeros_like(dl1_ref)

        for d, (q, f, v, sp, do_ref, dq_ref, df_ref, dv_ref) in enumerate(
                ((qf, ff, vf, spf, dof, dqf, dff, dvf), (qb, fb, vb, spb, dob, dqb, dfb, dvb))):
            tri, tri_t, mref = _tri_consts(d)
            fn = lambda q_, f_, v_, a0, a1, stp: _hg_chunk(q_, f_, v_, a0, a1, stp, tri, tri_t, mref)
            stp = tuple(sp[h] for h in range(H))
            _, vjp = jax.vjp(fn, q[...], f[...], v[...], l0_ref[...], l1_ref[...], stp)
            dq, df, dvv, dl0, dl1, dstp = vjp((do_ref[...], tuple(dst[d, h] for h in range(H))))
            dq_ref[...] = dq
            df_ref[...] = df
            dv_ref[...] = dvv
            dl0_ref[d] += dl0
            dl1_ref[d] += dl1
            for h in range(H):
                dst[d, h] = dstp[h]

    tok = lambda d, col: pl.BlockSpec((CHUNK, HG_W), lambda c: (rows[d](c), col))
    par = pl.BlockSpec((1, HG_W), lambda c: (0, 0))
    acc = pl.BlockSpec((2, 1, HG_W), lambda c: (0, 0, 0))
    state = lambda d: pl.BlockSpec((None, H, dv, dk), lambda c: (rows[d](c), 0, 0, 0))
    res = pl.pallas_call(
        body, name=name, grid=(nC,),
        in_specs=[tok(0, 2), tok(0, 3), tok(0, 5), tok(1, 2), tok(1, 4), tok(1, 5), par, par,
                  state(0), state(1), tok(0, 0), tok(1, 0)],
        out_specs=[tok(0, 0)] * 3 + [tok(1, 0)] * 3 + [acc, acc],
        out_shape=[jax.ShapeDtypeStruct((T, HG_W), F32)] * 6 + [jax.ShapeDtypeStruct((2, 1, HG_W), F32)] * 2,
        scratch_shapes=[pltpu.VMEM((2, H, dv, dk), F32)],
        compiler_params=_params("arbitrary"))(proj, proj, proj, proj, proj, proj, l0, l1, sprev[0], sprev[1], do, do)
    return (res[0], res[3]), (res[1], res[4]), (res[2], res[5]), res[6], res[7]


def _gate_logits(proj, wup, bg, name):
    T = proj.shape[0]
    tm = min(512, T)

    def body(lr_ref, w_ref, b_ref, z_ref, lrb_ref):
        lr = lr_ref[...].astype(BF16)
        lrb_ref[...] = lr
        for d in range(2):
            z_ref[d] = _dg(lr, w_ref[d], 1, 0) + b_ref[d]

    return pl.pallas_call(
        body, name=name, grid=(T // tm,),
        in_specs=[pl.BlockSpec((tm, LANES), lambda i: (i, 24)), pl.BlockSpec((2, LANES, 512), lambda i: (0, 0, 0)),
                  pl.BlockSpec((2, 1, 512), lambda i: (0, 0, 0))],
        out_specs=[pl.BlockSpec((2, tm, 512), lambda i: (0, i, 0)), pl.BlockSpec((tm, LANES), lambda i: (i, 0))],
        out_shape=[jax.ShapeDtypeStruct((2, T, 512), F32), jax.ShapeDtypeStruct((T, LANES), BF16)],
        compiler_params=_params("parallel"))(proj, wup, bg)


def _gate_logits_bwd(dz, wup, name):
    T = dz[0].shape[0]
    tm = min(512, T)

    def body(dzf_ref, dzb_ref, w_ref, dlr_ref, db_ref, dzb16_ref):
        @pl.when(pl.program_id(0) == 0)
        def _():
            db_ref[...] = jnp.zeros_like(db_ref)

        acc = jnp.zeros((tm, LANES), F32)
        for d, dz_ref in enumerate((dzf_ref, dzb_ref)):
            g = dz_ref[...]
            gb = g.astype(BF16)
            dzb16_ref[d] = gb
            acc = acc + _dg(gb, w_ref[d], 1, 1)
            db_ref[d] += jnp.sum(g, axis=0, keepdims=True)
        dlr_ref[...] = acc

    tok = pl.BlockSpec((tm, 512), lambda i: (i, 0))
    return pl.pallas_call(
        body, name=name, grid=(T // tm,),
        in_specs=[tok, tok, pl.BlockSpec((2, LANES, 512), lambda i: (0, 0, 0))],
        out_specs=[pl.BlockSpec((tm, LANES), lambda i: (i, 0)), pl.BlockSpec((2, 1, 512), lambda i: (0, 0, 0)),
                   pl.BlockSpec((2, tm, 512), lambda i: (0, i, 0))],
        out_shape=[jax.ShapeDtypeStruct((T, LANES), F32), jax.ShapeDtypeStruct((2, 1, 512), F32),
                   jax.ShapeDtypeStruct((2, T, 512), BF16)],
        compiler_params=_params("arbitrary"))(dz[0], dz[1], wup)


def _gla_fwd(proj, z, name):
    T = proj.shape[0]
    nC = T // CHUNK
    H, dk, dv = 4, 128, 256
    rows = _chunk_rows(nC, False)

    def body(qf, kf, vf, zf, qb, kb, vb, zb, of, ob, spf, spb, st):
        @pl.when(pl.program_id(0) == 0)
        def _():
            st[...] = jnp.zeros_like(st)

        for d, (q, k, v, z_ref, o, sp) in enumerate(((qf, kf, vf, zf, of, spf), (qb, kb, vb, zb, ob, spb))):
            tri, tri_t, mref = _tri_consts(d)
            stp = tuple(st[d, h] for h in range(H))
            sp[...] = st[d]
            o_val, stn = _gla_chunk(q[...], k[...], v[...], z_ref[...], stp, tri, tri_t, mref)
            o[...] = o_val
            for h in range(H):
                st[d, h] = stn[h]

    tok = lambda d, w, col: pl.BlockSpec((CHUNK, w), lambda c: (rows[d](c), col))
    gate = lambda d: pl.BlockSpec((None, CHUNK, 512), lambda c: (d, rows[d](c), 0))
    state = lambda d: pl.BlockSpec((None, H, dv, dk), lambda c: (rows[d](c), 0, 0, 0))
    res = pl.pallas_call(
        body, name=name, grid=(nC,),
        in_specs=[tok(0, 512, 0), tok(0, 512, 1), tok(0, 1024, 1), gate(0),
                  tok(1, 512, 0), tok(1, 512, 1), tok(1, 1024, 1), gate(1)],
        out_specs=[tok(0, H * dv, 0), tok(1, H * dv, 0), state(0), state(1)],
        out_shape=[jax.ShapeDtypeStruct((T, H * dv), F32)] * 2 + [jax.ShapeDtypeStruct((nC, H, dv, dk), F32)] * 2,
        scratch_shapes=[pltpu.VMEM((2, H, dv, dk), F32)],
        compiler_params=_params("arbitrary"))(proj, proj, proj, z, proj, proj, proj, z)
    return (res[0], res[1]), (res[2], res[3])


def _gla_bwd(proj, z, sprev, do, name):
    T = proj.shape[0]
    nC = T // CHUNK
    H, dk, dv = 4, 128, 256
    rows = _chunk_rows(nC, True)

    def body(qf, kf, vf, zf, qb, kb, vb, zb, spf, spb, dof, dob,
             dqf, dkf, dvf, dzf, dqb, dkb, dvb, dzb, dst):
        @pl.when(pl.program_id(0) == 0)
        def _():
            dst[...] = jnp.zeros_like(dst)

        for d, (q, k, v, z_ref, sp, do_ref, dq_ref, dk_ref, dv_ref, dz_ref) in enumerate(
                ((qf, kf, vf, zf, spf, dof, dqf, dkf, dvf, dzf), (qb, kb, vb, zb, spb, dob, dqb, dkb, dvb, dzb))):
            tri, tri_t, mref = _tri_consts(d)
            fn = lambda q_, k_, v_, z_, stp: _gla_chunk(q_, k_, v_, z_, stp, tri, tri_t, mref)
            stp = tuple(sp[h] for h in range(H))
            _, vjp = jax.vjp(fn, q[...], k[...], v[...], z_ref[...], stp)
            dq, dkk, dvv, dzz, dstp = vjp((do_ref[...], tuple(dst[d, h] for h in range(H))))
            dq_ref[...] = dq
            dk_ref[...] = dkk
            dv_ref[...] = dvv
            dz_ref[...] = dzz
            for h in range(H):
                dst[d, h] = dstp[h]

    tok = lambda d, w, col: pl.BlockSpec((CHUNK, w), lambda c: (rows[d](c), col))
    gate = lambda d: pl.BlockSpec((None, CHUNK, 512), lambda c: (d, rows[d](c), 0))
    state = lambda d: pl.BlockSpec((None, H, dv, dk), lambda c: (rows[d](c), 0, 0, 0))
    outs = lambda d: [tok(d, 512, 0), tok(d, 512, 0), tok(d, 1024, 0), tok(d, 512, 0)]
    shapes = [jax.ShapeDtypeStruct((T, 512), F32), jax.ShapeDtypeStruct((T, 512), F32),
              jax.ShapeDtypeStruct((T, 1024), F32), jax.ShapeDtypeStruct((T, 512), F32)]
    res = pl.pallas_call(
        body, name=name, grid=(nC,),
        in_specs=[tok(0, 512, 0), tok(0, 512, 1), tok(0, 1024, 1), gate(0),
                  tok(1, 512, 0), tok(1, 512, 1), tok(1, 1024, 1), gate(1),
                  state(0), state(1), tok(0, H * dv, 0), tok(1, H * dv, 0)],
        out_specs=outs(0) + outs(1), out_shape=shapes + shapes,
        scratch_shapes=[pltpu.VMEM((2, H, dv, dk), F32)],
        compiler_params=_params("arbitrary"))(proj, proj, proj, z, proj, proj, proj, z, sprev[0], sprev[1], do, do)
    return (res[0], res[4]), (res[1], res[5]), (res[2], res[6]), (res[3], res[7])


def _l0_combine_fwd(hs, proj, o, gain, name):
    T = proj.shape[0]
    tm = min(512, T)

    def body(hf, hb, ga, of, ob, g, gn, out):
        out[...] = _l0_combine(hf[...], hb[...], ga[...], of[...], ob[...], g[...], gn[...]).astype(BF16)

    tok = pl.BlockSpec((tm, 512), lambda i: (i, 0))
    return pl.pallas_call(
        body, name=name, grid=(T // tm,),
        in_specs=[tok, tok, pl.BlockSpec((tm, 512), lambda i: (i, 1)), tok, tok,
                  pl.BlockSpec((tm, 512), lambda i: (i, 6)), pl.BlockSpec((1, 512), lambda i: (0, 0))],
        out_specs=pl.BlockSpec((tm, 1024), lambda i: (i, 0)),
        out_shape=jax.ShapeDtypeStruct((T, 1024), BF16),
        compiler_params=_params("parallel"))(hs[0], hs[1], proj, o[0], o[1], proj, gain)


def _l0_combine_bwd(hs, proj, o, gain, dmix, name):
    T = proj.shape[0]
    tm = min(512, T)

    def body(hf, hb, ga, of, ob, g, gn, dm, dho_ref, dga_ref, do_ref, dg_ref, dgn_ref):
        @pl.when(pl.program_id(0) == 0)
        def _():
            dgn_ref[...] = jnp.zeros_like(dgn_ref)

        _, vjp = jax.vjp(_l0_combine, hf[...], hb[...], ga[...], of[...], ob[...], g[...], gn[...])
        dhf, _, dga, dof, _, dg, dgn = vjp(dm[...])
        dho_ref[...] = dhf
        dga_ref[...] = dga
        do_ref[...] = dof
        dg_ref[...] = dg
        dgn_ref[...] += dgn

    tok = lambda: pl.BlockSpec((tm, 512), lambda i: (i, 0))
    return pl.pallas_call(
        body, name=name, grid=(T // tm,),
        in_specs=[tok(), tok(), pl.BlockSpec((tm, 512), lambda i: (i, 1)), tok(), tok(),
                  pl.BlockSpec((tm, 512), lambda i: (i, 6)), pl.BlockSpec((1, 512), lambda i: (0, 0)),
                  pl.BlockSpec((tm, 1024), lambda i: (i, 0))],
        out_specs=[tok(), tok(), tok(), tok(), pl.BlockSpec((1, 512), lambda i: (0, 0))],
        out_shape=[jax.ShapeDtypeStruct((T, 512), F32)] * 4 + [jax.ShapeDtypeStruct((1, 512), F32)],
        compiler_params=_params("arbitrary"))(hs[0], hs[1], proj, o[0], o[1], proj, gain, dmix)


def _l0_assemble(dxa, dga, dq, df, dv, dg, name):
    T = dxa.shape[0]
    tm = min(512, T)

    def body(xa, ga, q0, q1, f0, f1, v0, v1, g, out):
        out[...] = jnp.concatenate([xa[...], ga[...], q0[...] + q1[...], f0[...], f1[...], v0[...] + v1[...],
                                    g[...]], axis=1).astype(BF16)

    tok = lambda: pl.BlockSpec((tm, 512), lambda i: (i, 0))
    return pl.pallas_call(
        body, name=name, grid=(T // tm,),
        in_specs=[tok() for _ in range(9)],
        out_specs=pl.BlockSpec((tm, AB_IN), lambda i: (i, 0)),
        out_shape=jax.ShapeDtypeStruct((T, AB_IN), BF16),
        compiler_params=_params("parallel"))(dxa, dga, dq[0], dq[1], df[0], df[1], dv[0], dv[1], dg)


def _l1_combine_fwd(o, proj, gain, name):
    T = proj.shape[0]
    tm = min(512, T)

    def body(of, ob, r, gn, out):
        out[...] = _l1_combine(of[...], ob[...], r[...], gn[...]).astype(BF16)

    tok = pl.BlockSpec((tm, 1024), lambda i: (i, 0))
    return pl.pallas_call(
        body, name=name, grid=(T // tm,),
        in_specs=[tok, tok, pl.BlockSpec((tm, 1024), lambda i: (i, 2)), pl.BlockSpec((1, 1024), lambda i: (0, 0))],
        out_specs=pl.BlockSpec((tm, 1024), lambda i: (i, 0)),
        out_shape=jax.ShapeDtypeStruct((T, 1024), BF16),
        compiler_params=_params("parallel"))(o[0], o[1], proj, gain)


def _l1_combine_bwd(o, proj, gain, dmix, name):
    T = proj.shape[0]
    tm = min(512, T)

    def body(of, ob, r, gn, dm, do_ref, dr_ref, dgn_ref):
        @pl.when(pl.program_id(0) == 0)
        def _():
            dgn_ref[...] = jnp.zeros_like(dgn_ref)

        _, vjp = jax.vjp(_l1_combine, of[...], ob[...], r[...], gn[...])
        dof, _, dr, dgn = vjp(dm[...])
        do_ref[...] = dof
        dr_ref[...] = dr
        dgn_ref[...] += dgn

    tok = lambda: pl.BlockSpec((tm, 1024), lambda i: (i, 0))
    return pl.pallas_call(
        body, name=name, grid=(T // tm,),
        in_specs=[tok(), tok(), pl.BlockSpec((tm, 1024), lambda i: (i, 2)),
                  pl.BlockSpec((1, 1024), lambda i: (0, 0)), tok()],
        out_specs=[tok(), tok(), pl.BlockSpec((1, 1024), lambda i: (0, 0))],
        out_shape=[jax.ShapeDtypeStruct((T, 1024), F32)] * 2 + [jax.ShapeDtypeStruct((1, 1024), F32)],
        compiler_params=_params("arbitrary"))(o[0], o[1], proj, gain, dmix)


def _l1_assemble(dq, dk, dv, dr, dlr, name):
    T = dr.shape[0]
    tm = min(512, T)

    def body(q0, q1, k0, k1, v0, v1, r, a, out):
        out[...] = jnp.concatenate([q0[...] + q1[...], k0[...] + k1[...], v0[...] + v1[...], r[...], a[...]],
                                   axis=1).astype(BF16)

    tok = lambda w: pl.BlockSpec((tm, w), lambda i: (i, 0))
    return pl.pallas_call(
        body, name=name, grid=(T // tm,),
        in_specs=[tok(512), tok(512), tok(512), tok(512), tok(1024), tok(1024), tok(1024), tok(LANES)],
        out_specs=pl.BlockSpec((tm, GLA_IN_PAD), lambda i: (i, 0)),
        out_shape=jax.ShapeDtypeStruct((T, GLA_IN_PAD), BF16),
        compiler_params=_params("parallel"))(dq[0], dq[1], dk[0], dk[1], dv[0], dv[1], dr, dlr)


HBM_SPEC = pl.BlockSpec(memory_space=pltpu.HBM)


def _place():
    x, y, c = lax.axis_index("x"), lax.axis_index("y"), lax.axis_index("c")
    return x, y, c


def _allgather_vmem(x_shard, name, reduce=False):
    m_per, n = x_shard.shape

    def body(x_ref, out_ref, *rest):
        if reduce:
            sum_ref, send_sems, recv_sems, local_sem = rest
        else:
            send_sems, recv_sems, local_sem = rest
        x, y, c = _place()
        me, sibling = (x, y, c), (x, y, 1 - c)
        chips = [(1 - x, y), (x, 1 - y), (1 - x, 1 - y)]

        def rows(px, py, pc):
            return out_ref.at[pl.ds((4 * px + 2 * py + pc) * m_per, m_per), :]

        def copy(k, block, to, src=None):
            return pltpu.make_async_remote_copy(
                src_ref=rows(*block) if src is None else src, dst_ref=rows(*block),
                send_sem=send_sems.at[k], recv_sem=recv_sems.at[k], device_id=to, device_id_type=MESH)

        mine = pltpu.make_async_copy(x_ref, rows(*me), local_sem)
        mine.start()
        first = [copy(0, me, sibling, src=x_ref)]
        first += [copy(1 + j, me, (*chip, c), src=x_ref) for j, chip in enumerate(chips)]
        for cp in first:
            cp.start()
        passed = [copy(4 + j, (*chip, c), sibling) for j, chip in enumerate(chips)]
        for j, chip in enumerate(chips):
            copy(1 + j, (*chip, c), me).wait_recv()
            passed[j].start()
        copy(0, sibling, me).wait_recv()
        for j, chip in enumerate(chips):
            copy(4 + j, (*chip, 1 - c), me).wait_recv()
        for cp in first + passed:
            cp.wait_send()
        mine.wait()
        if reduce:
            acc = out_ref[pl.ds(0, m_per), :]
            for j in range(1, N_DEV):
                acc = acc + out_ref[pl.ds(j * m_per, m_per), :]
            sum_ref[...] = acc

    vm = pl.BlockSpec(memory_space=pltpu.VMEM)
    out_shape = [jax.ShapeDtypeStruct((N_DEV * m_per, n), x_shard.dtype)]
    if reduce:
        out_shape.append(jax.ShapeDtypeStruct((m_per, n), x_shard.dtype))
    res = pl.pallas_call(
        body, name=name, in_specs=[vm], out_specs=[vm] * len(out_shape), out_shape=out_shape,
        scratch_shapes=[pltpu.SemaphoreType.DMA((7,)), pltpu.SemaphoreType.DMA((7,)), pltpu.SemaphoreType.DMA],
        compiler_params=pltpu.CompilerParams(has_side_effects=True, vmem_limit_bytes=VMEM_LIMIT))(x_shard)
    return res[1] if reduce else res[0]


SEM_SPEC = pl.BlockSpec(memory_space=pltpu.SEMAPHORE)
DATAFLOW_EFFECT = pltpu.SideEffectType.DATAFLOW_SIDE_EFFECTING


def _copies(plan, srcs, lands, send_sems, recv_sems):
    x, y, c = _place()
    return [pltpu.make_async_remote_copy(src_ref=s, dst_ref=d, send_sem=send_sems.at[k], recv_sem=recv_sems.at[k],
                                         device_id=dev, device_id_type=MESH)
            for k, (s, d, dev) in enumerate(plan(srcs, lands, x, y, c))]


def _copies_start(plan, n_copies, srcs, lands, name):
    ns, nl = len(srcs), len(lands)

    def body(*refs):
        send_sems, recv_sems = refs[ns + nl], refs[ns + nl + 1]
        for cp in _copies(plan, refs[:ns], refs[ns:ns + nl], send_sems, recv_sems):
            cp.start()
        refs[-1][...] = jnp.zeros_like(refs[-1])

    arrays = list(srcs) + list(lands)
    res = pl.pallas_call(
        body, name=name,
        in_specs=[HBM_SPEC] * (ns + nl),
        out_specs=tuple([SEM_SPEC, SEM_SPEC] + [HBM_SPEC] * (ns + nl) + [pl.BlockSpec(memory_space=pltpu.VMEM)]),
        out_shape=tuple([pltpu.SemaphoreType.DMA((n_copies,)), pltpu.SemaphoreType.DMA((n_copies,))]
                        + [pltpu.HBM(a.shape, a.dtype) for a in arrays]
                        + [jax.ShapeDtypeStruct((SUBLANES, LANES), F32)]),
        input_output_aliases={i: 2 + i for i in range(ns + nl)},
        compiler_params=pltpu.CompilerParams(has_side_effects=DATAFLOW_EFFECT),
    )(*[pltpu.with_memory_space_constraint(a, pltpu.HBM) for a in arrays])
    return res[0], res[1], list(res[2:2 + ns]), list(res[2 + ns:2 + ns + nl]), res[-1]


def _copies_wait(plan, started, after, name):
    send_sems, recv_sems, srcs, lands, _ = started
    ns, nl = len(srcs), len(lands)

    def body(*refs):
        for cp in _copies(plan, refs[:ns], refs[ns:ns + nl], refs[ns + nl], refs[ns + nl + 1]):
            cp.wait_send()
            cp.wait_recv()

    arrays = list(srcs) + list(lands)
    res = pl.pallas_call(
        body, name=name,
        in_specs=[HBM_SPEC] * (ns + nl) + [SEM_SPEC, SEM_SPEC, pl.BlockSpec(memory_space=pl.ANY)],
        out_specs=tuple([HBM_SPEC] * (ns + nl)),
        out_shape=tuple(pltpu.HBM(a.shape, a.dtype) for a in arrays),
        input_output_aliases={i: i for i in range(ns + nl)},
        compiler_params=pltpu.CompilerParams(has_side_effects=DATAFLOW_EFFECT),
    )(*arrays, send_sems, recv_sems, after)
    return list(res[:ns]), list(res[ns:])


def _after(token, value):
    return value + token[0:1, 0:1].astype(value.dtype)


def _chips(x, y):
    return [(1 - x, y), (x, 1 - y), (1 - x, 1 - y)]


def _plan_gather_first(srcs, lands, x, y, c):
    me = 4 * x + 2 * y + c
    out = []
    for s, l in zip(srcs, lands):
        out.append((s, l.at[me], (x, y, 1 - c)))
        out += [(s, l.at[me], (*chip, c)) for chip in _chips(x, y)]
    return out


def _plan_gather_pass(srcs, lands, x, y, c):
    out = []
    for l in lands:
        for chip in _chips(x, y):
            slot = l.at[4 * chip[0] + 2 * chip[1] + c]
            out.append((slot, slot, (x, y, 1 - c)))
    return out


def _plan_grads_sibling(srcs, lands, x, y, c):
    return [(s.at[2 * q + (1 - c)], l.at[q], (x, y, 1 - c)) for s, l in zip(srcs, lands) for q in range(4)]


def _plan_grads_chips(srcs, lands, x, y, c):
    return [(s.at[2 * chip[0] + chip[1]], l.at[k], (*chip, c))
            for s, l in zip(srcs, lands) for k, chip in enumerate(_chips(x, y))]


def _landing(n_slots, like):
    return [lax.empty((n_slots,) + a.shape[1:], a.dtype) for a in like]


def _chip_partial(g, r1, place, name):
    _, R, C = g.shape
    tr = min(256, R)
    assert R % tr == 0

    def body(pl_ref, g_ref, r_ref, pb_ref, pm_ref):
        q = pl.program_id(1)
        s = g_ref[...] + r_ref[...]
        pb_ref[...] = s.astype(BF16)

        @pl.when(q == pl_ref[1])
        def _():
            pm_ref[...] = s

    grid_spec = pltpu.PrefetchScalarGridSpec(
        num_scalar_prefetch=1, grid=(R // tr, 4),
        in_specs=[pl.BlockSpec((None, tr, C), lambda r, q, p: (2 * q + p[0], r, 0)),
                  pl.BlockSpec((None, tr, C), lambda r, q, p: (q, r, 0))],
        out_specs=[pl.BlockSpec((None, tr, C), lambda r, q, p: (q, r, 0)),
                   pl.BlockSpec((tr, C), lambda r, q, p: (r, 0))])
    return pl.pallas_call(
        body, name=name, grid_spec=grid_spec,
        out_shape=[jax.ShapeDtypeStruct((4, R, C), BF16), jax.ShapeDtypeStruct((R, C), F32)],
        compiler_params=_params("parallel", "arbitrary"))(place, g, r1)


def _adamw(w, gparts, m, v, name):
    R, C = w.shape
    tr = min(256, R)
    assert R % tr == 0
    g0, g3 = gparts

    def body(w_ref, g0_ref, *rest):
        if g3 is not None:
            g3_ref, m_ref, v_ref, go, do, mo, vo = rest
        else:
            m_ref, v_ref, go, do, mo, vo = rest
        g = g0_ref[...]
        if g3 is not None:
            for k in range(3):
                g = g + g3_ref[k].astype(F32)
        wv = w_ref[...]
        mn = ADAM_B1 * m_ref[...] + (1.0 - ADAM_B1) * g
        vn = ADAM_B2 * v_ref[...] + (1.0 - ADAM_B2) * jnp.square(g)
        m_hat = mn / (1.0 - ADAM_B1 ** ADAM_STEP)
        v_hat = vn / (1.0 - ADAM_B2 ** ADAM_STEP)
        go[...] = g
        do[...] = -ADAM_LR * (m_hat / (jnp.sqrt(v_hat) + ADAM_EPS) + ADAM_WD * wv)
        mo[...] = mn
        vo[...] = vn

    blk = pl.BlockSpec((tr, C), lambda i: (i, 0))
    in_specs = [blk, blk] + ([pl.BlockSpec((3, tr, C), lambda i: (0, i, 0))] if g3 is not None else []) + [blk, blk]
    args = [w, g0] + ([g3] if g3 is not None else []) + [m, v]
    return pl.pallas_call(
        body, name=name, grid=(R // tr,), in_specs=in_specs, out_specs=[blk] * 4,
        out_shape=[jax.ShapeDtypeStruct((R, C), F32)] * 4,
        compiler_params=_params("parallel"))(*args)


SMALL_SHARDED = ("rg_conv_w", "rg_b_a", "rg_b_x", "rg_lambda", "gla_w_gate_up", "gla_b_gate", "gla_norm")
SMALL_REPLICATED = ("norm_mix", "norm_mlp", "norm_final", "rg_conv_b", "rg_w_a", "rg_w_x", "hg_lb_logits", "hg_norm")
WEIGHT_NAMES = ("norm_mix", "norm_mlp", "norm_final", "mlp_w1", "mlp_w2", "ab_w_in", "ab_w_out", "rg_conv_w",
                "rg_conv_b", "rg_w_a", "rg_b_a", "rg_w_x", "rg_b_x", "rg_lambda", "hg_lb_logits", "hg_norm",
                "gla_w_in", "gla_w_out", "gla_w_gate_up", "gla_b_gate", "gla_norm")


def _rows128(a):
    return a.reshape(-1, LANES)


def _part_rows(a):
    return -(-(a.size // LANES) // SUBLANES) * SUBLANES


def _pack_rows(arrays, pad_to=SUBLANES):
    parts = [jnp.pad(_rows128(a), ((0, _part_rows(a) - a.size // LANES), (0, 0))) for a in arrays]
    total = sum(p.shape[0] for p in parts)
    extra = (-total) % pad_to
    if extra:
        parts.append(jnp.zeros((extra, LANES), parts[0].dtype))
    return jnp.concatenate(parts, axis=0)


def _unshard_last(g, shape_local):
    nd = len(shape_local)
    t = g.reshape((N_DEV,) + tuple(shape_local))
    t = jnp.moveaxis(t, 0, nd - 1)
    return t.reshape(tuple(shape_local[:-1]) + (N_DEV * shape_local[-1],))


def _block_diag(w):
    eye = jnp.eye(8, dtype=w.dtype)
    return (w[:, :, :, None, :] * eye[None, :, None, :, None]).reshape(2, RG_W, RG_W)


def _block_diag_extract(dw):
    t = dw.reshape(2, 8, 64, 8, 64)
    return jnp.moveaxis(jnp.diagonal(t, axis1=1, axis2=3), -1, 1)


def kernel(x, norm_mix, norm_mlp, norm_final, mlp_w1, mlp_w2, ab_w_in, ab_w_out, rg_conv_w, rg_conv_b, rg_w_a, rg_b_a, rg_w_x, rg_b_x, rg_lambda, hg_lb_logits, hg_norm, gla_w_in, gla_w_out, gla_w_gate_up, gla_b_gate, gla_norm, loss_target, m_norm_mix, m_norm_mlp, m_norm_final, m_mlp_w1, m_mlp_w2, m_ab_w_in, m_ab_w_out, m_rg_conv_w, m_rg_conv_b, m_rg_w_a, m_rg_b_a, m_rg_w_x, m_rg_b_x, m_rg_lambda, m_hg_lb_logits, m_hg_norm, m_gla_w_in, m_gla_w_out, m_gla_w_gate_up, m_gla_b_gate, m_gla_norm, v_norm_mix, v_norm_mlp, v_norm_final, v_mlp_w1, v_mlp_w2, v_ab_w_in, v_ab_w_out, v_rg_conv_w, v_rg_conv_b, v_rg_w_a, v_rg_b_a, v_rg_w_x, v_rg_b_x, v_rg_lambda, v_hg_lb_logits, v_hg_norm, v_gla_w_in, v_gla_w_out, v_gla_w_gate_up, v_gla_b_gate, v_gla_norm):
    w_loc = dict(norm_mix=norm_mix, norm_mlp=norm_mlp, norm_final=norm_final, mlp_w1=mlp_w1, mlp_w2=mlp_w2,
                 ab_w_in=ab_w_in, ab_w_out=ab_w_out, rg_conv_w=rg_conv_w, rg_conv_b=rg_conv_b, rg_w_a=rg_w_a,
                 rg_b_a=rg_b_a, rg_w_x=rg_w_x, rg_b_x=rg_b_x, rg_lambda=rg_lambda, hg_lb_logits=hg_lb_logits,
                 hg_norm=hg_norm, gla_w_in=gla_w_in, gla_w_out=gla_w_out, gla_w_gate_up=gla_w_gate_up,
                 gla_b_gate=gla_b_gate, gla_norm=gla_norm)
    m_loc = dict(norm_mix=m_norm_mix, norm_mlp=m_norm_mlp, norm_final=m_norm_final, mlp_w1=m_mlp_w1,
                 mlp_w2=m_mlp_w2, ab_w_in=m_ab_w_in, ab_w_out=m_ab_w_out, rg_conv_w=m_rg_conv_w,
                 rg_conv_b=m_rg_conv_b, rg_w_a=m_rg_w_a, rg_b_a=m_rg_b_a, rg_w_x=m_rg_w_x, rg_b_x=m_rg_b_x,
                 rg_lambda=m_rg_lambda, hg_lb_logits=m_hg_lb_logits, hg_norm=m_hg_norm, gla_w_in=m_gla_w_in,
                 gla_w_out=m_gla_w_out, gla_w_gate_up=m_gla_w_gate_up, gla_b_gate=m_gla_b_gate,
                 gla_norm=m_gla_norm)
    v_loc = dict(norm_mix=v_norm_mix, norm_mlp=v_norm_mlp, norm_final=v_norm_final, mlp_w1=v_mlp_w1,
                 mlp_w2=v_mlp_w2, ab_w_in=v_ab_w_in, ab_w_out=v_ab_w_out, rg_conv_w=v_rg_conv_w,
                 rg_conv_b=v_rg_conv_b, rg_w_a=v_rg_w_a, rg_b_a=v_rg_b_a, rg_w_x=v_rg_w_x, rg_b_x=v_rg_b_x,
                 rg_lambda=v_rg_lambda, hg_lb_logits=v_hg_lb_logits, hg_norm=v_hg_norm, gla_w_in=v_gla_w_in,
                 gla_w_out=v_gla_w_out, gla_w_gate_up=v_gla_w_gate_up, gla_b_gate=v_gla_b_gate,
                 gla_norm=v_gla_norm)

    T = x.shape[1]
    h0 = x.reshape(T, D_MODEL)
    target = loss_target.reshape(T, D_MODEL)
    ax, ay, ac = lax.axis_index("x"), lax.axis_index("y"), lax.axis_index("c")
    dev = 4 * ax + 2 * ay + ac
    place = jnp.stack([ac, 2 * ax + ay]).astype(jnp.int32)

    abin_shard = ab_w_in[0].astype(BF16)
    first_started = _copies_start(_plan_gather_first, 4, [abin_shard], _landing(N_DEV, [abin_shard[None]]),
                                  "ag_first_start")
    rest_shards = [mlp_w1[0].astype(BF16), mlp_w2[0].astype(BF16), gla_w_in[0].astype(BF16),
                   gla_w_out[0].astype(BF16), mlp_w1[1].astype(BF16), mlp_w2[1].astype(BF16),
                   _after(first_started[4], ab_w_out[0].astype(BF16))]
    ag_started = _copies_start(_plan_gather_first, 4 * len(rest_shards), rest_shards,
                               _landing(N_DEV, [s[None] for s in rest_shards]), "ag_rest_start")

    small_local = [w_loc[n] for n in SMALL_SHARDED]
    small_g = _allgather_vmem(_pack_rows(small_local, 8), "ag_small")
    small_g = small_g.reshape(N_DEV, -1, LANES)
    full = {}
    off = 0
    for n, a in zip(SMALL_SHARDED, small_local):
        full[n] = _unshard_last(small_g[:, off:off + a.size // LANES].reshape(N_DEV, a.size), a.shape)
        off += _part_rows(a)
    conv_w = full["rg_conv_w"][0]
    b_a, b_x, lam = full["rg_b_a"][0], full["rg_b_x"][0], full["rg_lambda"][0]
    w_up, b_gate, g_norm = full["gla_w_gate_up"][0], full["gla_b_gate"][0], full["gla_norm"]

    cw8 = jnp.pad(conv_w, ((0, 4), (0, 0)))
    wbd = jnp.concatenate([_block_diag(rg_w_a[0]), _block_diag(rg_w_x[0])], axis=2).astype(BF16)
    rg_bias = jnp.concatenate([b_a, b_x], axis=1).reshape(2, 1, 2 * RG_W)
    lam3 = lam.reshape(2, 1, RG_W)
    l0, l1 = hg_lb_logits[0:1], hg_lb_logits[1:2]
    wup_pad = jnp.zeros((2, LANES, 512), F32).at[0, 0:16].set(w_up[0]).at[1, 16:32].set(w_up[1])
    bg3 = b_gate.reshape(2, 1, 512)
    nmix0, nmix1 = norm_mix[0:1], norm_mix[1:2]
    nmlp0, nmlp1 = norm_mlp[0:1], norm_mlp[1:2]
    nfin = norm_final.reshape(1, D_MODEL)

    (abin_shard,), abin_l = _copies_wait(_plan_gather_first, first_started, wbd, "ag_first_wait")
    first_pass = _copies_start(_plan_gather_pass, 3, [], abin_l, "ag_first_pass_start")
    _, (abin_g,) = _copies_wait(_plan_gather_pass, first_pass, first_pass[4], "ag_first_pass_wait")
    abin_g = lax.dynamic_update_index_in_dim(abin_g, abin_shard, dev, 0)
    wab_in = jnp.transpose(abin_g, (1, 0, 2)).reshape(D_MODEL, AB_IN)
    proj0, y0 = _norm_matmul(h0, _after(ag_started[4], nmix0), wab_in, "l0_in_proj")
    xc = _rg_conv_fwd(proj0, cw8, rg_conv_b, "rg_conv")
    hs = _rg_scan_fwd(xc, wbd, rg_bias, lam3, "rg_scan")
    o_hg, s_hg = _hg_fwd(proj0, l0, l1, "hg_chunks")
    rest_shards, rest_lands = _copies_wait(_plan_gather_first, ag_started, hs[0], "ag_rest_wait")
    pass_started = _copies_start(_plan_gather_pass, 3 * len(rest_lands), [], rest_lands, "ag_pass_start")
    mixin0 = _l0_combine_fwd(hs, proj0, o_hg, _after(pass_started[4], hg_norm), "l0_combine")
    _, rest_g = _copies_wait(_plan_gather_pass, pass_started, mixin0, "ag_pass_wait")
    rest_g = [lax.dynamic_update_index_in_dim(g, s, dev, 0) for g, s in zip(rest_g, rest_shards)]
    wab_out = rest_g[6].reshape(D_MODEL, D_MODEL)
    h1 = _matmul_res(mixin0, wab_out, h0, "l0_out_proj")
    w1g = (rest_g[0], rest_g[4])
    w2f = (rest_g[1].reshape(D_FF, D_MODEL), rest_g[5].reshape(D_FF, D_MODEL))
    wgla_in = jnp.pad(jnp.transpose(rest_g[2], (1, 0, 2)).reshape(D_MODEL, GLA_IN),
                      ((0, 0), (0, GLA_IN_PAD - GLA_IN)))
    wgla_out = rest_g[3].reshape(D_MODEL, D_MODEL)
    h2, pre0, ym0 = _mlp_fwd(h1, nmlp0, w1g[0], w2f[0], "mlp0")
    proj1, y1 = _norm_matmul(h2, nmix1, wgla_in, "l1_in_proj")
    z_gate, lr_b = _gate_logits(proj1, wup_pad, bg3, "gla_gate_logits")
    o_gla, s_gla = _gla_fwd(proj1, z_gate, "gla_chunks")
    mixin1 = _l1_combine_fwd(o_gla, proj1, g_norm, "l1_combine")
    h3 = _matmul_res(mixin1, wgla_out, h2, "l1_out_proj")
    h4, pre1, ym1 = _mlp_fwd(h3, nmlp1, w1g[1], w2f[1], "mlp1")
    loss_blk, dh4, dh4b, d_nfin = _final_loss(h4, nfin, target, "final_loss")
    loss = lax.psum(loss_blk[0, 0], ("x", "y", "c"))

    dh3, dh3b, dpre1, act1, d_nmlp1 = _mlp_bwd(dh4, dh4b, h3, nmlp1, pre1, w1g[1], w2f[1], "mlp1_bwd")
    g_w1_1 = _wgrad(ym1, dpre1, 512, "mlp1_dw1", sharded_cols=True)
    g_w2_1 = _wgrad(act1, dh4b, 1024, "mlp1_dw2")
    dmixin1 = _dgrad(dh3b, wgla_out, "l1_out_dgrad")
    g_gla_out = _wgrad(mixin1, dh3b, 1024, "l1_out_dw")
    do_gla, dr, d_gnorm = _l1_combine_bwd(o_gla, proj1, g_norm, dmixin1, "l1_combine_bwd")
    dq1, dk1, dv1, dz_gate = _gla_bwd(proj1, z_gate, s_gla, do_gla, "gla_chunks_bwd")
    dlr1, d_bg, dz_b = _gate_logits_bwd(dz_gate, wup_pad, "gla_gate_logits_bwd")
    d_wup = [_wgrad(lr_b, dz_b[d], 512, "gla_gate_dw%d" % d) for d in range(2)]
    dproj1 = _l1_assemble(dq1, dk1, dv1, dr, dlr1, "l1_assemble")
    dh2, dh2b, d_nmix1 = _dgrad_norm(dproj1, wgla_in, h2, nmix1, dh3, "l1_in_dgrad")
    g_gla_in = _wgrad(y1, dproj1, 640, "l1_in_dw")

    def reduce_start(grads, tag):
        return _copies_start(_plan_grads_sibling, 4 * len(grads), grads, _landing(4, grads), "rs_%s_d2d_start" % tag)

    def reduce_mid(started, after, tag):
        grads, got = _copies_wait(_plan_grads_sibling, started, after, "rs_%s_d2d_wait" % tag)
        parts = [_chip_partial(g, r, place, "rs_%s_partial%d" % (tag, a)) for a, (g, r) in enumerate(zip(grads, got))]
        pb = [p[0] for p in parts]
        return _copies_start(_plan_grads_chips, 3 * len(pb), pb, _landing(3, pb), "rs_%s_ici_start" % tag), \
            [p[1] for p in parts]

    def reduce_end(started, mine, after, tag, ws, ms, vs):
        _, got = _copies_wait(_plan_grads_chips, started, after, "rs_%s_ici_wait" % tag)
        return [_adamw(w, (p, r), m, v, "adamw_%s%d" % (tag, a))
                for a, (w, p, r, m, v) in enumerate(zip(ws, mine, got, ms, vs))]

    slots_l1 = [g_w1_1, g_w2_1.reshape(N_DEV, 512, D_MODEL),
                jnp.transpose(g_gla_in[:, :GLA_IN].reshape(D_MODEL, N_DEV, GLA_IN // N_DEV), (1, 0, 2)),
                g_gla_out.reshape(N_DEV, 128, D_MODEL)]
    ra_d2d = reduce_start(slots_l1, "l1")

    dh1, dh1b, dpre0, act0, d_nmlp0 = _mlp_bwd(dh2, dh2b, h1, _after(ra_d2d[4], nmlp0), pre0, w1g[0], w2f[0],
                                               "mlp0_bwd")
    g_w1_0 = _wgrad(ym0, dpre0, 512, "mlp0_dw1", sharded_cols=True)
    g_w2_0 = _wgrad(act0, dh2b, 1024, "mlp0_dw2")
    ra_ici, ra_mine = reduce_mid(ra_d2d, g_w2_0, "l1")
    rb_d2d = reduce_start([g_w1_0, g_w2_0.reshape(N_DEV, 512, D_MODEL)], "mlp0")
    dmixin0 = _dgrad(dh1b, wab_out, "l0_out_dgrad")
    g_ab_out = _wgrad(mixin0, dh1b, 1024, "l0_out_dw")
    dho, dga, do_hg, dg_gate, d_hgnorm = _l0_combine_bwd(
        hs, proj0, o_hg, _after(rb_d2d[4], _after(ra_ici[4], hg_norm)), dmixin0, "l0_combine_bwd")
    dxc, d_wbd, d_rgb, d_lam = _rg_scan_bwd(xc, wbd, rg_bias, lam3, hs, dho, "rg_scan_bwd")
    dxa, d_cw8, d_cb = _rg_conv_bwd(dxc, proj0, cw8, "rg_conv_bwd")
    dq0, df0, dv0, d_l0, d_l1 = _hg_bwd(proj0, l0, l1, s_hg, do_hg, "hg_chunks_bwd")
    rb_ici, rb_mine = reduce_mid(rb_d2d, d_l0, "mlp0")
    dproj0 = _l0_assemble(dxa, dga, dq0, df0, dv0, dg_gate, "l0_assemble")
    g_ab_in = _wgrad(y0, dproj0, 512, "l0_in_dw")
    rc_d2d = reduce_start([jnp.transpose(g_ab_in.reshape(D_MODEL, N_DEV, AB_IN // N_DEV), (1, 0, 2)),
                           g_ab_out.reshape(N_DEV, 128, D_MODEL)], "ab")
    dx, _, d_nmix0 = _dgrad_norm(dproj0, wab_in, h0, _after(rc_d2d[4], _after(rb_ici[4], nmix0)), dh1,
                                 "l0_in_dgrad")
    rc_ici, rc_mine = reduce_mid(rc_d2d, d_nmix0, "ab")

    res_l1 = reduce_end(ra_ici, ra_mine, rc_ici[4], "l1",
                        [mlp_w1[1], mlp_w2[1], gla_w_in[0], gla_w_out[0]],
                        [m_mlp_w1[1], m_mlp_w2[1], m_gla_w_in[0], m_gla_w_out[0]],
                        [v_mlp_w1[1], v_mlp_w2[1], v_gla_w_in[0], v_gla_w_out[0]])
    res_mlp0 = reduce_end(rb_ici, rb_mine, res_l1[3][0], "mlp0", [mlp_w1[0], mlp_w2[0]],
                          [m_mlp_w1[0], m_mlp_w2[0]], [v_mlp_w1[0], v_mlp_w2[0]])

    def stacked(a, b):
        return tuple(jnp.stack([a[k], b[k]]) for k in range(4))

    res = {"mlp_w1": stacked(res_mlp0[0], res_l1[0]), "mlp_w2": stacked(res_mlp0[1], res_l1[1]),
           "gla_w_in": tuple(res_l1[2][k][None] for k in range(4)),
           "gla_w_out": tuple(res_l1[3][k][None] for k in range(4))}

    d_wa = _block_diag_extract(d_wbd[:, :, :RG_W])[None]
    d_wx = _block_diag_extract(d_wbd[:, :, RG_W:])[None]
    small_full = {
        "norm_mix": jnp.concatenate([d_nmix0, d_nmix1], axis=0), "norm_mlp": jnp.concatenate([d_nmlp0, d_nmlp1], axis=0),
        "norm_final": d_nfin.reshape(D_MODEL), "rg_conv_b": d_cb, "rg_w_a": d_wa, "rg_w_x": d_wx,
        "hg_lb_logits": jnp.concatenate([d_l0[0] + d_l0[1], d_l1[0] + d_l1[1]], axis=0), "hg_norm": d_hgnorm,
        "rg_conv_w": d_cw8[0:4][None], "rg_b_a": d_rgb[:, 0, :RG_W][None], "rg_b_x": d_rgb[:, 0, RG_W:][None],
        "rg_lambda": d_lam[:, 0, :][None],
        "gla_w_gate_up": jnp.stack([d_wup[0][0:16], d_wup[1][16:32]])[None], "gla_b_gate": d_bg[:, 0, :][None],
        "gla_norm": d_gnorm}
    small_names = SMALL_REPLICATED + SMALL_SHARDED
    packed = _pack_rows([small_full[n] for n in small_names], 8)
    summed = _allgather_vmem(packed, "ar_small", reduce=True)
    g_small = {}
    off = 0
    for n in small_names:
        a = small_full[n]
        gfull = summed[off:off + a.size // LANES].reshape(a.shape)
        off += _part_rows(a)
        if n in SMALL_SHARDED:
            loc = w_loc[n].shape[-1]
            gfull = lax.dynamic_slice_in_dim(gfull, dev * loc, loc, axis=gfull.ndim - 1)
        g_small[n] = gfull
    sw = _pack_rows([w_loc[n] for n in small_names], 256)
    sg = _pack_rows([g_small[n] for n in small_names], 256)
    sm = _pack_rows([m_loc[n] for n in small_names], 256)
    sv = _pack_rows([v_loc[n] for n in small_names], 256)
    small_res = _adamw(sw, (sg, None), sm, sv, "adamw_small")
    others_done = sum(r[1][0:SUBLANES, 0:LANES] for r in res_l1 + res_mlp0 + [small_res])
    res_ab = reduce_end(rc_ici, rc_mine, others_done, "ab", [ab_w_in[0], ab_w_out[0]],
                        [m_ab_w_in[0], m_ab_w_out[0]], [v_ab_w_in[0], v_ab_w_out[0]])
    res["ab_w_in"] = tuple(res_ab[0][k][None] for k in range(4))
    res["ab_w_out"] = tuple(res_ab[1][k][None] for k in range(4))
    off = 0
    for n in small_names:
        a = w_loc[n]
        nr = a.size // LANES
        res[n] = tuple(small_res[k][off:off + nr].reshape(a.shape) for k in range(4))
        off += _part_rows(a)

    grad_x = dx.reshape(1, T, D_MODEL)
    out = [loss, grad_x]
    for k in range(4):
        out += [res[n][k] for n in WEIGHT_NAMES]
    return tuple(out)
```

```python
import jax
import jax.numpy as jnp
from jax import lax
from jax.experimental import pallas as pl
from jax.experimental.pallas import tpu as pltpu

F32, BF16 = jnp.float32, jnp.bfloat16
HI = lax.Precision.HIGHEST
MESH = pl.DeviceIdType.MESH

D_MODEL = 1024
D_FF = 4096
RG_W = 512
HG_W = 512
CHUNK = 64
EPS = 1e-6
RG_C = 8.0
AB_IN = 3584
GLA_IN = 3104
GLA_IN_PAD = 3200
N_DEV = 8
LANES = 128
SUBLANES = 8
VMEM_LIMIT = 48 * 1024 * 1024

ADAM_LR, ADAM_B1, ADAM_B2, ADAM_EPS, ADAM_WD, ADAM_STEP = 0.001, 0.9, 0.999, 1e-08, 0.01, 10


def _params(*sem):
    return pltpu.CompilerParams(dimension_semantics=sem, vmem_limit_bytes=VMEM_LIMIT)


def _dg(a, b, ca, cb):
    return lax.dot_general(a.astype(BF16), b.astype(BF16), (((ca,), (cb,)), ((), ())),
                           preferred_element_type=F32)


@jax.custom_vjp
def _mm_nn(a, b):
    return _dg(a, b, 1, 0)


_mm_nn.defvjp(lambda a, b: (_dg(a, b, 1, 0), (a, b)),
              lambda res, g: (_dg(g, res[1], 1, 1), _dg(res[0], g, 0, 0)))


@jax.custom_vjp
def _mm_nt(a, b):
    return _dg(a, b, 1, 1)


_mm_nt.defvjp(lambda a, b: (_dg(a, b, 1, 1), (a, b)),
              lambda res, g: (_dg(g, res[1], 1, 0), _dg(g, res[0], 0, 0)))


@jax.custom_vjp
def _mm_tn(a, b):
    return _dg(a, b, 0, 0)


_mm_tn.defvjp(lambda a, b: (_dg(a, b, 0, 0), (a, b)),
              lambda res, g: (_dg(res[1], g, 1, 1), _dg(res[0], g, 1, 0)))


@jax.custom_vjp
def _cum(tri, tri_t, x):
    return jnp.dot(tri, x, precision=HI, preferred_element_type=F32)


_cum.defvjp(lambda tri, tri_t, x: (jnp.dot(tri, x, precision=HI, preferred_element_type=F32), (tri, tri_t)),
            lambda res, g: (jnp.zeros_like(res[0]), jnp.zeros_like(res[1]),
                            jnp.dot(res[1], g, precision=HI, preferred_element_type=F32)))


def _sig(x):
    return 1.0 / (1.0 + jnp.exp(-x))


def _gelu(x):
    return 0.5 * x * (1.0 + jnp.tanh(0.7978845608028654 * (x + 0.044715 * (x * x * x))))


def _softplus(z):
    return jnp.maximum(z, 0.0) + jnp.log(1.0 + jnp.exp(-jnp.abs(z)))


def _rms(x):
    return lax.rsqrt(jnp.mean(x * x, axis=-1, keepdims=True) + EPS)


def _rmsnorm_bwd(x, gain, dy):
    r = _rms(x)
    xh = x * r
    dgain = jnp.sum(dy * xh, axis=0, keepdims=True)
    dxh = dy * gain
    dx = r * (dxh - xh * jnp.mean(dxh * xh, axis=-1, keepdims=True))
    return dx, dgain


def _headnorm(o, gain, n_heads, hd):
    parts = []
    for h in range(n_heads):
        oh = o[:, h * hd:(h + 1) * hd]
        parts.append(oh * _rms(oh))
    return jnp.concatenate(parts, axis=1) * gain


def _tri_consts(d):
    row = lax.broadcasted_iota(jnp.int32, (CHUNK, CHUNK), 0)
    col = lax.broadcasted_iota(jnp.int32, (CHUNK, CHUNK), 1)
    ge = (row >= col).astype(F32)
    le = (row <= col).astype(F32)
    r1 = lax.broadcasted_iota(jnp.int32, (CHUNK, 1), 0)
    if d == 0:
        return ge, le, (r1 <= CHUNK // 2).astype(F32)
    return le, ge, (r1 >= CHUNK // 2 - 1).astype(F32)


def _chunk_core(qh, k, v, logf, st_prev, tri, tri_t, mref, n_heads, dk, dv):
    cum = _cum(tri, tri_t, logf)
    ref = jnp.sum(logf * mref, axis=0, keepdims=True)
    last = jnp.sum(logf, axis=0, keepdims=True)
    q_in = qh * jnp.exp(cum - ref)
    k_in = k * jnp.exp(ref - cum)
    k_st = k * jnp.exp(last - cum)
    q_dec = qh * jnp.exp(cum)
    decay = jnp.exp(last)
    outs, sts = [], []
    for h in range(n_heads):
        sk = slice(h * dk, (h + 1) * dk)
        sv = slice(h * dv, (h + 1) * dv)
        sc = _mm_nt(q_in[:, sk], k_in[:, sk]) * tri
        o = _mm_nn(sc, v[:, sv]) + _mm_nt(q_dec[:, sk], st_prev[h])
        sts.append(st_prev[h] * decay[:, sk] + _mm_tn(v[:, sv], k_st[:, sk]))
        outs.append(o)
    return jnp.concatenate(outs, axis=1), tuple(sts)


def _hg_chunk(q, f, v, l0, l1, st_prev, tri, tri_t, mref):
    lb = _sig(l0 - l1)
    sg = _sig(f)
    qh = q * _sig(q)
    logf = jnp.log(lb + (1.0 - lb) * sg)
    k = (1.0 - lb) * (1.0 - sg)
    return _chunk_core(qh, k, v, logf, st_prev, tri, tri_t, mref, 4, 128, 128)


def _gla_chunk(q, k, v, z, st_prev, tri, tri_t, mref):
    logf = (jnp.minimum(z, 0.0) - jnp.log(1.0 + jnp.exp(-jnp.abs(z)))) * (1.0 / 16.0)
    qh = q * (128.0 ** -0.5)
    return _chunk_core(qh, k, v, logf, st_prev, tri, tri_t, mref, 4, 128, 256)


def _rg_gates(xc, wbd, bias, lam):
    z = _mm_nn(xc, wbd) + bias
    r = _sig(z[:, :RG_W])
    i = _sig(z[:, RG_W:])
    log_a = -RG_C * r * _softplus(-lam)
    a = jnp.exp(log_a)
    x2 = 2.0 * log_a
    neg_expm1 = jnp.where(x2 > -1e-2, -(x2 + 0.5 * x2 * x2 + x2 * x2 * x2 * (1.0 / 6.0)), 1.0 - jnp.exp(x2))
    u = jnp.sqrt(neg_expm1) * (i * xc)
    return a, u


def _l0_combine(hf, hb, ga, of, ob, g, gain):
    ya = (hf + hb) * _gelu(ga)
    yb = _headnorm(of + ob, gain, 4, 128) * (g * _sig(g))
    return jnp.concatenate([ya, yb], axis=1)


def _l1_combine(of, ob, r, gain):
    return _headnorm(of + ob, gain, 4, 256) * (r * _sig(r))


def _norm_matmul(h, gain, w, name):
    T, D = h.shape
    N = w.shape[1]
    tm = min(512, T)

    def body(h_ref, g_ref, w_ref, o_ref, y_ref):
        x = h_ref[...]
        y = (x * _rms(x) * g_ref[...]).astype(BF16)
        y_ref[...] = y
        o_ref[...] = jnp.dot(y, w_ref[...], preferred_element_type=F32)

    return pl.pallas_call(
        body, name=name, grid=(T // tm,),
        in_specs=[pl.BlockSpec((tm, D), lambda i: (i, 0)), pl.BlockSpec((1, D), lambda i: (0, 0)),
                  pl.BlockSpec((D, N), lambda i: (0, 0))],
        out_specs=[pl.BlockSpec((tm, N), lambda i: (i, 0)), pl.BlockSpec((tm, D), lambda i: (i, 0))],
        out_shape=[jax.ShapeDtypeStruct((T, N), F32), jax.ShapeDtypeStruct((T, D), BF16)],
        compiler_params=_params("parallel"))(h, gain, w)


def _matmul_res(a, w, res, name):
    T, K = a.shape
    N = w.shape[1]
    tm = min(512, T)

    def body(a_ref, w_ref, r_ref, o_ref):
        o_ref[...] = r_ref[...] + jnp.dot(a_ref[...], w_ref[...], preferred_element_type=F32)

    return pl.pallas_call(
        body, name=name, grid=(T // tm,),
        in_specs=[pl.BlockSpec((tm, K), lambda i: (i, 0)), pl.BlockSpec((K, N), lambda i: (0, 0)),
                  pl.BlockSpec((tm, N), lambda i: (i, 0))],
        out_specs=pl.BlockSpec((tm, N), lambda i: (i, 0)),
        out_shape=jax.ShapeDtypeStruct((T, N), F32),
        compiler_params=_params("parallel"))(a, w, res)


def _dgrad(dc, w, name):
    T, N = dc.shape
    K = w.shape[0]
    tm = min(512, T)

    def body(d_ref, w_ref, o_ref):
        o_ref[...] = _dg(d_ref[...], w_ref[...], 1, 1)

    return pl.pallas_call(
        body, name=name, grid=(T // tm,),
        in_specs=[pl.BlockSpec((tm, N), lambda i: (i, 0)), pl.BlockSpec((K, N), lambda i: (0, 0))],
        out_specs=pl.BlockSpec((tm, K), lambda i: (i, 0)),
        out_shape=jax.ShapeDtypeStruct((T, K), F32),
        compiler_params=_params("parallel"))(dc, w)


def _dgrad_norm(dproj, w, h, gain, dres, name):
    T, N = dproj.shape
    D = w.shape[0]
    tm = min(512, T)

    def body(dp_ref, w_ref, h_ref, g_ref, dr_ref, dh_ref, dhb_ref, dg_ref):
        @pl.when(pl.program_id(0) == 0)
        def _():
            dg_ref[...] = jnp.zeros_like(dg_ref)

        dy = _dg(dp_ref[...], w_ref[...], 1, 1)
        dx, dgain = _rmsnorm_bwd(h_ref[...], g_ref[...], dy)
        dh = dr_ref[...] + dx
        dh_ref[...] = dh
        dhb_ref[...] = dh.astype(BF16)
        dg_ref[...] += dgain

    return pl.pallas_call(
        body, name=name, grid=(T // tm,),
        in_specs=[pl.BlockSpec((tm, N), lambda i: (i, 0)), pl.BlockSpec((D, N), lambda i: (0, 0)),
                  pl.BlockSpec((tm, D), lambda i: (i, 0)), pl.BlockSpec((1, D), lambda i: (0, 0)),
                  pl.BlockSpec((tm, D), lambda i: (i, 0))],
        out_specs=[pl.BlockSpec((tm, D), lambda i: (i, 0)), pl.BlockSpec((tm, D), lambda i: (i, 0)),
                   pl.BlockSpec((1, D), lambda i: (0, 0))],
        out_shape=[jax.ShapeDtypeStruct((T, D), F32), jax.ShapeDtypeStruct((T, D), BF16),
                   jax.ShapeDtypeStruct((1, D), F32)],
        compiler_params=_params("arbitrary"))(dproj, w, h, gain, dres)


def _wgrad(a, b, tn, name, sharded_cols=False):
    T, K = a.shape
    N = b.shape[1]
    tk = min(1024, K)
    tt = min(1024, T)
    nt = T // tt

    def body(a_ref, b_ref, o_ref):
        @pl.when(pl.program_id(2) == 0)
        def _():
            o_ref[...] = jnp.zeros_like(o_ref)

        o_ref[...] += _dg(a_ref[...], b_ref[...], 0, 0)

    if sharded_cols:
        out_spec = pl.BlockSpec((None, tk, tn), lambda k, n, t: (n, k, 0))
        out_shape = jax.ShapeDtypeStruct((N // tn, K, tn), F32)
    else:
        out_spec = pl.BlockSpec((tk, tn), lambda k, n, t: (k, n))
        out_shape = jax.ShapeDtypeStruct((K, N), F32)
    return pl.pallas_call(
        body, name=name, grid=(K // tk, N // tn, nt),
        in_specs=[pl.BlockSpec((tt, tk), lambda k, n, t: (t, k)), pl.BlockSpec((tt, tn), lambda k, n, t: (t, n))],
        out_specs=out_spec, out_shape=out_shape,
        compiler_params=_params("parallel", "parallel", "arbitrary"))(a, b)


def _mlp_fwd(h, gain, w1g, w2, name):
    T, D = h.shape
    nf, _, tf = w1g.shape
    tm = min(1024, T)

    def body(h_ref, g_ref, w1_ref, w2_ref, o_ref, pre_ref, y_ref, ysc, acc):
        j = pl.program_id(1)

        @pl.when(j == 0)
        def _():
            x = h_ref[...]
            y = (x * _rms(x) * g_ref[...]).astype(BF16)
            ysc[...] = y
            y_ref[...] = y
            acc[...] = jnp.zeros_like(acc)

        pre = jnp.dot(ysc[...], w1_ref[...], preferred_element_type=F32)
        pre_ref[...] = pre.astype(BF16)
        act = jnp.square(jnp.maximum(pre, 0.0))
        acc[...] += jnp.dot(act.astype(BF16), w2_ref[...], preferred_element_type=F32)

        @pl.when(j == nf - 1)
        def _():
            o_ref[...] = h_ref[...] + acc[...]

    return pl.pallas_call(
        body, name=name, grid=(T // tm, nf),
        in_specs=[pl.BlockSpec((tm, D), lambda i, j: (i, 0)), pl.BlockSpec((1, D), lambda i, j: (0, 0)),
                  pl.BlockSpec((None, D, tf), lambda i, j: (j, 0, 0)), pl.BlockSpec((tf, D), lambda i, j: (j, 0))],
        out_specs=[pl.BlockSpec((tm, D), lambda i, j: (i, 0)), pl.BlockSpec((tm, tf), lambda i, j: (i, j)),
                   pl.BlockSpec((tm, D), lambda i, j: (i, 0))],
        out_shape=[jax.ShapeDtypeStruct((T, D), F32), jax.ShapeDtypeStruct((T, nf * tf), BF16),
                   jax.ShapeDtypeStruct((T, D), BF16)],
        scratch_shapes=[pltpu.VMEM((tm, D), BF16), pltpu.VMEM((tm, D), F32)],
        compiler_params=_params("parallel", "arbitrary"))(h, gain, w1g, w2)


def _mlp_bwd(dout, dout_b, h, gain, pre, w1g, w2, name):
    T, D = h.shape
    nf, _, tf = w1g.shape
    tm = min(512, T)

    def body(do_ref, dob_ref, h_ref, g_ref, pre_ref, w1_ref, w2_ref, dh_ref, dhb_ref, dpre_ref, act_ref, dg_ref, dy):
        i, j = pl.program_id(0), pl.program_id(1)

        @pl.when(j == 0)
        def _():
            dy[...] = jnp.zeros_like(dy)

        @pl.when((i == 0) & (j == 0))
        def _():
            dg_ref[...] = jnp.zeros_like(dg_ref)

        rp = jnp.maximum(pre_ref[...].astype(F32), 0.0)
        dact = _dg(dob_ref[...], w2_ref[...], 1, 1)
        dpre = (dact * (2.0 * rp)).astype(BF16)
        dpre_ref[...] = dpre
        act_ref[...] = (rp * rp).astype(BF16)
        dy[...] += _dg(dpre, w1_ref[...], 1, 1)

        @pl.when(j == nf - 1)
        def _():
            dx, dgain = _rmsnorm_bwd(h_ref[...], g_ref[...], dy[...])
            dh = do_ref[...] + dx
            dh_ref[...] = dh
            dhb_ref[...] = dh.astype(BF16)
            dg_ref[...] += dgain

    return pl.pallas_call(
        body, name=name, grid=(T // tm, nf),
        in_specs=[pl.BlockSpec((tm, D), lambda i, j: (i, 0)), pl.BlockSpec((tm, D), lambda i, j: (i, 0)),
                  pl.BlockSpec((tm, D), lambda i, j: (i, 0)),
                  pl.BlockSpec((1, D), lambda i, j: (0, 0)), pl.BlockSpec((tm, tf), lambda i, j: (i, j)),
                  pl.BlockSpec((None, D, tf), lambda i, j: (j, 0, 0)), pl.BlockSpec((tf, D), lambda i, j: (j, 0))],
        out_specs=[pl.BlockSpec((tm, D), lambda i, j: (i, 0)), pl.BlockSpec((tm, D), lambda i, j: (i, 0)),
                   pl.BlockSpec((tm, tf), lambda i, j: (i, j)),
                   pl.BlockSpec((tm, tf), lambda i, j: (i, j)), pl.BlockSpec((1, D), lambda i, j: (0, 0))],
        out_shape=[jax.ShapeDtypeStruct((T, D), F32), jax.ShapeDtypeStruct((T, D), BF16),
                   jax.ShapeDtypeStruct((T, nf * tf), BF16),
                   jax.ShapeDtypeStruct((T, nf * tf), BF16), jax.ShapeDtypeStruct((1, D), F32)],
        scratch_shapes=[pltpu.VMEM((tm, D), F32)],
        compiler_params=_params("arbitrary", "arbitrary"))(dout, dout_b, h, gain, pre, w1g, w2)


def _final_loss(h, gain, target, name):
    T, D = h.shape
    tm = min(512, T)

    def body(h_ref, g_ref, t_ref, l_ref, dh_ref, dhb_ref, dg_ref):
        @pl.when(pl.program_id(0) == 0)
        def _():
            l_ref[...] = jnp.zeros_like(l_ref)
            dg_ref[...] = jnp.zeros_like(dg_ref)

        x = h_ref[...]
        err = x * _rms(x) * g_ref[...] - t_ref[...]
        l_ref[...] += 0.5 * jnp.sum(jnp.mean(err * err, axis=-1, keepdims=True), axis=0, keepdims=True)
        dx, dgain = _rmsnorm_bwd(x, g_ref[...], err * (1.0 / D))
        dh_ref[...] = dx
        dhb_ref[...] = dx.astype(BF16)
        dg_ref[...] += dgain

    return pl.pallas_call(
        body, name=name, grid=(T // tm,),
        in_specs=[pl.BlockSpec((tm, D), lambda i: (i, 0)), pl.BlockSpec((1, D), lambda i: (0, 0)),
                  pl.BlockSpec((tm, D), lambda i: (i, 0))],
        out_specs=[pl.BlockSpec((SUBLANES, LANES), lambda i: (0, 0)), pl.BlockSpec((tm, D), lambda i: (i, 0)),
                   pl.BlockSpec((tm, D), lambda i: (i, 0)), pl.BlockSpec((1, D), lambda i: (0, 0))],
        out_shape=[jax.ShapeDtypeStruct((SUBLANES, LANES), F32), jax.ShapeDtypeStruct((T, D), F32),
                   jax.ShapeDtypeStruct((T, D), BF16), jax.ShapeDtypeStruct((1, D), F32)],
        compiler_params=_params("arbitrary"))(h, gain, target)


def _halo_specs(tm, T, width, col, tile=lambda i: i):
    r8 = tm // SUBLANES
    nb8 = T // SUBLANES
    return [pl.BlockSpec((tm, width), lambda i: (tile(i), col)),
            pl.BlockSpec((SUBLANES, width), lambda i: (jnp.maximum(tile(i) * r8 - 1, 0), col)),
            pl.BlockSpec((SUBLANES, width), lambda i: (jnp.minimum((tile(i) + 1) * r8, nb8 - 1), col))]


def _ext(cur, prev, nxt, has_prev, has_next):
    return jnp.concatenate([jnp.where(has_prev, prev, 0.0), cur, jnp.where(has_next, nxt, 0.0)], axis=0)


def _shifted(ext, offset, tm):
    n = ext.shape[0]
    sh = (-offset) % n
    r = ext if sh == 0 else pltpu.roll(ext, sh, 0)
    return r[SUBLANES:SUBLANES + tm]


def _rg_conv_fwd(proj, cw8, cb, name):
    T = proj.shape[0]
    tm = min(512, T)
    nT = T // tm

    def body(cur_ref, prev_ref, next_ref, w_ref, b_ref, o_ref):
        i = pl.program_id(0)
        ext = _ext(cur_ref[...], prev_ref[...], next_ref[...], i > 0, i < nT - 1)
        acc = jnp.broadcast_to(b_ref[...], (tm, RG_W))
        for k in range(4):
            acc = acc + w_ref[k:k + 1, :] * _shifted(ext, k - 2, tm)
        o_ref[...] = acc

    return pl.pallas_call(
        body, name=name, grid=(nT,),
        in_specs=_halo_specs(tm, T, RG_W, 0) + [pl.BlockSpec((SUBLANES, RG_W), lambda i: (0, 0)),
                                                pl.BlockSpec((1, RG_W), lambda i: (0, 0))],
        out_specs=pl.BlockSpec((tm, RG_W), lambda i: (i, 0)),
        out_shape=jax.ShapeDtypeStruct((T, RG_W), F32),
        compiler_params=_params("parallel"))(proj, proj, proj, cw8, cb)


def _rg_conv_bwd(dxc, proj, cw8, name):
    T = proj.shape[0]
    tm = min(512, T)
    nT = T // tm

    def body(a0, p0, n0, a1, p1, n1, xa, xp, xn, w_ref, dxa_ref, dw_ref, db_ref):
        i = pl.program_id(0)

        @pl.when(i == 0)
        def _():
            dw_ref[...] = jnp.zeros_like(dw_ref)
            db_ref[...] = jnp.zeros_like(db_ref)

        has_p, has_n = i > 0, i < nT - 1
        cur = a0[...] + a1[...]
        dext = _ext(cur, p0[...] + p1[...], n0[...] + n1[...], has_p, has_n)
        xext = _ext(xa[...], xp[...], xn[...], has_p, has_n)
        acc = jnp.zeros((tm, RG_W), F32)
        rows = []
        for k in range(4):
            acc = acc + w_ref[k:k + 1, :] * _shifted(dext, 2 - k, tm)
            rows.append(jnp.sum(cur * _shifted(xext, k - 2, tm), axis=0, keepdims=True))
        dxa_ref[...] = acc
        dw_ref[...] += jnp.concatenate(rows + [jnp.zeros((4, RG_W), F32)], axis=0)
        db_ref[...] += jnp.sum(cur, axis=0, keepdims=True)

    return pl.pallas_call(
        body, name=name, grid=(nT,),
        in_specs=(_halo_specs(tm, T, RG_W, 0) + _halo_specs(tm, T, RG_W, 0)
                  + _halo_specs(tm, T, RG_W, 0) + [pl.BlockSpec((SUBLANES, RG_W), lambda i: (0, 0))]),
        out_specs=[pl.BlockSpec((tm, RG_W), lambda i: (i, 0)), pl.BlockSpec((SUBLANES, RG_W), lambda i: (0, 0)),
                   pl.BlockSpec((1, RG_W), lambda i: (0, 0))],
        out_shape=[jax.ShapeDtypeStruct((T, RG_W), F32), jax.ShapeDtypeStruct((SUBLANES, RG_W), F32),
                   jax.ShapeDtypeStruct((1, RG_W), F32)],
        compiler_params=_params("arbitrary"))(dxc[0], dxc[0], dxc[0], dxc[1], dxc[1], dxc[1], proj, proj, proj, cw8)


def _local_scan(a, b, ascending):
    n = a.shape[0]
    pos = jnp.bitwise_and(lax.broadcasted_iota(jnp.int32, a.shape, 0), SUBLANES - 1)
    for s in (1, 2, 4):
        sh = s if ascending else n - s
        ok = (pos >= s) if ascending else (pos < SUBLANES - s)
        a_sh, b_sh = pltpu.roll(a, sh, 0), pltpu.roll(b, sh, 0)
        b = jnp.where(ok, a * b_sh + b, b)
        a = jnp.where(ok, a * a_sh, a)
    return a, b


def _group_scan(chains, a_sc, b_sc, carry, n_groups):
    def step(g, hs):
        new = []
        for (d, out_ref, asc), h in zip(chains, hs):
            r0 = pl.multiple_of((g if asc else n_groups - 1 - g) * SUBLANES, SUBLANES)
            out_ref[pl.ds(r0, SUBLANES), :] = a_sc[d, pl.ds(r0, SUBLANES), :] * h + b_sc[d, pl.ds(r0, SUBLANES), :]
            new.append(out_ref[pl.ds(r0 + (SUBLANES - 1 if asc else 0), 1), :])
        return tuple(new)

    hs = lax.fori_loop(0, n_groups, step, tuple(carry[d, 0:1, :] for d, _, _ in chains))
    for (d, _, _), h in zip(chains, hs):
        carry[d, 0:1, :] = h


def _rg_scan_fwd(xc, wbd, bias, lam, name):
    T = xc.shape[0]
    tm = min(512, T)
    nT = T // tm

    def body(xf_ref, xb_ref, w_ref, b_ref, lam_ref, hf_ref, hb_ref, a_sc, b_sc, carry):
        @pl.when(pl.program_id(0) == 0)
        def _():
            carry[...] = jnp.zeros_like(carry)

        for d, x_ref in enumerate((xf_ref, xb_ref)):
            a, u = _rg_gates(x_ref[...], w_ref[d], b_ref[d], lam_ref[d])
            a_sc[d], b_sc[d] = _local_scan(a, u, d == 0)
        _group_scan(((0, hf_ref, True), (1, hb_ref, False)), a_sc, b_sc, carry, tm // SUBLANES)

    full = lambda a: pl.BlockSpec(a.shape, lambda i: (0,) * len(a.shape))
    res = pl.pallas_call(
        body, name=name, grid=(nT,),
        in_specs=[pl.BlockSpec((tm, RG_W), lambda i: (i, 0)), pl.BlockSpec((tm, RG_W), lambda i: (nT - 1 - i, 0)),
                  full(wbd), full(bias), full(lam)],
        out_specs=[pl.BlockSpec((tm, RG_W), lambda i: (i, 0)), pl.BlockSpec((tm, RG_W), lambda i: (nT - 1 - i, 0))],
        out_shape=[jax.ShapeDtypeStruct((T, RG_W), F32)] * 2,
        scratch_shapes=[pltpu.VMEM((2, tm, RG_W), F32), pltpu.VMEM((2, tm, RG_W), F32),
                        pltpu.VMEM((2, SUBLANES, RG_W), F32)],
        compiler_params=_params("arbitrary"))(xc, xc, wbd, bias, lam)
    return res[0], res[1]


def _rg_scan_bwd(xc, wbd, bias, lam, hs, dho, name):
    T = xc.shape[0]
    tm = min(256, T)
    nT = T // tm
    tiles = (lambda i: nT - 1 - i, lambda i: i)

    def body(xf_ref, xb_ref, w_ref, b_ref, lam_ref, hfc, hfp, hfn, hbc, hbp, hbn, dof_ref, dob_ref,
             dxf_ref, dxb_ref, dw_ref, db_ref, dlam_ref, a_sc, b_sc, y_sc, carry):
        i = pl.program_id(0)

        @pl.when(i == 0)
        def _():
            carry[...] = jnp.zeros_like(carry)
            dw_ref[...] = jnp.zeros_like(dw_ref)
            db_ref[...] = jnp.zeros_like(db_ref)
            dlam_ref[...] = jnp.zeros_like(dlam_ref)

        vjps, entering = [], []
        for d, (x_ref, do_ref) in enumerate(((xf_ref, dof_ref), (xb_ref, dob_ref))):
            (a, _), vjp = jax.vjp(_rg_gates, x_ref[...], w_ref[d].astype(F32), b_ref[d], lam_ref[d])
            vjps.append(vjp)
            entering.append(carry[d, 0:1, :])
            a_sc[d], b_sc[d] = _local_scan(a, a * do_ref[...], d == 1)
        _group_scan(((0, y_sc.at[0], False), (1, y_sc.at[1], True)), a_sc, b_sc, carry, tm // SUBLANES)

        row = lax.broadcasted_iota(jnp.int32, (tm, RG_W), 0)
        for d, (do_ref, dx_ref, hc, hp, hn, ti) in enumerate(
                ((dof_ref, dxf_ref, hfc, hfp, hfn, nT - 1 - i), (dob_ref, dxb_ref, hbc, hbp, hbn, i))):
            y = y_sc[d]
            if d == 0:
                y_next = jnp.where(row == tm - 1, entering[d], pltpu.roll(y, tm - 1, 0))
            else:
                y_next = jnp.where(row == 0, entering[d], pltpu.roll(y, 1, 0))
            dtot = do_ref[...] + y_next
            ext = _ext(hc[...], hp[...], hn[...], ti > 0, ti < nT - 1)
            hprev = _shifted(ext, -1 if d == 0 else 1, tm)
            dxc, dw, db, dlam = vjps[d]((dtot * hprev, dtot))
            dx_ref[...] = dxc
            dw_ref[d] += dw
            db_ref[d] += db
            dlam_ref[d] += dlam

    full = lambda a: pl.BlockSpec(a.shape, lambda i: (0,) * len(a.shape))
    tok = lambda d: pl.BlockSpec((tm, RG_W), lambda i: (tiles[d](i), 0))
    acc_shapes = [jax.ShapeDtypeStruct((2, RG_W, 2 * RG_W), F32), jax.ShapeDtypeStruct((2, 1, 2 * RG_W), F32),
                  jax.ShapeDtypeStruct((2, 1, RG_W), F32)]
    res = pl.pallas_call(
        body, name=name, grid=(nT,),
        in_specs=([tok(0), tok(1), full(wbd), full(bias), full(lam)]
                  + _halo_specs(tm, T, RG_W, 0, tiles[0]) + _halo_specs(tm, T, RG_W, 0, tiles[1]) + [tok(0), tok(1)]),
        out_specs=[tok(0), tok(1)] + [full(s) for s in acc_shapes],
        out_shape=[jax.ShapeDtypeStruct((T, RG_W), F32)] * 2 + acc_shapes,
        scratch_shapes=[pltpu.VMEM((2, tm, RG_W), F32), pltpu.VMEM((2, tm, RG_W), F32),
                        pltpu.VMEM((2, tm, RG_W), F32), pltpu.VMEM((2, SUBLANES, RG_W), F32)],
        compiler_params=_params("arbitrary"))(xc, xc, wbd, bias, lam, hs[0], hs[0], hs[0], hs[1], hs[1], hs[1],
                                              dho, dho)
    return (res[0], res[1]), res[2], res[3], res[4]


def _chunk_rows(n_chunks, reverse):
    up, down = (lambda c: c), (lambda c: n_chunks - 1 - c)
    return (down, up) if reverse else (up, down)


def _hg_fwd(proj, l0, l1, name):
    T = proj.shape[0]
    nC = T // CHUNK
    H, dk, dv = 4, 128, 128
    rows = _chunk_rows(nC, False)

    def body(qf, ff, vf, qb, fb, vb, l0_ref, l1_ref, of, ob, spf, spb, st):
        @pl.when(pl.program_id(0) == 0)
        def _():
            st[...] = jnp.zeros_like(st)

        for d, (q, f, v, o, sp) in enumerate(((qf, ff, vf, of, spf), (qb, fb, vb, ob, spb))):
            tri, tri_t, mref = _tri_consts(d)
            stp = tuple(st[d, h] for h in range(H))
            sp[...] = st[d]
            o_val, stn = _hg_chunk(q[...], f[...], v[...], l0_ref[...], l1_ref[...], stp, tri, tri_t, mref)
            o[...] = o_val
            for h in range(H):
                st[d, h] = stn[h]

    tok = lambda d, col: pl.BlockSpec((CHUNK, HG_W), lambda c: (rows[d](c), col))
    par = pl.BlockSpec((1, HG_W), lambda c: (0, 0))
    state = lambda d: pl.BlockSpec((None, H, dv, dk), lambda c: (rows[d](c), 0, 0, 0))
    res = pl.pallas_call(
        body, name=name, grid=(nC,),
        in_specs=[tok(0, 2), tok(0, 3), tok(0, 5), tok(1, 2), tok(1, 4), tok(1, 5), par, par],
        out_specs=[tok(0, 0), tok(1, 0), state(0), state(1)],
        out_shape=[jax.ShapeDtypeStruct((T, H * dv), F32)] * 2 + [jax.ShapeDtypeStruct((nC, H, dv, dk), F32)] * 2,
        scratch_shapes=[pltpu.VMEM((2, H, dv, dk), F32)],
        compiler_params=_params("arbitrary"))(proj, proj, proj, proj, proj, proj, l0, l1)
    return (res[0], res[1]), (res[2], res[3])


def _hg_bwd(proj, l0, l1, sprev, do, name):
    T = proj.shape[0]
    nC = T // CHUNK
    H, dk, dv = 4, 128, 128
    rows = _chunk_rows(nC, True)

    def body(qf, ff, vf, qb, fb, vb, l0_ref, l1_ref, spf, spb, dof, dob,
             dqf, dff, dvf, dqb, dfb, dvb, dl0_ref, dl1_ref, dst):
        @pl.when(pl.program_id(0) == 0)
        def _():
            dst[...] = jnp.zeros_like(dst)
            dl0_ref[...] = jnp.zeros_like(dl0_ref)
            dl1_ref[...] = jnp.zeros_like(dl1_ref)

        for d, (q, f, v, sp, do_ref, dq_ref, df_ref, dv_ref) in enumerate(
                ((qf, ff, vf, spf, dof, dqf, dff, dvf), (qb, fb, vb, spb, dob, dqb, dfb, dvb))):
            tri, tri_t, mref = _tri_consts(d)
            fn = lambda q_, f_, v_, a0, a1, stp: _hg_chunk(q_, f_, v_, a0, a1, stp, tri, tri_t, mref)
            stp = tuple(sp[h] for h in range(H))
            _, vjp = jax.vjp(fn, q[...], f[...], v[...], l0_ref[...], l1_ref[...], stp)
            dq, df, dvv, dl0, dl1, dstp = vjp((do_ref[...], tuple(dst[d, h] for h in range(H))))
            dq_ref[...] = dq
            df_ref[...] = df
            dv_ref[...] = dvv
            dl0_ref[d] += dl0
            dl1_ref[d] += dl1
            for h in range(H):
                dst[d, h] = dstp[h]

    tok = lambda d, col: pl.BlockSpec((CHUNK, HG_W), lambda c: (rows[d](c), col))
    par = pl.BlockSpec((1, HG_W), lambda c: (0, 0))
    acc = pl.BlockSpec((2, 1, HG_W), lambda c: (0, 0, 0))
    state = lambda d: pl.BlockSpec((None, H, dv, dk), lambda c: (rows[d](c), 0, 0, 0))
    res = pl.pallas_call(
        body, name=name, grid=(nC,),
        in_specs=[tok(0, 2), tok(0, 3), tok(0, 5), tok(1, 2), tok(1, 4), tok(1, 5), par, par,
                  state(0), state(1), tok(0, 0), tok(1, 0)],
        out_specs=[tok(0, 0)] * 3 + [tok(1, 0)] * 3 + [acc, acc],
        out_shape=[jax.ShapeDtypeStruct((T, HG_W), F32)] * 6 + [jax.ShapeDtypeStruct((2, 1, HG_W), F32)] * 2,
        scratch_shapes=[pltpu.VMEM((2, H, dv, dk), F32)],
        compiler_params=_params("arbitrary"))(proj, proj, proj, proj, proj, proj, l0, l1, sprev[0], sprev[1], do, do)
    return (res[0], res[3]), (res[1], res[4]), (res[2], res[5]), res[6], res[7]


def _gate_logits(proj, wup, bg, name):
    T = proj.shape[0]
    tm = min(512, T)

    def body(lr_ref, w_ref, b_ref, z_ref, lrb_ref):
        lr = lr_ref[...].astype(BF16)
        lrb_ref[...] = lr
        for d in range(2):
            z_ref[d] = _dg(lr, w_ref[d], 1, 0) + b_ref[d]

    return pl.pallas_call(
        body, name=name, grid=(T // tm,),
        in_specs=[pl.BlockSpec((tm, LANES), lambda i: (i, 24)), pl.BlockSpec((2, LANES, 512), lambda i: (0, 0, 0)),
                  pl.BlockSpec((2, 1, 512), lambda i: (0, 0, 0))],
        out_specs=[pl.BlockSpec((2, tm, 512), lambda i: (0, i, 0)), pl.BlockSpec((tm, LANES), lambda i: (i, 0))],
        out_shape=[jax.ShapeDtypeStruct((2, T, 512), F32), jax.ShapeDtypeStruct((T, LANES), BF16)],
        compiler_params=_params("parallel"))(proj, wup, bg)


def _gate_logits_bwd(dz, wup, name):
    T = dz[0].shape[0]
    tm = min(512, T)

    def body(dzf_ref, dzb_ref, w_ref, dlr_ref, db_ref, dzb16_ref):
        @pl.when(pl.program_id(0) == 0)
        def _():
            db_ref[...] = jnp.zeros_like(db_ref)

        acc = jnp.zeros((tm, LANES), F32)
        for d, dz_ref in enumerate((dzf_ref, dzb_ref)):
            g = dz_ref[...]
            gb = g.astype(BF16)
            dzb16_ref[d] = gb
            acc = acc + _dg(gb, w_ref[d], 1, 1)
            db_ref[d] += jnp.sum(g, axis=0, keepdims=True)
        dlr_ref[...] = acc

    tok = pl.BlockSpec((tm, 512), lambda i: (i, 0))
    return pl.pallas_call(
        body, name=name, grid=(T // tm,),
        in_specs=[tok, tok, pl.BlockSpec((2, LANES, 512), lambda i: (0, 0, 0))],
        out_specs=[pl.BlockSpec((tm, LANES), lambda i: (i, 0)), pl.BlockSpec((2, 1, 512), lambda i: (0, 0, 0)),
                   pl.BlockSpec((2, tm, 512), lambda i: (0, i, 0))],
        out_shape=[jax.ShapeDtypeStruct((T, LANES), F32), jax.ShapeDtypeStruct((2, 1, 512), F32),
                   jax.ShapeDtypeStruct((2, T, 512), BF16)],
        compiler_params=_params("arbitrary"))(dz[0], dz[1], wup)


def _gla_fwd(proj, z, name):
    T = proj.shape[0]
    nC = T // CHUNK
    H, dk, dv = 4, 128, 256
    rows = _chunk_rows(nC, False)

    def body(qf, kf, vf, zf, qb, kb, vb, zb, of, ob, spf, spb, st):
        @pl.when(pl.program_id(0) == 0)
        def _():
            st[...] = jnp.zeros_like(st)

        for d, (q, k, v, z_ref, o, sp) in enumerate(((qf, kf, vf, zf, of, spf), (qb, kb, vb, zb, ob, spb))):
            tri, tri_t, mref = _tri_consts(d)
            stp = tuple(st[d, h] for h in range(H))
            sp[...] = st[d]
            o_val, stn = _gla_chunk(q[...], k[...], v[...], z_ref[...], stp, tri, tri_t, mref)
            o[...] = o_val
            for h in range(H):
                st[d, h] = stn[h]

    tok = lambda d, w, col: pl.BlockSpec((CHUNK, w), lambda c: (rows[d](c), col))
    gate = lambda d: pl.BlockSpec((None, CHUNK, 512), lambda c: (d, rows[d](c), 0))
    state = lambda d: pl.BlockSpec((None, H, dv, dk), lambda c: (rows[d](c), 0, 0, 0))
    res = pl.pallas_call(
        body, name=name, grid=(nC,),
        in_specs=[tok(0, 512, 0), tok(0, 512, 1), tok(0, 1024, 1), gate(0),
                  tok(1, 512, 0), tok(1, 512, 1), tok(1, 1024, 1), gate(1)],
        out_specs=[tok(0, H * dv, 0), tok(1, H * dv, 0), state(0), state(1)],
        out_shape=[jax.ShapeDtypeStruct((T, H * dv), F32)] * 2 + [jax.ShapeDtypeStruct((nC, H, dv, dk), F32)] * 2,
        scratch_shapes=[pltpu.VMEM((2, H, dv, dk), F32)],
        compiler_params=_params("arbitrary"))(proj, proj, proj, z, proj, proj, proj, z)
    return (res[0], res[1]), (res[2], res[3])


def _gla_bwd(proj, z, sprev, do, name):
    T = proj.shape[0]
    nC = T // CHUNK
    H, dk, dv = 4, 128, 256
    rows = _chunk_rows(nC, True)

    def body(qf, kf, vf, zf, qb, kb, vb, zb, spf, spb, dof, dob,
             dqf, dkf, dvf, dzf, dqb, dkb, dvb, dzb, dst):
        @pl.when(pl.program_id(0) == 0)
        def _():
            dst[...] = jnp.zeros_like(dst)

        for d, (q, k, v, z_ref, sp, do_ref, dq_ref, dk_ref, dv_ref, dz_ref) in enumerate(
                ((qf, kf, vf, zf, spf, dof, dqf, dkf, dvf, dzf), (qb, kb, vb, zb, spb, dob, dqb, dkb, dvb, dzb))):
            tri, tri_t, mref = _tri_consts(d)
            fn = lambda q_, k_, v_, z_, stp: _gla_chunk(q_, k_, v_, z_, stp, tri, tri_t, mref)
            stp = tuple(sp[h] for h in range(H))
            _, vjp = jax.vjp(fn, q[...], k[...], v[...], z_ref[...], stp)
            dq, dkk, dvv, dzz, dstp = vjp((do_ref[...], tuple(dst[d, h] for h in range(H))))
            dq_ref[...] = dq
            dk_ref[...] = dkk
            dv_ref[...] = dvv
            dz_ref[...] = dzz
            for h in range(H):
                dst[d, h] = dstp[h]

    tok = lambda d, w, col: pl.BlockSpec((CHUNK, w), lambda c: (rows[d](c), col))
    gate = lambda d: pl.BlockSpec((None, CHUNK, 512), lambda c: (d, rows[d](c), 0))
    state = lambda d: pl.BlockSpec((None, H, dv, dk), lambda c: (rows[d](c), 0, 0, 0))
    outs = lambda d: [tok(d, 512, 0), tok(d, 512, 0), tok(d, 1024, 0), tok(d, 512, 0)]
    shapes = [jax.ShapeDtypeStruct((T, 512), F32), jax.ShapeDtypeStruct((T, 512), F32),
              jax.ShapeDtypeStruct((T, 1024), F32), jax.ShapeDtypeStruct((T, 512), F32)]
    res = pl.pallas_call(
        body, name=name, grid=(nC,),
        in_specs=[tok(0, 512, 0), tok(0, 512, 1), tok(0, 1024, 1), gate(0),
                  tok(1, 512, 0), tok(1, 512, 1), tok(1, 1024, 1), gate(1),
                  state(0), state(1), tok(0, H * dv, 0), tok(1, H * dv, 0)],
        out_specs=outs(0) + outs(1), out_shape=shapes + shapes,
        scratch_shapes=[pltpu.VMEM((2, H, dv, dk), F32)],
        compiler_params=_params("arbitrary"))(proj, proj, proj, z, proj, proj, proj, z, sprev[0], sprev[1], do, do)
    return (res[0], res[4]), (res[1], res[5]), (res[2], res[6]), (res[3], res[7])


def _l0_combine_fwd(hs, proj, o, gain, name):
    T = proj.shape[0]
    tm = min(512, T)

    def body(hf, hb, ga, of, ob, g, gn, out):
        out[...] = _l0_combine(hf[...], hb[...], ga[...], of[...], ob[...], g[...], gn[...]).astype(BF16)

    tok = pl.BlockSpec((tm, 512), lambda i: (i, 0))
    return pl.pallas_call(
        body, name=name, grid=(T // tm,),
        in_specs=[tok, tok, pl.BlockSpec((tm, 512), lambda i: (i, 1)), tok, tok,
                  pl.BlockSpec((tm, 512), lambda i: (i, 6)), pl.BlockSpec((1, 512), lambda i: (0, 0))],
        out_specs=pl.BlockSpec((tm, 1024), lambda i: (i, 0)),
        out_shape=jax.ShapeDtypeStruct((T, 1024), BF16),
        compiler_params=_params("parallel"))(hs[0], hs[1], proj, o[0], o[1], proj, gain)


def _l0_combine_bwd(hs, proj, o, gain, dmix, name):
    T = proj.shape[0]
    tm = min(512, T)

    def body(hf, hb, ga, of, ob, g, gn, dm, dho_ref, dga_ref, do_ref, dg_ref, dgn_ref):
        @pl.when(pl.program_id(0) == 0)
        def _():
            dgn_ref[...] = jnp.zeros_like(dgn_ref)

        _, vjp = jax.vjp(_l0_combine, hf[...], hb[...], ga[...], of[...], ob[...], g[...], gn[...])
        dhf, _, dga, dof, _, dg, dgn = vjp(dm[...])
        dho_ref[...] = dhf
        dga_ref[...] = dga
        do_ref[...] = dof
        dg_ref[...] = dg
        dgn_ref[...] += dgn

    tok = lambda: pl.BlockSpec((tm, 512), lambda i: (i, 0))
    return pl.pallas_call(
        body, name=name, grid=(T // tm,),
        in_specs=[tok(), tok(), pl.BlockSpec((tm, 512), lambda i: (i, 1)), tok(), tok(),
                  pl.BlockSpec((tm, 512), lambda i: (i, 6)), pl.BlockSpec((1, 512), lambda i: (0, 0)),
                  pl.BlockSpec((tm, 1024), lambda i: (i, 0))],
        out_specs=[tok(), tok(), tok(), tok(), pl.BlockSpec((1, 512), lambda i: (0, 0))],
        out_shape=[jax.ShapeDtypeStruct((T, 512), F32)] * 4 + [jax.ShapeDtypeStruct((1, 512), F32)],
        compiler_params=_params("arbitrary"))(hs[0], hs[1], proj, o[0], o[1], proj, gain, dmix)


def _l0_assemble(dxa, dga, dq, df, dv, dg, name):
    T = dxa.shape[0]
    tm = min(512, T)

    def body(xa, ga, q0, q1, f0, f1, v0, v1, g, out):
        out[...] = jnp.concatenate([xa[...], ga[...], q0[...] + q1[...], f0[...], f1[...], v0[...] + v1[...],
                                    g[...]], axis=1).astype(BF16)

    tok = lambda: pl.BlockSpec((tm, 512), lambda i: (i, 0))
    return pl.pallas_call(
        body, name=name, grid=(T // tm,),
        in_specs=[tok() for _ in range(9)],
        out_specs=pl.BlockSpec((tm, AB_IN), lambda i: (i, 0)),
        out_shape=jax.ShapeDtypeStruct((T, AB_IN), BF16),
        compiler_params=_params("parallel"))(dxa, dga, dq[0], dq[1], df[0], df[1], dv[0], dv[1], dg)


def _l1_combine_fwd(o, proj, gain, name):
    T = proj.shape[0]
    tm = min(512, T)

    def body(of, ob, r, gn, out):
        out[...] = _l1_combine(of[...], ob[...], r[...], gn[...]).astype(BF16)

    tok = pl.BlockSpec((tm, 1024), lambda i: (i, 0))
    return pl.pallas_call(
        body, name=name, grid=(T // tm,),
        in_specs=[tok, tok, pl.BlockSpec((tm, 1024), lambda i: (i, 2)), pl.BlockSpec((1, 1024), lambda i: (0, 0))],
        out_specs=pl.BlockSpec((tm, 1024), lambda i: (i, 0)),
        out_shape=jax.ShapeDtypeStruct((T, 1024), BF16),
        compiler_params=_params("parallel"))(o[0], o[1], proj, gain)


def _l1_combine_bwd(o, proj, gain, dmix, name):
    T = proj.shape[0]
    tm = min(512, T)

    def body(of, ob, r, gn, dm, do_ref, dr_ref, dgn_ref):
        @pl.when(pl.program_id(0) == 0)
        def _():
            dgn_ref[...] = jnp.zeros_like(dgn_ref)

        _, vjp = jax.vjp(_l1_combine, of[...], ob[...], r[...], gn[...])
        dof, _, dr, dgn = vjp(dm[...])
        do_ref[...] = dof
        dr_ref[...] = dr
        dgn_ref[...] += dgn

    tok = lambda: pl.BlockSpec((tm, 1024), lambda i: (i, 0))
    return pl.pallas_call(
        body, name=name, grid=(T // tm,),
        in_specs=[tok(), tok(), pl.BlockSpec((tm, 1024), lambda i: (i, 2)),
                  pl.BlockSpec((1, 1024), lambda i: (0, 0)), tok()],
        out_specs=[tok(), tok(), pl.BlockSpec((1, 1024), lambda i: (0, 0))],
        out_shape=[jax.ShapeDtypeStruct((T, 1024), F32)] * 2 + [jax.ShapeDtypeStruct((1, 1024), F32)],
        compiler_params=_params("arbitrary"))(o[0], o[1], proj, gain, dmix)


def _l1_assemble(dq, dk, dv, dr, dlr, name):
    T = dr.shape[0]
    tm = min(512, T)

    def body(q0, q1, k0, k1, v0, v1, r, a, out):
        out[...] = jnp.concatenate([q0[...] + q1[...], k0[...] + k1[...], v0[...] + v1[...], r[...], a[...]],
                                   axis=1).astype(BF16)

    tok = lambda w: pl.BlockSpec((tm, w), lambda i: (i, 0))
    return pl.pallas_call(
        body, name=name, grid=(T // tm,),
        in_specs=[tok(512), tok(512), tok(512), tok(512), tok(1024), tok(1024), tok(1024), tok(LANES)],
        out_specs=pl.BlockSpec((tm, GLA_IN_PAD), lambda i: (i, 0)),
        out_shape=jax.ShapeDtypeStruct((T, GLA_IN_PAD), BF16),
        compiler_params=_params("parallel"))(dq[0], dq[1], dk[0], dk[1], dv[0], dv[1], dr, dlr)


HBM_SPEC = pl.BlockSpec(memory_space=pltpu.HBM)


def _place():
    x, y, c = lax.axis_index("x"), lax.axis_index("y"), lax.axis_index("c")
    return x, y, c


def _allgather_vmem(x_shard, name, reduce=False):
    m_per, n = x_shard.shape

    def body(x_ref, out_ref, *rest):
        if reduce:
            sum_ref, send_sems, recv_sems, local_sem = rest
        else:
            send_sems, recv_sems, local_sem = rest
        x, y, c = _place()
        me, sibling = (x, y, c), (x, y, 1 - c)
        chips = [(1 - x, y), (x, 1 - y), (1 - x, 1 - y)]

        def rows(px, py, pc):
            return out_ref.at[pl.ds((4 * px + 2 * py + pc) * m_per, m_per), :]

        def copy(k, block, to, src=None):
            return pltpu.make_async_remote_copy(
                src_ref=rows(*block) if src is None else src, dst_ref=rows(*block),
                send_sem=send_sems.at[k], recv_sem=recv_sems.at[k], device_id=to, device_id_type=MESH)

        mine = pltpu.make_async_copy(x_ref, rows(*me), local_sem)
        mine.start()
        first = [copy(0, me, sibling, src=x_ref)]
        first += [copy(1 + j, me, (*chip, c), src=x_ref) for j, chip in enumerate(chips)]
        for cp in first:
            cp.start()
        passed = [copy(4 + j, (*chip, c), sibling) for j, chip in enumerate(chips)]
        for j, chip in enumerate(chips):
            copy(1 + j, (*chip, c), me).wait_recv()
            passed[j].start()
        copy(0, sibling, me).wait_recv()
        for j, chip in enumerate(chips):
            copy(4 + j, (*chip, 1 - c), me).wait_recv()
        for cp in first + passed:
            cp.wait_send()
        mine.wait()
        if reduce:
            acc = out_ref[pl.ds(0, m_per), :]
            for j in range(1, N_DEV):
                acc = acc + out_ref[pl.ds(j * m_per, m_per), :]
            sum_ref[...] = acc

    vm = pl.BlockSpec(memory_space=pltpu.VMEM)
    out_shape = [jax.ShapeDtypeStruct((N_DEV * m_per, n), x_shard.dtype)]
    if reduce:
        out_shape.append(jax.ShapeDtypeStruct((m_per, n), x_shard.dtype))
    res = pl.pallas_call(
        body, name=name, in_specs=[vm], out_specs=[vm] * len(out_shape), out_shape=out_shape,
        scratch_shapes=[pltpu.SemaphoreType.DMA((7,)), pltpu.SemaphoreType.DMA((7,)), pltpu.SemaphoreType.DMA],
        compiler_params=pltpu.CompilerParams(has_side_effects=True, vmem_limit_bytes=VMEM_LIMIT))(x_shard)
    return res[1] if reduce else res[0]


SEM_SPEC = pl.BlockSpec(memory_space=pltpu.SEMAPHORE)
DATAFLOW_EFFECT = pltpu.SideEffectType.DATAFLOW_SIDE_EFFECTING


def _copies(plan, srcs, lands, send_sems, recv_sems):
    x, y, c = _place()
    return [pltpu.make_async_remote_copy(src_ref=s, dst_ref=d, send_sem=send_sems.at[k], recv_sem=recv_sems.at[k],
                                         device_id=dev, device_id_type=MESH)
            for k, (s, d, dev) in enumerate(plan(srcs, lands, x, y, c))]


def _copies_start(plan, n_copies, srcs, lands, name):
    ns, nl = len(srcs), len(lands)

    def body(*refs):
        send_sems, recv_sems = refs[ns + nl], refs[ns + nl + 1]
        for cp in _copies(plan, refs[:ns], refs[ns:ns + nl], send_sems, recv_sems):
            cp.start()
        refs[-1][...] = jnp.zeros_like(refs[-1])

    arrays = list(srcs) + list(lands)
    res = pl.pallas_call(
        body, name=name,
        in_specs=[HBM_SPEC] * (ns + nl),
        out_specs=tuple([SEM_SPEC, SEM_SPEC] + [HBM_SPEC] * (ns + nl) + [pl.BlockSpec(memory_space=pltpu.VMEM)]),
        out_shape=tuple([pltpu.SemaphoreType.DMA((n_copies,)), pltpu.SemaphoreType.DMA((n_copies,))]
                        + [pltpu.HBM(a.shape, a.dtype) for a in arrays]
                        + [jax.ShapeDtypeStruct((SUBLANES, LANES), F32)]),
        input_output_aliases={i: 2 + i for i in range(ns + nl)},
        compiler_params=pltpu.CompilerParams(has_side_effects=DATAFLOW_EFFECT),
    )(*[pltpu.with_memory_space_constraint(a, pltpu.HBM) for a in arrays])
    return res[0], res[1], list(res[2:2 + ns]), list(res[2 + ns:2 + ns + nl]), res[-1]


def _copies_wait(plan, started, after, name):
    send_sems, recv_sems, srcs, lands, _ = started
    ns, nl = len(srcs), len(lands)

    def body(*refs):
        for cp in _copies(plan, refs[:ns], refs[ns:ns + nl], refs[ns + nl], refs[ns + nl + 1]):
            cp.wait_send()
            cp.wait_recv()

    arrays = list(srcs) + list(lands)
    res = pl.pallas_call(
        body, name=name,
        in_specs=[HBM_SPEC] * (ns + nl) + [SEM_SPEC, SEM_SPEC, pl.BlockSpec(memory_space=pl.ANY)],
        out_specs=tuple([HBM_SPEC] * (ns + nl)),
        out_shape=tuple(pltpu.HBM(a.shape, a.dtype) for a in arrays),
        input_output_aliases={i: i for i in range(ns + nl)},
        compiler_params=pltpu.CompilerParams(has_side_effects=DATAFLOW_EFFECT),
    )(*arrays, send_sems, recv_sems, after)
    return list(res[:ns]), list(res[ns:])


def _after(token, value):
    return value + token[0:1, 0:1].astype(value.dtype)


def _chips(x, y):
    return [(1 - x, y), (x, 1 - y), (1 - x, 1 - y)]


def _plan_gather_first(srcs, lands, x, y, c):
    me = 4 * x + 2 * y + c
    out = []
    for s, l in zip(srcs, lands):
        out.append((s, l.at[me], (x, y, 1 - c)))
        out += [(s, l.at[me], (*chip, c)) for chip in _chips(x, y)]
    return out


def _plan_gather_pass(srcs, lands, x, y, c):
    out = []
    for l in lands:
        for chip in _chips(x, y):
            slot = l.at[4 * chip[0] + 2 * chip[1] + c]
            out.append((slot, slot, (x, y, 1 - c)))
    return out


def _plan_grads_sibling(srcs, lands, x, y, c):
    return [(s.at[2 * q + (1 - c)], l.at[q], (x, y, 1 - c)) for s, l in zip(srcs, lands) for q in range(4)]


def _plan_grads_chips(srcs, lands, x, y, c):
    return [(s.at[2 * chip[0] + chip[1]], l.at[k], (*chip, c))
            for s, l in zip(srcs, lands) for k, chip in enumerate(_chips(x, y))]


def _landing(n_slots, like):
    return [lax.empty((n_slots,) + a.shape[1:], a.dtype) for a in like]


def _chip_partial(g, r1, place, name):
    _, R, C = g.shape
    tr = min(256, R)
    assert R % tr == 0

    def body(pl_ref, g_ref, r_ref, pb_ref, pm_ref):
        q = pl.program_id(1)
        s = g_ref[...] + r_ref[...]
        pb_ref[...] = s.astype(BF16)

        @pl.when(q == pl_ref[1])
        def _():
            pm_ref[...] = s

    grid_spec = pltpu.PrefetchScalarGridSpec(
        num_scalar_prefetch=1, grid=(R // tr, 4),
        in_specs=[pl.BlockSpec((None, tr, C), lambda r, q, p: (2 * q + p[0], r, 0)),
                  pl.BlockSpec((None, tr, C), lambda r, q, p: (q, r, 0))],
        out_specs=[pl.BlockSpec((None, tr, C), lambda r, q, p: (q, r, 0)),
                   pl.BlockSpec((tr, C), lambda r, q, p: (r, 0))])
    return pl.pallas_call(
        body, name=name, grid_spec=grid_spec,
        out_shape=[jax.ShapeDtypeStruct((4, R, C), BF16), jax.ShapeDtypeStruct((R, C), F32)],
        compiler_params=_params("parallel", "arbitrary"))(place, g, r1)


def _adamw(w, gparts, m, v, name):
    R, C = w.shape
    tr = min(256, R)
    assert R % tr == 0
    g0, g3 = gparts

    def body(w_ref, g0_ref, *rest):
        if g3 is not None:
            g3_ref, m_ref, v_ref, go, do, mo, vo = rest
        else:
            m_ref, v_ref, go, do, mo, vo = rest
        g = g0_ref[...]
        if g3 is not None:
            for k in range(3):
                g = g + g3_ref[k].astype(F32)
        wv = w_ref[...]
        mn = ADAM_B1 * m_ref[...] + (1.0 - ADAM_B1) * g
        vn = ADAM_B2 * v_ref[...] + (1.0 - ADAM_B2) * jnp.square(g)
        m_hat = mn / (1.0 - ADAM_B1 ** ADAM_STEP)
        v_hat = vn / (1.0 - ADAM_B2 ** ADAM_STEP)
        go[...] = g
        do[...] = -ADAM_LR * (m_hat / (jnp.sqrt(v_hat) + ADAM_EPS) + ADAM_WD * wv)
        mo[...] = mn
        vo[...] = vn

    blk = pl.BlockSpec((tr, C), lambda i: (i, 0))
    in_specs = [blk, blk] + ([pl.BlockSpec((3, tr, C), lambda i: (0, i, 0))] if g3 is not None else []) + [blk, blk]
    args = [w, g0] + ([g3] if g3 is not None else []) + [m, v]
    return pl.pallas_call(
        body, name=name, grid=(R // tr,), in_specs=in_specs, out_specs=[blk] * 4,
        out_shape=[jax.ShapeDtypeStruct((R, C), F32)] * 4,
        compiler_params=_params("parallel"))(*args)


def _adamw_layers(w, parts, m, v, name):
    _, R, C = w.shape
    tr = min(256, R)
    assert R % tr == 0

    def body(w_ref, p0, r0, p1, r1, m_ref, v_ref, go, do, mo, vo):
        gs = []
        for p, r in ((p0, r0), (p1, r1)):
            g = p[...]
            for k in range(3):
                g = g + r[k].astype(F32)
            gs.append(g)
        g = jnp.where(pl.program_id(0) == 0, gs[0], gs[1])
        mn = ADAM_B1 * m_ref[...] + (1.0 - ADAM_B1) * g
        vn = ADAM_B2 * v_ref[...] + (1.0 - ADAM_B2) * jnp.square(g)
        m_hat = mn / (1.0 - ADAM_B1 ** ADAM_STEP)
        v_hat = vn / (1.0 - ADAM_B2 ** ADAM_STEP)
        go[...] = g
        do[...] = -ADAM_LR * (m_hat / (jnp.sqrt(v_hat) + ADAM_EPS) + ADAM_WD * w_ref[...])
        mo[...] = mn
        vo[...] = vn

    lay = pl.BlockSpec((None, tr, C), lambda l, i: (l, i, 0))
    one = pl.BlockSpec((tr, C), lambda l, i: (i, 0))
    three = pl.BlockSpec((3, tr, C), lambda l, i: (0, i, 0))
    return pl.pallas_call(
        body, name=name, grid=(2, R // tr), in_specs=[lay, one, three, one, three, lay, lay],
        out_specs=[lay] * 4, out_shape=[jax.ShapeDtypeStruct((2, R, C), F32)] * 4,
        compiler_params=_params("parallel", "parallel"))(w, parts[0][0], parts[0][1], parts[1][0], parts[1][1], m, v)


SMALL_SHARDED = ("rg_conv_w", "rg_b_a", "rg_b_x", "rg_lambda", "gla_w_gate_up", "gla_b_gate", "gla_norm")
SMALL_REPLICATED = ("norm_mix", "norm_mlp", "norm_final", "rg_conv_b", "rg_w_a", "rg_w_x", "hg_lb_logits", "hg_norm")
WEIGHT_NAMES = ("norm_mix", "norm_mlp", "norm_final", "mlp_w1", "mlp_w2", "ab_w_in", "ab_w_out", "rg_conv_w",
                "rg_conv_b", "rg_w_a", "rg_b_a", "rg_w_x", "rg_b_x", "rg_lambda", "hg_lb_logits", "hg_norm",
                "gla_w_in", "gla_w_out", "gla_w_gate_up", "gla_b_gate", "gla_norm")


def _rows128(a):
    return a.reshape(-1, LANES)


def _part_rows(a):
    return -(-(a.size // LANES) // SUBLANES) * SUBLANES


def _pack_rows(arrays, pad_to=SUBLANES):
    parts = [jnp.pad(_rows128(a), ((0, _part_rows(a) - a.size // LANES), (0, 0))) for a in arrays]
    total = sum(p.shape[0] for p in parts)
    extra = (-total) % pad_to
    if extra:
        parts.append(jnp.zeros((extra, LANES), parts[0].dtype))
    return jnp.concatenate(parts, axis=0)


def _unshard_last(g, shape_local):
    nd = len(shape_local)
    t = g.reshape((N_DEV,) + tuple(shape_local))
    t = jnp.moveaxis(t, 0, nd - 1)
    return t.reshape(tuple(shape_local[:-1]) + (N_DEV * shape_local[-1],))


def _block_diag(w):
    eye = jnp.eye(8, dtype=w.dtype)
    return (w[:, :, :, None, :] * eye[None, :, None, :, None]).reshape(2, RG_W, RG_W)


def _block_diag_extract(dw):
    t = dw.reshape(2, 8, 64, 8, 64)
    return jnp.moveaxis(jnp.diagonal(t, axis1=1, axis2=3), -1, 1)


def kernel(x, norm_mix, norm_mlp, norm_final, mlp_w1, mlp_w2, ab_w_in, ab_w_out, rg_conv_w, rg_conv_b, rg_w_a, rg_b_a, rg_w_x, rg_b_x, rg_lambda, hg_lb_logits, hg_norm, gla_w_in, gla_w_out, gla_w_gate_up, gla_b_gate, gla_norm, loss_target, m_norm_mix, m_norm_mlp, m_norm_final, m_mlp_w1, m_mlp_w2, m_ab_w_in, m_ab_w_out, m_rg_conv_w, m_rg_conv_b, m_rg_w_a, m_rg_b_a, m_rg_w_x, m_rg_b_x, m_rg_lambda, m_hg_lb_logits, m_hg_norm, m_gla_w_in, m_gla_w_out, m_gla_w_gate_up, m_gla_b_gate, m_gla_norm, v_norm_mix, v_norm_mlp, v_norm_final, v_mlp_w1, v_mlp_w2, v_ab_w_in, v_ab_w_out, v_rg_conv_w, v_rg_conv_b, v_rg_w_a, v_rg_b_a, v_rg_w_x, v_rg_b_x, v_rg_lambda, v_hg_lb_logits, v_hg_norm, v_gla_w_in, v_gla_w_out, v_gla_w_gate_up, v_gla_b_gate, v_gla_norm):
    w_loc = dict(norm_mix=norm_mix, norm_mlp=norm_mlp, norm_final=norm_final, mlp_w1=mlp_w1, mlp_w2=mlp_w2,
                 ab_w_in=ab_w_in, ab_w_out=ab_w_out, rg_conv_w=rg_conv_w, rg_conv_b=rg_conv_b, rg_w_a=rg_w_a,
                 rg_b_a=rg_b_a, rg_w_x=rg_w_x, rg_b_x=rg_b_x, rg_lambda=rg_lambda, hg_lb_logits=hg_lb_logits,
                 hg_norm=hg_norm, gla_w_in=gla_w_in, gla_w_out=gla_w_out, gla_w_gate_up=gla_w_gate_up,
                 gla_b_gate=gla_b_gate, gla_norm=gla_norm)
    m_loc = dict(norm_mix=m_norm_mix, norm_mlp=m_norm_mlp, norm_final=m_norm_final, mlp_w1=m_mlp_w1,
                 mlp_w2=m_mlp_w2, ab_w_in=m_ab_w_in, ab_w_out=m_ab_w_out, rg_conv_w=m_rg_conv_w,
                 rg_conv_b=m_rg_conv_b, rg_w_a=m_rg_w_a, rg_b_a=m_rg_b_a, rg_w_x=m_rg_w_x, rg_b_x=m_rg_b_x,
                 rg_lambda=m_rg_lambda, hg_lb_logits=m_hg_lb_logits, hg_norm=m_hg_norm, gla_w_in=m_gla_w_in,
                 gla_w_out=m_gla_w_out, gla_w_gate_up=m_gla_w_gate_up, gla_b_gate=m_gla_b_gate,
                 gla_norm=m_gla_norm)
    v_loc = dict(norm_mix=v_norm_mix, norm_mlp=v_norm_mlp, norm_final=v_norm_final, mlp_w1=v_mlp_w1,
                 mlp_w2=v_mlp_w2, ab_w_in=v_ab_w_in, ab_w_out=v_ab_w_out, rg_conv_w=v_rg_conv_w,
                 rg_conv_b=v_rg_conv_b, rg_w_a=v_rg_w_a, rg_b_a=v_rg_b_a, rg_w_x=v_rg_w_x, rg_b_x=v_rg_b_x,
                 rg_lambda=v_rg_lambda, hg_lb_logits=v_hg_lb_logits, hg_norm=v_hg_norm, gla_w_in=v_gla_w_in,
                 gla_w_out=v_gla_w_out, gla_w_gate_up=v_gla_w_gate_up, gla_b_gate=v_gla_b_gate,
                 gla_norm=v_gla_norm)

    T = x.shape[1]
    h0 = x.reshape(T, D_MODEL)
    target = loss_target.reshape(T, D_MODEL)
    ax, ay, ac = lax.axis_index("x"), lax.axis_index("y"), lax.axis_index("c")
    dev = 4 * ax + 2 * ay + ac
    place = jnp.stack([ac, 2 * ax + ay]).astype(jnp.int32)

    abin_shard = ab_w_in[0].astype(BF16)
    first_started = _copies_start(_plan_gather_first, 4, [abin_shard], _landing(N_DEV, [abin_shard[None]]),
                                  "ag_first_start")
    rest_shards = [mlp_w1[0].astype(BF16), mlp_w2[0].astype(BF16), gla_w_in[0].astype(BF16),
                   gla_w_out[0].astype(BF16), mlp_w1[1].astype(BF16), mlp_w2[1].astype(BF16),
                   _after(first_started[4], ab_w_out[0].astype(BF16))]
    ag_started = _copies_start(_plan_gather_first, 4 * len(rest_shards), rest_shards,
                               _landing(N_DEV, [s[None] for s in rest_shards]), "ag_rest_start")

    small_local = [w_loc[n] for n in SMALL_SHARDED]
    small_g = _allgather_vmem(_pack_rows(small_local, 8), "ag_small")
    small_g = small_g.reshape(N_DEV, -1, LANES)
    full = {}
    off = 0
    for n, a in zip(SMALL_SHARDED, small_local):
        full[n] = _unshard_last(small_g[:, off:off + a.size // LANES].reshape(N_DEV, a.size), a.shape)
        off += _part_rows(a)
    conv_w = full["rg_conv_w"][0]
    b_a, b_x, lam = full["rg_b_a"][0], full["rg_b_x"][0], full["rg_lambda"][0]
    w_up, b_gate, g_norm = full["gla_w_gate_up"][0], full["gla_b_gate"][0], full["gla_norm"]

    cw8 = jnp.pad(conv_w, ((0, 4), (0, 0)))
    wbd = jnp.concatenate([_block_diag(rg_w_a[0]), _block_diag(rg_w_x[0])], axis=2).astype(BF16)
    rg_bias = jnp.concatenate([b_a, b_x], axis=1).reshape(2, 1, 2 * RG_W)
    lam3 = lam.reshape(2, 1, RG_W)
    l0, l1 = hg_lb_logits[0:1], hg_lb_logits[1:2]
    wup_pad = jnp.zeros((2, LANES, 512), F32).at[0, 0:16].set(w_up[0]).at[1, 16:32].set(w_up[1])
    bg3 = b_gate.reshape(2, 1, 512)
    nmix0, nmix1 = norm_mix[0:1], norm_mix[1:2]
    nmlp0, nmlp1 = norm_mlp[0:1], norm_mlp[1:2]
    nfin = norm_final.reshape(1, D_MODEL)

    prepared = (ag_started[4] + cw8[:, 0:LANES] + wup_pad[0, 0:SUBLANES, 0:LANES] + rg_bias[0, :, 0:LANES]
                + wbd[0, 0:SUBLANES, 0:LANES].astype(F32) + lam3[0, :, 0:LANES] + bg3[0, :, 0:LANES])
    (abin_shard,), abin_l = _copies_wait(_plan_gather_first, first_started, prepared, "ag_first_wait")
    first_pass = _copies_start(_plan_gather_pass, 3, [], abin_l, "ag_first_pass_start")
    _, (abin_g,) = _copies_wait(_plan_gather_pass, first_pass, first_pass[4], "ag_first_pass_wait")
    abin_g = lax.dynamic_update_index_in_dim(abin_g, abin_shard, dev, 0)
    wab_in = jnp.transpose(abin_g, (1, 0, 2)).reshape(D_MODEL, AB_IN)
    proj0, y0 = _norm_matmul(h0, _after(ag_started[4], nmix0), wab_in, "l0_in_proj")
    xc = _rg_conv_fwd(proj0, cw8, rg_conv_b, "rg_conv")
    hs = _rg_scan_fwd(xc, wbd, rg_bias, lam3, "rg_scan")
    o_hg, s_hg = _hg_fwd(proj0, l0, l1, "hg_chunks")
    both_done = hs[0][0:SUBLANES, 0:LANES] + o_hg[0][0:SUBLANES, 0:LANES]
    rest_shards, rest_lands = _copies_wait(_plan_gather_first, ag_started, both_done, "ag_rest_wait")
    pass_started = _copies_start(_plan_gather_pass, 3 * len(rest_lands), [], rest_lands, "ag_pass_start")
    mixin0 = _l0_combine_fwd(hs, proj0, o_hg, _after(pass_started[4], hg_norm), "l0_combine")
    _, rest_g = _copies_wait(_plan_gather_pass, pass_started, mixin0, "ag_pass_wait")
    rest_g = [lax.dynamic_update_index_in_dim(g, s, dev, 0) for g, s in zip(rest_g, rest_shards)]
    wab_out = rest_g[6].reshape(D_MODEL, D_MODEL)
    h1 = _matmul_res(mixin0, wab_out, h0, "l0_out_proj")
    w1g = (rest_g[0], rest_g[4])
    w2f = (rest_g[1].reshape(D_FF, D_MODEL), rest_g[5].reshape(D_FF, D_MODEL))
    wgla_in = jnp.pad(jnp.transpose(rest_g[2], (1, 0, 2)).reshape(D_MODEL, GLA_IN),
                      ((0, 0), (0, GLA_IN_PAD - GLA_IN)))
    wgla_out = rest_g[3].reshape(D_MODEL, D_MODEL)
    h2, pre0, ym0 = _mlp_fwd(h1, nmlp0, w1g[0], w2f[0], "mlp0")
    proj1, y1 = _norm_matmul(h2, nmix1, wgla_in, "l1_in_proj")
    z_gate, lr_b = _gate_logits(proj1, wup_pad, bg3, "gla_gate_logits")
    o_gla, s_gla = _gla_fwd(proj1, z_gate, "gla_chunks")
    mixin1 = _l1_combine_fwd(o_gla, proj1, g_norm, "l1_combine")
    h3 = _matmul_res(mixin1, wgla_out, h2, "l1_out_proj")
    h4, pre1, ym1 = _mlp_fwd(h3, nmlp1, w1g[1], w2f[1], "mlp1")
    loss_blk, dh4, dh4b, d_nfin = _final_loss(h4, nfin, target, "final_loss")

    dh3, dh3b, dpre1, act1, d_nmlp1 = _mlp_bwd(dh4, dh4b, h3, nmlp1, pre1, w1g[1], w2f[1], "mlp1_bwd")
    g_w1_1 = _wgrad(ym1, dpre1, 512, "mlp1_dw1", sharded_cols=True)
    g_w2_1 = _wgrad(act1, dh4b, 1024, "mlp1_dw2")
    dmixin1 = _dgrad(dh3b, wgla_out, "l1_out_dgrad")
    g_gla_out = _wgrad(mixin1, dh3b, 1024, "l1_out_dw")
    do_gla, dr, d_gnorm = _l1_combine_bwd(o_gla, proj1, g_norm, dmixin1, "l1_combine_bwd")
    dq1, dk1, dv1, dz_gate = _gla_bwd(proj1, z_gate, s_gla, do_gla, "gla_chunks_bwd")
    dlr1, d_bg, dz_b = _gate_logits_bwd(dz_gate, wup_pad, "gla_gate_logits_bwd")
    d_wup = [_wgrad(lr_b, dz_b[d], 512, "gla_gate_dw%d" % d) for d in range(2)]
    dproj1 = _l1_assemble(dq1, dk1, dv1, dr, dlr1, "l1_assemble")
    dh2, dh2b, d_nmix1 = _dgrad_norm(dproj1, wgla_in, h2, nmix1, dh3, "l1_in_dgrad")
    g_gla_in = _wgrad(y1, dproj1, 640, "l1_in_dw")

    def reduce_start(grads, tag):
        return _copies_start(_plan_grads_sibling, 4 * len(grads), grads, _landing(4, grads), "rs_%s_d2d_start" % tag)

    def reduce_mid(started, after, tag):
        grads, got = _copies_wait(_plan_grads_sibling, started, after, "rs_%s_d2d_wait" % tag)
        parts = [_chip_partial(g, r, place, "rs_%s_partial%d" % (tag, a)) for a, (g, r) in enumerate(zip(grads, got))]
        pb = [p[0] for p in parts]
        return _copies_start(_plan_grads_chips, 3 * len(pb), pb, _landing(3, pb), "rs_%s_ici_start" % tag), \
            [p[1] for p in parts]

    def reduce_end(started, mine, after, tag):
        _, got = _copies_wait(_plan_grads_chips, started, after, "rs_%s_ici_wait" % tag)
        return list(zip(mine, got))

    slots_l1 = [g_w1_1, g_w2_1.reshape(N_DEV, 512, D_MODEL),
                jnp.transpose(g_gla_in[:, :GLA_IN].reshape(D_MODEL, N_DEV, GLA_IN // N_DEV), (1, 0, 2)),
                g_gla_out.reshape(N_DEV, 128, D_MODEL)]
    ra_d2d = reduce_start(slots_l1, "l1")

    dh1, dh1b, dpre0, act0, d_nmlp0 = _mlp_bwd(dh2, dh2b, h1, _after(ra_d2d[4], nmlp0), pre0, w1g[0], w2f[0],
                                               "mlp0_bwd")
    g_w1_0 = _wgrad(ym0, dpre0, 512, "mlp0_dw1", sharded_cols=True)
    g_w2_0 = _wgrad(act0, dh2b, 1024, "mlp0_dw2")
    ra_ici, ra_mine = reduce_mid(ra_d2d, g_w2_0, "l1")
    rb_d2d = reduce_start([g_w1_0, g_w2_0.reshape(N_DEV, 512, D_MODEL)], "mlp0")
    dmixin0 = _dgrad(dh1b, wab_out, "l0_out_dgrad")
    g_ab_out = _wgrad(mixin0, dh1b, 1024, "l0_out_dw")
    dho, dga, do_hg, dg_gate, d_hgnorm = _l0_combine_bwd(
        hs, proj0, o_hg, _after(rb_d2d[4], _after(ra_ici[4], hg_norm)), dmixin0, "l0_combine_bwd")
    dxc, d_wbd, d_rgb, d_lam = _rg_scan_bwd(xc, wbd, rg_bias, lam3, hs, dho, "rg_scan_bwd")
    dxa, d_cw8, d_cb = _rg_conv_bwd(dxc, proj0, cw8, "rg_conv_bwd")
    dq0, df0, dv0, d_l0, d_l1 = _hg_bwd(proj0, l0, l1, s_hg, do_hg, "hg_chunks_bwd")
    rb_ici, rb_mine = reduce_mid(rb_d2d, d_l0, "mlp0")
    dproj0 = _l0_assemble(dxa, dga, dq0, df0, dv0, dg_gate, "l0_assemble")
    g_ab_in = _wgrad(y0, dproj0, 512, "l0_in_dw")
    rc_d2d = reduce_start([jnp.transpose(g_ab_in.reshape(D_MODEL, N_DEV, AB_IN // N_DEV), (1, 0, 2)),
                           g_ab_out.reshape(N_DEV, 128, D_MODEL)], "ab")
    dx, _, d_nmix0 = _dgrad_norm(dproj0, wab_in, h0, _after(rc_d2d[4], _after(rb_ici[4], nmix0)), dh1,
                                 "l0_in_dgrad")
    rc_ici, rc_mine = reduce_mid(rc_d2d, d_nmix0, "ab")

    pieces_l1 = reduce_end(ra_ici, ra_mine, rc_ici[4], "l1")
    res_gla_in = _adamw(gla_w_in[0], pieces_l1[2], m_gla_w_in[0], v_gla_w_in[0], "adamw_gla_in")
    res_gla_out = _adamw(gla_w_out[0], pieces_l1[3], m_gla_w_out[0], v_gla_w_out[0], "adamw_gla_out")
    pieces_mlp0 = reduce_end(rb_ici, rb_mine, res_gla_out[0], "mlp0")
    res_w1 = _adamw_layers(mlp_w1, (pieces_mlp0[0], pieces_l1[0]), m_mlp_w1, v_mlp_w1, "adamw_mlp_w1")
    res_w2 = _adamw_layers(mlp_w2, (pieces_mlp0[1], pieces_l1[1]), m_mlp_w2, v_mlp_w2, "adamw_mlp_w2")
    res = {"mlp_w1": tuple(res_w1), "mlp_w2": tuple(res_w2),
           "gla_w_in": tuple(res_gla_in[k][None] for k in range(4)),
           "gla_w_out": tuple(res_gla_out[k][None] for k in range(4))}

    d_wa = _block_diag_extract(d_wbd[:, :, :RG_W])[None]
    d_wx = _block_diag_extract(d_wbd[:, :, RG_W:])[None]
    small_full = {
        "norm_mix": jnp.concatenate([d_nmix0, d_nmix1], axis=0), "norm_mlp": jnp.concatenate([d_nmlp0, d_nmlp1], axis=0),
        "norm_final": d_nfin.reshape(D_MODEL), "rg_conv_b": d_cb, "rg_w_a": d_wa, "rg_w_x": d_wx,
        "hg_lb_logits": jnp.concatenate([d_l0[0] + d_l0[1], d_l1[0] + d_l1[1]], axis=0), "hg_norm": d_hgnorm,
        "rg_conv_w": d_cw8[0:4][None], "rg_b_a": d_rgb[:, 0, :RG_W][None], "rg_b_x": d_rgb[:, 0, RG_W:][None],
        "rg_lambda": d_lam[:, 0, :][None],
        "gla_w_gate_up": jnp.stack([d_wup[0][0:16], d_wup[1][16:32]])[None], "gla_b_gate": d_bg[:, 0, :][None],
        "gla_norm": d_gnorm}
    small_names = SMALL_REPLICATED + SMALL_SHARDED
    packed = _pack_rows([loss_blk] + [small_full[n] for n in small_names], 8)
    summed = _allgather_vmem(packed, "ar_small", reduce=True)
    loss = summed[0, 0]
    g_small = {}
    off = SUBLANES
    for n in small_names:
        a = small_full[n]
        gfull = summed[off:off + a.size // LANES].reshape(a.shape)
        off += _part_rows(a)
        if n in SMALL_SHARDED:
            loc = w_loc[n].shape[-1]
            gfull = lax.dynamic_slice_in_dim(gfull, dev * loc, loc, axis=gfull.ndim - 1)
        g_small[n] = gfull
    sw = _pack_rows([w_loc[n] for n in small_names], 256)
    sg = _pack_rows([g_small[n] for n in small_names], 256)
    sm = _pack_rows([m_loc[n] for n in small_names], 256)
    sv = _pack_rows([v_loc[n] for n in small_names], 256)
    small_res = _adamw(sw, (sg, None), sm, sv, "adamw_small")
    others_done = (res_w1[1][0, 0:SUBLANES, 0:LANES] + res_w2[1][0, 0:SUBLANES, 0:LANES]
                   + res_gla_in[1][0:SUBLANES, 0:LANES] + small_res[1][0:SUBLANES, 0:LANES])
    pieces_ab = reduce_end(rc_ici, rc_mine, others_done, "ab")
    res_ab_in = _adamw(ab_w_in[0], pieces_ab[0], m_ab_w_in[0], v_ab_w_in[0], "adamw_ab_in")
    res_ab_out = _adamw(ab_w_out[0], pieces_ab[1], m_ab_w_out[0], v_ab_w_out[0], "adamw_ab_out")
    res["ab_w_in"] = tuple(res_ab_in[k][None] for k in range(4))
    res["ab_w_out"] = tuple(res_ab_out[k][None] for k in range(4))
    off = 0
    for n in small_names:
        a = w_loc[n]
        nr = a.size // LANES
        res[n] = tuple(small_res[k][off:off + nr].reshape(a.shape) for k in range(4))
        off += _part_rows(a)

    grad_x = dx.reshape(1, T, D_MODEL)
    out = [loss, grad_x]
    for k in range(4):
        out += [res[n][k] for n in WEIGHT_NAMES]
    return tuple(out)
```

```python
import jax
import jax.numpy as jnp
from jax import lax
from jax.experimental import pallas as pl
from jax.experimental.pallas import tpu as pltpu

F32, BF16 = jnp.float32, jnp.bfloat16
HI = lax.Precision.HIGHEST
MESH = pl.DeviceIdType.MESH

D_MODEL = 1024
D_FF = 4096
RG_W = 512
HG_W = 512
CHUNK = 64
EPS = 1e-6
RG_C = 8.0
AB_IN = 3584
GLA_IN = 3104
GLA_IN_PAD = 3200
N_DEV = 8
LANES = 128
SUBLANES = 8
VMEM_LIMIT = 48 * 1024 * 1024

ADAM_LR, ADAM_B1, ADAM_B2, ADAM_EPS, ADAM_WD, ADAM_STEP = 0.001, 0.9, 0.999, 1e-08, 0.01, 10


def _params(*sem):
    return pltpu.CompilerParams(dimension_semantics=sem, vmem_limit_bytes=VMEM_LIMIT)


def _dg(a, b, ca, cb):
    return lax.dot_general(a.astype(BF16), b.astype(BF16), (((ca,), (cb,)), ((), ())),
                           preferred_element_type=F32)


@jax.custom_vjp
def _mm_nn(a, b):
    return _dg(a, b, 1, 0)


_mm_nn.defvjp(lambda a, b: (_dg(a, b, 1, 0), (a, b)),
              lambda res, g: (_dg(g, res[1], 1, 1), _dg(res[0], g, 0, 0)))


@jax.custom_vjp
def _mm_nt(a, b):
    return _dg(a, b, 1, 1)


_mm_nt.defvjp(lambda a, b: (_dg(a, b, 1, 1), (a, b)),
              lambda res, g: (_dg(g, res[1], 1, 0), _dg(g, res[0], 0, 0)))


@jax.custom_vjp
def _mm_tn(a, b):
    return _dg(a, b, 0, 0)


_mm_tn.defvjp(lambda a, b: (_dg(a, b, 0, 0), (a, b)),
              lambda res, g: (_dg(res[1], g, 1, 1), _dg(res[0], g, 1, 0)))


@jax.custom_vjp
def _cum(tri, tri_t, x):
    return jnp.dot(tri, x, precision=HI, preferred_element_type=F32)


_cum.defvjp(lambda tri, tri_t, x: (jnp.dot(tri, x, precision=HI, preferred_element_type=F32), (tri, tri_t)),
            lambda res, g: (jnp.zeros_like(res[0]), jnp.zeros_like(res[1]),
                            jnp.dot(res[1], g, precision=HI, preferred_element_type=F32)))


def _sig(x):
    return 1.0 / (1.0 + jnp.exp(-x))


def _gelu(x):
    return 0.5 * x * (1.0 + jnp.tanh(0.7978845608028654 * (x + 0.044715 * (x * x * x))))


def _softplus(z):
    return jnp.maximum(z, 0.0) + jnp.log(1.0 + jnp.exp(-jnp.abs(z)))


def _rms(x):
    return lax.rsqrt(jnp.mean(x * x, axis=-1, keepdims=True) + EPS)


def _rmsnorm_bwd(x, gain, dy):
    r = _rms(x)
    xh = x * r
    dgain = jnp.sum(dy * xh, axis=0, keepdims=True)
    dxh = dy * gain
    dx = r * (dxh - xh * jnp.mean(dxh * xh, axis=-1, keepdims=True))
    return dx, dgain


def _headnorm(o, gain, n_heads, hd):
    parts = []
    for h in range(n_heads):
        oh = o[:, h * hd:(h + 1) * hd]
        parts.append(oh * _rms(oh))
    return jnp.concatenate(parts, axis=1) * gain


def _tri_consts(d):
    row = lax.broadcasted_iota(jnp.int32, (CHUNK, CHUNK), 0)
    col = lax.broadcasted_iota(jnp.int32, (CHUNK, CHUNK), 1)
    ge = (row >= col).astype(F32)
    le = (row <= col).astype(F32)
    r1 = lax.broadcasted_iota(jnp.int32, (CHUNK, 1), 0)
    if d == 0:
        return ge, le, (r1 <= CHUNK // 2).astype(F32)
    return le, ge, (r1 >= CHUNK // 2 - 1).astype(F32)


def _chunk_core(qh, k, v, logf, st_prev, tri, tri_t, mref, n_heads, dk, dv):
    cum = _cum(tri, tri_t, logf)
    ref = jnp.sum(logf * mref, axis=0, keepdims=True)
    last = jnp.sum(logf, axis=0, keepdims=True)
    q_in = qh * jnp.exp(cum - ref)
    k_in = k * jnp.exp(ref - cum)
    k_st = k * jnp.exp(last - cum)
    q_dec = qh * jnp.exp(cum)
    decay = jnp.exp(last)
    outs, sts = [], []
    for h in range(n_heads):
        sk = slice(h * dk, (h + 1) * dk)
        sv = slice(h * dv, (h + 1) * dv)
        sc = _mm_nt(q_in[:, sk], k_in[:, sk]) * tri
        o = _mm_nn(sc, v[:, sv]) + _mm_nt(q_dec[:, sk], st_prev[h])
        sts.append(st_prev[h] * decay[:, sk] + _mm_tn(v[:, sv], k_st[:, sk]))
        outs.append(o)
    return jnp.concatenate(outs, axis=1), tuple(sts)


def _hg_chunk(q, f, v, l0, l1, st_prev, tri, tri_t, mref):
    lb = _sig(l0 - l1)
    sg = _sig(f)
    qh = q * _sig(q)
    logf = jnp.log(lb + (1.0 - lb) * sg)
    k = (1.0 - lb) * (1.0 - sg)
    return _chunk_core(qh, k, v, logf, st_prev, tri, tri_t, mref, 4, 128, 128)


def _gla_chunk(q, k, v, z, st_prev, tri, tri_t, mref):
    logf = (jnp.minimum(z, 0.0) - jnp.log(1.0 + jnp.exp(-jnp.abs(z)))) * (1.0 / 16.0)
    qh = q * (128.0 ** -0.5)
    return _chunk_core(qh, k, v, logf, st_prev, tri, tri_t, mref, 4, 128, 256)


def _rg_gates(xc, wbd, bias, lam):
    z = _mm_nn(xc, wbd) + bias
    r = _sig(z[:, :RG_W])
    i = _sig(z[:, RG_W:])
    log_a = -RG_C * r * _softplus(-lam)
    a = jnp.exp(log_a)
    x2 = 2.0 * log_a
    neg_expm1 = jnp.where(x2 > -1e-2, -(x2 + 0.5 * x2 * x2 + x2 * x2 * x2 * (1.0 / 6.0)), 1.0 - jnp.exp(x2))
    u = jnp.sqrt(neg_expm1) * (i * xc)
    return a, u


def _l0_combine(hf, hb, ga, of, ob, g, gain):
    ya = (hf + hb) * _gelu(ga)
    yb = _headnorm(of + ob, gain, 4, 128) * (g * _sig(g))
    return jnp.concatenate([ya, yb], axis=1)


def _l1_combine(of, ob, r, gain):
    return _headnorm(of + ob, gain, 4, 256) * (r * _sig(r))


def _norm_matmul(h, gain, w, name):
    T, D = h.shape
    N = w.shape[1]
    tm = min(512, T)

    def body(h_ref, g_ref, w_ref, o_ref, y_ref):
        x = h_ref[...]
        y = (x * _rms(x) * g_ref[...]).astype(BF16)
        y_ref[...] = y
        o_ref[...] = jnp.dot(y, w_ref[...], preferred_element_type=F32)

    return pl.pallas_call(
        body, name=name, grid=(T // tm,),
        in_specs=[pl.BlockSpec((tm, D), lambda i: (i, 0)), pl.BlockSpec((1, D), lambda i: (0, 0)),
                  pl.BlockSpec((D, N), lambda i: (0, 0))],
        out_specs=[pl.BlockSpec((tm, N), lambda i: (i, 0)), pl.BlockSpec((tm, D), lambda i: (i, 0))],
        out_shape=[jax.ShapeDtypeStruct((T, N), F32), jax.ShapeDtypeStruct((T, D), BF16)],
        compiler_params=_params("parallel"))(h, gain, w)


def _matmul_res(a, w, res, name):
    T, K = a.shape
    N = w.shape[1]
    tm = min(512, T)

    def body(a_ref, w_ref, r_ref, o_ref):
        o_ref[...] = r_ref[...] + jnp.dot(a_ref[...], w_ref[...], preferred_element_type=F32)

    return pl.pallas_call(
        body, name=name, grid=(T // tm,),
        in_specs=[pl.BlockSpec((tm, K), lambda i: (i, 0)), pl.BlockSpec((K, N), lambda i: (0, 0)),
                  pl.BlockSpec((tm, N), lambda i: (i, 0))],
        out_specs=pl.BlockSpec((tm, N), lambda i: (i, 0)),
        out_shape=jax.ShapeDtypeStruct((T, N), F32),
        compiler_params=_params("parallel"))(a, w, res)


def _dgrad(dc, w, name):
    T, N = dc.shape
    K = w.shape[0]
    tm = min(512, T)

    def body(d_ref, w_ref, o_ref):
        o_ref[...] = _dg(d_ref[...], w_ref[...], 1, 1)

    return pl.pallas_call(
        body, name=name, grid=(T // tm,),
        in_specs=[pl.BlockSpec((tm, N), lambda i: (i, 0)), pl.BlockSpec((K, N), lambda i: (0, 0))],
        out_specs=pl.BlockSpec((tm, K), lambda i: (i, 0)),
        out_shape=jax.ShapeDtypeStruct((T, K), F32),
        compiler_params=_params("parallel"))(dc, w)


def _dgrad_norm(dproj, w, h, gain, dres, name):
    T, N = dproj.shape
    D = w.shape[0]
    tm = min(512, T)

    def body(dp_ref, w_ref, h_ref, g_ref, dr_ref, dh_ref, dhb_ref, dg_ref):
        @pl.when(pl.program_id(0) == 0)
        def _():
            dg_ref[...] = jnp.zeros_like(dg_ref)

        dy = _dg(dp_ref[...], w_ref[...], 1, 1)
        dx, dgain = _rmsnorm_bwd(h_ref[...], g_ref[...], dy)
        dh = dr_ref[...] + dx
        dh_ref[...] = dh
        dhb_ref[...] = dh.astype(BF16)
        dg_ref[...] += dgain

    return pl.pallas_call(
        body, name=name, grid=(T // tm,),
        in_specs=[pl.BlockSpec((tm, N), lambda i: (i, 0)), pl.BlockSpec((D, N), lambda i: (0, 0)),
                  pl.BlockSpec((tm, D), lambda i: (i, 0)), pl.BlockSpec((1, D), lambda i: (0, 0)),
                  pl.BlockSpec((tm, D), lambda i: (i, 0))],
        out_specs=[pl.BlockSpec((tm, D), lambda i: (i, 0)), pl.BlockSpec((tm, D), lambda i: (i, 0)),
                   pl.BlockSpec((1, D), lambda i: (0, 0))],
        out_shape=[jax.ShapeDtypeStruct((T, D), F32), jax.ShapeDtypeStruct((T, D), BF16),
                   jax.ShapeDtypeStruct((1, D), F32)],
        compiler_params=_params("arbitrary"))(dproj, w, h, gain, dres)


def _wgrad(a, b, tn, name, sharded_cols=False):
    T, K = a.shape
    N = b.shape[1]
    tk = min(1024, K)

    def body(a_ref, b_ref, o_ref):
        o_ref[...] = _dg(a_ref[...], b_ref[...], 0, 0)

    if sharded_cols:
        out_spec = pl.BlockSpec((None, tk, tn), lambda k, n: (n, k, 0))
        out_shape = jax.ShapeDtypeStruct((N // tn, K, tn), F32)
    else:
        out_spec = pl.BlockSpec((tk, tn), lambda k, n: (k, n))
        out_shape = jax.ShapeDtypeStruct((K, N), F32)
    return pl.pallas_call(
        body, name=name, grid=(K // tk, N // tn),
        in_specs=[pl.BlockSpec((T, tk), lambda k, n: (0, k)), pl.BlockSpec((T, tn), lambda k, n: (0, n))],
        out_specs=out_spec, out_shape=out_shape,
        compiler_params=_params("parallel", "parallel"))(a, b)


def _resident(shape):
    return pl.BlockSpec(shape, lambda i: (0,) * len(shape), pipeline_mode=pl.Buffered(1))


def _mlp_fwd(h, gain, w1g, w2, name):
    T, D = h.shape
    nf, _, tf = w1g.shape
    tm = min(512, T)

    def body(h_ref, g_ref, w1_ref, w2_ref, o_ref, pre_ref, y_ref):
        x = h_ref[...]
        y = (x * _rms(x) * g_ref[...]).astype(BF16)
        y_ref[...] = y
        acc = x
        for j in range(nf):
            cols = slice(j * tf, (j + 1) * tf)
            pre = jnp.dot(y, w1_ref[j], preferred_element_type=F32)
            pre_ref[:, cols] = pre.astype(BF16)
            act = jnp.square(jnp.maximum(pre, 0.0)).astype(BF16)
            acc = acc + jnp.dot(act, w2_ref[cols, :], preferred_element_type=F32)
        o_ref[...] = acc

    return pl.pallas_call(
        body, name=name, grid=(T // tm,),
        in_specs=[pl.BlockSpec((tm, D), lambda i: (i, 0)), pl.BlockSpec((1, D), lambda i: (0, 0)),
                  _resident(w1g.shape), _resident(w2.shape)],
        out_specs=[pl.BlockSpec((tm, D), lambda i: (i, 0)), pl.BlockSpec((tm, nf * tf), lambda i: (i, 0)),
                   pl.BlockSpec((tm, D), lambda i: (i, 0))],
        out_shape=[jax.ShapeDtypeStruct((T, D), F32), jax.ShapeDtypeStruct((T, nf * tf), BF16),
                   jax.ShapeDtypeStruct((T, D), BF16)],
        compiler_params=_params("parallel"))(h, gain, w1g, w2)


def _mlp_bwd(dout, dout_b, h, gain, pre, w1g, w2, name):
    T, D = h.shape
    nf, _, tf = w1g.shape
    tm = min(256, T)

    def body(do_ref, dob_ref, h_ref, g_ref, pre_ref, w1_ref, w2_ref, dh_ref, dhb_ref, dpre_ref, act_ref, dg_ref):
        @pl.when(pl.program_id(0) == 0)
        def _():
            dg_ref[...] = jnp.zeros_like(dg_ref)

        dob = dob_ref[...]
        dy = None
        for j in range(nf):
            cols = slice(j * tf, (j + 1) * tf)
            rp = jnp.maximum(pre_ref[:, cols].astype(F32), 0.0)
            dpre = (_dg(dob, w2_ref[cols, :], 1, 1) * (2.0 * rp)).astype(BF16)
            dpre_ref[:, cols] = dpre
            act_ref[:, cols] = (rp * rp).astype(BF16)
            part = _dg(dpre, w1_ref[j], 1, 1)
            dy = part if dy is None else dy + part
        dx, dgain = _rmsnorm_bwd(h_ref[...], g_ref[...], dy)
        dh = do_ref[...] + dx
        dh_ref[...] = dh
        dhb_ref[...] = dh.astype(BF16)
        dg_ref[...] += dgain

    tok = lambda w: pl.BlockSpec((tm, w), lambda i: (i, 0))
    return pl.pallas_call(
        body, name=name, grid=(T // tm,),
        in_specs=[tok(D), tok(D), tok(D), pl.BlockSpec((1, D), lambda i: (0, 0)), tok(nf * tf),
                  _resident(w1g.shape), _resident(w2.shape)],
        out_specs=[tok(D), tok(D), tok(nf * tf), tok(nf * tf), pl.BlockSpec((1, D), lambda i: (0, 0))],
        out_shape=[jax.ShapeDtypeStruct((T, D), F32), jax.ShapeDtypeStruct((T, D), BF16),
                   jax.ShapeDtypeStruct((T, nf * tf), BF16),
                   jax.ShapeDtypeStruct((T, nf * tf), BF16), jax.ShapeDtypeStruct((1, D), F32)],
        compiler_params=_params("arbitrary"))(dout, dout_b, h, gain, pre, w1g, w2)


def _final_loss(h, gain, target, name):
    T, D = h.shape
    tm = min(512, T)

    def body(h_ref, g_ref, t_ref, l_ref, dh_ref, dhb_ref, dg_ref):
        @pl.when(pl.program_id(0) == 0)
        def _():
            l_ref[...] = jnp.zeros_like(l_ref)
            dg_ref[...] = jnp.zeros_like(dg_ref)

        x = h_ref[...]
        err = x * _rms(x) * g_ref[...] - t_ref[...]
        l_ref[...] += 0.5 * jnp.sum(jnp.mean(err * err, axis=-1, keepdims=True), axis=0, keepdims=True)
        dx, dgain = _rmsnorm_bwd(x, g_ref[...], err * (1.0 / D))
        dh_ref[...] = dx
        dhb_ref[...] = dx.astype(BF16)
        dg_ref[...] += dgain

    return pl.pallas_call(
        body, name=name, grid=(T // tm,),
        in_specs=[pl.BlockSpec((tm, D), lambda i: (i, 0)), pl.BlockSpec((1, D), lambda i: (0, 0)),
                  pl.BlockSpec((tm, D), lambda i: (i, 0))],
        out_specs=[pl.BlockSpec((SUBLANES, LANES), lambda i: (0, 0)), pl.BlockSpec((tm, D), lambda i: (i, 0)),
                   pl.BlockSpec((tm, D), lambda i: (i, 0)), pl.BlockSpec((1, D), lambda i: (0, 0))],
        out_shape=[jax.ShapeDtypeStruct((SUBLANES, LANES), F32), jax.ShapeDtypeStruct((T, D), F32),
                   jax.ShapeDtypeStruct((T, D), BF16), jax.ShapeDtypeStruct((1, D), F32)],
        compiler_params=_params("arbitrary"))(h, gain, target)


def _halo_specs(tm, T, width, col, tile=lambda i: i):
    r8 = tm // SUBLANES
    nb8 = T // SUBLANES
    return [pl.BlockSpec((tm, width), lambda i: (tile(i), col)),
            pl.BlockSpec((SUBLANES, width), lambda i: (jnp.maximum(tile(i) * r8 - 1, 0), col)),
            pl.BlockSpec((SUBLANES, width), lambda i: (jnp.minimum((tile(i) + 1) * r8, nb8 - 1), col))]


def _ext(cur, prev, nxt, has_prev, has_next):
    return jnp.concatenate([jnp.where(has_prev, prev, 0.0), cur, jnp.where(has_next, nxt, 0.0)], axis=0)


def _shifted(ext, offset, tm):
    n = ext.shape[0]
    sh = (-offset) % n
    r = ext if sh == 0 else pltpu.roll(ext, sh, 0)
    return r[SUBLANES:SUBLANES + tm]


def _rg_conv_fwd(proj, cw8, cb, name):
    T = proj.shape[0]
    tm = min(512, T)
    nT = T // tm

    def body(cur_ref, prev_ref, next_ref, w_ref, b_ref, o_ref):
        i = pl.program_id(0)
        ext = _ext(cur_ref[...], prev_ref[...], next_ref[...], i > 0, i < nT - 1)
        acc = jnp.broadcast_to(b_ref[...], (tm, RG_W))
        for k in range(4):
            acc = acc + w_ref[k:k + 1, :] * _shifted(ext, k - 2, tm)
        o_ref[...] = acc

    return pl.pallas_call(
        body, name=name, grid=(nT,),
        in_specs=_halo_specs(tm, T, RG_W, 0) + [pl.BlockSpec((SUBLANES, RG_W), lambda i: (0, 0)),
                                                pl.BlockSpec((1, RG_W), lambda i: (0, 0))],
        out_specs=pl.BlockSpec((tm, RG_W), lambda i: (i, 0)),
        out_shape=jax.ShapeDtypeStruct((T, RG_W), F32),
        compiler_params=_params("parallel"))(proj, proj, proj, cw8, cb)


def _rg_conv_bwd(dxc, proj, cw8, name):
    T = proj.shape[0]
    tm = min(512, T)
    nT = T // tm

    def body(a0, p0, n0, a1, p1, n1, xa, xp, xn, w_ref, dxa_ref, dw_ref, db_ref):
        i = pl.program_id(0)

        @pl.when(i == 0)
        def _():
            dw_ref[...] = jnp.zeros_like(dw_ref)
            db_ref[...] = jnp.zeros_like(db_ref)

        has_p, has_n = i > 0, i < nT - 1
        cur = a0[...] + a1[...]
        dext = _ext(cur, p0[...] + p1[...], n0[...] + n1[...], has_p, has_n)
        xext = _ext(xa[...], xp[...], xn[...], has_p, has_n)
        acc = jnp.zeros((tm, RG_W), F32)
        rows = []
        for k in range(4):
            acc = acc + w_ref[k:k + 1, :] * _shifted(dext, 2 - k, tm)
            rows.append(jnp.sum(cur * _shifted(xext, k - 2, tm), axis=0, keepdims=True))
        dxa_ref[...] = acc
        dw_ref[...] += jnp.concatenate(rows + [jnp.zeros((4, RG_W), F32)], axis=0)
        db_ref[...] += jnp.sum(cur, axis=0, keepdims=True)

    return pl.pallas_call(
        body, name=name, grid=(nT,),
        in_specs=(_halo_specs(tm, T, RG_W, 0) + _halo_specs(tm, T, RG_W, 0)
                  + _halo_specs(tm, T, RG_W, 0) + [pl.BlockSpec((SUBLANES, RG_W), lambda i: (0, 0))]),
        out_specs=[pl.BlockSpec((tm, RG_W), lambda i: (i, 0)), pl.BlockSpec((SUBLANES, RG_W), lambda i: (0, 0)),
                   pl.BlockSpec((1, RG_W), lambda i: (0, 0))],
        out_shape=[jax.ShapeDtypeStruct((T, RG_W), F32), jax.ShapeDtypeStruct((SUBLANES, RG_W), F32),
                   jax.ShapeDtypeStruct((1, RG_W), F32)],
        compiler_params=_params("arbitrary"))(dxc[0], dxc[0], dxc[0], dxc[1], dxc[1], dxc[1], proj, proj, proj, cw8)


def _local_scan(a, b, ascending):
    n = a.shape[0]
    pos = jnp.bitwise_and(lax.broadcasted_iota(jnp.int32, a.shape, 0), SUBLANES - 1)
    for s in (1, 2, 4):
        sh = s if ascending else n - s
        ok = (pos >= s) if ascending else (pos < SUBLANES - s)
        a_sh, b_sh = pltpu.roll(a, sh, 0), pltpu.roll(b, sh, 0)
        b = jnp.where(ok, a * b_sh + b, b)
        a = jnp.where(ok, a * a_sh, a)
    return a, b


def _group_scan(chains, a_sc, b_sc, carry, n_groups):
    def step(g, hs):
        new = []
        for (d, out_ref, asc), h in zip(chains, hs):
            r0 = pl.multiple_of((g if asc else n_groups - 1 - g) * SUBLANES, SUBLANES)
            out_ref[pl.ds(r0, SUBLANES), :] = a_sc[d, pl.ds(r0, SUBLANES), :] * h + b_sc[d, pl.ds(r0, SUBLANES), :]
            new.append(out_ref[pl.ds(r0 + (SUBLANES - 1 if asc else 0), 1), :])
        return tuple(new)

    hs = lax.fori_loop(0, n_groups, step, tuple(carry[d, 0:1, :] for d, _, _ in chains))
    for (d, _, _), h in zip(chains, hs):
        carry[d, 0:1, :] = h


def _rg_scan_fwd(xc, wbd, bias, lam, name):
    T = xc.shape[0]
    tm = min(512, T)
    nT = T // tm

    def body(xf_ref, xb_ref, w_ref, b_ref, lam_ref, hf_ref, hb_ref, a_sc, b_sc, carry):
        @pl.when(pl.program_id(0) == 0)
        def _():
            carry[...] = jnp.zeros_like(carry)

        for d, x_ref in enumerate((xf_ref, xb_ref)):
            a, u = _rg_gates(x_ref[...], w_ref[d], b_ref[d], lam_ref[d])
            a_sc[d], b_sc[d] = _local_scan(a, u, d == 0)
        _group_scan(((0, hf_ref, True), (1, hb_ref, False)), a_sc, b_sc, carry, tm // SUBLANES)

    full = lambda a: pl.BlockSpec(a.shape, lambda i: (0,) * len(a.shape))
    res = pl.pallas_call(
        body, name=name, grid=(nT,),
        in_specs=[pl.BlockSpec((tm, RG_W), lambda i: (i, 0)), pl.BlockSpec((tm, RG_W), lambda i: (nT - 1 - i, 0)),
                  full(wbd), full(bias), full(lam)],
        out_specs=[pl.BlockSpec((tm, RG_W), lambda i: (i, 0)), pl.BlockSpec((tm, RG_W), lambda i: (nT - 1 - i, 0))],
        out_shape=[jax.ShapeDtypeStruct((T, RG_W), F32)] * 2,
        scratch_shapes=[pltpu.VMEM((2, tm, RG_W), F32), pltpu.VMEM((2, tm, RG_W), F32),
                        pltpu.VMEM((2, SUBLANES, RG_W), F32)],
        compiler_params=_params("arbitrary"))(xc, xc, wbd, bias, lam)
    return res[0], res[1]


def _rg_scan_bwd(xc, wbd, bias, lam, hs, dho, name):
    T = xc.shape[0]
    tm = min(256, T)
    nT = T // tm
    tiles = (lambda i: nT - 1 - i, lambda i: i)

    def body(xf_ref, xb_ref, w_ref, b_ref, lam_ref, hfc, hfp, hfn, hbc, hbp, hbn, dof_ref, dob_ref,
             dxf_ref, dxb_ref, dw_ref, db_ref, dlam_ref, a_sc, b_sc, y_sc, carry):
        i = pl.program_id(0)

        @pl.when(i == 0)
        def _():
            carry[...] = jnp.zeros_like(carry)
            dw_ref[...] = jnp.zeros_like(dw_ref)
            db_ref[...] = jnp.zeros_like(db_ref)
            dlam_ref[...] = jnp.zeros_like(dlam_ref)

        vjps, entering = [], []
        for d, (x_ref, do_ref) in enumerate(((xf_ref, dof_ref), (xb_ref, dob_ref))):
            (a, _), vjp = jax.vjp(_rg_gates, x_ref[...], w_ref[d].astype(F32), b_ref[d], lam_ref[d])
            vjps.append(vjp)
            entering.append(carry[d, 0:1, :])
            a_sc[d], b_sc[d] = _local_scan(a, a * do_ref[...], d == 1)
        _group_scan(((0, y_sc.at[0], False), (1, y_sc.at[1], True)), a_sc, b_sc, carry, tm // SUBLANES)

        row = lax.broadcasted_iota(jnp.int32, (tm, RG_W), 0)
        for d, (do_ref, dx_ref, hc, hp, hn, ti) in enumerate(
                ((dof_ref, dxf_ref, hfc, hfp, hfn, nT - 1 - i), (dob_ref, dxb_ref, hbc, hbp, hbn, i))):
            y = y_sc[d]
            if d == 0:
                y_next = jnp.where(row == tm - 1, entering[d], pltpu.roll(y, tm - 1, 0))
            else:
                y_next = jnp.where(row == 0, entering[d], pltpu.roll(y, 1, 0))
            dtot = do_ref[...] + y_next
            ext = _ext(hc[...], hp[...], hn[...], ti > 0, ti < nT - 1)
            hprev = _shifted(ext, -1 if d == 0 else 1, tm)
            dxc, dw, db, dlam = vjps[d]((dtot * hprev, dtot))
            dx_ref[...] = dxc
            dw_ref[d] += dw
            db_ref[d] += db
            dlam_ref[d] += dlam

    full = lambda a: pl.BlockSpec(a.shape, lambda i: (0,) * len(a.shape))
    tok = lambda d: pl.BlockSpec((tm, RG_W), lambda i: (tiles[d](i), 0))
    acc_shapes = [jax.ShapeDtypeStruct((2, RG_W, 2 * RG_W), F32), jax.ShapeDtypeStruct((2, 1, 2 * RG_W), F32),
                  jax.ShapeDtypeStruct((2, 1, RG_W), F32)]
    res = pl.pallas_call(
        body, name=name, grid=(nT,),
        in_specs=([tok(0), tok(1), full(wbd), full(bias), full(lam)]
                  + _halo_specs(tm, T, RG_W, 0, tiles[0]) + _halo_specs(tm, T, RG_W, 0, tiles[1]) + [tok(0), tok(1)]),
        out_specs=[tok(0), tok(1)] + [full(s) for s in acc_shapes],
        out_shape=[jax.ShapeDtypeStruct((T, RG_W), F32)] * 2 + acc_shapes,
        scratch_shapes=[pltpu.VMEM((2, tm, RG_W), F32), pltpu.VMEM((2, tm, RG_W), F32),
                        pltpu.VMEM((2, tm, RG_W), F32), pltpu.VMEM((2, SUBLANES, RG_W), F32)],
        compiler_params=_params("arbitrary"))(xc, xc, wbd, bias, lam, hs[0], hs[0], hs[0], hs[1], hs[1], hs[1],
                                              dho, dho)
    return (res[0], res[1]), res[2], res[3], res[4]


def _chunk_rows(n_chunks, reverse):
    up, down = (lambda c: c), (lambda c: n_chunks - 1 - c)
    return (down, up) if reverse else (up, down)


def _hg_fwd(proj, l0, l1, name):
    T = proj.shape[0]
    nC = T // CHUNK
    H, dk, dv = 4, 128, 128
    rows = _chunk_rows(nC, False)

    def body(qf, ff, vf, qb, fb, vb, l0_ref, l1_ref, of, ob, spf, spb, st):
        @pl.when(pl.program_id(0) == 0)
        def _():
            st[...] = jnp.zeros_like(st)

        for d, (q, f, v, o, sp) in enumerate(((qf, ff, vf, of, spf), (qb, fb, vb, ob, spb))):
            tri, tri_t, mref = _tri_consts(d)
            stp = tuple(st[d, h] for h in range(H))
            sp[...] = st[d]
            o_val, stn = _hg_chunk(q[...], f[...], v[...], l0_ref[...], l1_ref[...], stp, tri, tri_t, mref)
            o[...] = o_val
            for h in range(H):
                st[d, h] = stn[h]

    tok = lambda d, col: pl.BlockSpec((CHUNK, HG_W), lambda c: (rows[d](c), col))
    par = pl.BlockSpec((1, HG_W), lambda c: (0, 0))
    state = lambda d: pl.BlockSpec((None, H, dv, dk), lambda c: (rows[d](c), 0, 0, 0))
    res = pl.pallas_call(
        body, name=name, grid=(nC,),
        in_specs=[tok(0, 2), tok(0, 3), tok(0, 5), tok(1, 2), tok(1, 4), tok(1, 5), par, par],
        out_specs=[tok(0, 0), tok(1, 0), state(0), state(1)],
        out_shape=[jax.ShapeDtypeStruct((T, H * dv), F32)] * 2 + [jax.ShapeDtypeStruct((nC, H, dv, dk), F32)] * 2,
        scratch_shapes=[pltpu.VMEM((2, H, dv, dk), F32)],
        compiler_params=_params("arbitrary"))(proj, proj, proj, proj, proj, proj, l0, l1)
    return (res[0], res[1]), (res[2], res[3])


def _hg_bwd(proj, l0, l1, sprev, do, name):
    T = proj.shape[0]
    nC = T // CHUNK
    H, dk, dv = 4, 128, 128
    rows = _chunk_rows(nC, True)

    def body(qf, ff, vf, qb, fb, vb, l0_ref, l1_ref, spf, spb, dof, dob,
             dqf, dff, dvf, dqb, dfb, dvb, dl0_ref, dl1_ref, dst):
        @pl.when(pl.program_id(0) == 0)
        def _():
            dst[...] = jnp.zeros_like(dst)
            dl0_ref[...] = jnp.zeros_like(dl0_ref)
            dl1_ref[...] = jnp.zeros_like(dl1_ref)

        for d, (q, f, v, sp, do_ref, dq_ref, df_ref, dv_ref) in enumerate(
                ((qf, ff, vf, spf, dof, dqf, dff, dvf), (qb, fb, vb, spb, dob, dqb, dfb, dvb))):
            tri, tri_t, mref = _tri_consts(d)
            fn = lambda q_, f_, v_, a0, a1, stp: _hg_chunk(q_, f_, v_, a0, a1, stp, tri, tri_t, mref)
            stp = tuple(sp[h] for h in range(H))
            _, vjp = jax.vjp(fn, q[...], f[...], v[...], l0_ref[...], l1_ref[...], stp)
            dq, df, dvv, dl0, dl1, dstp = vjp((do_ref[...], tuple(dst[d, h] for h in range(H))))
            dq_ref[...] = dq
            df_ref[...] = df
            dv_ref[...] = dvv
            dl0_ref[d] += dl0
            dl1_ref[d] += dl1
            for h in range(H):
                dst[d, h] = dstp[h]

    tok = lambda d, col: pl.BlockSpec((CHUNK, HG_W), lambda c: (rows[d](c), col))
    par = pl.BlockSpec((1, HG_W), lambda c: (0, 0))
    acc = pl.BlockSpec((2, 1, HG_W), lambda c: (0, 0, 0))
    state = lambda d: pl.BlockSpec((None, H, dv, dk), lambda c: (rows[d](c), 0, 0, 0))
    res = pl.pallas_call(
        body, name=name, grid=(nC,),
        in_specs=[tok(0, 2), tok(0, 3), tok(0, 5), tok(1, 2), tok(1, 4), tok(1, 5), par, par,
                  state(0), state(1), tok(0, 0), tok(1, 0)],
        out_specs=[tok(0, 0)] * 3 + [tok(1, 0)] * 3 + [acc, acc],
        out_shape=[jax.ShapeDtypeStruct((T, HG_W), F32)] * 6 + [jax.ShapeDtypeStruct((2, 1, HG_W), F32)] * 2,
        scratch_shapes=[pltpu.VMEM((2, H, dv, dk), F32)],
        compiler_params=_params("arbitrary"))(proj, proj, proj, proj, proj, proj, l0, l1, sprev[0], sprev[1], do, do)
    return (res[0], res[3]), (res[1], res[4]), (res[2], res[5]), res[6], res[7]


def _gate_logits(proj, wup, bg, name):
    T = proj.shape[0]
    tm = min(512, T)

    def body(lr_ref, w_ref, b_ref, z_ref, lrb_ref):
        lr = lr_ref[...].astype(BF16)
        lrb_ref[...] = lr
        for d in range(2):
            z_ref[d] = _dg(lr, w_ref[d], 1, 0) + b_ref[d]

    return pl.pallas_call(
        body, name=name, grid=(T // tm,),
        in_specs=[pl.BlockSpec((tm, LANES), lambda i: (i, 24)), pl.BlockSpec((2, LANES, 512), lambda i: (0, 0, 0)),
                  pl.BlockSpec((2, 1, 512), lambda i: (0, 0, 0))],
        out_specs=[pl.BlockSpec((2, tm, 512), lambda i: (0, i, 0)), pl.BlockSpec((tm, LANES), lambda i: (i, 0))],
        out_shape=[jax.ShapeDtypeStruct((2, T, 512), F32), jax.ShapeDtypeStruct((T, LANES), BF16)],
        compiler_params=_params("parallel"))(proj, wup, bg)


def _gate_logits_bwd(dz, wup, name):
    T = dz[0].shape[0]
    tm = min(512, T)

    def body(dzf_ref, dzb_ref, w_ref, dlr_ref, db_ref, dzb16_ref):
        @pl.when(pl.program_id(0) == 0)
        def _():
            db_ref[...] = jnp.zeros_like(db_ref)

        acc = jnp.zeros((tm, LANES), F32)
        for d, dz_ref in enumerate((dzf_ref, dzb_ref)):
            g = dz_ref[...]
            gb = g.astype(BF16)
            dzb16_ref[d] = gb
            acc = acc + _dg(gb, w_ref[d], 1, 1)
            db_ref[d] += jnp.sum(g, axis=0, keepdims=True)
        dlr_ref[...] = acc

    tok = pl.BlockSpec((tm, 512), lambda i: (i, 0))
    return pl.pallas_call(
        body, name=name, grid=(T // tm,),
        in_specs=[tok, tok, pl.BlockSpec((2, LANES, 512), lambda i: (0, 0, 0))],
        out_specs=[pl.BlockSpec((tm, LANES), lambda i: (i, 0)), pl.BlockSpec((2, 1, 512), lambda i: (0, 0, 0)),
                   pl.BlockSpec((2, tm, 512), lambda i: (0, i, 0))],
        out_shape=[jax.ShapeDtypeStruct((T, LANES), F32), jax.ShapeDtypeStruct((2, 1, 512), F32),
                   jax.ShapeDtypeStruct((2, T, 512), BF16)],
        compiler_params=_params("arbitrary"))(dz[0], dz[1], wup)


def _gla_fwd(proj, z, name):
    T = proj.shape[0]
    nC = T // CHUNK
    H, dk, dv = 4, 128, 256
    rows = _chunk_rows(nC, False)

    def body(qf, kf, vf, zf, qb, kb, vb, zb, of, ob, spf, spb, st):
        @pl.when(pl.program_id(0) == 0)
        def _():
            st[...] = jnp.zeros_like(st)

        for d, (q, k, v, z_ref, o, sp) in enumerate(((qf, kf, vf, zf, of, spf), (qb, kb, vb, zb, ob, spb))):
            tri, tri_t, mref = _tri_consts(d)
            stp = tuple(st[d, h] for h in range(H))
            sp[...] = st[d]
            o_val, stn = _gla_chunk(q[...], k[...], v[...], z_ref[...], stp, tri, tri_t, mref)
            o[...] = o_val
            for h in range(H):
                st[d, h] = stn[h]

    tok = lambda d, w, col: pl.BlockSpec((CHUNK, w), lambda c: (rows[d](c), col))
    gate = lambda d: pl.BlockSpec((None, CHUNK, 512), lambda c: (d, rows[d](c), 0))
    state = lambda d: pl.BlockSpec((None, H, dv, dk), lambda c: (rows[d](c), 0, 0, 0))
    res = pl.pallas_call(
        body, name=name, grid=(nC,),
        in_specs=[tok(0, 512, 0), tok(0, 512, 1), tok(0, 1024, 1), gate(0),
                  tok(1, 512, 0), tok(1, 512, 1), tok(1, 1024, 1), gate(1)],
        out_specs=[tok(0, H * dv, 0), tok(1, H * dv, 0), state(0), state(1)],
        out_shape=[jax.ShapeDtypeStruct((T, H * dv), F32)] * 2 + [jax.ShapeDtypeStruct((nC, H, dv, dk), F32)] * 2,
        scratch_shapes=[pltpu.VMEM((2, H, dv, dk), F32)],
        compiler_params=_params("arbitrary"))(proj, proj, proj, z, proj, proj, proj, z)
    return (res[0], res[1]), (res[2], res[3])


def _gla_bwd(proj, z, sprev, do, name):
    T = proj.shape[0]
    nC = T // CHUNK
    H, dk, dv = 4, 128, 256
    rows = _chunk_rows(nC, True)

    def body(qf, kf, vf, zf, qb, kb, vb, zb, spf, spb, dof, dob,
             dqf, dkf, dvf, dzf, dqb, dkb, dvb, dzb, dst):
        @pl.when(pl.program_id(0) == 0)
        def _():
            dst[...] = jnp.zeros_like(dst)

        for d, (q, k, v, z_ref, sp, do_ref, dq_ref, dk_ref, dv_ref, dz_ref) in enumerate(
                ((qf, kf, vf, zf, spf, dof, dqf, dkf, dvf, dzf), (qb, kb, vb, zb, spb, dob, dqb, dkb, dvb, dzb))):
            tri, tri_t, mref = _tri_consts(d)
            fn = lambda q_, k_, v_, z_, stp: _gla_chunk(q_, k_, v_, z_, stp, tri, tri_t, mref)
            stp = tuple(sp[h] for h in range(H))
            _, vjp = jax.vjp(fn, q[...], k[...], v[...], z_ref[...], stp)
            dq, dkk, dvv, dzz, dstp = vjp((do_ref[...], tuple(dst[d, h] for h in range(H))))
            dq_ref[...] = dq
            dk_ref[...] = dkk
            dv_ref[...] = dvv
            dz_ref[...] = dzz
            for h in range(H):
                dst[d, h] = dstp[h]

    tok = lambda d, w, col: pl.BlockSpec((CHUNK, w), lambda c: (rows[d](c), col))
    gate = lambda d: pl.BlockSpec((None, CHUNK, 512), lambda c: (d, rows[d](c), 0))
    state = lambda d: pl.BlockSpec((None, H, dv, dk), lambda c: (rows[d](c), 0, 0, 0))
    outs = lambda d: [tok(d, 512, 0), tok(d, 512, 0), tok(d, 1024, 0), tok(d, 512, 0)]
    shapes = [jax.ShapeDtypeStruct((T, 512), F32), jax.ShapeDtypeStruct((T, 512), F32),
              jax.ShapeDtypeStruct((T, 1024), F32), jax.ShapeDtypeStruct((T, 512), F32)]
    res = pl.pallas_call(
        body, name=name, grid=(nC,),
        in_specs=[tok(0, 512, 0), tok(0, 512, 1), tok(0, 1024, 1), gate(0),
                  tok(1, 512, 0), tok(1, 512, 1), tok(1, 1024, 1), gate(1),
                  state(0), state(1), tok(0, H * dv, 0), tok(1, H * dv, 0)],
        out_specs=outs(0) + outs(1), out_shape=shapes + shapes,
        scratch_shapes=[pltpu.VMEM((2, H, dv, dk), F32)],
        compiler_params=_params("arbitrary"))(proj, proj, proj, z, proj, proj, proj, z, sprev[0], sprev[1], do, do)
    return (res[0], res[4]), (res[1], res[5]), (res[2], res[6]), (res[3], res[7])


def _l0_combine_fwd(hs, proj, o, gain, name):
    T = proj.shape[0]
    tm = min(512, T)

    def body(hf, hb, ga, of, ob, g, gn, out):
        out[...] = _l0_combine(hf[...], hb[...], ga[...], of[...], ob[...], g[...], gn[...]).astype(BF16)

    tok = pl.BlockSpec((tm, 512), lambda i: (i, 0))
    return pl.pallas_call(
        body, name=name, grid=(T // tm,),
        in_specs=[tok, tok, pl.BlockSpec((tm, 512), lambda i: (i, 1)), tok, tok,
                  pl.BlockSpec((tm, 512), lambda i: (i, 6)), pl.BlockSpec((1, 512), lambda i: (0, 0))],
        out_specs=pl.BlockSpec((tm, 1024), lambda i: (i, 0)),
        out_shape=jax.ShapeDtypeStruct((T, 1024), BF16),
        compiler_params=_params("parallel"))(hs[0], hs[1], proj, o[0], o[1], proj, gain)


def _l0_combine_bwd(hs, proj, o, gain, dmix, name):
    T = proj.shape[0]
    tm = min(512, T)

    def body(hf, hb, ga, of, ob, g, gn, dm, dho_ref, dga_ref, do_ref, dg_ref, dgn_ref):
        @pl.when(pl.program_id(0) == 0)
        def _():
            dgn_ref[...] = jnp.zeros_like(dgn_ref)

        _, vjp = jax.vjp(_l0_combine, hf[...], hb[...], ga[...], of[...], ob[...], g[...], gn[...])
        dhf, _, dga, dof, _, dg, dgn = vjp(dm[...])
        dho_ref[...] = dhf
        dga_ref[...] = dga
        do_ref[...] = dof
        dg_ref[...] = dg
        dgn_ref[...] += dgn

    tok = lambda: pl.BlockSpec((tm, 512), lambda i: (i, 0))
    return pl.pallas_call(
        body, name=name, grid=(T // tm,),
        in_specs=[tok(), tok(), pl.BlockSpec((tm, 512), lambda i: (i, 1)), tok(), tok(),
                  pl.BlockSpec((tm, 512), lambda i: (i, 6)), pl.BlockSpec((1, 512), lambda i: (0, 0)),
                  pl.BlockSpec((tm, 1024), lambda i: (i, 0))],
        out_specs=[tok(), tok(), tok(), tok(), pl.BlockSpec((1, 512), lambda i: (0, 0))],
        out_shape=[jax.ShapeDtypeStruct((T, 512), F32)] * 4 + [jax.ShapeDtypeStruct((1, 512), F32)],
        compiler_params=_params("arbitrary"))(hs[0], hs[1], proj, o[0], o[1], proj, gain, dmix)


def _l0_assemble(dxa, dga, dq, df, dv, dg, name):
    T = dxa.shape[0]
    tm = min(512, T)

    def body(xa, ga, q0, q1, f0, f1, v0, v1, g, out):
        out[...] = jnp.concatenate([xa[...], ga[...], q0[...] + q1[...], f0[...], f1[...], v0[...] + v1[...],
                                    g[...]], axis=1).astype(BF16)

    tok = lambda: pl.BlockSpec((tm, 512), lambda i: (i, 0))
    return pl.pallas_call(
        body, name=name, grid=(T // tm,),
        in_specs=[tok() for _ in range(9)],
        out_specs=pl.BlockSpec((tm, AB_IN), lambda i: (i, 0)),
        out_shape=jax.ShapeDtypeStruct((T, AB_IN), BF16),
        compiler_params=_params("parallel"))(dxa, dga, dq[0], dq[1], df[0], df[1], dv[0], dv[1], dg)


def _l1_combine_fwd(o, proj, gain, name):
    T = proj.shape[0]
    tm = min(512, T)

    def body(of, ob, r, gn, out):
        out[...] = _l1_combine(of[...], ob[...], r[...], gn[...]).astype(BF16)

    tok = pl.BlockSpec((tm, 1024), lambda i: (i, 0))
    return pl.pallas_call(
        body, name=name, grid=(T // tm,),
        in_specs=[tok, tok, pl.BlockSpec((tm, 1024), lambda i: (i, 2)), pl.BlockSpec((1, 1024), lambda i: (0, 0))],
        out_specs=pl.BlockSpec((tm, 1024), lambda i: (i, 0)),
        out_shape=jax.ShapeDtypeStruct((T, 1024), BF16),
        compiler_params=_params("parallel"))(o[0], o[1], proj, gain)


def _l1_combine_bwd(o, proj, gain, dmix, name):
    T = proj.shape[0]
    tm = min(512, T)

    def body(of, ob, r, gn, dm, do_ref, dr_ref, dgn_ref):
        @pl.when(pl.program_id(0) == 0)
        def _():
            dgn_ref[...] = jnp.zeros_like(dgn_ref)

        _, vjp = jax.vjp(_l1_combine, of[...], ob[...], r[...], gn[...])
        dof, _, dr, dgn = vjp(dm[...])
        do_ref[...] = dof
        dr_ref[...] = dr
        dgn_ref[...] += dgn

    tok = lambda: pl.BlockSpec((tm, 1024), lambda i: (i, 0))
    return pl.pallas_call(
        body, name=name, grid=(T // tm,),
        in_specs=[tok(), tok(), pl.BlockSpec((tm, 1024), lambda i: (i, 2)),
                  pl.BlockSpec((1, 1024), lambda i: (0, 0)), tok()],
        out_specs=[tok(), tok(), pl.BlockSpec((1, 1024), lambda i: (0, 0))],
        out_shape=[jax.ShapeDtypeStruct((T, 1024), F32)] * 2 + [jax.ShapeDtypeStruct((1, 1024), F32)],
        compiler_params=_params("arbitrary"))(o[0], o[1], proj, gain, dmix)


def _l1_assemble(dq, dk, dv, dr, dlr, name):
    T = dr.shape[0]
    tm = min(512, T)

    def body(q0, q1, k0, k1, v0, v1, r, a, out):
        out[...] = jnp.concatenate([q0[...] + q1[...], k0[...] + k1[...], v0[...] + v1[...], r[...], a[...]],
                                   axis=1).astype(BF16)

    tok = lambda w: pl.BlockSpec((tm, w), lambda i: (i, 0))
    return pl.pallas_call(
        body, name=name, grid=(T // tm,),
        in_specs=[tok(512), tok(512), tok(512), tok(512), tok(1024), tok(1024), tok(1024), tok(LANES)],
        out_specs=pl.BlockSpec((tm, GLA_IN_PAD), lambda i: (i, 0)),
        out_shape=jax.ShapeDtypeStruct((T, GLA_IN_PAD), BF16),
        compiler_params=_params("parallel"))(dq[0], dq[1], dk[0], dk[1], dv[0], dv[1], dr, dlr)


HBM_SPEC = pl.BlockSpec(memory_space=pltpu.HBM)


def _place():
    x, y, c = lax.axis_index("x"), lax.axis_index("y"), lax.axis_index("c")
    return x, y, c


def _allgather_vmem(x_shard, name, reduce=False):
    m_per, n = x_shard.shape

    def body(x_ref, out_ref, *rest):
        if reduce:
            sum_ref, send_sems, recv_sems, local_sem = rest
        else:
            send_sems, recv_sems, local_sem = rest
        x, y, c = _place()
        me, sibling = (x, y, c), (x, y, 1 - c)
        chips = [(1 - x, y), (x, 1 - y), (1 - x, 1 - y)]

        def rows(px, py, pc):
            return out_ref.at[pl.ds((4 * px + 2 * py + pc) * m_per, m_per), :]

        def copy(k, block, to, src=None):
            return pltpu.make_async_remote_copy(
                src_ref=rows(*block) if src is None else src, dst_ref=rows(*block),
                send_sem=send_sems.at[k], recv_sem=recv_sems.at[k], device_id=to, device_id_type=MESH)

        mine = pltpu.make_async_copy(x_ref, rows(*me), local_sem)
        mine.start()
        first = [copy(0, me, sibling, src=x_ref)]
        first += [copy(1 + j, me, (*chip, c), src=x_ref) for j, chip in enumerate(chips)]
        for cp in first:
            cp.start()
        passed = [copy(4 + j, (*chip, c), sibling) for j, chip in enumerate(chips)]
        for j, chip in enumerate(chips):
            copy(1 + j, (*chip, c), me).wait_recv()
            passed[j].start()
        copy(0, sibling, me).wait_recv()
        for j, chip in enumerate(chips):
            copy(4 + j, (*chip, 1 - c), me).wait_recv()
        for cp in first + passed:
            cp.wait_send()
        mine.wait()
        if reduce:
            acc = out_ref[pl.ds(0, m_per), :]
            for j in range(1, N_DEV):
                acc = acc + out_ref[pl.ds(j * m_per, m_per), :]
            sum_ref[...] = acc

    vm = pl.BlockSpec(memory_space=pltpu.VMEM)
    out_shape = [jax.ShapeDtypeStruct((N_DEV * m_per, n), x_shard.dtype)]
    if reduce:
        out_shape.append(jax.ShapeDtypeStruct((m_per, n), x_shard.dtype))
    res = pl.pallas_call(
        body, name=name, in_specs=[vm], out_specs=[vm] * len(out_shape), out_shape=out_shape,
        scratch_shapes=[pltpu.SemaphoreType.DMA((7,)), pltpu.SemaphoreType.DMA((7,)), pltpu.SemaphoreType.DMA],
        compiler_params=pltpu.CompilerParams(has_side_effects=True, vmem_limit_bytes=VMEM_LIMIT))(x_shard)
    return res[1] if reduce else res[0]


SEM_SPEC = pl.BlockSpec(memory_space=pltpu.SEMAPHORE)
DATAFLOW_EFFECT = pltpu.SideEffectType.DATAFLOW_SIDE_EFFECTING


def _copies(plan, srcs, lands, send_sems, recv_sems):
    x, y, c = _place()
    return [pltpu.make_async_remote_copy(src_ref=s, dst_ref=d, send_sem=send_sems.at[k], recv_sem=recv_sems.at[k],
                                         device_id=dev, device_id_type=MESH)
            for k, (s, d, dev) in enumerate(plan(srcs, lands, x, y, c))]


def _copies_start(plan, n_copies, srcs, lands, name):
    ns, nl = len(srcs), len(lands)

    def body(*refs):
        send_sems, recv_sems = refs[ns + nl], refs[ns + nl + 1]
        for cp in _copies(plan, refs[:ns], refs[ns:ns + nl], send_sems, recv_sems):
            cp.start()
        refs[-1][...] = jnp.zeros_like(refs[-1])

    arrays = list(srcs) + list(lands)
    res = pl.pallas_call(
        body, name=name,
        in_specs=[HBM_SPEC] * (ns + nl),
        out_specs=tuple([SEM_SPEC, SEM_SPEC] + [HBM_SPEC] * (ns + nl) + [pl.BlockSpec(memory_space=pltpu.VMEM)]),
        out_shape=tuple([pltpu.SemaphoreType.DMA((n_copies,)), pltpu.SemaphoreType.DMA((n_copies,))]
                        + [pltpu.HBM(a.shape, a.dtype) for a in arrays]
                        + [jax.ShapeDtypeStruct((SUBLANES, LANES), F32)]),
        input_output_aliases={i: 2 + i for i in range(ns + nl)},
        compiler_params=pltpu.CompilerParams(has_side_effects=DATAFLOW_EFFECT),
    )(*[pltpu.with_memory_space_constraint(a, pltpu.HBM) for a in arrays])
    return res[0], res[1], list(res[2:2 + ns]), list(res[2 + ns:2 + ns + nl]), res[-1]


def _copies_wait(plan, started, after, name):
    send_sems, recv_sems, srcs, lands, _ = started
    ns, nl = len(srcs), len(lands)

    def body(*refs):
        for cp in _copies(plan, refs[:ns], refs[ns:ns + nl], refs[ns + nl], refs[ns + nl + 1]):
            cp.wait_send()
            cp.wait_recv()

    arrays = list(srcs) + list(lands)
    res = pl.pallas_call(
        body, name=name,
        in_specs=[HBM_SPEC] * (ns + nl) + [SEM_SPEC, SEM_SPEC, pl.BlockSpec(memory_space=pl.ANY)],
        out_specs=tuple([HBM_SPEC] * (ns + nl)),
        out_shape=tuple(pltpu.HBM(a.shape, a.dtype) for a in arrays),
        input_output_aliases={i: i for i in range(ns + nl)},
        compiler_params=pltpu.CompilerParams(has_side_effects=DATAFLOW_EFFECT),
    )(*arrays, send_sems, recv_sems, after)
    return list(res[:ns]), list(res[ns:])


def _after(token, value):
    return value + token[0:1, 0:1].astype(value.dtype)


def _chips(x, y):
    return [(1 - x, y), (x, 1 - y), (1 - x, 1 - y)]


def _plan_gather_first(srcs, lands, x, y, c):
    me = 4 * x + 2 * y + c
    out = []
    for s, l in zip(srcs, lands):
        out.append((s, l.at[me], (x, y, 1 - c)))
        out += [(s, l.at[me], (*chip, c)) for chip in _chips(x, y)]
    return out


def _plan_gather_pass(srcs, lands, x, y, c):
    out = []
    for l in lands:
        for chip in _chips(x, y):
            slot = l.at[4 * chip[0] + 2 * chip[1] + c]
            out.append((slot, slot, (x, y, 1 - c)))
    return out


def _plan_grads_sibling(srcs, lands, x, y, c):
    return [(s.at[2 * q + (1 - c)], l.at[q], (x, y, 1 - c)) for s, l in zip(srcs, lands) for q in range(4)]


def _plan_grads_chips(srcs, lands, x, y, c):
    return [(s.at[2 * chip[0] + chip[1]], l.at[k], (*chip, c))
            for s, l in zip(srcs, lands) for k, chip in enumerate(_chips(x, y))]


def _landing(n_slots, like):
    return [lax.empty((n_slots,) + a.shape[1:], a.dtype) for a in like]


def _chip_partial(g, r1, place, name):
    _, R, C = g.shape
    tr = min(256, R)
    assert R % tr == 0

    def body(pl_ref, g_ref, r_ref, pb_ref, pm_ref):
        q = pl.program_id(1)
        s = g_ref[...] + r_ref[...]
        pb_ref[...] = s.astype(BF16)

        @pl.when(q == pl_ref[1])
        def _():
            pm_ref[...] = s

    grid_spec = pltpu.PrefetchScalarGridSpec(
        num_scalar_prefetch=1, grid=(R // tr, 4),
        in_specs=[pl.BlockSpec((None, tr, C), lambda r, q, p: (2 * q + p[0], r, 0)),
                  pl.BlockSpec((None, tr, C), lambda r, q, p: (q, r, 0))],
        out_specs=[pl.BlockSpec((None, tr, C), lambda r, q, p: (q, r, 0)),
                   pl.BlockSpec((tr, C), lambda r, q, p: (r, 0))])
    return pl.pallas_call(
        body, name=name, grid_spec=grid_spec,
        out_shape=[jax.ShapeDtypeStruct((4, R, C), BF16), jax.ShapeDtypeStruct((R, C), F32)],
        compiler_params=_params("parallel", "arbitrary"))(place, g, r1)


def _adamw(w, gparts, m, v, name):
    R, C = w.shape
    tr = min(256, R)
    assert R % tr == 0
    g0, g3 = gparts

    def body(w_ref, g0_ref, *rest):
        if g3 is not None:
            g3_ref, m_ref, v_ref, go, do, mo, vo = rest
        else:
            m_ref, v_ref, go, do, mo, vo = rest
        g = g0_ref[...]
        if g3 is not None:
            for k in range(3):
                g = g + g3_ref[k].astype(F32)
        wv = w_ref[...]
        mn = ADAM_B1 * m_ref[...] + (1.0 - ADAM_B1) * g
        vn = ADAM_B2 * v_ref[...] + (1.0 - ADAM_B2) * jnp.square(g)
        m_hat = mn / (1.0 - ADAM_B1 ** ADAM_STEP)
        v_hat = vn / (1.0 - ADAM_B2 ** ADAM_STEP)
        go[...] = g
        do[...] = -ADAM_LR * (m_hat / (jnp.sqrt(v_hat) + ADAM_EPS) + ADAM_WD * wv)
        mo[...] = mn
        vo[...] = vn

    blk = pl.BlockSpec((tr, C), lambda i: (i, 0))
    in_specs = [blk, blk] + ([pl.BlockSpec((3, tr, C), lambda i: (0, i, 0))] if g3 is not None else []) + [blk, blk]
    args = [w, g0] + ([g3] if g3 is not None else []) + [m, v]
    return pl.pallas_call(
        body, name=name, grid=(R // tr,), in_specs=in_specs, out_specs=[blk] * 4,
        out_shape=[jax.ShapeDtypeStruct((R, C), F32)] * 4,
        compiler_params=_params("parallel"))(*args)


def _adamw_layers(w, parts, m, v, name):
    _, R, C = w.shape
    tr = min(256, R)
    assert R % tr == 0

    def body(w_ref, p0, r0, p1, r1, m_ref, v_ref, go, do, mo, vo):
        gs = []
        for p, r in ((p0, r0), (p1, r1)):
            g = p[...]
            for k in range(3):
                g = g + r[k].astype(F32)
            gs.append(g)
        g = jnp.where(pl.program_id(0) == 0, gs[0], gs[1])
        mn = ADAM_B1 * m_ref[...] + (1.0 - ADAM_B1) * g
        vn = ADAM_B2 * v_ref[...] + (1.0 - ADAM_B2) * jnp.square(g)
        m_hat = mn / (1.0 - ADAM_B1 ** ADAM_STEP)
        v_hat = vn / (1.0 - ADAM_B2 ** ADAM_STEP)
        go[...] = g
        do[...] = -ADAM_LR * (m_hat / (jnp.sqrt(v_hat) + ADAM_EPS) + ADAM_WD * w_ref[...])
        mo[...] = mn
        vo[...] = vn

    lay = pl.BlockSpec((None, tr, C), lambda l, i: (l, i, 0))
    one = pl.BlockSpec((tr, C), lambda l, i: (i, 0))
    three = pl.BlockSpec((3, tr, C), lambda l, i: (0, i, 0))
    return pl.pallas_call(
        body, name=name, grid=(2, R // tr), in_specs=[lay, one, three, one, three, lay, lay],
        out_specs=[lay] * 4, out_shape=[jax.ShapeDtypeStruct((2, R, C), F32)] * 4,
        compiler_params=_params("parallel", "parallel"))(w, parts[0][0], parts[0][1], parts[1][0], parts[1][1], m, v)


SMALL_SHARDED = ("rg_conv_w", "rg_b_a", "rg_b_x", "rg_lambda", "gla_w_gate_up", "gla_b_gate", "gla_norm")
SMALL_REPLICATED = ("norm_mix", "norm_mlp", "norm_final", "rg_conv_b", "rg_w_a", "rg_w_x", "hg_lb_logits", "hg_norm")
WEIGHT_NAMES = ("norm_mix", "norm_mlp", "norm_final", "mlp_w1", "mlp_w2", "ab_w_in", "ab_w_out", "rg_conv_w",
                "rg_conv_b", "rg_w_a", "rg_b_a", "rg_w_x", "rg_b_x", "rg_lambda", "hg_lb_logits", "hg_norm",
                "gla_w_in", "gla_w_out", "gla_w_gate_up", "gla_b_gate", "gla_norm")


def _rows128(a):
    return a.reshape(-1, LANES)


def _part_rows(a):
    return -(-(a.size // LANES) // SUBLANES) * SUBLANES


def _pack_rows(arrays, pad_to=SUBLANES):
    parts = [jnp.pad(_rows128(a), ((0, _part_rows(a) - a.size // LANES), (0, 0))) for a in arrays]
    total = sum(p.shape[0] for p in parts)
    extra = (-total) % pad_to
    if extra:
        parts.append(jnp.zeros((extra, LANES), parts[0].dtype))
    return jnp.concatenate(parts, axis=0)


def _unshard_last(g, shape_local):
    nd = len(shape_local)
    t = g.reshape((N_DEV,) + tuple(shape_local))
    t = jnp.moveaxis(t, 0, nd - 1)
    return t.reshape(tuple(shape_local[:-1]) + (N_DEV * shape_local[-1],))


def _block_diag(w):
    eye = jnp.eye(8, dtype=w.dtype)
    return (w[:, :, :, None, :] * eye[None, :, None, :, None]).reshape(2, RG_W, RG_W)


def _block_diag_extract(dw):
    t = dw.reshape(2, 8, 64, 8, 64)
    return jnp.moveaxis(jnp.diagonal(t, axis1=1, axis2=3), -1, 1)


def kernel(x, norm_mix, norm_mlp, norm_final, mlp_w1, mlp_w2, ab_w_in, ab_w_out, rg_conv_w, rg_conv_b, rg_w_a, rg_b_a, rg_w_x, rg_b_x, rg_lambda, hg_lb_logits, hg_norm, gla_w_in, gla_w_out, gla_w_gate_up, gla_b_gate, gla_norm, loss_target, m_norm_mix, m_norm_mlp, m_norm_final, m_mlp_w1, m_mlp_w2, m_ab_w_in, m_ab_w_out, m_rg_conv_w, m_rg_conv_b, m_rg_w_a, m_rg_b_a, m_rg_w_x, m_rg_b_x, m_rg_lambda, m_hg_lb_logits, m_hg_norm, m_gla_w_in, m_gla_w_out, m_gla_w_gate_up, m_gla_b_gate, m_gla_norm, v_norm_mix, v_norm_mlp, v_norm_final, v_mlp_w1, v_mlp_w2, v_ab_w_in, v_ab_w_out, v_rg_conv_w, v_rg_conv_b, v_rg_w_a, v_rg_b_a, v_rg_w_x, v_rg_b_x, v_rg_lambda, v_hg_lb_logits, v_hg_norm, v_gla_w_in, v_gla_w_out, v_gla_w_gate_up, v_gla_b_gate, v_gla_norm):
    w_loc = dict(norm_mix=norm_mix, norm_mlp=norm_mlp, norm_final=norm_final, mlp_w1=mlp_w1, mlp_w2=mlp_w2,
                 ab_w_in=ab_w_in, ab_w_out=ab_w_out, rg_conv_w=rg_conv_w, rg_conv_b=rg_conv_b, rg_w_a=rg_w_a,
                 rg_b_a=rg_b_a, rg_w_x=rg_w_x, rg_b_x=rg_b_x, rg_lambda=rg_lambda, hg_lb_logits=hg_lb_logits,
                 hg_norm=hg_norm, gla_w_in=gla_w_in, gla_w_out=gla_w_out, gla_w_gate_up=gla_w_gate_up,
                 gla_b_gate=gla_b_gate, gla_norm=gla_norm)
    m_loc = dict(norm_mix=m_norm_mix, norm_mlp=m_norm_mlp, norm_final=m_norm_final, mlp_w1=m_mlp_w1,
                 mlp_w2=m_mlp_w2, ab_w_in=m_ab_w_in, ab_w_out=m_ab_w_out, rg_conv_w=m_rg_conv_w,
                 rg_conv_b=m_rg_conv_b, rg_w_a=m_rg_w_a, rg_b_a=m_rg_b_a, rg_w_x=m_rg_w_x, rg_b_x=m_rg_b_x,
                 rg_lambda=m_rg_lambda, hg_lb_logits=m_hg_lb_logits, hg_norm=m_hg_norm, gla_w_in=m_gla_w_in,
                 gla_w_out=m_gla_w_out, gla_w_gate_up=m_gla_w_gate_up, gla_b_gate=m_gla_b_gate,
                 gla_norm=m_gla_norm)
    v_loc = dict(norm_mix=v_norm_mix, norm_mlp=v_norm_mlp, norm_final=v_norm_final, mlp_w1=v_mlp_w1,
                 mlp_w2=v_mlp_w2, ab_w_in=v_ab_w_in, ab_w_out=v_ab_w_out, rg_conv_w=v_rg_conv_w,
                 rg_conv_b=v_rg_conv_b, rg_w_a=v_rg_w_a, rg_b_a=v_rg_b_a, rg_w_x=v_rg_w_x, rg_b_x=v_rg_b_x,
                 rg_lambda=v_rg_lambda, hg_lb_logits=v_hg_lb_logits, hg_norm=v_hg_norm, gla_w_in=v_gla_w_in,
                 gla_w_out=v_gla_w_out, gla_w_gate_up=v_gla_w_gate_up, gla_b_gate=v_gla_b_gate,
                 gla_norm=v_gla_norm)

    T = x.shape[1]
    h0 = x.reshape(T, D_MODEL)
    target = loss_target.reshape(T, D_MODEL)
    ax, ay, ac = lax.axis_index("x"), lax.axis_index("y"), lax.axis_index("c")
    dev = 4 * ax + 2 * ay + ac
    place = jnp.stack([ac, 2 * ax + ay]).astype(jnp.int32)

    abin_shard = ab_w_in[0].astype(BF16)
    first_started = _copies_start(_plan_gather_first, 4, [abin_shard], _landing(N_DEV, [abin_shard[None]]),
                                  "ag_first_start")
    rest_shards = [mlp_w1[0].astype(BF16), mlp_w2[0].astype(BF16), gla_w_in[0].astype(BF16),
                   gla_w_out[0].astype(BF16), mlp_w1[1].astype(BF16), mlp_w2[1].astype(BF16),
                   _after(first_started[4], ab_w_out[0].astype(BF16))]
    ag_started = _copies_start(_plan_gather_first, 4 * len(rest_shards), rest_shards,
                               _landing(N_DEV, [s[None] for s in rest_shards]), "ag_rest_start")

    small_local = [w_loc[n] for n in SMALL_SHARDED]
    small_g = _allgather_vmem(_pack_rows(small_local, 8), "ag_small")
    small_g = small_g.reshape(N_DEV, -1, LANES)
    full = {}
    off = 0
    for n, a in zip(SMALL_SHARDED, small_local):
        full[n] = _unshard_last(small_g[:, off:off + a.size // LANES].reshape(N_DEV, a.size), a.shape)
        off += _part_rows(a)
    conv_w = full["rg_conv_w"][0]
    b_a, b_x, lam = full["rg_b_a"][0], full["rg_b_x"][0], full["rg_lambda"][0]
    w_up, b_gate, g_norm = full["gla_w_gate_up"][0], full["gla_b_gate"][0], full["gla_norm"]

    cw8 = jnp.pad(conv_w, ((0, 4), (0, 0)))
    wbd = jnp.concatenate([_block_diag(rg_w_a[0]), _block_diag(rg_w_x[0])], axis=2).astype(BF16)
    rg_bias = jnp.concatenate([b_a, b_x], axis=1).reshape(2, 1, 2 * RG_W)
    lam3 = lam.reshape(2, 1, RG_W)
    l0, l1 = hg_lb_logits[0:1], hg_lb_logits[1:2]
    wup_pad = jnp.zeros((2, LANES, 512), F32).at[0, 0:16].set(w_up[0]).at[1, 16:32].set(w_up[1])
    bg3 = b_gate.reshape(2, 1, 512)
    nmix0, nmix1 = norm_mix[0:1], norm_mix[1:2]
    nmlp0, nmlp1 = norm_mlp[0:1], norm_mlp[1:2]
    nfin = norm_final.reshape(1, D_MODEL)

    prepared = (ag_started[4] + cw8[:, 0:LANES] + wup_pad[0, 0:SUBLANES, 0:LANES] + rg_bias[0, :, 0:LANES]
                + wbd[0, 0:SUBLANES, 0:LANES].astype(F32) + lam3[0, :, 0:LANES] + bg3[0, :, 0:LANES])
    (abin_shard,), abin_l = _copies_wait(_plan_gather_first, first_started, prepared, "ag_first_wait")
    first_pass = _copies_start(_plan_gather_pass, 3, [], abin_l, "ag_first_pass_start")
    _, (abin_g,) = _copies_wait(_plan_gather_pass, first_pass, first_pass[4], "ag_first_pass_wait")
    abin_g = lax.dynamic_update_index_in_dim(abin_g, abin_shard, dev, 0)
    wab_in = jnp.transpose(abin_g, (1, 0, 2)).reshape(D_MODEL, AB_IN)
    proj0, y0 = _norm_matmul(h0, _after(ag_started[4], nmix0), wab_in, "l0_in_proj")
    xc = _rg_conv_fwd(proj0, cw8, rg_conv_b, "rg_conv")
    hs = _rg_scan_fwd(xc, wbd, rg_bias, lam3, "rg_scan")
    o_hg, s_hg = _hg_fwd(proj0, l0, l1, "hg_chunks")
    both_done = hs[0][0:SUBLANES, 0:LANES] + o_hg[0][0:SUBLANES, 0:LANES]
    rest_shards, rest_lands = _copies_wait(_plan_gather_first, ag_started, both_done, "ag_rest_wait")
    pass_started = _copies_start(_plan_gather_pass, 3 * len(rest_lands), [], rest_lands, "ag_pass_start")
    mixin0 = _l0_combine_fwd(hs, proj0, o_hg, _after(pass_started[4], hg_norm), "l0_combine")
    _, rest_g = _copies_wait(_plan_gather_pass, pass_started, mixin0, "ag_pass_wait")
    rest_g = [lax.dynamic_update_index_in_dim(g, s, dev, 0) for g, s in zip(rest_g, rest_shards)]
    wab_out = rest_g[6].reshape(D_MODEL, D_MODEL)
    h1 = _matmul_res(mixin0, wab_out, h0, "l0_out_proj")
    w1g = (rest_g[0], rest_g[4])
    w2f = (rest_g[1].reshape(D_FF, D_MODEL), rest_g[5].reshape(D_FF, D_MODEL))
    wgla_in = jnp.pad(jnp.transpose(rest_g[2], (1, 0, 2)).reshape(D_MODEL, GLA_IN),
                      ((0, 0), (0, GLA_IN_PAD - GLA_IN)))
    wgla_out = rest_g[3].reshape(D_MODEL, D_MODEL)
    h2, pre0, ym0 = _mlp_fwd(h1, nmlp0, w1g[0], w2f[0], "mlp0")
    proj1, y1 = _norm_matmul(h2, nmix1, wgla_in, "l1_in_proj")
    z_gate, lr_b = _gate_logits(proj1, wup_pad, bg3, "gla_gate_logits")
    o_gla, s_gla = _gla_fwd(proj1, z_gate, "gla_chunks")
    mixin1 = _l1_combine_fwd(o_gla, proj1, g_norm, "l1_combine")
    h3 = _matmul_res(mixin1, wgla_out, h2, "l1_out_proj")
    h4, pre1, ym1 = _mlp_fwd(h3, nmlp1, w1g[1], w2f[1], "mlp1")
    loss_blk, dh4, dh4b, d_nfin = _final_loss(h4, nfin, target, "final_loss")

    dh3, dh3b, dpre1, act1, d_nmlp1 = _mlp_bwd(dh4, dh4b, h3, nmlp1, pre1, w1g[1], w2f[1], "mlp1_bwd")
    g_w1_1 = _wgrad(ym1, dpre1, 512, "mlp1_dw1", sharded_cols=True)
    g_w2_1 = _wgrad(act1, dh4b, 512, "mlp1_dw2")
    dmixin1 = _dgrad(dh3b, wgla_out, "l1_out_dgrad")
    g_gla_out = _wgrad(mixin1, dh3b, 512, "l1_out_dw")
    do_gla, dr, d_gnorm = _l1_combine_bwd(o_gla, proj1, g_norm, dmixin1, "l1_combine_bwd")
    dq1, dk1, dv1, dz_gate = _gla_bwd(proj1, z_gate, s_gla, do_gla, "gla_chunks_bwd")
    dlr1, d_bg, dz_b = _gate_logits_bwd(dz_gate, wup_pad, "gla_gate_logits_bwd")
    d_wup = [_wgrad(lr_b, dz_b[d], 512, "gla_gate_dw%d" % d) for d in range(2)]
    dproj1 = _l1_assemble(dq1, dk1, dv1, dr, dlr1, "l1_assemble")
    dh2, dh2b, d_nmix1 = _dgrad_norm(dproj1, wgla_in, h2, nmix1, dh3, "l1_in_dgrad")
    g_gla_in = _wgrad(y1, dproj1, 640, "l1_in_dw")

    def reduce_start(grads, tag):
        return _copies_start(_plan_grads_sibling, 4 * len(grads), grads, _landing(4, grads), "rs_%s_d2d_start" % tag)

    def reduce_mid(started, after, tag):
        grads, got = _copies_wait(_plan_grads_sibling, started, after, "rs_%s_d2d_wait" % tag)
        parts = [_chip_partial(g, r, place, "rs_%s_partial%d" % (tag, a)) for a, (g, r) in enumerate(zip(grads, got))]
        pb = [p[0] for p in parts]
        return _copies_start(_plan_grads_chips, 3 * len(pb), pb, _landing(3, pb), "rs_%s_ici_start" % tag), \
            [p[1] for p in parts]

    def reduce_end(started, mine, after, tag):
        _, got = _copies_wait(_plan_grads_chips, started, after, "rs_%s_ici_wait" % tag)
        return list(zip(mine, got))

    slots_l1 = [g_w1_1, g_w2_1.reshape(N_DEV, 512, D_MODEL),
                jnp.transpose(g_gla_in[:, :GLA_IN].reshape(D_MODEL, N_DEV, GLA_IN // N_DEV), (1, 0, 2)),
                g_gla_out.reshape(N_DEV, 128, D_MODEL)]
    ra_d2d = reduce_start(slots_l1, "l1")

    dh1, dh1b, dpre0, act0, d_nmlp0 = _mlp_bwd(dh2, dh2b, h1, _after(ra_d2d[4], nmlp0), pre0, w1g[0], w2f[0],
                                               "mlp0_bwd")
    g_w1_0 = _wgrad(ym0, dpre0, 512, "mlp0_dw1", sharded_cols=True)
    g_w2_0 = _wgrad(act0, dh2b, 512, "mlp0_dw2")
    ra_ici, ra_mine = reduce_mid(ra_d2d, g_w2_0, "l1")
    rb_d2d = reduce_start([g_w1_0, g_w2_0.reshape(N_DEV, 512, D_MODEL)], "mlp0")
    dmixin0 = _dgrad(dh1b, wab_out, "l0_out_dgrad")
    g_ab_out = _wgrad(mixin0, dh1b, 512, "l0_out_dw")
    dho, dga, do_hg, dg_gate, d_hgnorm = _l0_combine_bwd(
        hs, proj0, o_hg, _after(rb_d2d[4], _after(ra_ici[4], hg_norm)), dmixin0, "l0_combine_bwd")
    dxc, d_wbd, d_rgb, d_lam = _rg_scan_bwd(xc, wbd, rg_bias, lam3, hs, dho, "rg_scan_bwd")
    dxa, d_cw8, d_cb = _rg_conv_bwd(dxc, proj0, cw8, "rg_conv_bwd")
    dq0, df0, dv0, d_l0, d_l1 = _hg_bwd(proj0, l0, l1, s_hg, do_hg, "hg_chunks_bwd")
    rb_ici, rb_mine = reduce_mid(rb_d2d, d_l0, "mlp0")
    dproj0 = _l0_assemble(dxa, dga, dq0, df0, dv0, dg_gate, "l0_assemble")
    g_ab_in = _wgrad(y0, dproj0, 512, "l0_in_dw")
    rc_d2d = reduce_start([jnp.transpose(g_ab_in.reshape(D_MODEL, N_DEV, AB_IN // N_DEV), (1, 0, 2)),
                           g_ab_out.reshape(N_DEV, 128, D_MODEL)], "ab")
    dx, _, d_nmix0 = _dgrad_norm(dproj0, wab_in, h0, _after(rc_d2d[4], _after(rb_ici[4], nmix0)), dh1,
                                 "l0_in_dgrad")
    rc_ici, rc_mine = reduce_mid(rc_d2d, d_nmix0, "ab")

    pieces_l1 = reduce_end(ra_ici, ra_mine, rc_ici[4], "l1")
    res_gla_in = _adamw(gla_w_in[0], pieces_l1[2], m_gla_w_in[0], v_gla_w_in[0], "adamw_gla_in")
    res_gla_out = _adamw(gla_w_out[0], pieces_l1[3], m_gla_w_out[0], v_gla_w_out[0], "adamw_gla_out")
    pieces_mlp0 = reduce_end(rb_ici, rb_mine, res_gla_out[0], "mlp0")
    res_w1 = _adamw_layers(mlp_w1, (pieces_mlp0[0], pieces_l1[0]), m_mlp_w1, v_mlp_w1, "adamw_mlp_w1")
    res_w2 = _adamw_layers(mlp_w2, (pieces_mlp0[1], pieces_l1[1]), m_mlp_w2, v_mlp_w2, "adamw_mlp_w2")
    res = {"mlp_w1": tuple(res_w1), "mlp_w2": tuple(res_w2),
           "gla_w_in": tuple(res_gla_in[k][None] for k in range(4)),
           "gla_w_out": tuple(res_gla_out[k][None] for k in range(4))}

    d_wa = _block_diag_extract(d_wbd[:, :, :RG_W])[None]
    d_wx = _block_diag_extract(d_wbd[:, :, RG_W:])[None]
    small_full = {
        "norm_mix": jnp.concatenate([d_nmix0, d_nmix1], axis=0), "norm_mlp": jnp.concatenate([d_nmlp0, d_nmlp1], axis=0),
        "norm_final": d_nfin.reshape(D_MODEL), "rg_conv_b": d_cb, "rg_w_a": d_wa, "rg_w_x": d_wx,
        "hg_lb_logits": jnp.concatenate([d_l0[0] + d_l0[1], d_l1[0] + d_l1[1]], axis=0), "hg_norm": d_hgnorm,
        "rg_conv_w": d_cw8[0:4][None], "rg_b_a": d_rgb[:, 0, :RG_W][None], "rg_b_x": d_rgb[:, 0, RG_W:][None],
        "rg_lambda": d_lam[:, 0, :][None],
        "gla_w_gate_up": jnp.stack([d_wup[0][0:16], d_wup[1][16:32]])[None], "gla_b_gate": d_bg[:, 0, :][None],
        "gla_norm": d_gnorm}
    small_names = SMALL_REPLICATED + SMALL_SHARDED
    packed = _pack_rows([loss_blk] + [small_full[n] for n in small_names], 8)
    summed = _allgather_vmem(packed, "ar_small", reduce=True)
    loss = summed[0, 0]
    g_small = {}
    off = SUBLANES
    for n in small_names:
        a = small_full[n]
        gfull = summed[off:off + a.size // LANES].reshape(a.shape)
        off += _part_rows(a)
        if n in SMALL_SHARDED:
            loc = w_loc[n].shape[-1]
            gfull = lax.dynamic_slice_in_dim(gfull, dev * loc, loc, axis=gfull.ndim - 1)
        g_small[n] = gfull
    sw = _pack_rows([w_loc[n] for n in small_names], 256)
    sg = _pack_rows([g_small[n] for n in small_names], 256)
    sm = _pack_rows([m_loc[n] for n in small_names], 256)
    sv = _pack_rows([v_loc[n] for n in small_names], 256)
    small_res = _adamw(sw, (sg, None), sm, sv, "adamw_small")
    others_done = (res_w1[1][0, 0:SUBLANES, 0:LANES] + res_w2[1][0, 0:SUBLANES, 0:LANES]
                   + res_gla_in[1][0:SUBLANES, 0:LANES] + small_res[1][0:SUBLANES, 0:LANES])
    pieces_ab = reduce_end(rc_ici, rc_mine, others_done, "ab")
    res_ab_in = _adamw(ab_w_in[0], pieces_ab[0], m_ab_w_in[0], v_ab_w_in[0], "adamw_ab_in")
    res_ab_out = _adamw(ab_w_out[0], pieces_ab[1], m_ab_w_out[0], v_ab_w_out[0], "adamw_ab_out")
    res["ab_w_in"] = tuple(res_ab_in[k][None] for k in range(4))
    res["ab_w_out"] = tuple(res_ab_out[k][None] for k in range(4))
    off = 0
    for n in small_names:
        a = w_loc[n]
        nr = a.size // LANES
        res[n] = tuple(small_res[k][off:off + nr].reshape(a.shape) for k in range(4))
        off += _part_rows(a)

    grad_x = dx.reshape(1, T, D_MODEL)
    out = [loss, grad_x]
    for k in range(4):
        out += [res[n][k] for n in WEIGHT_NAMES]
    return tuple(out)
```

```python
import jax
import jax.numpy as jnp
from jax import lax
from jax.experimental import pallas as pl
from jax.experimental.pallas import tpu as pltpu

F32, BF16 = jnp.float32, jnp.bfloat16
HI = lax.Precision.HIGHEST
MESH = pl.DeviceIdType.MESH

D_MODEL = 1024
D_FF = 4096
RG_W = 512
HG_W = 512
CHUNK = 64
EPS = 1e-6
RG_C = 8.0
AB_IN = 3584
GLA_IN = 3104
GLA_IN_PAD = 3200
N_DEV = 8
LANES = 128
SUBLANES = 8
VMEM_LIMIT = 48 * 1024 * 1024

ADAM_LR, ADAM_B1, ADAM_B2, ADAM_EPS, ADAM_WD, ADAM_STEP = 0.001, 0.9, 0.999, 1e-08, 0.01, 10


def _params(*sem):
    return pltpu.CompilerParams(dimension_semantics=sem, vmem_limit_bytes=VMEM_LIMIT)


def _dg(a, b, ca, cb):
    return lax.dot_general(a.astype(BF16), b.astype(BF16), (((ca,), (cb,)), ((), ())),
                           preferred_element_type=F32)


@jax.custom_vjp
def _mm_nn(a, b):
    return _dg(a, b, 1, 0)


_mm_nn.defvjp(lambda a, b: (_dg(a, b, 1, 0), (a, b)),
              lambda res, g: (_dg(g, res[1], 1, 1), _dg(res[0], g, 0, 0)))


@jax.custom_vjp
def _mm_nt(a, b):
    return _dg(a, b, 1, 1)


_mm_nt.defvjp(lambda a, b: (_dg(a, b, 1, 1), (a, b)),
              lambda res, g: (_dg(g, res[1], 1, 0), _dg(g, res[0], 0, 0)))


@jax.custom_vjp
def _mm_tn(a, b):
    return _dg(a, b, 0, 0)


_mm_tn.defvjp(lambda a, b: (_dg(a, b, 0, 0), (a, b)),
              lambda res, g: (_dg(res[1], g, 1, 1), _dg(res[0], g, 1, 0)))


@jax.custom_vjp
def _cum(tri, tri_t, x):
    return jnp.dot(tri, x, precision=HI, preferred_element_type=F32)


_cum.defvjp(lambda tri, tri_t, x: (jnp.dot(tri, x, precision=HI, preferred_element_type=F32), (tri, tri_t)),
            lambda res, g: (jnp.zeros_like(res[0]), jnp.zeros_like(res[1]),
                            jnp.dot(res[1], g, precision=HI, preferred_element_type=F32)))


def _sig(x):
    return 1.0 / (1.0 + jnp.exp(-x))


def _gelu(x):
    return 0.5 * x * (1.0 + jnp.tanh(0.7978845608028654 * (x + 0.044715 * (x * x * x))))


def _softplus(z):
    return jnp.maximum(z, 0.0) + jnp.log(1.0 + jnp.exp(-jnp.abs(z)))


def _rms(x):
    return lax.rsqrt(jnp.mean(x * x, axis=-1, keepdims=True) + EPS)


def _rmsnorm_bwd(x, gain, dy):
    r = _rms(x)
    xh = x * r
    dgain = jnp.sum(dy * xh, axis=0, keepdims=True)
    dxh = dy * gain
    dx = r * (dxh - xh * jnp.mean(dxh * xh, axis=-1, keepdims=True))
    return dx, dgain


def _headnorm(o, gain, n_heads, hd):
    parts = []
    for h in range(n_heads):
        oh = o[:, h * hd:(h + 1) * hd]
        parts.append(oh * _rms(oh))
    return jnp.concatenate(parts, axis=1) * gain


def _tri_consts(d):
    row = lax.broadcasted_iota(jnp.int32, (CHUNK, CHUNK), 0)
    col = lax.broadcasted_iota(jnp.int32, (CHUNK, CHUNK), 1)
    ge = (row >= col).astype(F32)
    le = (row <= col).astype(F32)
    r1 = lax.broadcasted_iota(jnp.int32, (CHUNK, 1), 0)
    if d == 0:
        return ge, le, (r1 <= CHUNK // 2).astype(F32)
    return le, ge, (r1 >= CHUNK // 2 - 1).astype(F32)


def _chunk_core(qh, k, v, logf, st_prev, tri, tri_t, mref, n_heads, dk, dv):
    cum = _cum(tri, tri_t, logf)
    ref = jnp.sum(logf * mref, axis=0, keepdims=True)
    last = jnp.sum(logf, axis=0, keepdims=True)
    q_in = qh * jnp.exp(cum - ref)
    k_in = k * jnp.exp(ref - cum)
    k_st = k * jnp.exp(last - cum)
    q_dec = qh * jnp.exp(cum)
    decay = jnp.exp(last)
    outs, sts = [], []
    for h in range(n_heads):
        sk = slice(h * dk, (h + 1) * dk)
        sv = slice(h * dv, (h + 1) * dv)
        sc = _mm_nt(q_in[:, sk], k_in[:, sk]) * tri
        o = _mm_nn(sc, v[:, sv]) + _mm_nt(q_dec[:, sk], st_prev[h])
        sts.append(st_prev[h] * decay[:, sk] + _mm_tn(v[:, sv], k_st[:, sk]))
        outs.append(o)
    return jnp.concatenate(outs, axis=1), tuple(sts)


def _hg_chunk(q, f, v, l0, l1, st_prev, tri, tri_t, mref):
    lb = _sig(l0 - l1)
    sg = _sig(f)
    qh = q * _sig(q)
    logf = jnp.log(lb + (1.0 - lb) * sg)
    k = (1.0 - lb) * (1.0 - sg)
    return _chunk_core(qh, k, v, logf, st_prev, tri, tri_t, mref, 4, 128, 128)


def _gla_chunk(q, k, v, z, st_prev, tri, tri_t, mref):
    logf = (jnp.minimum(z, 0.0) - jnp.log(1.0 + jnp.exp(-jnp.abs(z)))) * (1.0 / 16.0)
    qh = q * (128.0 ** -0.5)
    return _chunk_core(qh, k, v, logf, st_prev, tri, tri_t, mref, 4, 128, 256)


def _rg_gates(xc, wbd, bias, lam):
    z = _mm_nn(xc, wbd) + bias
    r = _sig(z[:, :RG_W])
    i = _sig(z[:, RG_W:])
    log_a = -RG_C * r * _softplus(-lam)
    a = jnp.exp(log_a)
    x2 = 2.0 * log_a
    neg_expm1 = jnp.where(x2 > -1e-2, -(x2 + 0.5 * x2 * x2 + x2 * x2 * x2 * (1.0 / 6.0)), 1.0 - jnp.exp(x2))
    u = jnp.sqrt(neg_expm1) * (i * xc)
    return a, u


def _l0_combine(hf, hb, ga, of, ob, g, gain):
    ya = (hf + hb) * _gelu(ga)
    yb = _headnorm(of + ob, gain, 4, 128) * (g * _sig(g))
    return jnp.concatenate([ya, yb], axis=1)


def _l1_combine(of, ob, r, gain):
    return _headnorm(of + ob, gain, 4, 256) * (r * _sig(r))


def _norm_matmul(h, gain, w, name):
    T, D = h.shape
    N = w.shape[1]
    tm = min(512, T)

    def body(h_ref, g_ref, w_ref, o_ref, y_ref):
        x = h_ref[...]
        y = (x * _rms(x) * g_ref[...]).astype(BF16)
        y_ref[...] = y
        o_ref[...] = jnp.dot(y, w_ref[...], preferred_element_type=F32)

    return pl.pallas_call(
        body, name=name, grid=(T // tm,),
        in_specs=[pl.BlockSpec((tm, D), lambda i: (i, 0)), pl.BlockSpec((1, D), lambda i: (0, 0)),
                  pl.BlockSpec((D, N), lambda i: (0, 0))],
        out_specs=[pl.BlockSpec((tm, N), lambda i: (i, 0)), pl.BlockSpec((tm, D), lambda i: (i, 0))],
        out_shape=[jax.ShapeDtypeStruct((T, N), F32), jax.ShapeDtypeStruct((T, D), BF16)],
        compiler_params=_params("parallel"))(h, gain, w)


def _matmul_res(a, w, res, name):
    T, K = a.shape
    N = w.shape[1]
    tm = min(512, T)

    def body(a_ref, w_ref, r_ref, o_ref):
        o_ref[...] = r_ref[...] + jnp.dot(a_ref[...], w_ref[...], preferred_element_type=F32)

    return pl.pallas_call(
        body, name=name, grid=(T // tm,),
        in_specs=[pl.BlockSpec((tm, K), lambda i: (i, 0)), pl.BlockSpec((K, N), lambda i: (0, 0)),
                  pl.BlockSpec((tm, N), lambda i: (i, 0))],
        out_specs=pl.BlockSpec((tm, N), lambda i: (i, 0)),
        out_shape=jax.ShapeDtypeStruct((T, N), F32),
        compiler_params=_params("parallel"))(a, w, res)


def _dgrad(dc, w, name):
    T, N = dc.shape
    K = w.shape[0]
    tm = min(512, T)

    def body(d_ref, w_ref, o_ref):
        o_ref[...] = _dg(d_ref[...], w_ref[...], 1, 1)

    return pl.pallas_call(
        body, name=name, grid=(T // tm,),
        in_specs=[pl.BlockSpec((tm, N), lambda i: (i, 0)), pl.BlockSpec((K, N), lambda i: (0, 0))],
        out_specs=pl.BlockSpec((tm, K), lambda i: (i, 0)),
        out_shape=jax.ShapeDtypeStruct((T, K), F32),
        compiler_params=_params("parallel"))(dc, w)


def _dgrad_norm(dproj, w, h, gain, dres, name):
    T, N = dproj.shape
    D = w.shape[0]
    tm = min(512, T)

    def body(dp_ref, w_ref, h_ref, g_ref, dr_ref, dh_ref, dhb_ref, dg_ref):
        @pl.when(pl.program_id(0) == 0)
        def _():
            dg_ref[...] = jnp.zeros_like(dg_ref)

        dy = _dg(dp_ref[...], w_ref[...], 1, 1)
        dx, dgain = _rmsnorm_bwd(h_ref[...], g_ref[...], dy)
        dh = dr_ref[...] + dx
        dh_ref[...] = dh
        dhb_ref[...] = dh.astype(BF16)
        dg_ref[...] += dgain

    return pl.pallas_call(
        body, name=name, grid=(T // tm,),
        in_specs=[pl.BlockSpec((tm, N), lambda i: (i, 0)), pl.BlockSpec((D, N), lambda i: (0, 0)),
                  pl.BlockSpec((tm, D), lambda i: (i, 0)), pl.BlockSpec((1, D), lambda i: (0, 0)),
                  pl.BlockSpec((tm, D), lambda i: (i, 0))],
        out_specs=[pl.BlockSpec((tm, D), lambda i: (i, 0)), pl.BlockSpec((tm, D), lambda i: (i, 0)),
                   pl.BlockSpec((1, D), lambda i: (0, 0))],
        out_shape=[jax.ShapeDtypeStruct((T, D), F32), jax.ShapeDtypeStruct((T, D), BF16),
                   jax.ShapeDtypeStruct((1, D), F32)],
        compiler_params=_params("arbitrary"))(dproj, w, h, gain, dres)


def _wgrad(a, b, tn, name, sharded_cols=False, behind=None):
    T, K = a.shape
    N = b.shape[1]
    tk = min(1024, K)

    def body(a_ref, b_ref, *rest):
        rest[-1][...] = _dg(a_ref[...], b_ref[...], 0, 0)

    if sharded_cols:
        out_spec = pl.BlockSpec((None, tk, tn), lambda k, n: (n, k, 0))
        out_shape = jax.ShapeDtypeStruct((N // tn, K, tn), F32)
    else:
        out_spec = pl.BlockSpec((tk, tn), lambda k, n: (k, n))
        out_shape = jax.ShapeDtypeStruct((K, N), F32)
    in_specs = [pl.BlockSpec((T, tk), lambda k, n: (0, k)), pl.BlockSpec((T, tn), lambda k, n: (0, n))]
    args = [a, b]
    if behind is not None:
        in_specs.append(pl.BlockSpec((SUBLANES, LANES), lambda k, n: (0, 0)))
        args.append(behind)
    return pl.pallas_call(
        body, name=name, grid=(K // tk, N // tn), in_specs=in_specs, out_specs=out_spec, out_shape=out_shape,
        compiler_params=_params("parallel", "parallel"))(*args)


def _resident(shape):
    return pl.BlockSpec(shape, lambda i: (0,) * len(shape), pipeline_mode=pl.Buffered(1))


def _mlp_fwd(h, gain, w1g, w2, name):
    T, D = h.shape
    nf, _, tf = w1g.shape
    tm = min(512, T)

    def body(h_ref, g_ref, w1_ref, w2_ref, o_ref, pre_ref, y_ref):
        x = h_ref[...]
        y = (x * _rms(x) * g_ref[...]).astype(BF16)
        y_ref[...] = y
        acc = x
        for j in range(nf):
            cols = slice(j * tf, (j + 1) * tf)
            pre = jnp.dot(y, w1_ref[j], preferred_element_type=F32)
            pre_ref[:, cols] = pre.astype(BF16)
            act = jnp.square(jnp.maximum(pre, 0.0)).astype(BF16)
            acc = acc + jnp.dot(act, w2_ref[cols, :], preferred_element_type=F32)
        o_ref[...] = acc

    return pl.pallas_call(
        body, name=name, grid=(T // tm,),
        in_specs=[pl.BlockSpec((tm, D), lambda i: (i, 0)), pl.BlockSpec((1, D), lambda i: (0, 0)),
                  _resident(w1g.shape), _resident(w2.shape)],
        out_specs=[pl.BlockSpec((tm, D), lambda i: (i, 0)), pl.BlockSpec((tm, nf * tf), lambda i: (i, 0)),
                   pl.BlockSpec((tm, D), lambda i: (i, 0))],
        out_shape=[jax.ShapeDtypeStruct((T, D), F32), jax.ShapeDtypeStruct((T, nf * tf), BF16),
                   jax.ShapeDtypeStruct((T, D), BF16)],
        compiler_params=_params("parallel"))(h, gain, w1g, w2)


def _mlp_bwd(dout, dout_b, h, gain, pre, w1g, w2, name):
    T, D = h.shape
    nf, _, tf = w1g.shape
    tm = min(256, T)

    def body(do_ref, dob_ref, h_ref, g_ref, pre_ref, w1_ref, w2_ref, dh_ref, dhb_ref, dpre_ref, act_ref, dg_ref):
        @pl.when(pl.program_id(0) == 0)
        def _():
            dg_ref[...] = jnp.zeros_like(dg_ref)

        dob = dob_ref[...]
        dy = None
        for j in range(nf):
            cols = slice(j * tf, (j + 1) * tf)
            rp = jnp.maximum(pre_ref[:, cols].astype(F32), 0.0)
            dpre = (_dg(dob, w2_ref[cols, :], 1, 1) * (2.0 * rp)).astype(BF16)
            dpre_ref[:, cols] = dpre
            act_ref[:, cols] = (rp * rp).astype(BF16)
            part = _dg(dpre, w1_ref[j], 1, 1)
            dy = part if dy is None else dy + part
        dx, dgain = _rmsnorm_bwd(h_ref[...], g_ref[...], dy)
        dh = do_ref[...] + dx
        dh_ref[...] = dh
        dhb_ref[...] = dh.astype(BF16)
        dg_ref[...] += dgain

    tok = lambda w: pl.BlockSpec((tm, w), lambda i: (i, 0))
    return pl.pallas_call(
        body, name=name, grid=(T // tm,),
        in_specs=[tok(D), tok(D), tok(D), pl.BlockSpec((1, D), lambda i: (0, 0)), tok(nf * tf),
                  _resident(w1g.shape), _resident(w2.shape)],
        out_specs=[tok(D), tok(D), tok(nf * tf), tok(nf * tf), pl.BlockSpec((1, D), lambda i: (0, 0))],
        out_shape=[jax.ShapeDtypeStruct((T, D), F32), jax.ShapeDtypeStruct((T, D), BF16),
                   jax.ShapeDtypeStruct((T, nf * tf), BF16),
                   jax.ShapeDtypeStruct((T, nf * tf), BF16), jax.ShapeDtypeStruct((1, D), F32)],
        compiler_params=_params("arbitrary"))(dout, dout_b, h, gain, pre, w1g, w2)


def _final_loss(h, gain, target, name):
    T, D = h.shape
    tm = min(512, T)

    def body(h_ref, g_ref, t_ref, l_ref, dh_ref, dhb_ref, dg_ref):
        @pl.when(pl.program_id(0) == 0)
        def _():
            l_ref[...] = jnp.zeros_like(l_ref)
            dg_ref[...] = jnp.zeros_like(dg_ref)

        x = h_ref[...]
        err = x * _rms(x) * g_ref[...] - t_ref[...]
        l_ref[...] += 0.5 * jnp.sum(jnp.mean(err * err, axis=-1, keepdims=True), axis=0, keepdims=True)
        dx, dgain = _rmsnorm_bwd(x, g_ref[...], err * (1.0 / D))
        dh_ref[...] = dx
        dhb_ref[...] = dx.astype(BF16)
        dg_ref[...] += dgain

    return pl.pallas_call(
        body, name=name, grid=(T // tm,),
        in_specs=[pl.BlockSpec((tm, D), lambda i: (i, 0)), pl.BlockSpec((1, D), lambda i: (0, 0)),
                  pl.BlockSpec((tm, D), lambda i: (i, 0))],
        out_specs=[pl.BlockSpec((SUBLANES, LANES), lambda i: (0, 0)), pl.BlockSpec((tm, D), lambda i: (i, 0)),
                   pl.BlockSpec((tm, D), lambda i: (i, 0)), pl.BlockSpec((1, D), lambda i: (0, 0))],
        out_shape=[jax.ShapeDtypeStruct((SUBLANES, LANES), F32), jax.ShapeDtypeStruct((T, D), F32),
                   jax.ShapeDtypeStruct((T, D), BF16), jax.ShapeDtypeStruct((1, D), F32)],
        compiler_params=_params("arbitrary"))(h, gain, target)


def _halo_specs(tm, T, width, col, tile=lambda i: i):
    r8 = tm // SUBLANES
    nb8 = T // SUBLANES
    return [pl.BlockSpec((tm, width), lambda i: (tile(i), col)),
            pl.BlockSpec((SUBLANES, width), lambda i: (jnp.maximum(tile(i) * r8 - 1, 0), col)),
            pl.BlockSpec((SUBLANES, width), lambda i: (jnp.minimum((tile(i) + 1) * r8, nb8 - 1), col))]


def _ext(cur, prev, nxt, has_prev, has_next):
    return jnp.concatenate([jnp.where(has_prev, prev, 0.0), cur, jnp.where(has_next, nxt, 0.0)], axis=0)


def _shifted(ext, offset, tm):
    n = ext.shape[0]
    sh = (-offset) % n
    r = ext if sh == 0 else pltpu.roll(ext, sh, 0)
    return r[SUBLANES:SUBLANES + tm]


def _rg_conv_fwd(proj, cw8, cb, name):
    T = proj.shape[0]
    tm = min(512, T)
    nT = T // tm

    def body(cur_ref, prev_ref, next_ref, w_ref, b_ref, o_ref):
        i = pl.program_id(0)
        ext = _ext(cur_ref[...], prev_ref[...], next_ref[...], i > 0, i < nT - 1)
        acc = jnp.broadcast_to(b_ref[...], (tm, RG_W))
        for k in range(4):
            acc = acc + w_ref[k:k + 1, :] * _shifted(ext, k - 2, tm)
        o_ref[...] = acc

    return pl.pallas_call(
        body, name=name, grid=(nT,),
        in_specs=_halo_specs(tm, T, RG_W, 0) + [pl.BlockSpec((SUBLANES, RG_W), lambda i: (0, 0)),
                                                pl.BlockSpec((1, RG_W), lambda i: (0, 0))],
        out_specs=pl.BlockSpec((tm, RG_W), lambda i: (i, 0)),
        out_shape=jax.ShapeDtypeStruct((T, RG_W), F32),
        compiler_params=_params("parallel"))(proj, proj, proj, cw8, cb)


def _rg_conv_bwd(dxc, proj, cw8, name):
    T = proj.shape[0]
    tm = min(512, T)
    nT = T // tm

    def body(a0, p0, n0, a1, p1, n1, xa, xp, xn, w_ref, dxa_ref, dw_ref, db_ref):
        i = pl.program_id(0)

        @pl.when(i == 0)
        def _():
            dw_ref[...] = jnp.zeros_like(dw_ref)
            db_ref[...] = jnp.zeros_like(db_ref)

        has_p, has_n = i > 0, i < nT - 1
        cur = a0[...] + a1[...]
        dext = _ext(cur, p0[...] + p1[...], n0[...] + n1[...], has_p, has_n)
        xext = _ext(xa[...], xp[...], xn[...], has_p, has_n)
        acc = jnp.zeros((tm, RG_W), F32)
        rows = []
        for k in range(4):
            acc = acc + w_ref[k:k + 1, :] * _shifted(dext, 2 - k, tm)
            rows.append(jnp.sum(cur * _shifted(xext, k - 2, tm), axis=0, keepdims=True))
        dxa_ref[...] = acc
        dw_ref[...] += jnp.concatenate(rows + [jnp.zeros((4, RG_W), F32)], axis=0)
        db_ref[...] += jnp.sum(cur, axis=0, keepdims=True)

    return pl.pallas_call(
        body, name=name, grid=(nT,),
        in_specs=(_halo_specs(tm, T, RG_W, 0) + _halo_specs(tm, T, RG_W, 0)
                  + _halo_specs(tm, T, RG_W, 0) + [pl.BlockSpec((SUBLANES, RG_W), lambda i: (0, 0))]),
        out_specs=[pl.BlockSpec((tm, RG_W), lambda i: (i, 0)), pl.BlockSpec((SUBLANES, RG_W), lambda i: (0, 0)),
                   pl.BlockSpec((1, RG_W), lambda i: (0, 0))],
        out_shape=[jax.ShapeDtypeStruct((T, RG_W), F32), jax.ShapeDtypeStruct((SUBLANES, RG_W), F32),
                   jax.ShapeDtypeStruct((1, RG_W), F32)],
        compiler_params=_params("arbitrary"))(dxc[0], dxc[0], dxc[0], dxc[1], dxc[1], dxc[1], proj, proj, proj, cw8)


def _local_scan(a, b, ascending):
    n = a.shape[0]
    pos = jnp.bitwise_and(lax.broadcasted_iota(jnp.int32, a.shape, 0), SUBLANES - 1)
    for s in (1, 2, 4):
        sh = s if ascending else n - s
        ok = (pos >= s) if ascending else (pos < SUBLANES - s)
        a_sh, b_sh = pltpu.roll(a, sh, 0), pltpu.roll(b, sh, 0)
        b = jnp.where(ok, a * b_sh + b, b)
        a = jnp.where(ok, a * a_sh, a)
    return a, b


def _group_scan(chains, a_sc, b_sc, carry, n_groups):
    def step(g, hs):
        new = []
        for (d, out_ref, asc), h in zip(chains, hs):
            r0 = pl.multiple_of((g if asc else n_groups - 1 - g) * SUBLANES, SUBLANES)
            out_ref[pl.ds(r0, SUBLANES), :] = a_sc[d, pl.ds(r0, SUBLANES), :] * h + b_sc[d, pl.ds(r0, SUBLANES), :]
            new.append(out_ref[pl.ds(r0 + (SUBLANES - 1 if asc else 0), 1), :])
        return tuple(new)

    hs = lax.fori_loop(0, n_groups, step, tuple(carry[d, 0:1, :] for d, _, _ in chains))
    for (d, _, _), h in zip(chains, hs):
        carry[d, 0:1, :] = h


def _rg_scan_fwd(xc, wbd, bias, lam, name):
    T = xc.shape[0]
    tm = min(512, T)
    nT = T // tm

    def body(xf_ref, xb_ref, w_ref, b_ref, lam_ref, hf_ref, hb_ref, a_sc, b_sc, carry):
        @pl.when(pl.program_id(0) == 0)
        def _():
            carry[...] = jnp.zeros_like(carry)

        for d, x_ref in enumerate((xf_ref, xb_ref)):
            a, u = _rg_gates(x_ref[...], w_ref[d], b_ref[d], lam_ref[d])
            a_sc[d], b_sc[d] = _local_scan(a, u, d == 0)
        _group_scan(((0, hf_ref, True), (1, hb_ref, False)), a_sc, b_sc, carry, tm // SUBLANES)

    full = lambda a: pl.BlockSpec(a.shape, lambda i: (0,) * len(a.shape))
    res = pl.pallas_call(
        body, name=name, grid=(nT,),
        in_specs=[pl.BlockSpec((tm, RG_W), lambda i: (i, 0)), pl.BlockSpec((tm, RG_W), lambda i: (nT - 1 - i, 0)),
                  full(wbd), full(bias), full(lam)],
        out_specs=[pl.BlockSpec((tm, RG_W), lambda i: (i, 0)), pl.BlockSpec((tm, RG_W), lambda i: (nT - 1 - i, 0))],
        out_shape=[jax.ShapeDtypeStruct((T, RG_W), F32)] * 2,
        scratch_shapes=[pltpu.VMEM((2, tm, RG_W), F32), pltpu.VMEM((2, tm, RG_W), F32),
                        pltpu.VMEM((2, SUBLANES, RG_W), F32)],
        compiler_params=_params("arbitrary"))(xc, xc, wbd, bias, lam)
    return res[0], res[1]


def _rg_scan_bwd(xc, wbd, bias, lam, hs, dho, name):
    T = xc.shape[0]
    tm = min(256, T)
    nT = T // tm
    tiles = (lambda i: nT - 1 - i, lambda i: i)

    def body(xf_ref, xb_ref, w_ref, b_ref, lam_ref, hfc, hfp, hfn, hbc, hbp, hbn, dof_ref, dob_ref,
             dxf_ref, dxb_ref, dw_ref, db_ref, dlam_ref, a_sc, b_sc, y_sc, carry):
        i = pl.program_id(0)

        @pl.when(i == 0)
        def _():
            carry[...] = jnp.zeros_like(carry)
            dw_ref[...] = jnp.zeros_like(dw_ref)
            db_ref[...] = jnp.zeros_like(db_ref)
            dlam_ref[...] = jnp.zeros_like(dlam_ref)

        vjps, entering = [], []
        for d, (x_ref, do_ref) in enumerate(((xf_ref, dof_ref), (xb_ref, dob_ref))):
            (a, _), vjp = jax.vjp(_rg_gates, x_ref[...], w_ref[d].astype(F32), b_ref[d], lam_ref[d])
            vjps.append(vjp)
            entering.append(carry[d, 0:1, :])
            a_sc[d], b_sc[d] = _local_scan(a, a * do_ref[...], d == 1)
        _group_scan(((0, y_sc.at[0], False), (1, y_sc.at[1], True)), a_sc, b_sc, carry, tm // SUBLANES)

        row = lax.broadcasted_iota(jnp.int32, (tm, RG_W), 0)
        for d, (do_ref, dx_ref, hc, hp, hn, ti) in enumerate(
                ((dof_ref, dxf_ref, hfc, hfp, hfn, nT - 1 - i), (dob_ref, dxb_ref, hbc, hbp, hbn, i))):
            y = y_sc[d]
            if d == 0:
                y_next = jnp.where(row == tm - 1, entering[d], pltpu.roll(y, tm - 1, 0))
            else:
                y_next = jnp.where(row == 0, entering[d], pltpu.roll(y, 1, 0))
            dtot = do_ref[...] + y_next
            ext = _ext(hc[...], hp[...], hn[...], ti > 0, ti < nT - 1)
            hprev = _shifted(ext, -1 if d == 0 else 1, tm)
            dxc, dw, db, dlam = vjps[d]((dtot * hprev, dtot))
            dx_ref[...] = dxc
            dw_ref[d] += dw
            db_ref[d] += db
            dlam_ref[d] += dlam

    full = lambda a: pl.BlockSpec(a.shape, lambda i: (0,) * len(a.shape))
    tok = lambda d: pl.BlockSpec((tm, RG_W), lambda i: (tiles[d](i), 0))
    acc_shapes = [jax.ShapeDtypeStruct((2, RG_W, 2 * RG_W), F32), jax.ShapeDtypeStruct((2, 1, 2 * RG_W), F32),
                  jax.ShapeDtypeStruct((2, 1, RG_W), F32)]
    res = pl.pallas_call(
        body, name=name, grid=(nT,),
        in_specs=([tok(0), tok(1), full(wbd), full(bias), full(lam)]
                  + _halo_specs(tm, T, RG_W, 0, tiles[0]) + _halo_specs(tm, T, RG_W, 0, tiles[1]) + [tok(0), tok(1)]),
        out_specs=[tok(0), tok(1)] + [full(s) for s in acc_shapes],
        out_shape=[jax.ShapeDtypeStruct((T, RG_W), F32)] * 2 + acc_shapes,
        scratch_shapes=[pltpu.VMEM((2, tm, RG_W), F32), pltpu.VMEM((2, tm, RG_W), F32),
                        pltpu.VMEM((2, tm, RG_W), F32), pltpu.VMEM((2, SUBLANES, RG_W), F32)],
        compiler_params=_params("arbitrary"))(xc, xc, wbd, bias, lam, hs[0], hs[0], hs[0], hs[1], hs[1], hs[1],
                                              dho, dho)
    return (res[0], res[1]), res[2], res[3], res[4]


def _chunk_rows(n_chunks, reverse):
    up, down = (lambda c: c), (lambda c: n_chunks - 1 - c)
    return (down, up) if reverse else (up, down)


def _hg_fwd(proj, l0, l1, name):
    T = proj.shape[0]
    nC = T // CHUNK
    H, dk, dv = 4, 128, 128
    rows = _chunk_rows(nC, False)

    def body(qf, ff, vf, qb, fb, vb, l0_ref, l1_ref, of, ob, spf, spb, st):
        @pl.when(pl.program_id(0) == 0)
        def _():
            st[...] = jnp.zeros_like(st)

        for d, (q, f, v, o, sp) in enumerate(((qf, ff, vf, of, spf), (qb, fb, vb, ob, spb))):
            tri, tri_t, mref = _tri_consts(d)
            stp = tuple(st[d, h] for h in range(H))
            sp[...] = st[d]
            o_val, stn = _hg_chunk(q[...], f[...], v[...], l0_ref[...], l1_ref[...], stp, tri, tri_t, mref)
            o[...] = o_val
            for h in range(H):
                st[d, h] = stn[h]

    tok = lambda d, col: pl.BlockSpec((CHUNK, HG_W), lambda c: (rows[d](c), col))
    par = pl.BlockSpec((1, HG_W), lambda c: (0, 0))
    state = lambda d: pl.BlockSpec((None, H, dv, dk), lambda c: (rows[d](c), 0, 0, 0))
    res = pl.pallas_call(
        body, name=name, grid=(nC,),
        in_specs=[tok(0, 2), tok(0, 3), tok(0, 5), tok(1, 2), tok(1, 4), tok(1, 5), par, par],
        out_specs=[tok(0, 0), tok(1, 0), state(0), state(1)],
        out_shape=[jax.ShapeDtypeStruct((T, H * dv), F32)] * 2 + [jax.ShapeDtypeStruct((nC, H, dv, dk), F32)] * 2,
        scratch_shapes=[pltpu.VMEM((2, H, dv, dk), F32)],
        compiler_params=_params("arbitrary"))(proj, proj, proj, proj, proj, proj, l0, l1)
    return (res[0], res[1]), (res[2], res[3])


def _hg_bwd(proj, l0, l1, sprev, do, name):
    T = proj.shape[0]
    nC = T // CHUNK
    H, dk, dv = 4, 128, 128
    rows = _chunk_rows(nC, True)

    def body(qf, ff, vf, qb, fb, vb, l0_ref, l1_ref, spf, spb, dof, dob,
             dqf, dff, dvf, dqb, dfb, dvb, dl0_ref, dl1_ref, dst):
        @pl.when(pl.program_id(0) == 0)
        def _():
            dst[...] = jnp.zeros_like(dst)
            dl0_ref[...] = jnp.zeros_like(dl0_ref)
            dl1_ref[...] = jnp.zeros_like(dl1_ref)

        for d, (q, f, v, sp, do_ref, dq_ref, df_ref, dv_ref) in enumerate(
                ((qf, ff, vf, spf, dof, dqf, dff, dvf), (qb, fb, vb, spb, dob, dqb, dfb, dvb))):
            tri, tri_t, mref = _tri_consts(d)
            fn = lambda q_, f_, v_, a0, a1, stp: _hg_chunk(q_, f_, v_, a0, a1, stp, tri, tri_t, mref)
            stp = tuple(sp[h] for h in range(H))
            _, vjp = jax.vjp(fn, q[...], f[...], v[...], l0_ref[...], l1_ref[...], stp)
            dq, df, dvv, dl0, dl1, dstp = vjp((do_ref[...], tuple(dst[d, h] for h in range(H))))
            dq_ref[...] = dq
            df_ref[...] = df
            dv_ref[...] = dvv
            dl0_ref[d] += dl0
            dl1_ref[d] += dl1
            for h in range(H):
                dst[d, h] = dstp[h]

    tok = lambda d, col: pl.BlockSpec((CHUNK, HG_W), lambda c: (rows[d](c), col))
    par = pl.BlockSpec((1, HG_W), lambda c: (0, 0))
    acc = pl.BlockSpec((2, 1, HG_W), lambda c: (0, 0, 0))
    state = lambda d: pl.BlockSpec((None, H, dv, dk), lambda c: (rows[d](c), 0, 0, 0))
    res = pl.pallas_call(
        body, name=name, grid=(nC,),
        in_specs=[tok(0, 2), tok(0, 3), tok(0, 5), tok(1, 2), tok(1, 4), tok(1, 5), par, par,
                  state(0), state(1), tok(0, 0), tok(1, 0)],
        out_specs=[tok(0, 0)] * 3 + [tok(1, 0)] * 3 + [acc, acc],
        out_shape=[jax.ShapeDtypeStruct((T, HG_W), F32)] * 6 + [jax.ShapeDtypeStruct((2, 1, HG_W), F32)] * 2,
        scratch_shapes=[pltpu.VMEM((2, H, dv, dk), F32)],
        compiler_params=_params("arbitrary"))(proj, proj, proj, proj, proj, proj, l0, l1, sprev[0], sprev[1], do, do)
    return (res[0], res[3]), (res[1], res[4]), (res[2], res[5]), res[6], res[7]


def _gate_logits(proj, wup, bg, name):
    T = proj.shape[0]
    tm = min(512, T)

    def body(lr_ref, w_ref, b_ref, z_ref, lrb_ref):
        lr = lr_ref[...].astype(BF16)
        lrb_ref[...] = lr
        for d in range(2):
            z_ref[d] = _dg(lr, w_ref[d], 1, 0) + b_ref[d]

    return pl.pallas_call(
        body, name=name, grid=(T // tm,),
        in_specs=[pl.BlockSpec((tm, LANES), lambda i: (i, 24)), pl.BlockSpec((2, LANES, 512), lambda i: (0, 0, 0)),
                  pl.BlockSpec((2, 1, 512), lambda i: (0, 0, 0))],
        out_specs=[pl.BlockSpec((2, tm, 512), lambda i: (0, i, 0)), pl.BlockSpec((tm, LANES), lambda i: (i, 0))],
        out_shape=[jax.ShapeDtypeStruct((2, T, 512), F32), jax.ShapeDtypeStruct((T, LANES), BF16)],
        compiler_params=_params("parallel"))(proj, wup, bg)


def _gate_logits_bwd(dz, wup, name):
    T = dz[0].shape[0]
    tm = min(512, T)

    def body(dzf_ref, dzb_ref, w_ref, dlr_ref, db_ref, dzb16_ref):
        @pl.when(pl.program_id(0) == 0)
        def _():
            db_ref[...] = jnp.zeros_like(db_ref)

        acc = jnp.zeros((tm, LANES), F32)
        for d, dz_ref in enumerate((dzf_ref, dzb_ref)):
            g = dz_ref[...]
            gb = g.astype(BF16)
            dzb16_ref[d] = gb
            acc = acc + _dg(gb, w_ref[d], 1, 1)
            db_ref[d] += jnp.sum(g, axis=0, keepdims=True)
        dlr_ref[...] = acc

    tok = pl.BlockSpec((tm, 512), lambda i: (i, 0))
    return pl.pallas_call(
        body, name=name, grid=(T // tm,),
        in_specs=[tok, tok, pl.BlockSpec((2, LANES, 512), lambda i: (0, 0, 0))],
        out_specs=[pl.BlockSpec((tm, LANES), lambda i: (i, 0)), pl.BlockSpec((2, 1, 512), lambda i: (0, 0, 0)),
                   pl.BlockSpec((2, tm, 512), lambda i: (0, i, 0))],
        out_shape=[jax.ShapeDtypeStruct((T, LANES), F32), jax.ShapeDtypeStruct((2, 1, 512), F32),
                   jax.ShapeDtypeStruct((2, T, 512), BF16)],
        compiler_params=_params("arbitrary"))(dz[0], dz[1], wup)


def _gla_fwd(proj, z, name):
    T = proj.shape[0]
    nC = T // CHUNK
    H, dk, dv = 4, 128, 256
    rows = _chunk_rows(nC, False)

    def body(qf, kf, vf, zf, qb, kb, vb, zb, of, ob, spf, spb, st):
        @pl.when(pl.program_id(0) == 0)
        def _():
            st[...] = jnp.zeros_like(st)

        for d, (q, k, v, z_ref, o, sp) in enumerate(((qf, kf, vf, zf, of, spf), (qb, kb, vb, zb, ob, spb))):
            tri, tri_t, mref = _tri_consts(d)
            stp = tuple(st[d, h] for h in range(H))
            sp[...] = st[d]
            o_val, stn = _gla_chunk(q[...], k[...], v[...], z_ref[...], stp, tri, tri_t, mref)
            o[...] = o_val
            for h in range(H):
                st[d, h] = stn[h]

    tok = lambda d, w, col: pl.BlockSpec((CHUNK, w), lambda c: (rows[d](c), col))
    gate = lambda d: pl.BlockSpec((None, CHUNK, 512), lambda c: (d, rows[d](c), 0))
    state = lambda d: pl.BlockSpec((None, H, dv, dk), lambda c: (rows[d](c), 0, 0, 0))
    res = pl.pallas_call(
        body, name=name, grid=(nC,),
        in_specs=[tok(0, 512, 0), tok(0, 512, 1), tok(0, 1024, 1), gate(0),
                  tok(1, 512, 0), tok(1, 512, 1), tok(1, 1024, 1), gate(1)],
        out_specs=[tok(0, H * dv, 0), tok(1, H * dv, 0), state(0), state(1)],
        out_shape=[jax.ShapeDtypeStruct((T, H * dv), F32)] * 2 + [jax.ShapeDtypeStruct((nC, H, dv, dk), F32)] * 2,
        scratch_shapes=[pltpu.VMEM((2, H, dv, dk), F32)],
        compiler_params=_params("arbitrary"))(proj, proj, proj, z, proj, proj, proj, z)
    return (res[0], res[1]), (res[2], res[3])


def _gla_bwd(proj, z, sprev, do, name):
    T = proj.shape[0]
    nC = T // CHUNK
    H, dk, dv = 4, 128, 256
    rows = _chunk_rows(nC, True)

    def body(qf, kf, vf, zf, qb, kb, vb, zb, spf, spb, dof, dob,
             dqf, dkf, dvf, dzf, dqb, dkb, dvb, dzb, dst):
        @pl.when(pl.program_id(0) == 0)
        def _():
            dst[...] = jnp.zeros_like(dst)

        for d, (q, k, v, z_ref, sp, do_ref, dq_ref, dk_ref, dv_ref, dz_ref) in enumerate(
                ((qf, kf, vf, zf, spf, dof, dqf, dkf, dvf, dzf), (qb, kb, vb, zb, spb, dob, dqb, dkb, dvb, dzb))):
            tri, tri_t, mref = _tri_consts(d)
            fn = lambda q_, k_, v_, z_, stp: _gla_chunk(q_, k_, v_, z_, stp, tri, tri_t, mref)
            stp = tuple(sp[h] for h in range(H))
            _, vjp = jax.vjp(fn, q[...], k[...], v[...], z_ref[...], stp)
            dq, dkk, dvv, dzz, dstp = vjp((do_ref[...], tuple(dst[d, h] for h in range(H))))
            dq_ref[...] = dq
            dk_ref[...] = dkk
            dv_ref[...] = dvv
            dz_ref[...] = dzz
            for h in range(H):
                dst[d, h] = dstp[h]

    tok = lambda d, w, col: pl.BlockSpec((CHUNK, w), lambda c: (rows[d](c), col))
    gate = lambda d: pl.BlockSpec((None, CHUNK, 512), lambda c: (d, rows[d](c), 0))
    state = lambda d: pl.BlockSpec((None, H, dv, dk), lambda c: (rows[d](c), 0, 0, 0))
    outs = lambda d: [tok(d, 512, 0), tok(d, 512, 0), tok(d, 1024, 0), tok(d, 512, 0)]
    shapes = [jax.ShapeDtypeStruct((T, 512), F32), jax.ShapeDtypeStruct((T, 512), F32),
              jax.ShapeDtypeStruct((T, 1024), F32), jax.ShapeDtypeStruct((T, 512), F32)]
    res = pl.pallas_call(
        body, name=name, grid=(nC,),
        in_specs=[tok(0, 512, 0), tok(0, 512, 1), tok(0, 1024, 1), gate(0),
                  tok(1, 512, 0), tok(1, 512, 1), tok(1, 1024, 1), gate(1),
                  state(0), state(1), tok(0, H * dv, 0), tok(1, H * dv, 0)],
        out_specs=outs(0) + outs(1), out_shape=shapes + shapes,
        scratch_shapes=[pltpu.VMEM((2, H, dv, dk), F32)],
        compiler_params=_params("arbitrary"))(proj, proj, proj, z, proj, proj, proj, z, sprev[0], sprev[1], do, do)
    return (res[0], res[4]), (res[1], res[5]), (res[2], res[6]), (res[3], res[7])


def _l0_combine_fwd(hs, proj, o, gain, name):
    T = proj.shape[0]
    tm = min(512, T)

    def body(hf, hb, ga, of, ob, g, gn, out):
        out[...] = _l0_combine(hf[...], hb[...], ga[...], of[...], ob[...], g[...], gn[...]).astype(BF16)

    tok = pl.BlockSpec((tm, 512), lambda i: (i, 0))
    return pl.pallas_call(
        body, name=name, grid=(T // tm,),
        in_specs=[tok, tok, pl.BlockSpec((tm, 512), lambda i: (i, 1)), tok, tok,
                  pl.BlockSpec((tm, 512), lambda i: (i, 6)), pl.BlockSpec((1, 512), lambda i: (0, 0))],
        out_specs=pl.BlockSpec((tm, 1024), lambda i: (i, 0)),
        out_shape=jax.ShapeDtypeStruct((T, 1024), BF16),
        compiler_params=_params("parallel"))(hs[0], hs[1], proj, o[0], o[1], proj, gain)


def _l0_combine_bwd(hs, proj, o, gain, dmix, name):
    T = proj.shape[0]
    tm = min(512, T)

    def body(hf, hb, ga, of, ob, g, gn, dm, dho_ref, dga_ref, do_ref, dg_ref, dgn_ref):
        @pl.when(pl.program_id(0) == 0)
        def _():
            dgn_ref[...] = jnp.zeros_like(dgn_ref)

        _, vjp = jax.vjp(_l0_combine, hf[...], hb[...], ga[...], of[...], ob[...], g[...], gn[...])
        dhf, _, dga, dof, _, dg, dgn = vjp(dm[...])
        dho_ref[...] = dhf
        dga_ref[...] = dga
        do_ref[...] = dof
        dg_ref[...] = dg
        dgn_ref[...] += dgn

    tok = lambda: pl.BlockSpec((tm, 512), lambda i: (i, 0))
    return pl.pallas_call(
        body, name=name, grid=(T // tm,),
        in_specs=[tok(), tok(), pl.BlockSpec((tm, 512), lambda i: (i, 1)), tok(), tok(),
                  pl.BlockSpec((tm, 512), lambda i: (i, 6)), pl.BlockSpec((1, 512), lambda i: (0, 0)),
                  pl.BlockSpec((tm, 1024), lambda i: (i, 0))],
        out_specs=[tok(), tok(), tok(), tok(), pl.BlockSpec((1, 512), lambda i: (0, 0))],
        out_shape=[jax.ShapeDtypeStruct((T, 512), F32)] * 4 + [jax.ShapeDtypeStruct((1, 512), F32)],
        compiler_params=_params("arbitrary"))(hs[0], hs[1], proj, o[0], o[1], proj, gain, dmix)


def _l0_assemble(dxa, dga, dq, df, dv, dg, name):
    T = dxa.shape[0]
    tm = min(512, T)

    def body(xa, ga, q0, q1, f0, f1, v0, v1, g, out):
        out[...] = jnp.concatenate([xa[...], ga[...], q0[...] + q1[...], f0[...], f1[...], v0[...] + v1[...],
                                    g[...]], axis=1).astype(BF16)

    tok = lambda: pl.BlockSpec((tm, 512), lambda i: (i, 0))
    return pl.pallas_call(
        body, name=name, grid=(T // tm,),
        in_specs=[tok() for _ in range(9)],
        out_specs=pl.BlockSpec((tm, AB_IN), lambda i: (i, 0)),
        out_shape=jax.ShapeDtypeStruct((T, AB_IN), BF16),
        compiler_params=_params("parallel"))(dxa, dga, dq[0], dq[1], df[0], df[1], dv[0], dv[1], dg)


def _l1_combine_fwd(o, proj, gain, name):
    T = proj.shape[0]
    tm = min(512, T)

    def body(of, ob, r, gn, out):
        out[...] = _l1_combine(of[...], ob[...], r[...], gn[...]).astype(BF16)

    tok = pl.BlockSpec((tm, 1024), lambda i: (i, 0))
    return pl.pallas_call(
        body, name=name, grid=(T // tm,),
        in_specs=[tok, tok, pl.BlockSpec((tm, 1024), lambda i: (i, 2)), pl.BlockSpec((1, 1024), lambda i: (0, 0))],
        out_specs=pl.BlockSpec((tm, 1024), lambda i: (i, 0)),
        out_shape=jax.ShapeDtypeStruct((T, 1024), BF16),
        compiler_params=_params("parallel"))(o[0], o[1], proj, gain)


def _l1_combine_bwd(o, proj, gain, dmix, name):
    T = proj.shape[0]
    tm = min(512, T)

    def body(of, ob, r, gn, dm, do_ref, dr_ref, dgn_ref):
        @pl.when(pl.program_id(0) == 0)
        def _():
            dgn_ref[...] = jnp.zeros_like(dgn_ref)

        _, vjp = jax.vjp(_l1_combine, of[...], ob[...], r[...], gn[...])
        dof, _, dr, dgn = vjp(dm[...])
        do_ref[...] = dof
        dr_ref[...] = dr
        dgn_ref[...] += dgn

    tok = lambda: pl.BlockSpec((tm, 1024), lambda i: (i, 0))
    return pl.pallas_call(
        body, name=name, grid=(T // tm,),
        in_specs=[tok(), tok(), pl.BlockSpec((tm, 1024), lambda i: (i, 2)),
                  pl.BlockSpec((1, 1024), lambda i: (0, 0)), tok()],
        out_specs=[tok(), tok(), pl.BlockSpec((1, 1024), lambda i: (0, 0))],
        out_shape=[jax.ShapeDtypeStruct((T, 1024), F32)] * 2 + [jax.ShapeDtypeStruct((1, 1024), F32)],
        compiler_params=_params("arbitrary"))(o[0], o[1], proj, gain, dmix)


def _l1_assemble(dq, dk, dv, dr, dlr, name):
    T = dr.shape[0]
    tm = min(512, T)

    def body(q0, q1, k0, k1, v0, v1, r, a, out):
        out[...] = jnp.concatenate([q0[...] + q1[...], k0[...] + k1[...], v0[...] + v1[...], r[...], a[...]],
                                   axis=1).astype(BF16)

    tok = lambda w: pl.BlockSpec((tm, w), lambda i: (i, 0))
    return pl.pallas_call(
        body, name=name, grid=(T // tm,),
        in_specs=[tok(512), tok(512), tok(512), tok(512), tok(1024), tok(1024), tok(1024), tok(LANES)],
        out_specs=pl.BlockSpec((tm, GLA_IN_PAD), lambda i: (i, 0)),
        out_shape=jax.ShapeDtypeStruct((T, GLA_IN_PAD), BF16),
        compiler_params=_params("parallel"))(dq[0], dq[1], dk[0], dk[1], dv[0], dv[1], dr, dlr)


HBM_SPEC = pl.BlockSpec(memory_space=pltpu.HBM)


def _place():
    x, y, c = lax.axis_index("x"), lax.axis_index("y"), lax.axis_index("c")
    return x, y, c


def _allgather_vmem(x_shard, name):
    m_per, n = x_shard.shape

    def body(x_ref, out_ref, send_sems, recv_sems, local_sem):
        x, y, c = _place()
        me, sibling = (x, y, c), (x, y, 1 - c)
        chips = [(1 - x, y), (x, 1 - y), (1 - x, 1 - y)]

        def rows(px, py, pc):
            return out_ref.at[pl.ds((4 * px + 2 * py + pc) * m_per, m_per), :]

        def copy(k, block, to, src=None):
            return pltpu.make_async_remote_copy(
                src_ref=rows(*block) if src is None else src, dst_ref=rows(*block),
                send_sem=send_sems.at[k], recv_sem=recv_sems.at[k], device_id=to, device_id_type=MESH)

        mine = pltpu.make_async_copy(x_ref, rows(*me), local_sem)
        mine.start()
        first = [copy(0, me, sibling, src=x_ref)]
        first += [copy(1 + j, me, (*chip, c), src=x_ref) for j, chip in enumerate(chips)]
        for cp in first:
            cp.start()
        passed = [copy(4 + j, (*chip, c), sibling) for j, chip in enumerate(chips)]
        for j, chip in enumerate(chips):
            copy(1 + j, (*chip, c), me).wait_recv()
            passed[j].start()
        copy(0, sibling, me).wait_recv()
        for j, chip in enumerate(chips):
            copy(4 + j, (*chip, 1 - c), me).wait_recv()
        for cp in first + passed:
            cp.wait_send()
        mine.wait()

    vm = pl.BlockSpec(memory_space=pltpu.VMEM)
    return pl.pallas_call(
        body, name=name, in_specs=[vm], out_specs=vm,
        out_shape=jax.ShapeDtypeStruct((N_DEV * m_per, n), x_shard.dtype),
        scratch_shapes=[pltpu.SemaphoreType.DMA((7,)), pltpu.SemaphoreType.DMA((7,)), pltpu.SemaphoreType.DMA],
        compiler_params=pltpu.CompilerParams(has_side_effects=True, vmem_limit_bytes=VMEM_LIMIT))(x_shard)


SEM_SPEC = pl.BlockSpec(memory_space=pltpu.SEMAPHORE)
DATAFLOW_EFFECT = pltpu.SideEffectType.DATAFLOW_SIDE_EFFECTING


def _copies(plan, srcs, lands, send_sems, recv_sems):
    x, y, c = _place()
    return [pltpu.make_async_remote_copy(src_ref=s, dst_ref=d, send_sem=send_sems.at[k], recv_sem=recv_sems.at[k],
                                         device_id=dev, device_id_type=MESH)
            for k, (s, d, dev) in enumerate(plan(srcs, lands, x, y, c))]


def _copies_start(plan, n_copies, srcs, lands, name):
    ns, nl = len(srcs), len(lands)

    def body(*refs):
        send_sems, recv_sems = refs[ns + nl], refs[ns + nl + 1]
        for cp in _copies(plan, refs[:ns], refs[ns:ns + nl], send_sems, recv_sems):
            cp.start()
        refs[-1][...] = jnp.zeros_like(refs[-1])

    arrays = list(srcs) + list(lands)
    res = pl.pallas_call(
        body, name=name,
        in_specs=[HBM_SPEC] * (ns + nl),
        out_specs=tuple([SEM_SPEC, SEM_SPEC] + [HBM_SPEC] * (ns + nl) + [pl.BlockSpec(memory_space=pltpu.VMEM)]),
        out_shape=tuple([pltpu.SemaphoreType.DMA((n_copies,)), pltpu.SemaphoreType.DMA((n_copies,))]
                        + [pltpu.HBM(a.shape, a.dtype) for a in arrays]
                        + [jax.ShapeDtypeStruct((SUBLANES, LANES), F32)]),
        input_output_aliases={i: 2 + i for i in range(ns + nl)},
        compiler_params=pltpu.CompilerParams(has_side_effects=DATAFLOW_EFFECT),
    )(*[pltpu.with_memory_space_constraint(a, pltpu.HBM) for a in arrays])
    return res[0], res[1], list(res[2:2 + ns]), list(res[2 + ns:2 + ns + nl]), res[-1]


def _copies_wait(plan, started, after, name):
    send_sems, recv_sems, srcs, lands, _ = started
    ns, nl = len(srcs), len(lands)

    def body(*refs):
        for cp in _copies(plan, refs[:ns], refs[ns:ns + nl], refs[ns + nl], refs[ns + nl + 1]):
            cp.wait_send()
            cp.wait_recv()

    arrays = list(srcs) + list(lands)
    res = pl.pallas_call(
        body, name=name,
        in_specs=[HBM_SPEC] * (ns + nl) + [SEM_SPEC, SEM_SPEC, pl.BlockSpec(memory_space=pl.ANY)],
        out_specs=tuple([HBM_SPEC] * (ns + nl)),
        out_shape=tuple(pltpu.HBM(a.shape, a.dtype) for a in arrays),
        input_output_aliases={i: i for i in range(ns + nl)},
        compiler_params=pltpu.CompilerParams(has_side_effects=DATAFLOW_EFFECT),
    )(*arrays, send_sems, recv_sems, after)
    return list(res[:ns]), list(res[ns:])


def _after(token, value):
    return value + token[0:1, 0:1].astype(value.dtype)


def _chips(x, y):
    return [(1 - x, y), (x, 1 - y), (1 - x, 1 - y)]


def _plan_gather_first(srcs, lands, x, y, c):
    me = 4 * x + 2 * y + c
    out = []
    for s, l in zip(srcs, lands):
        out.append((s, l.at[me], (x, y, 1 - c)))
        out += [(s, l.at[me], (*chip, c)) for chip in _chips(x, y)]
    return out


def _plan_gather_pass(srcs, lands, x, y, c):
    out = []
    for l in lands:
        for chip in _chips(x, y):
            slot = l.at[4 * chip[0] + 2 * chip[1] + c]
            out.append((slot, slot, (x, y, 1 - c)))
    return out


def _plan_grads_sibling(srcs, lands, x, y, c):
    return [(s.at[2 * q + (1 - c)], l.at[q], (x, y, 1 - c)) for s, l in zip(srcs, lands) for q in range(4)]


def _plan_grads_chips(srcs, lands, x, y, c):
    return [(s.at[2 * chip[0] + chip[1]], l.at[k], (*chip, c))
            for s, l in zip(srcs, lands) for k, chip in enumerate(_chips(x, y))]


def _landing(n_slots, like):
    return [lax.empty((n_slots,) + a.shape[1:], a.dtype) for a in like]


def _sum_slots(g, name):
    _, R, C = g.shape
    tr = min(256, R)
    assert R % tr == 0

    def body(g_ref, o_ref):
        acc = g_ref[0]
        for j in range(1, N_DEV):
            acc = acc + g_ref[j]
        o_ref[...] = acc

    return pl.pallas_call(
        body, name=name, grid=(R // tr,),
        in_specs=[pl.BlockSpec((N_DEV, tr, C), lambda i: (0, i, 0))],
        out_specs=pl.BlockSpec((tr, C), lambda i: (i, 0)),
        out_shape=jax.ShapeDtypeStruct((R, C), F32),
        compiler_params=_params("parallel"))(g)


def _chip_partial(g, r1, place, name):
    _, R, C = g.shape
    tr = min(256, R)
    assert R % tr == 0

    def body(pl_ref, g_ref, r_ref, pb_ref, pm_ref):
        q = pl.program_id(1)
        s = g_ref[...] + r_ref[...]
        pb_ref[...] = s.astype(BF16)

        @pl.when(q == pl_ref[1])
        def _():
            pm_ref[...] = s

    grid_spec = pltpu.PrefetchScalarGridSpec(
        num_scalar_prefetch=1, grid=(R // tr, 4),
        in_specs=[pl.BlockSpec((None, tr, C), lambda r, q, p: (2 * q + p[0], r, 0)),
                  pl.BlockSpec((None, tr, C), lambda r, q, p: (q, r, 0))],
        out_specs=[pl.BlockSpec((None, tr, C), lambda r, q, p: (q, r, 0)),
                   pl.BlockSpec((tr, C), lambda r, q, p: (r, 0))])
    return pl.pallas_call(
        body, name=name, grid_spec=grid_spec,
        out_shape=[jax.ShapeDtypeStruct((4, R, C), BF16), jax.ShapeDtypeStruct((R, C), F32)],
        compiler_params=_params("parallel", "arbitrary"))(place, g, r1)


def _adamw(w, gparts, m, v, name):
    lead = w.ndim == 3
    R, C = w.shape[-2:]
    tr = min(256, R)
    assert R % tr == 0
    g0, g3 = gparts

    def body(w_ref, g0_ref, *rest):
        if g3 is not None:
            g3_ref, m_ref, v_ref, go, do, mo, vo = rest
        else:
            m_ref, v_ref, go, do, mo, vo = rest
        g = g0_ref[...]
        if g3 is not None:
            for k in range(3):
                g = g + g3_ref[k].astype(F32)
        wv = w_ref[...]
        mn = ADAM_B1 * m_ref[...] + (1.0 - ADAM_B1) * g
        vn = ADAM_B2 * v_ref[...] + (1.0 - ADAM_B2) * jnp.square(g)
        m_hat = mn / (1.0 - ADAM_B1 ** ADAM_STEP)
        v_hat = vn / (1.0 - ADAM_B2 ** ADAM_STEP)
        go[...] = g
        do[...] = -ADAM_LR * (m_hat / (jnp.sqrt(v_hat) + ADAM_EPS) + ADAM_WD * wv)
        mo[...] = mn
        vo[...] = vn

    blk = pl.BlockSpec((tr, C), lambda i: (i, 0))
    wblk = pl.BlockSpec((None, tr, C), lambda i: (0, i, 0)) if lead else blk
    in_specs = [wblk, blk] + ([pl.BlockSpec((3, tr, C), lambda i: (0, i, 0))] if g3 is not None else []) + [wblk, wblk]
    args = [w, g0] + ([g3] if g3 is not None else []) + [m, v]
    return pl.pallas_call(
        body, name=name, grid=(R // tr,), in_specs=in_specs, out_specs=[wblk] * 4,
        out_shape=[jax.ShapeDtypeStruct(w.shape, F32)] * 4,
        compiler_params=_params("parallel"))(*args)


def _adamw_layers(w, parts, m, v, name):
    _, R, C = w.shape
    tr = min(256, R)
    assert R % tr == 0

    def body(w_ref, p0, r0, p1, r1, m_ref, v_ref, go, do, mo, vo):
        gs = []
        for p, r in ((p0, r0), (p1, r1)):
            g = p[...]
            for k in range(3):
                g = g + r[k].astype(F32)
            gs.append(g)
        g = jnp.where(pl.program_id(0) == 0, gs[0], gs[1])
        mn = ADAM_B1 * m_ref[...] + (1.0 - ADAM_B1) * g
        vn = ADAM_B2 * v_ref[...] + (1.0 - ADAM_B2) * jnp.square(g)
        m_hat = mn / (1.0 - ADAM_B1 ** ADAM_STEP)
        v_hat = vn / (1.0 - ADAM_B2 ** ADAM_STEP)
        go[...] = g
        do[...] = -ADAM_LR * (m_hat / (jnp.sqrt(v_hat) + ADAM_EPS) + ADAM_WD * w_ref[...])
        mo[...] = mn
        vo[...] = vn

    lay = pl.BlockSpec((None, tr, C), lambda l, i: (l, i, 0))
    one = pl.BlockSpec((tr, C), lambda l, i: (i, 0))
    three = pl.BlockSpec((3, tr, C), lambda l, i: (0, i, 0))
    return pl.pallas_call(
        body, name=name, grid=(2, R // tr), in_specs=[lay, one, three, one, three, lay, lay],
        out_specs=[lay] * 4, out_shape=[jax.ShapeDtypeStruct((2, R, C), F32)] * 4,
        compiler_params=_params("parallel", "parallel"))(w, parts[0][0], parts[0][1], parts[1][0], parts[1][1], m, v)


SMALL_SHARDED = ("rg_conv_w", "rg_b_a", "rg_b_x", "rg_lambda", "gla_w_gate_up", "gla_b_gate", "gla_norm")
SMALL_REPLICATED = ("norm_mix", "norm_mlp", "norm_final", "rg_conv_b", "rg_w_a", "rg_w_x", "hg_lb_logits", "hg_norm")
WEIGHT_NAMES = ("norm_mix", "norm_mlp", "norm_final", "mlp_w1", "mlp_w2", "ab_w_in", "ab_w_out", "rg_conv_w",
                "rg_conv_b", "rg_w_a", "rg_b_a", "rg_w_x", "rg_b_x", "rg_lambda", "hg_lb_logits", "hg_norm",
                "gla_w_in", "gla_w_out", "gla_w_gate_up", "gla_b_gate", "gla_norm")


def _rows128(a):
    return a.reshape(-1, LANES)


def _part_rows(a):
    return -(-(a.size // LANES) // SUBLANES) * SUBLANES


def _pack_rows(arrays, pad_to=SUBLANES):
    parts = [jnp.pad(_rows128(a), ((0, _part_rows(a) - a.size // LANES), (0, 0))) for a in arrays]
    total = sum(p.shape[0] for p in parts)
    extra = (-total) % pad_to
    if extra:
        parts.append(jnp.zeros((extra, LANES), parts[0].dtype))
    return jnp.concatenate(parts, axis=0)


def _unshard_last(g, shape_local):
    nd = len(shape_local)
    t = g.reshape((N_DEV,) + tuple(shape_local))
    t = jnp.moveaxis(t, 0, nd - 1)
    return t.reshape(tuple(shape_local[:-1]) + (N_DEV * shape_local[-1],))


def _block_diag(w):
    eye = jnp.eye(8, dtype=w.dtype)
    return (w[:, :, :, None, :] * eye[None, :, None, :, None]).reshape(2, RG_W, RG_W)


def _block_diag_extract(dw):
    t = dw.reshape(2, 8, 64, 8, 64)
    return jnp.moveaxis(jnp.diagonal(t, axis1=1, axis2=3), -1, 1)


def kernel(x, norm_mix, norm_mlp, norm_final, mlp_w1, mlp_w2, ab_w_in, ab_w_out, rg_conv_w, rg_conv_b, rg_w_a, rg_b_a, rg_w_x, rg_b_x, rg_lambda, hg_lb_logits, hg_norm, gla_w_in, gla_w_out, gla_w_gate_up, gla_b_gate, gla_norm, loss_target, m_norm_mix, m_norm_mlp, m_norm_final, m_mlp_w1, m_mlp_w2, m_ab_w_in, m_ab_w_out, m_rg_conv_w, m_rg_conv_b, m_rg_w_a, m_rg_b_a, m_rg_w_x, m_rg_b_x, m_rg_lambda, m_hg_lb_logits, m_hg_norm, m_gla_w_in, m_gla_w_out, m_gla_w_gate_up, m_gla_b_gate, m_gla_norm, v_norm_mix, v_norm_mlp, v_norm_final, v_mlp_w1, v_mlp_w2, v_ab_w_in, v_ab_w_out, v_rg_conv_w, v_rg_conv_b, v_rg_w_a, v_rg_b_a, v_rg_w_x, v_rg_b_x, v_rg_lambda, v_hg_lb_logits, v_hg_norm, v_gla_w_in, v_gla_w_out, v_gla_w_gate_up, v_gla_b_gate, v_gla_norm):
    w_loc = dict(norm_mix=norm_mix, norm_mlp=norm_mlp, norm_final=norm_final, mlp_w1=mlp_w1, mlp_w2=mlp_w2,
                 ab_w_in=ab_w_in, ab_w_out=ab_w_out, rg_conv_w=rg_conv_w, rg_conv_b=rg_conv_b, rg_w_a=rg_w_a,
                 rg_b_a=rg_b_a, rg_w_x=rg_w_x, rg_b_x=rg_b_x, rg_lambda=rg_lambda, hg_lb_logits=hg_lb_logits,
                 hg_norm=hg_norm, gla_w_in=gla_w_in, gla_w_out=gla_w_out, gla_w_gate_up=gla_w_gate_up,
                 gla_b_gate=gla_b_gate, gla_norm=gla_norm)
    m_loc = dict(norm_mix=m_norm_mix, norm_mlp=m_norm_mlp, norm_final=m_norm_final, mlp_w1=m_mlp_w1,
                 mlp_w2=m_mlp_w2, ab_w_in=m_ab_w_in, ab_w_out=m_ab_w_out, rg_conv_w=m_rg_conv_w,
                 rg_conv_b=m_rg_conv_b, rg_w_a=m_rg_w_a, rg_b_a=m_rg_b_a, rg_w_x=m_rg_w_x, rg_b_x=m_rg_b_x,
                 rg_lambda=m_rg_lambda, hg_lb_logits=m_hg_lb_logits, hg_norm=m_hg_norm, gla_w_in=m_gla_w_in,
                 gla_w_out=m_gla_w_out, gla_w_gate_up=m_gla_w_gate_up, gla_b_gate=m_gla_b_gate,
                 gla_norm=m_gla_norm)
    v_loc = dict(norm_mix=v_norm_mix, norm_mlp=v_norm_mlp, norm_final=v_norm_final, mlp_w1=v_mlp_w1,
                 mlp_w2=v_mlp_w2, ab_w_in=v_ab_w_in, ab_w_out=v_ab_w_out, rg_conv_w=v_rg_conv_w,
                 rg_conv_b=v_rg_conv_b, rg_w_a=v_rg_w_a, rg_b_a=v_rg_b_a, rg_w_x=v_rg_w_x, rg_b_x=v_rg_b_x,
                 rg_lambda=v_rg_lambda, hg_lb_logits=v_hg_lb_logits, hg_norm=v_hg_norm, gla_w_in=v_gla_w_in,
                 gla_w_out=v_gla_w_out, gla_w_gate_up=v_gla_w_gate_up, gla_b_gate=v_gla_b_gate,
                 gla_norm=v_gla_norm)

    T = x.shape[1]
    h0 = x.reshape(T, D_MODEL)
    target = loss_target.reshape(T, D_MODEL)
    ax, ay, ac = lax.axis_index("x"), lax.axis_index("y"), lax.axis_index("c")
    dev = 4 * ax + 2 * ay + ac
    place = jnp.stack([ac, 2 * ax + ay]).astype(jnp.int32)

    abin_shard = ab_w_in[0].astype(BF16)
    first_started = _copies_start(_plan_gather_first, 4, [abin_shard], _landing(N_DEV, [abin_shard[None]]),
                                  "ag_first_start")
    rest_shards = [mlp_w1[0].astype(BF16), mlp_w2[0].astype(BF16), gla_w_in[0].astype(BF16),
                   gla_w_out[0].astype(BF16), mlp_w1[1].astype(BF16), mlp_w2[1].astype(BF16),
                   _after(first_started[4], ab_w_out[0].astype(BF16))]
    ag_started = _copies_start(_plan_gather_first, 4 * len(rest_shards), rest_shards,
                               _landing(N_DEV, [s[None] for s in rest_shards]), "ag_rest_start")

    small_local = [w_loc[n] for n in SMALL_SHARDED]
    small_g = _allgather_vmem(_pack_rows(small_local, 8), "ag_small")
    small_g = small_g.reshape(N_DEV, -1, LANES)
    full = {}
    off = 0
    for n, a in zip(SMALL_SHARDED, small_local):
        full[n] = _unshard_last(small_g[:, off:off + a.size // LANES].reshape(N_DEV, a.size), a.shape)
        off += _part_rows(a)
    conv_w = full["rg_conv_w"][0]
    b_a, b_x, lam = full["rg_b_a"][0], full["rg_b_x"][0], full["rg_lambda"][0]
    w_up, b_gate, g_norm = full["gla_w_gate_up"][0], full["gla_b_gate"][0], full["gla_norm"]

    cw8 = jnp.pad(conv_w, ((0, 4), (0, 0)))
    wbd = jnp.concatenate([_block_diag(rg_w_a[0]), _block_diag(rg_w_x[0])], axis=2).astype(BF16)
    rg_bias = jnp.concatenate([b_a, b_x], axis=1).reshape(2, 1, 2 * RG_W)
    lam3 = lam.reshape(2, 1, RG_W)
    l0, l1 = hg_lb_logits[0:1], hg_lb_logits[1:2]
    wup_pad = jnp.zeros((2, LANES, 512), F32).at[0, 0:16].set(w_up[0]).at[1, 16:32].set(w_up[1])
    bg3 = b_gate.reshape(2, 1, 512)
    nmix0, nmix1 = norm_mix[0:1], norm_mix[1:2]
    nmlp0, nmlp1 = norm_mlp[0:1], norm_mlp[1:2]
    nfin = norm_final.reshape(1, D_MODEL)

    prepared = (ag_started[4] + cw8[:, 0:LANES] + wup_pad[0, 0:SUBLANES, 0:LANES] + rg_bias[0, :, 0:LANES]
                + wbd[0, 0:SUBLANES, 0:LANES].astype(F32) + lam3[0, :, 0:LANES] + bg3[0, :, 0:LANES])
    (abin_shard,), abin_l = _copies_wait(_plan_gather_first, first_started, prepared, "ag_first_wait")
    first_pass = _copies_start(_plan_gather_pass, 3, [], abin_l, "ag_first_pass_start")
    _, (abin_g,) = _copies_wait(_plan_gather_pass, first_pass, first_pass[4], "ag_first_pass_wait")
    abin_g = lax.dynamic_update_index_in_dim(abin_g, abin_shard, dev, 0)
    wab_in = jnp.transpose(abin_g, (1, 0, 2)).reshape(D_MODEL, AB_IN)
    proj0, y0 = _norm_matmul(h0, _after(ag_started[4], nmix0), wab_in, "l0_in_proj")
    xc = _rg_conv_fwd(proj0, cw8, rg_conv_b, "rg_conv")
    hs = _rg_scan_fwd(xc, wbd, rg_bias, lam3, "rg_scan")
    o_hg, s_hg = _hg_fwd(proj0, l0, l1, "hg_chunks")
    both_done = hs[0][0:SUBLANES, 0:LANES] + o_hg[0][0:SUBLANES, 0:LANES]
    rest_shards, rest_lands = _copies_wait(_plan_gather_first, ag_started, both_done, "ag_rest_wait")
    pass_started = _copies_start(_plan_gather_pass, 3 * len(rest_lands), [], rest_lands, "ag_pass_start")
    mixin0 = _l0_combine_fwd(hs, proj0, o_hg, _after(pass_started[4], hg_norm), "l0_combine")
    _, rest_g = _copies_wait(_plan_gather_pass, pass_started, mixin0, "ag_pass_wait")
    rest_g = [lax.dynamic_update_index_in_dim(g, s, dev, 0) for g, s in zip(rest_g, rest_shards)]
    wab_out = rest_g[6].reshape(D_MODEL, D_MODEL)
    h1 = _matmul_res(mixin0, wab_out, h0, "l0_out_proj")
    w1g = (rest_g[0], rest_g[4])
    w2f = (rest_g[1].reshape(D_FF, D_MODEL), rest_g[5].reshape(D_FF, D_MODEL))
    wgla_in = jnp.pad(jnp.transpose(rest_g[2], (1, 0, 2)).reshape(D_MODEL, GLA_IN),
                      ((0, 0), (0, GLA_IN_PAD - GLA_IN)))
    wgla_out = rest_g[3].reshape(D_MODEL, D_MODEL)
    h2, pre0, ym0 = _mlp_fwd(h1, nmlp0, w1g[0], w2f[0], "mlp0")
    proj1, y1 = _norm_matmul(h2, nmix1, wgla_in, "l1_in_proj")
    z_gate, lr_b = _gate_logits(proj1, wup_pad, bg3, "gla_gate_logits")
    o_gla, s_gla = _gla_fwd(proj1, z_gate, "gla_chunks")
    mixin1 = _l1_combine_fwd(o_gla, proj1, g_norm, "l1_combine")
    h3 = _matmul_res(mixin1, wgla_out, h2, "l1_out_proj")
    h4, pre1, ym1 = _mlp_fwd(h3, nmlp1, w1g[1], w2f[1], "mlp1")
    loss_blk, dh4, dh4b, d_nfin = _final_loss(h4, nfin, target, "final_loss")

    dh3, dh3b, dpre1, act1, d_nmlp1 = _mlp_bwd(dh4, dh4b, h3, nmlp1, pre1, w1g[1], w2f[1], "mlp1_bwd")
    g_w1_1 = _wgrad(ym1, dpre1, 512, "mlp1_dw1", sharded_cols=True)
    g_w2_1 = _wgrad(act1, dh4b, 512, "mlp1_dw2")
    dmixin1 = _dgrad(dh3b, wgla_out, "l1_out_dgrad")
    g_gla_out = _wgrad(mixin1, dh3b, 512, "l1_out_dw")
    do_gla, dr, d_gnorm = _l1_combine_bwd(o_gla, proj1, g_norm, dmixin1, "l1_combine_bwd")
    dq1, dk1, dv1, dz_gate = _gla_bwd(proj1, z_gate, s_gla, do_gla, "gla_chunks_bwd")
    dlr1, d_bg, dz_b = _gate_logits_bwd(dz_gate, wup_pad, "gla_gate_logits_bwd")
    d_wup = [_wgrad(lr_b, dz_b[d], 512, "gla_gate_dw%d" % d) for d in range(2)]
    dproj1 = _l1_assemble(dq1, dk1, dv1, dr, dlr1, "l1_assemble")
    dh2, dh2b, d_nmix1 = _dgrad_norm(dproj1, wgla_in, h2, nmix1, dh3, "l1_in_dgrad")
    g_gla_in = _wgrad(y1, dproj1, 640, "l1_in_dw")

    def reduce_start(grads, tag):
        return _copies_start(_plan_grads_sibling, 4 * len(grads), grads, _landing(4, grads), "rs_%s_d2d_start" % tag)

    def reduce_mid(started, after, tag):
        grads, got = _copies_wait(_plan_grads_sibling, started, after, "rs_%s_d2d_wait" % tag)
        parts = [_chip_partial(g, r, place, "rs_%s_partial%d" % (tag, a)) for a, (g, r) in enumerate(zip(grads, got))]
        pb = [p[0] for p in parts]
        return _copies_start(_plan_grads_chips, 3 * len(pb), pb, _landing(3, pb), "rs_%s_ici_start" % tag), \
            [p[1] for p in parts]

    def reduce_end(started, mine, after, tag):
        _, got = _copies_wait(_plan_grads_chips, started, after, "rs_%s_ici_wait" % tag)
        return list(zip(mine, got))

    slots_l1 = [g_w1_1, g_w2_1.reshape(N_DEV, 512, D_MODEL),
                jnp.transpose(g_gla_in[:, :GLA_IN].reshape(D_MODEL, N_DEV, GLA_IN // N_DEV), (1, 0, 2)),
                g_gla_out.reshape(N_DEV, 128, D_MODEL)]
    ra_d2d = reduce_start(slots_l1, "l1")

    dh1, dh1b, dpre0, act0, d_nmlp0 = _mlp_bwd(dh2, dh2b, h1, _after(ra_d2d[4], nmlp0), pre0, w1g[0], w2f[0],
                                               "mlp0_bwd")
    g_w1_0 = _wgrad(ym0, dpre0, 512, "mlp0_dw1", sharded_cols=True)
    g_w2_0 = _wgrad(act0, dh2b, 512, "mlp0_dw2")
    ra_ici, ra_mine = reduce_mid(ra_d2d, g_w2_0, "l1")
    rb_d2d = reduce_start([g_w1_0, g_w2_0.reshape(N_DEV, 512, D_MODEL)], "mlp0")
    dmixin0 = _dgrad(dh1b, wab_out, "l0_out_dgrad")
    g_ab_out = _wgrad(mixin0, dh1b, 512, "l0_out_dw")
    dho, dga, do_hg, dg_gate, d_hgnorm = _l0_combine_bwd(
        hs, proj0, o_hg, _after(rb_d2d[4], _after(ra_ici[4], hg_norm)), dmixin0, "l0_combine_bwd")
    dxc, d_wbd, d_rgb, d_lam = _rg_scan_bwd(xc, wbd, rg_bias, lam3, hs, dho, "rg_scan_bwd")
    dxa, d_cw8, d_cb = _rg_conv_bwd(dxc, proj0, cw8, "rg_conv_bwd")
    dq0, df0, dv0, d_l0, d_l1 = _hg_bwd(proj0, l0, l1, s_hg, do_hg, "hg_chunks_bwd")
    rb_ici, rb_mine = reduce_mid(rb_d2d, d_l0, "mlp0")
    dproj0 = _l0_assemble(dxa, dga, dq0, df0, dv0, dg_gate, "l0_assemble")
    dx, _, d_nmix0 = _dgrad_norm(dproj0, wab_in, h0, _after(rb_ici[4], nmix0), dh1, "l0_in_dgrad")

    d_wa = _block_diag_extract(d_wbd[:, :, :RG_W])[None]
    d_wx = _block_diag_extract(d_wbd[:, :, RG_W:])[None]
    small_full = {
        "norm_mix": jnp.concatenate([d_nmix0, d_nmix1], axis=0), "norm_mlp": jnp.concatenate([d_nmlp0, d_nmlp1], axis=0),
        "norm_final": d_nfin.reshape(D_MODEL), "rg_conv_b": d_cb, "rg_w_a": d_wa, "rg_w_x": d_wx,
        "hg_lb_logits": jnp.concatenate([d_l0[0] + d_l0[1], d_l1[0] + d_l1[1]], axis=0), "hg_norm": d_hgnorm,
        "rg_conv_w": d_cw8[0:4][None], "rg_b_a": d_rgb[:, 0, :RG_W][None], "rg_b_x": d_rgb[:, 0, RG_W:][None],
        "rg_lambda": d_lam[:, 0, :][None],
        "gla_w_gate_up": jnp.stack([d_wup[0][0:16], d_wup[1][16:32]])[None], "gla_b_gate": d_bg[:, 0, :][None],
        "gla_norm": d_gnorm}
    small_names = SMALL_REPLICATED + SMALL_SHARDED
    packed = _pack_rows([loss_blk] + [small_full[n] for n in small_names], 256)
    ar_first = _copies_start(_plan_gather_first, 4, [packed], _landing(N_DEV, [packed[None]]), "ar_small_start")

    g_ab_in = _wgrad(y0, dproj0, 512, "l0_in_dw", behind=ar_first[4])
    rc_d2d = reduce_start([jnp.transpose(g_ab_in.reshape(D_MODEL, N_DEV, AB_IN // N_DEV), (1, 0, 2)),
                           g_ab_out.reshape(N_DEV, 128, D_MODEL)], "ab")
    (packed,), ar_lands = _copies_wait(_plan_gather_first, ar_first, rc_d2d[4], "ar_small_wait")
    ar_pass = _copies_start(_plan_gather_pass, 3, [], ar_lands, "ar_small_pass_start")
    rc_ici, rc_mine = reduce_mid(rc_d2d, ar_pass[4], "ab")
    _, (ar_gathered,) = _copies_wait(_plan_gather_pass, ar_pass, rc_ici[4], "ar_small_pass_wait")
    summed = _sum_slots(lax.dynamic_update_index_in_dim(ar_gathered, packed, dev, 0), "ar_small_sum")
    loss = summed[0, 0]

    pieces_l1 = reduce_end(ra_ici, ra_mine, rc_ici[4], "l1")
    res_gla_in = _adamw(gla_w_in, pieces_l1[2], m_gla_w_in, v_gla_w_in, "adamw_gla_in")
    res_gla_out = _adamw(gla_w_out, pieces_l1[3], m_gla_w_out, v_gla_w_out, "adamw_gla_out")
    pieces_mlp0 = reduce_end(rb_ici, rb_mine, res_gla_out[0], "mlp0")
    res_w1 = _adamw_layers(mlp_w1, (pieces_mlp0[0], pieces_l1[0]), m_mlp_w1, v_mlp_w1, "adamw_mlp_w1")
    res_w2 = _adamw_layers(mlp_w2, (pieces_mlp0[1], pieces_l1[1]), m_mlp_w2, v_mlp_w2, "adamw_mlp_w2")
    res = {"mlp_w1": tuple(res_w1), "mlp_w2": tuple(res_w2),
           "gla_w_in": tuple(res_gla_in), "gla_w_out": tuple(res_gla_out)}

    g_small = {}
    off = SUBLANES
    for n in small_names:
        a = small_full[n]
        gfull = summed[off:off + a.size // LANES].reshape(a.shape)
        off += _part_rows(a)
        if n in SMALL_SHARDED:
            loc = w_loc[n].shape[-1]
            gfull = lax.dynamic_slice_in_dim(gfull, dev * loc, loc, axis=gfull.ndim - 1)
        g_small[n] = gfull
    sw = _pack_rows([w_loc[n] for n in small_names], 256)
    sg = _pack_rows([g_small[n] for n in small_names], 256)
    sm = _pack_rows([m_loc[n] for n in small_names], 256)
    sv = _pack_rows([v_loc[n] for n in small_names], 256)
    small_res = _adamw(sw, (sg, None), sm, sv, "adamw_small")
    others_done = (res_w1[1][0, 0:SUBLANES, 0:LANES] + res_w2[1][0, 0:SUBLANES, 0:LANES]
                   + res_gla_in[1][0, 0:SUBLANES, 0:LANES] + small_res[1][0:SUBLANES, 0:LANES])
    pieces_ab = reduce_end(rc_ici, rc_mine, others_done, "ab")
    res["ab_w_in"] = tuple(_adamw(ab_w_in, pieces_ab[0], m_ab_w_in, v_ab_w_in, "adamw_ab_in"))
    res["ab_w_out"] = tuple(_adamw(ab_w_out, pieces_ab[1], m_ab_w_out, v_ab_w_out, "adamw_ab_out"))
    off = 0
    for n in small_names:
        a = w_loc[n]
        nr = a.size // LANES
        res[n] = tuple(small_res[k][off:off + nr].reshape(a.shape) for k in range(4))
        off += _part_rows(a)

    grad_x = dx.reshape(1, T, D_MODEL)
    out = [loss, grad_x]
    for k in range(4):
        out += [res[n][k] for n in WEIGHT_NAMES]
    return tuple(out)
```

```python
import jax
import jax.numpy as jnp
from jax import lax
from jax.experimental import pallas as pl
from jax.experimental.pallas import tpu as pltpu

F32, BF16 = jnp.float32, jnp.bfloat16
HI = lax.Precision.HIGHEST
MESH = pl.DeviceIdType.MESH

D_MODEL = 1024
D_FF = 4096
RG_W = 512
HG_W = 512
CHUNK = 64
EPS = 1e-6
RG_C = 8.0
AB_IN = 3584
GLA_IN = 3104
GLA_IN_PAD = 3200
N_DEV = 8
LANES = 128
SUBLANES = 8
VMEM_LIMIT = 48 * 1024 * 1024

ADAM_LR, ADAM_B1, ADAM_B2, ADAM_EPS, ADAM_WD, ADAM_STEP = 0.001, 0.9, 0.999, 1e-08, 0.01, 10


def _params(*sem):
    return pltpu.CompilerParams(dimension_semantics=sem, vmem_limit_bytes=VMEM_LIMIT)


def _dg(a, b, ca, cb):
    return lax.dot_general(a.astype(BF16), b.astype(BF16), (((ca,), (cb,)), ((), ())),
                           preferred_element_type=F32)


@jax.custom_vjp
def _mm_nn(a, b):
    return _dg(a, b, 1, 0)


_mm_nn.defvjp(lambda a, b: (_dg(a, b, 1, 0), (a, b)),
              lambda res, g: (_dg(g, res[1], 1, 1), _dg(res[0], g, 0, 0)))


@jax.custom_vjp
def _mm_nt(a, b):
    return _dg(a, b, 1, 1)


_mm_nt.defvjp(lambda a, b: (_dg(a, b, 1, 1), (a, b)),
              lambda res, g: (_dg(g, res[1], 1, 0), _dg(g, res[0], 0, 0)))


@jax.custom_vjp
def _mm_tn(a, b):
    return _dg(a, b, 0, 0)


_mm_tn.defvjp(lambda a, b: (_dg(a, b, 0, 0), (a, b)),
              lambda res, g: (_dg(res[1], g, 1, 1), _dg(res[0], g, 1, 0)))


@jax.custom_vjp
def _cum(tri, tri_t, x):
    return jnp.dot(tri, x, precision=HI, preferred_element_type=F32)


_cum.defvjp(lambda tri, tri_t, x: (jnp.dot(tri, x, precision=HI, preferred_element_type=F32), (tri, tri_t)),
            lambda res, g: (jnp.zeros_like(res[0]), jnp.zeros_like(res[1]),
                            jnp.dot(res[1], g, precision=HI, preferred_element_type=F32)))


def _sig(x):
    return 1.0 / (1.0 + jnp.exp(-x))


def _gelu(x):
    return 0.5 * x * (1.0 + jnp.tanh(0.7978845608028654 * (x + 0.044715 * (x * x * x))))


def _softplus(z):
    return jnp.maximum(z, 0.0) + jnp.log(1.0 + jnp.exp(-jnp.abs(z)))


def _rms(x):
    return lax.rsqrt(jnp.mean(x * x, axis=-1, keepdims=True) + EPS)


def _rmsnorm_bwd(x, gain, dy):
    r = _rms(x)
    xh = x * r
    dgain = jnp.sum(dy * xh, axis=0, keepdims=True)
    dxh = dy * gain
    dx = r * (dxh - xh * jnp.mean(dxh * xh, axis=-1, keepdims=True))
    return dx, dgain


def _headnorm(o, gain, n_heads, hd):
    parts = []
    for h in range(n_heads):
        oh = o[:, h * hd:(h + 1) * hd]
        parts.append(oh * _rms(oh))
    return jnp.concatenate(parts, axis=1) * gain


def _tri_consts(d):
    row = lax.broadcasted_iota(jnp.int32, (CHUNK, CHUNK), 0)
    col = lax.broadcasted_iota(jnp.int32, (CHUNK, CHUNK), 1)
    ge = (row >= col).astype(F32)
    le = (row <= col).astype(F32)
    r1 = lax.broadcasted_iota(jnp.int32, (CHUNK, 1), 0)
    if d == 0:
        return ge, le, (r1 <= CHUNK // 2).astype(F32)
    return le, ge, (r1 >= CHUNK // 2 - 1).astype(F32)


def _chunk_core(qh, k, v, logf, st_prev, tri, tri_t, mref, n_heads, dk, dv):
    cum = _cum(tri, tri_t, logf)
    ref = jnp.sum(logf * mref, axis=0, keepdims=True)
    last = jnp.sum(logf, axis=0, keepdims=True)
    q_in = qh * jnp.exp(cum - ref)
    k_in = k * jnp.exp(ref - cum)
    k_st = k * jnp.exp(last - cum)
    q_dec = qh * jnp.exp(cum)
    decay = jnp.exp(last)
    outs, sts = [], []
    for h in range(n_heads):
        sk = slice(h * dk, (h + 1) * dk)
        sv = slice(h * dv, (h + 1) * dv)
        sc = _mm_nt(q_in[:, sk], k_in[:, sk]) * tri
        o = _mm_nn(sc, v[:, sv]) + _mm_nt(q_dec[:, sk], st_prev[h])
        sts.append(st_prev[h] * decay[:, sk] + _mm_tn(v[:, sv], k_st[:, sk]))
        outs.append(o)
    return jnp.concatenate(outs, axis=1), tuple(sts)


def _hg_chunk(q, f, v, l0, l1, st_prev, tri, tri_t, mref):
    lb = _sig(l0 - l1)
    sg = _sig(f)
    qh = q * _sig(q)
    logf = jnp.log(lb + (1.0 - lb) * sg)
    k = (1.0 - lb) * (1.0 - sg)
    return _chunk_core(qh, k, v, logf, st_prev, tri, tri_t, mref, 4, 128, 128)


def _gla_chunk(q, k, v, z, st_prev, tri, tri_t, mref):
    logf = (jnp.minimum(z, 0.0) - jnp.log(1.0 + jnp.exp(-jnp.abs(z)))) * (1.0 / 16.0)
    qh = q * (128.0 ** -0.5)
    return _chunk_core(qh, k, v, logf, st_prev, tri, tri_t, mref, 4, 128, 256)


def _rg_gates(xc, wbd, bias, lam):
    z = _mm_nn(xc, wbd) + bias
    r = _sig(z[:, :RG_W])
    i = _sig(z[:, RG_W:])
    log_a = -RG_C * r * _softplus(-lam)
    a = jnp.exp(log_a)
    x2 = 2.0 * log_a
    neg_expm1 = jnp.where(x2 > -1e-2, -(x2 + 0.5 * x2 * x2 + x2 * x2 * x2 * (1.0 / 6.0)), 1.0 - jnp.exp(x2))
    u = jnp.sqrt(neg_expm1) * (i * xc)
    return a, u


def _l0_combine(hf, hb, ga, of, ob, g, gain):
    ya = (hf + hb) * _gelu(ga)
    yb = _headnorm(of + ob, gain, 4, 128) * (g * _sig(g))
    return jnp.concatenate([ya, yb], axis=1)


def _l1_combine(of, ob, r, gain):
    return _headnorm(of + ob, gain, 4, 256) * (r * _sig(r))


def _norm_matmul(h, gain, w, name):
    T, D = h.shape
    N = w.shape[1]
    tm = min(512, T)

    def body(h_ref, g_ref, w_ref, o_ref, y_ref):
        x = h_ref[...]
        y = (x * _rms(x) * g_ref[...]).astype(BF16)
        y_ref[...] = y
        o_ref[...] = jnp.dot(y, w_ref[...], preferred_element_type=F32)

    return pl.pallas_call(
        body, name=name, grid=(T // tm,),
        in_specs=[pl.BlockSpec((tm, D), lambda i: (i, 0)), pl.BlockSpec((1, D), lambda i: (0, 0)),
                  pl.BlockSpec((D, N), lambda i: (0, 0))],
        out_specs=[pl.BlockSpec((tm, N), lambda i: (i, 0)), pl.BlockSpec((tm, D), lambda i: (i, 0))],
        out_shape=[jax.ShapeDtypeStruct((T, N), F32), jax.ShapeDtypeStruct((T, D), BF16)],
        compiler_params=_params("parallel"))(h, gain, w)


def _matmul_res(a, w, res, name):
    T, K = a.shape
    N = w.shape[1]
    tm = min(512, T)

    def body(a_ref, w_ref, r_ref, o_ref):
        o_ref[...] = r_ref[...] + jnp.dot(a_ref[...], w_ref[...], preferred_element_type=F32)

    return pl.pallas_call(
        body, name=name, grid=(T // tm,),
        in_specs=[pl.BlockSpec((tm, K), lambda i: (i, 0)), pl.BlockSpec((K, N), lambda i: (0, 0)),
                  pl.BlockSpec((tm, N), lambda i: (i, 0))],
        out_specs=pl.BlockSpec((tm, N), lambda i: (i, 0)),
        out_shape=jax.ShapeDtypeStruct((T, N), F32),
        compiler_params=_params("parallel"))(a, w, res)


def _dgrad(dc, w, name):
    T, N = dc.shape
    K = w.shape[0]
    tm = min(512, T)

    def body(d_ref, w_ref, o_ref):
        o_ref[...] = _dg(d_ref[...], w_ref[...], 1, 1)

    return pl.pallas_call(
        body, name=name, grid=(T // tm,),
        in_specs=[pl.BlockSpec((tm, N), lambda i: (i, 0)), pl.BlockSpec((K, N), lambda i: (0, 0))],
        out_specs=pl.BlockSpec((tm, K), lambda i: (i, 0)),
        out_shape=jax.ShapeDtypeStruct((T, K), F32),
        compiler_params=_params("parallel"))(dc, w)


def _dgrad_norm(dproj, w, h, gain, dres, name):
    T, N = dproj.shape
    D = w.shape[0]
    tm = min(512, T)

    def body(dp_ref, w_ref, h_ref, g_ref, dr_ref, dh_ref, dhb_ref, dg_ref):
        @pl.when(pl.program_id(0) == 0)
        def _():
            dg_ref[...] = jnp.zeros_like(dg_ref)

        dy = _dg(dp_ref[...], w_ref[...], 1, 1)
        dx, dgain = _rmsnorm_bwd(h_ref[...], g_ref[...], dy)
        dh = dr_ref[...] + dx
        dh_ref[...] = dh
        dhb_ref[...] = dh.astype(BF16)
        dg_ref[...] += dgain

    return pl.pallas_call(
        body, name=name, grid=(T // tm,),
        in_specs=[pl.BlockSpec((tm, N), lambda i: (i, 0)), pl.BlockSpec((D, N), lambda i: (0, 0)),
                  pl.BlockSpec((tm, D), lambda i: (i, 0)), pl.BlockSpec((1, D), lambda i: (0, 0)),
                  pl.BlockSpec((tm, D), lambda i: (i, 0))],
        out_specs=[pl.BlockSpec((tm, D), lambda i: (i, 0)), pl.BlockSpec((tm, D), lambda i: (i, 0)),
                   pl.BlockSpec((1, D), lambda i: (0, 0))],
        out_shape=[jax.ShapeDtypeStruct((T, D), F32), jax.ShapeDtypeStruct((T, D), BF16),
                   jax.ShapeDtypeStruct((1, D), F32)],
        compiler_params=_params("arbitrary"))(dproj, w, h, gain, dres)


def _wgrad(a, b, tn, name, sharded_cols=False, behind=None):
    T, K = a.shape
    N = b.shape[1]
    tk = min(1024, K)

    def body(a_ref, b_ref, *rest):
        rest[-1][...] = _dg(a_ref[...], b_ref[...], 0, 0)

    if sharded_cols:
        out_spec = pl.BlockSpec((None, tk, tn), lambda k, n: (n, k, 0))
        out_shape = jax.ShapeDtypeStruct((N // tn, K, tn), F32)
    else:
        out_spec = pl.BlockSpec((tk, tn), lambda k, n: (k, n))
        out_shape = jax.ShapeDtypeStruct((K, N), F32)
    in_specs = [pl.BlockSpec((T, tk), lambda k, n: (0, k)), pl.BlockSpec((T, tn), lambda k, n: (0, n))]
    args = [a, b]
    if behind is not None:
        in_specs.append(pl.BlockSpec((SUBLANES, LANES), lambda k, n: (0, 0)))
        args.append(behind)
    return pl.pallas_call(
        body, name=name, grid=(K // tk, N // tn), in_specs=in_specs, out_specs=out_spec, out_shape=out_shape,
        compiler_params=_params("parallel", "parallel"))(*args)


def _resident(shape):
    return pl.BlockSpec(shape, lambda i: (0,) * len(shape), pipeline_mode=pl.Buffered(1))


def _mlp_fwd(h, gain, w1g, w2, name):
    T, D = h.shape
    nf, _, tf = w1g.shape
    tm = min(512, T)

    def body(h_ref, g_ref, w1_ref, w2_ref, o_ref, pre_ref, y_ref):
        x = h_ref[...]
        y = (x * _rms(x) * g_ref[...]).astype(BF16)
        y_ref[...] = y
        acc = x
        for j in range(nf):
            cols = slice(j * tf, (j + 1) * tf)
            pre = jnp.dot(y, w1_ref[j], preferred_element_type=F32)
            pre_ref[:, cols] = pre.astype(BF16)
            act = jnp.square(jnp.maximum(pre, 0.0)).astype(BF16)
            acc = acc + jnp.dot(act, w2_ref[cols, :], preferred_element_type=F32)
        o_ref[...] = acc

    return pl.pallas_call(
        body, name=name, grid=(T // tm,),
        in_specs=[pl.BlockSpec((tm, D), lambda i: (i, 0)), pl.BlockSpec((1, D), lambda i: (0, 0)),
                  _resident(w1g.shape), _resident(w2.shape)],
        out_specs=[pl.BlockSpec((tm, D), lambda i: (i, 0)), pl.BlockSpec((tm, nf * tf), lambda i: (i, 0)),
                   pl.BlockSpec((tm, D), lambda i: (i, 0))],
        out_shape=[jax.ShapeDtypeStruct((T, D), F32), jax.ShapeDtypeStruct((T, nf * tf), BF16),
                   jax.ShapeDtypeStruct((T, D), BF16)],
        compiler_params=_params("parallel"))(h, gain, w1g, w2)


def _mlp_bwd(dout, dout_b, h, gain, pre, w1g, w2, name):
    T, D = h.shape
    nf, _, tf = w1g.shape
    tm = min(256, T)

    def body(do_ref, dob_ref, h_ref, g_ref, pre_ref, w1_ref, w2_ref, dh_ref, dhb_ref, dpre_ref, act_ref, dg_ref):
        @pl.when(pl.program_id(0) == 0)
        def _():
            dg_ref[...] = jnp.zeros_like(dg_ref)

        dob = dob_ref[...]
        dy = None
        for j in range(nf):
            cols = slice(j * tf, (j + 1) * tf)
            rp = jnp.maximum(pre_ref[:, cols].astype(F32), 0.0)
            dpre = (_dg(dob, w2_ref[cols, :], 1, 1) * (2.0 * rp)).astype(BF16)
            dpre_ref[:, cols] = dpre
            act_ref[:, cols] = (rp * rp).astype(BF16)
            part = _dg(dpre, w1_ref[j], 1, 1)
            dy = part if dy is None else dy + part
        dx, dgain = _rmsnorm_bwd(h_ref[...], g_ref[...], dy)
        dh = do_ref[...] + dx
        dh_ref[...] = dh
        dhb_ref[...] = dh.astype(BF16)
        dg_ref[...] += dgain

    tok = lambda w: pl.BlockSpec((tm, w), lambda i: (i, 0))
    return pl.pallas_call(
        body, name=name, grid=(T // tm,),
        in_specs=[tok(D), tok(D), tok(D), pl.BlockSpec((1, D), lambda i: (0, 0)), tok(nf * tf),
                  _resident(w1g.shape), _resident(w2.shape)],
        out_specs=[tok(D), tok(D), tok(nf * tf), tok(nf * tf), pl.BlockSpec((1, D), lambda i: (0, 0))],
        out_shape=[jax.ShapeDtypeStruct((T, D), F32), jax.ShapeDtypeStruct((T, D), BF16),
                   jax.ShapeDtypeStruct((T, nf * tf), BF16),
                   jax.ShapeDtypeStruct((T, nf * tf), BF16), jax.ShapeDtypeStruct((1, D), F32)],
        compiler_params=_params("arbitrary"))(dout, dout_b, h, gain, pre, w1g, w2)


def _final_loss(h, gain, target, name):
    T, D = h.shape
    tm = min(512, T)

    def body(h_ref, g_ref, t_ref, l_ref, dh_ref, dhb_ref, dg_ref):
        @pl.when(pl.program_id(0) == 0)
        def _():
            l_ref[...] = jnp.zeros_like(l_ref)
            dg_ref[...] = jnp.zeros_like(dg_ref)

        x = h_ref[...]
        err = x * _rms(x) * g_ref[...] - t_ref[...]
        l_ref[...] += 0.5 * jnp.sum(jnp.mean(err * err, axis=-1, keepdims=True), axis=0, keepdims=True)
        dx, dgain = _rmsnorm_bwd(x, g_ref[...], err * (1.0 / D))
        dh_ref[...] = dx
        dhb_ref[...] = dx.astype(BF16)
        dg_ref[...] += dgain

    return pl.pallas_call(
        body, name=name, grid=(T // tm,),
        in_specs=[pl.BlockSpec((tm, D), lambda i: (i, 0)), pl.BlockSpec((1, D), lambda i: (0, 0)),
                  pl.BlockSpec((tm, D), lambda i: (i, 0))],
        out_specs=[pl.BlockSpec((SUBLANES, LANES), lambda i: (0, 0)), pl.BlockSpec((tm, D), lambda i: (i, 0)),
                   pl.BlockSpec((tm, D), lambda i: (i, 0)), pl.BlockSpec((1, D), lambda i: (0, 0))],
        out_shape=[jax.ShapeDtypeStruct((SUBLANES, LANES), F32), jax.ShapeDtypeStruct((T, D), F32),
                   jax.ShapeDtypeStruct((T, D), BF16), jax.ShapeDtypeStruct((1, D), F32)],
        compiler_params=_params("arbitrary"))(h, gain, target)


def _halo_specs(tm, T, width, col, tile=lambda i: i):
    r8 = tm // SUBLANES
    nb8 = T // SUBLANES
    return [pl.BlockSpec((tm, width), lambda i: (tile(i), col)),
            pl.BlockSpec((SUBLANES, width), lambda i: (jnp.maximum(tile(i) * r8 - 1, 0), col)),
            pl.BlockSpec((SUBLANES, width), lambda i: (jnp.minimum((tile(i) + 1) * r8, nb8 - 1), col))]


def _ext(cur, prev, nxt, has_prev, has_next):
    return jnp.concatenate([jnp.where(has_prev, prev, 0.0), cur, jnp.where(has_next, nxt, 0.0)], axis=0)


def _shifted(ext, offset, tm):
    n = ext.shape[0]
    sh = (-offset) % n
    r = ext if sh == 0 else pltpu.roll(ext, sh, 0)
    return r[SUBLANES:SUBLANES + tm]


def _rg_conv_fwd(proj, cw8, cb, name):
    T = proj.shape[0]
    tm = min(512, T)
    nT = T // tm

    def body(cur_ref, prev_ref, next_ref, w_ref, b_ref, o_ref):
        i = pl.program_id(0)
        ext = _ext(cur_ref[...], prev_ref[...], next_ref[...], i > 0, i < nT - 1)
        acc = jnp.broadcast_to(b_ref[...], (tm, RG_W))
        for k in range(4):
            acc = acc + w_ref[k:k + 1, :] * _shifted(ext, k - 2, tm)
        o_ref[...] = acc

    return pl.pallas_call(
        body, name=name, grid=(nT,),
        in_specs=_halo_specs(tm, T, RG_W, 0) + [pl.BlockSpec((SUBLANES, RG_W), lambda i: (0, 0)),
                                                pl.BlockSpec((1, RG_W), lambda i: (0, 0))],
        out_specs=pl.BlockSpec((tm, RG_W), lambda i: (i, 0)),
        out_shape=jax.ShapeDtypeStruct((T, RG_W), F32),
        compiler_params=_params("parallel"))(proj, proj, proj, cw8, cb)


def _rg_conv_bwd(dxc, proj, cw8, name):
    T = proj.shape[0]
    tm = min(512, T)
    nT = T // tm

    def body(a0, p0, n0, a1, p1, n1, xa, xp, xn, w_ref, dxa_ref, dw_ref, db_ref):
        i = pl.program_id(0)

        @pl.when(i == 0)
        def _():
            dw_ref[...] = jnp.zeros_like(dw_ref)
            db_ref[...] = jnp.zeros_like(db_ref)

        has_p, has_n = i > 0, i < nT - 1
        cur = a0[...] + a1[...]
        dext = _ext(cur, p0[...] + p1[...], n0[...] + n1[...], has_p, has_n)
        xext = _ext(xa[...], xp[...], xn[...], has_p, has_n)
        acc = jnp.zeros((tm, RG_W), F32)
        rows = []
        for k in range(4):
            acc = acc + w_ref[k:k + 1, :] * _shifted(dext, 2 - k, tm)
            rows.append(jnp.sum(cur * _shifted(xext, k - 2, tm), axis=0, keepdims=True))
        dxa_ref[...] = acc
        dw_ref[...] += jnp.concatenate(rows + [jnp.zeros((4, RG_W), F32)], axis=0)
        db_ref[...] += jnp.sum(cur, axis=0, keepdims=True)

    return pl.pallas_call(
        body, name=name, grid=(nT,),
        in_specs=(_halo_specs(tm, T, RG_W, 0) + _halo_specs(tm, T, RG_W, 0)
                  + _halo_specs(tm, T, RG_W, 0) + [pl.BlockSpec((SUBLANES, RG_W), lambda i: (0, 0))]),
        out_specs=[pl.BlockSpec((tm, RG_W), lambda i: (i, 0)), pl.BlockSpec((SUBLANES, RG_W), lambda i: (0, 0)),
                   pl.BlockSpec((1, RG_W), lambda i: (0, 0))],
        out_shape=[jax.ShapeDtypeStruct((T, RG_W), F32), jax.ShapeDtypeStruct((SUBLANES, RG_W), F32),
                   jax.ShapeDtypeStruct((1, RG_W), F32)],
        compiler_params=_params("arbitrary"))(dxc[0], dxc[0], dxc[0], dxc[1], dxc[1], dxc[1], proj, proj, proj, cw8)


def _local_scan(a, b, ascending):
    n = a.shape[0]
    pos = jnp.bitwise_and(lax.broadcasted_iota(jnp.int32, a.shape, 0), SUBLANES - 1)
    for s in (1, 2, 4):
        sh = s if ascending else n - s
        ok = (pos >= s) if ascending else (pos < SUBLANES - s)
        a_sh, b_sh = pltpu.roll(a, sh, 0), pltpu.roll(b, sh, 0)
        b = jnp.where(ok, a * b_sh + b, b)
        a = jnp.where(ok, a * a_sh, a)
    return a, b


def _group_scan(chains, a_sc, b_sc, carry, n_groups):
    def step(g, hs):
        new = []
        for (d, out_ref, asc), h in zip(chains, hs):
            r0 = pl.multiple_of((g if asc else n_groups - 1 - g) * SUBLANES, SUBLANES)
            out_ref[pl.ds(r0, SUBLANES), :] = a_sc[d, pl.ds(r0, SUBLANES), :] * h + b_sc[d, pl.ds(r0, SUBLANES), :]
            new.append(out_ref[pl.ds(r0 + (SUBLANES - 1 if asc else 0), 1), :])
        return tuple(new)

    hs = lax.fori_loop(0, n_groups, step, tuple(carry[d, 0:1, :] for d, _, _ in chains))
    for (d, _, _), h in zip(chains, hs):
        carry[d, 0:1, :] = h


def _rg_scan_fwd(xc, wbd, bias, lam, name):
    T = xc.shape[0]
    tm = min(512, T)
    nT = T // tm

    def body(xf_ref, xb_ref, w_ref, b_ref, lam_ref, hf_ref, hb_ref, a_sc, b_sc, carry):
        @pl.when(pl.program_id(0) == 0)
        def _():
            carry[...] = jnp.zeros_like(carry)

        for d, x_ref in enumerate((xf_ref, xb_ref)):
            a, u = _rg_gates(x_ref[...], w_ref[d], b_ref[d], lam_ref[d])
            a_sc[d], b_sc[d] = _local_scan(a, u, d == 0)
        _group_scan(((0, hf_ref, True), (1, hb_ref, False)), a_sc, b_sc, carry, tm // SUBLANES)

    full = lambda a: pl.BlockSpec(a.shape, lambda i: (0,) * len(a.shape))
    res = pl.pallas_call(
        body, name=name, grid=(nT,),
        in_specs=[pl.BlockSpec((tm, RG_W), lambda i: (i, 0)), pl.BlockSpec((tm, RG_W), lambda i: (nT - 1 - i, 0)),
                  full(wbd), full(bias), full(lam)],
        out_specs=[pl.BlockSpec((tm, RG_W), lambda i: (i, 0)), pl.BlockSpec((tm, RG_W), lambda i: (nT - 1 - i, 0))],
        out_shape=[jax.ShapeDtypeStruct((T, RG_W), F32)] * 2,
        scratch_shapes=[pltpu.VMEM((2, tm, RG_W), F32), pltpu.VMEM((2, tm, RG_W), F32),
                        pltpu.VMEM((2, SUBLANES, RG_W), F32)],
        compiler_params=_params("arbitrary"))(xc, xc, wbd, bias, lam)
    return res[0], res[1]


def _rg_scan_bwd(xc, wbd, bias, lam, hs, dho, name):
    T = xc.shape[0]
    tm = min(256, T)
    nT = T // tm
    tiles = (lambda i: nT - 1 - i, lambda i: i)

    def body(xf_ref, xb_ref, w_ref, b_ref, lam_ref, hfc, hfp, hfn, hbc, hbp, hbn, dof_ref, dob_ref,
             dxf_ref, dxb_ref, dw_ref, db_ref, dlam_ref, a_sc, b_sc, y_sc, carry):
        i = pl.program_id(0)

        @pl.when(i == 0)
        def _():
            carry[...] = jnp.zeros_like(carry)
            dw_ref[...] = jnp.zeros_like(dw_ref)
            db_ref[...] = jnp.zeros_like(db_ref)
            dlam_ref[...] = jnp.zeros_like(dlam_ref)

        vjps, entering = [], []
        for d, (x_ref, do_ref) in enumerate(((xf_ref, dof_ref), (xb_ref, dob_ref))):
            (a, _), vjp = jax.vjp(_rg_gates, x_ref[...], w_ref[d].astype(F32), b_ref[d], lam_ref[d])
            vjps.append(vjp)
            entering.append(carry[d, 0:1, :])
            a_sc[d], b_sc[d] = _local_scan(a, a * do_ref[...], d == 1)
        _group_scan(((0, y_sc.at[0], False), (1, y_sc.at[1], True)), a_sc, b_sc, carry, tm // SUBLANES)

        row = lax.broadcasted_iota(jnp.int32, (tm, RG_W), 0)
        for d, (do_ref, dx_ref, hc, hp, hn, ti) in enumerate(
                ((dof_ref, dxf_ref, hfc, hfp, hfn, nT - 1 - i), (dob_ref, dxb_ref, hbc, hbp, hbn, i))):
            y = y_sc[d]
            if d == 0:
                y_next = jnp.where(row == tm - 1, entering[d], pltpu.roll(y, tm - 1, 0))
            else:
                y_next = jnp.where(row == 0, entering[d], pltpu.roll(y, 1, 0))
            dtot = do_ref[...] + y_next
            ext = _ext(hc[...], hp[...], hn[...], ti > 0, ti < nT - 1)
            hprev = _shifted(ext, -1 if d == 0 else 1, tm)
            dxc, dw, db, dlam = vjps[d]((dtot * hprev, dtot))
            dx_ref[...] = dxc
            dw_ref[d] += dw
            db_ref[d] += db
            dlam_ref[d] += dlam

    full = lambda a: pl.BlockSpec(a.shape, lambda i: (0,) * len(a.shape))
    tok = lambda d: pl.BlockSpec((tm, RG_W), lambda i: (tiles[d](i), 0))
    acc_shapes = [jax.ShapeDtypeStruct((2, RG_W, 2 * RG_W), F32), jax.ShapeDtypeStruct((2, 1, 2 * RG_W), F32),
                  jax.ShapeDtypeStruct((2, 1, RG_W), F32)]
    res = pl.pallas_call(
        body, name=name, grid=(nT,),
        in_specs=([tok(0), tok(1), full(wbd), full(bias), full(lam)]
                  + _halo_specs(tm, T, RG_W, 0, tiles[0]) + _halo_specs(tm, T, RG_W, 0, tiles[1]) + [tok(0), tok(1)]),
        out_specs=[tok(0), tok(1)] + [full(s) for s in acc_shapes],
        out_shape=[jax.ShapeDtypeStruct((T, RG_W), F32)] * 2 + acc_shapes,
        scratch_shapes=[pltpu.VMEM((2, tm, RG_W), F32), pltpu.VMEM((2, tm, RG_W), F32),
                        pltpu.VMEM((2, tm, RG_W), F32), pltpu.VMEM((2, SUBLANES, RG_W), F32)],
        compiler_params=_params("arbitrary"))(xc, xc, wbd, bias, lam, hs[0], hs[0], hs[0], hs[1], hs[1], hs[1],
                                              dho, dho)
    return (res[0], res[1]), res[2], res[3], res[4]


def _chunk_rows(n_chunks, reverse):
    up, down = (lambda c: c), (lambda c: n_chunks - 1 - c)
    return (down, up) if reverse else (up, down)


STEP_CHUNKS = 2
STEP_ROWS = STEP_CHUNKS * CHUNK


def _sub_chunks(ascending):
    order = range(STEP_CHUNKS) if ascending else range(STEP_CHUNKS - 1, -1, -1)
    return [(s, slice(s * CHUNK, (s + 1) * CHUNK)) for s in order]


def _hg_fwd(proj, l0, l1, name):
    T = proj.shape[0]
    nC = T // CHUNK
    nS = nC // STEP_CHUNKS
    H, dk, dv = 4, 128, 128
    rows = _chunk_rows(nS, False)

    def body(qf, ff, vf, qb, fb, vb, l0_ref, l1_ref, of, ob, spf, spb, st):
        @pl.when(pl.program_id(0) == 0)
        def _():
            st[...] = jnp.zeros_like(st)

        for d, (q, f, v, o, sp) in enumerate(((qf, ff, vf, of, spf), (qb, fb, vb, ob, spb))):
            tri, tri_t, mref = _tri_consts(d)
            stp = tuple(st[d, h] for h in range(H))
            for s, r in _sub_chunks(d == 0):
                for h in range(H):
                    sp[s, h] = stp[h]
                o_val, stp = _hg_chunk(q[r, :], f[r, :], v[r, :], l0_ref[...], l1_ref[...], stp, tri, tri_t, mref)
                o[r, :] = o_val
            for h in range(H):
                st[d, h] = stp[h]

    tok = lambda d, col: pl.BlockSpec((STEP_ROWS, HG_W), lambda c: (rows[d](c), col))
    par = pl.BlockSpec((1, HG_W), lambda c: (0, 0))
    state = lambda d: pl.BlockSpec((STEP_CHUNKS, H, dv, dk), lambda c: (rows[d](c), 0, 0, 0))
    res = pl.pallas_call(
        body, name=name, grid=(nS,),
        in_specs=[tok(0, 2), tok(0, 3), tok(0, 5), tok(1, 2), tok(1, 4), tok(1, 5), par, par],
        out_specs=[tok(0, 0), tok(1, 0), state(0), state(1)],
        out_shape=[jax.ShapeDtypeStruct((T, H * dv), F32)] * 2 + [jax.ShapeDtypeStruct((nC, H, dv, dk), F32)] * 2,
        scratch_shapes=[pltpu.VMEM((2, H, dv, dk), F32)],
        compiler_params=_params("arbitrary"))(proj, proj, proj, proj, proj, proj, l0, l1)
    return (res[0], res[1]), (res[2], res[3])


def _hg_bwd(proj, l0, l1, sprev, do, name):
    T = proj.shape[0]
    nC = T // CHUNK
    nS = nC // STEP_CHUNKS
    H, dk, dv = 4, 128, 128
    rows = _chunk_rows(nS, True)

    def body(qf, ff, vf, qb, fb, vb, l0_ref, l1_ref, spf, spb, dof, dob,
             dqf, dff, dvf, dqb, dfb, dvb, dl0_ref, dl1_ref, dst):
        @pl.when(pl.program_id(0) == 0)
        def _():
            dst[...] = jnp.zeros_like(dst)
            dl0_ref[...] = jnp.zeros_like(dl0_ref)
            dl1_ref[...] = jnp.zeros_like(dl1_ref)

        for d, (q, f, v, sp, do_ref, dq_ref, df_ref, dv_ref) in enumerate(
                ((qf, ff, vf, spf, dof, dqf, dff, dvf), (qb, fb, vb, spb, dob, dqb, dfb, dvb))):
            tri, tri_t, mref = _tri_consts(d)
            fn = lambda q_, f_, v_, a0, a1, stp: _hg_chunk(q_, f_, v_, a0, a1, stp, tri, tri_t, mref)
            dstp = tuple(dst[d, h] for h in range(H))
            for s, r in _sub_chunks(d == 1):
                stp = tuple(sp[s, h] for h in range(H))
                _, vjp = jax.vjp(fn, q[r, :], f[r, :], v[r, :], l0_ref[...], l1_ref[...], stp)
                dq, df, dvv, dl0, dl1, dstp = vjp((do_ref[r, :], dstp))
                dq_ref[r, :] = dq
                df_ref[r, :] = df
                dv_ref[r, :] = dvv
                dl0_ref[d] += dl0
                dl1_ref[d] += dl1
            for h in range(H):
                dst[d, h] = dstp[h]

    tok = lambda d, col: pl.BlockSpec((STEP_ROWS, HG_W), lambda c: (rows[d](c), col))
    par = pl.BlockSpec((1, HG_W), lambda c: (0, 0))
    acc = pl.BlockSpec((2, 1, HG_W), lambda c: (0, 0, 0))
    state = lambda d: pl.BlockSpec((STEP_CHUNKS, H, dv, dk), lambda c: (rows[d](c), 0, 0, 0))
    res = pl.pallas_call(
        body, name=name, grid=(nS,),
        in_specs=[tok(0, 2), tok(0, 3), tok(0, 5), tok(1, 2), tok(1, 4), tok(1, 5), par, par,
                  state(0), state(1), tok(0, 0), tok(1, 0)],
        out_specs=[tok(0, 0)] * 3 + [tok(1, 0)] * 3 + [acc, acc],
        out_shape=[jax.ShapeDtypeStruct((T, HG_W), F32)] * 6 + [jax.ShapeDtypeStruct((2, 1, HG_W), F32)] * 2,
        scratch_shapes=[pltpu.VMEM((2, H, dv, dk), F32)],
        compiler_params=_params("arbitrary"))(proj, proj, proj, proj, proj, proj, l0, l1, sprev[0], sprev[1], do, do)
    return (res[0], res[3]), (res[1], res[4]), (res[2], res[5]), res[6], res[7]


def _gate_logits(proj, wup, bg, name):
    T = proj.shape[0]
    tm = min(512, T)

    def body(lr_ref, w_ref, b_ref, z_ref, lrb_ref):
        lr = lr_ref[...].astype(BF16)
        lrb_ref[...] = lr
        for d in range(2):
            z_ref[d] = _dg(lr, w_ref[d], 1, 0) + b_ref[d]

    return pl.pallas_call(
        body, name=name, grid=(T // tm,),
        in_specs=[pl.BlockSpec((tm, LANES), lambda i: (i, 24)), pl.BlockSpec((2, LANES, 512), lambda i: (0, 0, 0)),
                  pl.BlockSpec((2, 1, 512), lambda i: (0, 0, 0))],
        out_specs=[pl.BlockSpec((2, tm, 512), lambda i: (0, i, 0)), pl.BlockSpec((tm, LANES), lambda i: (i, 0))],
        out_shape=[jax.ShapeDtypeStruct((2, T, 512), F32), jax.ShapeDtypeStruct((T, LANES), BF16)],
        compiler_params=_params("parallel"))(proj, wup, bg)


def _gate_logits_bwd(dz, wup, name):
    T = dz[0].shape[0]
    tm = min(512, T)

    def body(dzf_ref, dzb_ref, w_ref, dlr_ref, db_ref, dzb16_ref):
        @pl.when(pl.program_id(0) == 0)
        def _():
            db_ref[...] = jnp.zeros_like(db_ref)

        acc = jnp.zeros((tm, LANES), F32)
        for d, dz_ref in enumerate((dzf_ref, dzb_ref)):
            g = dz_ref[...]
            gb = g.astype(BF16)
            dzb16_ref[d] = gb
            acc = acc + _dg(gb, w_ref[d], 1, 1)
            db_ref[d] += jnp.sum(g, axis=0, keepdims=True)
        dlr_ref[...] = acc

    tok = pl.BlockSpec((tm, 512), lambda i: (i, 0))
    return pl.pallas_call(
        body, name=name, grid=(T // tm,),
        in_specs=[tok, tok, pl.BlockSpec((2, LANES, 512), lambda i: (0, 0, 0))],
        out_specs=[pl.BlockSpec((tm, LANES), lambda i: (i, 0)), pl.BlockSpec((2, 1, 512), lambda i: (0, 0, 0)),
                   pl.BlockSpec((2, tm, 512), lambda i: (0, i, 0))],
        out_shape=[jax.ShapeDtypeStruct((T, LANES), F32), jax.ShapeDtypeStruct((2, 1, 512), F32),
                   jax.ShapeDtypeStruct((2, T, 512), BF16)],
        compiler_params=_params("arbitrary"))(dz[0], dz[1], wup)


def _gla_fwd(proj, z, name):
    T = proj.shape[0]
    nC = T // CHUNK
    nS = nC // STEP_CHUNKS
    H, dk, dv = 4, 128, 256
    rows = _chunk_rows(nS, False)

    def body(qf, kf, vf, zf, qb, kb, vb, zb, of, ob, spf, spb, st):
        @pl.when(pl.program_id(0) == 0)
        def _():
            st[...] = jnp.zeros_like(st)

        for d, (q, k, v, z_ref, o, sp) in enumerate(((qf, kf, vf, zf, of, spf), (qb, kb, vb, zb, ob, spb))):
            tri, tri_t, mref = _tri_consts(d)
            stp = tuple(st[d, h] for h in range(H))
            for s, r in _sub_chunks(d == 0):
                for h in range(H):
                    sp[s, h] = stp[h]
                o_val, stp = _gla_chunk(q[r, :], k[r, :], v[r, :], z_ref[r, :], stp, tri, tri_t, mref)
                o[r, :] = o_val
            for h in range(H):
                st[d, h] = stp[h]

    tok = lambda d, w, col: pl.BlockSpec((STEP_ROWS, w), lambda c: (rows[d](c), col))
    gate = lambda d: pl.BlockSpec((None, STEP_ROWS, 512), lambda c: (d, rows[d](c), 0))
    state = lambda d: pl.BlockSpec((STEP_CHUNKS, H, dv, dk), lambda c: (rows[d](c), 0, 0, 0))
    res = pl.pallas_call(
        body, name=name, grid=(nS,),
        in_specs=[tok(0, 512, 0), tok(0, 512, 1), tok(0, 1024, 1), gate(0),
                  tok(1, 512, 0), tok(1, 512, 1), tok(1, 1024, 1), gate(1)],
        out_specs=[tok(0, H * dv, 0), tok(1, H * dv, 0), state(0), state(1)],
        out_shape=[jax.ShapeDtypeStruct((T, H * dv), F32)] * 2 + [jax.ShapeDtypeStruct((nC, H, dv, dk), F32)] * 2,
        scratch_shapes=[pltpu.VMEM((2, H, dv, dk), F32)],
        compiler_params=_params("arbitrary"))(proj, proj, proj, z, proj, proj, proj, z)
    return (res[0], res[1]), (res[2], res[3])


def _gla_bwd(proj, z, sprev, do, name):
    T = proj.shape[0]
    nC = T // CHUNK
    nS = nC // STEP_CHUNKS
    H, dk, dv = 4, 128, 256
    rows = _chunk_rows(nS, True)

    def body(qf, kf, vf, zf, qb, kb, vb, zb, spf, spb, dof, dob,
             dqf, dkf, dvf, dzf, dqb, dkb, dvb, dzb, dst):
        @pl.when(pl.program_id(0) == 0)
        def _():
            dst[...] = jnp.zeros_like(dst)

        for d, (q, k, v, z_ref, sp, do_ref, dq_ref, dk_ref, dv_ref, dz_ref) in enumerate(
                ((qf, kf, vf, zf, spf, dof, dqf, dkf, dvf, dzf), (qb, kb, vb, zb, spb, dob, dqb, dkb, dvb, dzb))):
            tri, tri_t, mref = _tri_consts(d)
            fn = lambda q_, k_, v_, z_, stp: _gla_chunk(q_, k_, v_, z_, stp, tri, tri_t, mref)
            dstp = tuple(dst[d, h] for h in range(H))
            for s, r in _sub_chunks(d == 1):
                stp = tuple(sp[s, h] for h in range(H))
                _, vjp = jax.vjp(fn, q[r, :], k[r, :], v[r, :], z_ref[r, :], stp)
                dq, dkk, dvv, dzz, dstp = vjp((do_ref[r, :], dstp))
                dq_ref[r, :] = dq
                dk_ref[r, :] = dkk
                dv_ref[r, :] = dvv
                dz_ref[r, :] = dzz
            for h in range(H):
                dst[d, h] = dstp[h]

    tok = lambda d, w, col: pl.BlockSpec((STEP_ROWS, w), lambda c: (rows[d](c), col))
    gate = lambda d: pl.BlockSpec((None, STEP_ROWS, 512), lambda c: (d, rows[d](c), 0))
    state = lambda d: pl.BlockSpec((STEP_CHUNKS, H, dv, dk), lambda c: (rows[d](c), 0, 0, 0))
    outs = lambda d: [tok(d, 512, 0), tok(d, 512, 0), tok(d, 1024, 0), tok(d, 512, 0)]
    shapes = [jax.ShapeDtypeStruct((T, 512), F32), jax.ShapeDtypeStruct((T, 512), F32),
              jax.ShapeDtypeStruct((T, 1024), F32), jax.ShapeDtypeStruct((T, 512), F32)]
    res = pl.pallas_call(
        body, name=name, grid=(nS,),
        in_specs=[tok(0, 512, 0), tok(0, 512, 1), tok(0, 1024, 1), gate(0),
                  tok(1, 512, 0), tok(1, 512, 1), tok(1, 1024, 1), gate(1),
                  state(0), state(1), tok(0, H * dv, 0), tok(1, H * dv, 0)],
        out_specs=outs(0) + outs(1), out_shape=shapes + shapes,
        scratch_shapes=[pltpu.VMEM((2, H, dv, dk), F32)],
        compiler_params=_params("arbitrary"))(proj, proj, proj, z, proj, proj, proj, z, sprev[0], sprev[1], do, do)
    return (res[0], res[4]), (res[1], res[5]), (res[2], res[6]), (res[3], res[7])


def _l0_combine_fwd(hs, proj, o, gain, name):
    T = proj.shape[0]
    tm = min(512, T)

    def body(hf, hb, ga, of, ob, g, gn, out):
        out[...] = _l0_combine(hf[...], hb[...], ga[...], of[...], ob[...], g[...], gn[...]).astype(BF16)

    tok = pl.BlockSpec((tm, 512), lambda i: (i, 0))
    return pl.pallas_call(
        body, name=name, grid=(T // tm,),
        in_specs=[tok, tok, pl.BlockSpec((tm, 512), lambda i: (i, 1)), tok, tok,
                  pl.BlockSpec((tm, 512), lambda i: (i, 6)), pl.BlockSpec((1, 512), lambda i: (0, 0))],
        out_specs=pl.BlockSpec((tm, 1024), lambda i: (i, 0)),
        out_shape=jax.ShapeDtypeStruct((T, 1024), BF16),
        compiler_params=_params("parallel"))(hs[0], hs[1], proj, o[0], o[1], proj, gain)


def _l0_combine_bwd(hs, proj, o, gain, dmix, name):
    T = proj.shape[0]
    tm = min(512, T)

    def body(hf, hb, ga, of, ob, g, gn, dm, dho_ref, dga_ref, do_ref, dg_ref, dgn_ref):
        @pl.when(pl.program_id(0) == 0)
        def _():
            dgn_ref[...] = jnp.zeros_like(dgn_ref)

        _, vjp = jax.vjp(_l0_combine, hf[...], hb[...], ga[...], of[...], ob[...], g[...], gn[...])
        dhf, _, dga, dof, _, dg, dgn = vjp(dm[...])
        dho_ref[...] = dhf
        dga_ref[...] = dga
        do_ref[...] = dof
        dg_ref[...] = dg
        dgn_ref[...] += dgn

    tok = lambda: pl.BlockSpec((tm, 512), lambda i: (i, 0))
    return pl.pallas_call(
        body, name=name, grid=(T // tm,),
        in_specs=[tok(), tok(), pl.BlockSpec((tm, 512), lambda i: (i, 1)), tok(), tok(),
                  pl.BlockSpec((tm, 512), lambda i: (i, 6)), pl.BlockSpec((1, 512), lambda i: (0, 0)),
                  pl.BlockSpec((tm, 1024), lambda i: (i, 0))],
        out_specs=[tok(), tok(), tok(), tok(), pl.BlockSpec((1, 512), lambda i: (0, 0))],
        out_shape=[jax.ShapeDtypeStruct((T, 512), F32)] * 4 + [jax.ShapeDtypeStruct((1, 512), F32)],
        compiler_params=_params("arbitrary"))(hs[0], hs[1], proj, o[0], o[1], proj, gain, dmix)


def _l0_assemble(dxa, dga, dq, df, dv, dg, name):
    T = dxa.shape[0]
    tm = min(512, T)

    def body(xa, ga, q0, q1, f0, f1, v0, v1, g, out):
        out[...] = jnp.concatenate([xa[...], ga[...], q0[...] + q1[...], f0[...], f1[...], v0[...] + v1[...],
                                    g[...]], axis=1).astype(BF16)

    tok = lambda: pl.BlockSpec((tm, 512), lambda i: (i, 0))
    return pl.pallas_call(
        body, name=name, grid=(T // tm,),
        in_specs=[tok() for _ in range(9)],
        out_specs=pl.BlockSpec((tm, AB_IN), lambda i: (i, 0)),
        out_shape=jax.ShapeDtypeStruct((T, AB_IN), BF16),
        compiler_params=_params("parallel"))(dxa, dga, dq[0], dq[1], df[0], df[1], dv[0], dv[1], dg)


def _l1_combine_fwd(o, proj, gain, name):
    T = proj.shape[0]
    tm = min(512, T)

    def body(of, ob, r, gn, out):
        out[...] = _l1_combine(of[...], ob[...], r[...], gn[...]).astype(BF16)

    tok = pl.BlockSpec((tm, 1024), lambda i: (i, 0))
    return pl.pallas_call(
        body, name=name, grid=(T // tm,),
        in_specs=[tok, tok, pl.BlockSpec((tm, 1024), lambda i: (i, 2)), pl.BlockSpec((1, 1024), lambda i: (0, 0))],
        out_specs=pl.BlockSpec((tm, 1024), lambda i: (i, 0)),
        out_shape=jax.ShapeDtypeStruct((T, 1024), BF16),
        compiler_params=_params("parallel"))(o[0], o[1], proj, gain)


def _l1_combine_bwd(o, proj, gain, dmix, name):
    T = proj.shape[0]
    tm = min(512, T)

    def body(of, ob, r, gn, dm, do_ref, dr_ref, dgn_ref):
        @pl.when(pl.program_id(0) == 0)
        def _():
            dgn_ref[...] = jnp.zeros_like(dgn_ref)

        _, vjp = jax.vjp(_l1_combine, of[...], ob[...], r[...], gn[...])
        dof, _, dr, dgn = vjp(dm[...])
        do_ref[...] = dof
        dr_ref[...] = dr
        dgn_ref[...] += dgn

    tok = lambda: pl.BlockSpec((tm, 1024), lambda i: (i, 0))
    return pl.pallas_call(
        body, name=name, grid=(T // tm,),
        in_specs=[tok(), tok(), pl.BlockSpec((tm, 1024), lambda i: (i, 2)),
                  pl.BlockSpec((1, 1024), lambda i: (0, 0)), tok()],
        out_specs=[tok(), tok(), pl.BlockSpec((1, 1024), lambda i: (0, 0))],
        out_shape=[jax.ShapeDtypeStruct((T, 1024), F32)] * 2 + [jax.ShapeDtypeStruct((1, 1024), F32)],
        compiler_params=_params("arbitrary"))(o[0], o[1], proj, gain, dmix)


def _l1_assemble(dq, dk, dv, dr, dlr, name):
    T = dr.shape[0]
    tm = min(512, T)

    def body(q0, q1, k0, k1, v0, v1, r, a, out):
        out[...] = jnp.concatenate([q0[...] + q1[...], k0[...] + k1[...], v0[...] + v1[...], r[...], a[...]],
                                   axis=1).astype(BF16)

    tok = lambda w: pl.BlockSpec((tm, w), lambda i: (i, 0))
    return pl.pallas_call(
        body, name=name, grid=(T // tm,),
        in_specs=[tok(512), tok(512), tok(512), tok(512), tok(1024), tok(1024), tok(1024), tok(LANES)],
        out_specs=pl.BlockSpec((tm, GLA_IN_PAD), lambda i: (i, 0)),
        out_shape=jax.ShapeDtypeStruct((T, GLA_IN_PAD), BF16),
        compiler_params=_params("parallel"))(dq[0], dq[1], dk[0], dk[1], dv[0], dv[1], dr, dlr)


HBM_SPEC = pl.BlockSpec(memory_space=pltpu.HBM)


def _place():
    x, y, c = lax.axis_index("x"), lax.axis_index("y"), lax.axis_index("c")
    return x, y, c


def _allgather_vmem(x_shard, name):
    m_per, n = x_shard.shape

    def body(x_ref, out_ref, send_sems, recv_sems, local_sem):
        x, y, c = _place()
        me, sibling = (x, y, c), (x, y, 1 - c)
        chips = [(1 - x, y), (x, 1 - y), (1 - x, 1 - y)]

        def rows(px, py, pc):
            return out_ref.at[pl.ds((4 * px + 2 * py + pc) * m_per, m_per), :]

        def copy(k, block, to, src=None):
            return pltpu.make_async_remote_copy(
                src_ref=rows(*block) if src is None else src, dst_ref=rows(*block),
                send_sem=send_sems.at[k], recv_sem=recv_sems.at[k], device_id=to, device_id_type=MESH)

        mine = pltpu.make_async_copy(x_ref, rows(*me), local_sem)
        mine.start()
        first = [copy(0, me, sibling, src=x_ref)]
        first += [copy(1 + j, me, (*chip, c), src=x_ref) for j, chip in enumerate(chips)]
        for cp in first:
            cp.start()
        passed = [copy(4 + j, (*chip, c), sibling) for j, chip in enumerate(chips)]
        for j, chip in enumerate(chips):
            copy(1 + j, (*chip, c), me).wait_recv()
            passed[j].start()
        copy(0, sibling, me).wait_recv()
        for j, chip in enumerate(chips):
            copy(4 + j, (*chip, 1 - c), me).wait_recv()
        for cp in first + passed:
            cp.wait_send()
        mine.wait()

    vm = pl.BlockSpec(memory_space=pltpu.VMEM)
    return pl.pallas_call(
        body, name=name, in_specs=[vm], out_specs=vm,
        out_shape=jax.ShapeDtypeStruct((N_DEV * m_per, n), x_shard.dtype),
        scratch_shapes=[pltpu.SemaphoreType.DMA((7,)), pltpu.SemaphoreType.DMA((7,)), pltpu.SemaphoreType.DMA],
        compiler_params=pltpu.CompilerParams(has_side_effects=True, vmem_limit_bytes=VMEM_LIMIT))(x_shard)


SEM_SPEC = pl.BlockSpec(memory_space=pltpu.SEMAPHORE)
DATAFLOW_EFFECT = pltpu.SideEffectType.DATAFLOW_SIDE_EFFECTING


def _copies(plan, srcs, lands, send_sems, recv_sems):
    x, y, c = _place()
    return [pltpu.make_async_remote_copy(src_ref=s, dst_ref=d, send_sem=send_sems.at[k], recv_sem=recv_sems.at[k],
                                         device_id=dev, device_id_type=MESH)
            for k, (s, d, dev) in enumerate(plan(srcs, lands, x, y, c))]


def _copies_start(plan, n_copies, srcs, lands, name):
    ns, nl = len(srcs), len(lands)

    def body(*refs):
        send_sems, recv_sems = refs[ns + nl], refs[ns + nl + 1]
        for cp in _copies(plan, refs[:ns], refs[ns:ns + nl], send_sems, recv_sems):
            cp.start()
        refs[-1][...] = jnp.zeros_like(refs[-1])

    arrays = list(srcs) + list(lands)
    res = pl.pallas_call(
        body, name=name,
        in_specs=[HBM_SPEC] * (ns + nl),
        out_specs=tuple([SEM_SPEC, SEM_SPEC] + [HBM_SPEC] * (ns + nl) + [pl.BlockSpec(memory_space=pltpu.VMEM)]),
        out_shape=tuple([pltpu.SemaphoreType.DMA((n_copies,)), pltpu.SemaphoreType.DMA((n_copies,))]
                        + [pltpu.HBM(a.shape, a.dtype) for a in arrays]
                        + [jax.ShapeDtypeStruct((SUBLANES, LANES), F32)]),
        input_output_aliases={i: 2 + i for i in range(ns + nl)},
        compiler_params=pltpu.CompilerParams(has_side_effects=DATAFLOW_EFFECT),
    )(*[pltpu.with_memory_space_constraint(a, pltpu.HBM) for a in arrays])
    return res[0], res[1], list(res[2:2 + ns]), list(res[2 + ns:2 + ns + nl]), res[-1]


def _copies_wait(plan, started, after, name):
    send_sems, recv_sems, srcs, lands, _ = started
    ns, nl = len(srcs), len(lands)

    def body(*refs):
        for cp in _copies(plan, refs[:ns], refs[ns:ns + nl], refs[ns + nl], refs[ns + nl + 1]):
            cp.wait_send()
            cp.wait_recv()

    arrays = list(srcs) + list(lands)
    res = pl.pallas_call(
        body, name=name,
        in_specs=[HBM_SPEC] * (ns + nl) + [SEM_SPEC, SEM_SPEC, pl.BlockSpec(memory_space=pl.ANY)],
        out_specs=tuple([HBM_SPEC] * (ns + nl)),
        out_shape=tuple(pltpu.HBM(a.shape, a.dtype) for a in arrays),
        input_output_aliases={i: i for i in range(ns + nl)},
        compiler_params=pltpu.CompilerParams(has_side_effects=DATAFLOW_EFFECT),
    )(*arrays, send_sems, recv_sems, after)
    return list(res[:ns]), list(res[ns:])


def _after(token, value):
    return value + token[0:1, 0:1].astype(value.dtype)


def _chips(x, y):
    return [(1 - x, y), (x, 1 - y), (1 - x, 1 - y)]


def _plan_gather_first(srcs, lands, x, y, c):
    me = 4 * x + 2 * y + c
    out = []
    for s, l in zip(srcs, lands):
        out.append((s, l.at[me], (x, y, 1 - c)))
        out += [(s, l.at[me], (*chip, c)) for chip in _chips(x, y)]
    return out


def _plan_gather_pass(srcs, lands, x, y, c):
    out = []
    for l in lands:
        for chip in _chips(x, y):
            slot = l.at[4 * chip[0] + 2 * chip[1] + c]
            out.append((slot, slot, (x, y, 1 - c)))
    return out


def _plan_grads_sibling(srcs, lands, x, y, c):
    return [(s.at[2 * q + (1 - c)], l.at[q], (x, y, 1 - c)) for s, l in zip(srcs, lands) for q in range(4)]


def _plan_grads_chips(srcs, lands, x, y, c):
    return [(s.at[2 * chip[0] + chip[1]], l.at[k], (*chip, c))
            for s, l in zip(srcs, lands) for k, chip in enumerate(_chips(x, y))]


def _landing(n_slots, like):
    return [lax.empty((n_slots,) + a.shape[1:], a.dtype) for a in like]


def _sum_slots(g, name):
    _, R, C = g.shape
    tr = min(256, R)
    assert R % tr == 0

    def body(g_ref, o_ref):
        acc = g_ref[0]
        for j in range(1, N_DEV):
            acc = acc + g_ref[j]
        o_ref[...] = acc

    return pl.pallas_call(
        body, name=name, grid=(R // tr,),
        in_specs=[pl.BlockSpec((N_DEV, tr, C), lambda i: (0, i, 0))],
        out_specs=pl.BlockSpec((tr, C), lambda i: (i, 0)),
        out_shape=jax.ShapeDtypeStruct((R, C), F32),
        compiler_params=_params("parallel"))(g)


def _chip_partial(g, r1, place, name):
    _, R, C = g.shape
    tr = min(256, R)
    assert R % tr == 0

    def body(pl_ref, g_ref, r_ref, pb_ref, pm_ref):
        q = pl.program_id(1)
        s = g_ref[...] + r_ref[...]
        pb_ref[...] = s.astype(BF16)

        @pl.when(q == pl_ref[1])
        def _():
            pm_ref[...] = s

    grid_spec = pltpu.PrefetchScalarGridSpec(
        num_scalar_prefetch=1, grid=(R // tr, 4),
        in_specs=[pl.BlockSpec((None, tr, C), lambda r, q, p: (2 * q + p[0], r, 0)),
                  pl.BlockSpec((None, tr, C), lambda r, q, p: (q, r, 0))],
        out_specs=[pl.BlockSpec((None, tr, C), lambda r, q, p: (q, r, 0)),
                   pl.BlockSpec((tr, C), lambda r, q, p: (r, 0))])
    return pl.pallas_call(
        body, name=name, grid_spec=grid_spec,
        out_shape=[jax.ShapeDtypeStruct((4, R, C), BF16), jax.ShapeDtypeStruct((R, C), F32)],
        compiler_params=_params("parallel", "arbitrary"))(place, g, r1)


def _adamw(w, gparts, m, v, name):
    lead = w.ndim == 3
    R, C = w.shape[-2:]
    tr = min(256, R)
    assert R % tr == 0
    g0, g3 = gparts

    def body(w_ref, g0_ref, *rest):
        if g3 is not None:
            g3_ref, m_ref, v_ref, go, do, mo, vo = rest
        else:
            m_ref, v_ref, go, do, mo, vo = rest
        g = g0_ref[...]
        if g3 is not None:
            for k in range(3):
                g = g + g3_ref[k].astype(F32)
        wv = w_ref[...]
        mn = ADAM_B1 * m_ref[...] + (1.0 - ADAM_B1) * g
        vn = ADAM_B2 * v_ref[...] + (1.0 - ADAM_B2) * jnp.square(g)
        m_hat = mn / (1.0 - ADAM_B1 ** ADAM_STEP)
        v_hat = vn / (1.0 - ADAM_B2 ** ADAM_STEP)
        go[...] = g
        do[...] = -ADAM_LR * (m_hat / (jnp.sqrt(v_hat) + ADAM_EPS) + ADAM_WD * wv)
        mo[...] = mn
        vo[...] = vn

    blk = pl.BlockSpec((tr, C), lambda i: (i, 0))
    wblk = pl.BlockSpec((None, tr, C), lambda i: (0, i, 0)) if lead else blk
    in_specs = [wblk, blk] + ([pl.BlockSpec((3, tr, C), lambda i: (0, i, 0))] if g3 is not None else []) + [wblk, wblk]
    args = [w, g0] + ([g3] if g3 is not None else []) + [m, v]
    return pl.pallas_call(
        body, name=name, grid=(R // tr,), in_specs=in_specs, out_specs=[wblk] * 4,
        out_shape=[jax.ShapeDtypeStruct(w.shape, F32)] * 4,
        compiler_params=_params("parallel"))(*args)


def _adamw_layers(w, parts, m, v, name):
    _, R, C = w.shape
    tr = min(256, R)
    assert R % tr == 0

    def body(w_ref, p0, r0, p1, r1, m_ref, v_ref, go, do, mo, vo):
        gs = []
        for p, r in ((p0, r0), (p1, r1)):
            g = p[...]
            for k in range(3):
                g = g + r[k].astype(F32)
            gs.append(g)
        g = jnp.where(pl.program_id(0) == 0, gs[0], gs[1])
        mn = ADAM_B1 * m_ref[...] + (1.0 - ADAM_B1) * g
        vn = ADAM_B2 * v_ref[...] + (1.0 - ADAM_B2) * jnp.square(g)
        m_hat = mn / (1.0 - ADAM_B1 ** ADAM_STEP)
        v_hat = vn / (1.0 - ADAM_B2 ** ADAM_STEP)
        go[...] = g
        do[...] = -ADAM_LR * (m_hat / (jnp.sqrt(v_hat) + ADAM_EPS) + ADAM_WD * w_ref[...])
        mo[...] = mn
        vo[...] = vn

    lay = pl.BlockSpec((None, tr, C), lambda l, i: (l, i, 0))
    one = pl.BlockSpec((tr, C), lambda l, i: (i, 0))
    three = pl.BlockSpec((3, tr, C), lambda l, i: (0, i, 0))
    return pl.pallas_call(
        body, name=name, grid=(2, R // tr), in_specs=[lay, one, three, one, three, lay, lay],
        out_specs=[lay] * 4, out_shape=[jax.ShapeDtypeStruct((2, R, C), F32)] * 4,
        compiler_params=_params("parallel", "parallel"))(w, parts[0][0], parts[0][1], parts[1][0], parts[1][1], m, v)


SMALL_SHARDED = ("rg_conv_w", "rg_b_a", "rg_b_x", "rg_lambda", "gla_w_gate_up", "gla_b_gate", "gla_norm")
SMALL_REPLICATED = ("norm_mix", "norm_mlp", "norm_final", "rg_conv_b", "rg_w_a", "rg_w_x", "hg_lb_logits", "hg_norm")
WEIGHT_NAMES = ("norm_mix", "norm_mlp", "norm_final", "mlp_w1", "mlp_w2", "ab_w_in", "ab_w_out", "rg_conv_w",
                "rg_conv_b", "rg_w_a", "rg_b_a", "rg_w_x", "rg_b_x", "rg_lambda", "hg_lb_logits", "hg_norm",
                "gla_w_in", "gla_w_out", "gla_w_gate_up", "gla_b_gate", "gla_norm")


def _rows128(a):
    return a.reshape(-1, LANES)


def _part_rows(a):
    return -(-(a.size // LANES) // SUBLANES) * SUBLANES


def _pack_rows(arrays, pad_to=SUBLANES):
    parts = [jnp.pad(_rows128(a), ((0, _part_rows(a) - a.size // LANES), (0, 0))) for a in arrays]
    total = sum(p.shape[0] for p in parts)
    extra = (-total) % pad_to
    if extra:
        parts.append(jnp.zeros((extra, LANES), parts[0].dtype))
    return jnp.concatenate(parts, axis=0)


def _unshard_last(g, shape_local):
    nd = len(shape_local)
    t = g.reshape((N_DEV,) + tuple(shape_local))
    t = jnp.moveaxis(t, 0, nd - 1)
    return t.reshape(tuple(shape_local[:-1]) + (N_DEV * shape_local[-1],))


def _block_diag(w):
    eye = jnp.eye(8, dtype=w.dtype)
    return (w[:, :, :, None, :] * eye[None, :, None, :, None]).reshape(2, RG_W, RG_W)


def _block_diag_extract(dw):
    t = dw.reshape(2, 8, 64, 8, 64)
    return jnp.moveaxis(jnp.diagonal(t, axis1=1, axis2=3), -1, 1)


def kernel(x, norm_mix, norm_mlp, norm_final, mlp_w1, mlp_w2, ab_w_in, ab_w_out, rg_conv_w, rg_conv_b, rg_w_a, rg_b_a, rg_w_x, rg_b_x, rg_lambda, hg_lb_logits, hg_norm, gla_w_in, gla_w_out, gla_w_gate_up, gla_b_gate, gla_norm, loss_target, m_norm_mix, m_norm_mlp, m_norm_final, m_mlp_w1, m_mlp_w2, m_ab_w_in, m_ab_w_out, m_rg_conv_w, m_rg_conv_b, m_rg_w_a, m_rg_b_a, m_rg_w_x, m_rg_b_x, m_rg_lambda, m_hg_lb_logits, m_hg_norm, m_gla_w_in, m_gla_w_out, m_gla_w_gate_up, m_gla_b_gate, m_gla_norm, v_norm_mix, v_norm_mlp, v_norm_final, v_mlp_w1, v_mlp_w2, v_ab_w_in, v_ab_w_out, v_rg_conv_w, v_rg_conv_b, v_rg_w_a, v_rg_b_a, v_rg_w_x, v_rg_b_x, v_rg_lambda, v_hg_lb_logits, v_hg_norm, v_gla_w_in, v_gla_w_out, v_gla_w_gate_up, v_gla_b_gate, v_gla_norm):
    w_loc = dict(norm_mix=norm_mix, norm_mlp=norm_mlp, norm_final=norm_final, mlp_w1=mlp_w1, mlp_w2=mlp_w2,
                 ab_w_in=ab_w_in, ab_w_out=ab_w_out, rg_conv_w=rg_conv_w, rg_conv_b=rg_conv_b, rg_w_a=rg_w_a,
                 rg_b_a=rg_b_a, rg_w_x=rg_w_x, rg_b_x=rg_b_x, rg_lambda=rg_lambda, hg_lb_logits=hg_lb_logits,
                 hg_norm=hg_norm, gla_w_in=gla_w_in, gla_w_out=gla_w_out, gla_w_gate_up=gla_w_gate_up,
                 gla_b_gate=gla_b_gate, gla_norm=gla_norm)
    m_loc = dict(norm_mix=m_norm_mix, norm_mlp=m_norm_mlp, norm_final=m_norm_final, mlp_w1=m_mlp_w1,
                 mlp_w2=m_mlp_w2, ab_w_in=m_ab_w_in, ab_w_out=m_ab_w_out, rg_conv_w=m_rg_conv_w,
                 rg_conv_b=m_rg_conv_b, rg_w_a=m_rg_w_a, rg_b_a=m_rg_b_a, rg_w_x=m_rg_w_x, rg_b_x=m_rg_b_x,
                 rg_lambda=m_rg_lambda, hg_lb_logits=m_hg_lb_logits, hg_norm=m_hg_norm, gla_w_in=m_gla_w_in,
                 gla_w_out=m_gla_w_out, gla_w_gate_up=m_gla_w_gate_up, gla_b_gate=m_gla_b_gate,
                 gla_norm=m_gla_norm)
    v_loc = dict(norm_mix=v_norm_mix, norm_mlp=v_norm_mlp, norm_final=v_norm_final, mlp_w1=v_mlp_w1,
                 mlp_w2=v_mlp_w2, ab_w_in=v_ab_w_in, ab_w_out=v_ab_w_out, rg_conv_w=v_rg_conv_w,
                 rg_conv_b=v_rg_conv_b, rg_w_a=v_rg_w_a, rg_b_a=v_rg_b_a, rg_w_x=v_rg_w_x, rg_b_x=v_rg_b_x,
                 rg_lambda=v_rg_lambda, hg_lb_logits=v_hg_lb_logits, hg_norm=v_hg_norm, gla_w_in=v_gla_w_in,
                 gla_w_out=v_gla_w_out, gla_w_gate_up=v_gla_w_gate_up, gla_b_gate=v_gla_b_gate,
                 gla_norm=v_gla_norm)

    T = x.shape[1]
    h0 = x.reshape(T, D_MODEL)
    target = loss_target.reshape(T, D_MODEL)
    ax, ay, ac = lax.axis_index("x"), lax.axis_index("y"), lax.axis_index("c")
    dev = 4 * ax + 2 * ay + ac
    place = jnp.stack([ac, 2 * ax + ay]).astype(jnp.int32)

    abin_shard = ab_w_in[0].astype(BF16)
    first_started = _copies_start(_plan_gather_first, 4, [abin_shard], _landing(N_DEV, [abin_shard[None]]),
                                  "ag_first_start")
    rest_shards = [mlp_w1[0].astype(BF16), mlp_w2[0].astype(BF16), gla_w_in[0].astype(BF16),
                   gla_w_out[0].astype(BF16), mlp_w1[1].astype(BF16), mlp_w2[1].astype(BF16),
                   _after(first_started[4], ab_w_out[0].astype(BF16))]
    ag_started = _copies_start(_plan_gather_first, 4 * len(rest_shards), rest_shards,
                               _landing(N_DEV, [s[None] for s in rest_shards]), "ag_rest_start")

    small_local = [w_loc[n] for n in SMALL_SHARDED]
    small_g = _allgather_vmem(_pack_rows(small_local, 8), "ag_small")
    small_g = small_g.reshape(N_DEV, -1, LANES)
    full = {}
    off = 0
    for n, a in zip(SMALL_SHARDED, small_local):
        full[n] = _unshard_last(small_g[:, off:off + a.size // LANES].reshape(N_DEV, a.size), a.shape)
        off += _part_rows(a)
    conv_w = full["rg_conv_w"][0]
    b_a, b_x, lam = full["rg_b_a"][0], full["rg_b_x"][0], full["rg_lambda"][0]
    w_up, b_gate, g_norm = full["gla_w_gate_up"][0], full["gla_b_gate"][0], full["gla_norm"]

    cw8 = jnp.pad(conv_w, ((0, 4), (0, 0)))
    wbd = jnp.concatenate([_block_diag(rg_w_a[0]), _block_diag(rg_w_x[0])], axis=2).astype(BF16)
    rg_bias = jnp.concatenate([b_a, b_x], axis=1).reshape(2, 1, 2 * RG_W)
    lam3 = lam.reshape(2, 1, RG_W)
    l0, l1 = hg_lb_logits[0:1], hg_lb_logits[1:2]
    wup_pad = jnp.zeros((2, LANES, 512), F32).at[0, 0:16].set(w_up[0]).at[1, 16:32].set(w_up[1])
    bg3 = b_gate.reshape(2, 1, 512)
    nmix0, nmix1 = norm_mix[0:1], norm_mix[1:2]
    nmlp0, nmlp1 = norm_mlp[0:1], norm_mlp[1:2]
    nfin = norm_final.reshape(1, D_MODEL)

    prepared = (ag_started[4] + cw8[:, 0:LANES] + wup_pad[0, 0:SUBLANES, 0:LANES] + rg_bias[0, :, 0:LANES]
                + wbd[0, 0:SUBLANES, 0:LANES].astype(F32) + lam3[0, :, 0:LANES] + bg3[0, :, 0:LANES])
    (abin_shard,), abin_l = _copies_wait(_plan_gather_first, first_started, prepared, "ag_first_wait")
    first_pass = _copies_start(_plan_gather_pass, 3, [], abin_l, "ag_first_pass_start")
    _, (abin_g,) = _copies_wait(_plan_gather_pass, first_pass, first_pass[4], "ag_first_pass_wait")
    abin_g = lax.dynamic_update_index_in_dim(abin_g, abin_shard, dev, 0)
    wab_in = jnp.transpose(abin_g, (1, 0, 2)).reshape(D_MODEL, AB_IN)
    proj0, y0 = _norm_matmul(h0, _after(ag_started[4], nmix0), wab_in, "l0_in_proj")
    xc = _rg_conv_fwd(proj0, cw8, rg_conv_b, "rg_conv")
    hs = _rg_scan_fwd(xc, wbd, rg_bias, lam3, "rg_scan")
    o_hg, s_hg = _hg_fwd(proj0, l0, l1, "hg_chunks")
    both_done = hs[0][0:SUBLANES, 0:LANES] + o_hg[0][0:SUBLANES, 0:LANES]
    rest_shards, rest_lands = _copies_wait(_plan_gather_first, ag_started, both_done, "ag_rest_wait")
    pass_started = _copies_start(_plan_gather_pass, 3 * len(rest_lands), [], rest_lands, "ag_pass_start")
    mixin0 = _l0_combine_fwd(hs, proj0, o_hg, _after(pass_started[4], hg_norm), "l0_combine")
    _, rest_g = _copies_wait(_plan_gather_pass, pass_started, mixin0, "ag_pass_wait")
    rest_g = [lax.dynamic_update_index_in_dim(g, s, dev, 0) for g, s in zip(rest_g, rest_shards)]
    wab_out = rest_g[6].reshape(D_MODEL, D_MODEL)
    h1 = _matmul_res(mixin0, wab_out, h0, "l0_out_proj")
    w1g = (rest_g[0], rest_g[4])
    w2f = (rest_g[1].reshape(D_FF, D_MODEL), rest_g[5].reshape(D_FF, D_MODEL))
    wgla_in = jnp.pad(jnp.transpose(rest_g[2], (1, 0, 2)).reshape(D_MODEL, GLA_IN),
                      ((0, 0), (0, GLA_IN_PAD - GLA_IN)))
    wgla_out = rest_g[3].reshape(D_MODEL, D_MODEL)
    h2, pre0, ym0 = _mlp_fwd(h1, nmlp0, w1g[0], w2f[0], "mlp0")
    proj1, y1 = _norm_matmul(h2, nmix1, wgla_in, "l1_in_proj")
    z_gate, lr_b = _gate_logits(proj1, wup_pad, bg3, "gla_gate_logits")
    o_gla, s_gla = _gla_fwd(proj1, z_gate, "gla_chunks")
    mixin1 = _l1_combine_fwd(o_gla, proj1, g_norm, "l1_combine")
    h3 = _matmul_res(mixin1, wgla_out, h2, "l1_out_proj")
    h4, pre1, ym1 = _mlp_fwd(h3, nmlp1, w1g[1], w2f[1], "mlp1")
    loss_blk, dh4, dh4b, d_nfin = _final_loss(h4, nfin, target, "final_loss")

    dh3, dh3b, dpre1, act1, d_nmlp1 = _mlp_bwd(dh4, dh4b, h3, nmlp1, pre1, w1g[1], w2f[1], "mlp1_bwd")
    g_w1_1 = _wgrad(ym1, dpre1, 512, "mlp1_dw1", sharded_cols=True)
    g_w2_1 = _wgrad(act1, dh4b, 512, "mlp1_dw2")
    dmixin1 = _dgrad(dh3b, wgla_out, "l1_out_dgrad")
    g_gla_out = _wgrad(mixin1, dh3b, 512, "l1_out_dw")
    do_gla, dr, d_gnorm = _l1_combine_bwd(o_gla, proj1, g_norm, dmixin1, "l1_combine_bwd")
    dq1, dk1, dv1, dz_gate = _gla_bwd(proj1, z_gate, s_gla, do_gla, "gla_chunks_bwd")
    dlr1, d_bg, dz_b = _gate_logits_bwd(dz_gate, wup_pad, "gla_gate_logits_bwd")
    d_wup = [_wgrad(lr_b, dz_b[d], 512, "gla_gate_dw%d" % d) for d in range(2)]
    dproj1 = _l1_assemble(dq1, dk1, dv1, dr, dlr1, "l1_assemble")
    dh2, dh2b, d_nmix1 = _dgrad_norm(dproj1, wgla_in, h2, nmix1, dh3, "l1_in_dgrad")
    g_gla_in = _wgrad(y1, dproj1, 640, "l1_in_dw")

    def reduce_start(grads, tag):
        return _copies_start(_plan_grads_sibling, 4 * len(grads), grads, _landing(4, grads), "rs_%s_d2d_start" % tag)

    def reduce_mid(started, after, tag):
        grads, got = _copies_wait(_plan_grads_sibling, started, after, "rs_%s_d2d_wait" % tag)
        parts = [_chip_partial(g, r, place, "rs_%s_partial%d" % (tag, a)) for a, (g, r) in enumerate(zip(grads, got))]
        pb = [p[0] for p in parts]
        return _copies_start(_plan_grads_chips, 3 * len(pb), pb, _landing(3, pb), "rs_%s_ici_start" % tag), \
            [p[1] for p in parts]

    def reduce_end(started, mine, after, tag):
        _, got = _copies_wait(_plan_grads_chips, started, after, "rs_%s_ici_wait" % tag)
        return list(zip(mine, got))

    slots_l1 = [g_w1_1, g_w2_1.reshape(N_DEV, 512, D_MODEL),
                jnp.transpose(g_gla_in[:, :GLA_IN].reshape(D_MODEL, N_DEV, GLA_IN // N_DEV), (1, 0, 2)),
                g_gla_out.reshape(N_DEV, 128, D_MODEL)]
    ra_d2d = reduce_start(slots_l1, "l1")

    dh1, dh1b, dpre0, act0, d_nmlp0 = _mlp_bwd(dh2, dh2b, h1, _after(ra_d2d[4], nmlp0), pre0, w1g[0], w2f[0],
                                               "mlp0_bwd")
    g_w1_0 = _wgrad(ym0, dpre0, 512, "mlp0_dw1", sharded_cols=True)
    g_w2_0 = _wgrad(act0, dh2b, 512, "mlp0_dw2")
    ra_ici, ra_mine = reduce_mid(ra_d2d, g_w2_0, "l1")
    rb_d2d = reduce_start([g_w1_0, g_w2_0.reshape(N_DEV, 512, D_MODEL)], "mlp0")
    dmixin0 = _dgrad(dh1b, wab_out, "l0_out_dgrad")
    g_ab_out = _wgrad(mixin0, dh1b, 512, "l0_out_dw")
    dho, dga, do_hg, dg_gate, d_hgnorm = _l0_combine_bwd(
        hs, proj0, o_hg, _after(rb_d2d[4], _after(ra_ici[4], hg_norm)), dmixin0, "l0_combine_bwd")
    dxc, d_wbd, d_rgb, d_lam = _rg_scan_bwd(xc, wbd, rg_bias, lam3, hs, dho, "rg_scan_bwd")
    dxa, d_cw8, d_cb = _rg_conv_bwd(dxc, proj0, cw8, "rg_conv_bwd")
    dq0, df0, dv0, d_l0, d_l1 = _hg_bwd(proj0, l0, l1, s_hg, do_hg, "hg_chunks_bwd")
    rb_ici, rb_mine = reduce_mid(rb_d2d, d_l0, "mlp0")
    dproj0 = _l0_assemble(dxa, dga, dq0, df0, dv0, dg_gate, "l0_assemble")
    dx, _, d_nmix0 = _dgrad_norm(dproj0, wab_in, h0, _after(rb_ici[4], nmix0), dh1, "l0_in_dgrad")

    d_wa = _block_diag_extract(d_wbd[:, :, :RG_W])[None]
    d_wx = _block_diag_extract(d_wbd[:, :, RG_W:])[None]
    small_full = {
        "norm_mix": jnp.concatenate([d_nmix0, d_nmix1], axis=0), "norm_mlp": jnp.concatenate([d_nmlp0, d_nmlp1], axis=0),
        "norm_final": d_nfin.reshape(D_MODEL), "rg_conv_b": d_cb, "rg_w_a": d_wa, "rg_w_x": d_wx,
        "hg_lb_logits": jnp.concatenate([d_l0[0] + d_l0[1], d_l1[0] + d_l1[1]], axis=0), "hg_norm": d_hgnorm,
        "rg_conv_w": d_cw8[0:4][None], "rg_b_a": d_rgb[:, 0, :RG_W][None], "rg_b_x": d_rgb[:, 0, RG_W:][None],
        "rg_lambda": d_lam[:, 0, :][None],
        "gla_w_gate_up": jnp.stack([d_wup[0][0:16], d_wup[1][16:32]])[None], "gla_b_gate": d_bg[:, 0, :][None],
        "gla_norm": d_gnorm}
    small_names = SMALL_REPLICATED + SMALL_SHARDED
    packed = _pack_rows([loss_blk] + [small_full[n] for n in small_names], 256)
    ar_first = _copies_start(_plan_gather_first, 4, [packed], _landing(N_DEV, [packed[None]]), "ar_small_start")

    g_ab_in = _wgrad(y0, dproj0, 512, "l0_in_dw", behind=ar_first[4])
    rc_d2d = reduce_start([jnp.transpose(g_ab_in.reshape(D_MODEL, N_DEV, AB_IN // N_DEV), (1, 0, 2)),
                           g_ab_out.reshape(N_DEV, 128, D_MODEL)], "ab")
    (packed,), ar_lands = _copies_wait(_plan_gather_first, ar_first, rc_d2d[4], "ar_small_wait")
    ar_pass = _copies_start(_plan_gather_pass, 3, [], ar_lands, "ar_small_pass_start")
    rc_ici, rc_mine = reduce_mid(rc_d2d, ar_pass[4], "ab")
    _, (ar_gathered,) = _copies_wait(_plan_gather_pass, ar_pass, rc_ici[4], "ar_small_pass_wait")
    summed = _sum_slots(lax.dynamic_update_index_in_dim(ar_gathered, packed, dev, 0), "ar_small_sum")
    loss = summed[0, 0]

    pieces_l1 = reduce_end(ra_ici, ra_mine, rc_ici[4], "l1")
    res_gla_in = _adamw(gla_w_in, pieces_l1[2], m_gla_w_in, v_gla_w_in, "adamw_gla_in")
    res_gla_out = _adamw(gla_w_out, pieces_l1[3], m_gla_w_out, v_gla_w_out, "adamw_gla_out")
    pieces_mlp0 = reduce_end(rb_ici, rb_mine, res_gla_out[0], "mlp0")
    res_w1 = _adamw_layers(mlp_w1, (pieces_mlp0[0], pieces_l1[0]), m_mlp_w1, v_mlp_w1, "adamw_mlp_w1")
    res_w2 = _adamw_layers(mlp_w2, (pieces_mlp0[1], pieces_l1[1]), m_mlp_w2, v_mlp_w2, "adamw_mlp_w2")
    res = {"mlp_w1": tuple(res_w1), "mlp_w2": tuple(res_w2),
           "gla_w_in": tuple(res_gla_in), "gla_w_out": tuple(res_gla_out)}

    g_small = {}
    off = SUBLANES
    for n in small_names:
        a = small_full[n]
        gfull = summed[off:off + a.size // LANES].reshape(a.shape)
        off += _part_rows(a)
        if n in SMALL_SHARDED:
            loc = w_loc[n].shape[-1]
            gfull = lax.dynamic_slice_in_dim(gfull, dev * loc, loc, axis=gfull.ndim - 1)
        g_small[n] = gfull
    sw = _pack_rows([w_loc[n] for n in small_names], 256)
    sg = _pack_rows([g_small[n] for n in small_names], 256)
    sm = _pack_rows([m_loc[n] for n in small_names], 256)
    sv = _pack_rows([v_loc[n] for n in small_names], 256)
    small_res = _adamw(sw, (sg, None), sm, sv, "adamw_small")
    others_done = (res_w1[1][0, 0:SUBLANES, 0:LANES] + res_w2[1][0, 0:SUBLANES, 0:LANES]
                   + res_gla_in[1][0, 0:SUBLANES, 0:LANES] + small_res[1][0:SUBLANES, 0:LANES])
    pieces_ab = reduce_end(rc_ici, rc_mine, others_done, "ab")
    res["ab_w_in"] = tuple(_adamw(ab_w_in, pieces_ab[0], m_ab_w_in, v_ab_w_in, "adamw_ab_in"))
    res["ab_w_out"] = tuple(_adamw(ab_w_out, pieces_ab[1], m_ab_w_out, v_ab_w_out, "adamw_ab_out"))
    off = 0
    for n in small_names:
        a = w_loc[n]
        nr = a.size // LANES
        res[n] = tuple(small_res[k][off:off + nr].reshape(a.shape) for k in range(4))
        off += _part_rows(a)

    grad_x = dx.reshape(1, T, D_MODEL)
    out = [loss, grad_x]
    for k in range(4):
        out += [res[n][k] for n in WEIGHT_NAMES]
    return tuple(out)
```

```python
import jax
import jax.numpy as jnp
from jax import lax
from jax.experimental import pallas as pl
from jax.experimental.pallas import tpu as pltpu

F32, BF16 = jnp.float32, jnp.bfloat16
HI = lax.Precision.HIGHEST
MESH = pl.DeviceIdType.MESH

D_MODEL = 1024
D_FF = 4096
RG_W = 512
HG_W = 512
CHUNK = 64
EPS = 1e-6
RG_C = 8.0
AB_IN = 3584
GLA_IN = 3104
GLA_IN_PAD = 3200
N_DEV = 8
LANES = 128
SUBLANES = 8
VMEM_LIMIT = 48 * 1024 * 1024

ADAM_LR, ADAM_B1, ADAM_B2, ADAM_EPS, ADAM_WD, ADAM_STEP = 0.001, 0.9, 0.999, 1e-08, 0.01, 10


def _params(*sem):
    return pltpu.CompilerParams(dimension_semantics=sem, vmem_limit_bytes=VMEM_LIMIT)


def _dg(a, b, ca, cb):
    return lax.dot_general(a.astype(BF16), b.astype(BF16), (((ca,), (cb,)), ((), ())),
                           preferred_element_type=F32)


@jax.custom_vjp
def _mm_nn(a, b):
    return _dg(a, b, 1, 0)


_mm_nn.defvjp(lambda a, b: (_dg(a, b, 1, 0), (a, b)),
              lambda res, g: (_dg(g, res[1], 1, 1), _dg(res[0], g, 0, 0)))


@jax.custom_vjp
def _mm_nt(a, b):
    return _dg(a, b, 1, 1)


_mm_nt.defvjp(lambda a, b: (_dg(a, b, 1, 1), (a, b)),
              lambda res, g: (_dg(g, res[1], 1, 0), _dg(g, res[0], 0, 0)))


@jax.custom_vjp
def _mm_tn(a, b):
    return _dg(a, b, 0, 0)


_mm_tn.defvjp(lambda a, b: (_dg(a, b, 0, 0), (a, b)),
              lambda res, g: (_dg(res[1], g, 1, 1), _dg(res[0], g, 1, 0)))


@jax.custom_vjp
def _cum(tri, tri_t, x):
    return jnp.dot(tri, x, precision=HI, preferred_element_type=F32)


_cum.defvjp(lambda tri, tri_t, x: (jnp.dot(tri, x, precision=HI, preferred_element_type=F32), (tri, tri_t)),
            lambda res, g: (jnp.zeros_like(res[0]), jnp.zeros_like(res[1]),
                            jnp.dot(res[1], g, precision=HI, preferred_element_type=F32)))


def _sig(x):
    return 1.0 / (1.0 + jnp.exp(-x))


def _gelu(x):
    return 0.5 * x * (1.0 + jnp.tanh(0.7978845608028654 * (x + 0.044715 * (x * x * x))))


def _softplus(z):
    return jnp.maximum(z, 0.0) + jnp.log(1.0 + jnp.exp(-jnp.abs(z)))


def _rms(x):
    return lax.rsqrt(jnp.mean(x * x, axis=-1, keepdims=True) + EPS)


def _rmsnorm_bwd(x, gain, dy):
    r = _rms(x)
    xh = x * r
    dgain = jnp.sum(dy * xh, axis=0, keepdims=True)
    dxh = dy * gain
    dx = r * (dxh - xh * jnp.mean(dxh * xh, axis=-1, keepdims=True))
    return dx, dgain


def _headnorm(o, gain, n_heads, hd):
    parts = []
    for h in range(n_heads):
        oh = o[:, h * hd:(h + 1) * hd]
        parts.append(oh * _rms(oh))
    return jnp.concatenate(parts, axis=1) * gain


def _tri_consts(d):
    row = lax.broadcasted_iota(jnp.int32, (CHUNK, CHUNK), 0)
    col = lax.broadcasted_iota(jnp.int32, (CHUNK, CHUNK), 1)
    ge = (row >= col).astype(F32)
    le = (row <= col).astype(F32)
    r1 = lax.broadcasted_iota(jnp.int32, (CHUNK, 1), 0)
    if d == 0:
        return ge, le, (r1 <= CHUNK // 2).astype(F32)
    return le, ge, (r1 >= CHUNK // 2 - 1).astype(F32)


def _chunk_core(qh, k, v, logf, st_prev, tri, tri_t, mref, n_heads, dk, dv):
    cum = _cum(tri, tri_t, logf)
    ref = jnp.sum(logf * mref, axis=0, keepdims=True)
    last = jnp.sum(logf, axis=0, keepdims=True)
    q_in = qh * jnp.exp(cum - ref)
    k_in = k * jnp.exp(ref - cum)
    k_st = k * jnp.exp(last - cum)
    q_dec = qh * jnp.exp(cum)
    decay = jnp.exp(last)
    outs, sts = [], []
    for h in range(n_heads):
        sk = slice(h * dk, (h + 1) * dk)
        sv = slice(h * dv, (h + 1) * dv)
        sc = _mm_nt(q_in[:, sk], k_in[:, sk]) * tri
        o = _mm_nn(sc, v[:, sv]) + _mm_nt(q_dec[:, sk], st_prev[h])
        sts.append(st_prev[h] * decay[:, sk] + _mm_tn(v[:, sv], k_st[:, sk]))
        outs.append(o)
    return jnp.concatenate(outs, axis=1), tuple(sts)


def _hg_chunk(q, f, v, l0, l1, st_prev, tri, tri_t, mref):
    lb = _sig(l0 - l1)
    sg = _sig(f)
    qh = q * _sig(q)
    logf = jnp.log(lb + (1.0 - lb) * sg)
    k = (1.0 - lb) * (1.0 - sg)
    return _chunk_core(qh, k, v, logf, st_prev, tri, tri_t, mref, 4, 128, 128)


def _gla_chunk(q, k, v, z, st_prev, tri, tri_t, mref):
    logf = (jnp.minimum(z, 0.0) - jnp.log(1.0 + jnp.exp(-jnp.abs(z)))) * (1.0 / 16.0)
    qh = q * (128.0 ** -0.5)
    return _chunk_core(qh, k, v, logf, st_prev, tri, tri_t, mref, 4, 128, 256)


def _rg_gates(xc, wbd, bias, lam):
    z = _mm_nn(xc, wbd) + bias
    r = _sig(z[:, :RG_W])
    i = _sig(z[:, RG_W:])
    log_a = -RG_C * r * _softplus(-lam)
    a = jnp.exp(log_a)
    x2 = 2.0 * log_a
    neg_expm1 = jnp.where(x2 > -1e-2, -(x2 + 0.5 * x2 * x2 + x2 * x2 * x2 * (1.0 / 6.0)), 1.0 - jnp.exp(x2))
    u = jnp.sqrt(neg_expm1) * (i * xc)
    return a, u


def _l0_combine(hf, hb, ga, of, ob, g, gain):
    ya = (hf + hb) * _gelu(ga)
    yb = _headnorm(of + ob, gain, 4, 128) * (g * _sig(g))
    return jnp.concatenate([ya, yb], axis=1)


def _l1_combine(of, ob, r, gain):
    return _headnorm(of + ob, gain, 4, 256) * (r * _sig(r))


def _norm_matmul(h, gain, w, name):
    T, D = h.shape
    N = w.shape[1]
    tm = min(512, T)

    def body(h_ref, g_ref, w_ref, o_ref, y_ref):
        x = h_ref[...]
        y = (x * _rms(x) * g_ref[...]).astype(BF16)
        y_ref[...] = y
        o_ref[...] = jnp.dot(y, w_ref[...], preferred_element_type=F32)

    return pl.pallas_call(
        body, name=name, grid=(T // tm,),
        in_specs=[pl.BlockSpec((tm, D), lambda i: (i, 0)), pl.BlockSpec((1, D), lambda i: (0, 0)),
                  pl.BlockSpec((D, N), lambda i: (0, 0))],
        out_specs=[pl.BlockSpec((tm, N), lambda i: (i, 0)), pl.BlockSpec((tm, D), lambda i: (i, 0))],
        out_shape=[jax.ShapeDtypeStruct((T, N), F32), jax.ShapeDtypeStruct((T, D), BF16)],
        compiler_params=_params("parallel"))(h, gain, w)


def _matmul_res(a, w, res, name):
    T, K = a.shape
    N = w.shape[1]
    tm = min(512, T)

    def body(a_ref, w_ref, r_ref, o_ref):
        o_ref[...] = r_ref[...] + jnp.dot(a_ref[...], w_ref[...], preferred_element_type=F32)

    return pl.pallas_call(
        body, name=name, grid=(T // tm,),
        in_specs=[pl.BlockSpec((tm, K), lambda i: (i, 0)), pl.BlockSpec((K, N), lambda i: (0, 0)),
                  pl.BlockSpec((tm, N), lambda i: (i, 0))],
        out_specs=pl.BlockSpec((tm, N), lambda i: (i, 0)),
        out_shape=jax.ShapeDtypeStruct((T, N), F32),
        compiler_params=_params("parallel"))(a, w, res)


def _dgrad_norm(dproj, w, h, gain, dres, name):
    T, N = dproj.shape
    D = w.shape[0]
    tm = min(512, T)

    def body(dp_ref, w_ref, h_ref, g_ref, dr_ref, dh_ref, dhb_ref, dg_ref):
        @pl.when(pl.program_id(0) == 0)
        def _():
            dg_ref[...] = jnp.zeros_like(dg_ref)

        dy = _dg(dp_ref[...], w_ref[...], 1, 1)
        dx, dgain = _rmsnorm_bwd(h_ref[...], g_ref[...], dy)
        dh = dr_ref[...] + dx
        dh_ref[...] = dh
        dhb_ref[...] = dh.astype(BF16)
        dg_ref[...] += dgain

    return pl.pallas_call(
        body, name=name, grid=(T // tm,),
        in_specs=[pl.BlockSpec((tm, N), lambda i: (i, 0)), pl.BlockSpec((D, N), lambda i: (0, 0)),
                  pl.BlockSpec((tm, D), lambda i: (i, 0)), pl.BlockSpec((1, D), lambda i: (0, 0)),
                  pl.BlockSpec((tm, D), lambda i: (i, 0))],
        out_specs=[pl.BlockSpec((tm, D), lambda i: (i, 0)), pl.BlockSpec((tm, D), lambda i: (i, 0)),
                   pl.BlockSpec((1, D), lambda i: (0, 0))],
        out_shape=[jax.ShapeDtypeStruct((T, D), F32), jax.ShapeDtypeStruct((T, D), BF16),
                   jax.ShapeDtypeStruct((1, D), F32)],
        compiler_params=_params("arbitrary"))(dproj, w, h, gain, dres)


def _wgrad(a, b, tn, name, sharded_cols=False, behind=None):
    T, K = a.shape
    N = b.shape[1]
    tk = min(1024, K)

    def body(a_ref, b_ref, *rest):
        rest[-1][...] = _dg(a_ref[...], b_ref[...], 0, 0)

    if sharded_cols:
        out_spec = pl.BlockSpec((None, tk, tn), lambda k, n: (n, k, 0))
        out_shape = jax.ShapeDtypeStruct((N // tn, K, tn), F32)
    else:
        out_spec = pl.BlockSpec((tk, tn), lambda k, n: (k, n))
        out_shape = jax.ShapeDtypeStruct((K, N), F32)
    in_specs = [pl.BlockSpec((T, tk), lambda k, n: (0, k)), pl.BlockSpec((T, tn), lambda k, n: (0, n))]
    args = [a, b]
    if behind is not None:
        in_specs.append(pl.BlockSpec((SUBLANES, LANES), lambda k, n: (0, 0)))
        args.append(behind)
    return pl.pallas_call(
        body, name=name, grid=(K // tk, N // tn), in_specs=in_specs, out_specs=out_spec, out_shape=out_shape,
        compiler_params=_params("parallel", "parallel"))(*args)


def _resident(shape):
    return pl.BlockSpec(shape, lambda i: (0,) * len(shape), pipeline_mode=pl.Buffered(1))


def _mlp_fwd(h, gain, w1g, w2, name):
    T, D = h.shape
    nf, _, tf = w1g.shape
    tm = min(512, T)

    def body(h_ref, g_ref, w1_ref, w2_ref, o_ref, pre_ref, y_ref):
        x = h_ref[...]
        y = (x * _rms(x) * g_ref[...]).astype(BF16)
        y_ref[...] = y
        acc = x
        for j in range(nf):
            cols = slice(j * tf, (j + 1) * tf)
            pre = jnp.dot(y, w1_ref[j], preferred_element_type=F32)
            pre_ref[:, cols] = pre.astype(BF16)
            act = jnp.square(jnp.maximum(pre, 0.0)).astype(BF16)
            acc = acc + jnp.dot(act, w2_ref[cols, :], preferred_element_type=F32)
        o_ref[...] = acc

    return pl.pallas_call(
        body, name=name, grid=(T // tm,),
        in_specs=[pl.BlockSpec((tm, D), lambda i: (i, 0)), pl.BlockSpec((1, D), lambda i: (0, 0)),
                  _resident(w1g.shape), _resident(w2.shape)],
        out_specs=[pl.BlockSpec((tm, D), lambda i: (i, 0)), pl.BlockSpec((tm, nf * tf), lambda i: (i, 0)),
                   pl.BlockSpec((tm, D), lambda i: (i, 0))],
        out_shape=[jax.ShapeDtypeStruct((T, D), F32), jax.ShapeDtypeStruct((T, nf * tf), BF16),
                   jax.ShapeDtypeStruct((T, D), BF16)],
        compiler_params=_params("parallel"))(h, gain, w1g, w2)


def _mlp_bwd(dout, dout_b, h, gain, pre, w1g, w2, name):
    T, D = h.shape
    nf, _, tf = w1g.shape
    tm = min(256, T)

    def body(do_ref, dob_ref, h_ref, g_ref, pre_ref, w1_ref, w2_ref, dh_ref, dhb_ref, dpre_ref, act_ref, dg_ref):
        @pl.when(pl.program_id(0) == 0)
        def _():
            dg_ref[...] = jnp.zeros_like(dg_ref)

        dob = dob_ref[...]
        dy = None
        for j in range(nf):
            cols = slice(j * tf, (j + 1) * tf)
            rp = jnp.maximum(pre_ref[:, cols].astype(F32), 0.0)
            dpre = (_dg(dob, w2_ref[cols, :], 1, 1) * (2.0 * rp)).astype(BF16)
            dpre_ref[:, cols] = dpre
            act_ref[:, cols] = (rp * rp).astype(BF16)
            part = _dg(dpre, w1_ref[j], 1, 1)
            dy = part if dy is None else dy + part
        dx, dgain = _rmsnorm_bwd(h_ref[...], g_ref[...], dy)
        dh = do_ref[...] + dx
        dh_ref[...] = dh
        dhb_ref[...] = dh.astype(BF16)
        dg_ref[...] += dgain

    tok = lambda w: pl.BlockSpec((tm, w), lambda i: (i, 0))
    return pl.pallas_call(
        body, name=name, grid=(T // tm,),
        in_specs=[tok(D), tok(D), tok(D), pl.BlockSpec((1, D), lambda i: (0, 0)), tok(nf * tf),
                  _resident(w1g.shape), _resident(w2.shape)],
        out_specs=[tok(D), tok(D), tok(nf * tf), tok(nf * tf), pl.BlockSpec((1, D), lambda i: (0, 0))],
        out_shape=[jax.ShapeDtypeStruct((T, D), F32), jax.ShapeDtypeStruct((T, D), BF16),
                   jax.ShapeDtypeStruct((T, nf * tf), BF16),
                   jax.ShapeDtypeStruct((T, nf * tf), BF16), jax.ShapeDtypeStruct((1, D), F32)],
        compiler_params=_params("arbitrary"))(dout, dout_b, h, gain, pre, w1g, w2)


def _final_loss(h, gain, target, name):
    T, D = h.shape
    tm = min(512, T)

    def body(h_ref, g_ref, t_ref, l_ref, dh_ref, dhb_ref, dg_ref):
        @pl.when(pl.program_id(0) == 0)
        def _():
            l_ref[...] = jnp.zeros_like(l_ref)
            dg_ref[...] = jnp.zeros_like(dg_ref)

        x = h_ref[...]
        err = x * _rms(x) * g_ref[...] - t_ref[...]
        l_ref[...] += 0.5 * jnp.sum(jnp.mean(err * err, axis=-1, keepdims=True), axis=0, keepdims=True)
        dx, dgain = _rmsnorm_bwd(x, g_ref[...], err * (1.0 / D))
        dh_ref[...] = dx
        dhb_ref[...] = dx.astype(BF16)
        dg_ref[...] += dgain

    return pl.pallas_call(
        body, name=name, grid=(T // tm,),
        in_specs=[pl.BlockSpec((tm, D), lambda i: (i, 0)), pl.BlockSpec((1, D), lambda i: (0, 0)),
                  pl.BlockSpec((tm, D), lambda i: (i, 0))],
        out_specs=[pl.BlockSpec((SUBLANES, LANES), lambda i: (0, 0)), pl.BlockSpec((tm, D), lambda i: (i, 0)),
                   pl.BlockSpec((tm, D), lambda i: (i, 0)), pl.BlockSpec((1, D), lambda i: (0, 0))],
        out_shape=[jax.ShapeDtypeStruct((SUBLANES, LANES), F32), jax.ShapeDtypeStruct((T, D), F32),
                   jax.ShapeDtypeStruct((T, D), BF16), jax.ShapeDtypeStruct((1, D), F32)],
        compiler_params=_params("arbitrary"))(h, gain, target)


def _halo_specs(tm, T, width, col, tile=lambda i: i):
    r8 = tm // SUBLANES
    nb8 = T // SUBLANES
    return [pl.BlockSpec((tm, width), lambda i: (tile(i), col)),
            pl.BlockSpec((SUBLANES, width), lambda i: (jnp.maximum(tile(i) * r8 - 1, 0), col)),
            pl.BlockSpec((SUBLANES, width), lambda i: (jnp.minimum((tile(i) + 1) * r8, nb8 - 1), col))]


def _ext(cur, prev, nxt, has_prev, has_next):
    return jnp.concatenate([jnp.where(has_prev, prev, 0.0), cur, jnp.where(has_next, nxt, 0.0)], axis=0)


def _shifted(ext, offset, tm):
    n = ext.shape[0]
    sh = (-offset) % n
    r = ext if sh == 0 else pltpu.roll(ext, sh, 0)
    return r[SUBLANES:SUBLANES + tm]


def _rg_conv_fwd(proj, cw8, cb, name):
    T = proj.shape[0]
    tm = min(512, T)
    nT = T // tm

    def body(cur_ref, prev_ref, next_ref, w_ref, b_ref, o_ref):
        i = pl.program_id(0)
        ext = _ext(cur_ref[...], prev_ref[...], next_ref[...], i > 0, i < nT - 1)
        acc = jnp.broadcast_to(b_ref[...], (tm, RG_W))
        for k in range(4):
            acc = acc + w_ref[k:k + 1, :] * _shifted(ext, k - 2, tm)
        o_ref[...] = acc

    return pl.pallas_call(
        body, name=name, grid=(nT,),
        in_specs=_halo_specs(tm, T, RG_W, 0) + [pl.BlockSpec((SUBLANES, RG_W), lambda i: (0, 0)),
                                                pl.BlockSpec((1, RG_W), lambda i: (0, 0))],
        out_specs=pl.BlockSpec((tm, RG_W), lambda i: (i, 0)),
        out_shape=jax.ShapeDtypeStruct((T, RG_W), F32),
        compiler_params=_params("parallel"))(proj, proj, proj, cw8, cb)


def _rg_conv_bwd(dxc, proj, cw8, name):
    T = proj.shape[0]
    tm = min(512, T)
    nT = T // tm

    def body(a0, p0, n0, a1, p1, n1, xa, xp, xn, w_ref, dxa_ref, dw_ref, db_ref):
        i = pl.program_id(0)

        @pl.when(i == 0)
        def _():
            dw_ref[...] = jnp.zeros_like(dw_ref)
            db_ref[...] = jnp.zeros_like(db_ref)

        has_p, has_n = i > 0, i < nT - 1
        cur = a0[...] + a1[...]
        dext = _ext(cur, p0[...] + p1[...], n0[...] + n1[...], has_p, has_n)
        xext = _ext(xa[...], xp[...], xn[...], has_p, has_n)
        acc = jnp.zeros((tm, RG_W), F32)
        rows = []
        for k in range(4):
            acc = acc + w_ref[k:k + 1, :] * _shifted(dext, 2 - k, tm)
            rows.append(jnp.sum(cur * _shifted(xext, k - 2, tm), axis=0, keepdims=True))
        dxa_ref[...] = acc
        dw_ref[...] += jnp.concatenate(rows + [jnp.zeros((4, RG_W), F32)], axis=0)
        db_ref[...] += jnp.sum(cur, axis=0, keepdims=True)

    return pl.pallas_call(
        body, name=name, grid=(nT,),
        in_specs=(_halo_specs(tm, T, RG_W, 0) + _halo_specs(tm, T, RG_W, 0)
                  + _halo_specs(tm, T, RG_W, 0) + [pl.BlockSpec((SUBLANES, RG_W), lambda i: (0, 0))]),
        out_specs=[pl.BlockSpec((tm, RG_W), lambda i: (i, 0)), pl.BlockSpec((SUBLANES, RG_W), lambda i: (0, 0)),
                   pl.BlockSpec((1, RG_W), lambda i: (0, 0))],
        out_shape=[jax.ShapeDtypeStruct((T, RG_W), F32), jax.ShapeDtypeStruct((SUBLANES, RG_W), F32),
                   jax.ShapeDtypeStruct((1, RG_W), F32)],
        compiler_params=_params("arbitrary"))(dxc[0], dxc[0], dxc[0], dxc[1], dxc[1], dxc[1], proj, proj, proj, cw8)


def _local_scan(a, b, ascending):
    n = a.shape[0]
    pos = jnp.bitwise_and(lax.broadcasted_iota(jnp.int32, a.shape, 0), SUBLANES - 1)
    for s in (1, 2, 4):
        sh = s if ascending else n - s
        ok = (pos >= s) if ascending else (pos < SUBLANES - s)
        a_sh, b_sh = pltpu.roll(a, sh, 0), pltpu.roll(b, sh, 0)
        b = jnp.where(ok, a * b_sh + b, b)
        a = jnp.where(ok, a * a_sh, a)
    return a, b


def _group_scan(chains, a_sc, b_sc, carry, n_groups):
    def step(g, hs):
        new = []
        for (d, out_ref, asc), h in zip(chains, hs):
            r0 = pl.multiple_of((g if asc else n_groups - 1 - g) * SUBLANES, SUBLANES)
            out_ref[pl.ds(r0, SUBLANES), :] = a_sc[d, pl.ds(r0, SUBLANES), :] * h + b_sc[d, pl.ds(r0, SUBLANES), :]
            new.append(out_ref[pl.ds(r0 + (SUBLANES - 1 if asc else 0), 1), :])
        return tuple(new)

    hs = lax.fori_loop(0, n_groups, step, tuple(carry[d, 0:1, :] for d, _, _ in chains))
    for (d, _, _), h in zip(chains, hs):
        carry[d, 0:1, :] = h


def _rg_scan_fwd(xc, wbd, bias, lam, name):
    T = xc.shape[0]
    tm = min(512, T)
    nT = T // tm

    def body(xf_ref, xb_ref, w_ref, b_ref, lam_ref, hf_ref, hb_ref, a_sc, b_sc, carry):
        @pl.when(pl.program_id(0) == 0)
        def _():
            carry[...] = jnp.zeros_like(carry)

        for d, x_ref in enumerate((xf_ref, xb_ref)):
            a, u = _rg_gates(x_ref[...], w_ref[d], b_ref[d], lam_ref[d])
            a_sc[d], b_sc[d] = _local_scan(a, u, d == 0)
        _group_scan(((0, hf_ref, True), (1, hb_ref, False)), a_sc, b_sc, carry, tm // SUBLANES)

    full = lambda a: pl.BlockSpec(a.shape, lambda i: (0,) * len(a.shape))
    res = pl.pallas_call(
        body, name=name, grid=(nT,),
        in_specs=[pl.BlockSpec((tm, RG_W), lambda i: (i, 0)), pl.BlockSpec((tm, RG_W), lambda i: (nT - 1 - i, 0)),
                  full(wbd), full(bias), full(lam)],
        out_specs=[pl.BlockSpec((tm, RG_W), lambda i: (i, 0)), pl.BlockSpec((tm, RG_W), lambda i: (nT - 1 - i, 0))],
        out_shape=[jax.ShapeDtypeStruct((T, RG_W), F32)] * 2,
        scratch_shapes=[pltpu.VMEM((2, tm, RG_W), F32), pltpu.VMEM((2, tm, RG_W), F32),
                        pltpu.VMEM((2, SUBLANES, RG_W), F32)],
        compiler_params=_params("arbitrary"))(xc, xc, wbd, bias, lam)
    return res[0], res[1]


def _rg_scan_bwd(xc, wbd, bias, lam, hs, dho, name):
    T = xc.shape[0]
    tm = min(256, T)
    nT = T // tm
    tiles = (lambda i: nT - 1 - i, lambda i: i)

    def body(xf_ref, xb_ref, w_ref, b_ref, lam_ref, hfc, hfp, hfn, hbc, hbp, hbn, dof_ref, dob_ref,
             dxf_ref, dxb_ref, dw_ref, db_ref, dlam_ref, a_sc, b_sc, y_sc, carry):
        i = pl.program_id(0)

        @pl.when(i == 0)
        def _():
            carry[...] = jnp.zeros_like(carry)
            dw_ref[...] = jnp.zeros_like(dw_ref)
            db_ref[...] = jnp.zeros_like(db_ref)
            dlam_ref[...] = jnp.zeros_like(dlam_ref)

        vjps, entering = [], []
        for d, (x_ref, do_ref) in enumerate(((xf_ref, dof_ref), (xb_ref, dob_ref))):
            (a, _), vjp = jax.vjp(_rg_gates, x_ref[...], w_ref[d].astype(F32), b_ref[d], lam_ref[d])
            vjps.append(vjp)
            entering.append(carry[d, 0:1, :])
            a_sc[d], b_sc[d] = _local_scan(a, a * do_ref[...], d == 1)
        _group_scan(((0, y_sc.at[0], False), (1, y_sc.at[1], True)), a_sc, b_sc, carry, tm // SUBLANES)

        row = lax.broadcasted_iota(jnp.int32, (tm, RG_W), 0)
        for d, (do_ref, dx_ref, hc, hp, hn, ti) in enumerate(
                ((dof_ref, dxf_ref, hfc, hfp, hfn, nT - 1 - i), (dob_ref, dxb_ref, hbc, hbp, hbn, i))):
            y = y_sc[d]
            if d == 0:
                y_next = jnp.where(row == tm - 1, entering[d], pltpu.roll(y, tm - 1, 0))
            else:
                y_next = jnp.where(row == 0, entering[d], pltpu.roll(y, 1, 0))
            dtot = do_ref[...] + y_next
            ext = _ext(hc[...], hp[...], hn[...], ti > 0, ti < nT - 1)
            hprev = _shifted(ext, -1 if d == 0 else 1, tm)
            dxc, dw, db, dlam = vjps[d]((dtot * hprev, dtot))
            dx_ref[...] = dxc
            dw_ref[d] += dw
            db_ref[d] += db
            dlam_ref[d] += dlam

    full = lambda a: pl.BlockSpec(a.shape, lambda i: (0,) * len(a.shape))
    tok = lambda d: pl.BlockSpec((tm, RG_W), lambda i: (tiles[d](i), 0))
    acc_shapes = [jax.ShapeDtypeStruct((2, RG_W, 2 * RG_W), F32), jax.ShapeDtypeStruct((2, 1, 2 * RG_W), F32),
                  jax.ShapeDtypeStruct((2, 1, RG_W), F32)]
    res = pl.pallas_call(
        body, name=name, grid=(nT,),
        in_specs=([tok(0), tok(1), full(wbd), full(bias), full(lam)]
                  + _halo_specs(tm, T, RG_W, 0, tiles[0]) + _halo_specs(tm, T, RG_W, 0, tiles[1]) + [tok(0), tok(1)]),
        out_specs=[tok(0), tok(1)] + [full(s) for s in acc_shapes],
        out_shape=[jax.ShapeDtypeStruct((T, RG_W), F32)] * 2 + acc_shapes,
        scratch_shapes=[pltpu.VMEM((2, tm, RG_W), F32), pltpu.VMEM((2, tm, RG_W), F32),
                        pltpu.VMEM((2, tm, RG_W), F32), pltpu.VMEM((2, SUBLANES, RG_W), F32)],
        compiler_params=_params("arbitrary"))(xc, xc, wbd, bias, lam, hs[0], hs[0], hs[0], hs[1], hs[1], hs[1],
                                              dho, dho)
    return (res[0], res[1]), res[2], res[3], res[4]


def _chunk_rows(n_chunks, reverse):
    up, down = (lambda c: c), (lambda c: n_chunks - 1 - c)
    return (down, up) if reverse else (up, down)


STEP_CHUNKS = 4
STEP_ROWS = STEP_CHUNKS * CHUNK


def _sub_chunks(ascending):
    order = range(STEP_CHUNKS) if ascending else range(STEP_CHUNKS - 1, -1, -1)
    return [(s, slice(s * CHUNK, (s + 1) * CHUNK)) for s in order]


def _hg_fwd(proj, l0, l1, name):
    T = proj.shape[0]
    nC = T // CHUNK
    nS = nC // STEP_CHUNKS
    H, dk, dv = 4, 128, 128
    rows = _chunk_rows(nS, False)

    def body(qf, ff, vf, qb, fb, vb, l0_ref, l1_ref, of, ob, spf, spb, st):
        @pl.when(pl.program_id(0) == 0)
        def _():
            st[...] = jnp.zeros_like(st)

        for d, (q, f, v, o, sp) in enumerate(((qf, ff, vf, of, spf), (qb, fb, vb, ob, spb))):
            tri, tri_t, mref = _tri_consts(d)
            stp = tuple(st[d, h] for h in range(H))
            for s, r in _sub_chunks(d == 0):
                for h in range(H):
                    sp[s, h] = stp[h]
                o_val, stp = _hg_chunk(q[r, :], f[r, :], v[r, :], l0_ref[...], l1_ref[...], stp, tri, tri_t, mref)
                o[r, :] = o_val
            for h in range(H):
                st[d, h] = stp[h]

    tok = lambda d, col: pl.BlockSpec((STEP_ROWS, HG_W), lambda c: (rows[d](c), col))
    par = pl.BlockSpec((1, HG_W), lambda c: (0, 0))
    state = lambda d: pl.BlockSpec((STEP_CHUNKS, H, dv, dk), lambda c: (rows[d](c), 0, 0, 0))
    res = pl.pallas_call(
        body, name=name, grid=(nS,),
        in_specs=[tok(0, 2), tok(0, 3), tok(0, 5), tok(1, 2), tok(1, 4), tok(1, 5), par, par],
        out_specs=[tok(0, 0), tok(1, 0), state(0), state(1)],
        out_shape=[jax.ShapeDtypeStruct((T, H * dv), F32)] * 2 + [jax.ShapeDtypeStruct((nC, H, dv, dk), F32)] * 2,
        scratch_shapes=[pltpu.VMEM((2, H, dv, dk), F32)],
        compiler_params=_params("arbitrary"))(proj, proj, proj, proj, proj, proj, l0, l1)
    return (res[0], res[1]), (res[2], res[3])


def _hg_bwd(proj, l0, l1, sprev, do, name):
    T = proj.shape[0]
    nC = T // CHUNK
    nS = nC // STEP_CHUNKS
    H, dk, dv = 4, 128, 128
    rows = _chunk_rows(nS, True)

    def body(qf, ff, vf, qb, fb, vb, l0_ref, l1_ref, spf, spb, dof, dob,
             dqf, dff, dvf, dqb, dfb, dvb, dl0_ref, dl1_ref, dst):
        @pl.when(pl.program_id(0) == 0)
        def _():
            dst[...] = jnp.zeros_like(dst)
            dl0_ref[...] = jnp.zeros_like(dl0_ref)
            dl1_ref[...] = jnp.zeros_like(dl1_ref)

        for d, (q, f, v, sp, do_ref, dq_ref, df_ref, dv_ref) in enumerate(
                ((qf, ff, vf, spf, dof, dqf, dff, dvf), (qb, fb, vb, spb, dob, dqb, dfb, dvb))):
            tri, tri_t, mref = _tri_consts(d)
            fn = lambda q_, f_, v_, a0, a1, stp: _hg_chunk(q_, f_, v_, a0, a1, stp, tri, tri_t, mref)
            dstp = tuple(dst[d, h] for h in range(H))
            for s, r in _sub_chunks(d == 1):
                stp = tuple(sp[s, h] for h in range(H))
                _, vjp = jax.vjp(fn, q[r, :], f[r, :], v[r, :], l0_ref[...], l1_ref[...], stp)
                dq, df, dvv, dl0, dl1, dstp = vjp((do_ref[r, :], dstp))
                dq_ref[r, :] = dq.astype(BF16)
                df_ref[r, :] = df.astype(BF16)
                dv_ref[r, :] = dvv.astype(BF16)
                dl0_ref[d] += dl0
                dl1_ref[d] += dl1
            for h in range(H):
                dst[d, h] = dstp[h]

    tok = lambda d, col: pl.BlockSpec((STEP_ROWS, HG_W), lambda c: (rows[d](c), col))
    par = pl.BlockSpec((1, HG_W), lambda c: (0, 0))
    acc = pl.BlockSpec((2, 1, HG_W), lambda c: (0, 0, 0))
    state = lambda d: pl.BlockSpec((STEP_CHUNKS, H, dv, dk), lambda c: (rows[d](c), 0, 0, 0))
    res = pl.pallas_call(
        body, name=name, grid=(nS,),
        in_specs=[tok(0, 2), tok(0, 3), tok(0, 5), tok(1, 2), tok(1, 4), tok(1, 5), par, par,
                  state(0), state(1), tok(0, 0), tok(1, 0)],
        out_specs=[tok(0, 0)] * 3 + [tok(1, 0)] * 3 + [acc, acc],
        out_shape=[jax.ShapeDtypeStruct((T, HG_W), BF16)] * 6 + [jax.ShapeDtypeStruct((2, 1, HG_W), F32)] * 2,
        scratch_shapes=[pltpu.VMEM((2, H, dv, dk), F32)],
        compiler_params=_params("arbitrary"))(proj, proj, proj, proj, proj, proj, l0, l1, sprev[0], sprev[1], do, do)
    return (res[0], res[3]), (res[1], res[4]), (res[2], res[5]), res[6], res[7]


def _gate_logits(proj, wup, bg, name):
    T = proj.shape[0]
    tm = min(512, T)

    def body(lr_ref, w_ref, b_ref, z_ref, lrb_ref):
        lr = lr_ref[...].astype(BF16)
        lrb_ref[...] = lr
        for d in range(2):
            z_ref[d] = _dg(lr, w_ref[d], 1, 0) + b_ref[d]

    return pl.pallas_call(
        body, name=name, grid=(T // tm,),
        in_specs=[pl.BlockSpec((tm, LANES), lambda i: (i, 24)), pl.BlockSpec((2, LANES, 512), lambda i: (0, 0, 0)),
                  pl.BlockSpec((2, 1, 512), lambda i: (0, 0, 0))],
        out_specs=[pl.BlockSpec((2, tm, 512), lambda i: (0, i, 0)), pl.BlockSpec((tm, LANES), lambda i: (i, 0))],
        out_shape=[jax.ShapeDtypeStruct((2, T, 512), F32), jax.ShapeDtypeStruct((T, LANES), BF16)],
        compiler_params=_params("parallel"))(proj, wup, bg)


def _gate_logits_bwd(dz, wup, name):
    T = dz[0].shape[0]
    tm = min(512, T)

    def body(dzf_ref, dzb_ref, w_ref, dlr_ref, db_ref, dzb16_ref):
        @pl.when(pl.program_id(0) == 0)
        def _():
            db_ref[...] = jnp.zeros_like(db_ref)

        acc = jnp.zeros((tm, LANES), F32)
        for d, dz_ref in enumerate((dzf_ref, dzb_ref)):
            g = dz_ref[...]
            gb = g.astype(BF16)
            dzb16_ref[d] = gb
            acc = acc + _dg(gb, w_ref[d], 1, 1)
            db_ref[d] += jnp.sum(g, axis=0, keepdims=True)
        dlr_ref[...] = acc

    tok = pl.BlockSpec((tm, 512), lambda i: (i, 0))
    return pl.pallas_call(
        body, name=name, grid=(T // tm,),
        in_specs=[tok, tok, pl.BlockSpec((2, LANES, 512), lambda i: (0, 0, 0))],
        out_specs=[pl.BlockSpec((tm, LANES), lambda i: (i, 0)), pl.BlockSpec((2, 1, 512), lambda i: (0, 0, 0)),
                   pl.BlockSpec((2, tm, 512), lambda i: (0, i, 0))],
        out_shape=[jax.ShapeDtypeStruct((T, LANES), F32), jax.ShapeDtypeStruct((2, 1, 512), F32),
                   jax.ShapeDtypeStruct((2, T, 512), BF16)],
        compiler_params=_params("arbitrary"))(dz[0], dz[1], wup)


def _gla_fwd(proj, z, name):
    T = proj.shape[0]
    nC = T // CHUNK
    nS = nC // STEP_CHUNKS
    H, dk, dv = 4, 128, 256
    rows = _chunk_rows(nS, False)

    def body(qf, kf, vf, zf, qb, kb, vb, zb, of, ob, spf, spb, st):
        @pl.when(pl.program_id(0) == 0)
        def _():
            st[...] = jnp.zeros_like(st)

        for d, (q, k, v, z_ref, o, sp) in enumerate(((qf, kf, vf, zf, of, spf), (qb, kb, vb, zb, ob, spb))):
            tri, tri_t, mref = _tri_consts(d)
            stp = tuple(st[d, h] for h in range(H))
            for s, r in _sub_chunks(d == 0):
                for h in range(H):
                    sp[s, h] = stp[h]
                o_val, stp = _gla_chunk(q[r, :], k[r, :], v[r, :], z_ref[r, :], stp, tri, tri_t, mref)
                o[r, :] = o_val
            for h in range(H):
                st[d, h] = stp[h]

    tok = lambda d, w, col: pl.BlockSpec((STEP_ROWS, w), lambda c: (rows[d](c), col))
    gate = lambda d: pl.BlockSpec((None, STEP_ROWS, 512), lambda c: (d, rows[d](c), 0))
    state = lambda d: pl.BlockSpec((STEP_CHUNKS, H, dv, dk), lambda c: (rows[d](c), 0, 0, 0))
    res = pl.pallas_call(
        body, name=name, grid=(nS,),
        in_specs=[tok(0, 512, 0), tok(0, 512, 1), tok(0, 1024, 1), gate(0),
                  tok(1, 512, 0), tok(1, 512, 1), tok(1, 1024, 1), gate(1)],
        out_specs=[tok(0, H * dv, 0), tok(1, H * dv, 0), state(0), state(1)],
        out_shape=[jax.ShapeDtypeStruct((T, H * dv), F32)] * 2 + [jax.ShapeDtypeStruct((nC, H, dv, dk), F32)] * 2,
        scratch_shapes=[pltpu.VMEM((2, H, dv, dk), F32)],
        compiler_params=_params("arbitrary"))(proj, proj, proj, z, proj, proj, proj, z)
    return (res[0], res[1]), (res[2], res[3])


def _gla_bwd(proj, z, sprev, do, name):
    T = proj.shape[0]
    nC = T // CHUNK
    nS = nC // STEP_CHUNKS
    H, dk, dv = 4, 128, 256
    rows = _chunk_rows(nS, True)

    def body(qf, kf, vf, zf, qb, kb, vb, zb, spf, spb, dof, dob,
             dqf, dkf, dvf, dzf, dqb, dkb, dvb, dzb, dst):
        @pl.when(pl.program_id(0) == 0)
        def _():
            dst[...] = jnp.zeros_like(dst)

        for d, (q, k, v, z_ref, sp, do_ref, dq_ref, dk_ref, dv_ref, dz_ref) in enumerate(
                ((qf, kf, vf, zf, spf, dof, dqf, dkf, dvf, dzf), (qb, kb, vb, zb, spb, dob, dqb, dkb, dvb, dzb))):
            tri, tri_t, mref = _tri_consts(d)
            fn = lambda q_, k_, v_, z_, stp: _gla_chunk(q_, k_, v_, z_, stp, tri, tri_t, mref)
            dstp = tuple(dst[d, h] for h in range(H))
            for s, r in _sub_chunks(d == 1):
                stp = tuple(sp[s, h] for h in range(H))
                _, vjp = jax.vjp(fn, q[r, :], k[r, :], v[r, :], z_ref[r, :], stp)
                dq, dkk, dvv, dzz, dstp = vjp((do_ref[r, :], dstp))
                dq_ref[r, :] = dq.astype(BF16)
                dk_ref[r, :] = dkk.astype(BF16)
                dv_ref[r, :] = dvv.astype(BF16)
                dz_ref[r, :] = dzz
            for h in range(H):
                dst[d, h] = dstp[h]

    tok = lambda d, w, col: pl.BlockSpec((STEP_ROWS, w), lambda c: (rows[d](c), col))
    gate = lambda d: pl.BlockSpec((None, STEP_ROWS, 512), lambda c: (d, rows[d](c), 0))
    state = lambda d: pl.BlockSpec((STEP_CHUNKS, H, dv, dk), lambda c: (rows[d](c), 0, 0, 0))
    outs = lambda d: [tok(d, 512, 0), tok(d, 512, 0), tok(d, 1024, 0), tok(d, 512, 0)]
    shapes = [jax.ShapeDtypeStruct((T, 512), BF16), jax.ShapeDtypeStruct((T, 512), BF16),
              jax.ShapeDtypeStruct((T, 1024), BF16), jax.ShapeDtypeStruct((T, 512), F32)]
    res = pl.pallas_call(
        body, name=name, grid=(nS,),
        in_specs=[tok(0, 512, 0), tok(0, 512, 1), tok(0, 1024, 1), gate(0),
                  tok(1, 512, 0), tok(1, 512, 1), tok(1, 1024, 1), gate(1),
                  state(0), state(1), tok(0, H * dv, 0), tok(1, H * dv, 0)],
        out_specs=outs(0) + outs(1), out_shape=shapes + shapes,
        scratch_shapes=[pltpu.VMEM((2, H, dv, dk), F32)],
        compiler_params=_params("arbitrary"))(proj, proj, proj, z, proj, proj, proj, z, sprev[0], sprev[1], do, do)
    return (res[0], res[4]), (res[1], res[5]), (res[2], res[6]), (res[3], res[7])


def _l0_combine_fwd(hs, proj, o, gain, name):
    T = proj.shape[0]
    tm = min(512, T)

    def body(hf, hb, ga, of, ob, g, gn, out):
        out[...] = _l0_combine(hf[...], hb[...], ga[...], of[...], ob[...], g[...], gn[...]).astype(BF16)

    tok = pl.BlockSpec((tm, 512), lambda i: (i, 0))
    return pl.pallas_call(
        body, name=name, grid=(T // tm,),
        in_specs=[tok, tok, pl.BlockSpec((tm, 512), lambda i: (i, 1)), tok, tok,
                  pl.BlockSpec((tm, 512), lambda i: (i, 6)), pl.BlockSpec((1, 512), lambda i: (0, 0))],
        out_specs=pl.BlockSpec((tm, 1024), lambda i: (i, 0)),
        out_shape=jax.ShapeDtypeStruct((T, 1024), BF16),
        compiler_params=_params("parallel"))(hs[0], hs[1], proj, o[0], o[1], proj, gain)


def _l0_combine_bwd(hs, proj, o, gain, dh_b, w_out, name):
    T = proj.shape[0]
    tm = min(512, T)

    def body(hf, hb, ga, of, ob, g, gn, dhb_ref, w_ref, dho_ref, dga_ref, do_ref, dg_ref, dgn_ref):
        @pl.when(pl.program_id(0) == 0)
        def _():
            dgn_ref[...] = jnp.zeros_like(dgn_ref)

        _, vjp = jax.vjp(_l0_combine, hf[...], hb[...], ga[...], of[...], ob[...], g[...], gn[...])
        dhf, _, dga, dof, _, dg, dgn = vjp(_dg(dhb_ref[...], w_ref[...], 1, 1))
        dho_ref[...] = dhf
        dga_ref[...] = dga
        do_ref[...] = dof
        dg_ref[...] = dg
        dgn_ref[...] += dgn

    tok = lambda: pl.BlockSpec((tm, 512), lambda i: (i, 0))
    return pl.pallas_call(
        body, name=name, grid=(T // tm,),
        in_specs=[tok(), tok(), pl.BlockSpec((tm, 512), lambda i: (i, 1)), tok(), tok(),
                  pl.BlockSpec((tm, 512), lambda i: (i, 6)), pl.BlockSpec((1, 512), lambda i: (0, 0)),
                  pl.BlockSpec((tm, D_MODEL), lambda i: (i, 0)), pl.BlockSpec(w_out.shape, lambda i: (0, 0))],
        out_specs=[tok(), tok(), tok(), tok(), pl.BlockSpec((1, 512), lambda i: (0, 0))],
        out_shape=[jax.ShapeDtypeStruct((T, 512), F32)] * 4 + [jax.ShapeDtypeStruct((1, 512), F32)],
        compiler_params=_params("arbitrary"))(hs[0], hs[1], proj, o[0], o[1], proj, gain, dh_b, w_out)


def _l0_assemble(dxa, dga, dq, df, dv, dg, name):
    T = dxa.shape[0]
    tm = min(512, T)

    def body(xa, ga, q0, q1, f0, f1, v0, v1, g, out):
        both = lambda a, b: (a[...].astype(F32) + b[...].astype(F32)).astype(BF16)
        out[...] = jnp.concatenate([xa[...].astype(BF16), ga[...].astype(BF16), both(q0, q1), f0[...], f1[...],
                                    both(v0, v1), g[...].astype(BF16)], axis=1)

    tok = lambda: pl.BlockSpec((tm, 512), lambda i: (i, 0))
    return pl.pallas_call(
        body, name=name, grid=(T // tm,),
        in_specs=[tok() for _ in range(9)],
        out_specs=pl.BlockSpec((tm, AB_IN), lambda i: (i, 0)),
        out_shape=jax.ShapeDtypeStruct((T, AB_IN), BF16),
        compiler_params=_params("parallel"))(dxa, dga, dq[0], dq[1], df[0], df[1], dv[0], dv[1], dg)


def _l1_combine_fwd(o, proj, gain, name):
    T = proj.shape[0]
    tm = min(512, T)

    def body(of, ob, r, gn, out):
        out[...] = _l1_combine(of[...], ob[...], r[...], gn[...]).astype(BF16)

    tok = pl.BlockSpec((tm, 1024), lambda i: (i, 0))
    return pl.pallas_call(
        body, name=name, grid=(T // tm,),
        in_specs=[tok, tok, pl.BlockSpec((tm, 1024), lambda i: (i, 2)), pl.BlockSpec((1, 1024), lambda i: (0, 0))],
        out_specs=pl.BlockSpec((tm, 1024), lambda i: (i, 0)),
        out_shape=jax.ShapeDtypeStruct((T, 1024), BF16),
        compiler_params=_params("parallel"))(o[0], o[1], proj, gain)


def _l1_combine_bwd(o, proj, gain, dh_b, w_out, name):
    T = proj.shape[0]
    tm = min(512, T)

    def body(of, ob, r, gn, dhb_ref, w_ref, do_ref, dr_ref, dgn_ref):
        @pl.when(pl.program_id(0) == 0)
        def _():
            dgn_ref[...] = jnp.zeros_like(dgn_ref)

        _, vjp = jax.vjp(_l1_combine, of[...], ob[...], r[...], gn[...])
        dof, _, dr, dgn = vjp(_dg(dhb_ref[...], w_ref[...], 1, 1))
        do_ref[...] = dof
        dr_ref[...] = dr
        dgn_ref[...] += dgn

    tok = lambda: pl.BlockSpec((tm, 1024), lambda i: (i, 0))
    return pl.pallas_call(
        body, name=name, grid=(T // tm,),
        in_specs=[tok(), tok(), pl.BlockSpec((tm, 1024), lambda i: (i, 2)),
                  pl.BlockSpec((1, 1024), lambda i: (0, 0)), tok(), pl.BlockSpec(w_out.shape, lambda i: (0, 0))],
        out_specs=[tok(), tok(), pl.BlockSpec((1, 1024), lambda i: (0, 0))],
        out_shape=[jax.ShapeDtypeStruct((T, 1024), F32)] * 2 + [jax.ShapeDtypeStruct((1, 1024), F32)],
        compiler_params=_params("arbitrary"))(o[0], o[1], proj, gain, dh_b, w_out)


def _l1_assemble(dq, dk, dv, dr, dlr, name):
    T = dr.shape[0]
    tm = min(512, T)

    def body(q0, q1, k0, k1, v0, v1, r, a, out):
        both = lambda x, y: (x[...].astype(F32) + y[...].astype(F32)).astype(BF16)
        out[...] = jnp.concatenate([both(q0, q1), both(k0, k1), both(v0, v1), r[...].astype(BF16),
                                    a[...].astype(BF16)], axis=1)

    tok = lambda w: pl.BlockSpec((tm, w), lambda i: (i, 0))
    return pl.pallas_call(
        body, name=name, grid=(T // tm,),
        in_specs=[tok(512), tok(512), tok(512), tok(512), tok(1024), tok(1024), tok(1024), tok(LANES)],
        out_specs=pl.BlockSpec((tm, GLA_IN_PAD), lambda i: (i, 0)),
        out_shape=jax.ShapeDtypeStruct((T, GLA_IN_PAD), BF16),
        compiler_params=_params("parallel"))(dq[0], dq[1], dk[0], dk[1], dv[0], dv[1], dr, dlr)


HBM_SPEC = pl.BlockSpec(memory_space=pltpu.HBM)


def _place():
    x, y, c = lax.axis_index("x"), lax.axis_index("y"), lax.axis_index("c")
    return x, y, c


def _allgather_vmem(x_shard, name):
    m_per, n = x_shard.shape

    def body(x_ref, out_ref, send_sems, recv_sems, local_sem):
        x, y, c = _place()
        me, sibling = (x, y, c), (x, y, 1 - c)
        chips = [(1 - x, y), (x, 1 - y), (1 - x, 1 - y)]

        def rows(px, py, pc):
            return out_ref.at[pl.ds((4 * px + 2 * py + pc) * m_per, m_per), :]

        def copy(k, block, to, src=None):
            return pltpu.make_async_remote_copy(
                src_ref=rows(*block) if src is None else src, dst_ref=rows(*block),
                send_sem=send_sems.at[k], recv_sem=recv_sems.at[k], device_id=to, device_id_type=MESH)

        mine = pltpu.make_async_copy(x_ref, rows(*me), local_sem)
        mine.start()
        first = [copy(0, me, sibling, src=x_ref)]
        first += [copy(1 + j, me, (*chip, c), src=x_ref) for j, chip in enumerate(chips)]
        for cp in first:
            cp.start()
        passed = [copy(4 + j, (*chip, c), sibling) for j, chip in enumerate(chips)]
        for j, chip in enumerate(chips):
            copy(1 + j, (*chip, c), me).wait_recv()
            passed[j].start()
        copy(0, sibling, me).wait_recv()
        for j, chip in enumerate(chips):
            copy(4 + j, (*chip, 1 - c), me).wait_recv()
        for cp in first + passed:
            cp.wait_send()
        mine.wait()

    vm = pl.BlockSpec(memory_space=pltpu.VMEM)
    return pl.pallas_call(
        body, name=name, in_specs=[vm], out_specs=vm,
        out_shape=jax.ShapeDtypeStruct((N_DEV * m_per, n), x_shard.dtype),
        scratch_shapes=[pltpu.SemaphoreType.DMA((7,)), pltpu.SemaphoreType.DMA((7,)), pltpu.SemaphoreType.DMA],
        compiler_params=pltpu.CompilerParams(has_side_effects=True, vmem_limit_bytes=VMEM_LIMIT))(x_shard)


SEM_SPEC = pl.BlockSpec(memory_space=pltpu.SEMAPHORE)
DATAFLOW_EFFECT = pltpu.SideEffectType.DATAFLOW_SIDE_EFFECTING


def _copies(plan, srcs, lands, send_sems, recv_sems):
    x, y, c = _place()
    return [pltpu.make_async_remote_copy(src_ref=s, dst_ref=d, send_sem=send_sems.at[k], recv_sem=recv_sems.at[k],
                                         device_id=dev, device_id_type=MESH)
            for k, (s, d, dev) in enumerate(plan(srcs, lands, x, y, c))]


def _copies_start(plan, n_copies, srcs, lands, name):
    ns, nl = len(srcs), len(lands)

    def body(*refs):
        send_sems, recv_sems = refs[ns + nl], refs[ns + nl + 1]
        for cp in _copies(plan, refs[:ns], refs[ns:ns + nl], send_sems, recv_sems):
            cp.start()
        refs[-1][...] = jnp.zeros_like(refs[-1])

    arrays = list(srcs) + list(lands)
    res = pl.pallas_call(
        body, name=name,
        in_specs=[HBM_SPEC] * (ns + nl),
        out_specs=tuple([SEM_SPEC, SEM_SPEC] + [HBM_SPEC] * (ns + nl) + [pl.BlockSpec(memory_space=pltpu.VMEM)]),
        out_shape=tuple([pltpu.SemaphoreType.DMA((n_copies,)), pltpu.SemaphoreType.DMA((n_copies,))]
                        + [pltpu.HBM(a.shape, a.dtype) for a in arrays]
                        + [jax.ShapeDtypeStruct((SUBLANES, LANES), F32)]),
        input_output_aliases={i: 2 + i for i in range(ns + nl)},
        compiler_params=pltpu.CompilerParams(has_side_effects=DATAFLOW_EFFECT),
    )(*[pltpu.with_memory_space_constraint(a, pltpu.HBM) for a in arrays])
    return res[0], res[1], list(res[2:2 + ns]), list(res[2 + ns:2 + ns + nl]), res[-1]


def _copies_wait(plan, started, after, name):
    send_sems, recv_sems, srcs, lands, _ = started
    ns, nl = len(srcs), len(lands)

    def body(*refs):
        for cp in _copies(plan, refs[:ns], refs[ns:ns + nl], refs[ns + nl], refs[ns + nl + 1]):
            cp.wait_send()
            cp.wait_recv()

    arrays = list(srcs) + list(lands)
    res = pl.pallas_call(
        body, name=name,
        in_specs=[HBM_SPEC] * (ns + nl) + [SEM_SPEC, SEM_SPEC, pl.BlockSpec(memory_space=pl.ANY)],
        out_specs=tuple([HBM_SPEC] * (ns + nl)),
        out_shape=tuple(pltpu.HBM(a.shape, a.dtype) for a in arrays),
        input_output_aliases={i: i for i in range(ns + nl)},
        compiler_params=pltpu.CompilerParams(has_side_effects=DATAFLOW_EFFECT),
    )(*arrays, send_sems, recv_sems, after)
    return list(res[:ns]), list(res[ns:])


def _after(token, value):
    return value + token[0:1, 0:1].astype(value.dtype)


def _chips(x, y):
    return [(1 - x, y), (x, 1 - y), (1 - x, 1 - y)]


def _plan_gather_first(srcs, lands, x, y, c):
    me = 4 * x + 2 * y + c
    out = []
    for s, l in zip(srcs, lands):
        out.append((s, l.at[me], (x, y, 1 - c)))
        out += [(s, l.at[me], (*chip, c)) for chip in _chips(x, y)]
    return out


def _plan_gather_pass(srcs, lands, x, y, c):
    out = []
    for l in lands:
        for chip in _chips(x, y):
            slot = l.at[4 * chip[0] + 2 * chip[1] + c]
            out.append((slot, slot, (x, y, 1 - c)))
    return out


def _plan_grads_sibling(srcs, lands, x, y, c):
    return [(s.at[2 * q + (1 - c)], l.at[q], (x, y, 1 - c)) for s, l in zip(srcs, lands) for q in range(4)]


def _plan_grads_chips(srcs, lands, x, y, c):
    return [(s.at[2 * chip[0] + chip[1]], l.at[k], (*chip, c))
            for s, l in zip(srcs, lands) for k, chip in enumerate(_chips(x, y))]


def _landing(n_slots, like):
    return [lax.empty((n_slots,) + a.shape[1:], a.dtype) for a in like]


def _sum_slots(g, name):
    _, R, C = g.shape
    tr = min(256, R)
    assert R % tr == 0

    def body(g_ref, o_ref):
        acc = g_ref[0]
        for j in range(1, N_DEV):
            acc = acc + g_ref[j]
        o_ref[...] = acc

    return pl.pallas_call(
        body, name=name, grid=(R // tr,),
        in_specs=[pl.BlockSpec((N_DEV, tr, C), lambda i: (0, i, 0))],
        out_specs=pl.BlockSpec((tr, C), lambda i: (i, 0)),
        out_shape=jax.ShapeDtypeStruct((R, C), F32),
        compiler_params=_params("parallel"))(g)


def _chip_partial(g, r1, place, name):
    _, R, C = g.shape
    tr = min(256, R)
    assert R % tr == 0

    def body(pl_ref, g_ref, r_ref, pb_ref, pm_ref):
        q = pl.program_id(1)
        s = g_ref[...] + r_ref[...]
        pb_ref[...] = s.astype(BF16)

        @pl.when(q == pl_ref[1])
        def _():
            pm_ref[...] = s

    grid_spec = pltpu.PrefetchScalarGridSpec(
        num_scalar_prefetch=1, grid=(R // tr, 4),
        in_specs=[pl.BlockSpec((None, tr, C), lambda r, q, p: (2 * q + p[0], r, 0)),
                  pl.BlockSpec((None, tr, C), lambda r, q, p: (q, r, 0))],
        out_specs=[pl.BlockSpec((None, tr, C), lambda r, q, p: (q, r, 0)),
                   pl.BlockSpec((tr, C), lambda r, q, p: (r, 0))])
    return pl.pallas_call(
        body, name=name, grid_spec=grid_spec,
        out_shape=[jax.ShapeDtypeStruct((4, R, C), BF16), jax.ShapeDtypeStruct((R, C), F32)],
        compiler_params=_params("parallel", "arbitrary"))(place, g, r1)


def _adamw(w, gparts, m, v, name):
    lead = w.ndim == 3
    R, C = w.shape[-2:]
    tr = min(256, R)
    assert R % tr == 0
    g0, g3 = gparts

    def body(w_ref, g0_ref, *rest):
        if g3 is not None:
            g3_ref, m_ref, v_ref, go, do, mo, vo = rest
        else:
            m_ref, v_ref, go, do, mo, vo = rest
        g = g0_ref[...]
        if g3 is not None:
            for k in range(3):
                g = g + g3_ref[k].astype(F32)
        wv = w_ref[...]
        mn = ADAM_B1 * m_ref[...] + (1.0 - ADAM_B1) * g
        vn = ADAM_B2 * v_ref[...] + (1.0 - ADAM_B2) * jnp.square(g)
        m_hat = mn / (1.0 - ADAM_B1 ** ADAM_STEP)
        v_hat = vn / (1.0 - ADAM_B2 ** ADAM_STEP)
        go[...] = g
        do[...] = -ADAM_LR * (m_hat / (jnp.sqrt(v_hat) + ADAM_EPS) + ADAM_WD * wv)
        mo[...] = mn
        vo[...] = vn

    blk = pl.BlockSpec((tr, C), lambda i: (i, 0))
    wblk = pl.BlockSpec((None, tr, C), lambda i: (0, i, 0)) if lead else blk
    in_specs = [wblk, blk] + ([pl.BlockSpec((3, tr, C), lambda i: (0, i, 0))] if g3 is not None else []) + [wblk, wblk]
    args = [w, g0] + ([g3] if g3 is not None else []) + [m, v]
    return pl.pallas_call(
        body, name=name, grid=(R // tr,), in_specs=in_specs, out_specs=[wblk] * 4,
        out_shape=[jax.ShapeDtypeStruct(w.shape, F32)] * 4,
        compiler_params=_params("parallel"))(*args)


def _adamw_layers(w, parts, m, v, name):
    _, R, C = w.shape
    tr = min(256, R)
    assert R % tr == 0

    def body(w_ref, p0, r0, p1, r1, m_ref, v_ref, go, do, mo, vo):
        gs = []
        for p, r in ((p0, r0), (p1, r1)):
            g = p[...]
            for k in range(3):
                g = g + r[k].astype(F32)
            gs.append(g)
        g = jnp.where(pl.program_id(0) == 0, gs[0], gs[1])
        mn = ADAM_B1 * m_ref[...] + (1.0 - ADAM_B1) * g
        vn = ADAM_B2 * v_ref[...] + (1.0 - ADAM_B2) * jnp.square(g)
        m_hat = mn / (1.0 - ADAM_B1 ** ADAM_STEP)
        v_hat = vn / (1.0 - ADAM_B2 ** ADAM_STEP)
        go[...] = g
        do[...] = -ADAM_LR * (m_hat / (jnp.sqrt(v_hat) + ADAM_EPS) + ADAM_WD * w_ref[...])
        mo[...] = mn
        vo[...] = vn

    lay = pl.BlockSpec((None, tr, C), lambda l, i: (l, i, 0))
    one = pl.BlockSpec((tr, C), lambda l, i: (i, 0))
    three = pl.BlockSpec((3, tr, C), lambda l, i: (0, i, 0))
    return pl.pallas_call(
        body, name=name, grid=(2, R // tr), in_specs=[lay, one, three, one, three, lay, lay],
        out_specs=[lay] * 4, out_shape=[jax.ShapeDtypeStruct((2, R, C), F32)] * 4,
        compiler_params=_params("parallel", "parallel"))(w, parts[0][0], parts[0][1], parts[1][0], parts[1][1], m, v)


SMALL_SHARDED = ("rg_conv_w", "rg_b_a", "rg_b_x", "rg_lambda", "gla_w_gate_up", "gla_b_gate", "gla_norm")
SMALL_REPLICATED = ("norm_mix", "norm_mlp", "norm_final", "rg_conv_b", "rg_w_a", "rg_w_x", "hg_lb_logits", "hg_norm")
WEIGHT_NAMES = ("norm_mix", "norm_mlp", "norm_final", "mlp_w1", "mlp_w2", "ab_w_in", "ab_w_out", "rg_conv_w",
                "rg_conv_b", "rg_w_a", "rg_b_a", "rg_w_x", "rg_b_x", "rg_lambda", "hg_lb_logits", "hg_norm",
                "gla_w_in", "gla_w_out", "gla_w_gate_up", "gla_b_gate", "gla_norm")


def _rows128(a):
    return a.reshape(-1, LANES)


def _part_rows(a):
    return -(-(a.size // LANES) // SUBLANES) * SUBLANES


def _pack_rows(arrays, pad_to=SUBLANES):
    parts = [jnp.pad(_rows128(a), ((0, _part_rows(a) - a.size // LANES), (0, 0))) for a in arrays]
    total = sum(p.shape[0] for p in parts)
    extra = (-total) % pad_to
    if extra:
        parts.append(jnp.zeros((extra, LANES), parts[0].dtype))
    return jnp.concatenate(parts, axis=0)


def _unshard_last(g, shape_local):
    nd = len(shape_local)
    t = g.reshape((N_DEV,) + tuple(shape_local))
    t = jnp.moveaxis(t, 0, nd - 1)
    return t.reshape(tuple(shape_local[:-1]) + (N_DEV * shape_local[-1],))


def _block_diag(w):
    eye = jnp.eye(8, dtype=w.dtype)
    return (w[:, :, :, None, :] * eye[None, :, None, :, None]).reshape(2, RG_W, RG_W)


def _block_diag_extract(dw):
    t = dw.reshape(2, 8, 64, 8, 64)
    return jnp.moveaxis(jnp.diagonal(t, axis1=1, axis2=3), -1, 1)


def kernel(x, norm_mix, norm_mlp, norm_final, mlp_w1, mlp_w2, ab_w_in, ab_w_out, rg_conv_w, rg_conv_b, rg_w_a, rg_b_a, rg_w_x, rg_b_x, rg_lambda, hg_lb_logits, hg_norm, gla_w_in, gla_w_out, gla_w_gate_up, gla_b_gate, gla_norm, loss_target, m_norm_mix, m_norm_mlp, m_norm_final, m_mlp_w1, m_mlp_w2, m_ab_w_in, m_ab_w_out, m_rg_conv_w, m_rg_conv_b, m_rg_w_a, m_rg_b_a, m_rg_w_x, m_rg_b_x, m_rg_lambda, m_hg_lb_logits, m_hg_norm, m_gla_w_in, m_gla_w_out, m_gla_w_gate_up, m_gla_b_gate, m_gla_norm, v_norm_mix, v_norm_mlp, v_norm_final, v_mlp_w1, v_mlp_w2, v_ab_w_in, v_ab_w_out, v_rg_conv_w, v_rg_conv_b, v_rg_w_a, v_rg_b_a, v_rg_w_x, v_rg_b_x, v_rg_lambda, v_hg_lb_logits, v_hg_norm, v_gla_w_in, v_gla_w_out, v_gla_w_gate_up, v_gla_b_gate, v_gla_norm):
    w_loc = dict(norm_mix=norm_mix, norm_mlp=norm_mlp, norm_final=norm_final, mlp_w1=mlp_w1, mlp_w2=mlp_w2,
                 ab_w_in=ab_w_in, ab_w_out=ab_w_out, rg_conv_w=rg_conv_w, rg_conv_b=rg_conv_b, rg_w_a=rg_w_a,
                 rg_b_a=rg_b_a, rg_w_x=rg_w_x, rg_b_x=rg_b_x, rg_lambda=rg_lambda, hg_lb_logits=hg_lb_logits,
                 hg_norm=hg_norm, gla_w_in=gla_w_in, gla_w_out=gla_w_out, gla_w_gate_up=gla_w_gate_up,
                 gla_b_gate=gla_b_gate, gla_norm=gla_norm)
    m_loc = dict(norm_mix=m_norm_mix, norm_mlp=m_norm_mlp, norm_final=m_norm_final, mlp_w1=m_mlp_w1,
                 mlp_w2=m_mlp_w2, ab_w_in=m_ab_w_in, ab_w_out=m_ab_w_out, rg_conv_w=m_rg_conv_w,
                 rg_conv_b=m_rg_conv_b, rg_w_a=m_rg_w_a, rg_b_a=m_rg_b_a, rg_w_x=m_rg_w_x, rg_b_x=m_rg_b_x,
                 rg_lambda=m_rg_lambda, hg_lb_logits=m_hg_lb_logits, hg_norm=m_hg_norm, gla_w_in=m_gla_w_in,
                 gla_w_out=m_gla_w_out, gla_w_gate_up=m_gla_w_gate_up, gla_b_gate=m_gla_b_gate,
                 gla_norm=m_gla_norm)
    v_loc = dict(norm_mix=v_norm_mix, norm_mlp=v_norm_mlp, norm_final=v_norm_final, mlp_w1=v_mlp_w1,
                 mlp_w2=v_mlp_w2, ab_w_in=v_ab_w_in, ab_w_out=v_ab_w_out, rg_conv_w=v_rg_conv_w,
                 rg_conv_b=v_rg_conv_b, rg_w_a=v_rg_w_a, rg_b_a=v_rg_b_a, rg_w_x=v_rg_w_x, rg_b_x=v_rg_b_x,
                 rg_lambda=v_rg_lambda, hg_lb_logits=v_hg_lb_logits, hg_norm=v_hg_norm, gla_w_in=v_gla_w_in,
                 gla_w_out=v_gla_w_out, gla_w_gate_up=v_gla_w_gate_up, gla_b_gate=v_gla_b_gate,
                 gla_norm=v_gla_norm)

    T = x.shape[1]
    h0 = x.reshape(T, D_MODEL)
    target = loss_target.reshape(T, D_MODEL)
    ax, ay, ac = lax.axis_index("x"), lax.axis_index("y"), lax.axis_index("c")
    dev = 4 * ax + 2 * ay + ac
    place = jnp.stack([ac, 2 * ax + ay]).astype(jnp.int32)

    abin_shard = ab_w_in[0].astype(BF16)
    first_started = _copies_start(_plan_gather_first, 4, [abin_shard], _landing(N_DEV, [abin_shard[None]]),
                                  "ag_first_start")
    rest_shards = [mlp_w1[0].astype(BF16), mlp_w2[0].astype(BF16), gla_w_in[0].astype(BF16),
                   gla_w_out[0].astype(BF16), mlp_w1[1].astype(BF16), mlp_w2[1].astype(BF16),
                   _after(first_started[4], ab_w_out[0].astype(BF16))]
    ag_started = _copies_start(_plan_gather_first, 4 * len(rest_shards), rest_shards,
                               _landing(N_DEV, [s[None] for s in rest_shards]), "ag_rest_start")

    small_local = [w_loc[n] for n in SMALL_SHARDED]
    small_g = _allgather_vmem(_pack_rows(small_local, 8), "ag_small")
    small_g = small_g.reshape(N_DEV, -1, LANES)
    full = {}
    off = 0
    for n, a in zip(SMALL_SHARDED, small_local):
        full[n] = _unshard_last(small_g[:, off:off + a.size // LANES].reshape(N_DEV, a.size), a.shape)
        off += _part_rows(a)
    conv_w = full["rg_conv_w"][0]
    b_a, b_x, lam = full["rg_b_a"][0], full["rg_b_x"][0], full["rg_lambda"][0]
    w_up, b_gate, g_norm = full["gla_w_gate_up"][0], full["gla_b_gate"][0], full["gla_norm"]

    cw8 = jnp.pad(conv_w, ((0, 4), (0, 0)))
    wbd = jnp.concatenate([_block_diag(rg_w_a[0]), _block_diag(rg_w_x[0])], axis=2).astype(BF16)
    rg_bias = jnp.concatenate([b_a, b_x], axis=1).reshape(2, 1, 2 * RG_W)
    lam3 = lam.reshape(2, 1, RG_W)
    l0, l1 = hg_lb_logits[0:1], hg_lb_logits[1:2]
    wup_pad = jnp.zeros((2, LANES, 512), F32).at[0, 0:16].set(w_up[0]).at[1, 16:32].set(w_up[1])
    bg3 = b_gate.reshape(2, 1, 512)
    nmix0, nmix1 = norm_mix[0:1], norm_mix[1:2]
    nmlp0, nmlp1 = norm_mlp[0:1], norm_mlp[1:2]
    nfin = norm_final.reshape(1, D_MODEL)

    prepared = (ag_started[4] + cw8[:, 0:LANES] + wup_pad[0, 0:SUBLANES, 0:LANES] + rg_bias[0, :, 0:LANES]
                + wbd[0, 0:SUBLANES, 0:LANES].astype(F32) + lam3[0, :, 0:LANES] + bg3[0, :, 0:LANES])
    (abin_shard,), abin_l = _copies_wait(_plan_gather_first, first_started, prepared, "ag_first_wait")
    first_pass = _copies_start(_plan_gather_pass, 3, [], abin_l, "ag_first_pass_start")
    _, (abin_g,) = _copies_wait(_plan_gather_pass, first_pass, first_pass[4], "ag_first_pass_wait")
    abin_g = lax.dynamic_update_index_in_dim(abin_g, abin_shard, dev, 0)
    wab_in = jnp.transpose(abin_g, (1, 0, 2)).reshape(D_MODEL, AB_IN)
    proj0, y0 = _norm_matmul(h0, _after(ag_started[4], nmix0), wab_in, "l0_in_proj")
    xc = _rg_conv_fwd(proj0, cw8, rg_conv_b, "rg_conv")
    hs = _rg_scan_fwd(xc, wbd, rg_bias, lam3, "rg_scan")
    o_hg, s_hg = _hg_fwd(proj0, l0, l1, "hg_chunks")
    both_done = hs[0][0:SUBLANES, 0:LANES] + o_hg[0][0:SUBLANES, 0:LANES]
    rest_shards, rest_lands = _copies_wait(_plan_gather_first, ag_started, both_done, "ag_rest_wait")
    pass_started = _copies_start(_plan_gather_pass, 3 * len(rest_lands), [], rest_lands, "ag_pass_start")
    mixin0 = _l0_combine_fwd(hs, proj0, o_hg, _after(pass_started[4], hg_norm), "l0_combine")
    _, rest_g = _copies_wait(_plan_gather_pass, pass_started, mixin0, "ag_pass_wait")
    rest_g = [lax.dynamic_update_index_in_dim(g, s, dev, 0) for g, s in zip(rest_g, rest_shards)]
    wab_out = rest_g[6].reshape(D_MODEL, D_MODEL)
    h1 = _matmul_res(mixin0, wab_out, h0, "l0_out_proj")
    w1g = (rest_g[0], rest_g[4])
    w2f = (rest_g[1].reshape(D_FF, D_MODEL), rest_g[5].reshape(D_FF, D_MODEL))
    wgla_in = jnp.pad(jnp.transpose(rest_g[2], (1, 0, 2)).reshape(D_MODEL, GLA_IN),
                      ((0, 0), (0, GLA_IN_PAD - GLA_IN)))
    wgla_out = rest_g[3].reshape(D_MODEL, D_MODEL)
    h2, pre0, ym0 = _mlp_fwd(h1, nmlp0, w1g[0], w2f[0], "mlp0")
    proj1, y1 = _norm_matmul(h2, nmix1, wgla_in, "l1_in_proj")
    z_gate, lr_b = _gate_logits(proj1, wup_pad, bg3, "gla_gate_logits")
    o_gla, s_gla = _gla_fwd(proj1, z_gate, "gla_chunks")
    mixin1 = _l1_combine_fwd(o_gla, proj1, g_norm, "l1_combine")
    h3 = _matmul_res(mixin1, wgla_out, h2, "l1_out_proj")
    h4, pre1, ym1 = _mlp_fwd(h3, nmlp1, w1g[1], w2f[1], "mlp1")
    loss_blk, dh4, dh4b, d_nfin = _final_loss(h4, nfin, target, "final_loss")

    dh3, dh3b, dpre1, act1, d_nmlp1 = _mlp_bwd(dh4, dh4b, h3, nmlp1, pre1, w1g[1], w2f[1], "mlp1_bwd")
    g_w1_1 = _wgrad(ym1, dpre1, 512, "mlp1_dw1", sharded_cols=True)
    g_w2_1 = _wgrad(act1, dh4b, 512, "mlp1_dw2")
    g_gla_out = _wgrad(mixin1, dh3b, 512, "l1_out_dw")
    do_gla, dr, d_gnorm = _l1_combine_bwd(o_gla, proj1, g_norm, dh3b, wgla_out, "l1_combine_bwd")
    dq1, dk1, dv1, dz_gate = _gla_bwd(proj1, z_gate, s_gla, do_gla, "gla_chunks_bwd")
    dlr1, d_bg, dz_b = _gate_logits_bwd(dz_gate, wup_pad, "gla_gate_logits_bwd")
    d_wup = [_wgrad(lr_b, dz_b[d], 512, "gla_gate_dw%d" % d) for d in range(2)]
    dproj1 = _l1_assemble(dq1, dk1, dv1, dr, dlr1, "l1_assemble")
    dh2, dh2b, d_nmix1 = _dgrad_norm(dproj1, wgla_in, h2, nmix1, dh3, "l1_in_dgrad")
    g_gla_in = _wgrad(y1, dproj1, 640, "l1_in_dw")

    def reduce_start(grads, tag):
        return _copies_start(_plan_grads_sibling, 4 * len(grads), grads, _landing(4, grads), "rs_%s_d2d_start" % tag)

    def reduce_mid(started, after, tag):
        grads, got = _copies_wait(_plan_grads_sibling, started, after, "rs_%s_d2d_wait" % tag)
        parts = [_chip_partial(g, r, place, "rs_%s_partial%d" % (tag, a)) for a, (g, r) in enumerate(zip(grads, got))]
        pb = [p[0] for p in parts]
        return _copies_start(_plan_grads_chips, 3 * len(pb), pb, _landing(3, pb), "rs_%s_ici_start" % tag), \
            [p[1] for p in parts]

    def reduce_end(started, mine, after, tag):
        _, got = _copies_wait(_plan_grads_chips, started, after, "rs_%s_ici_wait" % tag)
        return list(zip(mine, got))

    slots_l1 = [g_w1_1, g_w2_1.reshape(N_DEV, 512, D_MODEL),
                jnp.transpose(g_gla_in[:, :GLA_IN].reshape(D_MODEL, N_DEV, GLA_IN // N_DEV), (1, 0, 2)),
                g_gla_out.reshape(N_DEV, 128, D_MODEL)]
    ra_d2d = reduce_start(slots_l1, "l1")

    dh1, dh1b, dpre0, act0, d_nmlp0 = _mlp_bwd(dh2, dh2b, h1, _after(ra_d2d[4], nmlp0), pre0, w1g[0], w2f[0],
                                               "mlp0_bwd")
    g_w1_0 = _wgrad(ym0, dpre0, 512, "mlp0_dw1", sharded_cols=True)
    g_w2_0 = _wgrad(act0, dh2b, 512, "mlp0_dw2")
    ra_ici, ra_mine = reduce_mid(ra_d2d, g_w2_0, "l1")
    rb_d2d = reduce_start([g_w1_0, g_w2_0.reshape(N_DEV, 512, D_MODEL)], "mlp0")
    g_ab_out = _wgrad(mixin0, dh1b, 512, "l0_out_dw")
    dho, dga, do_hg, dg_gate, d_hgnorm = _l0_combine_bwd(
        hs, proj0, o_hg, _after(rb_d2d[4], _after(ra_ici[4], hg_norm)), dh1b, wab_out, "l0_combine_bwd")
    dxc, d_wbd, d_rgb, d_lam = _rg_scan_bwd(xc, wbd, rg_bias, lam3, hs, dho, "rg_scan_bwd")
    dxa, d_cw8, d_cb = _rg_conv_bwd(dxc, proj0, cw8, "rg_conv_bwd")
    dq0, df0, dv0, d_l0, d_l1 = _hg_bwd(proj0, l0, l1, s_hg, do_hg, "hg_chunks_bwd")
    rb_ici, rb_mine = reduce_mid(rb_d2d, d_l0, "mlp0")
    dproj0 = _l0_assemble(dxa, dga, dq0, df0, dv0, dg_gate, "l0_assemble")
    dx, _, d_nmix0 = _dgrad_norm(dproj0, wab_in, h0, _after(rb_ici[4], nmix0), dh1, "l0_in_dgrad")

    d_wa = _block_diag_extract(d_wbd[:, :, :RG_W])[None]
    d_wx = _block_diag_extract(d_wbd[:, :, RG_W:])[None]
    small_full = {
        "norm_mix": jnp.concatenate([d_nmix0, d_nmix1], axis=0), "norm_mlp": jnp.concatenate([d_nmlp0, d_nmlp1], axis=0),
        "norm_final": d_nfin.reshape(D_MODEL), "rg_conv_b": d_cb, "rg_w_a": d_wa, "rg_w_x": d_wx,
        "hg_lb_logits": jnp.concatenate([d_l0[0] + d_l0[1], d_l1[0] + d_l1[1]], axis=0), "hg_norm": d_hgnorm,
        "rg_conv_w": d_cw8[0:4][None], "rg_b_a": d_rgb[:, 0, :RG_W][None], "rg_b_x": d_rgb[:, 0, RG_W:][None],
        "rg_lambda": d_lam[:, 0, :][None],
        "gla_w_gate_up": jnp.stack([d_wup[0][0:16], d_wup[1][16:32]])[None], "gla_b_gate": d_bg[:, 0, :][None],
        "gla_norm": d_gnorm}
    small_names = SMALL_REPLICATED + SMALL_SHARDED
    packed = _pack_rows([loss_blk] + [small_full[n] for n in small_names], 256)
    ar_first = _copies_start(_plan_gather_first, 4, [packed], _landing(N_DEV, [packed[None]]), "ar_small_start")

    g_ab_in = _wgrad(y0, dproj0, 512, "l0_in_dw", behind=ar_first[4])
    rc_d2d = reduce_start([jnp.transpose(g_ab_in.reshape(D_MODEL, N_DEV, AB_IN // N_DEV), (1, 0, 2)),
                           g_ab_out.reshape(N_DEV, 128, D_MODEL)], "ab")
    (packed,), ar_lands = _copies_wait(_plan_gather_first, ar_first, rc_d2d[4], "ar_small_wait")
    ar_pass = _copies_start(_plan_gather_pass, 3, [], ar_lands, "ar_small_pass_start")
    rc_ici, rc_mine = reduce_mid(rc_d2d, ar_pass[4], "ab")
    _, (ar_gathered,) = _copies_wait(_plan_gather_pass, ar_pass, rc_ici[4], "ar_small_pass_wait")
    summed = _sum_slots(lax.dynamic_update_index_in_dim(ar_gathered, packed, dev, 0), "ar_small_sum")
    loss = summed[0, 0]

    pieces_l1 = reduce_end(ra_ici, ra_mine, rc_ici[4], "l1")
    res_gla_in = _adamw(gla_w_in, pieces_l1[2], m_gla_w_in, v_gla_w_in, "adamw_gla_in")
    res_gla_out = _adamw(gla_w_out, pieces_l1[3], m_gla_w_out, v_gla_w_out, "adamw_gla_out")
    pieces_mlp0 = reduce_end(rb_ici, rb_mine, res_gla_out[0], "mlp0")
    res_w1 = _adamw_layers(mlp_w1, (pieces_mlp0[0], pieces_l1[0]), m_mlp_w1, v_mlp_w1, "adamw_mlp_w1")
    res_w2 = _adamw_layers(mlp_w2, (pieces_mlp0[1], pieces_l1[1]), m_mlp_w2, v_mlp_w2, "adamw_mlp_w2")
    res = {"mlp_w1": tuple(res_w1), "mlp_w2": tuple(res_w2),
           "gla_w_in": tuple(res_gla_in), "gla_w_out": tuple(res_gla_out)}

    g_small = {}
    off = SUBLANES
    for n in small_names:
        a = small_full[n]
        gfull = summed[off:off + a.size // LANES].reshape(a.shape)
        off += _part_rows(a)
        if n in SMALL_SHARDED:
            loc = w_loc[n].shape[-1]
            gfull = lax.dynamic_slice_in_dim(gfull, dev * loc, loc, axis=gfull.ndim - 1)
        g_small[n] = gfull
    sw = _pack_rows([w_loc[n] for n in small_names], 256)
    sg = _pack_rows([g_small[n] for n in small_names], 256)
    sm = _pack_rows([m_loc[n] for n in small_names], 256)
    sv = _pack_rows([v_loc[n] for n in small_names], 256)
    small_res = _adamw(sw, (sg, None), sm, sv, "adamw_small")
    others_done = (res_w1[1][0, 0:SUBLANES, 0:LANES] + res_w2[1][0, 0:SUBLANES, 0:LANES]
                   + res_gla_in[1][0, 0:SUBLANES, 0:LANES] + small_res[1][0:SUBLANES, 0:LANES])
    pieces_ab = reduce_end(rc_ici, rc_mine, others_done, "ab")
    res["ab_w_in"] = tuple(_adamw(ab_w_in, pieces_ab[0], m_ab_w_in, v_ab_w_in, "adamw_ab_in"))
    res["ab_w_out"] = tuple(_adamw(ab_w_out, pieces_ab[1], m_ab_w_out, v_ab_w_out, "adamw_ab_out"))
    off = 0
    for n in small_names:
        a = w_loc[n]
        nr = a.size // LANES
        res[n] = tuple(small_res[k][off:off + nr].reshape(a.shape) for k in range(4))
        off += _part_rows(a)

    grad_x = dx.reshape(1, T, D_MODEL)
    out = [loss, grad_x]
    for k in range(4):
        out += [res[n][k] for n in WEIGHT_NAMES]
    return tuple(out)
```

```python
import jax
import jax.numpy as jnp
from jax import lax
from jax.experimental import pallas as pl
from jax.experimental.pallas import tpu as pltpu

F32, BF16 = jnp.float32, jnp.bfloat16
HI = lax.Precision.HIGHEST
MESH = pl.DeviceIdType.MESH

D_MODEL = 1024
D_FF = 4096
RG_W = 512
HG_W = 512
CHUNK = 64
EPS = 1e-6
RG_C = 8.0
AB_IN = 3584
GLA_IN = 3104
GLA_IN_PAD = 3200
N_DEV = 8
LANES = 128
SUBLANES = 8
VMEM_LIMIT = 48 * 1024 * 1024

ADAM_LR, ADAM_B1, ADAM_B2, ADAM_EPS, ADAM_WD, ADAM_STEP = 0.001, 0.9, 0.999, 1e-08, 0.01, 10


def _params(*sem):
    return pltpu.CompilerParams(dimension_semantics=sem, vmem_limit_bytes=VMEM_LIMIT)


def _dg(a, b, ca, cb):
    return lax.dot_general(a.astype(BF16), b.astype(BF16), (((ca,), (cb,)), ((), ())),
                           preferred_element_type=F32)


@jax.custom_vjp
def _mm_nn(a, b):
    return _dg(a, b, 1, 0)


_mm_nn.defvjp(lambda a, b: (_dg(a, b, 1, 0), (a, b)),
              lambda res, g: (_dg(g, res[1], 1, 1), _dg(res[0], g, 0, 0)))


@jax.custom_vjp
def _mm_nt(a, b):
    return _dg(a, b, 1, 1)


_mm_nt.defvjp(lambda a, b: (_dg(a, b, 1, 1), (a, b)),
              lambda res, g: (_dg(g, res[1], 1, 0), _dg(g, res[0], 0, 0)))


@jax.custom_vjp
def _mm_tn(a, b):
    return _dg(a, b, 0, 0)


_mm_tn.defvjp(lambda a, b: (_dg(a, b, 0, 0), (a, b)),
              lambda res, g: (_dg(res[1], g, 1, 1), _dg(res[0], g, 1, 0)))


@jax.custom_vjp
def _cum(tri, tri_t, x):
    return jnp.dot(tri, x, precision=HI, preferred_element_type=F32)


_cum.defvjp(lambda tri, tri_t, x: (jnp.dot(tri, x, precision=HI, preferred_element_type=F32), (tri, tri_t)),
            lambda res, g: (jnp.zeros_like(res[0]), jnp.zeros_like(res[1]),
                            jnp.dot(res[1], g, precision=HI, preferred_element_type=F32)))


def _sig(x):
    return 1.0 / (1.0 + jnp.exp(-x))


def _gelu(x):
    return 0.5 * x * (1.0 + jnp.tanh(0.7978845608028654 * (x + 0.044715 * (x * x * x))))


def _softplus(z):
    return jnp.maximum(z, 0.0) + jnp.log(1.0 + jnp.exp(-jnp.abs(z)))


def _rms(x):
    return lax.rsqrt(jnp.mean(x * x, axis=-1, keepdims=True) + EPS)


def _rmsnorm_bwd(x, gain, dy):
    r = _rms(x)
    xh = x * r
    dgain = jnp.sum(dy * xh, axis=0, keepdims=True)
    dxh = dy * gain
    dx = r * (dxh - xh * jnp.mean(dxh * xh, axis=-1, keepdims=True))
    return dx, dgain


def _headnorm(o, gain, n_heads, hd):
    parts = []
    for h in range(n_heads):
        oh = o[:, h * hd:(h + 1) * hd]
        parts.append(oh * _rms(oh))
    return jnp.concatenate(parts, axis=1) * gain


def _tri_consts(d):
    row = lax.broadcasted_iota(jnp.int32, (CHUNK, CHUNK), 0)
    col = lax.broadcasted_iota(jnp.int32, (CHUNK, CHUNK), 1)
    ge = (row >= col).astype(F32)
    le = (row <= col).astype(F32)
    r1 = lax.broadcasted_iota(jnp.int32, (CHUNK, 1), 0)
    if d == 0:
        return ge, le, (r1 <= CHUNK // 2).astype(F32)
    return le, ge, (r1 >= CHUNK // 2 - 1).astype(F32)


def _chunk_core(qh, k, v, logf, st_prev, tri, tri_t, mref, n_heads, dk, dv):
    cum = _cum(tri, tri_t, logf)
    ref = jnp.sum(logf * mref, axis=0, keepdims=True)
    last = jnp.sum(logf, axis=0, keepdims=True)
    q_in = qh * jnp.exp(cum - ref)
    k_in = k * jnp.exp(ref - cum)
    k_st = k * jnp.exp(last - cum)
    q_dec = qh * jnp.exp(cum)
    decay = jnp.exp(last)
    outs, sts = [], []
    for h in range(n_heads):
        sk = slice(h * dk, (h + 1) * dk)
        sv = slice(h * dv, (h + 1) * dv)
        sc = _mm_nt(q_in[:, sk], k_in[:, sk]) * tri
        o = _mm_nn(sc, v[:, sv]) + _mm_nt(q_dec[:, sk], st_prev[h])
        sts.append(st_prev[h] * decay[:, sk] + _mm_tn(v[:, sv], k_st[:, sk]))
        outs.append(o)
    return jnp.concatenate(outs, axis=1), tuple(sts)


def _hg_chunk(q, f, v, l0, l1, st_prev, tri, tri_t, mref):
    lb = _sig(l0 - l1)
    sg = _sig(f)
    qh = q * _sig(q)
    logf = jnp.log(lb + (1.0 - lb) * sg)
    k = (1.0 - lb) * (1.0 - sg)
    return _chunk_core(qh, k, v, logf, st_prev, tri, tri_t, mref, 4, 128, 128)


def _gla_chunk(q, k, v, z, st_prev, tri, tri_t, mref):
    logf = (jnp.minimum(z, 0.0) - jnp.log(1.0 + jnp.exp(-jnp.abs(z)))) * (1.0 / 16.0)
    qh = q * (128.0 ** -0.5)
    return _chunk_core(qh, k, v, logf, st_prev, tri, tri_t, mref, 4, 128, 256)


def _rg_gates(xc, wbd, bias, lam):
    z = _mm_nn(xc, wbd) + bias
    r = _sig(z[:, :RG_W])
    i = _sig(z[:, RG_W:])
    log_a = -RG_C * r * _softplus(-lam)
    a = jnp.exp(log_a)
    x2 = 2.0 * log_a
    neg_expm1 = jnp.where(x2 > -1e-2, -(x2 + 0.5 * x2 * x2 + x2 * x2 * x2 * (1.0 / 6.0)), 1.0 - jnp.exp(x2))
    u = jnp.sqrt(neg_expm1) * (i * xc)
    return a, u


def _l0_combine(hf, hb, ga, of, ob, g, gain):
    ya = (hf + hb) * _gelu(ga)
    yb = _headnorm(of + ob, gain, 4, 128) * (g * _sig(g))
    return jnp.concatenate([ya, yb], axis=1)


def _l1_combine(of, ob, r, gain):
    return _headnorm(of + ob, gain, 4, 256) * (r * _sig(r))


def _norm_matmul(h, gain, w, name):
    T, D = h.shape
    N = w.shape[1]
    tm = min(512, T)

    def body(h_ref, g_ref, w_ref, o_ref, y_ref):
        x = h_ref[...]
        y = (x * _rms(x) * g_ref[...]).astype(BF16)
        y_ref[...] = y
        o_ref[...] = jnp.dot(y, w_ref[...], preferred_element_type=F32)

    return pl.pallas_call(
        body, name=name, grid=(T // tm,),
        in_specs=[pl.BlockSpec((tm, D), lambda i: (i, 0)), pl.BlockSpec((1, D), lambda i: (0, 0)),
                  pl.BlockSpec((D, N), lambda i: (0, 0))],
        out_specs=[pl.BlockSpec((tm, N), lambda i: (i, 0)), pl.BlockSpec((tm, D), lambda i: (i, 0))],
        out_shape=[jax.ShapeDtypeStruct((T, N), F32), jax.ShapeDtypeStruct((T, D), BF16)],
        compiler_params=_params("parallel"))(h, gain, w)


def _matmul_res(a, w, res, name):
    T, K = a.shape
    N = w.shape[1]
    tm = min(512, T)

    def body(a_ref, w_ref, r_ref, o_ref):
        o_ref[...] = r_ref[...] + jnp.dot(a_ref[...], w_ref[...], preferred_element_type=F32)

    return pl.pallas_call(
        body, name=name, grid=(T // tm,),
        in_specs=[pl.BlockSpec((tm, K), lambda i: (i, 0)), pl.BlockSpec((K, N), lambda i: (0, 0)),
                  pl.BlockSpec((tm, N), lambda i: (i, 0))],
        out_specs=pl.BlockSpec((tm, N), lambda i: (i, 0)),
        out_shape=jax.ShapeDtypeStruct((T, N), F32),
        compiler_params=_params("parallel"))(a, w, res)


def _dgrad_norm(dproj, w, h, gain, dres, name):
    T, N = dproj.shape
    D = w.shape[0]
    tm = min(512, T)

    def body(dp_ref, w_ref, h_ref, g_ref, dr_ref, dh_ref, dhb_ref, dg_ref):
        @pl.when(pl.program_id(0) == 0)
        def _():
            dg_ref[...] = jnp.zeros_like(dg_ref)

        dy = _dg(dp_ref[...], w_ref[...], 1, 1)
        dx, dgain = _rmsnorm_bwd(h_ref[...], g_ref[...], dy)
        dh = dr_ref[...] + dx
        dh_ref[...] = dh
        dhb_ref[...] = dh.astype(BF16)
        dg_ref[...] += dgain

    return pl.pallas_call(
        body, name=name, grid=(T // tm,),
        in_specs=[pl.BlockSpec((tm, N), lambda i: (i, 0)), pl.BlockSpec((D, N), lambda i: (0, 0)),
                  pl.BlockSpec((tm, D), lambda i: (i, 0)), pl.BlockSpec((1, D), lambda i: (0, 0)),
                  pl.BlockSpec((tm, D), lambda i: (i, 0))],
        out_specs=[pl.BlockSpec((tm, D), lambda i: (i, 0)), pl.BlockSpec((tm, D), lambda i: (i, 0)),
                   pl.BlockSpec((1, D), lambda i: (0, 0))],
        out_shape=[jax.ShapeDtypeStruct((T, D), F32), jax.ShapeDtypeStruct((T, D), BF16),
                   jax.ShapeDtypeStruct((1, D), F32)],
        compiler_params=_params("arbitrary"))(dproj, w, h, gain, dres)


def _wgrad(a, b, tn, name, sharded_cols=False, behind=None):
    T, K = a.shape
    N = b.shape[1]
    tk = min(1024, K)

    def body(a_ref, b_ref, *rest):
        rest[-1][...] = _dg(a_ref[...], b_ref[...], 0, 0)

    if sharded_cols:
        out_spec = pl.BlockSpec((None, tk, tn), lambda k, n: (n, k, 0))
        out_shape = jax.ShapeDtypeStruct((N // tn, K, tn), F32)
    else:
        out_spec = pl.BlockSpec((tk, tn), lambda k, n: (k, n))
        out_shape = jax.ShapeDtypeStruct((K, N), F32)
    in_specs = [pl.BlockSpec((T, tk), lambda k, n: (0, k)), pl.BlockSpec((T, tn), lambda k, n: (0, n))]
    args = [a, b]
    if behind is not None:
        in_specs.append(pl.BlockSpec((SUBLANES, LANES), lambda k, n: (0, 0)))
        args.append(behind)
    return pl.pallas_call(
        body, name=name, grid=(K // tk, N // tn), in_specs=in_specs, out_specs=out_spec, out_shape=out_shape,
        compiler_params=_params("parallel", "parallel"))(*args)


def _resident(shape):
    return pl.BlockSpec(shape, lambda i: (0,) * len(shape), pipeline_mode=pl.Buffered(1))


def _mlp_fwd(h, gain, w1g, w2, name):
    T, D = h.shape
    nf, _, tf = w1g.shape
    tm = min(512, T)

    def body(h_ref, g_ref, w1_ref, w2_ref, o_ref, pre_ref, y_ref):
        x = h_ref[...]
        y = (x * _rms(x) * g_ref[...]).astype(BF16)
        y_ref[...] = y
        acc = x
        for j in range(nf):
            cols = slice(j * tf, (j + 1) * tf)
            pre = jnp.dot(y, w1_ref[j], preferred_element_type=F32)
            pre_ref[:, cols] = pre.astype(BF16)
            act = jnp.square(jnp.maximum(pre, 0.0)).astype(BF16)
            acc = acc + jnp.dot(act, w2_ref[cols, :], preferred_element_type=F32)
        o_ref[...] = acc

    return pl.pallas_call(
        body, name=name, grid=(T // tm,),
        in_specs=[pl.BlockSpec((tm, D), lambda i: (i, 0)), pl.BlockSpec((1, D), lambda i: (0, 0)),
                  _resident(w1g.shape), _resident(w2.shape)],
        out_specs=[pl.BlockSpec((tm, D), lambda i: (i, 0)), pl.BlockSpec((tm, nf * tf), lambda i: (i, 0)),
                   pl.BlockSpec((tm, D), lambda i: (i, 0))],
        out_shape=[jax.ShapeDtypeStruct((T, D), F32), jax.ShapeDtypeStruct((T, nf * tf), BF16),
                   jax.ShapeDtypeStruct((T, D), BF16)],
        compiler_params=_params("parallel"))(h, gain, w1g, w2)


def _mlp_bwd(dout, dout_b, h, gain, pre, w1g, w2, name):
    T, D = h.shape
    nf, _, tf = w1g.shape
    tm = min(256, T)

    def body(do_ref, dob_ref, h_ref, g_ref, pre_ref, w1_ref, w2_ref, dh_ref, dhb_ref, dpre_ref, act_ref, dg_ref):
        @pl.when(pl.program_id(0) == 0)
        def _():
            dg_ref[...] = jnp.zeros_like(dg_ref)

        dob = dob_ref[...]
        dy = None
        for j in range(nf):
            cols = slice(j * tf, (j + 1) * tf)
            rp = jnp.maximum(pre_ref[:, cols].astype(F32), 0.0)
            dpre = (_dg(dob, w2_ref[cols, :], 1, 1) * (2.0 * rp)).astype(BF16)
            dpre_ref[:, cols] = dpre
            act_ref[:, cols] = (rp * rp).astype(BF16)
            part = _dg(dpre, w1_ref[j], 1, 1)
            dy = part if dy is None else dy + part
        dx, dgain = _rmsnorm_bwd(h_ref[...], g_ref[...], dy)
        dh = do_ref[...] + dx
        dh_ref[...] = dh
        dhb_ref[...] = dh.astype(BF16)
        dg_ref[...] += dgain

    tok = lambda w: pl.BlockSpec((tm, w), lambda i: (i, 0))
    return pl.pallas_call(
        body, name=name, grid=(T // tm,),
        in_specs=[tok(D), tok(D), tok(D), pl.BlockSpec((1, D), lambda i: (0, 0)), tok(nf * tf),
                  _resident(w1g.shape), _resident(w2.shape)],
        out_specs=[tok(D), tok(D), tok(nf * tf), tok(nf * tf), pl.BlockSpec((1, D), lambda i: (0, 0))],
        out_shape=[jax.ShapeDtypeStruct((T, D), F32), jax.ShapeDtypeStruct((T, D), BF16),
                   jax.ShapeDtypeStruct((T, nf * tf), BF16),
                   jax.ShapeDtypeStruct((T, nf * tf), BF16), jax.ShapeDtypeStruct((1, D), F32)],
        compiler_params=_params("arbitrary"))(dout, dout_b, h, gain, pre, w1g, w2)


def _final_loss(h, gain, target, name):
    T, D = h.shape
    tm = min(512, T)

    def body(h_ref, g_ref, t_ref, l_ref, dh_ref, dhb_ref, dg_ref):
        @pl.when(pl.program_id(0) == 0)
        def _():
            l_ref[...] = jnp.zeros_like(l_ref)
            dg_ref[...] = jnp.zeros_like(dg_ref)

        x = h_ref[...]
        err = x * _rms(x) * g_ref[...] - t_ref[...]
        l_ref[...] += 0.5 * jnp.sum(jnp.mean(err * err, axis=-1, keepdims=True), axis=0, keepdims=True)
        dx, dgain = _rmsnorm_bwd(x, g_ref[...], err * (1.0 / D))
        dh_ref[...] = dx
        dhb_ref[...] = dx.astype(BF16)
        dg_ref[...] += dgain

    return pl.pallas_call(
        body, name=name, grid=(T // tm,),
        in_specs=[pl.BlockSpec((tm, D), lambda i: (i, 0)), pl.BlockSpec((1, D), lambda i: (0, 0)),
                  pl.BlockSpec((tm, D), lambda i: (i, 0))],
        out_specs=[pl.BlockSpec((SUBLANES, LANES), lambda i: (0, 0)), pl.BlockSpec((tm, D), lambda i: (i, 0)),
                   pl.BlockSpec((tm, D), lambda i: (i, 0)), pl.BlockSpec((1, D), lambda i: (0, 0))],
        out_shape=[jax.ShapeDtypeStruct((SUBLANES, LANES), F32), jax.ShapeDtypeStruct((T, D), F32),
                   jax.ShapeDtypeStruct((T, D), BF16), jax.ShapeDtypeStruct((1, D), F32)],
        compiler_params=_params("arbitrary"))(h, gain, target)


def _halo_specs(tm, T, width, col, tile=lambda i: i):
    r8 = tm // SUBLANES
    nb8 = T // SUBLANES
    return [pl.BlockSpec((tm, width), lambda i: (tile(i), col)),
            pl.BlockSpec((SUBLANES, width), lambda i: (jnp.maximum(tile(i) * r8 - 1, 0), col)),
            pl.BlockSpec((SUBLANES, width), lambda i: (jnp.minimum((tile(i) + 1) * r8, nb8 - 1), col))]


def _ext(cur, prev, nxt, has_prev, has_next):
    return jnp.concatenate([jnp.where(has_prev, prev, 0.0), cur, jnp.where(has_next, nxt, 0.0)], axis=0)


def _shifted(ext, offset, tm):
    n = ext.shape[0]
    sh = (-offset) % n
    r = ext if sh == 0 else pltpu.roll(ext, sh, 0)
    return r[SUBLANES:SUBLANES + tm]


def _rg_conv_fwd(proj, cw8, cb, name):
    T = proj.shape[0]
    tm = min(512, T)
    nT = T // tm

    def body(cur_ref, prev_ref, next_ref, w_ref, b_ref, o_ref):
        i = pl.program_id(0)
        ext = _ext(cur_ref[...], prev_ref[...], next_ref[...], i > 0, i < nT - 1)
        acc = jnp.broadcast_to(b_ref[...], (tm, RG_W))
        for k in range(4):
            acc = acc + w_ref[k:k + 1, :] * _shifted(ext, k - 2, tm)
        o_ref[...] = acc

    return pl.pallas_call(
        body, name=name, grid=(nT,),
        in_specs=_halo_specs(tm, T, RG_W, 0) + [pl.BlockSpec((SUBLANES, RG_W), lambda i: (0, 0)),
                                                pl.BlockSpec((1, RG_W), lambda i: (0, 0))],
        out_specs=pl.BlockSpec((tm, RG_W), lambda i: (i, 0)),
        out_shape=jax.ShapeDtypeStruct((T, RG_W), F32),
        compiler_params=_params("parallel"))(proj, proj, proj, cw8, cb)


def _rg_conv_bwd(dxc, proj, cw8, name):
    T = proj.shape[0]
    tm = min(512, T)
    nT = T // tm

    def body(a0, p0, n0, a1, p1, n1, xa, xp, xn, w_ref, dxa_ref, dw_ref, db_ref):
        i = pl.program_id(0)

        @pl.when(i == 0)
        def _():
            dw_ref[...] = jnp.zeros_like(dw_ref)
            db_ref[...] = jnp.zeros_like(db_ref)

        has_p, has_n = i > 0, i < nT - 1
        cur = a0[...] + a1[...]
        dext = _ext(cur, p0[...] + p1[...], n0[...] + n1[...], has_p, has_n)
        xext = _ext(xa[...], xp[...], xn[...], has_p, has_n)
        acc = jnp.zeros((tm, RG_W), F32)
        rows = []
        for k in range(4):
            acc = acc + w_ref[k:k + 1, :] * _shifted(dext, 2 - k, tm)
            rows.append(jnp.sum(cur * _shifted(xext, k - 2, tm), axis=0, keepdims=True))
        dxa_ref[...] = acc
        dw_ref[...] += jnp.concatenate(rows + [jnp.zeros((4, RG_W), F32)], axis=0)
        db_ref[...] += jnp.sum(cur, axis=0, keepdims=True)

    return pl.pallas_call(
        body, name=name, grid=(nT,),
        in_specs=(_halo_specs(tm, T, RG_W, 0) + _halo_specs(tm, T, RG_W, 0)
                  + _halo_specs(tm, T, RG_W, 0) + [pl.BlockSpec((SUBLANES, RG_W), lambda i: (0, 0))]),
        out_specs=[pl.BlockSpec((tm, RG_W), lambda i: (i, 0)), pl.BlockSpec((SUBLANES, RG_W), lambda i: (0, 0)),
                   pl.BlockSpec((1, RG_W), lambda i: (0, 0))],
        out_shape=[jax.ShapeDtypeStruct((T, RG_W), F32), jax.ShapeDtypeStruct((SUBLANES, RG_W), F32),
                   jax.ShapeDtypeStruct((1, RG_W), F32)],
        compiler_params=_params("arbitrary"))(dxc[0], dxc[0], dxc[0], dxc[1], dxc[1], dxc[1], proj, proj, proj, cw8)


def _local_scan(a, b, ascending):
    n = a.shape[0]
    pos = jnp.bitwise_and(lax.broadcasted_iota(jnp.int32, a.shape, 0), SUBLANES - 1)
    for s in (1, 2, 4):
        sh = s if ascending else n - s
        ok = (pos >= s) if ascending else (pos < SUBLANES - s)
        a_sh, b_sh = pltpu.roll(a, sh, 0), pltpu.roll(b, sh, 0)
        b = jnp.where(ok, a * b_sh + b, b)
        a = jnp.where(ok, a * a_sh, a)
    return a, b


def _group_scan(chains, a_sc, b_sc, carry, n_groups):
    def step(g, hs):
        new = []
        for (d, out_ref, asc), h in zip(chains, hs):
            r0 = pl.multiple_of((g if asc else n_groups - 1 - g) * SUBLANES, SUBLANES)
            out_ref[pl.ds(r0, SUBLANES), :] = a_sc[d, pl.ds(r0, SUBLANES), :] * h + b_sc[d, pl.ds(r0, SUBLANES), :]
            new.append(out_ref[pl.ds(r0 + (SUBLANES - 1 if asc else 0), 1), :])
        return tuple(new)

    hs = lax.fori_loop(0, n_groups, step, tuple(carry[d, 0:1, :] for d, _, _ in chains))
    for (d, _, _), h in zip(chains, hs):
        carry[d, 0:1, :] = h


def _rg_scan_fwd(xc, wbd, bias, lam, name):
    T = xc.shape[0]
    tm = min(512, T)
    nT = T // tm

    def body(xf_ref, xb_ref, w_ref, b_ref, lam_ref, hf_ref, hb_ref, a_sc, b_sc, carry):
        @pl.when(pl.program_id(0) == 0)
        def _():
            carry[...] = jnp.zeros_like(carry)

        for d, x_ref in enumerate((xf_ref, xb_ref)):
            a, u = _rg_gates(x_ref[...], w_ref[d], b_ref[d], lam_ref[d])
            a_sc[d], b_sc[d] = _local_scan(a, u, d == 0)
        _group_scan(((0, hf_ref, True), (1, hb_ref, False)), a_sc, b_sc, carry, tm // SUBLANES)

    full = lambda a: pl.BlockSpec(a.shape, lambda i: (0,) * len(a.shape))
    res = pl.pallas_call(
        body, name=name, grid=(nT,),
        in_specs=[pl.BlockSpec((tm, RG_W), lambda i: (i, 0)), pl.BlockSpec((tm, RG_W), lambda i: (nT - 1 - i, 0)),
                  full(wbd), full(bias), full(lam)],
        out_specs=[pl.BlockSpec((tm, RG_W), lambda i: (i, 0)), pl.BlockSpec((tm, RG_W), lambda i: (nT - 1 - i, 0))],
        out_shape=[jax.ShapeDtypeStruct((T, RG_W), F32)] * 2,
        scratch_shapes=[pltpu.VMEM((2, tm, RG_W), F32), pltpu.VMEM((2, tm, RG_W), F32),
                        pltpu.VMEM((2, SUBLANES, RG_W), F32)],
        compiler_params=_params("arbitrary"))(xc, xc, wbd, bias, lam)
    return res[0], res[1]


def _rg_scan_bwd(xc, wbd, bias, lam, hs, dho, name):
    T = xc.shape[0]
    tm = min(256, T)
    nT = T // tm
    tiles = (lambda i: nT - 1 - i, lambda i: i)

    def body(xf_ref, xb_ref, w_ref, b_ref, lam_ref, hfc, hfp, hfn, hbc, hbp, hbn, dof_ref, dob_ref,
             dxf_ref, dxb_ref, dw_ref, db_ref, dlam_ref, a_sc, b_sc, y_sc, carry):
        i = pl.program_id(0)

        @pl.when(i == 0)
        def _():
            carry[...] = jnp.zeros_like(carry)
            dw_ref[...] = jnp.zeros_like(dw_ref)
            db_ref[...] = jnp.zeros_like(db_ref)
            dlam_ref[...] = jnp.zeros_like(dlam_ref)

        vjps, entering = [], []
        for d, (x_ref, do_ref) in enumerate(((xf_ref, dof_ref), (xb_ref, dob_ref))):
            (a, _), vjp = jax.vjp(_rg_gates, x_ref[...], w_ref[d].astype(F32), b_ref[d], lam_ref[d])
            vjps.append(vjp)
            entering.append(carry[d, 0:1, :])
            a_sc[d], b_sc[d] = _local_scan(a, a * do_ref[...], d == 1)
        _group_scan(((0, y_sc.at[0], False), (1, y_sc.at[1], True)), a_sc, b_sc, carry, tm // SUBLANES)

        row = lax.broadcasted_iota(jnp.int32, (tm, RG_W), 0)
        for d, (do_ref, dx_ref, hc, hp, hn, ti) in enumerate(
                ((dof_ref, dxf_ref, hfc, hfp, hfn, nT - 1 - i), (dob_ref, dxb_ref, hbc, hbp, hbn, i))):
            y = y_sc[d]
            if d == 0:
                y_next = jnp.where(row == tm - 1, entering[d], pltpu.roll(y, tm - 1, 0))
            else:
                y_next = jnp.where(row == 0, entering[d], pltpu.roll(y, 1, 0))
            dtot = do_ref[...] + y_next
            ext = _ext(hc[...], hp[...], hn[...], ti > 0, ti < nT - 1)
            hprev = _shifted(ext, -1 if d == 0 else 1, tm)
            dxc, dw, db, dlam = vjps[d]((dtot * hprev, dtot))
            dx_ref[...] = dxc
            dw_ref[d] += dw
            db_ref[d] += db
            dlam_ref[d] += dlam

    full = lambda a: pl.BlockSpec(a.shape, lambda i: (0,) * len(a.shape))
    tok = lambda d: pl.BlockSpec((tm, RG_W), lambda i: (tiles[d](i), 0))
    acc_shapes = [jax.ShapeDtypeStruct((2, RG_W, 2 * RG_W), F32), jax.ShapeDtypeStruct((2, 1, 2 * RG_W), F32),
                  jax.ShapeDtypeStruct((2, 1, RG_W), F32)]
    res = pl.pallas_call(
        body, name=name, grid=(nT,),
        in_specs=([tok(0), tok(1), full(wbd), full(bias), full(lam)]
                  + _halo_specs(tm, T, RG_W, 0, tiles[0]) + _halo_specs(tm, T, RG_W, 0, tiles[1]) + [tok(0), tok(1)]),
        out_specs=[tok(0), tok(1)] + [full(s) for s in acc_shapes],
        out_shape=[jax.ShapeDtypeStruct((T, RG_W), F32)] * 2 + acc_shapes,
        scratch_shapes=[pltpu.VMEM((2, tm, RG_W), F32), pltpu.VMEM((2, tm, RG_W), F32),
                        pltpu.VMEM((2, tm, RG_W), F32), pltpu.VMEM((2, SUBLANES, RG_W), F32)],
        compiler_params=_params("arbitrary"))(xc, xc, wbd, bias, lam, hs[0], hs[0], hs[0], hs[1], hs[1], hs[1],
                                              dho, dho)
    return (res[0], res[1]), res[2], res[3], res[4]


def _chunk_rows(n_chunks, reverse):
    up, down = (lambda c: c), (lambda c: n_chunks - 1 - c)
    return (down, up) if reverse else (up, down)


STEP_CHUNKS = 4
STEP_ROWS = STEP_CHUNKS * CHUNK


def _sub_chunks(ascending):
    order = range(STEP_CHUNKS) if ascending else range(STEP_CHUNKS - 1, -1, -1)
    return [(s, slice(s * CHUNK, (s + 1) * CHUNK)) for s in order]


def _hg_fwd(proj, l0, l1, name):
    T = proj.shape[0]
    nC = T // CHUNK
    nS = nC // STEP_CHUNKS
    H, dk, dv = 4, 128, 128
    rows = _chunk_rows(nS, False)

    def body(qf, ff, vf, qb, fb, vb, l0_ref, l1_ref, of, ob, spf, spb, st):
        @pl.when(pl.program_id(0) == 0)
        def _():
            st[...] = jnp.zeros_like(st)

        for d, (q, f, v, o, sp) in enumerate(((qf, ff, vf, of, spf), (qb, fb, vb, ob, spb))):
            tri, tri_t, mref = _tri_consts(d)
            stp = tuple(st[d, h] for h in range(H))
            for s, r in _sub_chunks(d == 0):
                for h in range(H):
                    sp[s, h] = stp[h]
                o_val, stp = _hg_chunk(q[r, :], f[r, :], v[r, :], l0_ref[...], l1_ref[...], stp, tri, tri_t, mref)
                o[r, :] = o_val
            for h in range(H):
                st[d, h] = stp[h]

    tok = lambda d, col: pl.BlockSpec((STEP_ROWS, HG_W), lambda c: (rows[d](c), col))
    par = pl.BlockSpec((1, HG_W), lambda c: (0, 0))
    state = lambda d: pl.BlockSpec((STEP_CHUNKS, H, dv, dk), lambda c: (rows[d](c), 0, 0, 0))
    res = pl.pallas_call(
        body, name=name, grid=(nS,),
        in_specs=[tok(0, 2), tok(0, 3), tok(0, 5), tok(1, 2), tok(1, 4), tok(1, 5), par, par],
        out_specs=[tok(0, 0), tok(1, 0), state(0), state(1)],
        out_shape=[jax.ShapeDtypeStruct((T, H * dv), F32)] * 2 + [jax.ShapeDtypeStruct((nC, H, dv, dk), F32)] * 2,
        scratch_shapes=[pltpu.VMEM((2, H, dv, dk), F32)],
        compiler_params=_params("arbitrary"))(proj, proj, proj, proj, proj, proj, l0, l1)
    return (res[0], res[1]), (res[2], res[3])


def _hg_bwd(proj, l0, l1, sprev, do, name):
    T = proj.shape[0]
    nC = T // CHUNK
    nS = nC // STEP_CHUNKS
    H, dk, dv = 4, 128, 128
    rows = _chunk_rows(nS, True)

    def body(qf, ff, vf, qb, fb, vb, l0_ref, l1_ref, spf, spb, dof, dob,
             dqf, dff, dvf, dqb, dfb, dvb, dl0_ref, dl1_ref, dst):
        @pl.when(pl.program_id(0) == 0)
        def _():
            dst[...] = jnp.zeros_like(dst)
            dl0_ref[...] = jnp.zeros_like(dl0_ref)
            dl1_ref[...] = jnp.zeros_like(dl1_ref)

        for d, (q, f, v, sp, do_ref, dq_ref, df_ref, dv_ref) in enumerate(
                ((qf, ff, vf, spf, dof, dqf, dff, dvf), (qb, fb, vb, spb, dob, dqb, dfb, dvb))):
            tri, tri_t, mref = _tri_consts(d)
            fn = lambda q_, f_, v_, a0, a1, stp: _hg_chunk(q_, f_, v_, a0, a1, stp, tri, tri_t, mref)
            dstp = tuple(dst[d, h] for h in range(H))
            for s, r in _sub_chunks(d == 1):
                stp = tuple(sp[s, h] for h in range(H))
                _, vjp = jax.vjp(fn, q[r, :], f[r, :], v[r, :], l0_ref[...], l1_ref[...], stp)
                dq, df, dvv, dl0, dl1, dstp = vjp((do_ref[r, :], dstp))
                dq_ref[r, :] = dq.astype(BF16)
                df_ref[r, :] = df.astype(BF16)
                dv_ref[r, :] = dvv.astype(BF16)
                dl0_ref[d] += dl0
                dl1_ref[d] += dl1
            for h in range(H):
                dst[d, h] = dstp[h]

    tok = lambda d, col: pl.BlockSpec((STEP_ROWS, HG_W), lambda c: (rows[d](c), col))
    par = pl.BlockSpec((1, HG_W), lambda c: (0, 0))
    acc = pl.BlockSpec((2, 1, HG_W), lambda c: (0, 0, 0))
    state = lambda d: pl.BlockSpec((STEP_CHUNKS, H, dv, dk), lambda c: (rows[d](c), 0, 0, 0))
    res = pl.pallas_call(
        body, name=name, grid=(nS,),
        in_specs=[tok(0, 2), tok(0, 3), tok(0, 5), tok(1, 2), tok(1, 4), tok(1, 5), par, par,
                  state(0), state(1), tok(0, 0), tok(1, 0)],
        out_specs=[tok(0, 0)] * 3 + [tok(1, 0)] * 3 + [acc, acc],
        out_shape=[jax.ShapeDtypeStruct((T, HG_W), BF16)] * 6 + [jax.ShapeDtypeStruct((2, 1, HG_W), F32)] * 2,
        scratch_shapes=[pltpu.VMEM((2, H, dv, dk), F32)],
        compiler_params=_params("arbitrary"))(proj, proj, proj, proj, proj, proj, l0, l1, sprev[0], sprev[1], do, do)
    return (res[0], res[3]), (res[1], res[4]), (res[2], res[5]), res[6], res[7]


def _gate_logits(proj, wup, bg, name):
    T = proj.shape[0]
    tm = min(512, T)

    def body(lr_ref, w_ref, b_ref, z_ref, lrb_ref):
        lr = lr_ref[...].astype(BF16)
        lrb_ref[...] = lr
        for d in range(2):
            z_ref[d] = _dg(lr, w_ref[d], 1, 0) + b_ref[d]

    return pl.pallas_call(
        body, name=name, grid=(T // tm,),
        in_specs=[pl.BlockSpec((tm, LANES), lambda i: (i, 24)), pl.BlockSpec((2, LANES, 512), lambda i: (0, 0, 0)),
                  pl.BlockSpec((2, 1, 512), lambda i: (0, 0, 0))],
        out_specs=[pl.BlockSpec((2, tm, 512), lambda i: (0, i, 0)), pl.BlockSpec((tm, LANES), lambda i: (i, 0))],
        out_shape=[jax.ShapeDtypeStruct((2, T, 512), F32), jax.ShapeDtypeStruct((T, LANES), BF16)],
        compiler_params=_params("parallel"))(proj, wup, bg)


def _gate_logits_bwd(dz, wup, name):
    T = dz[0].shape[0]
    tm = min(512, T)

    def body(dzf_ref, dzb_ref, w_ref, dlr_ref, db_ref, dzb16_ref):
        @pl.when(pl.program_id(0) == 0)
        def _():
            db_ref[...] = jnp.zeros_like(db_ref)

        acc = jnp.zeros((tm, LANES), F32)
        for d, dz_ref in enumerate((dzf_ref, dzb_ref)):
            g = dz_ref[...]
            gb = g.astype(BF16)
            dzb16_ref[d] = gb
            acc = acc + _dg(gb, w_ref[d], 1, 1)
            db_ref[d] += jnp.sum(g, axis=0, keepdims=True)
        dlr_ref[...] = acc

    tok = pl.BlockSpec((tm, 512), lambda i: (i, 0))
    return pl.pallas_call(
        body, name=name, grid=(T // tm,),
        in_specs=[tok, tok, pl.BlockSpec((2, LANES, 512), lambda i: (0, 0, 0))],
        out_specs=[pl.BlockSpec((tm, LANES), lambda i: (i, 0)), pl.BlockSpec((2, 1, 512), lambda i: (0, 0, 0)),
                   pl.BlockSpec((2, tm, 512), lambda i: (0, i, 0))],
        out_shape=[jax.ShapeDtypeStruct((T, LANES), F32), jax.ShapeDtypeStruct((2, 1, 512), F32),
                   jax.ShapeDtypeStruct((2, T, 512), BF16)],
        compiler_params=_params("arbitrary"))(dz[0], dz[1], wup)


def _gla_fwd(proj, z, name):
    T = proj.shape[0]
    nC = T // CHUNK
    nS = nC // STEP_CHUNKS
    H, dk, dv = 4, 128, 256
    rows = _chunk_rows(nS, False)

    def body(qf, kf, vf, zf, qb, kb, vb, zb, of, ob, spf, spb, st):
        @pl.when(pl.program_id(0) == 0)
        def _():
            st[...] = jnp.zeros_like(st)

        for d, (q, k, v, z_ref, o, sp) in enumerate(((qf, kf, vf, zf, of, spf), (qb, kb, vb, zb, ob, spb))):
            tri, tri_t, mref = _tri_consts(d)
            stp = tuple(st[d, h] for h in range(H))
            for s, r in _sub_chunks(d == 0):
                for h in range(H):
                    sp[s, h] = stp[h]
                o_val, stp = _gla_chunk(q[r, :], k[r, :], v[r, :], z_ref[r, :], stp, tri, tri_t, mref)
                o[r, :] = o_val
            for h in range(H):
                st[d, h] = stp[h]

    tok = lambda d, w, col: pl.BlockSpec((STEP_ROWS, w), lambda c: (rows[d](c), col))
    gate = lambda d: pl.BlockSpec((None, STEP_ROWS, 512), lambda c: (d, rows[d](c), 0))
    state = lambda d: pl.BlockSpec((STEP_CHUNKS, H, dv, dk), lambda c: (rows[d](c), 0, 0, 0))
    res = pl.pallas_call(
        body, name=name, grid=(nS,),
        in_specs=[tok(0, 512, 0), tok(0, 512, 1), tok(0, 1024, 1), gate(0),
                  tok(1, 512, 0), tok(1, 512, 1), tok(1, 1024, 1), gate(1)],
        out_specs=[tok(0, H * dv, 0), tok(1, H * dv, 0), state(0), state(1)],
        out_shape=[jax.ShapeDtypeStruct((T, H * dv), F32)] * 2 + [jax.ShapeDtypeStruct((nC, H, dv, dk), F32)] * 2,
        scratch_shapes=[pltpu.VMEM((2, H, dv, dk), F32)],
        compiler_params=_params("arbitrary"))(proj, proj, proj, z, proj, proj, proj, z)
    return (res[0], res[1]), (res[2], res[3])


def _gla_bwd(proj, z, sprev, do, name):
    T = proj.shape[0]
    nC = T // CHUNK
    nS = nC // STEP_CHUNKS
    H, dk, dv = 4, 128, 256
    rows = _chunk_rows(nS, True)

    def body(qf, kf, vf, zf, qb, kb, vb, zb, spf, spb, dof, dob,
             dqf, dkf, dvf, dzf, dqb, dkb, dvb, dzb, dst):
        @pl.when(pl.program_id(0) == 0)
        def _():
            dst[...] = jnp.zeros_like(dst)

        for d, (q, k, v, z_ref, sp, do_ref, dq_ref, dk_ref, dv_ref, dz_ref) in enumerate(
                ((qf, kf, vf, zf, spf, dof, dqf, dkf, dvf, dzf), (qb, kb, vb, zb, spb, dob, dqb, dkb, dvb, dzb))):
            tri, tri_t, mref = _tri_consts(d)
            fn = lambda q_, k_, v_, z_, stp: _gla_chunk(q_, k_, v_, z_, stp, tri, tri_t, mref)
            dstp = tuple(dst[d, h] for h in range(H))
            for s, r in _sub_chunks(d == 1):
                stp = tuple(sp[s, h] for h in range(H))
                _, vjp = jax.vjp(fn, q[r, :], k[r, :], v[r, :], z_ref[r, :], stp)
                dq, dkk, dvv, dzz, dstp = vjp((do_ref[r, :], dstp))
                dq_ref[r, :] = dq.astype(BF16)
                dk_ref[r, :] = dkk.astype(BF16)
                dv_ref[r, :] = dvv.astype(BF16)
                dz_ref[r, :] = dzz
            for h in range(H):
                dst[d, h] = dstp[h]

    tok = lambda d, w, col: pl.BlockSpec((STEP_ROWS, w), lambda c: (rows[d](c), col))
    gate = lambda d: pl.BlockSpec((None, STEP_ROWS, 512), lambda c: (d, rows[d](c), 0))
    state = lambda d: pl.BlockSpec((STEP_CHUNKS, H, dv, dk), lambda c: (rows[d](c), 0, 0, 0))
    outs = lambda d: [tok(d, 512, 0), tok(d, 512, 0), tok(d, 1024, 0), tok(d, 512, 0)]
    shapes = [jax.ShapeDtypeStruct((T, 512), BF16), jax.ShapeDtypeStruct((T, 512), BF16),
              jax.ShapeDtypeStruct((T, 1024), BF16), jax.ShapeDtypeStruct((T, 512), F32)]
    res = pl.pallas_call(
        body, name=name, grid=(nS,),
        in_specs=[tok(0, 512, 0), tok(0, 512, 1), tok(0, 1024, 1), gate(0),
                  tok(1, 512, 0), tok(1, 512, 1), tok(1, 1024, 1), gate(1),
                  state(0), state(1), tok(0, H * dv, 0), tok(1, H * dv, 0)],
        out_specs=outs(0) + outs(1), out_shape=shapes + shapes,
        scratch_shapes=[pltpu.VMEM((2, H, dv, dk), F32)],
        compiler_params=_params("arbitrary"))(proj, proj, proj, z, proj, proj, proj, z, sprev[0], sprev[1], do, do)
    return (res[0], res[4]), (res[1], res[5]), (res[2], res[6]), (res[3], res[7])


def _l0_combine_fwd(hs, proj, o, gain, name):
    T = proj.shape[0]
    tm = min(512, T)

    def body(hf, hb, ga, of, ob, g, gn, out):
        out[...] = _l0_combine(hf[...], hb[...], ga[...], of[...], ob[...], g[...], gn[...]).astype(BF16)

    tok = pl.BlockSpec((tm, 512), lambda i: (i, 0))
    return pl.pallas_call(
        body, name=name, grid=(T // tm,),
        in_specs=[tok, tok, pl.BlockSpec((tm, 512), lambda i: (i, 1)), tok, tok,
                  pl.BlockSpec((tm, 512), lambda i: (i, 6)), pl.BlockSpec((1, 512), lambda i: (0, 0))],
        out_specs=pl.BlockSpec((tm, 1024), lambda i: (i, 0)),
        out_shape=jax.ShapeDtypeStruct((T, 1024), BF16),
        compiler_params=_params("parallel"))(hs[0], hs[1], proj, o[0], o[1], proj, gain)


def _l0_combine_bwd(hs, proj, o, gain, dh_b, w_out, name):
    T = proj.shape[0]
    tm = min(512, T)

    def body(hf, hb, ga, of, ob, g, gn, dhb_ref, w_ref, dho_ref, dga_ref, do_ref, dg_ref, dgn_ref):
        @pl.when(pl.program_id(0) == 0)
        def _():
            dgn_ref[...] = jnp.zeros_like(dgn_ref)

        _, vjp = jax.vjp(_l0_combine, hf[...], hb[...], ga[...], of[...], ob[...], g[...], gn[...])
        dhf, _, dga, dof, _, dg, dgn = vjp(_dg(dhb_ref[...], w_ref[...], 1, 1))
        dho_ref[...] = dhf
        dga_ref[...] = dga
        do_ref[...] = dof
        dg_ref[...] = dg
        dgn_ref[...] += dgn

    tok = lambda: pl.BlockSpec((tm, 512), lambda i: (i, 0))
    return pl.pallas_call(
        body, name=name, grid=(T // tm,),
        in_specs=[tok(), tok(), pl.BlockSpec((tm, 512), lambda i: (i, 1)), tok(), tok(),
                  pl.BlockSpec((tm, 512), lambda i: (i, 6)), pl.BlockSpec((1, 512), lambda i: (0, 0)),
                  pl.BlockSpec((tm, D_MODEL), lambda i: (i, 0)), pl.BlockSpec(w_out.shape, lambda i: (0, 0))],
        out_specs=[tok(), tok(), tok(), tok(), pl.BlockSpec((1, 512), lambda i: (0, 0))],
        out_shape=[jax.ShapeDtypeStruct((T, 512), F32)] * 4 + [jax.ShapeDtypeStruct((1, 512), F32)],
        compiler_params=_params("arbitrary"))(hs[0], hs[1], proj, o[0], o[1], proj, gain, dh_b, w_out)


def _l0_assemble(dxa, dga, dq, df, dv, dg, name):
    T = dxa.shape[0]
    tm = min(512, T)

    def body(xa, ga, q0, q1, f0, f1, v0, v1, g, out):
        both = lambda a, b: (a[...].astype(F32) + b[...].astype(F32)).astype(BF16)
        out[...] = jnp.concatenate([xa[...].astype(BF16), ga[...].astype(BF16), both(q0, q1), f0[...], f1[...],
                                    both(v0, v1), g[...].astype(BF16)], axis=1)

    tok = lambda: pl.BlockSpec((tm, 512), lambda i: (i, 0))
    return pl.pallas_call(
        body, name=name, grid=(T // tm,),
        in_specs=[tok() for _ in range(9)],
        out_specs=pl.BlockSpec((tm, AB_IN), lambda i: (i, 0)),
        out_shape=jax.ShapeDtypeStruct((T, AB_IN), BF16),
        compiler_params=_params("parallel"))(dxa, dga, dq[0], dq[1], df[0], df[1], dv[0], dv[1], dg)


def _l1_combine_fwd(o, proj, gain, name):
    T = proj.shape[0]
    tm = min(512, T)

    def body(of, ob, r, gn, out):
        out[...] = _l1_combine(of[...], ob[...], r[...], gn[...]).astype(BF16)

    tok = pl.BlockSpec((tm, 1024), lambda i: (i, 0))
    return pl.pallas_call(
        body, name=name, grid=(T // tm,),
        in_specs=[tok, tok, pl.BlockSpec((tm, 1024), lambda i: (i, 2)), pl.BlockSpec((1, 1024), lambda i: (0, 0))],
        out_specs=pl.BlockSpec((tm, 1024), lambda i: (i, 0)),
        out_shape=jax.ShapeDtypeStruct((T, 1024), BF16),
        compiler_params=_params("parallel"))(o[0], o[1], proj, gain)


def _l1_combine_bwd(o, proj, gain, dh_b, w_out, name):
    T = proj.shape[0]
    tm = min(512, T)

    def body(of, ob, r, gn, dhb_ref, w_ref, do_ref, dr_ref, dgn_ref):
        @pl.when(pl.program_id(0) == 0)
        def _():
            dgn_ref[...] = jnp.zeros_like(dgn_ref)

        _, vjp = jax.vjp(_l1_combine, of[...], ob[...], r[...], gn[...])
        dof, _, dr, dgn = vjp(_dg(dhb_ref[...], w_ref[...], 1, 1))
        do_ref[...] = dof
        dr_ref[...] = dr
        dgn_ref[...] += dgn

    tok = lambda: pl.BlockSpec((tm, 1024), lambda i: (i, 0))
    return pl.pallas_call(
        body, name=name, grid=(T // tm,),
        in_specs=[tok(), tok(), pl.BlockSpec((tm, 1024), lambda i: (i, 2)),
                  pl.BlockSpec((1, 1024), lambda i: (0, 0)), tok(), pl.BlockSpec(w_out.shape, lambda i: (0, 0))],
        out_specs=[tok(), tok(), pl.BlockSpec((1, 1024), lambda i: (0, 0))],
        out_shape=[jax.ShapeDtypeStruct((T, 1024), F32)] * 2 + [jax.ShapeDtypeStruct((1, 1024), F32)],
        compiler_params=_params("arbitrary"))(o[0], o[1], proj, gain, dh_b, w_out)


def _l1_assemble(dq, dk, dv, dr, dlr, name):
    T = dr.shape[0]
    tm = min(512, T)

    def body(q0, q1, k0, k1, v0, v1, r, a, out):
        both = lambda x, y: (x[...].astype(F32) + y[...].astype(F32)).astype(BF16)
        out[...] = jnp.concatenate([both(q0, q1), both(k0, k1), both(v0, v1), r[...].astype(BF16),
                                    a[...].astype(BF16)], axis=1)

    tok = lambda w: pl.BlockSpec((tm, w), lambda i: (i, 0))
    return pl.pallas_call(
        body, name=name, grid=(T // tm,),
        in_specs=[tok(512), tok(512), tok(512), tok(512), tok(1024), tok(1024), tok(1024), tok(LANES)],
        out_specs=pl.BlockSpec((tm, GLA_IN_PAD), lambda i: (i, 0)),
        out_shape=jax.ShapeDtypeStruct((T, GLA_IN_PAD), BF16),
        compiler_params=_params("parallel"))(dq[0], dq[1], dk[0], dk[1], dv[0], dv[1], dr, dlr)


HBM_SPEC = pl.BlockSpec(memory_space=pltpu.HBM)


def _place():
    x, y, c = lax.axis_index("x"), lax.axis_index("y"), lax.axis_index("c")
    return x, y, c


def _allgather_vmem(x_shard, name):
    m_per, n = x_shard.shape

    def body(x_ref, out_ref, send_sems, recv_sems, local_sem):
        x, y, c = _place()
        me, sibling = (x, y, c), (x, y, 1 - c)
        chips = [(1 - x, y), (x, 1 - y), (1 - x, 1 - y)]

        def rows(px, py, pc):
            return out_ref.at[pl.ds((4 * px + 2 * py + pc) * m_per, m_per), :]

        def copy(k, block, to, src=None):
            return pltpu.make_async_remote_copy(
                src_ref=rows(*block) if src is None else src, dst_ref=rows(*block),
                send_sem=send_sems.at[k], recv_sem=recv_sems.at[k], device_id=to, device_id_type=MESH)

        mine = pltpu.make_async_copy(x_ref, rows(*me), local_sem)
        mine.start()
        first = [copy(0, me, sibling, src=x_ref)]
        first += [copy(1 + j, me, (*chip, c), src=x_ref) for j, chip in enumerate(chips)]
        for cp in first:
            cp.start()
        passed = [copy(4 + j, (*chip, c), sibling) for j, chip in enumerate(chips)]
        for j, chip in enumerate(chips):
            copy(1 + j, (*chip, c), me).wait_recv()
            passed[j].start()
        copy(0, sibling, me).wait_recv()
        for j, chip in enumerate(chips):
            copy(4 + j, (*chip, 1 - c), me).wait_recv()
        for cp in first + passed:
            cp.wait_send()
        mine.wait()

    vm = pl.BlockSpec(memory_space=pltpu.VMEM)
    return pl.pallas_call(
        body, name=name, in_specs=[vm], out_specs=vm,
        out_shape=jax.ShapeDtypeStruct((N_DEV * m_per, n), x_shard.dtype),
        scratch_shapes=[pltpu.SemaphoreType.DMA((7,)), pltpu.SemaphoreType.DMA((7,)), pltpu.SemaphoreType.DMA],
        compiler_params=pltpu.CompilerParams(has_side_effects=True, vmem_limit_bytes=VMEM_LIMIT))(x_shard)


SEM_SPEC = pl.BlockSpec(memory_space=pltpu.SEMAPHORE)
DATAFLOW_EFFECT = pltpu.SideEffectType.DATAFLOW_SIDE_EFFECTING


def _copies(plan, srcs, lands, send_sems, recv_sems):
    x, y, c = _place()
    return [pltpu.make_async_remote_copy(src_ref=s, dst_ref=d, send_sem=send_sems.at[k], recv_sem=recv_sems.at[k],
                                         device_id=dev, device_id_type=MESH)
            for k, (s, d, dev) in enumerate(plan(srcs, lands, x, y, c))]


def _copies_start(plan, n_copies, srcs, lands, name):
    ns, nl = len(srcs), len(lands)

    def body(*refs):
        send_sems, recv_sems = refs[ns + nl], refs[ns + nl + 1]
        for cp in _copies(plan, refs[:ns], refs[ns:ns + nl], send_sems, recv_sems):
            cp.start()
        refs[-1][...] = jnp.zeros_like(refs[-1])

    arrays = list(srcs) + list(lands)
    res = pl.pallas_call(
        body, name=name,
        in_specs=[HBM_SPEC] * (ns + nl),
        out_specs=tuple([SEM_SPEC, SEM_SPEC] + [HBM_SPEC] * (ns + nl) + [pl.BlockSpec(memory_space=pltpu.VMEM)]),
        out_shape=tuple([pltpu.SemaphoreType.DMA((n_copies,)), pltpu.SemaphoreType.DMA((n_copies,))]
                        + [pltpu.HBM(a.shape, a.dtype) for a in arrays]
                        + [jax.ShapeDtypeStruct((SUBLANES, LANES), F32)]),
        input_output_aliases={i: 2 + i for i in range(ns + nl)},
        compiler_params=pltpu.CompilerParams(has_side_effects=DATAFLOW_EFFECT),
    )(*[pltpu.with_memory_space_constraint(a, pltpu.HBM) for a in arrays])
    return res[0], res[1], list(res[2:2 + ns]), list(res[2 + ns:2 + ns + nl]), res[-1]


def _copies_wait(plan, started, after, name):
    send_sems, recv_sems, srcs, lands, _ = started
    ns, nl = len(srcs), len(lands)

    def body(*refs):
        for cp in _copies(plan, refs[:ns], refs[ns:ns + nl], refs[ns + nl], refs[ns + nl + 1]):
            cp.wait_send()
            cp.wait_recv()

    arrays = list(srcs) + list(lands)
    res = pl.pallas_call(
        body, name=name,
        in_specs=[HBM_SPEC] * (ns + nl) + [SEM_SPEC, SEM_SPEC, pl.BlockSpec(memory_space=pl.ANY)],
        out_specs=tuple([HBM_SPEC] * (ns + nl)),
        out_shape=tuple(pltpu.HBM(a.shape, a.dtype) for a in arrays),
        input_output_aliases={i: i for i in range(ns + nl)},
        compiler_params=pltpu.CompilerParams(has_side_effects=DATAFLOW_EFFECT),
    )(*arrays, send_sems, recv_sems, after)
    return list(res[:ns]), list(res[ns:])


def _after(token, value):
    return value + token[0:1, 0:1].astype(value.dtype)


def _chips(x, y):
    return [(1 - x, y), (x, 1 - y), (1 - x, 1 - y)]


def _plan_gather_first(srcs, lands, x, y, c):
    me = 4 * x + 2 * y + c
    out = []
    for s, l in zip(srcs, lands):
        out.append((s, l.at[me], (x, y, 1 - c)))
        out += [(s, l.at[me], (*chip, c)) for chip in _chips(x, y)]
    return out


def _plan_gather_pass(srcs, lands, x, y, c):
    out = []
    for l in lands:
        for chip in _chips(x, y):
            slot = l.at[4 * chip[0] + 2 * chip[1] + c]
            out.append((slot, slot, (x, y, 1 - c)))
    return out


def _plan_grads_sibling(srcs, lands, x, y, c):
    return [(s.at[2 * q + (1 - c)], l.at[q], (x, y, 1 - c)) for s, l in zip(srcs, lands) for q in range(4)]


def _plan_grads_chips(srcs, lands, x, y, c):
    return [(s.at[2 * chip[0] + chip[1]], l.at[k], (*chip, c))
            for s, l in zip(srcs, lands) for k, chip in enumerate(_chips(x, y))]


def _landing(n_slots, like):
    return [lax.empty((n_slots,) + a.shape[1:], a.dtype) for a in like]


def _sum_slots(g, name):
    _, R, C = g.shape
    tr = min(256, R)
    assert R % tr == 0

    def body(g_ref, o_ref):
        acc = g_ref[0]
        for j in range(1, N_DEV):
            acc = acc + g_ref[j]
        o_ref[...] = acc

    return pl.pallas_call(
        body, name=name, grid=(R // tr,),
        in_specs=[pl.BlockSpec((N_DEV, tr, C), lambda i: (0, i, 0))],
        out_specs=pl.BlockSpec((tr, C), lambda i: (i, 0)),
        out_shape=jax.ShapeDtypeStruct((R, C), F32),
        compiler_params=_params("parallel"))(g)


def _chip_partial(g, r1, place, name):
    _, R, C = g.shape
    tr = min(256, R)
    assert R % tr == 0

    def body(pl_ref, g_ref, r_ref, pb_ref, pm_ref):
        q = pl.program_id(1)
        s = g_ref[...] + r_ref[...]
        pb_ref[...] = s.astype(BF16)

        @pl.when(q == pl_ref[1])
        def _():
            pm_ref[...] = s

    grid_spec = pltpu.PrefetchScalarGridSpec(
        num_scalar_prefetch=1, grid=(R // tr, 4),
        in_specs=[pl.BlockSpec((None, tr, C), lambda r, q, p: (2 * q + p[0], r, 0)),
                  pl.BlockSpec((None, tr, C), lambda r, q, p: (q, r, 0))],
        out_specs=[pl.BlockSpec((None, tr, C), lambda r, q, p: (q, r, 0)),
                   pl.BlockSpec((tr, C), lambda r, q, p: (r, 0))])
    return pl.pallas_call(
        body, name=name, grid_spec=grid_spec,
        out_shape=[jax.ShapeDtypeStruct((4, R, C), BF16), jax.ShapeDtypeStruct((R, C), F32)],
        compiler_params=_params("parallel", "arbitrary"))(place, g, r1)


def _adamw_update(w, g, m, v, grad_ref, delta_ref, m_ref, v_ref):
    mn = ADAM_B1 * m + (1.0 - ADAM_B1) * g
    vn = ADAM_B2 * v + (1.0 - ADAM_B2) * jnp.square(g)
    m_hat = mn / (1.0 - ADAM_B1 ** ADAM_STEP)
    v_hat = vn / (1.0 - ADAM_B2 ** ADAM_STEP)
    grad_ref[...] = g
    delta_ref[...] = -ADAM_LR * (m_hat / (jnp.sqrt(v_hat) + ADAM_EPS) + ADAM_WD * w)
    m_ref[...] = mn
    v_ref[...] = vn


def _adamw_whole(w, g, m, v, name):
    def body(w_ref, g_ref, m_ref, v_ref, go, do, mo, vo):
        _adamw_update(w_ref[...], g_ref[...], m_ref[...], v_ref[...], go, do, mo, vo)

    vm = pl.BlockSpec(memory_space=pltpu.VMEM)
    return pl.pallas_call(body, name=name, in_specs=[vm] * 4, out_specs=[vm] * 4,
                          out_shape=[jax.ShapeDtypeStruct(w.shape, F32)] * 4)(w, g, m, v)


def _adamw(w, gparts, m, v, name):
    _, R, C = w.shape
    tr = min(256, R)
    assert R % tr == 0

    def body(w_ref, g0_ref, g3_ref, m_ref, v_ref, go, do, mo, vo):
        g = g0_ref[...]
        for k in range(3):
            g = g + g3_ref[k].astype(F32)
        _adamw_update(w_ref[...], g, m_ref[...], v_ref[...], go, do, mo, vo)

    blk = pl.BlockSpec((tr, C), lambda i: (i, 0))
    wblk = pl.BlockSpec((None, tr, C), lambda i: (0, i, 0))
    return pl.pallas_call(
        body, name=name, grid=(R // tr,),
        in_specs=[wblk, blk, pl.BlockSpec((3, tr, C), lambda i: (0, i, 0)), wblk, wblk], out_specs=[wblk] * 4,
        out_shape=[jax.ShapeDtypeStruct(w.shape, F32)] * 4,
        compiler_params=_params("parallel"))(w, gparts[0], gparts[1], m, v)


def _adamw_layers(w, parts, m, v, name):
    _, R, C = w.shape
    tr = min(256, R)
    assert R % tr == 0

    def body(w_ref, p0, r0, p1, r1, m_ref, v_ref, go, do, mo, vo):
        gs = []
        for p, r in ((p0, r0), (p1, r1)):
            g = p[...]
            for k in range(3):
                g = g + r[k].astype(F32)
            gs.append(g)
        g = jnp.where(pl.program_id(0) == 0, gs[0], gs[1])
        _adamw_update(w_ref[...], g, m_ref[...], v_ref[...], go, do, mo, vo)

    lay = pl.BlockSpec((None, tr, C), lambda l, i: (l, i, 0))
    one = pl.BlockSpec((tr, C), lambda l, i: (i, 0))
    three = pl.BlockSpec((3, tr, C), lambda l, i: (0, i, 0))
    return pl.pallas_call(
        body, name=name, grid=(2, R // tr), in_specs=[lay, one, three, one, three, lay, lay],
        out_specs=[lay] * 4, out_shape=[jax.ShapeDtypeStruct((2, R, C), F32)] * 4,
        compiler_params=_params("parallel", "parallel"))(w, parts[0][0], parts[0][1], parts[1][0], parts[1][1], m, v)


SMALL_SHARDED = ("rg_conv_w", "rg_b_a", "rg_b_x", "rg_lambda", "gla_w_gate_up", "gla_b_gate", "gla_norm")
SMALL_REPLICATED = ("norm_mix", "norm_mlp", "norm_final", "rg_conv_b", "rg_w_a", "rg_w_x", "hg_lb_logits", "hg_norm")
WEIGHT_NAMES = ("norm_mix", "norm_mlp", "norm_final", "mlp_w1", "mlp_w2", "ab_w_in", "ab_w_out", "rg_conv_w",
                "rg_conv_b", "rg_w_a", "rg_b_a", "rg_w_x", "rg_b_x", "rg_lambda", "hg_lb_logits", "hg_norm",
                "gla_w_in", "gla_w_out", "gla_w_gate_up", "gla_b_gate", "gla_norm")


def _rows128(a):
    return a.reshape(-1, LANES)


def _part_rows(a):
    return -(-(a.size // LANES) // SUBLANES) * SUBLANES


def _pack_rows(arrays, pad_to=SUBLANES):
    parts = [jnp.pad(_rows128(a), ((0, _part_rows(a) - a.size // LANES), (0, 0))) for a in arrays]
    total = sum(p.shape[0] for p in parts)
    extra = (-total) % pad_to
    if extra:
        parts.append(jnp.zeros((extra, LANES), parts[0].dtype))
    return jnp.concatenate(parts, axis=0)


def _unshard_last(g, shape_local):
    nd = len(shape_local)
    t = g.reshape((N_DEV,) + tuple(shape_local))
    t = jnp.moveaxis(t, 0, nd - 1)
    return t.reshape(tuple(shape_local[:-1]) + (N_DEV * shape_local[-1],))


def _block_diag(w):
    eye = jnp.eye(8, dtype=w.dtype)
    return (w[:, :, :, None, :] * eye[None, :, None, :, None]).reshape(2, RG_W, RG_W)


def _block_diag_extract(dw):
    t = dw.reshape(2, 8, 64, 8, 64)
    return jnp.moveaxis(jnp.diagonal(t, axis1=1, axis2=3), -1, 1)


def kernel(x, norm_mix, norm_mlp, norm_final, mlp_w1, mlp_w2, ab_w_in, ab_w_out, rg_conv_w, rg_conv_b, rg_w_a, rg_b_a, rg_w_x, rg_b_x, rg_lambda, hg_lb_logits, hg_norm, gla_w_in, gla_w_out, gla_w_gate_up, gla_b_gate, gla_norm, loss_target, m_norm_mix, m_norm_mlp, m_norm_final, m_mlp_w1, m_mlp_w2, m_ab_w_in, m_ab_w_out, m_rg_conv_w, m_rg_conv_b, m_rg_w_a, m_rg_b_a, m_rg_w_x, m_rg_b_x, m_rg_lambda, m_hg_lb_logits, m_hg_norm, m_gla_w_in, m_gla_w_out, m_gla_w_gate_up, m_gla_b_gate, m_gla_norm, v_norm_mix, v_norm_mlp, v_norm_final, v_mlp_w1, v_mlp_w2, v_ab_w_in, v_ab_w_out, v_rg_conv_w, v_rg_conv_b, v_rg_w_a, v_rg_b_a, v_rg_w_x, v_rg_b_x, v_rg_lambda, v_hg_lb_logits, v_hg_norm, v_gla_w_in, v_gla_w_out, v_gla_w_gate_up, v_gla_b_gate, v_gla_norm):
    w_loc = dict(norm_mix=norm_mix, norm_mlp=norm_mlp, norm_final=norm_final, mlp_w1=mlp_w1, mlp_w2=mlp_w2,
                 ab_w_in=ab_w_in, ab_w_out=ab_w_out, rg_conv_w=rg_conv_w, rg_conv_b=rg_conv_b, rg_w_a=rg_w_a,
                 rg_b_a=rg_b_a, rg_w_x=rg_w_x, rg_b_x=rg_b_x, rg_lambda=rg_lambda, hg_lb_logits=hg_lb_logits,
                 hg_norm=hg_norm, gla_w_in=gla_w_in, gla_w_out=gla_w_out, gla_w_gate_up=gla_w_gate_up,
                 gla_b_gate=gla_b_gate, gla_norm=gla_norm)
    m_loc = dict(norm_mix=m_norm_mix, norm_mlp=m_norm_mlp, norm_final=m_norm_final, mlp_w1=m_mlp_w1,
                 mlp_w2=m_mlp_w2, ab_w_in=m_ab_w_in, ab_w_out=m_ab_w_out, rg_conv_w=m_rg_conv_w,
                 rg_conv_b=m_rg_conv_b, rg_w_a=m_rg_w_a, rg_b_a=m_rg_b_a, rg_w_x=m_rg_w_x, rg_b_x=m_rg_b_x,
                 rg_lambda=m_rg_lambda, hg_lb_logits=m_hg_lb_logits, hg_norm=m_hg_norm, gla_w_in=m_gla_w_in,
                 gla_w_out=m_gla_w_out, gla_w_gate_up=m_gla_w_gate_up, gla_b_gate=m_gla_b_gate,
                 gla_norm=m_gla_norm)
    v_loc = dict(norm_mix=v_norm_mix, norm_mlp=v_norm_mlp, norm_final=v_norm_final, mlp_w1=v_mlp_w1,
                 mlp_w2=v_mlp_w2, ab_w_in=v_ab_w_in, ab_w_out=v_ab_w_out, rg_conv_w=v_rg_conv_w,
                 rg_conv_b=v_rg_conv_b, rg_w_a=v_rg_w_a, rg_b_a=v_rg_b_a, rg_w_x=v_rg_w_x, rg_b_x=v_rg_b_x,
                 rg_lambda=v_rg_lambda, hg_lb_logits=v_hg_lb_logits, hg_norm=v_hg_norm, gla_w_in=v_gla_w_in,
                 gla_w_out=v_gla_w_out, gla_w_gate_up=v_gla_w_gate_up, gla_b_gate=v_gla_b_gate,
                 gla_norm=v_gla_norm)

    T = x.shape[1]
    h0 = x.reshape(T, D_MODEL)
    target = loss_target.reshape(T, D_MODEL)
    ax, ay, ac = lax.axis_index("x"), lax.axis_index("y"), lax.axis_index("c")
    dev = 4 * ax + 2 * ay + ac
    place = jnp.stack([ac, 2 * ax + ay]).astype(jnp.int32)

    abin_shard = ab_w_in[0].astype(BF16)
    first_started = _copies_start(_plan_gather_first, 4, [abin_shard], _landing(N_DEV, [abin_shard[None]]),
                                  "ag_first_start")
    rest_shards = [mlp_w1[0].astype(BF16), mlp_w2[0].astype(BF16), gla_w_in[0].astype(BF16),
                   gla_w_out[0].astype(BF16), mlp_w1[1].astype(BF16), mlp_w2[1].astype(BF16),
                   _after(first_started[4], ab_w_out[0].astype(BF16))]
    ag_started = _copies_start(_plan_gather_first, 4 * len(rest_shards), rest_shards,
                               _landing(N_DEV, [s[None] for s in rest_shards]), "ag_rest_start")

    small_local = [w_loc[n] for n in SMALL_SHARDED]
    small_g = _allgather_vmem(_pack_rows(small_local, 8), "ag_small")
    small_g = small_g.reshape(N_DEV, -1, LANES)
    full = {}
    off = 0
    for n, a in zip(SMALL_SHARDED, small_local):
        full[n] = _unshard_last(small_g[:, off:off + a.size // LANES].reshape(N_DEV, a.size), a.shape)
        off += _part_rows(a)
    conv_w = full["rg_conv_w"][0]
    b_a, b_x, lam = full["rg_b_a"][0], full["rg_b_x"][0], full["rg_lambda"][0]
    w_up, b_gate, g_norm = full["gla_w_gate_up"][0], full["gla_b_gate"][0], full["gla_norm"]

    cw8 = jnp.pad(conv_w, ((0, 4), (0, 0)))
    wbd = jnp.concatenate([_block_diag(rg_w_a[0]), _block_diag(rg_w_x[0])], axis=2).astype(BF16)
    rg_bias = jnp.concatenate([b_a, b_x], axis=1).reshape(2, 1, 2 * RG_W)
    lam3 = lam.reshape(2, 1, RG_W)
    l0, l1 = hg_lb_logits[0:1], hg_lb_logits[1:2]
    wup_pad = jnp.zeros((2, LANES, 512), F32).at[0, 0:16].set(w_up[0]).at[1, 16:32].set(w_up[1])
    bg3 = b_gate.reshape(2, 1, 512)
    nmix0, nmix1 = norm_mix[0:1], norm_mix[1:2]
    nmlp0, nmlp1 = norm_mlp[0:1], norm_mlp[1:2]
    nfin = norm_final.reshape(1, D_MODEL)

    prepared = (ag_started[4] + cw8[:, 0:LANES] + wup_pad[0, 0:SUBLANES, 0:LANES] + rg_bias[0, :, 0:LANES]
                + wbd[0, 0:SUBLANES, 0:LANES].astype(F32) + lam3[0, :, 0:LANES] + bg3[0, :, 0:LANES])
    (abin_shard,), abin_l = _copies_wait(_plan_gather_first, first_started, prepared, "ag_first_wait")
    first_pass = _copies_start(_plan_gather_pass, 3, [], abin_l, "ag_first_pass_start")
    _, (abin_g,) = _copies_wait(_plan_gather_pass, first_pass, first_pass[4], "ag_first_pass_wait")
    abin_g = lax.dynamic_update_index_in_dim(abin_g, abin_shard, dev, 0)
    wab_in = jnp.transpose(abin_g, (1, 0, 2)).reshape(D_MODEL, AB_IN)
    proj0, y0 = _norm_matmul(h0, _after(ag_started[4], nmix0), wab_in, "l0_in_proj")
    xc = _rg_conv_fwd(proj0, cw8, rg_conv_b, "rg_conv")
    hs = _rg_scan_fwd(xc, wbd, rg_bias, lam3, "rg_scan")
    o_hg, s_hg = _hg_fwd(proj0, l0, l1, "hg_chunks")
    both_done = hs[0][0:SUBLANES, 0:LANES] + o_hg[0][0:SUBLANES, 0:LANES]
    rest_shards, rest_lands = _copies_wait(_plan_gather_first, ag_started, both_done, "ag_rest_wait")
    pass_started = _copies_start(_plan_gather_pass, 3 * len(rest_lands), [], rest_lands, "ag_pass_start")
    mixin0 = _l0_combine_fwd(hs, proj0, o_hg, _after(pass_started[4], hg_norm), "l0_combine")
    _, rest_g = _copies_wait(_plan_gather_pass, pass_started, mixin0, "ag_pass_wait")
    rest_g = [lax.dynamic_update_index_in_dim(g, s, dev, 0) for g, s in zip(rest_g, rest_shards)]
    wab_out = rest_g[6].reshape(D_MODEL, D_MODEL)
    h1 = _matmul_res(mixin0, wab_out, h0, "l0_out_proj")
    w1g = (rest_g[0], rest_g[4])
    w2f = (rest_g[1].reshape(D_FF, D_MODEL), rest_g[5].reshape(D_FF, D_MODEL))
    wgla_in = jnp.pad(jnp.transpose(rest_g[2], (1, 0, 2)).reshape(D_MODEL, GLA_IN),
                      ((0, 0), (0, GLA_IN_PAD - GLA_IN)))
    wgla_out = rest_g[3].reshape(D_MODEL, D_MODEL)
    h2, pre0, ym0 = _mlp_fwd(h1, nmlp0, w1g[0], w2f[0], "mlp0")
    proj1, y1 = _norm_matmul(h2, nmix1, wgla_in, "l1_in_proj")
    z_gate, lr_b = _gate_logits(proj1, wup_pad, bg3, "gla_gate_logits")
    o_gla, s_gla = _gla_fwd(proj1, z_gate, "gla_chunks")
    mixin1 = _l1_combine_fwd(o_gla, proj1, g_norm, "l1_combine")
    h3 = _matmul_res(mixin1, wgla_out, h2, "l1_out_proj")
    h4, pre1, ym1 = _mlp_fwd(h3, nmlp1, w1g[1], w2f[1], "mlp1")
    loss_blk, dh4, dh4b, d_nfin = _final_loss(h4, nfin, target, "final_loss")

    dh3, dh3b, dpre1, act1, d_nmlp1 = _mlp_bwd(dh4, dh4b, h3, nmlp1, pre1, w1g[1], w2f[1], "mlp1_bwd")
    g_w1_1 = _wgrad(ym1, dpre1, 512, "mlp1_dw1", sharded_cols=True)
    g_w2_1 = _wgrad(act1, dh4b, 512, "mlp1_dw2")
    g_gla_out = _wgrad(mixin1, dh3b, 512, "l1_out_dw")
    do_gla, dr, d_gnorm = _l1_combine_bwd(o_gla, proj1, g_norm, dh3b, wgla_out, "l1_combine_bwd")
    dq1, dk1, dv1, dz_gate = _gla_bwd(proj1, z_gate, s_gla, do_gla, "gla_chunks_bwd")
    dlr1, d_bg, dz_b = _gate_logits_bwd(dz_gate, wup_pad, "gla_gate_logits_bwd")
    d_wup = [_wgrad(lr_b, dz_b[d], 512, "gla_gate_dw%d" % d) for d in range(2)]
    dproj1 = _l1_assemble(dq1, dk1, dv1, dr, dlr1, "l1_assemble")
    dh2, dh2b, d_nmix1 = _dgrad_norm(dproj1, wgla_in, h2, nmix1, dh3, "l1_in_dgrad")
    g_gla_in = _wgrad(y1, dproj1, 640, "l1_in_dw")

    def reduce_start(grads, tag):
        return _copies_start(_plan_grads_sibling, 4 * len(grads), grads, _landing(4, grads), "rs_%s_d2d_start" % tag)

    def reduce_mid(started, after, tag):
        grads, got = _copies_wait(_plan_grads_sibling, started, after, "rs_%s_d2d_wait" % tag)
        parts = [_chip_partial(g, r, place, "rs_%s_partial%d" % (tag, a)) for a, (g, r) in enumerate(zip(grads, got))]
        pb = [p[0] for p in parts]
        return _copies_start(_plan_grads_chips, 3 * len(pb), pb, _landing(3, pb), "rs_%s_ici_start" % tag), \
            [p[1] for p in parts]

    def reduce_end(started, mine, after, tag):
        _, got = _copies_wait(_plan_grads_chips, started, after, "rs_%s_ici_wait" % tag)
        return list(zip(mine, got))

    slots_l1 = [g_w1_1, g_w2_1.reshape(N_DEV, 512, D_MODEL),
                jnp.transpose(g_gla_in[:, :GLA_IN].reshape(D_MODEL, N_DEV, GLA_IN // N_DEV), (1, 0, 2)),
                g_gla_out.reshape(N_DEV, 128, D_MODEL)]
    ra_d2d = reduce_start(slots_l1, "l1")

    dh1, dh1b, dpre0, act0, d_nmlp0 = _mlp_bwd(dh2, dh2b, h1, _after(ra_d2d[4], nmlp0), pre0, w1g[0], w2f[0],
                                               "mlp0_bwd")
    g_w1_0 = _wgrad(ym0, dpre0, 512, "mlp0_dw1", sharded_cols=True)
    g_w2_0 = _wgrad(act0, dh2b, 512, "mlp0_dw2")
    ra_ici, ra_mine = reduce_mid(ra_d2d, g_w2_0, "l1")
    rb_d2d = reduce_start([g_w1_0, g_w2_0.reshape(N_DEV, 512, D_MODEL)], "mlp0")
    g_ab_out = _wgrad(mixin0, dh1b, 512, "l0_out_dw")
    dho, dga, do_hg, dg_gate, d_hgnorm = _l0_combine_bwd(
        hs, proj0, o_hg, _after(rb_d2d[4], _after(ra_ici[4], hg_norm)), dh1b, wab_out, "l0_combine_bwd")
    dxc, d_wbd, d_rgb, d_lam = _rg_scan_bwd(xc, wbd, rg_bias, lam3, hs, dho, "rg_scan_bwd")
    dxa, d_cw8, d_cb = _rg_conv_bwd(dxc, proj0, cw8, "rg_conv_bwd")
    dq0, df0, dv0, d_l0, d_l1 = _hg_bwd(proj0, l0, l1, s_hg, do_hg, "hg_chunks_bwd")
    rb_ici, rb_mine = reduce_mid(rb_d2d, d_l0, "mlp0")
    dproj0 = _l0_assemble(dxa, dga, dq0, df0, dv0, dg_gate, "l0_assemble")
    dx, _, d_nmix0 = _dgrad_norm(dproj0, wab_in, h0, _after(rb_ici[4], nmix0), dh1, "l0_in_dgrad")

    d_wa = _block_diag_extract(d_wbd[:, :, :RG_W])[None]
    d_wx = _block_diag_extract(d_wbd[:, :, RG_W:])[None]
    small_full = {
        "norm_mix": jnp.concatenate([d_nmix0, d_nmix1], axis=0), "norm_mlp": jnp.concatenate([d_nmlp0, d_nmlp1], axis=0),
        "norm_final": d_nfin.reshape(D_MODEL), "rg_conv_b": d_cb, "rg_w_a": d_wa, "rg_w_x": d_wx,
        "hg_lb_logits": jnp.concatenate([d_l0[0] + d_l0[1], d_l1[0] + d_l1[1]], axis=0), "hg_norm": d_hgnorm,
        "rg_conv_w": d_cw8[0:4][None], "rg_b_a": d_rgb[:, 0, :RG_W][None], "rg_b_x": d_rgb[:, 0, RG_W:][None],
        "rg_lambda": d_lam[:, 0, :][None],
        "gla_w_gate_up": jnp.stack([d_wup[0][0:16], d_wup[1][16:32]])[None], "gla_b_gate": d_bg[:, 0, :][None],
        "gla_norm": d_gnorm}
    small_names = SMALL_REPLICATED + SMALL_SHARDED
    packed = _pack_rows([loss_blk] + [small_full[n] for n in small_names], 256)
    ar_first = _copies_start(_plan_gather_first, 4, [packed], _landing(N_DEV, [packed[None]]), "ar_small_start")

    g_ab_in = _wgrad(y0, dproj0, 512, "l0_in_dw", behind=ar_first[4])
    rc_d2d = reduce_start([jnp.transpose(g_ab_in.reshape(D_MODEL, N_DEV, AB_IN // N_DEV), (1, 0, 2)),
                           g_ab_out.reshape(N_DEV, 128, D_MODEL)], "ab")
    (packed,), ar_lands = _copies_wait(_plan_gather_first, ar_first, rc_d2d[4], "ar_small_wait")
    ar_pass = _copies_start(_plan_gather_pass, 3, [], ar_lands, "ar_small_pass_start")
    rc_ici, rc_mine = reduce_mid(rc_d2d, ar_pass[4], "ab")
    _, (ar_gathered,) = _copies_wait(_plan_gather_pass, ar_pass, rc_ici[4], "ar_small_pass_wait")
    summed = _sum_slots(lax.dynamic_update_index_in_dim(ar_gathered, packed, dev, 0), "ar_small_sum")
    loss = summed[0, 0]

    pieces_l1 = reduce_end(ra_ici, ra_mine, rc_ici[4], "l1")
    res_gla_in = _adamw(gla_w_in, pieces_l1[2], m_gla_w_in, v_gla_w_in, "adamw_gla_in")
    res_gla_out = _adamw(gla_w_out, pieces_l1[3], m_gla_w_out, v_gla_w_out, "adamw_gla_out")
    pieces_mlp0 = reduce_end(rb_ici, rb_mine, res_gla_out[0], "mlp0")
    res_w1 = _adamw_layers(mlp_w1, (pieces_mlp0[0], pieces_l1[0]), m_mlp_w1, v_mlp_w1, "adamw_mlp_w1")
    res_w2 = _adamw_layers(mlp_w2, (pieces_mlp0[1], pieces_l1[1]), m_mlp_w2, v_mlp_w2, "adamw_mlp_w2")
    res = {"mlp_w1": tuple(res_w1), "mlp_w2": tuple(res_w2),
           "gla_w_in": tuple(res_gla_in), "gla_w_out": tuple(res_gla_out)}

    off = SUBLANES
    for n in small_names:
        a = small_full[n]
        gfull = summed[off:off + a.size // LANES].reshape(a.shape)
        off += _part_rows(a)
        local = w_loc[n].shape
        if n in SMALL_SHARDED:
            gfull = lax.dynamic_slice_in_dim(gfull, dev * local[-1], local[-1], axis=gfull.ndim - 1)
        flat = (-1, local[-1])
        outs = _adamw_whole(w_loc[n].reshape(flat), gfull.reshape(flat), m_loc[n].reshape(flat),
                            v_loc[n].reshape(flat), "adamw_" + n)
        res[n] = tuple(o.reshape(local) for o in outs)
    others_done = (res_w1[1][0, 0:SUBLANES, 0:LANES] + res_w2[1][0, 0:SUBLANES, 0:LANES]
                   + res_gla_in[1][0, 0:SUBLANES, 0:LANES])
    pieces_ab = reduce_end(rc_ici, rc_mine, others_done, "ab")
    res["ab_w_in"] = tuple(_adamw(ab_w_in, pieces_ab[0], m_ab_w_in, v_ab_w_in, "adamw_ab_in"))
    res["ab_w_out"] = tuple(_adamw(ab_w_out, pieces_ab[1], m_ab_w_out, v_ab_w_out, "adamw_ab_out"))

    grad_x = dx.reshape(1, T, D_MODEL)
    out = [loss, grad_x]
    for k in range(4):
        out += [res[n][k] for n in WEIGHT_NAMES]
    return tuple(out)
```

```python
import jax
import jax.numpy as jnp
from jax import lax
from jax.experimental import pallas as pl
from jax.experimental.pallas import tpu as pltpu

F32, BF16 = jnp.float32, jnp.bfloat16
HI = lax.Precision.HIGHEST
MESH = pl.DeviceIdType.MESH

D_MODEL = 1024
D_FF = 4096
RG_W = 512
HG_W = 512
CHUNK = 64
EPS = 1e-6
RG_C = 8.0
AB_IN = 3584
GLA_IN = 3104
GLA_IN_PAD = 3200
N_DEV = 8
LANES = 128
SUBLANES = 8
VMEM_LIMIT = 48 * 1024 * 1024

ADAM_LR, ADAM_B1, ADAM_B2, ADAM_EPS, ADAM_WD, ADAM_STEP = 0.001, 0.9, 0.999, 1e-08, 0.01, 10


def _params(*sem):
    return pltpu.CompilerParams(dimension_semantics=sem, vmem_limit_bytes=VMEM_LIMIT)


def _dg(a, b, ca, cb):
    return lax.dot_general(a.astype(BF16), b.astype(BF16), (((ca,), (cb,)), ((), ())),
                           preferred_element_type=F32)


@jax.custom_vjp
def _mm_nn(a, b):
    return _dg(a, b, 1, 0)


_mm_nn.defvjp(lambda a, b: (_dg(a, b, 1, 0), (a, b)),
              lambda res, g: (_dg(g, res[1], 1, 1), _dg(res[0], g, 0, 0)))


@jax.custom_vjp
def _mm_nt(a, b):
    return _dg(a, b, 1, 1)


_mm_nt.defvjp(lambda a, b: (_dg(a, b, 1, 1), (a, b)),
              lambda res, g: (_dg(g, res[1], 1, 0), _dg(g, res[0], 0, 0)))


@jax.custom_vjp
def _mm_tn(a, b):
    return _dg(a, b, 0, 0)


_mm_tn.defvjp(lambda a, b: (_dg(a, b, 0, 0), (a, b)),
              lambda res, g: (_dg(res[1], g, 1, 1), _dg(res[0], g, 1, 0)))


@jax.custom_vjp
def _cum(tri, tri_t, x):
    return jnp.dot(tri, x, precision=HI, preferred_element_type=F32)


_cum.defvjp(lambda tri, tri_t, x: (jnp.dot(tri, x, precision=HI, preferred_element_type=F32), (tri, tri_t)),
            lambda res, g: (jnp.zeros_like(res[0]), jnp.zeros_like(res[1]),
                            jnp.dot(res[1], g, precision=HI, preferred_element_type=F32)))


def _sig(x):
    return 1.0 / (1.0 + jnp.exp(-x))


def _gelu(x):
    return 0.5 * x * (1.0 + jnp.tanh(0.7978845608028654 * (x + 0.044715 * (x * x * x))))


def _softplus(z):
    return jnp.maximum(z, 0.0) + jnp.log(1.0 + jnp.exp(-jnp.abs(z)))


def _rms(x):
    return lax.rsqrt(jnp.mean(x * x, axis=-1, keepdims=True) + EPS)


def _rmsnorm_bwd(x, gain, dy):
    r = _rms(x)
    xh = x * r
    dgain = jnp.sum(dy * xh, axis=0, keepdims=True)
    dxh = dy * gain
    dx = r * (dxh - xh * jnp.mean(dxh * xh, axis=-1, keepdims=True))
    return dx, dgain


def _headnorm(o, gain, n_heads, hd):
    parts = []
    for h in range(n_heads):
        oh = o[:, h * hd:(h + 1) * hd]
        parts.append(oh * _rms(oh))
    return jnp.concatenate(parts, axis=1) * gain


def _tri_consts(d):
    row = lax.broadcasted_iota(jnp.int32, (CHUNK, CHUNK), 0)
    col = lax.broadcasted_iota(jnp.int32, (CHUNK, CHUNK), 1)
    ge = (row >= col).astype(F32)
    le = (row <= col).astype(F32)
    r1 = lax.broadcasted_iota(jnp.int32, (CHUNK, 1), 0)
    if d == 0:
        return ge, le, (r1 <= CHUNK // 2).astype(F32)
    return le, ge, (r1 >= CHUNK // 2 - 1).astype(F32)


def _chunk_core(qh, k, v, logf, st_prev, tri, tri_t, mref, n_heads, dk, dv):
    cum = _cum(tri, tri_t, logf)
    ref = jnp.sum(logf * mref, axis=0, keepdims=True)
    last = jnp.sum(logf, axis=0, keepdims=True)
    q_in = qh * jnp.exp(cum - ref)
    k_in = k * jnp.exp(ref - cum)
    k_st = k * jnp.exp(last - cum)
    q_dec = qh * jnp.exp(cum)
    decay = jnp.exp(last)
    outs, sts = [], []
    for h in range(n_heads):
        sk = slice(h * dk, (h + 1) * dk)
        sv = slice(h * dv, (h + 1) * dv)
        sc = _mm_nt(q_in[:, sk], k_in[:, sk]) * tri
        o = _mm_nn(sc, v[:, sv]) + _mm_nt(q_dec[:, sk], st_prev[h])
        sts.append(st_prev[h] * decay[:, sk] + _mm_tn(v[:, sv], k_st[:, sk]))
        outs.append(o)
    return jnp.concatenate(outs, axis=1), tuple(sts)


def _hg_chunk(q, f, v, l0, l1, st_prev, tri, tri_t, mref):
    lb = _sig(l0 - l1)
    sg = _sig(f)
    qh = q * _sig(q)
    logf = jnp.log(lb + (1.0 - lb) * sg)
    k = (1.0 - lb) * (1.0 - sg)
    return _chunk_core(qh, k, v, logf, st_prev, tri, tri_t, mref, 4, 128, 128)


def _gla_chunk(q, k, v, z, st_prev, tri, tri_t, mref):
    logf = (jnp.minimum(z, 0.0) - jnp.log(1.0 + jnp.exp(-jnp.abs(z)))) * (1.0 / 16.0)
    qh = q * (128.0 ** -0.5)
    return _chunk_core(qh, k, v, logf, st_prev, tri, tri_t, mref, 4, 128, 256)


def _rg_gates(xc, wbd, bias, lam):
    z = _mm_nn(xc, wbd) + bias
    r = _sig(z[:, :RG_W])
    i = _sig(z[:, RG_W:])
    log_a = -RG_C * r * _softplus(-lam)
    a = jnp.exp(log_a)
    x2 = 2.0 * log_a
    neg_expm1 = jnp.where(x2 > -1e-2, -(x2 + 0.5 * x2 * x2 + x2 * x2 * x2 * (1.0 / 6.0)), 1.0 - jnp.exp(x2))
    u = jnp.sqrt(neg_expm1) * (i * xc)
    return a, u


def _l0_combine(hf, hb, ga, of, ob, g, gain):
    ya = (hf + hb) * _gelu(ga)
    yb = _headnorm(of + ob, gain, 4, 128) * (g * _sig(g))
    return jnp.concatenate([ya, yb], axis=1)


def _l1_combine(of, ob, r, gain):
    return _headnorm(of + ob, gain, 4, 256) * (r * _sig(r))


def _norm_matmul(h, gain, w, name):
    T, D = h.shape
    N = w.shape[1]
    tm = min(512, T)

    def body(h_ref, g_ref, w_ref, o_ref, y_ref):
        x = h_ref[...]
        y = (x * _rms(x) * g_ref[...]).astype(BF16)
        y_ref[...] = y
        o_ref[...] = jnp.dot(y, w_ref[...], preferred_element_type=F32)

    return pl.pallas_call(
        body, name=name, grid=(T // tm,),
        in_specs=[pl.BlockSpec((tm, D), lambda i: (i, 0)), pl.BlockSpec((1, D), lambda i: (0, 0)),
                  pl.BlockSpec((D, N), lambda i: (0, 0))],
        out_specs=[pl.BlockSpec((tm, N), lambda i: (i, 0)), pl.BlockSpec((tm, D), lambda i: (i, 0))],
        out_shape=[jax.ShapeDtypeStruct((T, N), F32), jax.ShapeDtypeStruct((T, D), BF16)],
        compiler_params=_params("parallel"))(h, gain, w)


def _matmul_res(a, w, res, name):
    T, K = a.shape
    N = w.shape[1]
    tm = min(512, T)

    def body(a_ref, w_ref, r_ref, o_ref):
        o_ref[...] = r_ref[...] + jnp.dot(a_ref[...], w_ref[...], preferred_element_type=F32)

    return pl.pallas_call(
        body, name=name, grid=(T // tm,),
        in_specs=[pl.BlockSpec((tm, K), lambda i: (i, 0)), pl.BlockSpec((K, N), lambda i: (0, 0)),
                  pl.BlockSpec((tm, N), lambda i: (i, 0))],
        out_specs=pl.BlockSpec((tm, N), lambda i: (i, 0)),
        out_shape=jax.ShapeDtypeStruct((T, N), F32),
        compiler_params=_params("parallel"))(a, w, res)


def _dgrad_norm(dproj, w, h, gain, dres, name):
    T, N = dproj.shape
    D = w.shape[0]
    tm = min(512, T)

    def body(dp_ref, w_ref, h_ref, g_ref, dr_ref, dh_ref, dhb_ref, dg_ref):
        @pl.when(pl.program_id(0) == 0)
        def _():
            dg_ref[...] = jnp.zeros_like(dg_ref)

        dy = _dg(dp_ref[...], w_ref[...], 1, 1)
        dx, dgain = _rmsnorm_bwd(h_ref[...], g_ref[...], dy)
        dh = dr_ref[...] + dx
        dh_ref[...] = dh
        dhb_ref[...] = dh.astype(BF16)
        dg_ref[...] += dgain

    return pl.pallas_call(
        body, name=name, grid=(T // tm,),
        in_specs=[pl.BlockSpec((tm, N), lambda i: (i, 0)), pl.BlockSpec((D, N), lambda i: (0, 0)),
                  pl.BlockSpec((tm, D), lambda i: (i, 0)), pl.BlockSpec((1, D), lambda i: (0, 0)),
                  pl.BlockSpec((tm, D), lambda i: (i, 0))],
        out_specs=[pl.BlockSpec((tm, D), lambda i: (i, 0)), pl.BlockSpec((tm, D), lambda i: (i, 0)),
                   pl.BlockSpec((1, D), lambda i: (0, 0))],
        out_shape=[jax.ShapeDtypeStruct((T, D), F32), jax.ShapeDtypeStruct((T, D), BF16),
                   jax.ShapeDtypeStruct((1, D), F32)],
        compiler_params=_params("arbitrary"))(dproj, w, h, gain, dres)


def _wgrad(a, b, tn, name, sharded_cols=False, behind=None):
    T, K = a.shape
    N = b.shape[1]
    tk = min(1024, K)

    def body(a_ref, b_ref, *rest):
        rest[-1][...] = _dg(a_ref[...], b_ref[...], 0, 0)

    if sharded_cols:
        out_spec = pl.BlockSpec((None, tk, tn), lambda k, n: (n, k, 0))
        out_shape = jax.ShapeDtypeStruct((N // tn, K, tn), F32)
    else:
        out_spec = pl.BlockSpec((tk, tn), lambda k, n: (k, n))
        out_shape = jax.ShapeDtypeStruct((K, N), F32)
    in_specs = [pl.BlockSpec((T, tk), lambda k, n: (0, k)), pl.BlockSpec((T, tn), lambda k, n: (0, n))]
    args = [a, b]
    if behind is not None:
        in_specs.append(pl.BlockSpec((SUBLANES, LANES), lambda k, n: (0, 0)))
        args.append(behind)
    return pl.pallas_call(
        body, name=name, grid=(K // tk, N // tn), in_specs=in_specs, out_specs=out_spec, out_shape=out_shape,
        compiler_params=_params("parallel", "parallel"))(*args)


def _resident(shape):
    return pl.BlockSpec(shape, lambda i: (0,) * len(shape), pipeline_mode=pl.Buffered(1))


def _mlp_fwd(h, gain, w1g, w2, name):
    T, D = h.shape
    nf, _, tf = w1g.shape
    tm = min(512, T)

    def body(h_ref, g_ref, w1_ref, w2_ref, o_ref, pre_ref, y_ref):
        x = h_ref[...]
        y = (x * _rms(x) * g_ref[...]).astype(BF16)
        y_ref[...] = y
        acc = x
        for j in range(nf):
            cols = slice(j * tf, (j + 1) * tf)
            pre = jnp.dot(y, w1_ref[j], preferred_element_type=F32)
            pre_ref[:, cols] = pre.astype(BF16)
            act = jnp.square(jnp.maximum(pre, 0.0)).astype(BF16)
            acc = acc + jnp.dot(act, w2_ref[cols, :], preferred_element_type=F32)
        o_ref[...] = acc

    return pl.pallas_call(
        body, name=name, grid=(T // tm,),
        in_specs=[pl.BlockSpec((tm, D), lambda i: (i, 0)), pl.BlockSpec((1, D), lambda i: (0, 0)),
                  _resident(w1g.shape), _resident(w2.shape)],
        out_specs=[pl.BlockSpec((tm, D), lambda i: (i, 0)), pl.BlockSpec((tm, nf * tf), lambda i: (i, 0)),
                   pl.BlockSpec((tm, D), lambda i: (i, 0))],
        out_shape=[jax.ShapeDtypeStruct((T, D), F32), jax.ShapeDtypeStruct((T, nf * tf), BF16),
                   jax.ShapeDtypeStruct((T, D), BF16)],
        compiler_params=_params("parallel"))(h, gain, w1g, w2)


def _mlp_bwd(dout, dout_b, h, gain, pre, w1g, w2, name):
    T, D = h.shape
    nf, _, tf = w1g.shape
    tm = min(256, T)

    def body(do_ref, dob_ref, h_ref, g_ref, pre_ref, w1_ref, w2_ref, dh_ref, dhb_ref, dpre_ref, act_ref, dg_ref):
        @pl.when(pl.program_id(0) == 0)
        def _():
            dg_ref[...] = jnp.zeros_like(dg_ref)

        dob = dob_ref[...]
        dy = None
        for j in range(nf):
            cols = slice(j * tf, (j + 1) * tf)
            rp = jnp.maximum(pre_ref[:, cols].astype(F32), 0.0)
            dpre = (_dg(dob, w2_ref[cols, :], 1, 1) * (2.0 * rp)).astype(BF16)
            dpre_ref[:, cols] = dpre
            act_ref[:, cols] = (rp * rp).astype(BF16)
            part = _dg(dpre, w1_ref[j], 1, 1)
            dy = part if dy is None else dy + part
        dx, dgain = _rmsnorm_bwd(h_ref[...], g_ref[...], dy)
        dh = do_ref[...] + dx
        dh_ref[...] = dh
        dhb_ref[...] = dh.astype(BF16)
        dg_ref[...] += dgain

    tok = lambda w: pl.BlockSpec((tm, w), lambda i: (i, 0))
    return pl.pallas_call(
        body, name=name, grid=(T // tm,),
        in_specs=[tok(D), tok(D), tok(D), pl.BlockSpec((1, D), lambda i: (0, 0)), tok(nf * tf),
                  _resident(w1g.shape), _resident(w2.shape)],
        out_specs=[tok(D), tok(D), tok(nf * tf), tok(nf * tf), pl.BlockSpec((1, D), lambda i: (0, 0))],
        out_shape=[jax.ShapeDtypeStruct((T, D), F32), jax.ShapeDtypeStruct((T, D), BF16),
                   jax.ShapeDtypeStruct((T, nf * tf), BF16),
                   jax.ShapeDtypeStruct((T, nf * tf), BF16), jax.ShapeDtypeStruct((1, D), F32)],
        compiler_params=_params("arbitrary"))(dout, dout_b, h, gain, pre, w1g, w2)


def _final_loss(h, gain, target, name):
    T, D = h.shape
    tm = min(512, T)

    def body(h_ref, g_ref, t_ref, l_ref, dh_ref, dhb_ref, dg_ref):
        @pl.when(pl.program_id(0) == 0)
        def _():
            l_ref[...] = jnp.zeros_like(l_ref)
            dg_ref[...] = jnp.zeros_like(dg_ref)

        x = h_ref[...]
        err = x * _rms(x) * g_ref[...] - t_ref[...]
        l_ref[...] += 0.5 * jnp.sum(jnp.mean(err * err, axis=-1, keepdims=True), axis=0, keepdims=True)
        dx, dgain = _rmsnorm_bwd(x, g_ref[...], err * (1.0 / D))
        dh_ref[...] = dx
        dhb_ref[...] = dx.astype(BF16)
        dg_ref[...] += dgain

    return pl.pallas_call(
        body, name=name, grid=(T // tm,),
        in_specs=[pl.BlockSpec((tm, D), lambda i: (i, 0)), pl.BlockSpec((1, D), lambda i: (0, 0)),
                  pl.BlockSpec((tm, D), lambda i: (i, 0))],
        out_specs=[pl.BlockSpec((SUBLANES, LANES), lambda i: (0, 0)), pl.BlockSpec((tm, D), lambda i: (i, 0)),
                   pl.BlockSpec((tm, D), lambda i: (i, 0)), pl.BlockSpec((1, D), lambda i: (0, 0))],
        out_shape=[jax.ShapeDtypeStruct((SUBLANES, LANES), F32), jax.ShapeDtypeStruct((T, D), F32),
                   jax.ShapeDtypeStruct((T, D), BF16), jax.ShapeDtypeStruct((1, D), F32)],
        compiler_params=_params("arbitrary"))(h, gain, target)


def _halo_specs(tm, T, width, col, tile=lambda i: i):
    r8 = tm // SUBLANES
    nb8 = T // SUBLANES
    return [pl.BlockSpec((tm, width), lambda i: (tile(i), col)),
            pl.BlockSpec((SUBLANES, width), lambda i: (jnp.maximum(tile(i) * r8 - 1, 0), col)),
            pl.BlockSpec((SUBLANES, width), lambda i: (jnp.minimum((tile(i) + 1) * r8, nb8 - 1), col))]


def _ext(cur, prev, nxt, has_prev, has_next):
    return jnp.concatenate([jnp.where(has_prev, prev, 0.0), cur, jnp.where(has_next, nxt, 0.0)], axis=0)


def _shifted(ext, offset, tm):
    n = ext.shape[0]
    sh = (-offset) % n
    r = ext if sh == 0 else pltpu.roll(ext, sh, 0)
    return r[SUBLANES:SUBLANES + tm]


def _rg_conv_fwd(proj, cw8, cb, name):
    T = proj.shape[0]
    tm = min(512, T)
    nT = T // tm

    def body(cur_ref, prev_ref, next_ref, w_ref, b_ref, o_ref):
        i = pl.program_id(0)
        ext = _ext(cur_ref[...], prev_ref[...], next_ref[...], i > 0, i < nT - 1)
        acc = jnp.broadcast_to(b_ref[...], (tm, RG_W))
        for k in range(4):
            acc = acc + w_ref[k:k + 1, :] * _shifted(ext, k - 2, tm)
        o_ref[...] = acc

    return pl.pallas_call(
        body, name=name, grid=(nT,),
        in_specs=_halo_specs(tm, T, RG_W, 0) + [pl.BlockSpec((SUBLANES, RG_W), lambda i: (0, 0)),
                                                pl.BlockSpec((1, RG_W), lambda i: (0, 0))],
        out_specs=pl.BlockSpec((tm, RG_W), lambda i: (i, 0)),
        out_shape=jax.ShapeDtypeStruct((T, RG_W), F32),
        compiler_params=_params("parallel"))(proj, proj, proj, cw8, cb)


def _rg_conv_bwd(dxc, proj, cw8, name):
    T = proj.shape[0]
    tm = min(512, T)
    nT = T // tm

    def body(a0, p0, n0, a1, p1, n1, xa, xp, xn, w_ref, dxa_ref, dw_ref, db_ref):
        i = pl.program_id(0)

        @pl.when(i == 0)
        def _():
            dw_ref[...] = jnp.zeros_like(dw_ref)
            db_ref[...] = jnp.zeros_like(db_ref)

        has_p, has_n = i > 0, i < nT - 1
        cur = a0[...] + a1[...]
        dext = _ext(cur, p0[...] + p1[...], n0[...] + n1[...], has_p, has_n)
        xext = _ext(xa[...], xp[...], xn[...], has_p, has_n)
        acc = jnp.zeros((tm, RG_W), F32)
        rows = []
        for k in range(4):
            acc = acc + w_ref[k:k + 1, :] * _shifted(dext, 2 - k, tm)
            rows.append(jnp.sum(cur * _shifted(xext, k - 2, tm), axis=0, keepdims=True))
        dxa_ref[...] = acc
        dw_ref[...] += jnp.concatenate(rows + [jnp.zeros((4, RG_W), F32)], axis=0)
        db_ref[...] += jnp.sum(cur, axis=0, keepdims=True)

    return pl.pallas_call(
        body, name=name, grid=(nT,),
        in_specs=(_halo_specs(tm, T, RG_W, 0) + _halo_specs(tm, T, RG_W, 0)
                  + _halo_specs(tm, T, RG_W, 0) + [pl.BlockSpec((SUBLANES, RG_W), lambda i: (0, 0))]),
        out_specs=[pl.BlockSpec((tm, RG_W), lambda i: (i, 0)), pl.BlockSpec((SUBLANES, RG_W), lambda i: (0, 0)),
                   pl.BlockSpec((1, RG_W), lambda i: (0, 0))],
        out_shape=[jax.ShapeDtypeStruct((T, RG_W), F32), jax.ShapeDtypeStruct((SUBLANES, RG_W), F32),
                   jax.ShapeDtypeStruct((1, RG_W), F32)],
        compiler_params=_params("arbitrary"))(dxc[0], dxc[0], dxc[0], dxc[1], dxc[1], dxc[1], proj, proj, proj, cw8)


def _local_scan(a, b, ascending):
    n = a.shape[0]
    pos = jnp.bitwise_and(lax.broadcasted_iota(jnp.int32, a.shape, 0), SUBLANES - 1)
    for s in (1, 2, 4):
        sh = s if ascending else n - s
        ok = (pos >= s) if ascending else (pos < SUBLANES - s)
        a_sh, b_sh = pltpu.roll(a, sh, 0), pltpu.roll(b, sh, 0)
        b = jnp.where(ok, a * b_sh + b, b)
        a = jnp.where(ok, a * a_sh, a)
    return a, b


def _group_scan(chains, a_sc, b_sc, carry, n_groups):
    def step(g, hs):
        new = []
        for (d, out_ref, asc), h in zip(chains, hs):
            r0 = pl.multiple_of((g if asc else n_groups - 1 - g) * SUBLANES, SUBLANES)
            out_ref[pl.ds(r0, SUBLANES), :] = a_sc[d, pl.ds(r0, SUBLANES), :] * h + b_sc[d, pl.ds(r0, SUBLANES), :]
            new.append(out_ref[pl.ds(r0 + (SUBLANES - 1 if asc else 0), 1), :])
        return tuple(new)

    hs = lax.fori_loop(0, n_groups, step, tuple(carry[d, 0:1, :] for d, _, _ in chains))
    for (d, _, _), h in zip(chains, hs):
        carry[d, 0:1, :] = h


def _rg_scan_fwd(xc, wbd, bias, lam, name):
    T = xc.shape[0]
    tm = min(512, T)
    nT = T // tm

    def body(xf_ref, xb_ref, w_ref, b_ref, lam_ref, hf_ref, hb_ref, a_sc, b_sc, carry):
        @pl.when(pl.program_id(0) == 0)
        def _():
            carry[...] = jnp.zeros_like(carry)

        for d, x_ref in enumerate((xf_ref, xb_ref)):
            a, u = _rg_gates(x_ref[...], w_ref[d], b_ref[d], lam_ref[d])
            a_sc[d], b_sc[d] = _local_scan(a, u, d == 0)
        _group_scan(((0, hf_ref, True), (1, hb_ref, False)), a_sc, b_sc, carry, tm // SUBLANES)

    full = lambda a: pl.BlockSpec(a.shape, lambda i: (0,) * len(a.shape))
    res = pl.pallas_call(
        body, name=name, grid=(nT,),
        in_specs=[pl.BlockSpec((tm, RG_W), lambda i: (i, 0)), pl.BlockSpec((tm, RG_W), lambda i: (nT - 1 - i, 0)),
                  full(wbd), full(bias), full(lam)],
        out_specs=[pl.BlockSpec((tm, RG_W), lambda i: (i, 0)), pl.BlockSpec((tm, RG_W), lambda i: (nT - 1 - i, 0))],
        out_shape=[jax.ShapeDtypeStruct((T, RG_W), F32)] * 2,
        scratch_shapes=[pltpu.VMEM((2, tm, RG_W), F32), pltpu.VMEM((2, tm, RG_W), F32),
                        pltpu.VMEM((2, SUBLANES, RG_W), F32)],
        compiler_params=_params("arbitrary"))(xc, xc, wbd, bias, lam)
    return res[0], res[1]


def _rg_scan_bwd(xc, wbd, bias, lam, hs, dho, name):
    T = xc.shape[0]
    tm = min(256, T)
    nT = T // tm
    tiles = (lambda i: nT - 1 - i, lambda i: i)

    def body(xf_ref, xb_ref, w_ref, b_ref, lam_ref, hfc, hfp, hfn, hbc, hbp, hbn, dof_ref, dob_ref,
             dxf_ref, dxb_ref, dw_ref, db_ref, dlam_ref, a_sc, b_sc, y_sc, carry):
        i = pl.program_id(0)

        @pl.when(i == 0)
        def _():
            carry[...] = jnp.zeros_like(carry)
            dw_ref[...] = jnp.zeros_like(dw_ref)
            db_ref[...] = jnp.zeros_like(db_ref)
            dlam_ref[...] = jnp.zeros_like(dlam_ref)

        vjps, entering = [], []
        for d, (x_ref, do_ref) in enumerate(((xf_ref, dof_ref), (xb_ref, dob_ref))):
            (a, _), vjp = jax.vjp(_rg_gates, x_ref[...], w_ref[d].astype(F32), b_ref[d], lam_ref[d])
            vjps.append(vjp)
            entering.append(carry[d, 0:1, :])
            a_sc[d], b_sc[d] = _local_scan(a, a * do_ref[...], d == 1)
        _group_scan(((0, y_sc.at[0], False), (1, y_sc.at[1], True)), a_sc, b_sc, carry, tm // SUBLANES)

        row = lax.broadcasted_iota(jnp.int32, (tm, RG_W), 0)
        for d, (do_ref, dx_ref, hc, hp, hn, ti) in enumerate(
                ((dof_ref, dxf_ref, hfc, hfp, hfn, nT - 1 - i), (dob_ref, dxb_ref, hbc, hbp, hbn, i))):
            y = y_sc[d]
            if d == 0:
                y_next = jnp.where(row == tm - 1, entering[d], pltpu.roll(y, tm - 1, 0))
            else:
                y_next = jnp.where(row == 0, entering[d], pltpu.roll(y, 1, 0))
            dtot = do_ref[...] + y_next
            ext = _ext(hc[...], hp[...], hn[...], ti > 0, ti < nT - 1)
            hprev = _shifted(ext, -1 if d == 0 else 1, tm)
            dxc, dw, db, dlam = vjps[d]((dtot * hprev, dtot))
            dx_ref[...] = dxc
            dw_ref[d] += dw
            db_ref[d] += db
            dlam_ref[d] += dlam

    full = lambda a: pl.BlockSpec(a.shape, lambda i: (0,) * len(a.shape))
    tok = lambda d: pl.BlockSpec((tm, RG_W), lambda i: (tiles[d](i), 0))
    acc_shapes = [jax.ShapeDtypeStruct((2, RG_W, 2 * RG_W), F32), jax.ShapeDtypeStruct((2, 1, 2 * RG_W), F32),
                  jax.ShapeDtypeStruct((2, 1, RG_W), F32)]
    res = pl.pallas_call(
        body, name=name, grid=(nT,),
        in_specs=([tok(0), tok(1), full(wbd), full(bias), full(lam)]
                  + _halo_specs(tm, T, RG_W, 0, tiles[0]) + _halo_specs(tm, T, RG_W, 0, tiles[1]) + [tok(0), tok(1)]),
        out_specs=[tok(0), tok(1)] + [full(s) for s in acc_shapes],
        out_shape=[jax.ShapeDtypeStruct((T, RG_W), F32)] * 2 + acc_shapes,
        scratch_shapes=[pltpu.VMEM((2, tm, RG_W), F32), pltpu.VMEM((2, tm, RG_W), F32),
                        pltpu.VMEM((2, tm, RG_W), F32), pltpu.VMEM((2, SUBLANES, RG_W), F32)],
        compiler_params=_params("arbitrary"))(xc, xc, wbd, bias, lam, hs[0], hs[0], hs[0], hs[1], hs[1], hs[1],
                                              dho, dho)
    return (res[0], res[1]), res[2], res[3], res[4]


def _chunk_rows(n_chunks, reverse):
    up, down = (lambda c: c), (lambda c: n_chunks - 1 - c)
    return (down, up) if reverse else (up, down)


STEP_CHUNKS = 4
STEP_ROWS = STEP_CHUNKS * CHUNK


def _sub_chunks(ascending):
    order = range(STEP_CHUNKS) if ascending else range(STEP_CHUNKS - 1, -1, -1)
    return [(s, slice(s * CHUNK, (s + 1) * CHUNK)) for s in order]


def _hg_fwd(proj, l0, l1, name):
    T = proj.shape[0]
    nC = T // CHUNK
    nS = nC // STEP_CHUNKS
    H, dk, dv = 4, 128, 128
    rows = _chunk_rows(nS, False)

    def body(qf, ff, vf, qb, fb, vb, l0_ref, l1_ref, of, ob, spf, spb, st):
        @pl.when(pl.program_id(0) == 0)
        def _():
            st[...] = jnp.zeros_like(st)

        for d, (q, f, v, o, sp) in enumerate(((qf, ff, vf, of, spf), (qb, fb, vb, ob, spb))):
            tri, tri_t, mref = _tri_consts(d)
            stp = tuple(st[d, h] for h in range(H))
            for s, r in _sub_chunks(d == 0):
                for h in range(H):
                    sp[s, h] = stp[h]
                o_val, stp = _hg_chunk(q[r, :], f[r, :], v[r, :], l0_ref[...], l1_ref[...], stp, tri, tri_t, mref)
                o[r, :] = o_val
            for h in range(H):
                st[d, h] = stp[h]

    tok = lambda d, col: pl.BlockSpec((STEP_ROWS, HG_W), lambda c: (rows[d](c), col))
    par = pl.BlockSpec((1, HG_W), lambda c: (0, 0))
    state = lambda d: pl.BlockSpec((STEP_CHUNKS, H, dv, dk), lambda c: (rows[d](c), 0, 0, 0))
    res = pl.pallas_call(
        body, name=name, grid=(nS,),
        in_specs=[tok(0, 2), tok(0, 3), tok(0, 5), tok(1, 2), tok(1, 4), tok(1, 5), par, par],
        out_specs=[tok(0, 0), tok(1, 0), state(0), state(1)],
        out_shape=[jax.ShapeDtypeStruct((T, H * dv), F32)] * 2 + [jax.ShapeDtypeStruct((nC, H, dv, dk), F32)] * 2,
        scratch_shapes=[pltpu.VMEM((2, H, dv, dk), F32)],
        compiler_params=_params("arbitrary"))(proj, proj, proj, proj, proj, proj, l0, l1)
    return (res[0], res[1]), (res[2], res[3])


def _hg_bwd(proj, l0, l1, sprev, do, name):
    T = proj.shape[0]
    nC = T // CHUNK
    nS = nC // STEP_CHUNKS
    H, dk, dv = 4, 128, 128
    rows = _chunk_rows(nS, True)

    def body(qf, ff, vf, qb, fb, vb, l0_ref, l1_ref, spf, spb, dof, dob,
             dqf, dff, dvf, dqb, dfb, dvb, dl0_ref, dl1_ref, dst):
        @pl.when(pl.program_id(0) == 0)
        def _():
            dst[...] = jnp.zeros_like(dst)
            dl0_ref[...] = jnp.zeros_like(dl0_ref)
            dl1_ref[...] = jnp.zeros_like(dl1_ref)

        for d, (q, f, v, sp, do_ref, dq_ref, df_ref, dv_ref) in enumerate(
                ((qf, ff, vf, spf, dof, dqf, dff, dvf), (qb, fb, vb, spb, dob, dqb, dfb, dvb))):
            tri, tri_t, mref = _tri_consts(d)
            fn = lambda q_, f_, v_, a0, a1, stp: _hg_chunk(q_, f_, v_, a0, a1, stp, tri, tri_t, mref)
            dstp = tuple(dst[d, h] for h in range(H))
            for s, r in _sub_chunks(d == 1):
                stp = tuple(sp[s, h] for h in range(H))
                _, vjp = jax.vjp(fn, q[r, :], f[r, :], v[r, :], l0_ref[...], l1_ref[...], stp)
                dq, df, dvv, dl0, dl1, dstp = vjp((do_ref[r, :], dstp))
                dq_ref[r, :] = dq.astype(BF16)
                df_ref[r, :] = df.astype(BF16)
                dv_ref[r, :] = dvv.astype(BF16)
                dl0_ref[d] += dl0
                dl1_ref[d] += dl1
            for h in range(H):
                dst[d, h] = dstp[h]

    tok = lambda d, col: pl.BlockSpec((STEP_ROWS, HG_W), lambda c: (rows[d](c), col))
    par = pl.BlockSpec((1, HG_W), lambda c: (0, 0))
    acc = pl.BlockSpec((2, 1, HG_W), lambda c: (0, 0, 0))
    state = lambda d: pl.BlockSpec((STEP_CHUNKS, H, dv, dk), lambda c: (rows[d](c), 0, 0, 0))
    res = pl.pallas_call(
        body, name=name, grid=(nS,),
        in_specs=[tok(0, 2), tok(0, 3), tok(0, 5), tok(1, 2), tok(1, 4), tok(1, 5), par, par,
                  state(0), state(1), tok(0, 0), tok(1, 0)],
        out_specs=[tok(0, 0)] * 3 + [tok(1, 0)] * 3 + [acc, acc],
        out_shape=[jax.ShapeDtypeStruct((T, HG_W), BF16)] * 6 + [jax.ShapeDtypeStruct((2, 1, HG_W), F32)] * 2,
        scratch_shapes=[pltpu.VMEM((2, H, dv, dk), F32)],
        compiler_params=_params("arbitrary"))(proj, proj, proj, proj, proj, proj, l0, l1, sprev[0], sprev[1], do, do)
    return (res[0], res[3]), (res[1], res[4]), (res[2], res[5]), res[6], res[7]


def _gate_logits(proj, wup, bg, name):
    T = proj.shape[0]
    tm = min(512, T)

    def body(lr_ref, w_ref, b_ref, z_ref, lrb_ref):
        lr = lr_ref[...].astype(BF16)
        lrb_ref[...] = lr
        for d in range(2):
            z_ref[d] = _dg(lr, w_ref[d], 1, 0) + b_ref[d]

    return pl.pallas_call(
        body, name=name, grid=(T // tm,),
        in_specs=[pl.BlockSpec((tm, LANES), lambda i: (i, 24)), pl.BlockSpec((2, LANES, 512), lambda i: (0, 0, 0)),
                  pl.BlockSpec((2, 1, 512), lambda i: (0, 0, 0))],
        out_specs=[pl.BlockSpec((2, tm, 512), lambda i: (0, i, 0)), pl.BlockSpec((tm, LANES), lambda i: (i, 0))],
        out_shape=[jax.ShapeDtypeStruct((2, T, 512), F32), jax.ShapeDtypeStruct((T, LANES), BF16)],
        compiler_params=_params("parallel"))(proj, wup, bg)


def _gate_logits_bwd(dz, wup, name):
    T = dz[0].shape[0]
    tm = min(512, T)

    def body(dzf_ref, dzb_ref, w_ref, dlr_ref, db_ref, dzb16_ref):
        @pl.when(pl.program_id(0) == 0)
        def _():
            db_ref[...] = jnp.zeros_like(db_ref)

        acc = jnp.zeros((tm, LANES), F32)
        for d, dz_ref in enumerate((dzf_ref, dzb_ref)):
            g = dz_ref[...]
            gb = g.astype(BF16)
            dzb16_ref[d] = gb
            acc = acc + _dg(gb, w_ref[d], 1, 1)
            db_ref[d] += jnp.sum(g, axis=0, keepdims=True)
        dlr_ref[...] = acc

    tok = pl.BlockSpec((tm, 512), lambda i: (i, 0))
    return pl.pallas_call(
        body, name=name, grid=(T // tm,),
        in_specs=[tok, tok, pl.BlockSpec((2, LANES, 512), lambda i: (0, 0, 0))],
        out_specs=[pl.BlockSpec((tm, LANES), lambda i: (i, 0)), pl.BlockSpec((2, 1, 512), lambda i: (0, 0, 0)),
                   pl.BlockSpec((2, tm, 512), lambda i: (0, i, 0))],
        out_shape=[jax.ShapeDtypeStruct((T, LANES), F32), jax.ShapeDtypeStruct((2, 1, 512), F32),
                   jax.ShapeDtypeStruct((2, T, 512), BF16)],
        compiler_params=_params("arbitrary"))(dz[0], dz[1], wup)


def _gla_fwd(proj, z, name):
    T = proj.shape[0]
    nC = T // CHUNK
    nS = nC // STEP_CHUNKS
    H, dk, dv = 4, 128, 256
    rows = _chunk_rows(nS, False)

    def body(qf, kf, vf, zf, qb, kb, vb, zb, of, ob, spf, spb, st):
        @pl.when(pl.program_id(0) == 0)
        def _():
            st[...] = jnp.zeros_like(st)

        for d, (q, k, v, z_ref, o, sp) in enumerate(((qf, kf, vf, zf, of, spf), (qb, kb, vb, zb, ob, spb))):
            tri, tri_t, mref = _tri_consts(d)
            stp = tuple(st[d, h] for h in range(H))
            for s, r in _sub_chunks(d == 0):
                for h in range(H):
                    sp[s, h] = stp[h]
                o_val, stp = _gla_chunk(q[r, :], k[r, :], v[r, :], z_ref[r, :], stp, tri, tri_t, mref)
                o[r, :] = o_val
            for h in range(H):
                st[d, h] = stp[h]

    tok = lambda d, w, col: pl.BlockSpec((STEP_ROWS, w), lambda c: (rows[d](c), col))
    gate = lambda d: pl.BlockSpec((None, STEP_ROWS, 512), lambda c: (d, rows[d](c), 0))
    state = lambda d: pl.BlockSpec((STEP_CHUNKS, H, dv, dk), lambda c: (rows[d](c), 0, 0, 0))
    res = pl.pallas_call(
        body, name=name, grid=(nS,),
        in_specs=[tok(0, 512, 0), tok(0, 512, 1), tok(0, 1024, 1), gate(0),
                  tok(1, 512, 0), tok(1, 512, 1), tok(1, 1024, 1), gate(1)],
        out_specs=[tok(0, H * dv, 0), tok(1, H * dv, 0), state(0), state(1)],
        out_shape=[jax.ShapeDtypeStruct((T, H * dv), F32)] * 2 + [jax.ShapeDtypeStruct((nC, H, dv, dk), F32)] * 2,
        scratch_shapes=[pltpu.VMEM((2, H, dv, dk), F32)],
        compiler_params=_params("arbitrary"))(proj, proj, proj, z, proj, proj, proj, z)
    return (res[0], res[1]), (res[2], res[3])


def _gla_bwd(proj, z, sprev, do, name):
    T = proj.shape[0]
    nC = T // CHUNK
    nS = nC // STEP_CHUNKS
    H, dk, dv = 4, 128, 256
    rows = _chunk_rows(nS, True)

    def body(qf, kf, vf, zf, qb, kb, vb, zb, spf, spb, dof, dob,
             dqf, dkf, dvf, dzf, dqb, dkb, dvb, dzb, dst):
        @pl.when(pl.program_id(0) == 0)
        def _():
            dst[...] = jnp.zeros_like(dst)

        for d, (q, k, v, z_ref, sp, do_ref, dq_ref, dk_ref, dv_ref, dz_ref) in enumerate(
                ((qf, kf, vf, zf, spf, dof, dqf, dkf, dvf, dzf), (qb, kb, vb, zb, spb, dob, dqb, dkb, dvb, dzb))):
            tri, tri_t, mref = _tri_consts(d)
            fn = lambda q_, k_, v_, z_, stp: _gla_chunk(q_, k_, v_, z_, stp, tri, tri_t, mref)
            dstp = tuple(dst[d, h] for h in range(H))
            for s, r in _sub_chunks(d == 1):
                stp = tuple(sp[s, h] for h in range(H))
                _, vjp = jax.vjp(fn, q[r, :], k[r, :], v[r, :], z_ref[r, :], stp)
                dq, dkk, dvv, dzz, dstp = vjp((do_ref[r, :], dstp))
                dq_ref[r, :] = dq.astype(BF16)
                dk_ref[r, :] = dkk.astype(BF16)
                dv_ref[r, :] = dvv.astype(BF16)
                dz_ref[r, :] = dzz
            for h in range(H):
                dst[d, h] = dstp[h]

    tok = lambda d, w, col: pl.BlockSpec((STEP_ROWS, w), lambda c: (rows[d](c), col))
    gate = lambda d: pl.BlockSpec((None, STEP_ROWS, 512), lambda c: (d, rows[d](c), 0))
    state = lambda d: pl.BlockSpec((STEP_CHUNKS, H, dv, dk), lambda c: (rows[d](c), 0, 0, 0))
    outs = lambda d: [tok(d, 512, 0), tok(d, 512, 0), tok(d, 1024, 0), tok(d, 512, 0)]
    shapes = [jax.ShapeDtypeStruct((T, 512), BF16), jax.ShapeDtypeStruct((T, 512), BF16),
              jax.ShapeDtypeStruct((T, 1024), BF16), jax.ShapeDtypeStruct((T, 512), F32)]
    res = pl.pallas_call(
        body, name=name, grid=(nS,),
        in_specs=[tok(0, 512, 0), tok(0, 512, 1), tok(0, 1024, 1), gate(0),
                  tok(1, 512, 0), tok(1, 512, 1), tok(1, 1024, 1), gate(1),
                  state(0), state(1), tok(0, H * dv, 0), tok(1, H * dv, 0)],
        out_specs=outs(0) + outs(1), out_shape=shapes + shapes,
        scratch_shapes=[pltpu.VMEM((2, H, dv, dk), F32)],
        compiler_params=_params("arbitrary"))(proj, proj, proj, z, proj, proj, proj, z, sprev[0], sprev[1], do, do)
    return (res[0], res[4]), (res[1], res[5]), (res[2], res[6]), (res[3], res[7])


def _l0_combine_fwd(hs, proj, o, gain, name):
    T = proj.shape[0]
    tm = min(512, T)

    def body(hf, hb, ga, of, ob, g, gn, out):
        out[...] = _l0_combine(hf[...], hb[...], ga[...], of[...], ob[...], g[...], gn[...]).astype(BF16)

    tok = pl.BlockSpec((tm, 512), lambda i: (i, 0))
    return pl.pallas_call(
        body, name=name, grid=(T // tm,),
        in_specs=[tok, tok, pl.BlockSpec((tm, 512), lambda i: (i, 1)), tok, tok,
                  pl.BlockSpec((tm, 512), lambda i: (i, 6)), pl.BlockSpec((1, 512), lambda i: (0, 0))],
        out_specs=pl.BlockSpec((tm, 1024), lambda i: (i, 0)),
        out_shape=jax.ShapeDtypeStruct((T, 1024), BF16),
        compiler_params=_params("parallel"))(hs[0], hs[1], proj, o[0], o[1], proj, gain)


def _l0_combine_bwd(hs, proj, o, gain, dh_b, w_out, name):
    T = proj.shape[0]
    tm = min(512, T)

    def body(hf, hb, ga, of, ob, g, gn, dhb_ref, w_ref, dho_ref, dga_ref, do_ref, dg_ref, dgn_ref):
        @pl.when(pl.program_id(0) == 0)
        def _():
            dgn_ref[...] = jnp.zeros_like(dgn_ref)

        _, vjp = jax.vjp(_l0_combine, hf[...], hb[...], ga[...], of[...], ob[...], g[...], gn[...])
        dhf, _, dga, dof, _, dg, dgn = vjp(_dg(dhb_ref[...], w_ref[...], 1, 1))
        dho_ref[...] = dhf
        dga_ref[...] = dga
        do_ref[...] = dof
        dg_ref[...] = dg
        dgn_ref[...] += dgn

    tok = lambda: pl.BlockSpec((tm, 512), lambda i: (i, 0))
    return pl.pallas_call(
        body, name=name, grid=(T // tm,),
        in_specs=[tok(), tok(), pl.BlockSpec((tm, 512), lambda i: (i, 1)), tok(), tok(),
                  pl.BlockSpec((tm, 512), lambda i: (i, 6)), pl.BlockSpec((1, 512), lambda i: (0, 0)),
                  pl.BlockSpec((tm, D_MODEL), lambda i: (i, 0)), pl.BlockSpec(w_out.shape, lambda i: (0, 0))],
        out_specs=[tok(), tok(), tok(), tok(), pl.BlockSpec((1, 512), lambda i: (0, 0))],
        out_shape=[jax.ShapeDtypeStruct((T, 512), F32)] * 4 + [jax.ShapeDtypeStruct((1, 512), F32)],
        compiler_params=_params("arbitrary"))(hs[0], hs[1], proj, o[0], o[1], proj, gain, dh_b, w_out)


def _l0_assemble(dxa, dga, dq, df, dv, dg, name):
    T = dxa.shape[0]
    tm = min(512, T)

    def body(xa, ga, q0, q1, f0, f1, v0, v1, g, out):
        both = lambda a, b: (a[...].astype(F32) + b[...].astype(F32)).astype(BF16)
        out[...] = jnp.concatenate([xa[...].astype(BF16), ga[...].astype(BF16), both(q0, q1), f0[...], f1[...],
                                    both(v0, v1), g[...].astype(BF16)], axis=1)

    tok = lambda: pl.BlockSpec((tm, 512), lambda i: (i, 0))
    return pl.pallas_call(
        body, name=name, grid=(T // tm,),
        in_specs=[tok() for _ in range(9)],
        out_specs=pl.BlockSpec((tm, AB_IN), lambda i: (i, 0)),
        out_shape=jax.ShapeDtypeStruct((T, AB_IN), BF16),
        compiler_params=_params("parallel"))(dxa, dga, dq[0], dq[1], df[0], df[1], dv[0], dv[1], dg)


def _l1_combine_fwd(o, proj, gain, name):
    T = proj.shape[0]
    tm = min(512, T)

    def body(of, ob, r, gn, out):
        out[...] = _l1_combine(of[...], ob[...], r[...], gn[...]).astype(BF16)

    tok = pl.BlockSpec((tm, 1024), lambda i: (i, 0))
    return pl.pallas_call(
        body, name=name, grid=(T // tm,),
        in_specs=[tok, tok, pl.BlockSpec((tm, 1024), lambda i: (i, 2)), pl.BlockSpec((1, 1024), lambda i: (0, 0))],
        out_specs=pl.BlockSpec((tm, 1024), lambda i: (i, 0)),
        out_shape=jax.ShapeDtypeStruct((T, 1024), BF16),
        compiler_params=_params("parallel"))(o[0], o[1], proj, gain)


def _l1_combine_bwd(o, proj, gain, dh_b, w_out, name):
    T = proj.shape[0]
    tm = min(512, T)

    def body(of, ob, r, gn, dhb_ref, w_ref, do_ref, dr_ref, dgn_ref):
        @pl.when(pl.program_id(0) == 0)
        def _():
            dgn_ref[...] = jnp.zeros_like(dgn_ref)

        _, vjp = jax.vjp(_l1_combine, of[...], ob[...], r[...], gn[...])
        dof, _, dr, dgn = vjp(_dg(dhb_ref[...], w_ref[...], 1, 1))
        do_ref[...] = dof
        dr_ref[...] = dr
        dgn_ref[...] += dgn

    tok = lambda: pl.BlockSpec((tm, 1024), lambda i: (i, 0))
    return pl.pallas_call(
        body, name=name, grid=(T // tm,),
        in_specs=[tok(), tok(), pl.BlockSpec((tm, 1024), lambda i: (i, 2)),
                  pl.BlockSpec((1, 1024), lambda i: (0, 0)), tok(), pl.BlockSpec(w_out.shape, lambda i: (0, 0))],
        out_specs=[tok(), tok(), pl.BlockSpec((1, 1024), lambda i: (0, 0))],
        out_shape=[jax.ShapeDtypeStruct((T, 1024), F32)] * 2 + [jax.ShapeDtypeStruct((1, 1024), F32)],
        compiler_params=_params("arbitrary"))(o[0], o[1], proj, gain, dh_b, w_out)


def _l1_assemble(dq, dk, dv, dr, dlr, name):
    T = dr.shape[0]
    tm = min(512, T)

    def body(q0, q1, k0, k1, v0, v1, r, a, out):
        both = lambda x, y: (x[...].astype(F32) + y[...].astype(F32)).astype(BF16)
        out[...] = jnp.concatenate([both(q0, q1), both(k0, k1), both(v0, v1), r[...].astype(BF16),
                                    a[...].astype(BF16)], axis=1)

    tok = lambda w: pl.BlockSpec((tm, w), lambda i: (i, 0))
    return pl.pallas_call(
        body, name=name, grid=(T // tm,),
        in_specs=[tok(512), tok(512), tok(512), tok(512), tok(1024), tok(1024), tok(1024), tok(LANES)],
        out_specs=pl.BlockSpec((tm, GLA_IN_PAD), lambda i: (i, 0)),
        out_shape=jax.ShapeDtypeStruct((T, GLA_IN_PAD), BF16),
        compiler_params=_params("parallel"))(dq[0], dq[1], dk[0], dk[1], dv[0], dv[1], dr, dlr)


HBM_SPEC = pl.BlockSpec(memory_space=pltpu.HBM)


def _place():
    x, y, c = lax.axis_index("x"), lax.axis_index("y"), lax.axis_index("c")
    return x, y, c


def _allgather_vmem(x_shard, name):
    m_per, n = x_shard.shape

    def body(x_ref, out_ref, send_sems, recv_sems, local_sem):
        x, y, c = _place()
        me, sibling = (x, y, c), (x, y, 1 - c)
        chips = [(1 - x, y), (x, 1 - y), (1 - x, 1 - y)]

        def rows(px, py, pc):
            return out_ref.at[pl.ds((4 * px + 2 * py + pc) * m_per, m_per), :]

        def copy(k, block, to, src=None):
            return pltpu.make_async_remote_copy(
                src_ref=rows(*block) if src is None else src, dst_ref=rows(*block),
                send_sem=send_sems.at[k], recv_sem=recv_sems.at[k], device_id=to, device_id_type=MESH)

        mine = pltpu.make_async_copy(x_ref, rows(*me), local_sem)
        mine.start()
        first = [copy(0, me, sibling, src=x_ref)]
        first += [copy(1 + j, me, (*chip, c), src=x_ref) for j, chip in enumerate(chips)]
        for cp in first:
            cp.start()
        passed = [copy(4 + j, (*chip, c), sibling) for j, chip in enumerate(chips)]
        for j, chip in enumerate(chips):
            copy(1 + j, (*chip, c), me).wait_recv()
            passed[j].start()
        copy(0, sibling, me).wait_recv()
        for j, chip in enumerate(chips):
            copy(4 + j, (*chip, 1 - c), me).wait_recv()
        for cp in first + passed:
            cp.wait_send()
        mine.wait()

    vm = pl.BlockSpec(memory_space=pltpu.VMEM)
    return pl.pallas_call(
        body, name=name, in_specs=[vm], out_specs=vm,
        out_shape=jax.ShapeDtypeStruct((N_DEV * m_per, n), x_shard.dtype),
        scratch_shapes=[pltpu.SemaphoreType.DMA((7,)), pltpu.SemaphoreType.DMA((7,)), pltpu.SemaphoreType.DMA],
        compiler_params=pltpu.CompilerParams(has_side_effects=True, vmem_limit_bytes=VMEM_LIMIT))(x_shard)


SEM_SPEC = pl.BlockSpec(memory_space=pltpu.SEMAPHORE)
DATAFLOW_EFFECT = pltpu.SideEffectType.DATAFLOW_SIDE_EFFECTING


def _copies(plan, srcs, lands, send_sems, recv_sems):
    x, y, c = _place()
    return [pltpu.make_async_remote_copy(src_ref=s, dst_ref=d, send_sem=send_sems.at[k], recv_sem=recv_sems.at[k],
                                         device_id=dev, device_id_type=MESH)
            for k, (s, d, dev) in enumerate(plan(srcs, lands, x, y, c))]


def _copies_start(plan, n_copies, srcs, lands, name):
    ns, nl = len(srcs), len(lands)

    def body(*refs):
        send_sems, recv_sems = refs[ns + nl], refs[ns + nl + 1]
        for cp in _copies(plan, refs[:ns], refs[ns:ns + nl], send_sems, recv_sems):
            cp.start()
        refs[-1][...] = jnp.zeros_like(refs[-1])

    arrays = list(srcs) + list(lands)
    res = pl.pallas_call(
        body, name=name,
        in_specs=[HBM_SPEC] * (ns + nl),
        out_specs=tuple([SEM_SPEC, SEM_SPEC] + [HBM_SPEC] * (ns + nl) + [pl.BlockSpec(memory_space=pltpu.VMEM)]),
        out_shape=tuple([pltpu.SemaphoreType.DMA((n_copies,)), pltpu.SemaphoreType.DMA((n_copies,))]
                        + [pltpu.HBM(a.shape, a.dtype) for a in arrays]
                        + [jax.ShapeDtypeStruct((SUBLANES, LANES), F32)]),
        input_output_aliases={i: 2 + i for i in range(ns + nl)},
        compiler_params=pltpu.CompilerParams(has_side_effects=DATAFLOW_EFFECT),
    )(*[pltpu.with_memory_space_constraint(a, pltpu.HBM) for a in arrays])
    return res[0], res[1], list(res[2:2 + ns]), list(res[2 + ns:2 + ns + nl]), res[-1]


def _copies_wait(plan, started, after, name):
    send_sems, recv_sems, srcs, lands, _ = started
    ns, nl = len(srcs), len(lands)

    def body(*refs):
        for cp in _copies(plan, refs[:ns], refs[ns:ns + nl], refs[ns + nl], refs[ns + nl + 1]):
            cp.wait_send()
            cp.wait_recv()

    arrays = list(srcs) + list(lands)
    res = pl.pallas_call(
        body, name=name,
        in_specs=[HBM_SPEC] * (ns + nl) + [SEM_SPEC, SEM_SPEC, pl.BlockSpec(memory_space=pl.ANY)],
        out_specs=tuple([HBM_SPEC] * (ns + nl)),
        out_shape=tuple(pltpu.HBM(a.shape, a.dtype) for a in arrays),
        input_output_aliases={i: i for i in range(ns + nl)},
        compiler_params=pltpu.CompilerParams(has_side_effects=DATAFLOW_EFFECT),
    )(*arrays, send_sems, recv_sems, after)
    return list(res[:ns]), list(res[ns:])


def _after(token, value):
    return value + token[0:1, 0:1].astype(value.dtype)


def _chips(x, y):
    return [(1 - x, y), (x, 1 - y), (1 - x, 1 - y)]


def _plan_gather_first(srcs, lands, x, y, c):
    me = 4 * x + 2 * y + c
    out = []
    for s, l in zip(srcs, lands):
        out.append((s, l.at[me], (x, y, 1 - c)))
        out += [(s, l.at[me], (*chip, c)) for chip in _chips(x, y)]
    return out


def _plan_gather_pass(srcs, lands, x, y, c):
    out = []
    for l in lands:
        for chip in _chips(x, y):
            slot = l.at[4 * chip[0] + 2 * chip[1] + c]
            out.append((slot, slot, (x, y, 1 - c)))
    return out


def _plan_grads_sibling(srcs, lands, x, y, c):
    return [(s.at[2 * q + (1 - c)], l.at[q], (x, y, 1 - c)) for s, l in zip(srcs, lands) for q in range(4)]


def _plan_grads_chips(srcs, lands, x, y, c):
    return [(s.at[2 * chip[0] + chip[1]], l.at[k], (*chip, c))
            for s, l in zip(srcs, lands) for k, chip in enumerate(_chips(x, y))]


def _landing(n_slots, like):
    return [lax.empty((n_slots,) + a.shape[1:], a.dtype) for a in like]


def _sum_slots(g, name):
    _, R, C = g.shape
    tr = min(256, R)
    assert R % tr == 0

    def body(g_ref, o_ref):
        acc = g_ref[0]
        for j in range(1, N_DEV):
            acc = acc + g_ref[j]
        o_ref[...] = acc

    return pl.pallas_call(
        body, name=name, grid=(R // tr,),
        in_specs=[pl.BlockSpec((N_DEV, tr, C), lambda i: (0, i, 0))],
        out_specs=pl.BlockSpec((tr, C), lambda i: (i, 0)),
        out_shape=jax.ShapeDtypeStruct((R, C), F32),
        compiler_params=_params("parallel"))(g)


def _chip_partial(g, r1, place, name):
    _, R, C = g.shape
    tr = min(1024, R)
    assert R % tr == 0

    def body(pl_ref, g_ref, r_ref, pb_ref, pm_ref):
        q = pl.program_id(1)
        s = g_ref[...] + r_ref[...]
        pb_ref[...] = s.astype(BF16)

        @pl.when(q == pl_ref[1])
        def _():
            pm_ref[...] = s

    grid_spec = pltpu.PrefetchScalarGridSpec(
        num_scalar_prefetch=1, grid=(R // tr, 4),
        in_specs=[pl.BlockSpec((None, tr, C), lambda r, q, p: (2 * q + p[0], r, 0)),
                  pl.BlockSpec((None, tr, C), lambda r, q, p: (q, r, 0))],
        out_specs=[pl.BlockSpec((None, tr, C), lambda r, q, p: (q, r, 0)),
                   pl.BlockSpec((tr, C), lambda r, q, p: (r, 0))])
    return pl.pallas_call(
        body, name=name, grid_spec=grid_spec,
        out_shape=[jax.ShapeDtypeStruct((4, R, C), BF16), jax.ShapeDtypeStruct((R, C), F32)],
        compiler_params=_params("parallel", "arbitrary"))(place, g, r1)


def _adamw_update(w, g, m, v, grad_ref, delta_ref, m_ref, v_ref):
    mn = ADAM_B1 * m + (1.0 - ADAM_B1) * g
    vn = ADAM_B2 * v + (1.0 - ADAM_B2) * jnp.square(g)
    m_hat = mn / (1.0 - ADAM_B1 ** ADAM_STEP)
    v_hat = vn / (1.0 - ADAM_B2 ** ADAM_STEP)
    grad_ref[...] = g
    delta_ref[...] = -ADAM_LR * (m_hat / (jnp.sqrt(v_hat) + ADAM_EPS) + ADAM_WD * w)
    m_ref[...] = mn
    v_ref[...] = vn


def _adamw_whole(w, g, m, v, name):
    def body(w_ref, g_ref, m_ref, v_ref, go, do, mo, vo):
        _adamw_update(w_ref[...], g_ref[...], m_ref[...], v_ref[...], go, do, mo, vo)

    vm = pl.BlockSpec(memory_space=pltpu.VMEM)
    return pl.pallas_call(body, name=name, in_specs=[vm] * 4, out_specs=[vm] * 4,
                          out_shape=[jax.ShapeDtypeStruct(w.shape, F32)] * 4)(w, g, m, v)


def _adamw(w, gparts, m, v, name):
    _, R, C = w.shape
    tr = min(256, R)
    assert R % tr == 0

    def body(w_ref, g0_ref, g3_ref, m_ref, v_ref, go, do, mo, vo):
        g = g0_ref[...]
        for k in range(3):
            g = g + g3_ref[k].astype(F32)
        _adamw_update(w_ref[...], g, m_ref[...], v_ref[...], go, do, mo, vo)

    blk = pl.BlockSpec((tr, C), lambda i: (i, 0))
    wblk = pl.BlockSpec((None, tr, C), lambda i: (0, i, 0))
    return pl.pallas_call(
        body, name=name, grid=(R // tr,),
        in_specs=[wblk, blk, pl.BlockSpec((3, tr, C), lambda i: (0, i, 0)), wblk, wblk], out_specs=[wblk] * 4,
        out_shape=[jax.ShapeDtypeStruct(w.shape, F32)] * 4,
        compiler_params=_params("parallel"))(w, gparts[0], gparts[1], m, v)


def _adamw_layers(w, parts, m, v, name):
    _, R, C = w.shape
    tr = min(256, R)
    assert R % tr == 0

    def body(w_ref, p0, r0, p1, r1, m_ref, v_ref, go, do, mo, vo):
        gs = []
        for p, r in ((p0, r0), (p1, r1)):
            g = p[...]
            for k in range(3):
                g = g + r[k].astype(F32)
            gs.append(g)
        g = jnp.where(pl.program_id(0) == 0, gs[0], gs[1])
        _adamw_update(w_ref[...], g, m_ref[...], v_ref[...], go, do, mo, vo)

    lay = pl.BlockSpec((None, tr, C), lambda l, i: (l, i, 0))
    one = pl.BlockSpec((tr, C), lambda l, i: (i, 0))
    three = pl.BlockSpec((3, tr, C), lambda l, i: (0, i, 0))
    return pl.pallas_call(
        body, name=name, grid=(2, R // tr), in_specs=[lay, one, three, one, three, lay, lay],
        out_specs=[lay] * 4, out_shape=[jax.ShapeDtypeStruct((2, R, C), F32)] * 4,
        compiler_params=_params("parallel", "parallel"))(w, parts[0][0], parts[0][1], parts[1][0], parts[1][1], m, v)


SMALL_SHARDED = ("rg_conv_w", "rg_b_a", "rg_b_x", "rg_lambda", "gla_w_gate_up", "gla_b_gate", "gla_norm")
SMALL_REPLICATED = ("norm_mix", "norm_mlp", "norm_final", "rg_conv_b", "rg_w_a", "rg_w_x", "hg_lb_logits", "hg_norm")
WEIGHT_NAMES = ("norm_mix", "norm_mlp", "norm_final", "mlp_w1", "mlp_w2", "ab_w_in", "ab_w_out", "rg_conv_w",
                "rg_conv_b", "rg_w_a", "rg_b_a", "rg_w_x", "rg_b_x", "rg_lambda", "hg_lb_logits", "hg_norm",
                "gla_w_in", "gla_w_out", "gla_w_gate_up", "gla_b_gate", "gla_norm")


def _rows128(a):
    return a.reshape(-1, LANES)


def _part_rows(a):
    return -(-(a.size // LANES) // SUBLANES) * SUBLANES


def _pack_rows(arrays, pad_to=SUBLANES):
    parts = [jnp.pad(_rows128(a), ((0, _part_rows(a) - a.size // LANES), (0, 0))) for a in arrays]
    total = sum(p.shape[0] for p in parts)
    extra = (-total) % pad_to
    if extra:
        parts.append(jnp.zeros((extra, LANES), parts[0].dtype))
    return jnp.concatenate(parts, axis=0)


def _unshard_last(g, shape_local):
    nd = len(shape_local)
    t = g.reshape((N_DEV,) + tuple(shape_local))
    t = jnp.moveaxis(t, 0, nd - 1)
    return t.reshape(tuple(shape_local[:-1]) + (N_DEV * shape_local[-1],))


def _block_diag(w):
    eye = jnp.eye(8, dtype=w.dtype)
    return (w[:, :, :, None, :] * eye[None, :, None, :, None]).reshape(2, RG_W, RG_W)


def _block_diag_extract(dw):
    t = dw.reshape(2, 8, 64, 8, 64)
    return jnp.moveaxis(jnp.diagonal(t, axis1=1, axis2=3), -1, 1)


def kernel(x, norm_mix, norm_mlp, norm_final, mlp_w1, mlp_w2, ab_w_in, ab_w_out, rg_conv_w, rg_conv_b, rg_w_a, rg_b_a, rg_w_x, rg_b_x, rg_lambda, hg_lb_logits, hg_norm, gla_w_in, gla_w_out, gla_w_gate_up, gla_b_gate, gla_norm, loss_target, m_norm_mix, m_norm_mlp, m_norm_final, m_mlp_w1, m_mlp_w2, m_ab_w_in, m_ab_w_out, m_rg_conv_w, m_rg_conv_b, m_rg_w_a, m_rg_b_a, m_rg_w_x, m_rg_b_x, m_rg_lambda, m_hg_lb_logits, m_hg_norm, m_gla_w_in, m_gla_w_out, m_gla_w_gate_up, m_gla_b_gate, m_gla_norm, v_norm_mix, v_norm_mlp, v_norm_final, v_mlp_w1, v_mlp_w2, v_ab_w_in, v_ab_w_out, v_rg_conv_w, v_rg_conv_b, v_rg_w_a, v_rg_b_a, v_rg_w_x, v_rg_b_x, v_rg_lambda, v_hg_lb_logits, v_hg_norm, v_gla_w_in, v_gla_w_out, v_gla_w_gate_up, v_gla_b_gate, v_gla_norm):
    w_loc = dict(norm_mix=norm_mix, norm_mlp=norm_mlp, norm_final=norm_final, mlp_w1=mlp_w1, mlp_w2=mlp_w2,
                 ab_w_in=ab_w_in, ab_w_out=ab_w_out, rg_conv_w=rg_conv_w, rg_conv_b=rg_conv_b, rg_w_a=rg_w_a,
                 rg_b_a=rg_b_a, rg_w_x=rg_w_x, rg_b_x=rg_b_x, rg_lambda=rg_lambda, hg_lb_logits=hg_lb_logits,
                 hg_norm=hg_norm, gla_w_in=gla_w_in, gla_w_out=gla_w_out, gla_w_gate_up=gla_w_gate_up,
                 gla_b_gate=gla_b_gate, gla_norm=gla_norm)
    m_loc = dict(norm_mix=m_norm_mix, norm_mlp=m_norm_mlp, norm_final=m_norm_final, mlp_w1=m_mlp_w1,
                 mlp_w2=m_mlp_w2, ab_w_in=m_ab_w_in, ab_w_out=m_ab_w_out, rg_conv_w=m_rg_conv_w,
                 rg_conv_b=m_rg_conv_b, rg_w_a=m_rg_w_a, rg_b_a=m_rg_b_a, rg_w_x=m_rg_w_x, rg_b_x=m_rg_b_x,
                 rg_lambda=m_rg_lambda, hg_lb_logits=m_hg_lb_logits, hg_norm=m_hg_norm, gla_w_in=m_gla_w_in,
                 gla_w_out=m_gla_w_out, gla_w_gate_up=m_gla_w_gate_up, gla_b_gate=m_gla_b_gate,
                 gla_norm=m_gla_norm)
    v_loc = dict(norm_mix=v_norm_mix, norm_mlp=v_norm_mlp, norm_final=v_norm_final, mlp_w1=v_mlp_w1,
                 mlp_w2=v_mlp_w2, ab_w_in=v_ab_w_in, ab_w_out=v_ab_w_out, rg_conv_w=v_rg_conv_w,
                 rg_conv_b=v_rg_conv_b, rg_w_a=v_rg_w_a, rg_b_a=v_rg_b_a, rg_w_x=v_rg_w_x, rg_b_x=v_rg_b_x,
                 rg_lambda=v_rg_lambda, hg_lb_logits=v_hg_lb_logits, hg_norm=v_hg_norm, gla_w_in=v_gla_w_in,
                 gla_w_out=v_gla_w_out, gla_w_gate_up=v_gla_w_gate_up, gla_b_gate=v_gla_b_gate,
                 gla_norm=v_gla_norm)

    T = x.shape[1]
    h0 = x.reshape(T, D_MODEL)
    target = loss_target.reshape(T, D_MODEL)
    ax, ay, ac = lax.axis_index("x"), lax.axis_index("y"), lax.axis_index("c")
    dev = 4 * ax + 2 * ay + ac
    place = jnp.stack([ac, 2 * ax + ay]).astype(jnp.int32)

    abin_shard = ab_w_in[0].astype(BF16)
    first_started = _copies_start(_plan_gather_first, 4, [abin_shard], _landing(N_DEV, [abin_shard[None]]),
                                  "ag_first_start")
    rest_shards = [mlp_w1[0].astype(BF16), mlp_w2[0].astype(BF16), gla_w_in[0].astype(BF16),
                   gla_w_out[0].astype(BF16), mlp_w1[1].astype(BF16), mlp_w2[1].astype(BF16),
                   _after(first_started[4], ab_w_out[0].astype(BF16))]
    ag_started = _copies_start(_plan_gather_first, 4 * len(rest_shards), rest_shards,
                               _landing(N_DEV, [s[None] for s in rest_shards]), "ag_rest_start")

    small_local = [w_loc[n] for n in SMALL_SHARDED]
    small_g = _allgather_vmem(_pack_rows(small_local, 8), "ag_small")
    small_g = small_g.reshape(N_DEV, -1, LANES)
    full = {}
    off = 0
    for n, a in zip(SMALL_SHARDED, small_local):
        full[n] = _unshard_last(small_g[:, off:off + a.size // LANES].reshape(N_DEV, a.size), a.shape)
        off += _part_rows(a)
    conv_w = full["rg_conv_w"][0]
    b_a, b_x, lam = full["rg_b_a"][0], full["rg_b_x"][0], full["rg_lambda"][0]
    w_up, b_gate, g_norm = full["gla_w_gate_up"][0], full["gla_b_gate"][0], full["gla_norm"]

    cw8 = jnp.pad(conv_w, ((0, 4), (0, 0)))
    wbd = jnp.concatenate([_block_diag(rg_w_a[0]), _block_diag(rg_w_x[0])], axis=2).astype(BF16)
    rg_bias = jnp.concatenate([b_a, b_x], axis=1).reshape(2, 1, 2 * RG_W)
    lam3 = lam.reshape(2, 1, RG_W)
    l0, l1 = hg_lb_logits[0:1], hg_lb_logits[1:2]
    wup_pad = jnp.zeros((2, LANES, 512), F32).at[0, 0:16].set(w_up[0]).at[1, 16:32].set(w_up[1])
    bg3 = b_gate.reshape(2, 1, 512)
    nmix0, nmix1 = norm_mix[0:1], norm_mix[1:2]
    nmlp0, nmlp1 = norm_mlp[0:1], norm_mlp[1:2]
    nfin = norm_final.reshape(1, D_MODEL)

    prepared = (ag_started[4] + cw8[:, 0:LANES] + wup_pad[0, 0:SUBLANES, 0:LANES] + rg_bias[0, :, 0:LANES]
                + wbd[0, 0:SUBLANES, 0:LANES].astype(F32) + lam3[0, :, 0:LANES] + bg3[0, :, 0:LANES])
    (abin_shard,), abin_l = _copies_wait(_plan_gather_first, first_started, prepared, "ag_first_wait")
    first_pass = _copies_start(_plan_gather_pass, 3, [], abin_l, "ag_first_pass_start")
    _, (abin_g,) = _copies_wait(_plan_gather_pass, first_pass, first_pass[4], "ag_first_pass_wait")
    abin_g = lax.dynamic_update_index_in_dim(abin_g, abin_shard, dev, 0)
    wab_in = jnp.transpose(abin_g, (1, 0, 2)).reshape(D_MODEL, AB_IN)
    proj0, y0 = _norm_matmul(h0, _after(ag_started[4], nmix0), wab_in, "l0_in_proj")
    xc = _rg_conv_fwd(proj0, cw8, rg_conv_b, "rg_conv")
    hs = _rg_scan_fwd(xc, wbd, rg_bias, lam3, "rg_scan")
    o_hg, s_hg = _hg_fwd(proj0, l0, l1, "hg_chunks")
    both_done = hs[0][0:SUBLANES, 0:LANES] + o_hg[0][0:SUBLANES, 0:LANES]
    rest_shards, rest_lands = _copies_wait(_plan_gather_first, ag_started, both_done, "ag_rest_wait")
    pass_started = _copies_start(_plan_gather_pass, 3 * len(rest_lands), [], rest_lands, "ag_pass_start")
    mixin0 = _l0_combine_fwd(hs, proj0, o_hg, _after(pass_started[4], hg_norm), "l0_combine")
    _, rest_g = _copies_wait(_plan_gather_pass, pass_started, mixin0, "ag_pass_wait")
    rest_g = [lax.dynamic_update_index_in_dim(g, s, dev, 0) for g, s in zip(rest_g, rest_shards)]
    wab_out = rest_g[6].reshape(D_MODEL, D_MODEL)
    h1 = _matmul_res(mixin0, wab_out, h0, "l0_out_proj")
    w1g = (rest_g[0], rest_g[4])
    w2f = (rest_g[1].reshape(D_FF, D_MODEL), rest_g[5].reshape(D_FF, D_MODEL))
    wgla_in = jnp.pad(jnp.transpose(rest_g[2], (1, 0, 2)).reshape(D_MODEL, GLA_IN),
                      ((0, 0), (0, GLA_IN_PAD - GLA_IN)))
    wgla_out = rest_g[3].reshape(D_MODEL, D_MODEL)
    h2, pre0, ym0 = _mlp_fwd(h1, nmlp0, w1g[0], w2f[0], "mlp0")
    proj1, y1 = _norm_matmul(h2, nmix1, wgla_in, "l1_in_proj")
    z_gate, lr_b = _gate_logits(proj1, wup_pad, bg3, "gla_gate_logits")
    o_gla, s_gla = _gla_fwd(proj1, z_gate, "gla_chunks")
    mixin1 = _l1_combine_fwd(o_gla, proj1, g_norm, "l1_combine")
    h3 = _matmul_res(mixin1, wgla_out, h2, "l1_out_proj")
    h4, pre1, ym1 = _mlp_fwd(h3, nmlp1, w1g[1], w2f[1], "mlp1")
    loss_blk, dh4, dh4b, d_nfin = _final_loss(h4, nfin, target, "final_loss")

    dh3, dh3b, dpre1, act1, d_nmlp1 = _mlp_bwd(dh4, dh4b, h3, nmlp1, pre1, w1g[1], w2f[1], "mlp1_bwd")
    g_w1_1 = _wgrad(ym1, dpre1, 512, "mlp1_dw1", sharded_cols=True)
    g_w2_1 = _wgrad(act1, dh4b, 512, "mlp1_dw2")
    g_gla_out = _wgrad(mixin1, dh3b, 512, "l1_out_dw")
    do_gla, dr, d_gnorm = _l1_combine_bwd(o_gla, proj1, g_norm, dh3b, wgla_out, "l1_combine_bwd")
    dq1, dk1, dv1, dz_gate = _gla_bwd(proj1, z_gate, s_gla, do_gla, "gla_chunks_bwd")
    dlr1, d_bg, dz_b = _gate_logits_bwd(dz_gate, wup_pad, "gla_gate_logits_bwd")
    d_wup = [_wgrad(lr_b, dz_b[d], 512, "gla_gate_dw%d" % d) for d in range(2)]
    dproj1 = _l1_assemble(dq1, dk1, dv1, dr, dlr1, "l1_assemble")
    dh2, dh2b, d_nmix1 = _dgrad_norm(dproj1, wgla_in, h2, nmix1, dh3, "l1_in_dgrad")
    g_gla_in = _wgrad(y1, dproj1, 640, "l1_in_dw")

    def reduce_start(grads, tag):
        return _copies_start(_plan_grads_sibling, 4 * len(grads), grads, _landing(4, grads), "rs_%s_d2d_start" % tag)

    def reduce_mid(started, after, tag):
        grads, got = _copies_wait(_plan_grads_sibling, started, after, "rs_%s_d2d_wait" % tag)
        parts = [_chip_partial(g, r, place, "rs_%s_partial%d" % (tag, a)) for a, (g, r) in enumerate(zip(grads, got))]
        pb = [p[0] for p in parts]
        return _copies_start(_plan_grads_chips, 3 * len(pb), pb, _landing(3, pb), "rs_%s_ici_start" % tag), \
            [p[1] for p in parts]

    def reduce_end(started, mine, after, tag):
        _, got = _copies_wait(_plan_grads_chips, started, after, "rs_%s_ici_wait" % tag)
        return list(zip(mine, got))

    slots_l1 = [g_w1_1, g_w2_1.reshape(N_DEV, 512, D_MODEL),
                jnp.transpose(g_gla_in[:, :GLA_IN].reshape(D_MODEL, N_DEV, GLA_IN // N_DEV), (1, 0, 2)),
                g_gla_out.reshape(N_DEV, 128, D_MODEL)]
    ra_d2d = reduce_start(slots_l1, "l1")

    dh1, dh1b, dpre0, act0, d_nmlp0 = _mlp_bwd(dh2, dh2b, h1, _after(ra_d2d[4], nmlp0), pre0, w1g[0], w2f[0],
                                               "mlp0_bwd")
    g_w1_0 = _wgrad(ym0, dpre0, 512, "mlp0_dw1", sharded_cols=True)
    g_w2_0 = _wgrad(act0, dh2b, 512, "mlp0_dw2")
    ra_ici, ra_mine = reduce_mid(ra_d2d, g_w2_0, "l1")
    g_ab_out = _wgrad(mixin0, dh1b, 512, "l0_out_dw")
    rb_d2d = reduce_start([g_w1_0, g_w2_0.reshape(N_DEV, 512, D_MODEL), g_ab_out.reshape(N_DEV, 128, D_MODEL)],
                          "mlp0")
    dho, dga, do_hg, dg_gate, d_hgnorm = _l0_combine_bwd(
        hs, proj0, o_hg, _after(rb_d2d[4], _after(ra_ici[4], hg_norm)), dh1b, wab_out, "l0_combine_bwd")
    dxc, d_wbd, d_rgb, d_lam = _rg_scan_bwd(xc, wbd, rg_bias, lam3, hs, dho, "rg_scan_bwd")
    dxa, d_cw8, d_cb = _rg_conv_bwd(dxc, proj0, cw8, "rg_conv_bwd")
    dq0, df0, dv0, d_l0, d_l1 = _hg_bwd(proj0, l0, l1, s_hg, do_hg, "hg_chunks_bwd")
    rb_ici, rb_mine = reduce_mid(rb_d2d, d_l0, "mlp0")
    dproj0 = _l0_assemble(dxa, dga, dq0, df0, dv0, dg_gate, "l0_assemble")
    dx, _, d_nmix0 = _dgrad_norm(dproj0, wab_in, h0, _after(rb_ici[4], nmix0), dh1, "l0_in_dgrad")

    d_wa = _block_diag_extract(d_wbd[:, :, :RG_W])[None]
    d_wx = _block_diag_extract(d_wbd[:, :, RG_W:])[None]
    small_full = {
        "norm_mix": jnp.concatenate([d_nmix0, d_nmix1], axis=0), "norm_mlp": jnp.concatenate([d_nmlp0, d_nmlp1], axis=0),
        "norm_final": d_nfin.reshape(D_MODEL), "rg_conv_b": d_cb, "rg_w_a": d_wa, "rg_w_x": d_wx,
        "hg_lb_logits": jnp.concatenate([d_l0[0] + d_l0[1], d_l1[0] + d_l1[1]], axis=0), "hg_norm": d_hgnorm,
        "rg_conv_w": d_cw8[0:4][None], "rg_b_a": d_rgb[:, 0, :RG_W][None], "rg_b_x": d_rgb[:, 0, RG_W:][None],
        "rg_lambda": d_lam[:, 0, :][None],
        "gla_w_gate_up": jnp.stack([d_wup[0][0:16], d_wup[1][16:32]])[None], "gla_b_gate": d_bg[:, 0, :][None],
        "gla_norm": d_gnorm}
    small_names = SMALL_REPLICATED + SMALL_SHARDED
    packed = _pack_rows([loss_blk] + [small_full[n] for n in small_names], 256)
    ar_first = _copies_start(_plan_gather_first, 4, [packed], _landing(N_DEV, [packed[None]]), "ar_small_start")

    g_ab_in = _wgrad(y0, dproj0, 512, "l0_in_dw", behind=ar_first[4])
    rc_d2d = reduce_start([jnp.transpose(g_ab_in.reshape(D_MODEL, N_DEV, AB_IN // N_DEV), (1, 0, 2))], "ab")
    (packed,), ar_lands = _copies_wait(_plan_gather_first, ar_first, rc_d2d[4], "ar_small_wait")
    ar_pass = _copies_start(_plan_gather_pass, 3, [], ar_lands, "ar_small_pass_start")
    rc_ici, rc_mine = reduce_mid(rc_d2d, ar_pass[4], "ab")
    _, (ar_gathered,) = _copies_wait(_plan_gather_pass, ar_pass, rc_ici[4], "ar_small_pass_wait")
    summed = _sum_slots(lax.dynamic_update_index_in_dim(ar_gathered, packed, dev, 0), "ar_small_sum")
    loss = summed[0, 0]

    pieces_l1 = reduce_end(ra_ici, ra_mine, rc_ici[4], "l1")
    res_gla_in = _adamw(gla_w_in, pieces_l1[2], m_gla_w_in, v_gla_w_in, "adamw_gla_in")
    res_gla_out = _adamw(gla_w_out, pieces_l1[3], m_gla_w_out, v_gla_w_out, "adamw_gla_out")
    pieces_mlp0 = reduce_end(rb_ici, rb_mine, res_gla_out[0], "mlp0")
    res_w1 = _adamw_layers(mlp_w1, (pieces_mlp0[0], pieces_l1[0]), m_mlp_w1, v_mlp_w1, "adamw_mlp_w1")
    res_w2 = _adamw_layers(mlp_w2, (pieces_mlp0[1], pieces_l1[1]), m_mlp_w2, v_mlp_w2, "adamw_mlp_w2")
    res = {"mlp_w1": tuple(res_w1), "mlp_w2": tuple(res_w2),
           "gla_w_in": tuple(res_gla_in), "gla_w_out": tuple(res_gla_out),
           "ab_w_out": tuple(_adamw(ab_w_out, pieces_mlp0[2], m_ab_w_out, v_ab_w_out, "adamw_ab_out"))}

    off = SUBLANES
    for n in small_names:
        a = small_full[n]
        gfull = summed[off:off + a.size // LANES].reshape(a.shape)
        off += _part_rows(a)
        local = w_loc[n].shape
        if n in SMALL_SHARDED:
            gfull = lax.dynamic_slice_in_dim(gfull, dev * local[-1], local[-1], axis=gfull.ndim - 1)
        flat = (-1, local[-1])
        outs = _adamw_whole(w_loc[n].reshape(flat), gfull.reshape(flat), m_loc[n].reshape(flat),
                            v_loc[n].reshape(flat), "adamw_" + n)
        res[n] = tuple(o.reshape(local) for o in outs)
    others_done = (res_w1[1][0, 0:SUBLANES, 0:LANES] + res_w2[1][0, 0:SUBLANES, 0:LANES]
                   + res_gla_in[1][0, 0:SUBLANES, 0:LANES])
    pieces_ab = reduce_end(rc_ici, rc_mine, others_done, "ab")
    res["ab_w_in"] = tuple(_adamw(ab_w_in, pieces_ab[0], m_ab_w_in, v_ab_w_in, "adamw_ab_in"))

    grad_x = dx.reshape(1, T, D_MODEL)
    out = [loss, grad_x]
    for k in range(4):
        out += [res[n][k] for n in WEIGHT_NAMES]
    return tuple(out)
```

```python
import jax
import jax.numpy as jnp
from jax import lax
from jax.experimental import pallas as pl
from jax.experimental.pallas import tpu as pltpu

F32, BF16 = jnp.float32, jnp.bfloat16
HI = lax.Precision.HIGH
MESH = pl.DeviceIdType.MESH

D_MODEL = 1024
D_FF = 4096
RG_W = 512
HG_W = 512
CHUNK = 64
EPS = 1e-6
RG_C = 8.0
AB_IN = 3584
GLA_IN = 3104
GLA_IN_PAD = 3200
N_DEV = 8
LANES = 128
SUBLANES = 8
VMEM_LIMIT = 48 * 1024 * 1024

ADAM_LR, ADAM_B1, ADAM_B2, ADAM_EPS, ADAM_WD, ADAM_STEP = 0.001, 0.9, 0.999, 1e-08, 0.01, 10


def _params(*sem):
    return pltpu.CompilerParams(dimension_semantics=sem, vmem_limit_bytes=VMEM_LIMIT)


def _dg(a, b, ca, cb):
    return lax.dot_general(a.astype(BF16), b.astype(BF16), (((ca,), (cb,)), ((), ())),
                           preferred_element_type=F32)


@jax.custom_vjp
def _mm_nn(a, b):
    return _dg(a, b, 1, 0)


_mm_nn.defvjp(lambda a, b: (_dg(a, b, 1, 0), (a, b)),
              lambda res, g: (_dg(g, res[1], 1, 1), _dg(res[0], g, 0, 0)))


@jax.custom_vjp
def _mm_nt(a, b):
    return _dg(a, b, 1, 1)


_mm_nt.defvjp(lambda a, b: (_dg(a, b, 1, 1), (a, b)),
              lambda res, g: (_dg(g, res[1], 1, 0), _dg(g, res[0], 0, 0)))


@jax.custom_vjp
def _mm_tn(a, b):
    return _dg(a, b, 0, 0)


_mm_tn.defvjp(lambda a, b: (_dg(a, b, 0, 0), (a, b)),
              lambda res, g: (_dg(res[1], g, 1, 1), _dg(res[0], g, 1, 0)))


@jax.custom_vjp
def _cum(tri, tri_t, x):
    return jnp.dot(tri, x, precision=HI, preferred_element_type=F32)


_cum.defvjp(lambda tri, tri_t, x: (jnp.dot(tri, x, precision=HI, preferred_element_type=F32), (tri, tri_t)),
            lambda res, g: (jnp.zeros_like(res[0]), jnp.zeros_like(res[1]),
                            jnp.dot(res[1], g, precision=HI, preferred_element_type=F32)))


def _sig(x):
    return 1.0 / (1.0 + jnp.exp(-x))


def _gelu(x):
    return 0.5 * x * (1.0 + jnp.tanh(0.7978845608028654 * (x + 0.044715 * (x * x * x))))


def _softplus(z):
    return jnp.maximum(z, 0.0) + jnp.log(1.0 + jnp.exp(-jnp.abs(z)))


def _rms(x):
    return lax.rsqrt(jnp.mean(x * x, axis=-1, keepdims=True) + EPS)


def _rmsnorm_bwd(x, gain, dy):
    r = _rms(x)
    xh = x * r
    dgain = jnp.sum(dy * xh, axis=0, keepdims=True)
    dxh = dy * gain
    dx = r * (dxh - xh * jnp.mean(dxh * xh, axis=-1, keepdims=True))
    return dx, dgain


def _headnorm(o, gain, n_heads, hd):
    parts = []
    for h in range(n_heads):
        oh = o[:, h * hd:(h + 1) * hd]
        parts.append(oh * _rms(oh))
    return jnp.concatenate(parts, axis=1) * gain


def _tri_consts(d):
    row = lax.broadcasted_iota(jnp.int32, (CHUNK, CHUNK), 0)
    col = lax.broadcasted_iota(jnp.int32, (CHUNK, CHUNK), 1)
    ge = (row >= col).astype(F32)
    le = (row <= col).astype(F32)
    r1 = lax.broadcasted_iota(jnp.int32, (CHUNK, 1), 0)
    if d == 0:
        return ge, le, (r1 <= CHUNK // 2).astype(F32)
    return le, ge, (r1 >= CHUNK // 2 - 1).astype(F32)


def _chunk_core(qh, k, v, logf, st_prev, tri, tri_t, mref, n_heads, dk, dv):
    cum = _cum(tri, tri_t, logf)
    ref = jnp.sum(logf * mref, axis=0, keepdims=True)
    last = jnp.sum(logf, axis=0, keepdims=True)
    q_in = qh * jnp.exp(cum - ref)
    k_in = k * jnp.exp(ref - cum)
    k_st = k * jnp.exp(last - cum)
    q_dec = qh * jnp.exp(cum)
    decay = jnp.exp(last)
    outs, sts = [], []
    for h in range(n_heads):
        sk = slice(h * dk, (h + 1) * dk)
        sv = slice(h * dv, (h + 1) * dv)
        sc = _mm_nt(q_in[:, sk], k_in[:, sk]) * tri
        o = _mm_nn(sc, v[:, sv]) + _mm_nt(q_dec[:, sk], st_prev[h])
        sts.append(st_prev[h] * decay[:, sk] + _mm_tn(v[:, sv], k_st[:, sk]))
        outs.append(o)
    return jnp.concatenate(outs, axis=1), tuple(sts)


def _hg_chunk(q, f, v, l0, l1, st_prev, tri, tri_t, mref):
    lb = _sig(l0 - l1)
    sg = _sig(f)
    qh = q * _sig(q)
    logf = jnp.log(lb + (1.0 - lb) * sg)
    k = (1.0 - lb) * (1.0 - sg)
    return _chunk_core(qh, k, v, logf, st_prev, tri, tri_t, mref, 4, 128, 128)


def _gla_chunk(q, k, v, z, st_prev, tri, tri_t, mref):
    logf = (jnp.minimum(z, 0.0) - jnp.log(1.0 + jnp.exp(-jnp.abs(z)))) * (1.0 / 16.0)
    qh = q * (128.0 ** -0.5)
    return _chunk_core(qh, k, v, logf, st_prev, tri, tri_t, mref, 4, 128, 256)


def _rg_gates(xc, wbd, bias, lam):
    z = _mm_nn(xc, wbd) + bias
    r = _sig(z[:, :RG_W])
    i = _sig(z[:, RG_W:])
    log_a = -RG_C * r * _softplus(-lam)
    a = jnp.exp(log_a)
    x2 = 2.0 * log_a
    neg_expm1 = jnp.where(x2 > -1e-2, -(x2 + 0.5 * x2 * x2 + x2 * x2 * x2 * (1.0 / 6.0)), 1.0 - jnp.exp(x2))
    u = jnp.sqrt(neg_expm1) * (i * xc)
    return a, u


def _l0_combine(hf, hb, ga, of, ob, g, gain):
    ya = (hf + hb) * _gelu(ga)
    yb = _headnorm(of + ob, gain, 4, 128) * (g * _sig(g))
    return jnp.concatenate([ya, yb], axis=1)


def _l1_combine(of, ob, r, gain):
    return _headnorm(of + ob, gain, 4, 256) * (r * _sig(r))


def _norm_matmul(h, gain, w, name):
    T, D = h.shape
    N = w.shape[1]
    tm = min(512, T)

    def body(h_ref, g_ref, w_ref, o_ref, y_ref):
        x = h_ref[...]
        y = (x * _rms(x) * g_ref[...]).astype(BF16)
        y_ref[...] = y
        o_ref[...] = jnp.dot(y, w_ref[...], preferred_element_type=F32)

    return pl.pallas_call(
        body, name=name, grid=(T // tm,),
        in_specs=[pl.BlockSpec((tm, D), lambda i: (i, 0)), pl.BlockSpec((1, D), lambda i: (0, 0)),
                  pl.BlockSpec((D, N), lambda i: (0, 0))],
        out_specs=[pl.BlockSpec((tm, N), lambda i: (i, 0)), pl.BlockSpec((tm, D), lambda i: (i, 0))],
        out_shape=[jax.ShapeDtypeStruct((T, N), F32), jax.ShapeDtypeStruct((T, D), BF16)],
        compiler_params=_params("parallel"))(h, gain, w)


def _matmul_res(a, w, res, name):
    T, K = a.shape
    N = w.shape[1]
    tm = min(512, T)

    def body(a_ref, w_ref, r_ref, o_ref):
        o_ref[...] = r_ref[...] + jnp.dot(a_ref[...], w_ref[...], preferred_element_type=F32)

    return pl.pallas_call(
        body, name=name, grid=(T // tm,),
        in_specs=[pl.BlockSpec((tm, K), lambda i: (i, 0)), pl.BlockSpec((K, N), lambda i: (0, 0)),
                  pl.BlockSpec((tm, N), lambda i: (i, 0))],
        out_specs=pl.BlockSpec((tm, N), lambda i: (i, 0)),
        out_shape=jax.ShapeDtypeStruct((T, N), F32),
        compiler_params=_params("parallel"))(a, w, res)


def _dgrad_norm(dproj, w, h, gain, dres, name):
    T, N = dproj.shape
    D = w.shape[0]
    tm = min(512, T)

    def body(dp_ref, w_ref, h_ref, g_ref, dr_ref, dh_ref, dhb_ref, dg_ref):
        @pl.when(pl.program_id(0) == 0)
        def _():
            dg_ref[...] = jnp.zeros_like(dg_ref)

        dy = _dg(dp_ref[...], w_ref[...], 1, 1)
        dx, dgain = _rmsnorm_bwd(h_ref[...], g_ref[...], dy)
        dh = dr_ref[...] + dx
        dh_ref[...] = dh
        dhb_ref[...] = dh.astype(BF16)
        dg_ref[...] += dgain

    return pl.pallas_call(
        body, name=name, grid=(T // tm,),
        in_specs=[pl.BlockSpec((tm, N), lambda i: (i, 0)), pl.BlockSpec((D, N), lambda i: (0, 0)),
                  pl.BlockSpec((tm, D), lambda i: (i, 0)), pl.BlockSpec((1, D), lambda i: (0, 0)),
                  pl.BlockSpec((tm, D), lambda i: (i, 0))],
        out_specs=[pl.BlockSpec((tm, D), lambda i: (i, 0)), pl.BlockSpec((tm, D), lambda i: (i, 0)),
                   pl.BlockSpec((1, D), lambda i: (0, 0))],
        out_shape=[jax.ShapeDtypeStruct((T, D), F32), jax.ShapeDtypeStruct((T, D), BF16),
                   jax.ShapeDtypeStruct((1, D), F32)],
        compiler_params=_params("arbitrary"))(dproj, w, h, gain, dres)


def _wgrad(a, b, tn, name, sharded_cols=False, behind=None):
    T, K = a.shape
    N = b.shape[1]
    tk = min(1024, K)

    def body(a_ref, b_ref, *rest):
        rest[-1][...] = _dg(a_ref[...], b_ref[...], 0, 0)

    if sharded_cols:
        out_spec = pl.BlockSpec((None, tk, tn), lambda k, n: (n, k, 0))
        out_shape = jax.ShapeDtypeStruct((N // tn, K, tn), F32)
    else:
        out_spec = pl.BlockSpec((tk, tn), lambda k, n: (k, n))
        out_shape = jax.ShapeDtypeStruct((K, N), F32)
    in_specs = [pl.BlockSpec((T, tk), lambda k, n: (0, k)), pl.BlockSpec((T, tn), lambda k, n: (0, n))]
    args = [a, b]
    if behind is not None:
        in_specs.append(pl.BlockSpec((SUBLANES, LANES), lambda k, n: (0, 0)))
        args.append(behind)
    return pl.pallas_call(
        body, name=name, grid=(K // tk, N // tn), in_specs=in_specs, out_specs=out_spec, out_shape=out_shape,
        compiler_params=_params("parallel", "parallel"))(*args)


def _resident(shape):
    return pl.BlockSpec(shape, lambda i: (0,) * len(shape), pipeline_mode=pl.Buffered(1))


def _mlp_fwd(h, gain, w1g, w2, name):
    T, D = h.shape
    nf, _, tf = w1g.shape
    tm = min(512, T)

    def body(h_ref, g_ref, w1_ref, w2_ref, o_ref, pre_ref, y_ref):
        x = h_ref[...]
        y = (x * _rms(x) * g_ref[...]).astype(BF16)
        y_ref[...] = y
        acc = x
        for j in range(nf):
            cols = slice(j * tf, (j + 1) * tf)
            pre = jnp.dot(y, w1_ref[j], preferred_element_type=F32)
            pre_ref[:, cols] = pre.astype(BF16)
            act = jnp.square(jnp.maximum(pre, 0.0)).astype(BF16)
            acc = acc + jnp.dot(act, w2_ref[cols, :], preferred_element_type=F32)
        o_ref[...] = acc

    return pl.pallas_call(
        body, name=name, grid=(T // tm,),
        in_specs=[pl.BlockSpec((tm, D), lambda i: (i, 0)), pl.BlockSpec((1, D), lambda i: (0, 0)),
                  _resident(w1g.shape), _resident(w2.shape)],
        out_specs=[pl.BlockSpec((tm, D), lambda i: (i, 0)), pl.BlockSpec((tm, nf * tf), lambda i: (i, 0)),
                   pl.BlockSpec((tm, D), lambda i: (i, 0))],
        out_shape=[jax.ShapeDtypeStruct((T, D), F32), jax.ShapeDtypeStruct((T, nf * tf), BF16),
                   jax.ShapeDtypeStruct((T, D), BF16)],
        compiler_params=_params("parallel"))(h, gain, w1g, w2)


def _mlp_bwd(dout, dout_b, h, gain, pre, w1g, w2, name):
    T, D = h.shape
    nf, _, tf = w1g.shape
    tm = min(256, T)

    def body(do_ref, dob_ref, h_ref, g_ref, pre_ref, w1_ref, w2_ref, dh_ref, dhb_ref, dpre_ref, act_ref, dg_ref):
        @pl.when(pl.program_id(0) == 0)
        def _():
            dg_ref[...] = jnp.zeros_like(dg_ref)

        dob = dob_ref[...]
        dy = None
        for j in range(nf):
            cols = slice(j * tf, (j + 1) * tf)
            rp = jnp.maximum(pre_ref[:, cols].astype(F32), 0.0)
            dpre = (_dg(dob, w2_ref[cols, :], 1, 1) * (2.0 * rp)).astype(BF16)
            dpre_ref[:, cols] = dpre
            act_ref[:, cols] = (rp * rp).astype(BF16)
            part = _dg(dpre, w1_ref[j], 1, 1)
            dy = part if dy is None else dy + part
        dx, dgain = _rmsnorm_bwd(h_ref[...], g_ref[...], dy)
        dh = do_ref[...] + dx
        dh_ref[...] = dh
        dhb_ref[...] = dh.astype(BF16)
        dg_ref[...] += dgain

    tok = lambda w: pl.BlockSpec((tm, w), lambda i: (i, 0))
    return pl.pallas_call(
        body, name=name, grid=(T // tm,),
        in_specs=[tok(D), tok(D), tok(D), pl.BlockSpec((1, D), lambda i: (0, 0)), tok(nf * tf),
                  _resident(w1g.shape), _resident(w2.shape)],
        out_specs=[tok(D), tok(D), tok(nf * tf), tok(nf * tf), pl.BlockSpec((1, D), lambda i: (0, 0))],
        out_shape=[jax.ShapeDtypeStruct((T, D), F32), jax.ShapeDtypeStruct((T, D), BF16),
                   jax.ShapeDtypeStruct((T, nf * tf), BF16),
                   jax.ShapeDtypeStruct((T, nf * tf), BF16), jax.ShapeDtypeStruct((1, D), F32)],
        compiler_params=_params("arbitrary"))(dout, dout_b, h, gain, pre, w1g, w2)


def _final_loss(h, gain, target, name):
    T, D = h.shape
    tm = min(512, T)

    def body(h_ref, g_ref, t_ref, l_ref, dh_ref, dhb_ref, dg_ref):
        @pl.when(pl.program_id(0) == 0)
        def _():
            l_ref[...] = jnp.zeros_like(l_ref)
            dg_ref[...] = jnp.zeros_like(dg_ref)

        x = h_ref[...]
        err = x * _rms(x) * g_ref[...] - t_ref[...]
        l_ref[...] += 0.5 * jnp.sum(jnp.mean(err * err, axis=-1, keepdims=True), axis=0, keepdims=True)
        dx, dgain = _rmsnorm_bwd(x, g_ref[...], err * (1.0 / D))
        dh_ref[...] = dx
        dhb_ref[...] = dx.astype(BF16)
        dg_ref[...] += dgain

    return pl.pallas_call(
        body, name=name, grid=(T // tm,),
        in_specs=[pl.BlockSpec((tm, D), lambda i: (i, 0)), pl.BlockSpec((1, D), lambda i: (0, 0)),
                  pl.BlockSpec((tm, D), lambda i: (i, 0))],
        out_specs=[pl.BlockSpec((SUBLANES, LANES), lambda i: (0, 0)), pl.BlockSpec((tm, D), lambda i: (i, 0)),
                   pl.BlockSpec((tm, D), lambda i: (i, 0)), pl.BlockSpec((1, D), lambda i: (0, 0))],
        out_shape=[jax.ShapeDtypeStruct((SUBLANES, LANES), F32), jax.ShapeDtypeStruct((T, D), F32),
                   jax.ShapeDtypeStruct((T, D), BF16), jax.ShapeDtypeStruct((1, D), F32)],
        compiler_params=_params("arbitrary"))(h, gain, target)


def _halo_specs(tm, T, width, col, tile=lambda i: i):
    r8 = tm // SUBLANES
    nb8 = T // SUBLANES
    return [pl.BlockSpec((tm, width), lambda i: (tile(i), col)),
            pl.BlockSpec((SUBLANES, width), lambda i: (jnp.maximum(tile(i) * r8 - 1, 0), col)),
            pl.BlockSpec((SUBLANES, width), lambda i: (jnp.minimum((tile(i) + 1) * r8, nb8 - 1), col))]


def _ext(cur, prev, nxt, has_prev, has_next):
    return jnp.concatenate([jnp.where(has_prev, prev, 0.0), cur, jnp.where(has_next, nxt, 0.0)], axis=0)


def _shifted(ext, offset, tm):
    n = ext.shape[0]
    sh = (-offset) % n
    r = ext if sh == 0 else pltpu.roll(ext, sh, 0)
    return r[SUBLANES:SUBLANES + tm]


def _rg_conv_fwd(proj, cw8, cb, name):
    T = proj.shape[0]
    tm = min(512, T)
    nT = T // tm

    def body(cur_ref, prev_ref, next_ref, w_ref, b_ref, o_ref):
        i = pl.program_id(0)
        ext = _ext(cur_ref[...], prev_ref[...], next_ref[...], i > 0, i < nT - 1)
        acc = jnp.broadcast_to(b_ref[...], (tm, RG_W))
        for k in range(4):
            acc = acc + w_ref[k:k + 1, :] * _shifted(ext, k - 2, tm)
        o_ref[...] = acc

    return pl.pallas_call(
        body, name=name, grid=(nT,),
        in_specs=_halo_specs(tm, T, RG_W, 0) + [pl.BlockSpec((SUBLANES, RG_W), lambda i: (0, 0)),
                                                pl.BlockSpec((1, RG_W), lambda i: (0, 0))],
        out_specs=pl.BlockSpec((tm, RG_W), lambda i: (i, 0)),
        out_shape=jax.ShapeDtypeStruct((T, RG_W), F32),
        compiler_params=_params("parallel"))(proj, proj, proj, cw8, cb)


def _rg_conv_bwd(dxc, proj, cw8, name):
    T = proj.shape[0]
    tm = min(512, T)
    nT = T // tm

    def body(a0, p0, n0, a1, p1, n1, xa, xp, xn, w_ref, dxa_ref, dw_ref, db_ref):
        i = pl.program_id(0)

        @pl.when(i == 0)
        def _():
            dw_ref[...] = jnp.zeros_like(dw_ref)
            db_ref[...] = jnp.zeros_like(db_ref)

        has_p, has_n = i > 0, i < nT - 1
        cur = a0[...] + a1[...]
        dext = _ext(cur, p0[...] + p1[...], n0[...] + n1[...], has_p, has_n)
        xext = _ext(xa[...], xp[...], xn[...], has_p, has_n)
        acc = jnp.zeros((tm, RG_W), F32)
        rows = []
        for k in range(4):
            acc = acc + w_ref[k:k + 1, :] * _shifted(dext, 2 - k, tm)
            rows.append(jnp.sum(cur * _shifted(xext, k - 2, tm), axis=0, keepdims=True))
        dxa_ref[...] = acc
        dw_ref[...] += jnp.concatenate(rows + [jnp.zeros((4, RG_W), F32)], axis=0)
        db_ref[...] += jnp.sum(cur, axis=0, keepdims=True)

    return pl.pallas_call(
        body, name=name, grid=(nT,),
        in_specs=(_halo_specs(tm, T, RG_W, 0) + _halo_specs(tm, T, RG_W, 0)
                  + _halo_specs(tm, T, RG_W, 0) + [pl.BlockSpec((SUBLANES, RG_W), lambda i: (0, 0))]),
        out_specs=[pl.BlockSpec((tm, RG_W), lambda i: (i, 0)), pl.BlockSpec((SUBLANES, RG_W), lambda i: (0, 0)),
                   pl.BlockSpec((1, RG_W), lambda i: (0, 0))],
        out_shape=[jax.ShapeDtypeStruct((T, RG_W), F32), jax.ShapeDtypeStruct((SUBLANES, RG_W), F32),
                   jax.ShapeDtypeStruct((1, RG_W), F32)],
        compiler_params=_params("arbitrary"))(dxc[0], dxc[0], dxc[0], dxc[1], dxc[1], dxc[1], proj, proj, proj, cw8)


def _local_scan(a, b, ascending):
    n = a.shape[0]
    pos = jnp.bitwise_and(lax.broadcasted_iota(jnp.int32, a.shape, 0), SUBLANES - 1)
    for s in (1, 2, 4):
        sh = s if ascending else n - s
        ok = (pos >= s) if ascending else (pos < SUBLANES - s)
        a_sh, b_sh = pltpu.roll(a, sh, 0), pltpu.roll(b, sh, 0)
        b = jnp.where(ok, a * b_sh + b, b)
        a = jnp.where(ok, a * a_sh, a)
    return a, b


def _group_scan(chains, a_sc, b_sc, carry, n_groups):
    def step(g, hs):
        new = []
        for (d, out_ref, asc), h in zip(chains, hs):
            r0 = pl.multiple_of((g if asc else n_groups - 1 - g) * SUBLANES, SUBLANES)
            out_ref[pl.ds(r0, SUBLANES), :] = a_sc[d, pl.ds(r0, SUBLANES), :] * h + b_sc[d, pl.ds(r0, SUBLANES), :]
            new.append(out_ref[pl.ds(r0 + (SUBLANES - 1 if asc else 0), 1), :])
        return tuple(new)

    hs = lax.fori_loop(0, n_groups, step, tuple(carry[d, 0:1, :] for d, _, _ in chains))
    for (d, _, _), h in zip(chains, hs):
        carry[d, 0:1, :] = h


def _rg_scan_fwd(xc, wbd, bias, lam, name):
    T = xc.shape[0]
    tm = min(512, T)
    nT = T // tm

    def body(xf_ref, xb_ref, w_ref, b_ref, lam_ref, hf_ref, hb_ref, a_sc, b_sc, carry):
        @pl.when(pl.program_id(0) == 0)
        def _():
            carry[...] = jnp.zeros_like(carry)

        for d, x_ref in enumerate((xf_ref, xb_ref)):
            a, u = _rg_gates(x_ref[...], w_ref[d], b_ref[d], lam_ref[d])
            a_sc[d], b_sc[d] = _local_scan(a, u, d == 0)
        _group_scan(((0, hf_ref, True), (1, hb_ref, False)), a_sc, b_sc, carry, tm // SUBLANES)

    full = lambda a: pl.BlockSpec(a.shape, lambda i: (0,) * len(a.shape))
    res = pl.pallas_call(
        body, name=name, grid=(nT,),
        in_specs=[pl.BlockSpec((tm, RG_W), lambda i: (i, 0)), pl.BlockSpec((tm, RG_W), lambda i: (nT - 1 - i, 0)),
                  full(wbd), full(bias), full(lam)],
        out_specs=[pl.BlockSpec((tm, RG_W), lambda i: (i, 0)), pl.BlockSpec((tm, RG_W), lambda i: (nT - 1 - i, 0))],
        out_shape=[jax.ShapeDtypeStruct((T, RG_W), F32)] * 2,
        scratch_shapes=[pltpu.VMEM((2, tm, RG_W), F32), pltpu.VMEM((2, tm, RG_W), F32),
                        pltpu.VMEM((2, SUBLANES, RG_W), F32)],
        compiler_params=_params("arbitrary"))(xc, xc, wbd, bias, lam)
    return res[0], res[1]


def _rg_scan_bwd(xc, wbd, bias, lam, hs, dho, name):
    T = xc.shape[0]
    tm = min(256, T)
    nT = T // tm
    tiles = (lambda i: nT - 1 - i, lambda i: i)

    def body(xf_ref, xb_ref, w_ref, b_ref, lam_ref, hfc, hfp, hfn, hbc, hbp, hbn, dof_ref, dob_ref,
             dxf_ref, dxb_ref, dw_ref, db_ref, dlam_ref, a_sc, b_sc, y_sc, carry):
        i = pl.program_id(0)

        @pl.when(i == 0)
        def _():
            carry[...] = jnp.zeros_like(carry)
            dw_ref[...] = jnp.zeros_like(dw_ref)
            db_ref[...] = jnp.zeros_like(db_ref)
            dlam_ref[...] = jnp.zeros_like(dlam_ref)

        vjps, entering = [], []
        for d, (x_ref, do_ref) in enumerate(((xf_ref, dof_ref), (xb_ref, dob_ref))):
            (a, _), vjp = jax.vjp(_rg_gates, x_ref[...], w_ref[d].astype(F32), b_ref[d], lam_ref[d])
            vjps.append(vjp)
            entering.append(carry[d, 0:1, :])
            a_sc[d], b_sc[d] = _local_scan(a, a * do_ref[...], d == 1)
        _group_scan(((0, y_sc.at[0], False), (1, y_sc.at[1], True)), a_sc, b_sc, carry, tm // SUBLANES)

        row = lax.broadcasted_iota(jnp.int32, (tm, RG_W), 0)
        for d, (do_ref, dx_ref, hc, hp, hn, ti) in enumerate(
                ((dof_ref, dxf_ref, hfc, hfp, hfn, nT - 1 - i), (dob_ref, dxb_ref, hbc, hbp, hbn, i))):
            y = y_sc[d]
            if d == 0:
                y_next = jnp.where(row == tm - 1, entering[d], pltpu.roll(y, tm - 1, 0))
            else:
                y_next = jnp.where(row == 0, entering[d], pltpu.roll(y, 1, 0))
            dtot = do_ref[...] + y_next
            ext = _ext(hc[...], hp[...], hn[...], ti > 0, ti < nT - 1)
            hprev = _shifted(ext, -1 if d == 0 else 1, tm)
            dxc, dw, db, dlam = vjps[d]((dtot * hprev, dtot))
            dx_ref[...] = dxc
            dw_ref[d] += dw
            db_ref[d] += db
            dlam_ref[d] += dlam

    full = lambda a: pl.BlockSpec(a.shape, lambda i: (0,) * len(a.shape))
    tok = lambda d: pl.BlockSpec((tm, RG_W), lambda i: (tiles[d](i), 0))
    acc_shapes = [jax.ShapeDtypeStruct((2, RG_W, 2 * RG_W), F32), jax.ShapeDtypeStruct((2, 1, 2 * RG_W), F32),
                  jax.ShapeDtypeStruct((2, 1, RG_W), F32)]
    res = pl.pallas_call(
        body, name=name, grid=(nT,),
        in_specs=([tok(0), tok(1), full(wbd), full(bias), full(lam)]
                  + _halo_specs(tm, T, RG_W, 0, tiles[0]) + _halo_specs(tm, T, RG_W, 0, tiles[1]) + [tok(0), tok(1)]),
        out_specs=[tok(0), tok(1)] + [full(s) for s in acc_shapes],
        out_shape=[jax.ShapeDtypeStruct((T, RG_W), F32)] * 2 + acc_shapes,
        scratch_shapes=[pltpu.VMEM((2, tm, RG_W), F32), pltpu.VMEM((2, tm, RG_W), F32),
                        pltpu.VMEM((2, tm, RG_W), F32), pltpu.VMEM((2, SUBLANES, RG_W), F32)],
        compiler_params=_params("arbitrary"))(xc, xc, wbd, bias, lam, hs[0], hs[0], hs[0], hs[1], hs[1], hs[1],
                                              dho, dho)
    return (res[0], res[1]), res[2], res[3], res[4]


def _chunk_rows(n_chunks, reverse):
    up, down = (lambda c: c), (lambda c: n_chunks - 1 - c)
    return (down, up) if reverse else (up, down)


STEP_CHUNKS = 4
STEP_ROWS = STEP_CHUNKS * CHUNK


def _sub_chunks(ascending):
    order = range(STEP_CHUNKS) if ascending else range(STEP_CHUNKS - 1, -1, -1)
    return [(s, slice(s * CHUNK, (s + 1) * CHUNK)) for s in order]


def _hg_fwd(proj, l0, l1, name):
    T = proj.shape[0]
    nC = T // CHUNK
    nS = nC // STEP_CHUNKS
    H, dk, dv = 4, 128, 128
    rows = _chunk_rows(nS, False)

    def body(qf, ff, vf, qb, fb, vb, l0_ref, l1_ref, of, ob, spf, spb, st):
        @pl.when(pl.program_id(0) == 0)
        def _():
            st[...] = jnp.zeros_like(st)

        for d, (q, f, v, o, sp) in enumerate(((qf, ff, vf, of, spf), (qb, fb, vb, ob, spb))):
            tri, tri_t, mref = _tri_consts(d)
            stp = tuple(st[d, h] for h in range(H))
            for s, r in _sub_chunks(d == 0):
                for h in range(H):
                    sp[s, h] = stp[h]
                o_val, stp = _hg_chunk(q[r, :], f[r, :], v[r, :], l0_ref[...], l1_ref[...], stp, tri, tri_t, mref)
                o[r, :] = o_val
            for h in range(H):
                st[d, h] = stp[h]

    tok = lambda d, col: pl.BlockSpec((STEP_ROWS, HG_W), lambda c: (rows[d](c), col))
    par = pl.BlockSpec((1, HG_W), lambda c: (0, 0))
    state = lambda d: pl.BlockSpec((STEP_CHUNKS, H, dv, dk), lambda c: (rows[d](c), 0, 0, 0))
    res = pl.pallas_call(
        body, name=name, grid=(nS,),
        in_specs=[tok(0, 2), tok(0, 3), tok(0, 5), tok(1, 2), tok(1, 4), tok(1, 5), par, par],
        out_specs=[tok(0, 0), tok(1, 0), state(0), state(1)],
        out_shape=[jax.ShapeDtypeStruct((T, H * dv), F32)] * 2 + [jax.ShapeDtypeStruct((nC, H, dv, dk), F32)] * 2,
        scratch_shapes=[pltpu.VMEM((2, H, dv, dk), F32)],
        compiler_params=_params("arbitrary"))(proj, proj, proj, proj, proj, proj, l0, l1)
    return (res[0], res[1]), (res[2], res[3])


def _hg_bwd(proj, l0, l1, sprev, do, name):
    T = proj.shape[0]
    nC = T // CHUNK
    nS = nC // STEP_CHUNKS
    H, dk, dv = 4, 128, 128
    rows = _chunk_rows(nS, True)

    def body(qf, ff, vf, qb, fb, vb, l0_ref, l1_ref, spf, spb, dof, dob,
             dqf, dff, dvf, dqb, dfb, dvb, dl0_ref, dl1_ref, dst):
        @pl.when(pl.program_id(0) == 0)
        def _():
            dst[...] = jnp.zeros_like(dst)
            dl0_ref[...] = jnp.zeros_like(dl0_ref)
            dl1_ref[...] = jnp.zeros_like(dl1_ref)

        for d, (q, f, v, sp, do_ref, dq_ref, df_ref, dv_ref) in enumerate(
                ((qf, ff, vf, spf, dof, dqf, dff, dvf), (qb, fb, vb, spb, dob, dqb, dfb, dvb))):
            tri, tri_t, mref = _tri_consts(d)
            fn = lambda q_, f_, v_, a0, a1, stp: _hg_chunk(q_, f_, v_, a0, a1, stp, tri, tri_t, mref)
            dstp = tuple(dst[d, h] for h in range(H))
            for s, r in _sub_chunks(d == 1):
                stp = tuple(sp[s, h] for h in range(H))
                _, vjp = jax.vjp(fn, q[r, :], f[r, :], v[r, :], l0_ref[...], l1_ref[...], stp)
                dq, df, dvv, dl0, dl1, dstp = vjp((do_ref[r, :], dstp))
                dq_ref[r, :] = dq.astype(BF16)
                df_ref[r, :] = df.astype(BF16)
                dv_ref[r, :] = dvv.astype(BF16)
                dl0_ref[d] += dl0
                dl1_ref[d] += dl1
            for h in range(H):
                dst[d, h] = dstp[h]

    tok = lambda d, col: pl.BlockSpec((STEP_ROWS, HG_W), lambda c: (rows[d](c), col))
    par = pl.BlockSpec((1, HG_W), lambda c: (0, 0))
    acc = pl.BlockSpec((2, 1, HG_W), lambda c: (0, 0, 0))
    state = lambda d: pl.BlockSpec((STEP_CHUNKS, H, dv, dk), lambda c: (rows[d](c), 0, 0, 0))
    res = pl.pallas_call(
        body, name=name, grid=(nS,),
        in_specs=[tok(0, 2), tok(0, 3), tok(0, 5), tok(1, 2), tok(1, 4), tok(1, 5), par, par,
                  state(0), state(1), tok(0, 0), tok(1, 0)],
        out_specs=[tok(0, 0)] * 3 + [tok(1, 0)] * 3 + [acc, acc],
        out_shape=[jax.ShapeDtypeStruct((T, HG_W), BF16)] * 6 + [jax.ShapeDtypeStruct((2, 1, HG_W), F32)] * 2,
        scratch_shapes=[pltpu.VMEM((2, H, dv, dk), F32)],
        compiler_params=_params("arbitrary"))(proj, proj, proj, proj, proj, proj, l0, l1, sprev[0], sprev[1], do, do)
    return (res[0], res[3]), (res[1], res[4]), (res[2], res[5]), res[6], res[7]


def _gate_logits(proj, wup, bg, name):
    T = proj.shape[0]
    tm = min(512, T)

    def body(lr_ref, w_ref, b_ref, z_ref, lrb_ref):
        lr = lr_ref[...].astype(BF16)
        lrb_ref[...] = lr
        for d in range(2):
            z_ref[d] = _dg(lr, w_ref[d], 1, 0) + b_ref[d]

    return pl.pallas_call(
        body, name=name, grid=(T // tm,),
        in_specs=[pl.BlockSpec((tm, LANES), lambda i: (i, 24)), pl.BlockSpec((2, LANES, 512), lambda i: (0, 0, 0)),
                  pl.BlockSpec((2, 1, 512), lambda i: (0, 0, 0))],
        out_specs=[pl.BlockSpec((2, tm, 512), lambda i: (0, i, 0)), pl.BlockSpec((tm, LANES), lambda i: (i, 0))],
        out_shape=[jax.ShapeDtypeStruct((2, T, 512), F32), jax.ShapeDtypeStruct((T, LANES), BF16)],
        compiler_params=_params("parallel"))(proj, wup, bg)


def _gate_logits_bwd(dz, wup, name):
    T = dz[0].shape[0]
    tm = min(512, T)

    def body(dzf_ref, dzb_ref, w_ref, dlr_ref, db_ref, dzb16_ref):
        @pl.when(pl.program_id(0) == 0)
        def _():
            db_ref[...] = jnp.zeros_like(db_ref)

        acc = jnp.zeros((tm, LANES), F32)
        for d, dz_ref in enumerate((dzf_ref, dzb_ref)):
            g = dz_ref[...]
            gb = g.astype(BF16)
            dzb16_ref[d] = gb
            acc = acc + _dg(gb, w_ref[d], 1, 1)
            db_ref[d] += jnp.sum(g, axis=0, keepdims=True)
        dlr_ref[...] = acc

    tok = pl.BlockSpec((tm, 512), lambda i: (i, 0))
    return pl.pallas_call(
        body, name=name, grid=(T // tm,),
        in_specs=[tok, tok, pl.BlockSpec((2, LANES, 512), lambda i: (0, 0, 0))],
        out_specs=[pl.BlockSpec((tm, LANES), lambda i: (i, 0)), pl.BlockSpec((2, 1, 512), lambda i: (0, 0, 0)),
                   pl.BlockSpec((2, tm, 512), lambda i: (0, i, 0))],
        out_shape=[jax.ShapeDtypeStruct((T, LANES), F32), jax.ShapeDtypeStruct((2, 1, 512), F32),
                   jax.ShapeDtypeStruct((2, T, 512), BF16)],
        compiler_params=_params("arbitrary"))(dz[0], dz[1], wup)


def _gla_fwd(proj, z, name):
    T = proj.shape[0]
    nC = T // CHUNK
    nS = nC // STEP_CHUNKS
    H, dk, dv = 4, 128, 256
    rows = _chunk_rows(nS, False)

    def body(qf, kf, vf, zf, qb, kb, vb, zb, of, ob, spf, spb, st):
        @pl.when(pl.program_id(0) == 0)
        def _():
            st[...] = jnp.zeros_like(st)

        for d, (q, k, v, z_ref, o, sp) in enumerate(((qf, kf, vf, zf, of, spf), (qb, kb, vb, zb, ob, spb))):
            tri, tri_t, mref = _tri_consts(d)
            stp = tuple(st[d, h] for h in range(H))
            for s, r in _sub_chunks(d == 0):
                for h in range(H):
                    sp[s, h] = stp[h]
                o_val, stp = _gla_chunk(q[r, :], k[r, :], v[r, :], z_ref[r, :], stp, tri, tri_t, mref)
                o[r, :] = o_val
            for h in range(H):
                st[d, h] = stp[h]

    tok = lambda d, w, col: pl.BlockSpec((STEP_ROWS, w), lambda c: (rows[d](c), col))
    gate = lambda d: pl.BlockSpec((None, STEP_ROWS, 512), lambda c: (d, rows[d](c), 0))
    state = lambda d: pl.BlockSpec((STEP_CHUNKS, H, dv, dk), lambda c: (rows[d](c), 0, 0, 0))
    res = pl.pallas_call(
        body, name=name, grid=(nS,),
        in_specs=[tok(0, 512, 0), tok(0, 512, 1), tok(0, 1024, 1), gate(0),
                  tok(1, 512, 0), tok(1, 512, 1), tok(1, 1024, 1), gate(1)],
        out_specs=[tok(0, H * dv, 0), tok(1, H * dv, 0), state(0), state(1)],
        out_shape=[jax.ShapeDtypeStruct((T, H * dv), F32)] * 2 + [jax.ShapeDtypeStruct((nC, H, dv, dk), F32)] * 2,
        scratch_shapes=[pltpu.VMEM((2, H, dv, dk), F32)],
        compiler_params=_params("arbitrary"))(proj, proj, proj, z, proj, proj, proj, z)
    return (res[0], res[1]), (res[2], res[3])


def _gla_bwd(proj, z, sprev, do, name):
    T = proj.shape[0]
    nC = T // CHUNK
    nS = nC // STEP_CHUNKS
    H, dk, dv = 4, 128, 256
    rows = _chunk_rows(nS, True)

    def body(qf, kf, vf, zf, qb, kb, vb, zb, spf, spb, dof, dob,
             dqf, dkf, dvf, dzf, dqb, dkb, dvb, dzb, dst):
        @pl.when(pl.program_id(0) == 0)
        def _():
            dst[...] = jnp.zeros_like(dst)

        for d, (q, k, v, z_ref, sp, do_ref, dq_ref, dk_ref, dv_ref, dz_ref) in enumerate(
                ((qf, kf, vf, zf, spf, dof, dqf, dkf, dvf, dzf), (qb, kb, vb, zb, spb, dob, dqb, dkb, dvb, dzb))):
            tri, tri_t, mref = _tri_consts(d)
            fn = lambda q_, k_, v_, z_, stp: _gla_chunk(q_, k_, v_, z_, stp, tri, tri_t, mref)
            dstp = tuple(dst[d, h] for h in range(H))
            for s, r in _sub_chunks(d == 1):
                stp = tuple(sp[s, h] for h in range(H))
                _, vjp = jax.vjp(fn, q[r, :], k[r, :], v[r, :], z_ref[r, :], stp)
                dq, dkk, dvv, dzz, dstp = vjp((do_ref[r, :], dstp))
                dq_ref[r, :] = dq.astype(BF16)
                dk_ref[r, :] = dkk.astype(BF16)
                dv_ref[r, :] = dvv.astype(BF16)
                dz_ref[r, :] = dzz
            for h in range(H):
                dst[d, h] = dstp[h]

    tok = lambda d, w, col: pl.BlockSpec((STEP_ROWS, w), lambda c: (rows[d](c), col))
    gate = lambda d: pl.BlockSpec((None, STEP_ROWS, 512), lambda c: (d, rows[d](c), 0))
    state = lambda d: pl.BlockSpec((STEP_CHUNKS, H, dv, dk), lambda c: (rows[d](c), 0, 0, 0))
    outs = lambda d: [tok(d, 512, 0), tok(d, 512, 0), tok(d, 1024, 0), tok(d, 512, 0)]
    shapes = [jax.ShapeDtypeStruct((T, 512), BF16), jax.ShapeDtypeStruct((T, 512), BF16),
              jax.ShapeDtypeStruct((T, 1024), BF16), jax.ShapeDtypeStruct((T, 512), F32)]
    res = pl.pallas_call(
        body, name=name, grid=(nS,),
        in_specs=[tok(0, 512, 0), tok(0, 512, 1), tok(0, 1024, 1), gate(0),
                  tok(1, 512, 0), tok(1, 512, 1), tok(1, 1024, 1), gate(1),
                  state(0), state(1), tok(0, H * dv, 0), tok(1, H * dv, 0)],
        out_specs=outs(0) + outs(1), out_shape=shapes + shapes,
        scratch_shapes=[pltpu.VMEM((2, H, dv, dk), F32)],
        compiler_params=_params("arbitrary"))(proj, proj, proj, z, proj, proj, proj, z, sprev[0], sprev[1], do, do)
    return (res[0], res[4]), (res[1], res[5]), (res[2], res[6]), (res[3], res[7])


def _l0_combine_fwd(hs, proj, o, gain, name):
    T = proj.shape[0]
    tm = min(512, T)

    def body(hf, hb, ga, of, ob, g, gn, out):
        out[...] = _l0_combine(hf[...], hb[...], ga[...], of[...], ob[...], g[...], gn[...]).astype(BF16)

    tok = pl.BlockSpec((tm, 512), lambda i: (i, 0))
    return pl.pallas_call(
        body, name=name, grid=(T // tm,),
        in_specs=[tok, tok, pl.BlockSpec((tm, 512), lambda i: (i, 1)), tok, tok,
                  pl.BlockSpec((tm, 512), lambda i: (i, 6)), pl.BlockSpec((1, 512), lambda i: (0, 0))],
        out_specs=pl.BlockSpec((tm, 1024), lambda i: (i, 0)),
        out_shape=jax.ShapeDtypeStruct((T, 1024), BF16),
        compiler_params=_params("parallel"))(hs[0], hs[1], proj, o[0], o[1], proj, gain)


def _l0_combine_bwd(hs, proj, o, gain, dh_b, w_out, name):
    T = proj.shape[0]
    tm = min(512, T)

    def body(hf, hb, ga, of, ob, g, gn, dhb_ref, w_ref, dho_ref, dga_ref, do_ref, dg_ref, dgn_ref):
        @pl.when(pl.program_id(0) == 0)
        def _():
            dgn_ref[...] = jnp.zeros_like(dgn_ref)

        _, vjp = jax.vjp(_l0_combine, hf[...], hb[...], ga[...], of[...], ob[...], g[...], gn[...])
        dhf, _, dga, dof, _, dg, dgn = vjp(_dg(dhb_ref[...], w_ref[...], 1, 1))
        dho_ref[...] = dhf
        dga_ref[...] = dga
        do_ref[...] = dof
        dg_ref[...] = dg
        dgn_ref[...] += dgn

    tok = lambda: pl.BlockSpec((tm, 512), lambda i: (i, 0))
    return pl.pallas_call(
        body, name=name, grid=(T // tm,),
        in_specs=[tok(), tok(), pl.BlockSpec((tm, 512), lambda i: (i, 1)), tok(), tok(),
                  pl.BlockSpec((tm, 512), lambda i: (i, 6)), pl.BlockSpec((1, 512), lambda i: (0, 0)),
                  pl.BlockSpec((tm, D_MODEL), lambda i: (i, 0)), pl.BlockSpec(w_out.shape, lambda i: (0, 0))],
        out_specs=[tok(), tok(), tok(), tok(), pl.BlockSpec((1, 512), lambda i: (0, 0))],
        out_shape=[jax.ShapeDtypeStruct((T, 512), F32)] * 4 + [jax.ShapeDtypeStruct((1, 512), F32)],
        compiler_params=_params("arbitrary"))(hs[0], hs[1], proj, o[0], o[1], proj, gain, dh_b, w_out)


def _l0_assemble(dxa, dga, dq, df, dv, dg, name):
    T = dxa.shape[0]
    tm = min(512, T)

    def body(xa, ga, q0, q1, f0, f1, v0, v1, g, out):
        both = lambda a, b: (a[...].astype(F32) + b[...].astype(F32)).astype(BF16)
        out[...] = jnp.concatenate([xa[...].astype(BF16), ga[...].astype(BF16), both(q0, q1), f0[...], f1[...],
                                    both(v0, v1), g[...].astype(BF16)], axis=1)

    tok = lambda: pl.BlockSpec((tm, 512), lambda i: (i, 0))
    return pl.pallas_call(
        body, name=name, grid=(T // tm,),
        in_specs=[tok() for _ in range(9)],
        out_specs=pl.BlockSpec((tm, AB_IN), lambda i: (i, 0)),
        out_shape=jax.ShapeDtypeStruct((T, AB_IN), BF16),
        compiler_params=_params("parallel"))(dxa, dga, dq[0], dq[1], df[0], df[1], dv[0], dv[1], dg)


def _l1_combine_fwd(o, proj, gain, name):
    T = proj.shape[0]
    tm = min(512, T)

    def body(of, ob, r, gn, out):
        out[...] = _l1_combine(of[...], ob[...], r[...], gn[...]).astype(BF16)

    tok = pl.BlockSpec((tm, 1024), lambda i: (i, 0))
    return pl.pallas_call(
        body, name=name, grid=(T // tm,),
        in_specs=[tok, tok, pl.BlockSpec((tm, 1024), lambda i: (i, 2)), pl.BlockSpec((1, 1024), lambda i: (0, 0))],
        out_specs=pl.BlockSpec((tm, 1024), lambda i: (i, 0)),
        out_shape=jax.ShapeDtypeStruct((T, 1024), BF16),
        compiler_params=_params("parallel"))(o[0], o[1], proj, gain)


def _l1_combine_bwd(o, proj, gain, dh_b, w_out, name):
    T = proj.shape[0]
    tm = min(512, T)

    def body(of, ob, r, gn, dhb_ref, w_ref, do_ref, dr_ref, dgn_ref):
        @pl.when(pl.program_id(0) == 0)
        def _():
            dgn_ref[...] = jnp.zeros_like(dgn_ref)

        _, vjp = jax.vjp(_l1_combine, of[...], ob[...], r[...], gn[...])
        dof, _, dr, dgn = vjp(_dg(dhb_ref[...], w_ref[...], 1, 1))
        do_ref[...] = dof
        dr_ref[...] = dr
        dgn_ref[...] += dgn

    tok = lambda: pl.BlockSpec((tm, 1024), lambda i: (i, 0))
    return pl.pallas_call(
        body, name=name, grid=(T // tm,),
        in_specs=[tok(), tok(), pl.BlockSpec((tm, 1024), lambda i: (i, 2)),
                  pl.BlockSpec((1, 1024), lambda i: (0, 0)), tok(), pl.BlockSpec(w_out.shape, lambda i: (0, 0))],
        out_specs=[tok(), tok(), pl.BlockSpec((1, 1024), lambda i: (0, 0))],
        out_shape=[jax.ShapeDtypeStruct((T, 1024), F32)] * 2 + [jax.ShapeDtypeStruct((1, 1024), F32)],
        compiler_params=_params("arbitrary"))(o[0], o[1], proj, gain, dh_b, w_out)


def _l1_assemble(dq, dk, dv, dr, dlr, name):
    T = dr.shape[0]
    tm = min(512, T)

    def body(q0, q1, k0, k1, v0, v1, r, a, out):
        both = lambda x, y: (x[...].astype(F32) + y[...].astype(F32)).astype(BF16)
        out[...] = jnp.concatenate([both(q0, q1), both(k0, k1), both(v0, v1), r[...].astype(BF16),
                                    a[...].astype(BF16)], axis=1)

    tok = lambda w: pl.BlockSpec((tm, w), lambda i: (i, 0))
    return pl.pallas_call(
        body, name=name, grid=(T // tm,),
        in_specs=[tok(512), tok(512), tok(512), tok(512), tok(1024), tok(1024), tok(1024), tok(LANES)],
        out_specs=pl.BlockSpec((tm, GLA_IN_PAD), lambda i: (i, 0)),
        out_shape=jax.ShapeDtypeStruct((T, GLA_IN_PAD), BF16),
        compiler_params=_params("parallel"))(dq[0], dq[1], dk[0], dk[1], dv[0], dv[1], dr, dlr)


HBM_SPEC = pl.BlockSpec(memory_space=pltpu.HBM)


def _place():
    x, y, c = lax.axis_index("x"), lax.axis_index("y"), lax.axis_index("c")
    return x, y, c


def _allgather_vmem(x_shard, name):
    m_per, n = x_shard.shape

    def body(x_ref, out_ref, send_sems, recv_sems, local_sem):
        x, y, c = _place()
        me, sibling = (x, y, c), (x, y, 1 - c)
        chips = [(1 - x, y), (x, 1 - y), (1 - x, 1 - y)]

        def rows(px, py, pc):
            return out_ref.at[pl.ds((4 * px + 2 * py + pc) * m_per, m_per), :]

        def copy(k, block, to, src=None):
            return pltpu.make_async_remote_copy(
                src_ref=rows(*block) if src is None else src, dst_ref=rows(*block),
                send_sem=send_sems.at[k], recv_sem=recv_sems.at[k], device_id=to, device_id_type=MESH)

        mine = pltpu.make_async_copy(x_ref, rows(*me), local_sem)
        mine.start()
        first = [copy(0, me, sibling, src=x_ref)]
        first += [copy(1 + j, me, (*chip, c), src=x_ref) for j, chip in enumerate(chips)]
        for cp in first:
            cp.start()
        passed = [copy(4 + j, (*chip, c), sibling) for j, chip in enumerate(chips)]
        for j, chip in enumerate(chips):
            copy(1 + j, (*chip, c), me).wait_recv()
            passed[j].start()
        copy(0, sibling, me).wait_recv()
        for j, chip in enumerate(chips):
            copy(4 + j, (*chip, 1 - c), me).wait_recv()
        for cp in first + passed:
            cp.wait_send()
        mine.wait()

    vm = pl.BlockSpec(memory_space=pltpu.VMEM)
    return pl.pallas_call(
        body, name=name, in_specs=[vm], out_specs=vm,
        out_shape=jax.ShapeDtypeStruct((N_DEV * m_per, n), x_shard.dtype),
        scratch_shapes=[pltpu.SemaphoreType.DMA((7,)), pltpu.SemaphoreType.DMA((7,)), pltpu.SemaphoreType.DMA],
        compiler_params=pltpu.CompilerParams(has_side_effects=True, vmem_limit_bytes=VMEM_LIMIT))(x_shard)


SEM_SPEC = pl.BlockSpec(memory_space=pltpu.SEMAPHORE)
DATAFLOW_EFFECT = pltpu.SideEffectType.DATAFLOW_SIDE_EFFECTING


def _copies(plan, srcs, lands, send_sems, recv_sems):
    x, y, c = _place()
    return [pltpu.make_async_remote_copy(src_ref=s, dst_ref=d, send_sem=send_sems.at[k], recv_sem=recv_sems.at[k],
                                         device_id=dev, device_id_type=MESH)
            for k, (s, d, dev) in enumerate(plan(srcs, lands, x, y, c))]


def _copies_start(plan, n_copies, srcs, lands, name):
    ns, nl = len(srcs), len(lands)

    def body(*refs):
        send_sems, recv_sems = refs[ns + nl], refs[ns + nl + 1]
        for cp in _copies(plan, refs[:ns], refs[ns:ns + nl], send_sems, recv_sems):
            cp.start()
        refs[-1][...] = jnp.zeros_like(refs[-1])

    arrays = list(srcs) + list(lands)
    res = pl.pallas_call(
        body, name=name,
        in_specs=[HBM_SPEC] * (ns + nl),
        out_specs=tuple([SEM_SPEC, SEM_SPEC] + [HBM_SPEC] * (ns + nl) + [pl.BlockSpec(memory_space=pltpu.VMEM)]),
        out_shape=tuple([pltpu.SemaphoreType.DMA((n_copies,)), pltpu.SemaphoreType.DMA((n_copies,))]
                        + [pltpu.HBM(a.shape, a.dtype) for a in arrays]
                        + [jax.ShapeDtypeStruct((SUBLANES, LANES), F32)]),
        input_output_aliases={i: 2 + i for i in range(ns + nl)},
        compiler_params=pltpu.CompilerParams(has_side_effects=DATAFLOW_EFFECT),
    )(*[pltpu.with_memory_space_constraint(a, pltpu.HBM) for a in arrays])
    return res[0], res[1], list(res[2:2 + ns]), list(res[2 + ns:2 + ns + nl]), res[-1]


def _copies_wait(plan, started, after, name):
    send_sems, recv_sems, srcs, lands, _ = started
    ns, nl = len(srcs), len(lands)

    def body(*refs):
        for cp in _copies(plan, refs[:ns], refs[ns:ns + nl], refs[ns + nl], refs[ns + nl + 1]):
            cp.wait_send()
            cp.wait_recv()

    arrays = list(srcs) + list(lands)
    res = pl.pallas_call(
        body, name=name,
        in_specs=[HBM_SPEC] * (ns + nl) + [SEM_SPEC, SEM_SPEC, pl.BlockSpec(memory_space=pl.ANY)],
        out_specs=tuple([HBM_SPEC] * (ns + nl)),
        out_shape=tuple(pltpu.HBM(a.shape, a.dtype) for a in arrays),
        input_output_aliases={i: i for i in range(ns + nl)},
        compiler_params=pltpu.CompilerParams(has_side_effects=DATAFLOW_EFFECT),
    )(*arrays, send_sems, recv_sems, after)
    return list(res[:ns]), list(res[ns:])


def _after(token, value):
    return value + token[0:1, 0:1].astype(value.dtype)


def _chips(x, y):
    return [(1 - x, y), (x, 1 - y), (1 - x, 1 - y)]


def _plan_gather_first(srcs, lands, x, y, c):
    me = 4 * x + 2 * y + c
    out = []
    for s, l in zip(srcs, lands):
        out.append((s, l.at[me], (x, y, 1 - c)))
        out += [(s, l.at[me], (*chip, c)) for chip in _chips(x, y)]
    return out


def _plan_gather_pass(srcs, lands, x, y, c):
    out = []
    for l in lands:
        for chip in _chips(x, y):
            slot = l.at[4 * chip[0] + 2 * chip[1] + c]
            out.append((slot, slot, (x, y, 1 - c)))
    return out


def _plan_grads_sibling(srcs, lands, x, y, c):
    return [(s.at[2 * q + (1 - c)], l.at[q], (x, y, 1 - c)) for s, l in zip(srcs, lands) for q in range(4)]


def _plan_grads_chips(srcs, lands, x, y, c):
    return [(s.at[2 * chip[0] + chip[1]], l.at[k], (*chip, c))
            for s, l in zip(srcs, lands) for k, chip in enumerate(_chips(x, y))]


def _landing(n_slots, like):
    return [lax.empty((n_slots,) + a.shape[1:], a.dtype) for a in like]


def _sum_slots(g, name):
    _, R, C = g.shape
    tr = min(256, R)
    assert R % tr == 0

    def body(g_ref, o_ref):
        acc = g_ref[0]
        for j in range(1, N_DEV):
            acc = acc + g_ref[j]
        o_ref[...] = acc

    return pl.pallas_call(
        body, name=name, grid=(R // tr,),
        in_specs=[pl.BlockSpec((N_DEV, tr, C), lambda i: (0, i, 0))],
        out_specs=pl.BlockSpec((tr, C), lambda i: (i, 0)),
        out_shape=jax.ShapeDtypeStruct((R, C), F32),
        compiler_params=_params("parallel"))(g)


def _chip_partial(g, r1, place, name):
    _, R, C = g.shape
    tr = min(1024, R)
    assert R % tr == 0

    def body(pl_ref, g_ref, r_ref, pb_ref, pm_ref):
        q = pl.program_id(1)
        s = g_ref[...] + r_ref[...]
        pb_ref[...] = s.astype(BF16)

        @pl.when(q == pl_ref[1])
        def _():
            pm_ref[...] = s

    grid_spec = pltpu.PrefetchScalarGridSpec(
        num_scalar_prefetch=1, grid=(R // tr, 4),
        in_specs=[pl.BlockSpec((None, tr, C), lambda r, q, p: (2 * q + p[0], r, 0)),
                  pl.BlockSpec((None, tr, C), lambda r, q, p: (q, r, 0))],
        out_specs=[pl.BlockSpec((None, tr, C), lambda r, q, p: (q, r, 0)),
                   pl.BlockSpec((tr, C), lambda r, q, p: (r, 0))])
    return pl.pallas_call(
        body, name=name, grid_spec=grid_spec,
        out_shape=[jax.ShapeDtypeStruct((4, R, C), BF16), jax.ShapeDtypeStruct((R, C), F32)],
        compiler_params=_params("parallel", "arbitrary"))(place, g, r1)


def _adamw_update(w, g, m, v, grad_ref, delta_ref, m_ref, v_ref):
    mn = ADAM_B1 * m + (1.0 - ADAM_B1) * g
    vn = ADAM_B2 * v + (1.0 - ADAM_B2) * jnp.square(g)
    m_hat = mn / (1.0 - ADAM_B1 ** ADAM_STEP)
    v_hat = vn / (1.0 - ADAM_B2 ** ADAM_STEP)
    grad_ref[...] = g
    delta_ref[...] = -ADAM_LR * (m_hat / (jnp.sqrt(v_hat) + ADAM_EPS) + ADAM_WD * w)
    m_ref[...] = mn
    v_ref[...] = vn


def _adamw_whole(w, g, m, v, name):
    def body(w_ref, g_ref, m_ref, v_ref, go, do, mo, vo):
        _adamw_update(w_ref[...], g_ref[...], m_ref[...], v_ref[...], go, do, mo, vo)

    vm = pl.BlockSpec(memory_space=pltpu.VMEM)
    return pl.pallas_call(body, name=name, in_specs=[vm] * 4, out_specs=[vm] * 4,
                          out_shape=[jax.ShapeDtypeStruct(w.shape, F32)] * 4)(w, g, m, v)


def _adamw(w, gparts, m, v, name):
    _, R, C = w.shape
    tr = min(256, R)
    assert R % tr == 0

    def body(w_ref, g0_ref, g3_ref, m_ref, v_ref, go, do, mo, vo):
        g = g0_ref[...]
        for k in range(3):
            g = g + g3_ref[k].astype(F32)
        _adamw_update(w_ref[...], g, m_ref[...], v_ref[...], go, do, mo, vo)

    blk = pl.BlockSpec((tr, C), lambda i: (i, 0))
    wblk = pl.BlockSpec((None, tr, C), lambda i: (0, i, 0))
    return pl.pallas_call(
        body, name=name, grid=(R // tr,),
        in_specs=[wblk, blk, pl.BlockSpec((3, tr, C), lambda i: (0, i, 0)), wblk, wblk], out_specs=[wblk] * 4,
        out_shape=[jax.ShapeDtypeStruct(w.shape, F32)] * 4,
        compiler_params=_params("parallel"))(w, gparts[0], gparts[1], m, v)


def _adamw_layers(w, parts, m, v, name):
    _, R, C = w.shape
    tr = min(256, R)
    assert R % tr == 0

    def body(w_ref, p0, r0, p1, r1, m_ref, v_ref, go, do, mo, vo):
        gs = []
        for p, r in ((p0, r0), (p1, r1)):
            g = p[...]
            for k in range(3):
                g = g + r[k].astype(F32)
            gs.append(g)
        g = jnp.where(pl.program_id(0) == 0, gs[0], gs[1])
        _adamw_update(w_ref[...], g, m_ref[...], v_ref[...], go, do, mo, vo)

    lay = pl.BlockSpec((None, tr, C), lambda l, i: (l, i, 0))
    one = pl.BlockSpec((tr, C), lambda l, i: (i, 0))
    three = pl.BlockSpec((3, tr, C), lambda l, i: (0, i, 0))
    return pl.pallas_call(
        body, name=name, grid=(2, R // tr), in_specs=[lay, one, three, one, three, lay, lay],
        out_specs=[lay] * 4, out_shape=[jax.ShapeDtypeStruct((2, R, C), F32)] * 4,
        compiler_params=_params("parallel", "parallel"))(w, parts[0][0], parts[0][1], parts[1][0], parts[1][1], m, v)


SMALL_SHARDED = ("rg_conv_w", "rg_b_a", "rg_b_x", "rg_lambda", "gla_w_gate_up", "gla_b_gate", "gla_norm")
SMALL_REPLICATED = ("norm_mix", "norm_mlp", "norm_final", "rg_conv_b", "rg_w_a", "rg_w_x", "hg_lb_logits", "hg_norm")
WEIGHT_NAMES = ("norm_mix", "norm_mlp", "norm_final", "mlp_w1", "mlp_w2", "ab_w_in", "ab_w_out", "rg_conv_w",
                "rg_conv_b", "rg_w_a", "rg_b_a", "rg_w_x", "rg_b_x", "rg_lambda", "hg_lb_logits", "hg_norm",
                "gla_w_in", "gla_w_out", "gla_w_gate_up", "gla_b_gate", "gla_norm")


def _rows128(a):
    return a.reshape(-1, LANES)


def _part_rows(a):
    return -(-(a.size // LANES) // SUBLANES) * SUBLANES


def _pack_rows(arrays, pad_to=SUBLANES):
    parts = [jnp.pad(_rows128(a), ((0, _part_rows(a) - a.size // LANES), (0, 0))) for a in arrays]
    total = sum(p.shape[0] for p in parts)
    extra = (-total) % pad_to
    if extra:
        parts.append(jnp.zeros((extra, LANES), parts[0].dtype))
    return jnp.concatenate(parts, axis=0)


def _unshard_last(g, shape_local):
    nd = len(shape_local)
    t = g.reshape((N_DEV,) + tuple(shape_local))
    t = jnp.moveaxis(t, 0, nd - 1)
    return t.reshape(tuple(shape_local[:-1]) + (N_DEV * shape_local[-1],))


def _block_diag(w):
    eye = jnp.eye(8, dtype=w.dtype)
    return (w[:, :, :, None, :] * eye[None, :, None, :, None]).reshape(2, RG_W, RG_W)


def _block_diag_extract(dw):
    t = dw.reshape(2, 8, 64, 8, 64)
    return jnp.moveaxis(jnp.diagonal(t, axis1=1, axis2=3), -1, 1)


def kernel(x, norm_mix, norm_mlp, norm_final, mlp_w1, mlp_w2, ab_w_in, ab_w_out, rg_conv_w, rg_conv_b, rg_w_a, rg_b_a, rg_w_x, rg_b_x, rg_lambda, hg_lb_logits, hg_norm, gla_w_in, gla_w_out, gla_w_gate_up, gla_b_gate, gla_norm, loss_target, m_norm_mix, m_norm_mlp, m_norm_final, m_mlp_w1, m_mlp_w2, m_ab_w_in, m_ab_w_out, m_rg_conv_w, m_rg_conv_b, m_rg_w_a, m_rg_b_a, m_rg_w_x, m_rg_b_x, m_rg_lambda, m_hg_lb_logits, m_hg_norm, m_gla_w_in, m_gla_w_out, m_gla_w_gate_up, m_gla_b_gate, m_gla_norm, v_norm_mix, v_norm_mlp, v_norm_final, v_mlp_w1, v_mlp_w2, v_ab_w_in, v_ab_w_out, v_rg_conv_w, v_rg_conv_b, v_rg_w_a, v_rg_b_a, v_rg_w_x, v_rg_b_x, v_rg_lambda, v_hg_lb_logits, v_hg_norm, v_gla_w_in, v_gla_w_out, v_gla_w_gate_up, v_gla_b_gate, v_gla_norm):
    w_loc = dict(norm_mix=norm_mix, norm_mlp=norm_mlp, norm_final=norm_final, mlp_w1=mlp_w1, mlp_w2=mlp_w2,
                 ab_w_in=ab_w_in, ab_w_out=ab_w_out, rg_conv_w=rg_conv_w, rg_conv_b=rg_conv_b, rg_w_a=rg_w_a,
                 rg_b_a=rg_b_a, rg_w_x=rg_w_x, rg_b_x=rg_b_x, rg_lambda=rg_lambda, hg_lb_logits=hg_lb_logits,
                 hg_norm=hg_norm, gla_w_in=gla_w_in, gla_w_out=gla_w_out, gla_w_gate_up=gla_w_gate_up,
                 gla_b_gate=gla_b_gate, gla_norm=gla_norm)
    m_loc = dict(norm_mix=m_norm_mix, norm_mlp=m_norm_mlp, norm_final=m_norm_final, mlp_w1=m_mlp_w1,
                 mlp_w2=m_mlp_w2, ab_w_in=m_ab_w_in, ab_w_out=m_ab_w_out, rg_conv_w=m_rg_conv_w,
                 rg_conv_b=m_rg_conv_b, rg_w_a=m_rg_w_a, rg_b_a=m_rg_b_a, rg_w_x=m_rg_w_x, rg_b_x=m_rg_b_x,
                 rg_lambda=m_rg_lambda, hg_lb_logits=m_hg_lb_logits, hg_norm=m_hg_norm, gla_w_in=m_gla_w_in,
                 gla_w_out=m_gla_w_out, gla_w_gate_up=m_gla_w_gate_up, gla_b_gate=m_gla_b_gate,
                 gla_norm=m_gla_norm)
    v_loc = dict(norm_mix=v_norm_mix, norm_mlp=v_norm_mlp, norm_final=v_norm_final, mlp_w1=v_mlp_w1,
                 mlp_w2=v_mlp_w2, ab_w_in=v_ab_w_in, ab_w_out=v_ab_w_out, rg_conv_w=v_rg_conv_w,
                 rg_conv_b=v_rg_conv_b, rg_w_a=v_rg_w_a, rg_b_a=v_rg_b_a, rg_w_x=v_rg_w_x, rg_b_x=v_rg_b_x,
                 rg_lambda=v_rg_lambda, hg_lb_logits=v_hg_lb_logits, hg_norm=v_hg_norm, gla_w_in=v_gla_w_in,
                 gla_w_out=v_gla_w_out, gla_w_gate_up=v_gla_w_gate_up, gla_b_gate=v_gla_b_gate,
                 gla_norm=v_gla_norm)

    T = x.shape[1]
    h0 = x.reshape(T, D_MODEL)
    target = loss_target.reshape(T, D_MODEL)
    ax, ay, ac = lax.axis_index("x"), lax.axis_index("y"), lax.axis_index("c")
    dev = 4 * ax + 2 * ay + ac
    place = jnp.stack([ac, 2 * ax + ay]).astype(jnp.int32)

    abin_shard = ab_w_in[0].astype(BF16)
    first_started = _copies_start(_plan_gather_first, 4, [abin_shard], _landing(N_DEV, [abin_shard[None]]),
                                  "ag_first_start")
    rest_shards = [mlp_w1[0].astype(BF16), mlp_w2[0].astype(BF16), gla_w_in[0].astype(BF16),
                   gla_w_out[0].astype(BF16), mlp_w1[1].astype(BF16), mlp_w2[1].astype(BF16),
                   _after(first_started[4], ab_w_out[0].astype(BF16))]
    ag_started = _copies_start(_plan_gather_first, 4 * len(rest_shards), rest_shards,
                               _landing(N_DEV, [s[None] for s in rest_shards]), "ag_rest_start")

    small_local = [w_loc[n] for n in SMALL_SHARDED]
    small_g = _allgather_vmem(_pack_rows(small_local, 8), "ag_small")
    small_g = small_g.reshape(N_DEV, -1, LANES)
    full = {}
    off = 0
    for n, a in zip(SMALL_SHARDED, small_local):
        full[n] = _unshard_last(small_g[:, off:off + a.size // LANES].reshape(N_DEV, a.size), a.shape)
        off += _part_rows(a)
    conv_w = full["rg_conv_w"][0]
    b_a, b_x, lam = full["rg_b_a"][0], full["rg_b_x"][0], full["rg_lambda"][0]
    w_up, b_gate, g_norm = full["gla_w_gate_up"][0], full["gla_b_gate"][0], full["gla_norm"]

    cw8 = jnp.pad(conv_w, ((0, 4), (0, 0)))
    wbd = jnp.concatenate([_block_diag(rg_w_a[0]), _block_diag(rg_w_x[0])], axis=2).astype(BF16)
    rg_bias = jnp.concatenate([b_a, b_x], axis=1).reshape(2, 1, 2 * RG_W)
    lam3 = lam.reshape(2, 1, RG_W)
    l0, l1 = hg_lb_logits[0:1], hg_lb_logits[1:2]
    wup_pad = jnp.zeros((2, LANES, 512), F32).at[0, 0:16].set(w_up[0]).at[1, 16:32].set(w_up[1])
    bg3 = b_gate.reshape(2, 1, 512)
    nmix0, nmix1 = norm_mix[0:1], norm_mix[1:2]
    nmlp0, nmlp1 = norm_mlp[0:1], norm_mlp[1:2]
    nfin = norm_final.reshape(1, D_MODEL)

    prepared = (ag_started[4] + cw8[:, 0:LANES] + wup_pad[0, 0:SUBLANES, 0:LANES] + rg_bias[0, :, 0:LANES]
                + wbd[0, 0:SUBLANES, 0:LANES].astype(F32) + lam3[0, :, 0:LANES] + bg3[0, :, 0:LANES])
    (abin_shard,), abin_l = _copies_wait(_plan_gather_first, first_started, prepared, "ag_first_wait")
    first_pass = _copies_start(_plan_gather_pass, 3, [], abin_l, "ag_first_pass_start")
    _, (abin_g,) = _copies_wait(_plan_gather_pass, first_pass, first_pass[4], "ag_first_pass_wait")
    abin_g = lax.dynamic_update_index_in_dim(abin_g, abin_shard, dev, 0)
    wab_in = jnp.transpose(abin_g, (1, 0, 2)).reshape(D_MODEL, AB_IN)
    proj0, y0 = _norm_matmul(h0, _after(ag_started[4], nmix0), wab_in, "l0_in_proj")
    xc = _rg_conv_fwd(proj0, cw8, rg_conv_b, "rg_conv")
    hs = _rg_scan_fwd(xc, wbd, rg_bias, lam3, "rg_scan")
    o_hg, s_hg = _hg_fwd(proj0, l0, l1, "hg_chunks")
    both_done = hs[0][0:SUBLANES, 0:LANES] + o_hg[0][0:SUBLANES, 0:LANES]
    rest_shards, rest_lands = _copies_wait(_plan_gather_first, ag_started, both_done, "ag_rest_wait")
    pass_started = _copies_start(_plan_gather_pass, 3 * len(rest_lands), [], rest_lands, "ag_pass_start")
    mixin0 = _l0_combine_fwd(hs, proj0, o_hg, _after(pass_started[4], hg_norm), "l0_combine")
    _, rest_g = _copies_wait(_plan_gather_pass, pass_started, mixin0, "ag_pass_wait")
    rest_g = [lax.dynamic_update_index_in_dim(g, s, dev, 0) for g, s in zip(rest_g, rest_shards)]
    wab_out = rest_g[6].reshape(D_MODEL, D_MODEL)
    h1 = _matmul_res(mixin0, wab_out, h0, "l0_out_proj")
    w1g = (rest_g[0], rest_g[4])
    w2f = (rest_g[1].reshape(D_FF, D_MODEL), rest_g[5].reshape(D_FF, D_MODEL))
    wgla_in = jnp.pad(jnp.transpose(rest_g[2], (1, 0, 2)).reshape(D_MODEL, GLA_IN),
                      ((0, 0), (0, GLA_IN_PAD - GLA_IN)))
    wgla_out = rest_g[3].reshape(D_MODEL, D_MODEL)
    h2, pre0, ym0 = _mlp_fwd(h1, nmlp0, w1g[0], w2f[0], "mlp0")
    proj1, y1 = _norm_matmul(h2, nmix1, wgla_in, "l1_in_proj")
    z_gate, lr_b = _gate_logits(proj1, wup_pad, bg3, "gla_gate_logits")
    o_gla, s_gla = _gla_fwd(proj1, z_gate, "gla_chunks")
    mixin1 = _l1_combine_fwd(o_gla, proj1, g_norm, "l1_combine")
    h3 = _matmul_res(mixin1, wgla_out, h2, "l1_out_proj")
    h4, pre1, ym1 = _mlp_fwd(h3, nmlp1, w1g[1], w2f[1], "mlp1")
    loss_blk, dh4, dh4b, d_nfin = _final_loss(h4, nfin, target, "final_loss")

    dh3, dh3b, dpre1, act1, d_nmlp1 = _mlp_bwd(dh4, dh4b, h3, nmlp1, pre1, w1g[1], w2f[1], "mlp1_bwd")
    g_w1_1 = _wgrad(ym1, dpre1, 512, "mlp1_dw1", sharded_cols=True)
    g_w2_1 = _wgrad(act1, dh4b, 512, "mlp1_dw2")
    g_gla_out = _wgrad(mixin1, dh3b, 512, "l1_out_dw")
    do_gla, dr, d_gnorm = _l1_combine_bwd(o_gla, proj1, g_norm, dh3b, wgla_out, "l1_combine_bwd")
    dq1, dk1, dv1, dz_gate = _gla_bwd(proj1, z_gate, s_gla, do_gla, "gla_chunks_bwd")
    dlr1, d_bg, dz_b = _gate_logits_bwd(dz_gate, wup_pad, "gla_gate_logits_bwd")
    d_wup = [_wgrad(lr_b, dz_b[d], 512, "gla_gate_dw%d" % d) for d in range(2)]
    dproj1 = _l1_assemble(dq1, dk1, dv1, dr, dlr1, "l1_assemble")
    dh2, dh2b, d_nmix1 = _dgrad_norm(dproj1, wgla_in, h2, nmix1, dh3, "l1_in_dgrad")
    g_gla_in = _wgrad(y1, dproj1, 640, "l1_in_dw")

    def reduce_start(grads, tag):
        return _copies_start(_plan_grads_sibling, 4 * len(grads), grads, _landing(4, grads), "rs_%s_d2d_start" % tag)

    def reduce_mid(started, after, tag):
        grads, got = _copies_wait(_plan_grads_sibling, started, after, "rs_%s_d2d_wait" % tag)
        parts = [_chip_partial(g, r, place, "rs_%s_partial%d" % (tag, a)) for a, (g, r) in enumerate(zip(grads, got))]
        pb = [p[0] for p in parts]
        return _copies_start(_plan_grads_chips, 3 * len(pb), pb, _landing(3, pb), "rs_%s_ici_start" % tag), \
            [p[1] for p in parts]

    def reduce_end(started, mine, after, tag):
        _, got = _copies_wait(_plan_grads_chips, started, after, "rs_%s_ici_wait" % tag)
        return list(zip(mine, got))

    slots_l1 = [g_w1_1, g_w2_1.reshape(N_DEV, 512, D_MODEL),
                jnp.transpose(g_gla_in[:, :GLA_IN].reshape(D_MODEL, N_DEV, GLA_IN // N_DEV), (1, 0, 2)),
                g_gla_out.reshape(N_DEV, 128, D_MODEL)]
    ra_d2d = reduce_start(slots_l1, "l1")

    dh1, dh1b, dpre0, act0, d_nmlp0 = _mlp_bwd(dh2, dh2b, h1, _after(ra_d2d[4], nmlp0), pre0, w1g[0], w2f[0],
                                               "mlp0_bwd")
    g_w1_0 = _wgrad(ym0, dpre0, 512, "mlp0_dw1", sharded_cols=True)
    g_w2_0 = _wgrad(act0, dh2b, 512, "mlp0_dw2")
    ra_ici, ra_mine = reduce_mid(ra_d2d, g_w2_0, "l1")
    g_ab_out = _wgrad(mixin0, dh1b, 512, "l0_out_dw")
    rb_d2d = reduce_start([g_w1_0, g_w2_0.reshape(N_DEV, 512, D_MODEL), g_ab_out.reshape(N_DEV, 128, D_MODEL)],
                          "mlp0")
    dho, dga, do_hg, dg_gate, d_hgnorm = _l0_combine_bwd(
        hs, proj0, o_hg, _after(rb_d2d[4], _after(ra_ici[4], hg_norm)), dh1b, wab_out, "l0_combine_bwd")
    dxc, d_wbd, d_rgb, d_lam = _rg_scan_bwd(xc, wbd, rg_bias, lam3, hs, dho, "rg_scan_bwd")
    dxa, d_cw8, d_cb = _rg_conv_bwd(dxc, proj0, cw8, "rg_conv_bwd")
    dq0, df0, dv0, d_l0, d_l1 = _hg_bwd(proj0, l0, l1, s_hg, do_hg, "hg_chunks_bwd")
    rb_ici, rb_mine = reduce_mid(rb_d2d, d_l0, "mlp0")
    dproj0 = _l0_assemble(dxa, dga, dq0, df0, dv0, dg_gate, "l0_assemble")
    dx, _, d_nmix0 = _dgrad_norm(dproj0, wab_in, h0, _after(rb_ici[4], nmix0), dh1, "l0_in_dgrad")

    d_wa = _block_diag_extract(d_wbd[:, :, :RG_W])[None]
    d_wx = _block_diag_extract(d_wbd[:, :, RG_W:])[None]
    small_full = {
        "norm_mix": jnp.concatenate([d_nmix0, d_nmix1], axis=0), "norm_mlp": jnp.concatenate([d_nmlp0, d_nmlp1], axis=0),
        "norm_final": d_nfin.reshape(D_MODEL), "rg_conv_b": d_cb, "rg_w_a": d_wa, "rg_w_x": d_wx,
        "hg_lb_logits": jnp.concatenate([d_l0[0] + d_l0[1], d_l1[0] + d_l1[1]], axis=0), "hg_norm": d_hgnorm,
        "rg_conv_w": d_cw8[0:4][None], "rg_b_a": d_rgb[:, 0, :RG_W][None], "rg_b_x": d_rgb[:, 0, RG_W:][None],
        "rg_lambda": d_lam[:, 0, :][None],
        "gla_w_gate_up": jnp.stack([d_wup[0][0:16], d_wup[1][16:32]])[None], "gla_b_gate": d_bg[:, 0, :][None],
        "gla_norm": d_gnorm}
    small_names = SMALL_REPLICATED + SMALL_SHARDED
    packed = _pack_rows([loss_blk] + [small_full[n] for n in small_names], 256)
    ar_first = _copies_start(_plan_gather_first, 4, [packed], _landing(N_DEV, [packed[None]]), "ar_small_start")

    g_ab_in = _wgrad(y0, dproj0, 512, "l0_in_dw", behind=ar_first[4])
    rc_d2d = reduce_start([jnp.transpose(g_ab_in.reshape(D_MODEL, N_DEV, AB_IN // N_DEV), (1, 0, 2))], "ab")
    (packed,), ar_lands = _copies_wait(_plan_gather_first, ar_first, rc_d2d[4], "ar_small_wait")
    ar_pass = _copies_start(_plan_gather_pass, 3, [], ar_lands, "ar_small_pass_start")
    rc_ici, rc_mine = reduce_mid(rc_d2d, ar_pass[4], "ab")
    _, (ar_gathered,) = _copies_wait(_plan_gather_pass, ar_pass, rc_ici[4], "ar_small_pass_wait")
    summed = _sum_slots(lax.dynamic_update_index_in_dim(ar_gathered, packed, dev, 0), "ar_small_sum")
    loss = summed[0, 0]

    pieces_l1 = reduce_end(ra_ici, ra_mine, rc_ici[4], "l1")
    res_gla_in = _adamw(gla_w_in, pieces_l1[2], m_gla_w_in, v_gla_w_in, "adamw_gla_in")
    res_gla_out = _adamw(gla_w_out, pieces_l1[3], m_gla_w_out, v_gla_w_out, "adamw_gla_out")
    pieces_mlp0 = reduce_end(rb_ici, rb_mine, res_gla_out[0], "mlp0")
    res_w1 = _adamw_layers(mlp_w1, (pieces_mlp0[0], pieces_l1[0]), m_mlp_w1, v_mlp_w1, "adamw_mlp_w1")
    res_w2 = _adamw_layers(mlp_w2, (pieces_mlp0[1], pieces_l1[1]), m_mlp_w2, v_mlp_w2, "adamw_mlp_w2")
    res = {"mlp_w1": tuple(res_w1), "mlp_w2": tuple(res_w2),
           "gla_w_in": tuple(res_gla_in), "gla_w_out": tuple(res_gla_out),
           "ab_w_out": tuple(_adamw(ab_w_out, pieces_mlp0[2], m_ab_w_out, v_ab_w_out, "adamw_ab_out"))}

    off = SUBLANES
    for n in small_names:
        a = small_full[n]
        gfull = summed[off:off + a.size // LANES].reshape(a.shape)
        off += _part_rows(a)
        local = w_loc[n].shape
        if n in SMALL_SHARDED:
            gfull = lax.dynamic_slice_in_dim(gfull, dev * local[-1], local[-1], axis=gfull.ndim - 1)
        flat = (-1, local[-1])
        outs = _adamw_whole(w_loc[n].reshape(flat), gfull.reshape(flat), m_loc[n].reshape(flat),
                            v_loc[n].reshape(flat), "adamw_" + n)
        res[n] = tuple(o.reshape(local) for o in outs)
    others_done = (res_w1[1][0, 0:SUBLANES, 0:LANES] + res_w2[1][0, 0:SUBLANES, 0:LANES]
                   + res_gla_in[1][0, 0:SUBLANES, 0:LANES])
    pieces_ab = reduce_end(rc_ici, rc_mine, others_done, "ab")
    res["ab_w_in"] = tuple(_adamw(ab_w_in, pieces_ab[0], m_ab_w_in, v_ab_w_in, "adamw_ab_in"))

    grad_x = dx.reshape(1, T, D_MODEL)
    out = [loss, grad_x]
    for k in range(4):
        out += [res[n][k] for n in WEIGHT_NAMES]
    return tuple(out)
```

```python
import jax
import jax.numpy as jnp
from jax import lax
from jax.experimental import pallas as pl
from jax.experimental.pallas import tpu as pltpu

F32, BF16 = jnp.float32, jnp.bfloat16
MESH = pl.DeviceIdType.MESH

D_MODEL = 1024
D_FF = 4096
RG_W = 512
HG_W = 512
CHUNK = 64
EPS = 1e-6
RG_C = 8.0
AB_IN = 3584
GLA_IN = 3104
GLA_IN_PAD = 3200
N_DEV = 8
LANES = 128
SUBLANES = 8
VMEM_LIMIT = 48 * 1024 * 1024

ADAM_LR, ADAM_B1, ADAM_B2, ADAM_EPS, ADAM_WD, ADAM_STEP = 0.001, 0.9, 0.999, 1e-08, 0.01, 10


def _params(*sem):
    return pltpu.CompilerParams(dimension_semantics=sem, vmem_limit_bytes=VMEM_LIMIT)


def _dg(a, b, ca, cb):
    return lax.dot_general(a.astype(BF16), b.astype(BF16), (((ca,), (cb,)), ((), ())),
                           preferred_element_type=F32)


@jax.custom_vjp
def _mm_nn(a, b):
    return _dg(a, b, 1, 0)


_mm_nn.defvjp(lambda a, b: (_dg(a, b, 1, 0), (a, b)),
              lambda res, g: (_dg(g, res[1], 1, 1), _dg(res[0], g, 0, 0)))


@jax.custom_vjp
def _mm_nt(a, b):
    return _dg(a, b, 1, 1)


_mm_nt.defvjp(lambda a, b: (_dg(a, b, 1, 1), (a, b)),
              lambda res, g: (_dg(g, res[1], 1, 0), _dg(g, res[0], 0, 0)))


@jax.custom_vjp
def _mm_tn(a, b):
    return _dg(a, b, 0, 0)


_mm_tn.defvjp(lambda a, b: (_dg(a, b, 0, 0), (a, b)),
              lambda res, g: (_dg(res[1], g, 1, 1), _dg(res[0], g, 1, 0)))


def _tri_dot(tri, x):
    hi = x.astype(BF16)
    lo = (x - hi.astype(F32)).astype(BF16)
    t = tri.astype(BF16)
    return jnp.dot(t, hi, preferred_element_type=F32) + jnp.dot(t, lo, preferred_element_type=F32)


@jax.custom_vjp
def _cum(tri, tri_t, x):
    return _tri_dot(tri, x)


_cum.defvjp(lambda tri, tri_t, x: (_tri_dot(tri, x), (tri, tri_t)),
            lambda res, g: (jnp.zeros_like(res[0]), jnp.zeros_like(res[1]), _tri_dot(res[1], g)))


def _sig(x):
    return 1.0 / (1.0 + jnp.exp(-x))


def _gelu(x):
    return 0.5 * x * (1.0 + jnp.tanh(0.7978845608028654 * (x + 0.044715 * (x * x * x))))


def _softplus(z):
    return jnp.maximum(z, 0.0) + jnp.log(1.0 + jnp.exp(-jnp.abs(z)))


def _rms(x):
    return lax.rsqrt(jnp.mean(x * x, axis=-1, keepdims=True) + EPS)


def _rmsnorm_bwd(x, gain, dy):
    r = _rms(x)
    xh = x * r
    dgain = jnp.sum(dy * xh, axis=0, keepdims=True)
    dxh = dy * gain
    dx = r * (dxh - xh * jnp.mean(dxh * xh, axis=-1, keepdims=True))
    return dx, dgain


def _headnorm(o, gain, n_heads, hd):
    parts = []
    for h in range(n_heads):
        oh = o[:, h * hd:(h + 1) * hd]
        parts.append(oh * _rms(oh))
    return jnp.concatenate(parts, axis=1) * gain


def _tri_consts(d):
    row = lax.broadcasted_iota(jnp.int32, (CHUNK, CHUNK), 0)
    col = lax.broadcasted_iota(jnp.int32, (CHUNK, CHUNK), 1)
    ge = (row >= col).astype(F32)
    le = (row <= col).astype(F32)
    r1 = lax.broadcasted_iota(jnp.int32, (CHUNK, 1), 0)
    if d == 0:
        return ge, le, (r1 <= CHUNK // 2).astype(F32)
    return le, ge, (r1 >= CHUNK // 2 - 1).astype(F32)


def _chunk_core(qh, k, v, logf, st_prev, tri, tri_t, mref, n_heads, dk, dv):
    cum = _cum(tri, tri_t, logf)
    ref = jnp.sum(logf * mref, axis=0, keepdims=True)
    last = jnp.sum(logf, axis=0, keepdims=True)
    q_in = qh * jnp.exp(cum - ref)
    k_in = k * jnp.exp(ref - cum)
    k_st = k * jnp.exp(last - cum)
    q_dec = qh * jnp.exp(cum)
    decay = jnp.exp(last)
    outs, sts = [], []
    for h in range(n_heads):
        sk = slice(h * dk, (h + 1) * dk)
        sv = slice(h * dv, (h + 1) * dv)
        sc = _mm_nt(q_in[:, sk], k_in[:, sk]) * tri
        o = _mm_nn(sc, v[:, sv]) + _mm_nt(q_dec[:, sk], st_prev[h])
        sts.append(st_prev[h] * decay[:, sk] + _mm_tn(v[:, sv], k_st[:, sk]))
        outs.append(o)
    return jnp.concatenate(outs, axis=1), tuple(sts)


def _hg_chunk(q, f, v, l0, l1, st_prev, tri, tri_t, mref):
    lb = _sig(l0 - l1)
    sg = _sig(f)
    qh = q * _sig(q)
    logf = jnp.log(lb + (1.0 - lb) * sg)
    k = (1.0 - lb) * (1.0 - sg)
    return _chunk_core(qh, k, v, logf, st_prev, tri, tri_t, mref, 4, 128, 128)


def _gla_chunk(q, k, v, z, st_prev, tri, tri_t, mref):
    logf = (jnp.minimum(z, 0.0) - jnp.log(1.0 + jnp.exp(-jnp.abs(z)))) * (1.0 / 16.0)
    qh = q * (128.0 ** -0.5)
    return _chunk_core(qh, k, v, logf, st_prev, tri, tri_t, mref, 4, 128, 256)


def _rg_gates(xc, wbd, bias, lam):
    z = _mm_nn(xc, wbd) + bias
    r = _sig(z[:, :RG_W])
    i = _sig(z[:, RG_W:])
    log_a = -RG_C * r * _softplus(-lam)
    a = jnp.exp(log_a)
    x2 = 2.0 * log_a
    neg_expm1 = jnp.where(x2 > -1e-2, -(x2 + 0.5 * x2 * x2 + x2 * x2 * x2 * (1.0 / 6.0)), 1.0 - jnp.exp(x2))
    u = jnp.sqrt(neg_expm1) * (i * xc)
    return a, u


def _l0_combine(hf, hb, ga, of, ob, g, gain):
    ya = (hf + hb) * _gelu(ga)
    yb = _headnorm(of + ob, gain, 4, 128) * (g * _sig(g))
    return jnp.concatenate([ya, yb], axis=1)


def _l1_combine(of, ob, r, gain):
    return _headnorm(of + ob, gain, 4, 256) * (r * _sig(r))


def _norm_matmul(h, gain, w, name):
    T, D = h.shape
    N = w.shape[1]
    tm = min(512, T)

    def body(h_ref, g_ref, w_ref, o_ref, y_ref):
        x = h_ref[...]
        y = (x * _rms(x) * g_ref[...]).astype(BF16)
        y_ref[...] = y
        o_ref[...] = jnp.dot(y, w_ref[...], preferred_element_type=F32)

    return pl.pallas_call(
        body, name=name, grid=(T // tm,),
        in_specs=[pl.BlockSpec((tm, D), lambda i: (i, 0)), pl.BlockSpec((1, D), lambda i: (0, 0)),
                  pl.BlockSpec((D, N), lambda i: (0, 0))],
        out_specs=[pl.BlockSpec((tm, N), lambda i: (i, 0)), pl.BlockSpec((tm, D), lambda i: (i, 0))],
        out_shape=[jax.ShapeDtypeStruct((T, N), F32), jax.ShapeDtypeStruct((T, D), BF16)],
        compiler_params=_params("parallel"))(h, gain, w)


def _matmul_res(a, w, res, name):
    T, K = a.shape
    N = w.shape[1]
    tm = min(512, T)

    def body(a_ref, w_ref, r_ref, o_ref):
        o_ref[...] = r_ref[...] + jnp.dot(a_ref[...], w_ref[...], preferred_element_type=F32)

    return pl.pallas_call(
        body, name=name, grid=(T // tm,),
        in_specs=[pl.BlockSpec((tm, K), lambda i: (i, 0)), pl.BlockSpec((K, N), lambda i: (0, 0)),
                  pl.BlockSpec((tm, N), lambda i: (i, 0))],
        out_specs=pl.BlockSpec((tm, N), lambda i: (i, 0)),
        out_shape=jax.ShapeDtypeStruct((T, N), F32),
        compiler_params=_params("parallel"))(a, w, res)


def _dgrad_norm(dproj, w, h, gain, dres, name):
    T, N = dproj.shape
    D = w.shape[0]
    tm = min(512, T)

    def body(dp_ref, w_ref, h_ref, g_ref, dr_ref, dh_ref, dhb_ref, dg_ref):
        @pl.when(pl.program_id(0) == 0)
        def _():
            dg_ref[...] = jnp.zeros_like(dg_ref)

        dy = _dg(dp_ref[...], w_ref[...], 1, 1)
        dx, dgain = _rmsnorm_bwd(h_ref[...], g_ref[...], dy)
        dh = dr_ref[...] + dx
        dh_ref[...] = dh
        dhb_ref[...] = dh.astype(BF16)
        dg_ref[...] += dgain

    return pl.pallas_call(
        body, name=name, grid=(T // tm,),
        in_specs=[pl.BlockSpec((tm, N), lambda i: (i, 0)), pl.BlockSpec((D, N), lambda i: (0, 0)),
                  pl.BlockSpec((tm, D), lambda i: (i, 0)), pl.BlockSpec((1, D), lambda i: (0, 0)),
                  pl.BlockSpec((tm, D), lambda i: (i, 0))],
        out_specs=[pl.BlockSpec((tm, D), lambda i: (i, 0)), pl.BlockSpec((tm, D), lambda i: (i, 0)),
                   pl.BlockSpec((1, D), lambda i: (0, 0))],
        out_shape=[jax.ShapeDtypeStruct((T, D), F32), jax.ShapeDtypeStruct((T, D), BF16),
                   jax.ShapeDtypeStruct((1, D), F32)],
        compiler_params=_params("arbitrary"))(dproj, w, h, gain, dres)


def _wgrad(a, b, tn, name, sharded_cols=False, behind=None):
    T, K = a.shape
    N = b.shape[1]
    tk = min(1024, K)

    def body(a_ref, b_ref, *rest):
        rest[-1][...] = _dg(a_ref[...], b_ref[...], 0, 0)

    if sharded_cols:
        out_spec = pl.BlockSpec((None, tk, tn), lambda k, n: (n, k, 0))
        out_shape = jax.ShapeDtypeStruct((N // tn, K, tn), F32)
    else:
        out_spec = pl.BlockSpec((tk, tn), lambda k, n: (k, n))
        out_shape = jax.ShapeDtypeStruct((K, N), F32)
    in_specs = [pl.BlockSpec((T, tk), lambda k, n: (0, k)), pl.BlockSpec((T, tn), lambda k, n: (0, n))]
    args = [a, b]
    if behind is not None:
        in_specs.append(pl.BlockSpec((SUBLANES, LANES), lambda k, n: (0, 0)))
        args.append(behind)
    return pl.pallas_call(
        body, name=name, grid=(K // tk, N // tn), in_specs=in_specs, out_specs=out_spec, out_shape=out_shape,
        compiler_params=_params("parallel", "parallel"))(*args)


def _resident(shape):
    return pl.BlockSpec(shape, lambda i: (0,) * len(shape), pipeline_mode=pl.Buffered(1))


def _mlp_fwd(h, gain, w1g, w2, name):
    T, D = h.shape
    nf, _, tf = w1g.shape
    tm = min(512, T)

    def body(h_ref, g_ref, w1_ref, w2_ref, o_ref, pre_ref, y_ref):
        x = h_ref[...]
        y = (x * _rms(x) * g_ref[...]).astype(BF16)
        y_ref[...] = y
        acc = x
        for j in range(nf):
            cols = slice(j * tf, (j + 1) * tf)
            pre = jnp.dot(y, w1_ref[j], preferred_element_type=F32)
            pre_ref[:, cols] = pre.astype(BF16)
            act = jnp.square(jnp.maximum(pre, 0.0)).astype(BF16)
            acc = acc + jnp.dot(act, w2_ref[cols, :], preferred_element_type=F32)
        o_ref[...] = acc

    return pl.pallas_call(
        body, name=name, grid=(T // tm,),
        in_specs=[pl.BlockSpec((tm, D), lambda i: (i, 0)), pl.BlockSpec((1, D), lambda i: (0, 0)),
                  _resident(w1g.shape), _resident(w2.shape)],
        out_specs=[pl.BlockSpec((tm, D), lambda i: (i, 0)), pl.BlockSpec((tm, nf * tf), lambda i: (i, 0)),
                   pl.BlockSpec((tm, D), lambda i: (i, 0))],
        out_shape=[jax.ShapeDtypeStruct((T, D), F32), jax.ShapeDtypeStruct((T, nf * tf), BF16),
                   jax.ShapeDtypeStruct((T, D), BF16)],
        compiler_params=_params("parallel"))(h, gain, w1g, w2)


def _mlp_bwd(dout, dout_b, h, gain, pre, w1g, w2, name):
    T, D = h.shape
    nf, _, tf = w1g.shape
    tm = min(256, T)

    def body(do_ref, dob_ref, h_ref, g_ref, pre_ref, w1_ref, w2_ref, dh_ref, dhb_ref, dpre_ref, act_ref, dg_ref):
        @pl.when(pl.program_id(0) == 0)
        def _():
            dg_ref[...] = jnp.zeros_like(dg_ref)

        dob = dob_ref[...]
        dy = None
        for j in range(nf):
            cols = slice(j * tf, (j + 1) * tf)
            rp = jnp.maximum(pre_ref[:, cols].astype(F32), 0.0)
            dpre = (_dg(dob, w2_ref[cols, :], 1, 1) * (2.0 * rp)).astype(BF16)
            dpre_ref[:, cols] = dpre
            act_ref[:, cols] = (rp * rp).astype(BF16)
            part = _dg(dpre, w1_ref[j], 1, 1)
            dy = part if dy is None else dy + part
        dx, dgain = _rmsnorm_bwd(h_ref[...], g_ref[...], dy)
        dh = do_ref[...] + dx
        dh_ref[...] = dh
        dhb_ref[...] = dh.astype(BF16)
        dg_ref[...] += dgain

    tok = lambda w: pl.BlockSpec((tm, w), lambda i: (i, 0))
    return pl.pallas_call(
        body, name=name, grid=(T // tm,),
        in_specs=[tok(D), tok(D), tok(D), pl.BlockSpec((1, D), lambda i: (0, 0)), tok(nf * tf),
                  _resident(w1g.shape), _resident(w2.shape)],
        out_specs=[tok(D), tok(D), tok(nf * tf), tok(nf * tf), pl.BlockSpec((1, D), lambda i: (0, 0))],
        out_shape=[jax.ShapeDtypeStruct((T, D), F32), jax.ShapeDtypeStruct((T, D), BF16),
                   jax.ShapeDtypeStruct((T, nf * tf), BF16),
                   jax.ShapeDtypeStruct((T, nf * tf), BF16), jax.ShapeDtypeStruct((1, D), F32)],
        compiler_params=_params("arbitrary"))(dout, dout_b, h, gain, pre, w1g, w2)


def _final_loss(h, gain, target, name):
    T, D = h.shape
    tm = min(512, T)

    def body(h_ref, g_ref, t_ref, l_ref, dh_ref, dhb_ref, dg_ref):
        @pl.when(pl.program_id(0) == 0)
        def _():
            l_ref[...] = jnp.zeros_like(l_ref)
            dg_ref[...] = jnp.zeros_like(dg_ref)

        x = h_ref[...]
        err = x * _rms(x) * g_ref[...] - t_ref[...]
        l_ref[...] += 0.5 * jnp.sum(jnp.mean(err * err, axis=-1, keepdims=True), axis=0, keepdims=True)
        dx, dgain = _rmsnorm_bwd(x, g_ref[...], err * (1.0 / D))
        dh_ref[...] = dx
        dhb_ref[...] = dx.astype(BF16)
        dg_ref[...] += dgain

    return pl.pallas_call(
        body, name=name, grid=(T // tm,),
        in_specs=[pl.BlockSpec((tm, D), lambda i: (i, 0)), pl.BlockSpec((1, D), lambda i: (0, 0)),
                  pl.BlockSpec((tm, D), lambda i: (i, 0))],
        out_specs=[pl.BlockSpec((SUBLANES, LANES), lambda i: (0, 0)), pl.BlockSpec((tm, D), lambda i: (i, 0)),
                   pl.BlockSpec((tm, D), lambda i: (i, 0)), pl.BlockSpec((1, D), lambda i: (0, 0))],
        out_shape=[jax.ShapeDtypeStruct((SUBLANES, LANES), F32), jax.ShapeDtypeStruct((T, D), F32),
                   jax.ShapeDtypeStruct((T, D), BF16), jax.ShapeDtypeStruct((1, D), F32)],
        compiler_params=_params("arbitrary"))(h, gain, target)


def _halo_specs(tm, T, width, col, tile=lambda i: i):
    r8 = tm // SUBLANES
    nb8 = T // SUBLANES
    return [pl.BlockSpec((tm, width), lambda i: (tile(i), col)),
            pl.BlockSpec((SUBLANES, width), lambda i: (jnp.maximum(tile(i) * r8 - 1, 0), col)),
            pl.BlockSpec((SUBLANES, width), lambda i: (jnp.minimum((tile(i) + 1) * r8, nb8 - 1), col))]


def _ext(cur, prev, nxt, has_prev, has_next):
    return jnp.concatenate([jnp.where(has_prev, prev, 0.0), cur, jnp.where(has_next, nxt, 0.0)], axis=0)


def _shifted(ext, offset, tm):
    n = ext.shape[0]
    sh = (-offset) % n
    r = ext if sh == 0 else pltpu.roll(ext, sh, 0)
    return r[SUBLANES:SUBLANES + tm]


def _rg_conv_fwd(proj, cw8, cb, name):
    T = proj.shape[0]
    tm = min(512, T)
    nT = T // tm

    def body(cur_ref, prev_ref, next_ref, w_ref, b_ref, o_ref):
        i = pl.program_id(0)
        ext = _ext(cur_ref[...], prev_ref[...], next_ref[...], i > 0, i < nT - 1)
        acc = jnp.broadcast_to(b_ref[...], (tm, RG_W))
        for k in range(4):
            acc = acc + w_ref[k:k + 1, :] * _shifted(ext, k - 2, tm)
        o_ref[...] = acc

    return pl.pallas_call(
        body, name=name, grid=(nT,),
        in_specs=_halo_specs(tm, T, RG_W, 0) + [pl.BlockSpec((SUBLANES, RG_W), lambda i: (0, 0)),
                                                pl.BlockSpec((1, RG_W), lambda i: (0, 0))],
        out_specs=pl.BlockSpec((tm, RG_W), lambda i: (i, 0)),
        out_shape=jax.ShapeDtypeStruct((T, RG_W), F32),
        compiler_params=_params("parallel"))(proj, proj, proj, cw8, cb)


def _rg_conv_bwd(dxc, proj, cw8, name):
    T = proj.shape[0]
    tm = min(512, T)
    nT = T // tm

    def body(a0, p0, n0, a1, p1, n1, xa, xp, xn, w_ref, dxa_ref, dw_ref, db_ref):
        i = pl.program_id(0)

        @pl.when(i == 0)
        def _():
            dw_ref[...] = jnp.zeros_like(dw_ref)
            db_ref[...] = jnp.zeros_like(db_ref)

        has_p, has_n = i > 0, i < nT - 1
        cur = a0[...] + a1[...]
        dext = _ext(cur, p0[...] + p1[...], n0[...] + n1[...], has_p, has_n)
        xext = _ext(xa[...], xp[...], xn[...], has_p, has_n)
        acc = jnp.zeros((tm, RG_W), F32)
        rows = []
        for k in range(4):
            acc = acc + w_ref[k:k + 1, :] * _shifted(dext, 2 - k, tm)
            rows.append(jnp.sum(cur * _shifted(xext, k - 2, tm), axis=0, keepdims=True))
        dxa_ref[...] = acc
        dw_ref[...] += jnp.concatenate(rows + [jnp.zeros((4, RG_W), F32)], axis=0)
        db_ref[...] += jnp.sum(cur, axis=0, keepdims=True)

    return pl.pallas_call(
        body, name=name, grid=(nT,),
        in_specs=(_halo_specs(tm, T, RG_W, 0) + _halo_specs(tm, T, RG_W, 0)
                  + _halo_specs(tm, T, RG_W, 0) + [pl.BlockSpec((SUBLANES, RG_W), lambda i: (0, 0))]),
        out_specs=[pl.BlockSpec((tm, RG_W), lambda i: (i, 0)), pl.BlockSpec((SUBLANES, RG_W), lambda i: (0, 0)),
                   pl.BlockSpec((1, RG_W), lambda i: (0, 0))],
        out_shape=[jax.ShapeDtypeStruct((T, RG_W), F32), jax.ShapeDtypeStruct((SUBLANES, RG_W), F32),
                   jax.ShapeDtypeStruct((1, RG_W), F32)],
        compiler_params=_params("arbitrary"))(dxc[0], dxc[0], dxc[0], dxc[1], dxc[1], dxc[1], proj, proj, proj, cw8)


def _local_scan(a, b, ascending):
    n = a.shape[0]
    pos = jnp.bitwise_and(lax.broadcasted_iota(jnp.int32, a.shape, 0), SUBLANES - 1)
    for s in (1, 2, 4):
        sh = s if ascending else n - s
        ok = (pos >= s) if ascending else (pos < SUBLANES - s)
        a_sh, b_sh = pltpu.roll(a, sh, 0), pltpu.roll(b, sh, 0)
        b = jnp.where(ok, a * b_sh + b, b)
        a = jnp.where(ok, a * a_sh, a)
    return a, b


def _group_scan(chains, a_sc, b_sc, carry, n_groups):
    def step(g, hs):
        new = []
        for (d, out_ref, asc), h in zip(chains, hs):
            r0 = pl.multiple_of((g if asc else n_groups - 1 - g) * SUBLANES, SUBLANES)
            out_ref[pl.ds(r0, SUBLANES), :] = a_sc[d, pl.ds(r0, SUBLANES), :] * h + b_sc[d, pl.ds(r0, SUBLANES), :]
            new.append(out_ref[pl.ds(r0 + (SUBLANES - 1 if asc else 0), 1), :])
        return tuple(new)

    hs = lax.fori_loop(0, n_groups, step, tuple(carry[d, 0:1, :] for d, _, _ in chains))
    for (d, _, _), h in zip(chains, hs):
        carry[d, 0:1, :] = h


def _rg_scan_fwd(xc, wbd, bias, lam, name):
    T = xc.shape[0]
    tm = min(512, T)
    nT = T // tm

    def body(xf_ref, xb_ref, w_ref, b_ref, lam_ref, hf_ref, hb_ref, a_sc, b_sc, carry):
        @pl.when(pl.program_id(0) == 0)
        def _():
            carry[...] = jnp.zeros_like(carry)

        for d, x_ref in enumerate((xf_ref, xb_ref)):
            a, u = _rg_gates(x_ref[...], w_ref[d], b_ref[d], lam_ref[d])
            a_sc[d], b_sc[d] = _local_scan(a, u, d == 0)
        _group_scan(((0, hf_ref, True), (1, hb_ref, False)), a_sc, b_sc, carry, tm // SUBLANES)

    full = lambda a: pl.BlockSpec(a.shape, lambda i: (0,) * len(a.shape))
    res = pl.pallas_call(
        body, name=name, grid=(nT,),
        in_specs=[pl.BlockSpec((tm, RG_W), lambda i: (i, 0)), pl.BlockSpec((tm, RG_W), lambda i: (nT - 1 - i, 0)),
                  full(wbd), full(bias), full(lam)],
        out_specs=[pl.BlockSpec((tm, RG_W), lambda i: (i, 0)), pl.BlockSpec((tm, RG_W), lambda i: (nT - 1 - i, 0))],
        out_shape=[jax.ShapeDtypeStruct((T, RG_W), F32)] * 2,
        scratch_shapes=[pltpu.VMEM((2, tm, RG_W), F32), pltpu.VMEM((2, tm, RG_W), F32),
                        pltpu.VMEM((2, SUBLANES, RG_W), F32)],
        compiler_params=_params("arbitrary"))(xc, xc, wbd, bias, lam)
    return res[0], res[1]


def _rg_scan_bwd(xc, wbd, bias, lam, hs, dho, name):
    T = xc.shape[0]
    tm = min(256, T)
    nT = T // tm
    tiles = (lambda i: nT - 1 - i, lambda i: i)

    def body(xf_ref, xb_ref, w_ref, b_ref, lam_ref, hfc, hfp, hfn, hbc, hbp, hbn, dof_ref, dob_ref,
             dxf_ref, dxb_ref, dw_ref, db_ref, dlam_ref, a_sc, b_sc, y_sc, carry):
        i = pl.program_id(0)

        @pl.when(i == 0)
        def _():
            carry[...] = jnp.zeros_like(carry)
            dw_ref[...] = jnp.zeros_like(dw_ref)
            db_ref[...] = jnp.zeros_like(db_ref)
            dlam_ref[...] = jnp.zeros_like(dlam_ref)

        vjps, entering = [], []
        for d, (x_ref, do_ref) in enumerate(((xf_ref, dof_ref), (xb_ref, dob_ref))):
            (a, _), vjp = jax.vjp(_rg_gates, x_ref[...], w_ref[d].astype(F32), b_ref[d], lam_ref[d])
            vjps.append(vjp)
            entering.append(carry[d, 0:1, :])
            a_sc[d], b_sc[d] = _local_scan(a, a * do_ref[...], d == 1)
        _group_scan(((0, y_sc.at[0], False), (1, y_sc.at[1], True)), a_sc, b_sc, carry, tm // SUBLANES)

        row = lax.broadcasted_iota(jnp.int32, (tm, RG_W), 0)
        for d, (do_ref, dx_ref, hc, hp, hn, ti) in enumerate(
                ((dof_ref, dxf_ref, hfc, hfp, hfn, nT - 1 - i), (dob_ref, dxb_ref, hbc, hbp, hbn, i))):
            y = y_sc[d]
            if d == 0:
                y_next = jnp.where(row == tm - 1, entering[d], pltpu.roll(y, tm - 1, 0))
            else:
                y_next = jnp.where(row == 0, entering[d], pltpu.roll(y, 1, 0))
            dtot = do_ref[...] + y_next
            ext = _ext(hc[...], hp[...], hn[...], ti > 0, ti < nT - 1)
            hprev = _shifted(ext, -1 if d == 0 else 1, tm)
            dxc, dw, db, dlam = vjps[d]((dtot * hprev, dtot))
            dx_ref[...] = dxc
            dw_ref[d] += dw
            db_ref[d] += db
            dlam_ref[d] += dlam

    full = lambda a: pl.BlockSpec(a.shape, lambda i: (0,) * len(a.shape))
    tok = lambda d: pl.BlockSpec((tm, RG_W), lambda i: (tiles[d](i), 0))
    acc_shapes = [jax.ShapeDtypeStruct((2, RG_W, 2 * RG_W), F32), jax.ShapeDtypeStruct((2, 1, 2 * RG_W), F32),
                  jax.ShapeDtypeStruct((2, 1, RG_W), F32)]
    res = pl.pallas_call(
        body, name=name, grid=(nT,),
        in_specs=([tok(0), tok(1), full(wbd), full(bias), full(lam)]
                  + _halo_specs(tm, T, RG_W, 0, tiles[0]) + _halo_specs(tm, T, RG_W, 0, tiles[1]) + [tok(0), tok(1)]),
        out_specs=[tok(0), tok(1)] + [full(s) for s in acc_shapes],
        out_shape=[jax.ShapeDtypeStruct((T, RG_W), F32)] * 2 + acc_shapes,
        scratch_shapes=[pltpu.VMEM((2, tm, RG_W), F32), pltpu.VMEM((2, tm, RG_W), F32),
                        pltpu.VMEM((2, tm, RG_W), F32), pltpu.VMEM((2, SUBLANES, RG_W), F32)],
        compiler_params=_params("arbitrary"))(xc, xc, wbd, bias, lam, hs[0], hs[0], hs[0], hs[1], hs[1], hs[1],
                                              dho, dho)
    return (res[0], res[1]), res[2], res[3], res[4]


def _chunk_rows(n_chunks, reverse):
    up, down = (lambda c: c), (lambda c: n_chunks - 1 - c)
    return (down, up) if reverse else (up, down)


STEP_CHUNKS = 4
STEP_ROWS = STEP_CHUNKS * CHUNK


def _sub_chunks(ascending):
    order = range(STEP_CHUNKS) if ascending else range(STEP_CHUNKS - 1, -1, -1)
    return [(s, slice(s * CHUNK, (s + 1) * CHUNK)) for s in order]


def _hg_fwd(proj, l0, l1, name):
    T = proj.shape[0]
    nC = T // CHUNK
    nS = nC // STEP_CHUNKS
    H, dk, dv = 4, 128, 128
    rows = _chunk_rows(nS, False)

    def body(qf, ff, vf, qb, fb, vb, l0_ref, l1_ref, of, ob, spf, spb, st):
        @pl.when(pl.program_id(0) == 0)
        def _():
            st[...] = jnp.zeros_like(st)

        for d, (q, f, v, o, sp) in enumerate(((qf, ff, vf, of, spf), (qb, fb, vb, ob, spb))):
            tri, tri_t, mref = _tri_consts(d)
            stp = tuple(st[d, h] for h in range(H))
            for s, r in _sub_chunks(d == 0):
                for h in range(H):
                    sp[s, h] = stp[h]
                o_val, stp = _hg_chunk(q[r, :], f[r, :], v[r, :], l0_ref[...], l1_ref[...], stp, tri, tri_t, mref)
                o[r, :] = o_val
            for h in range(H):
                st[d, h] = stp[h]

    tok = lambda d, col: pl.BlockSpec((STEP_ROWS, HG_W), lambda c: (rows[d](c), col))
    par = pl.BlockSpec((1, HG_W), lambda c: (0, 0))
    state = lambda d: pl.BlockSpec((STEP_CHUNKS, H, dv, dk), lambda c: (rows[d](c), 0, 0, 0))
    res = pl.pallas_call(
        body, name=name, grid=(nS,),
        in_specs=[tok(0, 2), tok(0, 3), tok(0, 5), tok(1, 2), tok(1, 4), tok(1, 5), par, par],
        out_specs=[tok(0, 0), tok(1, 0), state(0), state(1)],
        out_shape=[jax.ShapeDtypeStruct((T, H * dv), F32)] * 2 + [jax.ShapeDtypeStruct((nC, H, dv, dk), F32)] * 2,
        scratch_shapes=[pltpu.VMEM((2, H, dv, dk), F32)],
        compiler_params=_params("arbitrary"))(proj, proj, proj, proj, proj, proj, l0, l1)
    return (res[0], res[1]), (res[2], res[3])


def _hg_bwd(proj, l0, l1, sprev, do, name):
    T = proj.shape[0]
    nC = T // CHUNK
    nS = nC // STEP_CHUNKS
    H, dk, dv = 4, 128, 128
    rows = _chunk_rows(nS, True)

    def body(qf, ff, vf, qb, fb, vb, l0_ref, l1_ref, spf, spb, dof, dob,
             dqf, dff, dvf, dqb, dfb, dvb, dl0_ref, dl1_ref, dst):
        @pl.when(pl.program_id(0) == 0)
        def _():
            dst[...] = jnp.zeros_like(dst)
            dl0_ref[...] = jnp.zeros_like(dl0_ref)
            dl1_ref[...] = jnp.zeros_like(dl1_ref)

        for d, (q, f, v, sp, do_ref, dq_ref, df_ref, dv_ref) in enumerate(
                ((qf, ff, vf, spf, dof, dqf, dff, dvf), (qb, fb, vb, spb, dob, dqb, dfb, dvb))):
            tri, tri_t, mref = _tri_consts(d)
            fn = lambda q_, f_, v_, a0, a1, stp: _hg_chunk(q_, f_, v_, a0, a1, stp, tri, tri_t, mref)
            dstp = tuple(dst[d, h] for h in range(H))
            for s, r in _sub_chunks(d == 1):
                stp = tuple(sp[s, h] for h in range(H))
                _, vjp = jax.vjp(fn, q[r, :], f[r, :], v[r, :], l0_ref[...], l1_ref[...], stp)
                dq, df, dvv, dl0, dl1, dstp = vjp((do_ref[r, :], dstp))
                dq_ref[r, :] = dq.astype(BF16)
                df_ref[r, :] = df.astype(BF16)
                dv_ref[r, :] = dvv.astype(BF16)
                dl0_ref[d] += dl0
                dl1_ref[d] += dl1
            for h in range(H):
                dst[d, h] = dstp[h]

    tok = lambda d, col: pl.BlockSpec((STEP_ROWS, HG_W), lambda c: (rows[d](c), col))
    par = pl.BlockSpec((1, HG_W), lambda c: (0, 0))
    acc = pl.BlockSpec((2, 1, HG_W), lambda c: (0, 0, 0))
    state = lambda d: pl.BlockSpec((STEP_CHUNKS, H, dv, dk), lambda c: (rows[d](c), 0, 0, 0))
    res = pl.pallas_call(
        body, name=name, grid=(nS,),
        in_specs=[tok(0, 2), tok(0, 3), tok(0, 5), tok(1, 2), tok(1, 4), tok(1, 5), par, par,
                  state(0), state(1), tok(0, 0), tok(1, 0)],
        out_specs=[tok(0, 0)] * 3 + [tok(1, 0)] * 3 + [acc, acc],
        out_shape=[jax.ShapeDtypeStruct((T, HG_W), BF16)] * 6 + [jax.ShapeDtypeStruct((2, 1, HG_W), F32)] * 2,
        scratch_shapes=[pltpu.VMEM((2, H, dv, dk), F32)],
        compiler_params=_params("arbitrary"))(proj, proj, proj, proj, proj, proj, l0, l1, sprev[0], sprev[1], do, do)
    return (res[0], res[3]), (res[1], res[4]), (res[2], res[5]), res[6], res[7]


def _gate_logits(proj, wup, bg, name):
    T = proj.shape[0]
    tm = min(512, T)

    def body(lr_ref, w_ref, b_ref, z_ref, lrb_ref):
        lr = lr_ref[...].astype(BF16)
        lrb_ref[...] = lr
        for d in range(2):
            z_ref[d] = _dg(lr, w_ref[d], 1, 0) + b_ref[d]

    return pl.pallas_call(
        body, name=name, grid=(T // tm,),
        in_specs=[pl.BlockSpec((tm, LANES), lambda i: (i, 24)), pl.BlockSpec((2, LANES, 512), lambda i: (0, 0, 0)),
                  pl.BlockSpec((2, 1, 512), lambda i: (0, 0, 0))],
        out_specs=[pl.BlockSpec((2, tm, 512), lambda i: (0, i, 0)), pl.BlockSpec((tm, LANES), lambda i: (i, 0))],
        out_shape=[jax.ShapeDtypeStruct((2, T, 512), F32), jax.ShapeDtypeStruct((T, LANES), BF16)],
        compiler_params=_params("parallel"))(proj, wup, bg)


def _gate_logits_bwd(dz, wup, name):
    T = dz[0].shape[0]
    tm = min(512, T)

    def body(dzf_ref, dzb_ref, w_ref, dlr_ref, db_ref, dzb16_ref):
        @pl.when(pl.program_id(0) == 0)
        def _():
            db_ref[...] = jnp.zeros_like(db_ref)

        acc = jnp.zeros((tm, LANES), F32)
        for d, dz_ref in enumerate((dzf_ref, dzb_ref)):
            g = dz_ref[...]
            gb = g.astype(BF16)
            dzb16_ref[d] = gb
            acc = acc + _dg(gb, w_ref[d], 1, 1)
            db_ref[d] += jnp.sum(g, axis=0, keepdims=True)
        dlr_ref[...] = acc

    tok = pl.BlockSpec((tm, 512), lambda i: (i, 0))
    return pl.pallas_call(
        body, name=name, grid=(T // tm,),
        in_specs=[tok, tok, pl.BlockSpec((2, LANES, 512), lambda i: (0, 0, 0))],
        out_specs=[pl.BlockSpec((tm, LANES), lambda i: (i, 0)), pl.BlockSpec((2, 1, 512), lambda i: (0, 0, 0)),
                   pl.BlockSpec((2, tm, 512), lambda i: (0, i, 0))],
        out_shape=[jax.ShapeDtypeStruct((T, LANES), F32), jax.ShapeDtypeStruct((2, 1, 512), F32),
                   jax.ShapeDtypeStruct((2, T, 512), BF16)],
        compiler_params=_params("arbitrary"))(dz[0], dz[1], wup)


def _gla_fwd(proj, z, name):
    T = proj.shape[0]
    nC = T // CHUNK
    nS = nC // STEP_CHUNKS
    H, dk, dv = 4, 128, 256
    rows = _chunk_rows(nS, False)

    def body(qf, kf, vf, zf, qb, kb, vb, zb, of, ob, spf, spb, st):
        @pl.when(pl.program_id(0) == 0)
        def _():
            st[...] = jnp.zeros_like(st)

        for d, (q, k, v, z_ref, o, sp) in enumerate(((qf, kf, vf, zf, of, spf), (qb, kb, vb, zb, ob, spb))):
            tri, tri_t, mref = _tri_consts(d)
            stp = tuple(st[d, h] for h in range(H))
            for s, r in _sub_chunks(d == 0):
                for h in range(H):
                    sp[s, h] = stp[h]
                o_val, stp = _gla_chunk(q[r, :], k[r, :], v[r, :], z_ref[r, :], stp, tri, tri_t, mref)
                o[r, :] = o_val
            for h in range(H):
                st[d, h] = stp[h]

    tok = lambda d, w, col: pl.BlockSpec((STEP_ROWS, w), lambda c: (rows[d](c), col))
    gate = lambda d: pl.BlockSpec((None, STEP_ROWS, 512), lambda c: (d, rows[d](c), 0))
    state = lambda d: pl.BlockSpec((STEP_CHUNKS, H, dv, dk), lambda c: (rows[d](c), 0, 0, 0))
    res = pl.pallas_call(
        body, name=name, grid=(nS,),
        in_specs=[tok(0, 512, 0), tok(0, 512, 1), tok(0, 1024, 1), gate(0),
                  tok(1, 512, 0), tok(1, 512, 1), tok(1, 1024, 1), gate(1)],
        out_specs=[tok(0, H * dv, 0), tok(1, H * dv, 0), state(0), state(1)],
        out_shape=[jax.ShapeDtypeStruct((T, H * dv), F32)] * 2 + [jax.ShapeDtypeStruct((nC, H, dv, dk), F32)] * 2,
        scratch_shapes=[pltpu.VMEM((2, H, dv, dk), F32)],
        compiler_params=_params("arbitrary"))(proj, proj, proj, z, proj, proj, proj, z)
    return (res[0], res[1]), (res[2], res[3])


def _gla_bwd(proj, z, sprev, do, name):
    T = proj.shape[0]
    nC = T // CHUNK
    nS = nC // STEP_CHUNKS
    H, dk, dv = 4, 128, 256
    rows = _chunk_rows(nS, True)

    def body(qf, kf, vf, zf, qb, kb, vb, zb, spf, spb, dof, dob,
             dqf, dkf, dvf, dzf, dqb, dkb, dvb, dzb, dst):
        @pl.when(pl.program_id(0) == 0)
        def _():
            dst[...] = jnp.zeros_like(dst)

        for d, (q, k, v, z_ref, sp, do_ref, dq_ref, dk_ref, dv_ref, dz_ref) in enumerate(
                ((qf, kf, vf, zf, spf, dof, dqf, dkf, dvf, dzf), (qb, kb, vb, zb, spb, dob, dqb, dkb, dvb, dzb))):
            tri, tri_t, mref = _tri_consts(d)
            fn = lambda q_, k_, v_, z_, stp: _gla_chunk(q_, k_, v_, z_, stp, tri, tri_t, mref)
            dstp = tuple(dst[d, h] for h in range(H))
            for s, r in _sub_chunks(d == 1):
                stp = tuple(sp[s, h] for h in range(H))
                _, vjp = jax.vjp(fn, q[r, :], k[r, :], v[r, :], z_ref[r, :], stp)
                dq, dkk, dvv, dzz, dstp = vjp((do_ref[r, :], dstp))
                dq_ref[r, :] = dq.astype(BF16)
                dk_ref[r, :] = dkk.astype(BF16)
                dv_ref[r, :] = dvv.astype(BF16)
                dz_ref[r, :] = dzz
            for h in range(H):
                dst[d, h] = dstp[h]

    tok = lambda d, w, col: pl.BlockSpec((STEP_ROWS, w), lambda c: (rows[d](c), col))
    gate = lambda d: pl.BlockSpec((None, STEP_ROWS, 512), lambda c: (d, rows[d](c), 0))
    state = lambda d: pl.BlockSpec((STEP_CHUNKS, H, dv, dk), lambda c: (rows[d](c), 0, 0, 0))
    outs = lambda d: [tok(d, 512, 0), tok(d, 512, 0), tok(d, 1024, 0), tok(d, 512, 0)]
    shapes = [jax.ShapeDtypeStruct((T, 512), BF16), jax.ShapeDtypeStruct((T, 512), BF16),
              jax.ShapeDtypeStruct((T, 1024), BF16), jax.ShapeDtypeStruct((T, 512), F32)]
    res = pl.pallas_call(
        body, name=name, grid=(nS,),
        in_specs=[tok(0, 512, 0), tok(0, 512, 1), tok(0, 1024, 1), gate(0),
                  tok(1, 512, 0), tok(1, 512, 1), tok(1, 1024, 1), gate(1),
                  state(0), state(1), tok(0, H * dv, 0), tok(1, H * dv, 0)],
        out_specs=outs(0) + outs(1), out_shape=shapes + shapes,
        scratch_shapes=[pltpu.VMEM((2, H, dv, dk), F32)],
        compiler_params=_params("arbitrary"))(proj, proj, proj, z, proj, proj, proj, z, sprev[0], sprev[1], do, do)
    return (res[0], res[4]), (res[1], res[5]), (res[2], res[6]), (res[3], res[7])


def _l0_combine_fwd(hs, proj, o, gain, name):
    T = proj.shape[0]
    tm = min(512, T)

    def body(hf, hb, ga, of, ob, g, gn, out):
        out[...] = _l0_combine(hf[...], hb[...], ga[...], of[...], ob[...], g[...], gn[...]).astype(BF16)

    tok = pl.BlockSpec((tm, 512), lambda i: (i, 0))
    return pl.pallas_call(
        body, name=name, grid=(T // tm,),
        in_specs=[tok, tok, pl.BlockSpec((tm, 512), lambda i: (i, 1)), tok, tok,
                  pl.BlockSpec((tm, 512), lambda i: (i, 6)), pl.BlockSpec((1, 512), lambda i: (0, 0))],
        out_specs=pl.BlockSpec((tm, 1024), lambda i: (i, 0)),
        out_shape=jax.ShapeDtypeStruct((T, 1024), BF16),
        compiler_params=_params("parallel"))(hs[0], hs[1], proj, o[0], o[1], proj, gain)


def _l0_combine_bwd(hs, proj, o, gain, dh_b, w_out, name):
    T = proj.shape[0]
    tm = min(512, T)

    def body(hf, hb, ga, of, ob, g, gn, dhb_ref, w_ref, dho_ref, dga_ref, do_ref, dg_ref, dgn_ref):
        @pl.when(pl.program_id(0) == 0)
        def _():
            dgn_ref[...] = jnp.zeros_like(dgn_ref)

        _, vjp = jax.vjp(_l0_combine, hf[...], hb[...], ga[...], of[...], ob[...], g[...], gn[...])
        dhf, _, dga, dof, _, dg, dgn = vjp(_dg(dhb_ref[...], w_ref[...], 1, 1))
        dho_ref[...] = dhf
        dga_ref[...] = dga
        do_ref[...] = dof
        dg_ref[...] = dg
        dgn_ref[...] += dgn

    tok = lambda: pl.BlockSpec((tm, 512), lambda i: (i, 0))
    return pl.pallas_call(
        body, name=name, grid=(T // tm,),
        in_specs=[tok(), tok(), pl.BlockSpec((tm, 512), lambda i: (i, 1)), tok(), tok(),
                  pl.BlockSpec((tm, 512), lambda i: (i, 6)), pl.BlockSpec((1, 512), lambda i: (0, 0)),
                  pl.BlockSpec((tm, D_MODEL), lambda i: (i, 0)), pl.BlockSpec(w_out.shape, lambda i: (0, 0))],
        out_specs=[tok(), tok(), tok(), tok(), pl.BlockSpec((1, 512), lambda i: (0, 0))],
        out_shape=[jax.ShapeDtypeStruct((T, 512), F32)] * 4 + [jax.ShapeDtypeStruct((1, 512), F32)],
        compiler_params=_params("arbitrary"))(hs[0], hs[1], proj, o[0], o[1], proj, gain, dh_b, w_out)


def _l0_assemble(dxa, dga, dq, df, dv, dg, name):
    T = dxa.shape[0]
    tm = min(512, T)

    def body(xa, ga, q0, q1, f0, f1, v0, v1, g, out):
        both = lambda a, b: (a[...].astype(F32) + b[...].astype(F32)).astype(BF16)
        out[...] = jnp.concatenate([xa[...].astype(BF16), ga[...].astype(BF16), both(q0, q1), f0[...], f1[...],
                                    both(v0, v1), g[...].astype(BF16)], axis=1)

    tok = lambda: pl.BlockSpec((tm, 512), lambda i: (i, 0))
    return pl.pallas_call(
        body, name=name, grid=(T // tm,),
        in_specs=[tok() for _ in range(9)],
        out_specs=pl.BlockSpec((tm, AB_IN), lambda i: (i, 0)),
        out_shape=jax.ShapeDtypeStruct((T, AB_IN), BF16),
        compiler_params=_params("parallel"))(dxa, dga, dq[0], dq[1], df[0], df[1], dv[0], dv[1], dg)


def _l1_combine_fwd(o, proj, gain, name):
    T = proj.shape[0]
    tm = min(512, T)

    def body(of, ob, r, gn, out):
        out[...] = _l1_combine(of[...], ob[...], r[...], gn[...]).astype(BF16)

    tok = pl.BlockSpec((tm, 1024), lambda i: (i, 0))
    return pl.pallas_call(
        body, name=name, grid=(T // tm,),
        in_specs=[tok, tok, pl.BlockSpec((tm, 1024), lambda i: (i, 2)), pl.BlockSpec((1, 1024), lambda i: (0, 0))],
        out_specs=pl.BlockSpec((tm, 1024), lambda i: (i, 0)),
        out_shape=jax.ShapeDtypeStruct((T, 1024), BF16),
        compiler_params=_params("parallel"))(o[0], o[1], proj, gain)


def _l1_combine_bwd(o, proj, gain, dh_b, w_out, name):
    T = proj.shape[0]
    tm = min(512, T)

    def body(of, ob, r, gn, dhb_ref, w_ref, do_ref, dr_ref, dgn_ref):
        @pl.when(pl.program_id(0) == 0)
        def _():
            dgn_ref[...] = jnp.zeros_like(dgn_ref)

        _, vjp = jax.vjp(_l1_combine, of[...], ob[...], r[...], gn[...])
        dof, _, dr, dgn = vjp(_dg(dhb_ref[...], w_ref[...], 1, 1))
        do_ref[...] = dof
        dr_ref[...] = dr
        dgn_ref[...] += dgn

    tok = lambda: pl.BlockSpec((tm, 1024), lambda i: (i, 0))
    return pl.pallas_call(
        body, name=name, grid=(T // tm,),
        in_specs=[tok(), tok(), pl.BlockSpec((tm, 1024), lambda i: (i, 2)),
                  pl.BlockSpec((1, 1024), lambda i: (0, 0)), tok(), pl.BlockSpec(w_out.shape, lambda i: (0, 0))],
        out_specs=[tok(), tok(), pl.BlockSpec((1, 1024), lambda i: (0, 0))],
        out_shape=[jax.ShapeDtypeStruct((T, 1024), F32)] * 2 + [jax.ShapeDtypeStruct((1, 1024), F32)],
        compiler_params=_params("arbitrary"))(o[0], o[1], proj, gain, dh_b, w_out)


def _l1_assemble(dq, dk, dv, dr, dlr, name):
    T = dr.shape[0]
    tm = min(512, T)

    def body(q0, q1, k0, k1, v0, v1, r, a, out):
        both = lambda x, y: (x[...].astype(F32) + y[...].astype(F32)).astype(BF16)
        out[...] = jnp.concatenate([both(q0, q1), both(k0, k1), both(v0, v1), r[...].astype(BF16),
                                    a[...].astype(BF16)], axis=1)

    tok = lambda w: pl.BlockSpec((tm, w), lambda i: (i, 0))
    return pl.pallas_call(
        body, name=name, grid=(T // tm,),
        in_specs=[tok(512), tok(512), tok(512), tok(512), tok(1024), tok(1024), tok(1024), tok(LANES)],
        out_specs=pl.BlockSpec((tm, GLA_IN_PAD), lambda i: (i, 0)),
        out_shape=jax.ShapeDtypeStruct((T, GLA_IN_PAD), BF16),
        compiler_params=_params("parallel"))(dq[0], dq[1], dk[0], dk[1], dv[0], dv[1], dr, dlr)


HBM_SPEC = pl.BlockSpec(memory_space=pltpu.HBM)


def _place():
    x, y, c = lax.axis_index("x"), lax.axis_index("y"), lax.axis_index("c")
    return x, y, c


def _allgather_vmem(x_shard, name):
    m_per, n = x_shard.shape

    def body(x_ref, out_ref, send_sems, recv_sems, local_sem):
        x, y, c = _place()
        me, sibling = (x, y, c), (x, y, 1 - c)
        chips = [(1 - x, y), (x, 1 - y), (1 - x, 1 - y)]

        def rows(px, py, pc):
            return out_ref.at[pl.ds((4 * px + 2 * py + pc) * m_per, m_per), :]

        def copy(k, block, to, src=None):
            return pltpu.make_async_remote_copy(
                src_ref=rows(*block) if src is None else src, dst_ref=rows(*block),
                send_sem=send_sems.at[k], recv_sem=recv_sems.at[k], device_id=to, device_id_type=MESH)

        mine = pltpu.make_async_copy(x_ref, rows(*me), local_sem)
        mine.start()
        first = [copy(0, me, sibling, src=x_ref)]
        first += [copy(1 + j, me, (*chip, c), src=x_ref) for j, chip in enumerate(chips)]
        for cp in first:
            cp.start()
        passed = [copy(4 + j, (*chip, c), sibling) for j, chip in enumerate(chips)]
        for j, chip in enumerate(chips):
            copy(1 + j, (*chip, c), me).wait_recv()
            passed[j].start()
        copy(0, sibling, me).wait_recv()
        for j, chip in enumerate(chips):
            copy(4 + j, (*chip, 1 - c), me).wait_recv()
        for cp in first + passed:
            cp.wait_send()
        mine.wait()

    vm = pl.BlockSpec(memory_space=pltpu.VMEM)
    return pl.pallas_call(
        body, name=name, in_specs=[vm], out_specs=vm,
        out_shape=jax.ShapeDtypeStruct((N_DEV * m_per, n), x_shard.dtype),
        scratch_shapes=[pltpu.SemaphoreType.DMA((7,)), pltpu.SemaphoreType.DMA((7,)), pltpu.SemaphoreType.DMA],
        compiler_params=pltpu.CompilerParams(has_side_effects=True, vmem_limit_bytes=VMEM_LIMIT))(x_shard)


SEM_SPEC = pl.BlockSpec(memory_space=pltpu.SEMAPHORE)
DATAFLOW_EFFECT = pltpu.SideEffectType.DATAFLOW_SIDE_EFFECTING


def _copies(plan, srcs, lands, send_sems, recv_sems):
    x, y, c = _place()
    return [pltpu.make_async_remote_copy(src_ref=s, dst_ref=d, send_sem=send_sems.at[k], recv_sem=recv_sems.at[k],
                                         device_id=dev, device_id_type=MESH)
            for k, (s, d, dev) in enumerate(plan(srcs, lands, x, y, c))]


def _copies_start(plan, n_copies, srcs, lands, name):
    ns, nl = len(srcs), len(lands)

    def body(*refs):
        send_sems, recv_sems = refs[ns + nl], refs[ns + nl + 1]
        for cp in _copies(plan, refs[:ns], refs[ns:ns + nl], send_sems, recv_sems):
            cp.start()
        refs[-1][...] = jnp.zeros_like(refs[-1])

    arrays = list(srcs) + list(lands)
    res = pl.pallas_call(
        body, name=name,
        in_specs=[HBM_SPEC] * (ns + nl),
        out_specs=tuple([SEM_SPEC, SEM_SPEC] + [HBM_SPEC] * (ns + nl) + [pl.BlockSpec(memory_space=pltpu.VMEM)]),
        out_shape=tuple([pltpu.SemaphoreType.DMA((n_copies,)), pltpu.SemaphoreType.DMA((n_copies,))]
                        + [pltpu.HBM(a.shape, a.dtype) for a in arrays]
                        + [jax.ShapeDtypeStruct((SUBLANES, LANES), F32)]),
        input_output_aliases={i: 2 + i for i in range(ns + nl)},
        compiler_params=pltpu.CompilerParams(has_side_effects=DATAFLOW_EFFECT),
    )(*[pltpu.with_memory_space_constraint(a, pltpu.HBM) for a in arrays])
    return res[0], res[1], list(res[2:2 + ns]), list(res[2 + ns:2 + ns + nl]), res[-1]


def _copies_wait(plan, started, after, name):
    send_sems, recv_sems, srcs, lands, _ = started
    ns, nl = len(srcs), len(lands)

    def body(*refs):
        for cp in _copies(plan, refs[:ns], refs[ns:ns + nl], refs[ns + nl], refs[ns + nl + 1]):
            cp.wait_send()
            cp.wait_recv()

    arrays = list(srcs) + list(lands)
    res = pl.pallas_call(
        body, name=name,
        in_specs=[HBM_SPEC] * (ns + nl) + [SEM_SPEC, SEM_SPEC, pl.BlockSpec(memory_space=pl.ANY)],
        out_specs=tuple([HBM_SPEC] * (ns + nl)),
        out_shape=tuple(pltpu.HBM(a.shape, a.dtype) for a in arrays),
        input_output_aliases={i: i for i in range(ns + nl)},
        compiler_params=pltpu.CompilerParams(has_side_effects=DATAFLOW_EFFECT),
    )(*arrays, send_sems, recv_sems, after)
    return list(res[:ns]), list(res[ns:])


def _after(token, value):
    return value + token[0:1, 0:1].astype(value.dtype)


def _chips(x, y):
    return [(1 - x, y), (x, 1 - y), (1 - x, 1 - y)]


def _plan_gather_first(srcs, lands, x, y, c):
    me = 4 * x + 2 * y + c
    out = []
    for s, l in zip(srcs, lands):
        out.append((s, l.at[me], (x, y, 1 - c)))
        out += [(s, l.at[me], (*chip, c)) for chip in _chips(x, y)]
    return out


def _plan_gather_pass(srcs, lands, x, y, c):
    out = []
    for l in lands:
        for chip in _chips(x, y):
            slot = l.at[4 * chip[0] + 2 * chip[1] + c]
            out.append((slot, slot, (x, y, 1 - c)))
    return out


def _plan_grads_sibling(srcs, lands, x, y, c):
    return [(s.at[2 * q + (1 - c)], l.at[q], (x, y, 1 - c)) for s, l in zip(srcs, lands) for q in range(4)]


def _plan_grads_chips(srcs, lands, x, y, c):
    return [(s.at[2 * chip[0] + chip[1]], l.at[k], (*chip, c))
            for s, l in zip(srcs, lands) for k, chip in enumerate(_chips(x, y))]


def _landing(n_slots, like):
    return [lax.empty((n_slots,) + a.shape[1:], a.dtype) for a in like]


def _sum_slots(g, name):
    _, R, C = g.shape
    tr = min(256, R)
    assert R % tr == 0

    def body(g_ref, o_ref):
        acc = g_ref[0]
        for j in range(1, N_DEV):
            acc = acc + g_ref[j]
        o_ref[...] = acc

    return pl.pallas_call(
        body, name=name, grid=(R // tr,),
        in_specs=[pl.BlockSpec((N_DEV, tr, C), lambda i: (0, i, 0))],
        out_specs=pl.BlockSpec((tr, C), lambda i: (i, 0)),
        out_shape=jax.ShapeDtypeStruct((R, C), F32),
        compiler_params=_params("parallel"))(g)


def _chip_partial(g, r1, place, name):
    _, R, C = g.shape
    tr = min(1024, R)
    assert R % tr == 0

    def body(pl_ref, g_ref, r_ref, pb_ref, pm_ref):
        q = pl.program_id(1)
        s = g_ref[...] + r_ref[...]
        pb_ref[...] = s.astype(BF16)

        @pl.when(q == pl_ref[1])
        def _():
            pm_ref[...] = s

    grid_spec = pltpu.PrefetchScalarGridSpec(
        num_scalar_prefetch=1, grid=(R // tr, 4),
        in_specs=[pl.BlockSpec((None, tr, C), lambda r, q, p: (2 * q + p[0], r, 0)),
                  pl.BlockSpec((None, tr, C), lambda r, q, p: (q, r, 0))],
        out_specs=[pl.BlockSpec((None, tr, C), lambda r, q, p: (q, r, 0)),
                   pl.BlockSpec((tr, C), lambda r, q, p: (r, 0))])
    return pl.pallas_call(
        body, name=name, grid_spec=grid_spec,
        out_shape=[jax.ShapeDtypeStruct((4, R, C), BF16), jax.ShapeDtypeStruct((R, C), F32)],
        compiler_params=_params("parallel", "arbitrary"))(place, g, r1)


def _adamw_update(w, g, m, v, grad_ref, delta_ref, m_ref, v_ref):
    mn = ADAM_B1 * m + (1.0 - ADAM_B1) * g
    vn = ADAM_B2 * v + (1.0 - ADAM_B2) * jnp.square(g)
    m_hat = mn / (1.0 - ADAM_B1 ** ADAM_STEP)
    v_hat = vn / (1.0 - ADAM_B2 ** ADAM_STEP)
    grad_ref[...] = g
    delta_ref[...] = -ADAM_LR * (m_hat / (jnp.sqrt(v_hat) + ADAM_EPS) + ADAM_WD * w)
    m_ref[...] = mn
    v_ref[...] = vn


def _adamw_whole(w, g, m, v, name):
    def body(w_ref, g_ref, m_ref, v_ref, go, do, mo, vo):
        _adamw_update(w_ref[...], g_ref[...], m_ref[...], v_ref[...], go, do, mo, vo)

    vm = pl.BlockSpec(memory_space=pltpu.VMEM)
    return pl.pallas_call(body, name=name, in_specs=[vm] * 4, out_specs=[vm] * 4,
                          out_shape=[jax.ShapeDtypeStruct(w.shape, F32)] * 4)(w, g, m, v)


def _adamw(w, gparts, m, v, name):
    _, R, C = w.shape
    tr = min(256, R)
    assert R % tr == 0

    def body(w_ref, g0_ref, g3_ref, m_ref, v_ref, go, do, mo, vo):
        g = g0_ref[...]
        for k in range(3):
            g = g + g3_ref[k].astype(F32)
        _adamw_update(w_ref[...], g, m_ref[...], v_ref[...], go, do, mo, vo)

    blk = pl.BlockSpec((tr, C), lambda i: (i, 0))
    wblk = pl.BlockSpec((None, tr, C), lambda i: (0, i, 0))
    return pl.pallas_call(
        body, name=name, grid=(R // tr,),
        in_specs=[wblk, blk, pl.BlockSpec((3, tr, C), lambda i: (0, i, 0)), wblk, wblk], out_specs=[wblk] * 4,
        out_shape=[jax.ShapeDtypeStruct(w.shape, F32)] * 4,
        compiler_params=_params("parallel"))(w, gparts[0], gparts[1], m, v)


def _adamw_layers(w, parts, m, v, name):
    _, R, C = w.shape
    tr = min(256, R)
    assert R % tr == 0

    def body(w_ref, p0, r0, p1, r1, m_ref, v_ref, go, do, mo, vo):
        gs = []
        for p, r in ((p0, r0), (p1, r1)):
            g = p[...]
            for k in range(3):
                g = g + r[k].astype(F32)
            gs.append(g)
        g = jnp.where(pl.program_id(0) == 0, gs[0], gs[1])
        _adamw_update(w_ref[...], g, m_ref[...], v_ref[...], go, do, mo, vo)

    lay = pl.BlockSpec((None, tr, C), lambda l, i: (l, i, 0))
    one = pl.BlockSpec((tr, C), lambda l, i: (i, 0))
    three = pl.BlockSpec((3, tr, C), lambda l, i: (0, i, 0))
    return pl.pallas_call(
        body, name=name, grid=(2, R // tr), in_specs=[lay, one, three, one, three, lay, lay],
        out_specs=[lay] * 4, out_shape=[jax.ShapeDtypeStruct((2, R, C), F32)] * 4,
        compiler_params=_params("parallel", "parallel"))(w, parts[0][0], parts[0][1], parts[1][0], parts[1][1], m, v)


SMALL_SHARDED = ("rg_conv_w", "rg_b_a", "rg_b_x", "rg_lambda", "gla_w_gate_up", "gla_b_gate", "gla_norm")
SMALL_REPLICATED = ("norm_mix", "norm_mlp", "norm_final", "rg_conv_b", "rg_w_a", "rg_w_x", "hg_lb_logits", "hg_norm")
WEIGHT_NAMES = ("norm_mix", "norm_mlp", "norm_final", "mlp_w1", "mlp_w2", "ab_w_in", "ab_w_out", "rg_conv_w",
                "rg_conv_b", "rg_w_a", "rg_b_a", "rg_w_x", "rg_b_x", "rg_lambda", "hg_lb_logits", "hg_norm",
                "gla_w_in", "gla_w_out", "gla_w_gate_up", "gla_b_gate", "gla_norm")


def _rows128(a):
    return a.reshape(-1, LANES)


def _part_rows(a):
    return -(-(a.size // LANES) // SUBLANES) * SUBLANES


def _pack_rows(arrays, pad_to=SUBLANES):
    parts = [jnp.pad(_rows128(a), ((0, _part_rows(a) - a.size // LANES), (0, 0))) for a in arrays]
    total = sum(p.shape[0] for p in parts)
    extra = (-total) % pad_to
    if extra:
        parts.append(jnp.zeros((extra, LANES), parts[0].dtype))
    return jnp.concatenate(parts, axis=0)


def _unshard_last(g, shape_local):
    nd = len(shape_local)
    t = g.reshape((N_DEV,) + tuple(shape_local))
    t = jnp.moveaxis(t, 0, nd - 1)
    return t.reshape(tuple(shape_local[:-1]) + (N_DEV * shape_local[-1],))


def _block_diag(w):
    eye = jnp.eye(8, dtype=w.dtype)
    return (w[:, :, :, None, :] * eye[None, :, None, :, None]).reshape(2, RG_W, RG_W)


def _block_diag_extract(dw):
    t = dw.reshape(2, 8, 64, 8, 64)
    return jnp.moveaxis(jnp.diagonal(t, axis1=1, axis2=3), -1, 1)


def kernel(x, norm_mix, norm_mlp, norm_final, mlp_w1, mlp_w2, ab_w_in, ab_w_out, rg_conv_w, rg_conv_b, rg_w_a, rg_b_a, rg_w_x, rg_b_x, rg_lambda, hg_lb_logits, hg_norm, gla_w_in, gla_w_out, gla_w_gate_up, gla_b_gate, gla_norm, loss_target, m_norm_mix, m_norm_mlp, m_norm_final, m_mlp_w1, m_mlp_w2, m_ab_w_in, m_ab_w_out, m_rg_conv_w, m_rg_conv_b, m_rg_w_a, m_rg_b_a, m_rg_w_x, m_rg_b_x, m_rg_lambda, m_hg_lb_logits, m_hg_norm, m_gla_w_in, m_gla_w_out, m_gla_w_gate_up, m_gla_b_gate, m_gla_norm, v_norm_mix, v_norm_mlp, v_norm_final, v_mlp_w1, v_mlp_w2, v_ab_w_in, v_ab_w_out, v_rg_conv_w, v_rg_conv_b, v_rg_w_a, v_rg_b_a, v_rg_w_x, v_rg_b_x, v_rg_lambda, v_hg_lb_logits, v_hg_norm, v_gla_w_in, v_gla_w_out, v_gla_w_gate_up, v_gla_b_gate, v_gla_norm):
    w_loc = dict(norm_mix=norm_mix, norm_mlp=norm_mlp, norm_final=norm_final, mlp_w1=mlp_w1, mlp_w2=mlp_w2,
                 ab_w_in=ab_w_in, ab_w_out=ab_w_out, rg_conv_w=rg_conv_w, rg_conv_b=rg_conv_b, rg_w_a=rg_w_a,
                 rg_b_a=rg_b_a, rg_w_x=rg_w_x, rg_b_x=rg_b_x, rg_lambda=rg_lambda, hg_lb_logits=hg_lb_logits,
                 hg_norm=hg_norm, gla_w_in=gla_w_in, gla_w_out=gla_w_out, gla_w_gate_up=gla_w_gate_up,
                 gla_b_gate=gla_b_gate, gla_norm=gla_norm)
    m_loc = dict(norm_mix=m_norm_mix, norm_mlp=m_norm_mlp, norm_final=m_norm_final, mlp_w1=m_mlp_w1,
                 mlp_w2=m_mlp_w2, ab_w_in=m_ab_w_in, ab_w_out=m_ab_w_out, rg_conv_w=m_rg_conv_w,
                 rg_conv_b=m_rg_conv_b, rg_w_a=m_rg_w_a, rg_b_a=m_rg_b_a, rg_w_x=m_rg_w_x, rg_b_x=m_rg_b_x,
                 rg_lambda=m_rg_lambda, hg_lb_logits=m_hg_lb_logits, hg_norm=m_hg_norm, gla_w_in=m_gla_w_in,
                 gla_w_out=m_gla_w_out, gla_w_gate_up=m_gla_w_gate_up, gla_b_gate=m_gla_b_gate,
                 gla_norm=m_gla_norm)
    v_loc = dict(norm_mix=v_norm_mix, norm_mlp=v_norm_mlp, norm_final=v_norm_final, mlp_w1=v_mlp_w1,
                 mlp_w2=v_mlp_w2, ab_w_in=v_ab_w_in, ab_w_out=v_ab_w_out, rg_conv_w=v_rg_conv_w,
                 rg_conv_b=v_rg_conv_b, rg_w_a=v_rg_w_a, rg_b_a=v_rg_b_a, rg_w_x=v_rg_w_x, rg_b_x=v_rg_b_x,
                 rg_lambda=v_rg_lambda, hg_lb_logits=v_hg_lb_logits, hg_norm=v_hg_norm, gla_w_in=v_gla_w_in,
                 gla_w_out=v_gla_w_out, gla_w_gate_up=v_gla_w_gate_up, gla_b_gate=v_gla_b_gate,
                 gla_norm=v_gla_norm)

    T = x.shape[1]
    h0 = x.reshape(T, D_MODEL)
    target = loss_target.reshape(T, D_MODEL)
    ax, ay, ac = lax.axis_index("x"), lax.axis_index("y"), lax.axis_index("c")
    dev = 4 * ax + 2 * ay + ac
    place = jnp.stack([ac, 2 * ax + ay]).astype(jnp.int32)

    abin_shard = ab_w_in[0].astype(BF16)
    first_started = _copies_start(_plan_gather_first, 4, [abin_shard], _landing(N_DEV, [abin_shard[None]]),
                                  "ag_first_start")
    rest_shards = [mlp_w1[0].astype(BF16), mlp_w2[0].astype(BF16), gla_w_in[0].astype(BF16),
                   gla_w_out[0].astype(BF16), mlp_w1[1].astype(BF16), mlp_w2[1].astype(BF16),
                   _after(first_started[4], ab_w_out[0].astype(BF16))]
    ag_started = _copies_start(_plan_gather_first, 4 * len(rest_shards), rest_shards,
                               _landing(N_DEV, [s[None] for s in rest_shards]), "ag_rest_start")

    small_local = [w_loc[n] for n in SMALL_SHARDED]
    small_g = _allgather_vmem(_pack_rows(small_local, 8), "ag_small")
    small_g = small_g.reshape(N_DEV, -1, LANES)
    full = {}
    off = 0
    for n, a in zip(SMALL_SHARDED, small_local):
        full[n] = _unshard_last(small_g[:, off:off + a.size // LANES].reshape(N_DEV, a.size), a.shape)
        off += _part_rows(a)
    conv_w = full["rg_conv_w"][0]
    b_a, b_x, lam = full["rg_b_a"][0], full["rg_b_x"][0], full["rg_lambda"][0]
    w_up, b_gate, g_norm = full["gla_w_gate_up"][0], full["gla_b_gate"][0], full["gla_norm"]

    cw8 = jnp.pad(conv_w, ((0, 4), (0, 0)))
    wbd = jnp.concatenate([_block_diag(rg_w_a[0]), _block_diag(rg_w_x[0])], axis=2).astype(BF16)
    rg_bias = jnp.concatenate([b_a, b_x], axis=1).reshape(2, 1, 2 * RG_W)
    lam3 = lam.reshape(2, 1, RG_W)
    l0, l1 = hg_lb_logits[0:1], hg_lb_logits[1:2]
    wup_pad = jnp.zeros((2, LANES, 512), F32).at[0, 0:16].set(w_up[0]).at[1, 16:32].set(w_up[1])
    bg3 = b_gate.reshape(2, 1, 512)
    nmix0, nmix1 = norm_mix[0:1], norm_mix[1:2]
    nmlp0, nmlp1 = norm_mlp[0:1], norm_mlp[1:2]
    nfin = norm_final.reshape(1, D_MODEL)

    prepared = (ag_started[4] + cw8[:, 0:LANES] + wup_pad[0, 0:SUBLANES, 0:LANES] + rg_bias[0, :, 0:LANES]
                + wbd[0, 0:SUBLANES, 0:LANES].astype(F32) + lam3[0, :, 0:LANES] + bg3[0, :, 0:LANES])
    (abin_shard,), abin_l = _copies_wait(_plan_gather_first, first_started, prepared, "ag_first_wait")
    first_pass = _copies_start(_plan_gather_pass, 3, [], abin_l, "ag_first_pass_start")
    _, (abin_g,) = _copies_wait(_plan_gather_pass, first_pass, first_pass[4], "ag_first_pass_wait")
    abin_g = lax.dynamic_update_index_in_dim(abin_g, abin_shard, dev, 0)
    wab_in = jnp.transpose(abin_g, (1, 0, 2)).reshape(D_MODEL, AB_IN)
    proj0, y0 = _norm_matmul(h0, _after(ag_started[4], nmix0), wab_in, "l0_in_proj")
    xc = _rg_conv_fwd(proj0, cw8, rg_conv_b, "rg_conv")
    hs = _rg_scan_fwd(xc, wbd, rg_bias, lam3, "rg_scan")
    o_hg, s_hg = _hg_fwd(proj0, l0, l1, "hg_chunks")
    both_done = hs[0][0:SUBLANES, 0:LANES] + o_hg[0][0:SUBLANES, 0:LANES]
    rest_shards, rest_lands = _copies_wait(_plan_gather_first, ag_started, both_done, "ag_rest_wait")
    pass_started = _copies_start(_plan_gather_pass, 3 * len(rest_lands), [], rest_lands, "ag_pass_start")
    mixin0 = _l0_combine_fwd(hs, proj0, o_hg, _after(pass_started[4], hg_norm), "l0_combine")
    _, rest_g = _copies_wait(_plan_gather_pass, pass_started, mixin0, "ag_pass_wait")
    rest_g = [lax.dynamic_update_index_in_dim(g, s, dev, 0) for g, s in zip(rest_g, rest_shards)]
    wab_out = rest_g[6].reshape(D_MODEL, D_MODEL)
    h1 = _matmul_res(mixin0, wab_out, h0, "l0_out_proj")
    w1g = (rest_g[0], rest_g[4])
    w2f = (rest_g[1].reshape(D_FF, D_MODEL), rest_g[5].reshape(D_FF, D_MODEL))
    wgla_in = jnp.pad(jnp.transpose(rest_g[2], (1, 0, 2)).reshape(D_MODEL, GLA_IN),
                      ((0, 0), (0, GLA_IN_PAD - GLA_IN)))
    wgla_out = rest_g[3].reshape(D_MODEL, D_MODEL)
    h2, pre0, ym0 = _mlp_fwd(h1, nmlp0, w1g[0], w2f[0], "mlp0")
    proj1, y1 = _norm_matmul(h2, nmix1, wgla_in, "l1_in_proj")
    z_gate, lr_b = _gate_logits(proj1, wup_pad, bg3, "gla_gate_logits")
    o_gla, s_gla = _gla_fwd(proj1, z_gate, "gla_chunks")
    mixin1 = _l1_combine_fwd(o_gla, proj1, g_norm, "l1_combine")
    h3 = _matmul_res(mixin1, wgla_out, h2, "l1_out_proj")
    h4, pre1, ym1 = _mlp_fwd(h3, nmlp1, w1g[1], w2f[1], "mlp1")
    loss_blk, dh4, dh4b, d_nfin = _final_loss(h4, nfin, target, "final_loss")

    dh3, dh3b, dpre1, act1, d_nmlp1 = _mlp_bwd(dh4, dh4b, h3, nmlp1, pre1, w1g[1], w2f[1], "mlp1_bwd")
    g_w1_1 = _wgrad(ym1, dpre1, 512, "mlp1_dw1", sharded_cols=True)
    g_w2_1 = _wgrad(act1, dh4b, 512, "mlp1_dw2")
    g_gla_out = _wgrad(mixin1, dh3b, 512, "l1_out_dw")
    do_gla, dr, d_gnorm = _l1_combine_bwd(o_gla, proj1, g_norm, dh3b, wgla_out, "l1_combine_bwd")
    dq1, dk1, dv1, dz_gate = _gla_bwd(proj1, z_gate, s_gla, do_gla, "gla_chunks_bwd")
    dlr1, d_bg, dz_b = _gate_logits_bwd(dz_gate, wup_pad, "gla_gate_logits_bwd")
    d_wup = [_wgrad(lr_b, dz_b[d], 512, "gla_gate_dw%d" % d) for d in range(2)]
    dproj1 = _l1_assemble(dq1, dk1, dv1, dr, dlr1, "l1_assemble")
    dh2, dh2b, d_nmix1 = _dgrad_norm(dproj1, wgla_in, h2, nmix1, dh3, "l1_in_dgrad")
    g_gla_in = _wgrad(y1, dproj1, 640, "l1_in_dw")

    def reduce_start(grads, tag):
        return _copies_start(_plan_grads_sibling, 4 * len(grads), grads, _landing(4, grads), "rs_%s_d2d_start" % tag)

    def reduce_mid(started, after, tag):
        grads, got = _copies_wait(_plan_grads_sibling, started, after, "rs_%s_d2d_wait" % tag)
        parts = [_chip_partial(g, r, place, "rs_%s_partial%d" % (tag, a)) for a, (g, r) in enumerate(zip(grads, got))]
        pb = [p[0] for p in parts]
        return _copies_start(_plan_grads_chips, 3 * len(pb), pb, _landing(3, pb), "rs_%s_ici_start" % tag), \
            [p[1] for p in parts]

    def reduce_end(started, mine, after, tag):
        _, got = _copies_wait(_plan_grads_chips, started, after, "rs_%s_ici_wait" % tag)
        return list(zip(mine, got))

    slots_l1 = [g_w1_1, g_w2_1.reshape(N_DEV, 512, D_MODEL),
                jnp.transpose(g_gla_in[:, :GLA_IN].reshape(D_MODEL, N_DEV, GLA_IN // N_DEV), (1, 0, 2)),
                g_gla_out.reshape(N_DEV, 128, D_MODEL)]
    ra_d2d = reduce_start(slots_l1, "l1")

    dh1, dh1b, dpre0, act0, d_nmlp0 = _mlp_bwd(dh2, dh2b, h1, _after(ra_d2d[4], nmlp0), pre0, w1g[0], w2f[0],
                                               "mlp0_bwd")
    g_w1_0 = _wgrad(ym0, dpre0, 512, "mlp0_dw1", sharded_cols=True)
    g_w2_0 = _wgrad(act0, dh2b, 512, "mlp0_dw2")
    ra_ici, ra_mine = reduce_mid(ra_d2d, g_w2_0, "l1")
    g_ab_out = _wgrad(mixin0, dh1b, 512, "l0_out_dw")
    rb_d2d = reduce_start([g_w1_0, g_w2_0.reshape(N_DEV, 512, D_MODEL), g_ab_out.reshape(N_DEV, 128, D_MODEL)],
                          "mlp0")
    dho, dga, do_hg, dg_gate, d_hgnorm = _l0_combine_bwd(
        hs, proj0, o_hg, _after(rb_d2d[4], _after(ra_ici[4], hg_norm)), dh1b, wab_out, "l0_combine_bwd")
    dxc, d_wbd, d_rgb, d_lam = _rg_scan_bwd(xc, wbd, rg_bias, lam3, hs, dho, "rg_scan_bwd")
    dxa, d_cw8, d_cb = _rg_conv_bwd(dxc, proj0, cw8, "rg_conv_bwd")
    dq0, df0, dv0, d_l0, d_l1 = _hg_bwd(proj0, l0, l1, s_hg, do_hg, "hg_chunks_bwd")
    rb_ici, rb_mine = reduce_mid(rb_d2d, d_l0, "mlp0")
    dproj0 = _l0_assemble(dxa, dga, dq0, df0, dv0, dg_gate, "l0_assemble")
    dx, _, d_nmix0 = _dgrad_norm(dproj0, wab_in, h0, _after(rb_ici[4], nmix0), dh1, "l0_in_dgrad")

    d_wa = _block_diag_extract(d_wbd[:, :, :RG_W])[None]
    d_wx = _block_diag_extract(d_wbd[:, :, RG_W:])[None]
    small_full = {
        "norm_mix": jnp.concatenate([d_nmix0, d_nmix1], axis=0), "norm_mlp": jnp.concatenate([d_nmlp0, d_nmlp1], axis=0),
        "norm_final": d_nfin.reshape(D_MODEL), "rg_conv_b": d_cb, "rg_w_a": d_wa, "rg_w_x": d_wx,
        "hg_lb_logits": jnp.concatenate([d_l0[0] + d_l0[1], d_l1[0] + d_l1[1]], axis=0), "hg_norm": d_hgnorm,
        "rg_conv_w": d_cw8[0:4][None], "rg_b_a": d_rgb[:, 0, :RG_W][None], "rg_b_x": d_rgb[:, 0, RG_W:][None],
        "rg_lambda": d_lam[:, 0, :][None],
        "gla_w_gate_up": jnp.stack([d_wup[0][0:16], d_wup[1][16:32]])[None], "gla_b_gate": d_bg[:, 0, :][None],
        "gla_norm": d_gnorm}
    small_names = SMALL_REPLICATED + SMALL_SHARDED
    packed = _pack_rows([loss_blk] + [small_full[n] for n in small_names], 256)
    ar_first = _copies_start(_plan_gather_first, 4, [packed], _landing(N_DEV, [packed[None]]), "ar_small_start")

    g_ab_in = _wgrad(y0, dproj0, 512, "l0_in_dw", behind=ar_first[4])
    rc_d2d = reduce_start([jnp.transpose(g_ab_in.reshape(D_MODEL, N_DEV, AB_IN // N_DEV), (1, 0, 2))], "ab")
    (packed,), ar_lands = _copies_wait(_plan_gather_first, ar_first, rc_d2d[4], "ar_small_wait")
    ar_pass = _copies_start(_plan_gather_pass, 3, [], ar_lands, "ar_small_pass_start")
    rc_ici, rc_mine = reduce_mid(rc_d2d, ar_pass[4], "ab")
    _, (ar_gathered,) = _copies_wait(_plan_gather_pass, ar_pass, rc_ici[4], "ar_small_pass_wait")
    summed = _sum_slots(lax.dynamic_update_index_in_dim(ar_gathered, packed, dev, 0), "ar_small_sum")
    loss = summed[0, 0]

    pieces_l1 = reduce_end(ra_ici, ra_mine, rc_ici[4], "l1")
    res_gla_in = _adamw(gla_w_in, pieces_l1[2], m_gla_w_in, v_gla_w_in, "adamw_gla_in")
    res_gla_out = _adamw(gla_w_out, pieces_l1[3], m_gla_w_out, v_gla_w_out, "adamw_gla_out")
    pieces_mlp0 = reduce_end(rb_ici, rb_mine, res_gla_out[0], "mlp0")
    res_w1 = _adamw_layers(mlp_w1, (pieces_mlp0[0], pieces_l1[0]), m_mlp_w1, v_mlp_w1, "adamw_mlp_w1")
    res_w2 = _adamw_layers(mlp_w2, (pieces_mlp0[1], pieces_l1[1]), m_mlp_w2, v_mlp_w2, "adamw_mlp_w2")
    res = {"mlp_w1": tuple(res_w1), "mlp_w2": tuple(res_w2),
           "gla_w_in": tuple(res_gla_in), "gla_w_out": tuple(res_gla_out),
           "ab_w_out": tuple(_adamw(ab_w_out, pieces_mlp0[2], m_ab_w_out, v_ab_w_out, "adamw_ab_out"))}

    off = SUBLANES
    for n in small_names:
        a = small_full[n]
        gfull = summed[off:off + a.size // LANES].reshape(a.shape)
        off += _part_rows(a)
        local = w_loc[n].shape
        if n in SMALL_SHARDED:
            gfull = lax.dynamic_slice_in_dim(gfull, dev * local[-1], local[-1], axis=gfull.ndim - 1)
        flat = (-1, local[-1])
        outs = _adamw_whole(w_loc[n].reshape(flat), gfull.reshape(flat), m_loc[n].reshape(flat),
                            v_loc[n].reshape(flat), "adamw_" + n)
        res[n] = tuple(o.reshape(local) for o in outs)
    others_done = (res_w1[1][0, 0:SUBLANES, 0:LANES] + res_w2[1][0, 0:SUBLANES, 0:LANES]
                   + res_gla_in[1][0, 0:SUBLANES, 0:LANES])
    pieces_ab = reduce_end(rc_ici, rc_mine, others_done, "ab")
    res["ab_w_in"] = tuple(_adamw(ab_w_in, pieces_ab[0], m_ab_w_in, v_ab_w_in, "adamw_ab_in"))

    grad_x = dx.reshape(1, T, D_MODEL)
    out = [loss, grad_x]
    for k in range(4):
        out += [res[n][k] for n in WEIGHT_NAMES]
    return tuple(out)
```

```python
import jax
import jax.numpy as jnp
from jax import lax
from jax.experimental import pallas as pl
from jax.experimental.pallas import tpu as pltpu

F32, BF16 = jnp.float32, jnp.bfloat16
MESH = pl.DeviceIdType.MESH

D_MODEL = 1024
D_FF = 4096
RG_W = 512
HG_W = 512
CHUNK = 64
EPS = 1e-6
RG_C = 8.0
AB_IN = 3584
GLA_IN = 3104
GLA_IN_PAD = 3200
N_DEV = 8
LANES = 128
SUBLANES = 8
VMEM_LIMIT = 48 * 1024 * 1024

ADAM_LR, ADAM_B1, ADAM_B2, ADAM_EPS, ADAM_WD, ADAM_STEP = 0.001, 0.9, 0.999, 1e-08, 0.01, 10


def _params(*sem):
    return pltpu.CompilerParams(dimension_semantics=sem, vmem_limit_bytes=VMEM_LIMIT)


def _dg(a, b, ca, cb):
    return lax.dot_general(a.astype(BF16), b.astype(BF16), (((ca,), (cb,)), ((), ())),
                           preferred_element_type=F32)


@jax.custom_vjp
def _mm_nn(a, b):
    return _dg(a, b, 1, 0)


_mm_nn.defvjp(lambda a, b: (_dg(a, b, 1, 0), (a, b)),
              lambda res, g: (_dg(g, res[1], 1, 1), _dg(res[0], g, 0, 0)))


@jax.custom_vjp
def _mm_nt(a, b):
    return _dg(a, b, 1, 1)


_mm_nt.defvjp(lambda a, b: (_dg(a, b, 1, 1), (a, b)),
              lambda res, g: (_dg(g, res[1], 1, 0), _dg(g, res[0], 0, 0)))


@jax.custom_vjp
def _mm_tn(a, b):
    return _dg(a, b, 0, 0)


_mm_tn.defvjp(lambda a, b: (_dg(a, b, 0, 0), (a, b)),
              lambda res, g: (_dg(res[1], g, 1, 1), _dg(res[0], g, 1, 0)))


def _tri_dot(tri, x):
    hi = x.astype(BF16)
    lo = (x - hi.astype(F32)).astype(BF16)
    t = tri.astype(BF16)
    return jnp.dot(t, hi, preferred_element_type=F32) + jnp.dot(t, lo, preferred_element_type=F32)


@jax.custom_vjp
def _cum(tri, tri_t, x):
    return _tri_dot(tri, x)


_cum.defvjp(lambda tri, tri_t, x: (_tri_dot(tri, x), (tri, tri_t)),
            lambda res, g: (jnp.zeros_like(res[0]), jnp.zeros_like(res[1]), _tri_dot(res[1], g)))


@jax.custom_vjp
def _sig(x):
    return 1.0 / (1.0 + jnp.exp(-x))


_sig.defvjp(lambda x: (lambda s: (s, s))(1.0 / (1.0 + jnp.exp(-x))),
            lambda s, g: (g * s * (1.0 - s),))


def _gelu(x):
    return 0.5 * x * (1.0 + jnp.tanh(0.7978845608028654 * (x + 0.044715 * (x * x * x))))


def _softplus(z):
    return jnp.maximum(z, 0.0) + jnp.log(1.0 + jnp.exp(-jnp.abs(z)))


def _rms(x):
    return lax.rsqrt(jnp.mean(x * x, axis=-1, keepdims=True) + EPS)


def _rmsnorm_bwd(x, gain, dy):
    r = _rms(x)
    xh = x * r
    dgain = jnp.sum(dy * xh, axis=0, keepdims=True)
    dxh = dy * gain
    dx = r * (dxh - xh * jnp.mean(dxh * xh, axis=-1, keepdims=True))
    return dx, dgain


def _headnorm(o, gain, n_heads, hd):
    parts = []
    for h in range(n_heads):
        oh = o[:, h * hd:(h + 1) * hd]
        parts.append(oh * _rms(oh))
    return jnp.concatenate(parts, axis=1) * gain


def _tri_consts(d):
    row = lax.broadcasted_iota(jnp.int32, (CHUNK, CHUNK), 0)
    col = lax.broadcasted_iota(jnp.int32, (CHUNK, CHUNK), 1)
    ge = (row >= col).astype(F32)
    le = (row <= col).astype(F32)
    r1 = lax.broadcasted_iota(jnp.int32, (CHUNK, 1), 0)
    if d == 0:
        return ge, le, (r1 <= CHUNK // 2).astype(F32)
    return le, ge, (r1 >= CHUNK // 2 - 1).astype(F32)


def _chunk_core(qh, k, v, logf, st_prev, tri, tri_t, mref, n_heads, dk, dv):
    cum = _cum(tri, tri_t, logf)
    ref = jnp.sum(logf * mref, axis=0, keepdims=True)
    last = jnp.sum(logf, axis=0, keepdims=True)
    q_in = qh * jnp.exp(cum - ref)
    k_in = k * jnp.exp(ref - cum)
    k_st = k * jnp.exp(last - cum)
    q_dec = qh * jnp.exp(cum)
    decay = jnp.exp(last)
    outs, sts = [], []
    for h in range(n_heads):
        sk = slice(h * dk, (h + 1) * dk)
        sv = slice(h * dv, (h + 1) * dv)
        sc = _mm_nt(q_in[:, sk], k_in[:, sk]) * tri
        o = _mm_nn(sc, v[:, sv]) + _mm_nt(q_dec[:, sk], st_prev[h])
        sts.append(st_prev[h] * decay[:, sk] + _mm_tn(v[:, sv], k_st[:, sk]))
        outs.append(o)
    return jnp.concatenate(outs, axis=1), tuple(sts)


def _hg_chunk(q, f, v, l0, l1, st_prev, tri, tri_t, mref):
    lb = _sig(l0 - l1)
    sg = _sig(f)
    qh = q * _sig(q)
    logf = jnp.log(lb + (1.0 - lb) * sg)
    k = (1.0 - lb) * (1.0 - sg)
    return _chunk_core(qh, k, v, logf, st_prev, tri, tri_t, mref, 4, 128, 128)


def _gla_chunk(q, k, v, z, st_prev, tri, tri_t, mref):
    logf = (jnp.minimum(z, 0.0) - jnp.log(1.0 + jnp.exp(-jnp.abs(z)))) * (1.0 / 16.0)
    qh = q * (128.0 ** -0.5)
    return _chunk_core(qh, k, v, logf, st_prev, tri, tri_t, mref, 4, 128, 256)


def _rg_gates(xc, wbd, bias, lam):
    z = _mm_nn(xc, wbd) + bias
    r = _sig(z[:, :RG_W])
    i = _sig(z[:, RG_W:])
    log_a = -RG_C * r * _softplus(-lam)
    a = jnp.exp(log_a)
    x2 = 2.0 * log_a
    neg_expm1 = jnp.where(x2 > -1e-2, -(x2 + 0.5 * x2 * x2 + x2 * x2 * x2 * (1.0 / 6.0)), 1.0 - jnp.exp(x2))
    u = jnp.sqrt(neg_expm1) * (i * xc)
    return a, u


def _l0_combine(hf, hb, ga, of, ob, g, gain):
    ya = (hf + hb) * _gelu(ga)
    yb = _headnorm(of + ob, gain, 4, 128) * (g * _sig(g))
    return jnp.concatenate([ya, yb], axis=1)


def _l1_combine(of, ob, r, gain):
    return _headnorm(of + ob, gain, 4, 256) * (r * _sig(r))


def _norm_matmul(h, gain, w, name):
    T, D = h.shape
    N = w.shape[1]
    tm = min(512, T)

    def body(h_ref, g_ref, w_ref, o_ref, y_ref):
        x = h_ref[...]
        y = (x * _rms(x) * g_ref[...]).astype(BF16)
        y_ref[...] = y
        o_ref[...] = jnp.dot(y, w_ref[...], preferred_element_type=F32)

    return pl.pallas_call(
        body, name=name, grid=(T // tm,),
        in_specs=[pl.BlockSpec((tm, D), lambda i: (i, 0)), pl.BlockSpec((1, D), lambda i: (0, 0)),
                  pl.BlockSpec((D, N), lambda i: (0, 0))],
        out_specs=[pl.BlockSpec((tm, N), lambda i: (i, 0)), pl.BlockSpec((tm, D), lambda i: (i, 0))],
        out_shape=[jax.ShapeDtypeStruct((T, N), F32), jax.ShapeDtypeStruct((T, D), BF16)],
        compiler_params=_params("parallel"))(h, gain, w)


def _matmul_res(a, w, res, name):
    T, K = a.shape
    N = w.shape[1]
    tm = min(512, T)

    def body(a_ref, w_ref, r_ref, o_ref):
        o_ref[...] = r_ref[...] + jnp.dot(a_ref[...], w_ref[...], preferred_element_type=F32)

    return pl.pallas_call(
        body, name=name, grid=(T // tm,),
        in_specs=[pl.BlockSpec((tm, K), lambda i: (i, 0)), pl.BlockSpec((K, N), lambda i: (0, 0)),
                  pl.BlockSpec((tm, N), lambda i: (i, 0))],
        out_specs=pl.BlockSpec((tm, N), lambda i: (i, 0)),
        out_shape=jax.ShapeDtypeStruct((T, N), F32),
        compiler_params=_params("parallel"))(a, w, res)


def _dgrad_norm(dproj, w, h, gain, dres, name):
    T, N = dproj.shape
    D = w.shape[0]
    tm = min(512, T)

    def body(dp_ref, w_ref, h_ref, g_ref, dr_ref, dh_ref, dhb_ref, dg_ref):
        @pl.when(pl.program_id(0) == 0)
        def _():
            dg_ref[...] = jnp.zeros_like(dg_ref)

        dy = _dg(dp_ref[...], w_ref[...], 1, 1)
        dx, dgain = _rmsnorm_bwd(h_ref[...], g_ref[...], dy)
        dh = dr_ref[...] + dx
        dh_ref[...] = dh
        dhb_ref[...] = dh.astype(BF16)
        dg_ref[...] += dgain

    return pl.pallas_call(
        body, name=name, grid=(T // tm,),
        in_specs=[pl.BlockSpec((tm, N), lambda i: (i, 0)), pl.BlockSpec((D, N), lambda i: (0, 0)),
                  pl.BlockSpec((tm, D), lambda i: (i, 0)), pl.BlockSpec((1, D), lambda i: (0, 0)),
                  pl.BlockSpec((tm, D), lambda i: (i, 0))],
        out_specs=[pl.BlockSpec((tm, D), lambda i: (i, 0)), pl.BlockSpec((tm, D), lambda i: (i, 0)),
                   pl.BlockSpec((1, D), lambda i: (0, 0))],
        out_shape=[jax.ShapeDtypeStruct((T, D), F32), jax.ShapeDtypeStruct((T, D), BF16),
                   jax.ShapeDtypeStruct((1, D), F32)],
        compiler_params=_params("arbitrary"))(dproj, w, h, gain, dres)


def _wgrad(a, b, tn, name, sharded_cols=False, behind=None):
    T, K = a.shape
    N = b.shape[1]
    tk = min(1024, K)

    def body(a_ref, b_ref, *rest):
        rest[-1][...] = _dg(a_ref[...], b_ref[...], 0, 0)

    if sharded_cols:
        out_spec = pl.BlockSpec((None, tk, tn), lambda k, n: (n, k, 0))
        out_shape = jax.ShapeDtypeStruct((N // tn, K, tn), F32)
    else:
        out_spec = pl.BlockSpec((tk, tn), lambda k, n: (k, n))
        out_shape = jax.ShapeDtypeStruct((K, N), F32)
    in_specs = [pl.BlockSpec((T, tk), lambda k, n: (0, k)), pl.BlockSpec((T, tn), lambda k, n: (0, n))]
    args = [a, b]
    if behind is not None:
        in_specs.append(pl.BlockSpec((SUBLANES, LANES), lambda k, n: (0, 0)))
        args.append(behind)
    return pl.pallas_call(
        body, name=name, grid=(K // tk, N // tn), in_specs=in_specs, out_specs=out_spec, out_shape=out_shape,
        compiler_params=_params("parallel", "parallel"))(*args)


def _resident(shape):
    return pl.BlockSpec(shape, lambda i: (0,) * len(shape), pipeline_mode=pl.Buffered(1))


def _mlp_fwd(h, gain, w1g, w2, name):
    T, D = h.shape
    nf, _, tf = w1g.shape
    tm = min(512, T)

    def body(h_ref, g_ref, w1_ref, w2_ref, o_ref, pre_ref, y_ref):
        x = h_ref[...]
        y = (x * _rms(x) * g_ref[...]).astype(BF16)
        y_ref[...] = y
        acc = x
        for j in range(nf):
            cols = slice(j * tf, (j + 1) * tf)
            pre = jnp.dot(y, w1_ref[j], preferred_element_type=F32)
            pre_ref[:, cols] = pre.astype(BF16)
            act = jnp.square(jnp.maximum(pre, 0.0)).astype(BF16)
            acc = acc + jnp.dot(act, w2_ref[cols, :], preferred_element_type=F32)
        o_ref[...] = acc

    return pl.pallas_call(
        body, name=name, grid=(T // tm,),
        in_specs=[pl.BlockSpec((tm, D), lambda i: (i, 0)), pl.BlockSpec((1, D), lambda i: (0, 0)),
                  _resident(w1g.shape), _resident(w2.shape)],
        out_specs=[pl.BlockSpec((tm, D), lambda i: (i, 0)), pl.BlockSpec((tm, nf * tf), lambda i: (i, 0)),
                   pl.BlockSpec((tm, D), lambda i: (i, 0))],
        out_shape=[jax.ShapeDtypeStruct((T, D), F32), jax.ShapeDtypeStruct((T, nf * tf), BF16),
                   jax.ShapeDtypeStruct((T, D), BF16)],
        compiler_params=_params("parallel"))(h, gain, w1g, w2)


def _mlp_bwd(dout, dout_b, h, gain, pre, w1g, w2, name):
    T, D = h.shape
    nf, _, tf = w1g.shape
    tm = min(256, T)

    def body(do_ref, dob_ref, h_ref, g_ref, pre_ref, w1_ref, w2_ref, dh_ref, dhb_ref, dpre_ref, act_ref, dg_ref):
        @pl.when(pl.program_id(0) == 0)
        def _():
            dg_ref[...] = jnp.zeros_like(dg_ref)

        dob = dob_ref[...]
        dy = None
        for j in range(nf):
            cols = slice(j * tf, (j + 1) * tf)
            rp = jnp.maximum(pre_ref[:, cols].astype(F32), 0.0)
            dpre = (_dg(dob, w2_ref[cols, :], 1, 1) * (2.0 * rp)).astype(BF16)
            dpre_ref[:, cols] = dpre
            act_ref[:, cols] = (rp * rp).astype(BF16)
            part = _dg(dpre, w1_ref[j], 1, 1)
            dy = part if dy is None else dy + part
        dx, dgain = _rmsnorm_bwd(h_ref[...], g_ref[...], dy)
        dh = do_ref[...] + dx
        dh_ref[...] = dh
        dhb_ref[...] = dh.astype(BF16)
        dg_ref[...] += dgain

    tok = lambda w: pl.BlockSpec((tm, w), lambda i: (i, 0))
    return pl.pallas_call(
        body, name=name, grid=(T // tm,),
        in_specs=[tok(D), tok(D), tok(D), pl.BlockSpec((1, D), lambda i: (0, 0)), tok(nf * tf),
                  _resident(w1g.shape), _resident(w2.shape)],
        out_specs=[tok(D), tok(D), tok(nf * tf), tok(nf * tf), pl.BlockSpec((1, D), lambda i: (0, 0))],
        out_shape=[jax.ShapeDtypeStruct((T, D), F32), jax.ShapeDtypeStruct((T, D), BF16),
                   jax.ShapeDtypeStruct((T, nf * tf), BF16),
                   jax.ShapeDtypeStruct((T, nf * tf), BF16), jax.ShapeDtypeStruct((1, D), F32)],
        compiler_params=_params("arbitrary"))(dout, dout_b, h, gain, pre, w1g, w2)


def _final_loss(h, gain, target, name):
    T, D = h.shape
    tm = min(512, T)

    def body(h_ref, g_ref, t_ref, l_ref, dh_ref, dhb_ref, dg_ref):
        @pl.when(pl.program_id(0) == 0)
        def _():
            l_ref[...] = jnp.zeros_like(l_ref)
            dg_ref[...] = jnp.zeros_like(dg_ref)

        x = h_ref[...]
        err = x * _rms(x) * g_ref[...] - t_ref[...]
        l_ref[...] += 0.5 * jnp.sum(jnp.mean(err * err, axis=-1, keepdims=True), axis=0, keepdims=True)
        dx, dgain = _rmsnorm_bwd(x, g_ref[...], err * (1.0 / D))
        dh_ref[...] = dx
        dhb_ref[...] = dx.astype(BF16)
        dg_ref[...] += dgain

    return pl.pallas_call(
        body, name=name, grid=(T // tm,),
        in_specs=[pl.BlockSpec((tm, D), lambda i: (i, 0)), pl.BlockSpec((1, D), lambda i: (0, 0)),
                  pl.BlockSpec((tm, D), lambda i: (i, 0))],
        out_specs=[pl.BlockSpec((SUBLANES, LANES), lambda i: (0, 0)), pl.BlockSpec((tm, D), lambda i: (i, 0)),
                   pl.BlockSpec((tm, D), lambda i: (i, 0)), pl.BlockSpec((1, D), lambda i: (0, 0))],
        out_shape=[jax.ShapeDtypeStruct((SUBLANES, LANES), F32), jax.ShapeDtypeStruct((T, D), F32),
                   jax.ShapeDtypeStruct((T, D), BF16), jax.ShapeDtypeStruct((1, D), F32)],
        compiler_params=_params("arbitrary"))(h, gain, target)


def _halo_specs(tm, T, width, col, tile=lambda i: i):
    r8 = tm // SUBLANES
    nb8 = T // SUBLANES
    return [pl.BlockSpec((tm, width), lambda i: (tile(i), col)),
            pl.BlockSpec((SUBLANES, width), lambda i: (jnp.maximum(tile(i) * r8 - 1, 0), col)),
            pl.BlockSpec((SUBLANES, width), lambda i: (jnp.minimum((tile(i) + 1) * r8, nb8 - 1), col))]


def _ext(cur, prev, nxt, has_prev, has_next):
    return jnp.concatenate([jnp.where(has_prev, prev, 0.0), cur, jnp.where(has_next, nxt, 0.0)], axis=0)


def _shifted(ext, offset, tm):
    n = ext.shape[0]
    sh = (-offset) % n
    r = ext if sh == 0 else pltpu.roll(ext, sh, 0)
    return r[SUBLANES:SUBLANES + tm]


def _rg_conv_fwd(proj, cw8, cb, name):
    T = proj.shape[0]
    tm = min(512, T)
    nT = T // tm

    def body(cur_ref, prev_ref, next_ref, w_ref, b_ref, o_ref):
        i = pl.program_id(0)
        ext = _ext(cur_ref[...], prev_ref[...], next_ref[...], i > 0, i < nT - 1)
        acc = jnp.broadcast_to(b_ref[...], (tm, RG_W))
        for k in range(4):
            acc = acc + w_ref[k:k + 1, :] * _shifted(ext, k - 2, tm)
        o_ref[...] = acc

    return pl.pallas_call(
        body, name=name, grid=(nT,),
        in_specs=_halo_specs(tm, T, RG_W, 0) + [pl.BlockSpec((SUBLANES, RG_W), lambda i: (0, 0)),
                                                pl.BlockSpec((1, RG_W), lambda i: (0, 0))],
        out_specs=pl.BlockSpec((tm, RG_W), lambda i: (i, 0)),
        out_shape=jax.ShapeDtypeStruct((T, RG_W), F32),
        compiler_params=_params("parallel"))(proj, proj, proj, cw8, cb)


def _rg_conv_bwd(dxc, proj, cw8, name):
    T = proj.shape[0]
    tm = min(512, T)
    nT = T // tm

    def body(a0, p0, n0, a1, p1, n1, xa, xp, xn, w_ref, dxa_ref, dw_ref, db_ref):
        i = pl.program_id(0)

        @pl.when(i == 0)
        def _():
            dw_ref[...] = jnp.zeros_like(dw_ref)
            db_ref[...] = jnp.zeros_like(db_ref)

        has_p, has_n = i > 0, i < nT - 1
        cur = a0[...] + a1[...]
        dext = _ext(cur, p0[...] + p1[...], n0[...] + n1[...], has_p, has_n)
        xext = _ext(xa[...], xp[...], xn[...], has_p, has_n)
        acc = jnp.zeros((tm, RG_W), F32)
        rows = []
        for k in range(4):
            acc = acc + w_ref[k:k + 1, :] * _shifted(dext, 2 - k, tm)
            rows.append(jnp.sum(cur * _shifted(xext, k - 2, tm), axis=0, keepdims=True))
        dxa_ref[...] = acc
        dw_ref[...] += jnp.concatenate(rows + [jnp.zeros((4, RG_W), F32)], axis=0)
        db_ref[...] += jnp.sum(cur, axis=0, keepdims=True)

    return pl.pallas_call(
        body, name=name, grid=(nT,),
        in_specs=(_halo_specs(tm, T, RG_W, 0) + _halo_specs(tm, T, RG_W, 0)
                  + _halo_specs(tm, T, RG_W, 0) + [pl.BlockSpec((SUBLANES, RG_W), lambda i: (0, 0))]),
        out_specs=[pl.BlockSpec((tm, RG_W), lambda i: (i, 0)), pl.BlockSpec((SUBLANES, RG_W), lambda i: (0, 0)),
                   pl.BlockSpec((1, RG_W), lambda i: (0, 0))],
        out_shape=[jax.ShapeDtypeStruct((T, RG_W), F32), jax.ShapeDtypeStruct((SUBLANES, RG_W), F32),
                   jax.ShapeDtypeStruct((1, RG_W), F32)],
        compiler_params=_params("arbitrary"))(dxc[0], dxc[0], dxc[0], dxc[1], dxc[1], dxc[1], proj, proj, proj, cw8)


def _local_scan(a, b, ascending):
    n = a.shape[0]
    pos = jnp.bitwise_and(lax.broadcasted_iota(jnp.int32, a.shape, 0), SUBLANES - 1)
    for s in (1, 2, 4):
        sh = s if ascending else n - s
        ok = (pos >= s) if ascending else (pos < SUBLANES - s)
        a_sh, b_sh = pltpu.roll(a, sh, 0), pltpu.roll(b, sh, 0)
        b = jnp.where(ok, a * b_sh + b, b)
        a = jnp.where(ok, a * a_sh, a)
    return a, b


def _group_scan(chains, a_sc, b_sc, carry, n_groups):
    def step(g, hs):
        new = []
        for (d, out_ref, asc), h in zip(chains, hs):
            r0 = pl.multiple_of((g if asc else n_groups - 1 - g) * SUBLANES, SUBLANES)
            out_ref[pl.ds(r0, SUBLANES), :] = a_sc[d, pl.ds(r0, SUBLANES), :] * h + b_sc[d, pl.ds(r0, SUBLANES), :]
            new.append(out_ref[pl.ds(r0 + (SUBLANES - 1 if asc else 0), 1), :])
        return tuple(new)

    hs = lax.fori_loop(0, n_groups, step, tuple(carry[d, 0:1, :] for d, _, _ in chains))
    for (d, _, _), h in zip(chains, hs):
        carry[d, 0:1, :] = h


def _rg_scan_fwd(xc, wbd, bias, lam, name):
    T = xc.shape[0]
    tm = min(512, T)
    nT = T // tm

    def body(xf_ref, xb_ref, w_ref, b_ref, lam_ref, hf_ref, hb_ref, a_sc, b_sc, carry):
        @pl.when(pl.program_id(0) == 0)
        def _():
            carry[...] = jnp.zeros_like(carry)

        for d, x_ref in enumerate((xf_ref, xb_ref)):
            a, u = _rg_gates(x_ref[...], w_ref[d], b_ref[d], lam_ref[d])
            a_sc[d], b_sc[d] = _local_scan(a, u, d == 0)
        _group_scan(((0, hf_ref, True), (1, hb_ref, False)), a_sc, b_sc, carry, tm // SUBLANES)

    full = lambda a: pl.BlockSpec(a.shape, lambda i: (0,) * len(a.shape))
    res = pl.pallas_call(
        body, name=name, grid=(nT,),
        in_specs=[pl.BlockSpec((tm, RG_W), lambda i: (i, 0)), pl.BlockSpec((tm, RG_W), lambda i: (nT - 1 - i, 0)),
                  full(wbd), full(bias), full(lam)],
        out_specs=[pl.BlockSpec((tm, RG_W), lambda i: (i, 0)), pl.BlockSpec((tm, RG_W), lambda i: (nT - 1 - i, 0))],
        out_shape=[jax.ShapeDtypeStruct((T, RG_W), F32)] * 2,
        scratch_shapes=[pltpu.VMEM((2, tm, RG_W), F32), pltpu.VMEM((2, tm, RG_W), F32),
                        pltpu.VMEM((2, SUBLANES, RG_W), F32)],
        compiler_params=_params("arbitrary"))(xc, xc, wbd, bias, lam)
    return res[0], res[1]


def _rg_scan_bwd(xc, wbd, bias, lam, hs, dho, name):
    T = xc.shape[0]
    tm = min(256, T)
    nT = T // tm
    tiles = (lambda i: nT - 1 - i, lambda i: i)

    def body(xf_ref, xb_ref, w_ref, b_ref, lam_ref, hfc, hfp, hfn, hbc, hbp, hbn, dof_ref, dob_ref,
             dxf_ref, dxb_ref, dw_ref, db_ref, dlam_ref, a_sc, b_sc, y_sc, carry):
        i = pl.program_id(0)

        @pl.when(i == 0)
        def _():
            carry[...] = jnp.zeros_like(carry)
            dw_ref[...] = jnp.zeros_like(dw_ref)
            db_ref[...] = jnp.zeros_like(db_ref)
            dlam_ref[...] = jnp.zeros_like(dlam_ref)

        vjps, entering = [], []
        for d, (x_ref, do_ref) in enumerate(((xf_ref, dof_ref), (xb_ref, dob_ref))):
            (a, _), vjp = jax.vjp(_rg_gates, x_ref[...], w_ref[d].astype(F32), b_ref[d], lam_ref[d])
            vjps.append(vjp)
            entering.append(carry[d, 0:1, :])
            a_sc[d], b_sc[d] = _local_scan(a, a * do_ref[...], d == 1)
        _group_scan(((0, y_sc.at[0], False), (1, y_sc.at[1], True)), a_sc, b_sc, carry, tm // SUBLANES)

        row = lax.broadcasted_iota(jnp.int32, (tm, RG_W), 0)
        for d, (do_ref, dx_ref, hc, hp, hn, ti) in enumerate(
                ((dof_ref, dxf_ref, hfc, hfp, hfn, nT - 1 - i), (dob_ref, dxb_ref, hbc, hbp, hbn, i))):
            y = y_sc[d]
            if d == 0:
                y_next = jnp.where(row == tm - 1, entering[d], pltpu.roll(y, tm - 1, 0))
            else:
                y_next = jnp.where(row == 0, entering[d], pltpu.roll(y, 1, 0))
            dtot = do_ref[...] + y_next
            ext = _ext(hc[...], hp[...], hn[...], ti > 0, ti < nT - 1)
            hprev = _shifted(ext, -1 if d == 0 else 1, tm)
            dxc, dw, db, dlam = vjps[d]((dtot * hprev, dtot))
            dx_ref[...] = dxc
            dw_ref[d] += dw
            db_ref[d] += db
            dlam_ref[d] += dlam

    full = lambda a: pl.BlockSpec(a.shape, lambda i: (0,) * len(a.shape))
    tok = lambda d: pl.BlockSpec((tm, RG_W), lambda i: (tiles[d](i), 0))
    acc_shapes = [jax.ShapeDtypeStruct((2, RG_W, 2 * RG_W), F32), jax.ShapeDtypeStruct((2, 1, 2 * RG_W), F32),
                  jax.ShapeDtypeStruct((2, 1, RG_W), F32)]
    res = pl.pallas_call(
        body, name=name, grid=(nT,),
        in_specs=([tok(0), tok(1), full(wbd), full(bias), full(lam)]
                  + _halo_specs(tm, T, RG_W, 0, tiles[0]) + _halo_specs(tm, T, RG_W, 0, tiles[1]) + [tok(0), tok(1)]),
        out_specs=[tok(0), tok(1)] + [full(s) for s in acc_shapes],
        out_shape=[jax.ShapeDtypeStruct((T, RG_W), F32)] * 2 + acc_shapes,
        scratch_shapes=[pltpu.VMEM((2, tm, RG_W), F32), pltpu.VMEM((2, tm, RG_W), F32),
                        pltpu.VMEM((2, tm, RG_W), F32), pltpu.VMEM((2, SUBLANES, RG_W), F32)],
        compiler_params=_params("arbitrary"))(xc, xc, wbd, bias, lam, hs[0], hs[0], hs[0], hs[1], hs[1], hs[1],
                                              dho, dho)
    return (res[0], res[1]), res[2], res[3], res[4]


def _chunk_rows(n_chunks, reverse):
    up, down = (lambda c: c), (lambda c: n_chunks - 1 - c)
    return (down, up) if reverse else (up, down)


STEP_CHUNKS = 4
STEP_ROWS = STEP_CHUNKS * CHUNK


def _sub_chunks(ascending):
    order = range(STEP_CHUNKS) if ascending else range(STEP_CHUNKS - 1, -1, -1)
    return [(s, slice(s * CHUNK, (s + 1) * CHUNK)) for s in order]


def _hg_fwd(proj, l0, l1, name):
    T = proj.shape[0]
    nC = T // CHUNK
    nS = nC // STEP_CHUNKS
    H, dk, dv = 4, 128, 128
    rows = _chunk_rows(nS, False)

    def body(qf, ff, vf, qb, fb, vb, l0_ref, l1_ref, of, ob, spf, spb, st):
        @pl.when(pl.program_id(0) == 0)
        def _():
            st[...] = jnp.zeros_like(st)

        for d, (q, f, v, o, sp) in enumerate(((qf, ff, vf, of, spf), (qb, fb, vb, ob, spb))):
            tri, tri_t, mref = _tri_consts(d)
            stp = tuple(st[d, h] for h in range(H))
            for s, r in _sub_chunks(d == 0):
                for h in range(H):
                    sp[s, h] = stp[h]
                o_val, stp = _hg_chunk(q[r, :], f[r, :], v[r, :], l0_ref[...], l1_ref[...], stp, tri, tri_t, mref)
                o[r, :] = o_val
            for h in range(H):
                st[d, h] = stp[h]

    tok = lambda d, col: pl.BlockSpec((STEP_ROWS, HG_W), lambda c: (rows[d](c), col))
    par = pl.BlockSpec((1, HG_W), lambda c: (0, 0))
    state = lambda d: pl.BlockSpec((STEP_CHUNKS, H, dv, dk), lambda c: (rows[d](c), 0, 0, 0))
    res = pl.pallas_call(
        body, name=name, grid=(nS,),
        in_specs=[tok(0, 2), tok(0, 3), tok(0, 5), tok(1, 2), tok(1, 4), tok(1, 5), par, par],
        out_specs=[tok(0, 0), tok(1, 0), state(0), state(1)],
        out_shape=[jax.ShapeDtypeStruct((T, H * dv), F32)] * 2 + [jax.ShapeDtypeStruct((nC, H, dv, dk), F32)] * 2,
        scratch_shapes=[pltpu.VMEM((2, H, dv, dk), F32)],
        compiler_params=_params("arbitrary"))(proj, proj, proj, proj, proj, proj, l0, l1)
    return (res[0], res[1]), (res[2], res[3])


def _hg_bwd(proj, l0, l1, sprev, do, name):
    T = proj.shape[0]
    nC = T // CHUNK
    nS = nC // STEP_CHUNKS
    H, dk, dv = 4, 128, 128
    rows = _chunk_rows(nS, True)

    def body(qf, ff, vf, qb, fb, vb, l0_ref, l1_ref, spf, spb, dof, dob,
             dqf, dff, dvf, dqb, dfb, dvb, dl0_ref, dl1_ref, dst):
        @pl.when(pl.program_id(0) == 0)
        def _():
            dst[...] = jnp.zeros_like(dst)
            dl0_ref[...] = jnp.zeros_like(dl0_ref)
            dl1_ref[...] = jnp.zeros_like(dl1_ref)

        for d, (q, f, v, sp, do_ref, dq_ref, df_ref, dv_ref) in enumerate(
                ((qf, ff, vf, spf, dof, dqf, dff, dvf), (qb, fb, vb, spb, dob, dqb, dfb, dvb))):
            tri, tri_t, mref = _tri_consts(d)
            fn = lambda q_, f_, v_, a0, a1, stp: _hg_chunk(q_, f_, v_, a0, a1, stp, tri, tri_t, mref)
            dstp = tuple(dst[d, h] for h in range(H))
            for s, r in _sub_chunks(d == 1):
                stp = tuple(sp[s, h] for h in range(H))
                _, vjp = jax.vjp(fn, q[r, :], f[r, :], v[r, :], l0_ref[...], l1_ref[...], stp)
                dq, df, dvv, dl0, dl1, dstp = vjp((do_ref[r, :], dstp))
                dq_ref[r, :] = dq.astype(BF16)
                df_ref[r, :] = df.astype(BF16)
                dv_ref[r, :] = dvv.astype(BF16)
                dl0_ref[d] += dl0
                dl1_ref[d] += dl1
            for h in range(H):
                dst[d, h] = dstp[h]

    tok = lambda d, col: pl.BlockSpec((STEP_ROWS, HG_W), lambda c: (rows[d](c), col))
    par = pl.BlockSpec((1, HG_W), lambda c: (0, 0))
    acc = pl.BlockSpec((2, 1, HG_W), lambda c: (0, 0, 0))
    state = lambda d: pl.BlockSpec((STEP_CHUNKS, H, dv, dk), lambda c: (rows[d](c), 0, 0, 0))
    res = pl.pallas_call(
        body, name=name, grid=(nS,),
        in_specs=[tok(0, 2), tok(0, 3), tok(0, 5), tok(1, 2), tok(1, 4), tok(1, 5), par, par,
                  state(0), state(1), tok(0, 0), tok(1, 0)],
        out_specs=[tok(0, 0)] * 3 + [tok(1, 0)] * 3 + [acc, acc],
        out_shape=[jax.ShapeDtypeStruct((T, HG_W), BF16)] * 6 + [jax.ShapeDtypeStruct((2, 1, HG_W), F32)] * 2,
        scratch_shapes=[pltpu.VMEM((2, H, dv, dk), F32)],
        compiler_params=_params("arbitrary"))(proj, proj, proj, proj, proj, proj, l0, l1, sprev[0], sprev[1], do, do)
    return (res[0], res[3]), (res[1], res[4]), (res[2], res[5]), res[6], res[7]


def _gate_logits(proj, wup, bg, name):
    T = proj.shape[0]
    tm = min(512, T)

    def body(lr_ref, w_ref, b_ref, z_ref, lrb_ref):
        lr = lr_ref[...].astype(BF16)
        lrb_ref[...] = lr
        for d in range(2):
            z_ref[d] = _dg(lr, w_ref[d], 1, 0) + b_ref[d]

    return pl.pallas_call(
        body, name=name, grid=(T // tm,),
        in_specs=[pl.BlockSpec((tm, LANES), lambda i: (i, 24)), pl.BlockSpec((2, LANES, 512), lambda i: (0, 0, 0)),
                  pl.BlockSpec((2, 1, 512), lambda i: (0, 0, 0))],
        out_specs=[pl.BlockSpec((2, tm, 512), lambda i: (0, i, 0)), pl.BlockSpec((tm, LANES), lambda i: (i, 0))],
        out_shape=[jax.ShapeDtypeStruct((2, T, 512), F32), jax.ShapeDtypeStruct((T, LANES), BF16)],
        compiler_params=_params("parallel"))(proj, wup, bg)


def _gate_logits_bwd(dz, wup, name):
    T = dz[0].shape[0]
    tm = min(512, T)

    def body(dzf_ref, dzb_ref, w_ref, dlr_ref, db_ref, dzb16_ref):
        @pl.when(pl.program_id(0) == 0)
        def _():
            db_ref[...] = jnp.zeros_like(db_ref)

        acc = jnp.zeros((tm, LANES), F32)
        for d, dz_ref in enumerate((dzf_ref, dzb_ref)):
            g = dz_ref[...]
            gb = g.astype(BF16)
            dzb16_ref[d] = gb
            acc = acc + _dg(gb, w_ref[d], 1, 1)
            db_ref[d] += jnp.sum(g, axis=0, keepdims=True)
        dlr_ref[...] = acc

    tok = pl.BlockSpec((tm, 512), lambda i: (i, 0))
    return pl.pallas_call(
        body, name=name, grid=(T // tm,),
        in_specs=[tok, tok, pl.BlockSpec((2, LANES, 512), lambda i: (0, 0, 0))],
        out_specs=[pl.BlockSpec((tm, LANES), lambda i: (i, 0)), pl.BlockSpec((2, 1, 512), lambda i: (0, 0, 0)),
                   pl.BlockSpec((2, tm, 512), lambda i: (0, i, 0))],
        out_shape=[jax.ShapeDtypeStruct((T, LANES), F32), jax.ShapeDtypeStruct((2, 1, 512), F32),
                   jax.ShapeDtypeStruct((2, T, 512), BF16)],
        compiler_params=_params("arbitrary"))(dz[0], dz[1], wup)


def _gla_fwd(proj, z, name):
    T = proj.shape[0]
    nC = T // CHUNK
    nS = nC // STEP_CHUNKS
    H, dk, dv = 4, 128, 256
    rows = _chunk_rows(nS, False)

    def body(qf, kf, vf, zf, qb, kb, vb, zb, of, ob, spf, spb, st):
        @pl.when(pl.program_id(0) == 0)
        def _():
            st[...] = jnp.zeros_like(st)

        for d, (q, k, v, z_ref, o, sp) in enumerate(((qf, kf, vf, zf, of, spf), (qb, kb, vb, zb, ob, spb))):
            tri, tri_t, mref = _tri_consts(d)
            stp = tuple(st[d, h] for h in range(H))
            for s, r in _sub_chunks(d == 0):
                for h in range(H):
                    sp[s, h] = stp[h]
                o_val, stp = _gla_chunk(q[r, :], k[r, :], v[r, :], z_ref[r, :], stp, tri, tri_t, mref)
                o[r, :] = o_val
            for h in range(H):
                st[d, h] = stp[h]

    tok = lambda d, w, col: pl.BlockSpec((STEP_ROWS, w), lambda c: (rows[d](c), col))
    gate = lambda d: pl.BlockSpec((None, STEP_ROWS, 512), lambda c: (d, rows[d](c), 0))
    state = lambda d: pl.BlockSpec((STEP_CHUNKS, H, dv, dk), lambda c: (rows[d](c), 0, 0, 0))
    res = pl.pallas_call(
        body, name=name, grid=(nS,),
        in_specs=[tok(0, 512, 0), tok(0, 512, 1), tok(0, 1024, 1), gate(0),
                  tok(1, 512, 0), tok(1, 512, 1), tok(1, 1024, 1), gate(1)],
        out_specs=[tok(0, H * dv, 0), tok(1, H * dv, 0), state(0), state(1)],
        out_shape=[jax.ShapeDtypeStruct((T, H * dv), F32)] * 2 + [jax.ShapeDtypeStruct((nC, H, dv, dk), F32)] * 2,
        scratch_shapes=[pltpu.VMEM((2, H, dv, dk), F32)],
        compiler_params=_params("arbitrary"))(proj, proj, proj, z, proj, proj, proj, z)
    return (res[0], res[1]), (res[2], res[3])


def _gla_bwd(proj, z, sprev, do, name):
    T = proj.shape[0]
    nC = T // CHUNK
    nS = nC // STEP_CHUNKS
    H, dk, dv = 4, 128, 256
    rows = _chunk_rows(nS, True)

    def body(qf, kf, vf, zf, qb, kb, vb, zb, spf, spb, dof, dob,
             dqf, dkf, dvf, dzf, dqb, dkb, dvb, dzb, dst):
        @pl.when(pl.program_id(0) == 0)
        def _():
            dst[...] = jnp.zeros_like(dst)

        for d, (q, k, v, z_ref, sp, do_ref, dq_ref, dk_ref, dv_ref, dz_ref) in enumerate(
                ((qf, kf, vf, zf, spf, dof, dqf, dkf, dvf, dzf), (qb, kb, vb, zb, spb, dob, dqb, dkb, dvb, dzb))):
            tri, tri_t, mref = _tri_consts(d)
            fn = lambda q_, k_, v_, z_, stp: _gla_chunk(q_, k_, v_, z_, stp, tri, tri_t, mref)
            dstp = tuple(dst[d, h] for h in range(H))
            for s, r in _sub_chunks(d == 1):
                stp = tuple(sp[s, h] for h in range(H))
                _, vjp = jax.vjp(fn, q[r, :], k[r, :], v[r, :], z_ref[r, :], stp)
                dq, dkk, dvv, dzz, dstp = vjp((do_ref[r, :], dstp))
                dq_ref[r, :] = dq.astype(BF16)
                dk_ref[r, :] = dkk.astype(BF16)
                dv_ref[r, :] = dvv.astype(BF16)
                dz_ref[r, :] = dzz
            for h in range(H):
                dst[d, h] = dstp[h]

    tok = lambda d, w, col: pl.BlockSpec((STEP_ROWS, w), lambda c: (rows[d](c), col))
    gate = lambda d: pl.BlockSpec((None, STEP_ROWS, 512), lambda c: (d, rows[d](c), 0))
    state = lambda d: pl.BlockSpec((STEP_CHUNKS, H, dv, dk), lambda c: (rows[d](c), 0, 0, 0))
    outs = lambda d: [tok(d, 512, 0), tok(d, 512, 0), tok(d, 1024, 0), tok(d, 512, 0)]
    shapes = [jax.ShapeDtypeStruct((T, 512), BF16), jax.ShapeDtypeStruct((T, 512), BF16),
              jax.ShapeDtypeStruct((T, 1024), BF16), jax.ShapeDtypeStruct((T, 512), F32)]
    res = pl.pallas_call(
        body, name=name, grid=(nS,),
        in_specs=[tok(0, 512, 0), tok(0, 512, 1), tok(0, 1024, 1), gate(0),
                  tok(1, 512, 0), tok(1, 512, 1), tok(1, 1024, 1), gate(1),
                  state(0), state(1), tok(0, H * dv, 0), tok(1, H * dv, 0)],
        out_specs=outs(0) + outs(1), out_shape=shapes + shapes,
        scratch_shapes=[pltpu.VMEM((2, H, dv, dk), F32)],
        compiler_params=_params("arbitrary"))(proj, proj, proj, z, proj, proj, proj, z, sprev[0], sprev[1], do, do)
    return (res[0], res[4]), (res[1], res[5]), (res[2], res[6]), (res[3], res[7])


def _l0_combine_fwd(hs, proj, o, gain, name):
    T = proj.shape[0]
    tm = min(512, T)

    def body(hf, hb, ga, of, ob, g, gn, out):
        out[...] = _l0_combine(hf[...], hb[...], ga[...], of[...], ob[...], g[...], gn[...]).astype(BF16)

    tok = pl.BlockSpec((tm, 512), lambda i: (i, 0))
    return pl.pallas_call(
        body, name=name, grid=(T // tm,),
        in_specs=[tok, tok, pl.BlockSpec((tm, 512), lambda i: (i, 1)), tok, tok,
                  pl.BlockSpec((tm, 512), lambda i: (i, 6)), pl.BlockSpec((1, 512), lambda i: (0, 0))],
        out_specs=pl.BlockSpec((tm, 1024), lambda i: (i, 0)),
        out_shape=jax.ShapeDtypeStruct((T, 1024), BF16),
        compiler_params=_params("parallel"))(hs[0], hs[1], proj, o[0], o[1], proj, gain)


def _l0_combine_bwd(hs, proj, o, gain, dh_b, w_out, name):
    T = proj.shape[0]
    tm = min(512, T)

    def body(hf, hb, ga, of, ob, g, gn, dhb_ref, w_ref, dho_ref, dga_ref, do_ref, dg_ref, dgn_ref):
        @pl.when(pl.program_id(0) == 0)
        def _():
            dgn_ref[...] = jnp.zeros_like(dgn_ref)

        _, vjp = jax.vjp(_l0_combine, hf[...], hb[...], ga[...], of[...], ob[...], g[...], gn[...])
        dhf, _, dga, dof, _, dg, dgn = vjp(_dg(dhb_ref[...], w_ref[...], 1, 1))
        dho_ref[...] = dhf
        dga_ref[...] = dga
        do_ref[...] = dof
        dg_ref[...] = dg
        dgn_ref[...] += dgn

    tok = lambda: pl.BlockSpec((tm, 512), lambda i: (i, 0))
    return pl.pallas_call(
        body, name=name, grid=(T // tm,),
        in_specs=[tok(), tok(), pl.BlockSpec((tm, 512), lambda i: (i, 1)), tok(), tok(),
                  pl.BlockSpec((tm, 512), lambda i: (i, 6)), pl.BlockSpec((1, 512), lambda i: (0, 0)),
                  pl.BlockSpec((tm, D_MODEL), lambda i: (i, 0)), pl.BlockSpec(w_out.shape, lambda i: (0, 0))],
        out_specs=[tok(), tok(), tok(), tok(), pl.BlockSpec((1, 512), lambda i: (0, 0))],
        out_shape=[jax.ShapeDtypeStruct((T, 512), F32)] * 4 + [jax.ShapeDtypeStruct((1, 512), F32)],
        compiler_params=_params("arbitrary"))(hs[0], hs[1], proj, o[0], o[1], proj, gain, dh_b, w_out)


def _l0_assemble(dxa, dga, dq, df, dv, dg, name):
    T = dxa.shape[0]
    tm = min(512, T)

    def body(xa, ga, q0, q1, f0, f1, v0, v1, g, out):
        both = lambda a, b: (a[...].astype(F32) + b[...].astype(F32)).astype(BF16)
        out[...] = jnp.concatenate([xa[...].astype(BF16), ga[...].astype(BF16), both(q0, q1), f0[...], f1[...],
                                    both(v0, v1), g[...].astype(BF16)], axis=1)

    tok = lambda: pl.BlockSpec((tm, 512), lambda i: (i, 0))
    return pl.pallas_call(
        body, name=name, grid=(T // tm,),
        in_specs=[tok() for _ in range(9)],
        out_specs=pl.BlockSpec((tm, AB_IN), lambda i: (i, 0)),
        out_shape=jax.ShapeDtypeStruct((T, AB_IN), BF16),
        compiler_params=_params("parallel"))(dxa, dga, dq[0], dq[1], df[0], df[1], dv[0], dv[1], dg)


def _l1_combine_fwd(o, proj, gain, name):
    T = proj.shape[0]
    tm = min(512, T)

    def body(of, ob, r, gn, out):
        out[...] = _l1_combine(of[...], ob[...], r[...], gn[...]).astype(BF16)

    tok = pl.BlockSpec((tm, 1024), lambda i: (i, 0))
    return pl.pallas_call(
        body, name=name, grid=(T // tm,),
        in_specs=[tok, tok, pl.BlockSpec((tm, 1024), lambda i: (i, 2)), pl.BlockSpec((1, 1024), lambda i: (0, 0))],
        out_specs=pl.BlockSpec((tm, 1024), lambda i: (i, 0)),
        out_shape=jax.ShapeDtypeStruct((T, 1024), BF16),
        compiler_params=_params("parallel"))(o[0], o[1], proj, gain)


def _l1_combine_bwd(o, proj, gain, dh_b, w_out, name):
    T = proj.shape[0]
    tm = min(512, T)

    def body(of, ob, r, gn, dhb_ref, w_ref, do_ref, dr_ref, dgn_ref):
        @pl.when(pl.program_id(0) == 0)
        def _():
            dgn_ref[...] = jnp.zeros_like(dgn_ref)

        _, vjp = jax.vjp(_l1_combine, of[...], ob[...], r[...], gn[...])
        dof, _, dr, dgn = vjp(_dg(dhb_ref[...], w_ref[...], 1, 1))
        do_ref[...] = dof
        dr_ref[...] = dr
        dgn_ref[...] += dgn

    tok = lambda: pl.BlockSpec((tm, 1024), lambda i: (i, 0))
    return pl.pallas_call(
        body, name=name, grid=(T // tm,),
        in_specs=[tok(), tok(), pl.BlockSpec((tm, 1024), lambda i: (i, 2)),
                  pl.BlockSpec((1, 1024), lambda i: (0, 0)), tok(), pl.BlockSpec(w_out.shape, lambda i: (0, 0))],
        out_specs=[tok(), tok(), pl.BlockSpec((1, 1024), lambda i: (0, 0))],
        out_shape=[jax.ShapeDtypeStruct((T, 1024), F32)] * 2 + [jax.ShapeDtypeStruct((1, 1024), F32)],
        compiler_params=_params("arbitrary"))(o[0], o[1], proj, gain, dh_b, w_out)


def _l1_assemble(dq, dk, dv, dr, dlr, name):
    T = dr.shape[0]
    tm = min(512, T)

    def body(q0, q1, k0, k1, v0, v1, r, a, out):
        both = lambda x, y: (x[...].astype(F32) + y[...].astype(F32)).astype(BF16)
        out[...] = jnp.concatenate([both(q0, q1), both(k0, k1), both(v0, v1), r[...].astype(BF16),
                                    a[...].astype(BF16)], axis=1)

    tok = lambda w: pl.BlockSpec((tm, w), lambda i: (i, 0))
    return pl.pallas_call(
        body, name=name, grid=(T // tm,),
        in_specs=[tok(512), tok(512), tok(512), tok(512), tok(1024), tok(1024), tok(1024), tok(LANES)],
        out_specs=pl.BlockSpec((tm, GLA_IN_PAD), lambda i: (i, 0)),
        out_shape=jax.ShapeDtypeStruct((T, GLA_IN_PAD), BF16),
        compiler_params=_params("parallel"))(dq[0], dq[1], dk[0], dk[1], dv[0], dv[1], dr, dlr)


HBM_SPEC = pl.BlockSpec(memory_space=pltpu.HBM)


def _place():
    x, y, c = lax.axis_index("x"), lax.axis_index("y"), lax.axis_index("c")
    return x, y, c


def _allgather_vmem(x_shard, name):
    m_per, n = x_shard.shape

    def body(x_ref, out_ref, send_sems, recv_sems, local_sem):
        x, y, c = _place()
        me, sibling = (x, y, c), (x, y, 1 - c)
        chips = [(1 - x, y), (x, 1 - y), (1 - x, 1 - y)]

        def rows(px, py, pc):
            return out_ref.at[pl.ds((4 * px + 2 * py + pc) * m_per, m_per), :]

        def copy(k, block, to, src=None):
            return pltpu.make_async_remote_copy(
                src_ref=rows(*block) if src is None else src, dst_ref=rows(*block),
                send_sem=send_sems.at[k], recv_sem=recv_sems.at[k], device_id=to, device_id_type=MESH)

        mine = pltpu.make_async_copy(x_ref, rows(*me), local_sem)
        mine.start()
        first = [copy(0, me, sibling, src=x_ref)]
        first += [copy(1 + j, me, (*chip, c), src=x_ref) for j, chip in enumerate(chips)]
        for cp in first:
            cp.start()
        passed = [copy(4 + j, (*chip, c), sibling) for j, chip in enumerate(chips)]
        for j, chip in enumerate(chips):
            copy(1 + j, (*chip, c), me).wait_recv()
            passed[j].start()
        copy(0, sibling, me).wait_recv()
        for j, chip in enumerate(chips):
            copy(4 + j, (*chip, 1 - c), me).wait_recv()
        for cp in first + passed:
            cp.wait_send()
        mine.wait()

    vm = pl.BlockSpec(memory_space=pltpu.VMEM)
    return pl.pallas_call(
        body, name=name, in_specs=[vm], out_specs=vm,
        out_shape=jax.ShapeDtypeStruct((N_DEV * m_per, n), x_shard.dtype),
        scratch_shapes=[pltpu.SemaphoreType.DMA((7,)), pltpu.SemaphoreType.DMA((7,)), pltpu.SemaphoreType.DMA],
        compiler_params=pltpu.CompilerParams(has_side_effects=True, vmem_limit_bytes=VMEM_LIMIT))(x_shard)


SEM_SPEC = pl.BlockSpec(memory_space=pltpu.SEMAPHORE)
DATAFLOW_EFFECT = pltpu.SideEffectType.DATAFLOW_SIDE_EFFECTING


def _copies(plan, srcs, lands, send_sems, recv_sems):
    x, y, c = _place()
    return [pltpu.make_async_remote_copy(src_ref=s, dst_ref=d, send_sem=send_sems.at[k], recv_sem=recv_sems.at[k],
                                         device_id=dev, device_id_type=MESH)
            for k, (s, d, dev) in enumerate(plan(srcs, lands, x, y, c))]


def _copies_start(plan, n_copies, srcs, lands, name):
    ns, nl = len(srcs), len(lands)

    def body(*refs):
        send_sems, recv_sems = refs[ns + nl], refs[ns + nl + 1]
        for cp in _copies(plan, refs[:ns], refs[ns:ns + nl], send_sems, recv_sems):
            cp.start()
        refs[-1][...] = jnp.zeros_like(refs[-1])

    arrays = list(srcs) + list(lands)
    res = pl.pallas_call(
        body, name=name,
        in_specs=[HBM_SPEC] * (ns + nl),
        out_specs=tuple([SEM_SPEC, SEM_SPEC] + [HBM_SPEC] * (ns + nl) + [pl.BlockSpec(memory_space=pltpu.VMEM)]),
        out_shape=tuple([pltpu.SemaphoreType.DMA((n_copies,)), pltpu.SemaphoreType.DMA((n_copies,))]
                        + [pltpu.HBM(a.shape, a.dtype) for a in arrays]
                        + [jax.ShapeDtypeStruct((SUBLANES, LANES), F32)]),
        input_output_aliases={i: 2 + i for i in range(ns + nl)},
        compiler_params=pltpu.CompilerParams(has_side_effects=DATAFLOW_EFFECT),
    )(*[pltpu.with_memory_space_constraint(a, pltpu.HBM) for a in arrays])
    return res[0], res[1], list(res[2:2 + ns]), list(res[2 + ns:2 + ns + nl]), res[-1]


def _copies_wait(plan, started, after, name):
    send_sems, recv_sems, srcs, lands, _ = started
    ns, nl = len(srcs), len(lands)

    def body(*refs):
        for cp in _copies(plan, refs[:ns], refs[ns:ns + nl], refs[ns + nl], refs[ns + nl + 1]):
            cp.wait_send()
            cp.wait_recv()

    arrays = list(srcs) + list(lands)
    res = pl.pallas_call(
        body, name=name,
        in_specs=[HBM_SPEC] * (ns + nl) + [SEM_SPEC, SEM_SPEC, pl.BlockSpec(memory_space=pl.ANY)],
        out_specs=tuple([HBM_SPEC] * (ns + nl)),
        out_shape=tuple(pltpu.HBM(a.shape, a.dtype) for a in arrays),
        input_output_aliases={i: i for i in range(ns + nl)},
        compiler_params=pltpu.CompilerParams(has_side_effects=DATAFLOW_EFFECT),
    )(*arrays, send_sems, recv_sems, after)
    return list(res[:ns]), list(res[ns:])


def _after(token, value):
    return value + token[0:1, 0:1].astype(value.dtype)


def _chips(x, y):
    return [(1 - x, y), (x, 1 - y), (1 - x, 1 - y)]


def _plan_gather_first(srcs, lands, x, y, c):
    me = 4 * x + 2 * y + c
    out = []
    for s, l in zip(srcs, lands):
        out.append((s, l.at[me], (x, y, 1 - c)))
        out += [(s, l.at[me], (*chip, c)) for chip in _chips(x, y)]
    return out


def _plan_gather_pass(srcs, lands, x, y, c):
    out = []
    for l in lands:
        for chip in _chips(x, y):
            slot = l.at[4 * chip[0] + 2 * chip[1] + c]
            out.append((slot, slot, (x, y, 1 - c)))
    return out


def _plan_grads_sibling(srcs, lands, x, y, c):
    return [(s.at[2 * q + (1 - c)], l.at[q], (x, y, 1 - c)) for s, l in zip(srcs, lands) for q in range(4)]


def _plan_grads_chips(srcs, lands, x, y, c):
    return [(s.at[2 * chip[0] + chip[1]], l.at[k], (*chip, c))
            for s, l in zip(srcs, lands) for k, chip in enumerate(_chips(x, y))]


def _landing(n_slots, like):
    return [lax.empty((n_slots,) + a.shape[1:], a.dtype) for a in like]


def _sum_slots(g, name):
    _, R, C = g.shape
    tr = min(256, R)
    assert R % tr == 0

    def body(g_ref, o_ref):
        acc = g_ref[0]
        for j in range(1, N_DEV):
            acc = acc + g_ref[j]
        o_ref[...] = acc

    return pl.pallas_call(
        body, name=name, grid=(R // tr,),
        in_specs=[pl.BlockSpec((N_DEV, tr, C), lambda i: (0, i, 0))],
        out_specs=pl.BlockSpec((tr, C), lambda i: (i, 0)),
        out_shape=jax.ShapeDtypeStruct((R, C), F32),
        compiler_params=_params("parallel"))(g)


def _chip_partial(g, r1, place, name):
    _, R, C = g.shape
    tr = min(1024, R)
    assert R % tr == 0

    def body(pl_ref, g_ref, r_ref, pb_ref, pm_ref):
        q = pl.program_id(1)
        s = g_ref[...] + r_ref[...]
        pb_ref[...] = s.astype(BF16)

        @pl.when(q == pl_ref[1])
        def _():
            pm_ref[...] = s

    grid_spec = pltpu.PrefetchScalarGridSpec(
        num_scalar_prefetch=1, grid=(R // tr, 4),
        in_specs=[pl.BlockSpec((None, tr, C), lambda r, q, p: (2 * q + p[0], r, 0)),
                  pl.BlockSpec((None, tr, C), lambda r, q, p: (q, r, 0))],
        out_specs=[pl.BlockSpec((None, tr, C), lambda r, q, p: (q, r, 0)),
                   pl.BlockSpec((tr, C), lambda r, q, p: (r, 0))])
    return pl.pallas_call(
        body, name=name, grid_spec=grid_spec,
        out_shape=[jax.ShapeDtypeStruct((4, R, C), BF16), jax.ShapeDtypeStruct((R, C), F32)],
        compiler_params=_params("parallel", "arbitrary"))(place, g, r1)


def _adamw_update(w, g, m, v, grad_ref, delta_ref, m_ref, v_ref):
    mn = ADAM_B1 * m + (1.0 - ADAM_B1) * g
    vn = ADAM_B2 * v + (1.0 - ADAM_B2) * jnp.square(g)
    m_hat = mn / (1.0 - ADAM_B1 ** ADAM_STEP)
    v_hat = vn / (1.0 - ADAM_B2 ** ADAM_STEP)
    grad_ref[...] = g
    delta_ref[...] = -ADAM_LR * (m_hat / (jnp.sqrt(v_hat) + ADAM_EPS) + ADAM_WD * w)
    m_ref[...] = mn
    v_ref[...] = vn


def _adamw_whole(w, g, m, v, name):
    def body(w_ref, g_ref, m_ref, v_ref, go, do, mo, vo):
        _adamw_update(w_ref[...], g_ref[...], m_ref[...], v_ref[...], go, do, mo, vo)

    vm = pl.BlockSpec(memory_space=pltpu.VMEM)
    return pl.pallas_call(body, name=name, in_specs=[vm] * 4, out_specs=[vm] * 4,
                          out_shape=[jax.ShapeDtypeStruct(w.shape, F32)] * 4)(w, g, m, v)


def _adamw(w, gparts, m, v, name):
    _, R, C = w.shape
    tr = min(256, R)
    assert R % tr == 0

    def body(w_ref, g0_ref, g3_ref, m_ref, v_ref, go, do, mo, vo):
        g = g0_ref[...]
        for k in range(3):
            g = g + g3_ref[k].astype(F32)
        _adamw_update(w_ref[...], g, m_ref[...], v_ref[...], go, do, mo, vo)

    blk = pl.BlockSpec((tr, C), lambda i: (i, 0))
    wblk = pl.BlockSpec((None, tr, C), lambda i: (0, i, 0))
    return pl.pallas_call(
        body, name=name, grid=(R // tr,),
        in_specs=[wblk, blk, pl.BlockSpec((3, tr, C), lambda i: (0, i, 0)), wblk, wblk], out_specs=[wblk] * 4,
        out_shape=[jax.ShapeDtypeStruct(w.shape, F32)] * 4,
        compiler_params=_params("parallel"))(w, gparts[0], gparts[1], m, v)


def _adamw_layers(w, parts, m, v, name):
    _, R, C = w.shape
    tr = min(256, R)
    assert R % tr == 0

    def body(w_ref, p0, r0, p1, r1, m_ref, v_ref, go, do, mo, vo):
        gs = []
        for p, r in ((p0, r0), (p1, r1)):
            g = p[...]
            for k in range(3):
                g = g + r[k].astype(F32)
            gs.append(g)
        g = jnp.where(pl.program_id(0) == 0, gs[0], gs[1])
        _adamw_update(w_ref[...], g, m_ref[...], v_ref[...], go, do, mo, vo)

    lay = pl.BlockSpec((None, tr, C), lambda l, i: (l, i, 0))
    one = pl.BlockSpec((tr, C), lambda l, i: (i, 0))
    three = pl.BlockSpec((3, tr, C), lambda l, i: (0, i, 0))
    return pl.pallas_call(
        body, name=name, grid=(2, R // tr), in_specs=[lay, one, three, one, three, lay, lay],
        out_specs=[lay] * 4, out_shape=[jax.ShapeDtypeStruct((2, R, C), F32)] * 4,
        compiler_params=_params("parallel", "parallel"))(w, parts[0][0], parts[0][1], parts[1][0], parts[1][1], m, v)


SMALL_SHARDED = ("rg_conv_w", "rg_b_a", "rg_b_x", "rg_lambda", "gla_w_gate_up", "gla_b_gate", "gla_norm")
SMALL_REPLICATED = ("norm_mix", "norm_mlp", "norm_final", "rg_conv_b", "rg_w_a", "rg_w_x", "hg_lb_logits", "hg_norm")
WEIGHT_NAMES = ("norm_mix", "norm_mlp", "norm_final", "mlp_w1", "mlp_w2", "ab_w_in", "ab_w_out", "rg_conv_w",
                "rg_conv_b", "rg_w_a", "rg_b_a", "rg_w_x", "rg_b_x", "rg_lambda", "hg_lb_logits", "hg_norm",
                "gla_w_in", "gla_w_out", "gla_w_gate_up", "gla_b_gate", "gla_norm")


def _rows128(a):
    return a.reshape(-1, LANES)


def _part_rows(a):
    return -(-(a.size // LANES) // SUBLANES) * SUBLANES


def _pack_rows(arrays, pad_to=SUBLANES):
    parts = [jnp.pad(_rows128(a), ((0, _part_rows(a) - a.size // LANES), (0, 0))) for a in arrays]
    total = sum(p.shape[0] for p in parts)
    extra = (-total) % pad_to
    if extra:
        parts.append(jnp.zeros((extra, LANES), parts[0].dtype))
    return jnp.concatenate(parts, axis=0)


def _unshard_last(g, shape_local):
    nd = len(shape_local)
    t = g.reshape((N_DEV,) + tuple(shape_local))
    t = jnp.moveaxis(t, 0, nd - 1)
    return t.reshape(tuple(shape_local[:-1]) + (N_DEV * shape_local[-1],))


def _block_diag(w):
    eye = jnp.eye(8, dtype=w.dtype)
    return (w[:, :, :, None, :] * eye[None, :, None, :, None]).reshape(2, RG_W, RG_W)


def _block_diag_extract(dw):
    t = dw.reshape(2, 8, 64, 8, 64)
    return jnp.moveaxis(jnp.diagonal(t, axis1=1, axis2=3), -1, 1)


def kernel(x, norm_mix, norm_mlp, norm_final, mlp_w1, mlp_w2, ab_w_in, ab_w_out, rg_conv_w, rg_conv_b, rg_w_a, rg_b_a, rg_w_x, rg_b_x, rg_lambda, hg_lb_logits, hg_norm, gla_w_in, gla_w_out, gla_w_gate_up, gla_b_gate, gla_norm, loss_target, m_norm_mix, m_norm_mlp, m_norm_final, m_mlp_w1, m_mlp_w2, m_ab_w_in, m_ab_w_out, m_rg_conv_w, m_rg_conv_b, m_rg_w_a, m_rg_b_a, m_rg_w_x, m_rg_b_x, m_rg_lambda, m_hg_lb_logits, m_hg_norm, m_gla_w_in, m_gla_w_out, m_gla_w_gate_up, m_gla_b_gate, m_gla_norm, v_norm_mix, v_norm_mlp, v_norm_final, v_mlp_w1, v_mlp_w2, v_ab_w_in, v_ab_w_out, v_rg_conv_w, v_rg_conv_b, v_rg_w_a, v_rg_b_a, v_rg_w_x, v_rg_b_x, v_rg_lambda, v_hg_lb_logits, v_hg_norm, v_gla_w_in, v_gla_w_out, v_gla_w_gate_up, v_gla_b_gate, v_gla_norm):
    w_loc = dict(norm_mix=norm_mix, norm_mlp=norm_mlp, norm_final=norm_final, mlp_w1=mlp_w1, mlp_w2=mlp_w2,
                 ab_w_in=ab_w_in, ab_w_out=ab_w_out, rg_conv_w=rg_conv_w, rg_conv_b=rg_conv_b, rg_w_a=rg_w_a,
                 rg_b_a=rg_b_a, rg_w_x=rg_w_x, rg_b_x=rg_b_x, rg_lambda=rg_lambda, hg_lb_logits=hg_lb_logits,
                 hg_norm=hg_norm, gla_w_in=gla_w_in, gla_w_out=gla_w_out, gla_w_gate_up=gla_w_gate_up,
                 gla_b_gate=gla_b_gate, gla_norm=gla_norm)
    m_loc = dict(norm_mix=m_norm_mix, norm_mlp=m_norm_mlp, norm_final=m_norm_final, mlp_w1=m_mlp_w1,
                 mlp_w2=m_mlp_w2, ab_w_in=m_ab_w_in, ab_w_out=m_ab_w_out, rg_conv_w=m_rg_conv_w,
                 rg_conv_b=m_rg_conv_b, rg_w_a=m_rg_w_a, rg_b_a=m_rg_b_a, rg_w_x=m_rg_w_x, rg_b_x=m_rg_b_x,
                 rg_lambda=m_rg_lambda, hg_lb_logits=m_hg_lb_logits, hg_norm=m_hg_norm, gla_w_in=m_gla_w_in,
                 gla_w_out=m_gla_w_out, gla_w_gate_up=m_gla_w_gate_up, gla_b_gate=m_gla_b_gate,
                 gla_norm=m_gla_norm)
    v_loc = dict(norm_mix=v_norm_mix, norm_mlp=v_norm_mlp, norm_final=v_norm_final, mlp_w1=v_mlp_w1,
                 mlp_w2=v_mlp_w2, ab_w_in=v_ab_w_in, ab_w_out=v_ab_w_out, rg_conv_w=v_rg_conv_w,
                 rg_conv_b=v_rg_conv_b, rg_w_a=v_rg_w_a, rg_b_a=v_rg_b_a, rg_w_x=v_rg_w_x, rg_b_x=v_rg_b_x,
                 rg_lambda=v_rg_lambda, hg_lb_logits=v_hg_lb_logits, hg_norm=v_hg_norm, gla_w_in=v_gla_w_in,
                 gla_w_out=v_gla_w_out, gla_w_gate_up=v_gla_w_gate_up, gla_b_gate=v_gla_b_gate,
                 gla_norm=v_gla_norm)

    T = x.shape[1]
    h0 = x.reshape(T, D_MODEL)
    target = loss_target.reshape(T, D_MODEL)
    ax, ay, ac = lax.axis_index("x"), lax.axis_index("y"), lax.axis_index("c")
    dev = 4 * ax + 2 * ay + ac
    place = jnp.stack([ac, 2 * ax + ay]).astype(jnp.int32)

    abin_shard = ab_w_in[0].astype(BF16)
    first_started = _copies_start(_plan_gather_first, 4, [abin_shard], _landing(N_DEV, [abin_shard[None]]),
                                  "ag_first_start")
    rest_shards = [mlp_w1[0].astype(BF16), mlp_w2[0].astype(BF16), gla_w_in[0].astype(BF16),
                   gla_w_out[0].astype(BF16), mlp_w1[1].astype(BF16), mlp_w2[1].astype(BF16),
                   _after(first_started[4], ab_w_out[0].astype(BF16))]
    ag_started = _copies_start(_plan_gather_first, 4 * len(rest_shards), rest_shards,
                               _landing(N_DEV, [s[None] for s in rest_shards]), "ag_rest_start")

    small_local = [w_loc[n] for n in SMALL_SHARDED]
    small_g = _allgather_vmem(_pack_rows(small_local, 8), "ag_small")
    small_g = small_g.reshape(N_DEV, -1, LANES)
    full = {}
    off = 0
    for n, a in zip(SMALL_SHARDED, small_local):
        full[n] = _unshard_last(small_g[:, off:off + a.size // LANES].reshape(N_DEV, a.size), a.shape)
        off += _part_rows(a)
    conv_w = full["rg_conv_w"][0]
    b_a, b_x, lam = full["rg_b_a"][0], full["rg_b_x"][0], full["rg_lambda"][0]
    w_up, b_gate, g_norm = full["gla_w_gate_up"][0], full["gla_b_gate"][0], full["gla_norm"]

    cw8 = jnp.pad(conv_w, ((0, 4), (0, 0)))
    wbd = jnp.concatenate([_block_diag(rg_w_a[0]), _block_diag(rg_w_x[0])], axis=2).astype(BF16)
    rg_bias = jnp.concatenate([b_a, b_x], axis=1).reshape(2, 1, 2 * RG_W)
    lam3 = lam.reshape(2, 1, RG_W)
    l0, l1 = hg_lb_logits[0:1], hg_lb_logits[1:2]
    wup_pad = jnp.zeros((2, LANES, 512), F32).at[0, 0:16].set(w_up[0]).at[1, 16:32].set(w_up[1])
    bg3 = b_gate.reshape(2, 1, 512)
    nmix0, nmix1 = norm_mix[0:1], norm_mix[1:2]
    nmlp0, nmlp1 = norm_mlp[0:1], norm_mlp[1:2]
    nfin = norm_final.reshape(1, D_MODEL)

    prepared = (ag_started[4] + cw8[:, 0:LANES] + wup_pad[0, 0:SUBLANES, 0:LANES] + rg_bias[0, :, 0:LANES]
                + wbd[0, 0:SUBLANES, 0:LANES].astype(F32) + lam3[0, :, 0:LANES] + bg3[0, :, 0:LANES])
    (abin_shard,), abin_l = _copies_wait(_plan_gather_first, first_started, prepared, "ag_first_wait")
    first_pass = _copies_start(_plan_gather_pass, 3, [], abin_l, "ag_first_pass_start")
    _, (abin_g,) = _copies_wait(_plan_gather_pass, first_pass, first_pass[4], "ag_first_pass_wait")
    abin_g = lax.dynamic_update_index_in_dim(abin_g, abin_shard, dev, 0)
    wab_in = jnp.transpose(abin_g, (1, 0, 2)).reshape(D_MODEL, AB_IN)
    proj0, y0 = _norm_matmul(h0, _after(ag_started[4], nmix0), wab_in, "l0_in_proj")
    xc = _rg_conv_fwd(proj0, cw8, rg_conv_b, "rg_conv")
    hs = _rg_scan_fwd(xc, wbd, rg_bias, lam3, "rg_scan")
    o_hg, s_hg = _hg_fwd(proj0, l0, l1, "hg_chunks")
    both_done = hs[0][0:SUBLANES, 0:LANES] + o_hg[0][0:SUBLANES, 0:LANES]
    rest_shards, rest_lands = _copies_wait(_plan_gather_first, ag_started, both_done, "ag_rest_wait")
    pass_started = _copies_start(_plan_gather_pass, 3 * len(rest_lands), [], rest_lands, "ag_pass_start")
    mixin0 = _l0_combine_fwd(hs, proj0, o_hg, _after(pass_started[4], hg_norm), "l0_combine")
    _, rest_g = _copies_wait(_plan_gather_pass, pass_started, mixin0, "ag_pass_wait")
    rest_g = [lax.dynamic_update_index_in_dim(g, s, dev, 0) for g, s in zip(rest_g, rest_shards)]
    wab_out = rest_g[6].reshape(D_MODEL, D_MODEL)
    h1 = _matmul_res(mixin0, wab_out, h0, "l0_out_proj")
    w1g = (rest_g[0], rest_g[4])
    w2f = (rest_g[1].reshape(D_FF, D_MODEL), rest_g[5].reshape(D_FF, D_MODEL))
    wgla_in = jnp.pad(jnp.transpose(rest_g[2], (1, 0, 2)).reshape(D_MODEL, GLA_IN),
                      ((0, 0), (0, GLA_IN_PAD - GLA_IN)))
    wgla_out = rest_g[3].reshape(D_MODEL, D_MODEL)
    h2, pre0, ym0 = _mlp_fwd(h1, nmlp0, w1g[0], w2f[0], "mlp0")
    proj1, y1 = _norm_matmul(h2, nmix1, wgla_in, "l1_in_proj")
    z_gate, lr_b = _gate_logits(proj1, wup_pad, bg3, "gla_gate_logits")
    o_gla, s_gla = _gla_fwd(proj1, z_gate, "gla_chunks")
    mixin1 = _l1_combine_fwd(o_gla, proj1, g_norm, "l1_combine")
    h3 = _matmul_res(mixin1, wgla_out, h2, "l1_out_proj")
    h4, pre1, ym1 = _mlp_fwd(h3, nmlp1, w1g[1], w2f[1], "mlp1")
    loss_blk, dh4, dh4b, d_nfin = _final_loss(h4, nfin, target, "final_loss")

    dh3, dh3b, dpre1, act1, d_nmlp1 = _mlp_bwd(dh4, dh4b, h3, nmlp1, pre1, w1g[1], w2f[1], "mlp1_bwd")
    g_w1_1 = _wgrad(ym1, dpre1, 512, "mlp1_dw1", sharded_cols=True)
    g_w2_1 = _wgrad(act1, dh4b, 512, "mlp1_dw2")
    g_gla_out = _wgrad(mixin1, dh3b, 512, "l1_out_dw")
    do_gla, dr, d_gnorm = _l1_combine_bwd(o_gla, proj1, g_norm, dh3b, wgla_out, "l1_combine_bwd")
    dq1, dk1, dv1, dz_gate = _gla_bwd(proj1, z_gate, s_gla, do_gla, "gla_chunks_bwd")
    dlr1, d_bg, dz_b = _gate_logits_bwd(dz_gate, wup_pad, "gla_gate_logits_bwd")
    d_wup = [_wgrad(lr_b, dz_b[d], 512, "gla_gate_dw%d" % d) for d in range(2)]
    dproj1 = _l1_assemble(dq1, dk1, dv1, dr, dlr1, "l1_assemble")
    dh2, dh2b, d_nmix1 = _dgrad_norm(dproj1, wgla_in, h2, nmix1, dh3, "l1_in_dgrad")
    g_gla_in = _wgrad(y1, dproj1, 640, "l1_in_dw")

    def reduce_start(grads, tag):
        return _copies_start(_plan_grads_sibling, 4 * len(grads), grads, _landing(4, grads), "rs_%s_d2d_start" % tag)

    def reduce_mid(started, after, tag):
        grads, got = _copies_wait(_plan_grads_sibling, started, after, "rs_%s_d2d_wait" % tag)
        parts = [_chip_partial(g, r, place, "rs_%s_partial%d" % (tag, a)) for a, (g, r) in enumerate(zip(grads, got))]
        pb = [p[0] for p in parts]
        return _copies_start(_plan_grads_chips, 3 * len(pb), pb, _landing(3, pb), "rs_%s_ici_start" % tag), \
            [p[1] for p in parts]

    def reduce_end(started, mine, after, tag):
        _, got = _copies_wait(_plan_grads_chips, started, after, "rs_%s_ici_wait" % tag)
        return list(zip(mine, got))

    slots_l1 = [g_w1_1, g_w2_1.reshape(N_DEV, 512, D_MODEL),
                jnp.transpose(g_gla_in[:, :GLA_IN].reshape(D_MODEL, N_DEV, GLA_IN // N_DEV), (1, 0, 2)),
                g_gla_out.reshape(N_DEV, 128, D_MODEL)]
    ra_d2d = reduce_start(slots_l1, "l1")

    dh1, dh1b, dpre0, act0, d_nmlp0 = _mlp_bwd(dh2, dh2b, h1, _after(ra_d2d[4], nmlp0), pre0, w1g[0], w2f[0],
                                               "mlp0_bwd")
    g_w1_0 = _wgrad(ym0, dpre0, 512, "mlp0_dw1", sharded_cols=True)
    g_w2_0 = _wgrad(act0, dh2b, 512, "mlp0_dw2")
    ra_ici, ra_mine = reduce_mid(ra_d2d, g_w2_0, "l1")
    g_ab_out = _wgrad(mixin0, dh1b, 512, "l0_out_dw")
    rb_d2d = reduce_start([g_w1_0, g_w2_0.reshape(N_DEV, 512, D_MODEL), g_ab_out.reshape(N_DEV, 128, D_MODEL)],
                          "mlp0")
    dho, dga, do_hg, dg_gate, d_hgnorm = _l0_combine_bwd(
        hs, proj0, o_hg, _after(rb_d2d[4], _after(ra_ici[4], hg_norm)), dh1b, wab_out, "l0_combine_bwd")
    dxc, d_wbd, d_rgb, d_lam = _rg_scan_bwd(xc, wbd, rg_bias, lam3, hs, dho, "rg_scan_bwd")
    dxa, d_cw8, d_cb = _rg_conv_bwd(dxc, proj0, cw8, "rg_conv_bwd")
    dq0, df0, dv0, d_l0, d_l1 = _hg_bwd(proj0, l0, l1, s_hg, do_hg, "hg_chunks_bwd")
    rb_ici, rb_mine = reduce_mid(rb_d2d, d_l0, "mlp0")
    dproj0 = _l0_assemble(dxa, dga, dq0, df0, dv0, dg_gate, "l0_assemble")
    dx, _, d_nmix0 = _dgrad_norm(dproj0, wab_in, h0, _after(rb_ici[4], nmix0), dh1, "l0_in_dgrad")

    d_wa = _block_diag_extract(d_wbd[:, :, :RG_W])[None]
    d_wx = _block_diag_extract(d_wbd[:, :, RG_W:])[None]
    small_full = {
        "norm_mix": jnp.concatenate([d_nmix0, d_nmix1], axis=0), "norm_mlp": jnp.concatenate([d_nmlp0, d_nmlp1], axis=0),
        "norm_final": d_nfin.reshape(D_MODEL), "rg_conv_b": d_cb, "rg_w_a": d_wa, "rg_w_x": d_wx,
        "hg_lb_logits": jnp.concatenate([d_l0[0] + d_l0[1], d_l1[0] + d_l1[1]], axis=0), "hg_norm": d_hgnorm,
        "rg_conv_w": d_cw8[0:4][None], "rg_b_a": d_rgb[:, 0, :RG_W][None], "rg_b_x": d_rgb[:, 0, RG_W:][None],
        "rg_lambda": d_lam[:, 0, :][None],
        "gla_w_gate_up": jnp.stack([d_wup[0][0:16], d_wup[1][16:32]])[None], "gla_b_gate": d_bg[:, 0, :][None],
        "gla_norm": d_gnorm}
    small_names = SMALL_REPLICATED + SMALL_SHARDED
    packed = _pack_rows([loss_blk] + [small_full[n] for n in small_names], 256)
    ar_first = _copies_start(_plan_gather_first, 4, [packed], _landing(N_DEV, [packed[None]]), "ar_small_start")

    g_ab_in = _wgrad(y0, dproj0, 512, "l0_in_dw", behind=ar_first[4])
    rc_d2d = reduce_start([jnp.transpose(g_ab_in.reshape(D_MODEL, N_DEV, AB_IN // N_DEV), (1, 0, 2))], "ab")
    (packed,), ar_lands = _copies_wait(_plan_gather_first, ar_first, rc_d2d[4], "ar_small_wait")
    ar_pass = _copies_start(_plan_gather_pass, 3, [], ar_lands, "ar_small_pass_start")
    rc_ici, rc_mine = reduce_mid(rc_d2d, ar_pass[4], "ab")
    _, (ar_gathered,) = _copies_wait(_plan_gather_pass, ar_pass, rc_ici[4], "ar_small_pass_wait")
    summed = _sum_slots(lax.dynamic_update_index_in_dim(ar_gathered, packed, dev, 0), "ar_small_sum")
    loss = summed[0, 0]

    pieces_l1 = reduce_end(ra_ici, ra_mine, rc_ici[4], "l1")
    res_gla_in = _adamw(gla_w_in, pieces_l1[2], m_gla_w_in, v_gla_w_in, "adamw_gla_in")
    res_gla_out = _adamw(gla_w_out, pieces_l1[3], m_gla_w_out, v_gla_w_out, "adamw_gla_out")
    pieces_mlp0 = reduce_end(rb_ici, rb_mine, res_gla_out[0], "mlp0")
    res_w1 = _adamw_layers(mlp_w1, (pieces_mlp0[0], pieces_l1[0]), m_mlp_w1, v_mlp_w1, "adamw_mlp_w1")
    res_w2 = _adamw_layers(mlp_w2, (pieces_mlp0[1], pieces_l1[1]), m_mlp_w2, v_mlp_w2, "adamw_mlp_w2")
    res = {"mlp_w1": tuple(res_w1), "mlp_w2": tuple(res_w2),
           "gla_w_in": tuple(res_gla_in), "gla_w_out": tuple(res_gla_out),
           "ab_w_out": tuple(_adamw(ab_w_out, pieces_mlp0[2], m_ab_w_out, v_ab_w_out, "adamw_ab_out"))}

    off = SUBLANES
    for n in small_names:
        a = small_full[n]
        gfull = summed[off:off + a.size // LANES].reshape(a.shape)
        off += _part_rows(a)
        local = w_loc[n].shape
        if n in SMALL_SHARDED:
            gfull = lax.dynamic_slice_in_dim(gfull, dev * local[-1], local[-1], axis=gfull.ndim - 1)
        flat = (-1, local[-1])
        outs = _adamw_whole(w_loc[n].reshape(flat), gfull.reshape(flat), m_loc[n].reshape(flat),
                            v_loc[n].reshape(flat), "adamw_" + n)
        res[n] = tuple(o.reshape(local) for o in outs)
    others_done = (res_w1[1][0, 0:SUBLANES, 0:LANES] + res_w2[1][0, 0:SUBLANES, 0:LANES]
                   + res_gla_in[1][0, 0:SUBLANES, 0:LANES])
    pieces_ab = reduce_end(rc_ici, rc_mine, others_done, "ab")
    res["ab_w_in"] = tuple(_adamw(ab_w_in, pieces_ab[0], m_ab_w_in, v_ab_w_in, "adamw_ab_in"))

    grad_x = dx.reshape(1, T, D_MODEL)
    out = [loss, grad_x]
    for k in range(4):
        out += [res[n][k] for n in WEIGHT_NAMES]
    return tuple(out)
```

```python
import jax
import jax.numpy as jnp
from jax import lax
from jax.experimental import pallas as pl
from jax.experimental.pallas import tpu as pltpu

F32, BF16 = jnp.float32, jnp.bfloat16
MESH = pl.DeviceIdType.MESH

D_MODEL = 1024
D_FF = 4096
RG_W = 512
HG_W = 512
CHUNK = 64
EPS = 1e-6
RG_C = 8.0
AB_IN = 3584
GLA_IN = 3104
GLA_IN_PAD = 3200
N_DEV = 8
LANES = 128
SUBLANES = 8
VMEM_LIMIT = 48 * 1024 * 1024

ADAM_LR, ADAM_B1, ADAM_B2, ADAM_EPS, ADAM_WD, ADAM_STEP = 0.001, 0.9, 0.999, 1e-08, 0.01, 10


def _params(*sem):
    return pltpu.CompilerParams(dimension_semantics=sem, vmem_limit_bytes=VMEM_LIMIT)


def _dg(a, b, ca, cb):
    return lax.dot_general(a.astype(BF16), b.astype(BF16), (((ca,), (cb,)), ((), ())),
                           preferred_element_type=F32)


@jax.custom_vjp
def _mm_nn(a, b):
    return _dg(a, b, 1, 0)


_mm_nn.defvjp(lambda a, b: (_dg(a, b, 1, 0), (a, b)),
              lambda res, g: (_dg(g, res[1], 1, 1), _dg(res[0], g, 0, 0)))


@jax.custom_vjp
def _mm_nt(a, b):
    return _dg(a, b, 1, 1)


_mm_nt.defvjp(lambda a, b: (_dg(a, b, 1, 1), (a, b)),
              lambda res, g: (_dg(g, res[1], 1, 0), _dg(g, res[0], 0, 0)))


@jax.custom_vjp
def _mm_tn(a, b):
    return _dg(a, b, 0, 0)


_mm_tn.defvjp(lambda a, b: (_dg(a, b, 0, 0), (a, b)),
              lambda res, g: (_dg(res[1], g, 1, 1), _dg(res[0], g, 1, 0)))


def _tri_dot(tri, x):
    hi = x.astype(BF16)
    lo = (x - hi.astype(F32)).astype(BF16)
    t = tri.astype(BF16)
    return jnp.dot(t, hi, preferred_element_type=F32) + jnp.dot(t, lo, preferred_element_type=F32)


@jax.custom_vjp
def _cum(tri, tri_t, x):
    return _tri_dot(tri, x)


_cum.defvjp(lambda tri, tri_t, x: (_tri_dot(tri, x), (tri, tri_t)),
            lambda res, g: (jnp.zeros_like(res[0]), jnp.zeros_like(res[1]), _tri_dot(res[1], g)))


@jax.custom_vjp
def _sig(x):
    return 1.0 / (1.0 + jnp.exp(-x))


_sig.defvjp(lambda x: (lambda s: (s, s))(1.0 / (1.0 + jnp.exp(-x))),
            lambda s, g: (g * s * (1.0 - s),))


def _gelu(x):
    return 0.5 * x * (1.0 + jnp.tanh(0.7978845608028654 * (x + 0.044715 * (x * x * x))))


def _softplus(z):
    return jnp.maximum(z, 0.0) + jnp.log(1.0 + jnp.exp(-jnp.abs(z)))


def _rms(x):
    return lax.rsqrt(jnp.mean(x * x, axis=-1, keepdims=True) + EPS)


def _rmsnorm_bwd(x, gain, dy):
    r = _rms(x)
    xh = x * r
    dgain = jnp.sum(dy * xh, axis=0, keepdims=True)
    dxh = dy * gain
    dx = r * (dxh - xh * jnp.mean(dxh * xh, axis=-1, keepdims=True))
    return dx, dgain


def _headnorm(o, gain, n_heads, hd):
    parts = []
    for h in range(n_heads):
        oh = o[:, h * hd:(h + 1) * hd]
        parts.append(oh * _rms(oh))
    return jnp.concatenate(parts, axis=1) * gain


def _tri_consts(d):
    row = lax.broadcasted_iota(jnp.int32, (CHUNK, CHUNK), 0)
    col = lax.broadcasted_iota(jnp.int32, (CHUNK, CHUNK), 1)
    ge = (row >= col).astype(F32)
    le = (row <= col).astype(F32)
    r1 = lax.broadcasted_iota(jnp.int32, (CHUNK, 1), 0)
    if d == 0:
        return ge, le, (r1 <= CHUNK // 2).astype(F32)
    return le, ge, (r1 >= CHUNK // 2 - 1).astype(F32)


def _chunk_core(qh, k, v, logf, st_prev, tri, tri_t, mref, n_heads, dk, dv):
    cum = _cum(tri, tri_t, logf)
    ref = jnp.sum(logf * mref, axis=0, keepdims=True)
    last = jnp.sum(logf, axis=0, keepdims=True)
    q_in = qh * jnp.exp(cum - ref)
    k_in = k * jnp.exp(ref - cum)
    k_st = k_in * jnp.exp(last - ref)
    q_dec = q_in * jnp.exp(ref)
    decay = jnp.exp(last)
    outs, sts = [], []
    for h in range(n_heads):
        sk = slice(h * dk, (h + 1) * dk)
        sv = slice(h * dv, (h + 1) * dv)
        sc = _mm_nt(q_in[:, sk], k_in[:, sk]) * tri
        o = _mm_nn(sc, v[:, sv]) + _mm_nt(q_dec[:, sk], st_prev[h])
        sts.append(st_prev[h] * decay[:, sk] + _mm_tn(v[:, sv], k_st[:, sk]))
        outs.append(o)
    return jnp.concatenate(outs, axis=1), tuple(sts)


def _hg_chunk(q, f, v, l0, l1, st_prev, tri, tri_t, mref):
    lb = _sig(l0 - l1)
    sg = _sig(f)
    qh = q * _sig(q)
    logf = jnp.log(lb + (1.0 - lb) * sg)
    k = (1.0 - lb) * (1.0 - sg)
    return _chunk_core(qh, k, v, logf, st_prev, tri, tri_t, mref, 4, 128, 128)


def _gla_chunk(q, k, v, z, st_prev, tri, tri_t, mref):
    logf = (jnp.minimum(z, 0.0) - jnp.log(1.0 + jnp.exp(-jnp.abs(z)))) * (1.0 / 16.0)
    qh = q * (128.0 ** -0.5)
    return _chunk_core(qh, k, v, logf, st_prev, tri, tri_t, mref, 4, 128, 256)


def _rg_gates(xc, wbd, bias, lam):
    z = _mm_nn(xc, wbd) + bias
    r = _sig(z[:, :RG_W])
    i = _sig(z[:, RG_W:])
    log_a = -RG_C * r * _softplus(-lam)
    a = jnp.exp(log_a)
    x2 = 2.0 * log_a
    neg_expm1 = jnp.where(x2 > -1e-2, -(x2 + 0.5 * x2 * x2 + x2 * x2 * x2 * (1.0 / 6.0)), 1.0 - jnp.exp(x2))
    u = jnp.sqrt(neg_expm1) * (i * xc)
    return a, u


def _l0_combine(hf, hb, ga, of, ob, g, gain):
    ya = (hf + hb) * _gelu(ga)
    yb = _headnorm(of + ob, gain, 4, 128) * (g * _sig(g))
    return jnp.concatenate([ya, yb], axis=1)


def _l1_combine(of, ob, r, gain):
    return _headnorm(of + ob, gain, 4, 256) * (r * _sig(r))


def _norm_matmul(h, gain, w, name):
    T, D = h.shape
    N = w.shape[1]
    tm = min(512, T)

    def body(h_ref, g_ref, w_ref, o_ref, y_ref):
        x = h_ref[...]
        y = (x * _rms(x) * g_ref[...]).astype(BF16)
        y_ref[...] = y
        o_ref[...] = jnp.dot(y, w_ref[...], preferred_element_type=F32)

    return pl.pallas_call(
        body, name=name, grid=(T // tm,),
        in_specs=[pl.BlockSpec((tm, D), lambda i: (i, 0)), pl.BlockSpec((1, D), lambda i: (0, 0)),
                  pl.BlockSpec((D, N), lambda i: (0, 0))],
        out_specs=[pl.BlockSpec((tm, N), lambda i: (i, 0)), pl.BlockSpec((tm, D), lambda i: (i, 0))],
        out_shape=[jax.ShapeDtypeStruct((T, N), F32), jax.ShapeDtypeStruct((T, D), BF16)],
        compiler_params=_params("parallel"))(h, gain, w)


def _matmul_res(a, w, res, name):
    T, K = a.shape
    N = w.shape[1]
    tm = min(512, T)

    def body(a_ref, w_ref, r_ref, o_ref):
        o_ref[...] = r_ref[...] + jnp.dot(a_ref[...], w_ref[...], preferred_element_type=F32)

    return pl.pallas_call(
        body, name=name, grid=(T // tm,),
        in_specs=[pl.BlockSpec((tm, K), lambda i: (i, 0)), pl.BlockSpec((K, N), lambda i: (0, 0)),
                  pl.BlockSpec((tm, N), lambda i: (i, 0))],
        out_specs=pl.BlockSpec((tm, N), lambda i: (i, 0)),
        out_shape=jax.ShapeDtypeStruct((T, N), F32),
        compiler_params=_params("parallel"))(a, w, res)


def _dgrad_norm(dproj, w, h, gain, dres, name):
    T, N = dproj.shape
    D = w.shape[0]
    tm = min(512, T)

    def body(dp_ref, w_ref, h_ref, g_ref, dr_ref, dh_ref, dhb_ref, dg_ref):
        @pl.when(pl.program_id(0) == 0)
        def _():
            dg_ref[...] = jnp.zeros_like(dg_ref)

        dy = _dg(dp_ref[...], w_ref[...], 1, 1)
        dx, dgain = _rmsnorm_bwd(h_ref[...], g_ref[...], dy)
        dh = dr_ref[...] + dx
        dh_ref[...] = dh
        dhb_ref[...] = dh.astype(BF16)
        dg_ref[...] += dgain

    return pl.pallas_call(
        body, name=name, grid=(T // tm,),
        in_specs=[pl.BlockSpec((tm, N), lambda i: (i, 0)), pl.BlockSpec((D, N), lambda i: (0, 0)),
                  pl.BlockSpec((tm, D), lambda i: (i, 0)), pl.BlockSpec((1, D), lambda i: (0, 0)),
                  pl.BlockSpec((tm, D), lambda i: (i, 0))],
        out_specs=[pl.BlockSpec((tm, D), lambda i: (i, 0)), pl.BlockSpec((tm, D), lambda i: (i, 0)),
                   pl.BlockSpec((1, D), lambda i: (0, 0))],
        out_shape=[jax.ShapeDtypeStruct((T, D), F32), jax.ShapeDtypeStruct((T, D), BF16),
                   jax.ShapeDtypeStruct((1, D), F32)],
        compiler_params=_params("arbitrary"))(dproj, w, h, gain, dres)


def _wgrad(a, b, tn, name, sharded_cols=False, behind=None):
    T, K = a.shape
    N = b.shape[1]
    tk = min(1024, K)

    def body(a_ref, b_ref, *rest):
        rest[-1][...] = _dg(a_ref[...], b_ref[...], 0, 0)

    if sharded_cols:
        out_spec = pl.BlockSpec((None, tk, tn), lambda k, n: (n, k, 0))
        out_shape = jax.ShapeDtypeStruct((N // tn, K, tn), F32)
    else:
        out_spec = pl.BlockSpec((tk, tn), lambda k, n: (k, n))
        out_shape = jax.ShapeDtypeStruct((K, N), F32)
    in_specs = [pl.BlockSpec((T, tk), lambda k, n: (0, k)), pl.BlockSpec((T, tn), lambda k, n: (0, n))]
    args = [a, b]
    if behind is not None:
        in_specs.append(pl.BlockSpec((SUBLANES, LANES), lambda k, n: (0, 0)))
        args.append(behind)
    return pl.pallas_call(
        body, name=name, grid=(K // tk, N // tn), in_specs=in_specs, out_specs=out_spec, out_shape=out_shape,
        compiler_params=_params("parallel", "parallel"))(*args)


def _resident(shape):
    return pl.BlockSpec(shape, lambda i: (0,) * len(shape), pipeline_mode=pl.Buffered(1))


def _mlp_fwd(h, gain, w1g, w2, name):
    T, D = h.shape
    nf, _, tf = w1g.shape
    tm = min(512, T)

    def body(h_ref, g_ref, w1_ref, w2_ref, o_ref, pre_ref, y_ref):
        x = h_ref[...]
        y = (x * _rms(x) * g_ref[...]).astype(BF16)
        y_ref[...] = y
        acc = x
        for j in range(nf):
            cols = slice(j * tf, (j + 1) * tf)
            pre = jnp.dot(y, w1_ref[j], preferred_element_type=F32)
            pre_ref[:, cols] = pre.astype(BF16)
            act = jnp.square(jnp.maximum(pre, 0.0)).astype(BF16)
            acc = acc + jnp.dot(act, w2_ref[cols, :], preferred_element_type=F32)
        o_ref[...] = acc

    return pl.pallas_call(
        body, name=name, grid=(T // tm,),
        in_specs=[pl.BlockSpec((tm, D), lambda i: (i, 0)), pl.BlockSpec((1, D), lambda i: (0, 0)),
                  _resident(w1g.shape), _resident(w2.shape)],
        out_specs=[pl.BlockSpec((tm, D), lambda i: (i, 0)), pl.BlockSpec((tm, nf * tf), lambda i: (i, 0)),
                   pl.BlockSpec((tm, D), lambda i: (i, 0))],
        out_shape=[jax.ShapeDtypeStruct((T, D), F32), jax.ShapeDtypeStruct((T, nf * tf), BF16),
                   jax.ShapeDtypeStruct((T, D), BF16)],
        compiler_params=_params("parallel"))(h, gain, w1g, w2)


def _mlp_bwd(dout, dout_b, h, gain, pre, w1g, w2, name):
    T, D = h.shape
    nf, _, tf = w1g.shape
    tm = min(256, T)

    def body(do_ref, dob_ref, h_ref, g_ref, pre_ref, w1_ref, w2_ref, dh_ref, dhb_ref, dpre_ref, act_ref, dg_ref):
        @pl.when(pl.program_id(0) == 0)
        def _():
            dg_ref[...] = jnp.zeros_like(dg_ref)

        dob = dob_ref[...]
        dy = None
        for j in range(nf):
            cols = slice(j * tf, (j + 1) * tf)
            rp = jnp.maximum(pre_ref[:, cols].astype(F32), 0.0)
            dpre = (_dg(dob, w2_ref[cols, :], 1, 1) * (2.0 * rp)).astype(BF16)
            dpre_ref[:, cols] = dpre
            act_ref[:, cols] = (rp * rp).astype(BF16)
            part = _dg(dpre, w1_ref[j], 1, 1)
            dy = part if dy is None else dy + part
        dx, dgain = _rmsnorm_bwd(h_ref[...], g_ref[...], dy)
        dh = do_ref[...] + dx
        dh_ref[...] = dh
        dhb_ref[...] = dh.astype(BF16)
        dg_ref[...] += dgain

    tok = lambda w: pl.BlockSpec((tm, w), lambda i: (i, 0))
    return pl.pallas_call(
        body, name=name, grid=(T // tm,),
        in_specs=[tok(D), tok(D), tok(D), pl.BlockSpec((1, D), lambda i: (0, 0)), tok(nf * tf),
                  _resident(w1g.shape), _resident(w2.shape)],
        out_specs=[tok(D), tok(D), tok(nf * tf), tok(nf * tf), pl.BlockSpec((1, D), lambda i: (0, 0))],
        out_shape=[jax.ShapeDtypeStruct((T, D), F32), jax.ShapeDtypeStruct((T, D), BF16),
                   jax.ShapeDtypeStruct((T, nf * tf), BF16),
                   jax.ShapeDtypeStruct((T, nf * tf), BF16), jax.ShapeDtypeStruct((1, D), F32)],
        compiler_params=_params("arbitrary"))(dout, dout_b, h, gain, pre, w1g, w2)


def _final_loss(h, gain, target, name):
    T, D = h.shape
    tm = min(512, T)

    def body(h_ref, g_ref, t_ref, l_ref, dh_ref, dhb_ref, dg_ref):
        @pl.when(pl.program_id(0) == 0)
        def _():
            l_ref[...] = jnp.zeros_like(l_ref)
            dg_ref[...] = jnp.zeros_like(dg_ref)

        x = h_ref[...]
        err = x * _rms(x) * g_ref[...] - t_ref[...]
        l_ref[...] += 0.5 * jnp.sum(jnp.mean(err * err, axis=-1, keepdims=True), axis=0, keepdims=True)
        dx, dgain = _rmsnorm_bwd(x, g_ref[...], err * (1.0 / D))
        dh_ref[...] = dx
        dhb_ref[...] = dx.astype(BF16)
        dg_ref[...] += dgain

    return pl.pallas_call(
        body, name=name, grid=(T // tm,),
        in_specs=[pl.BlockSpec((tm, D), lambda i: (i, 0)), pl.BlockSpec((1, D), lambda i: (0, 0)),
                  pl.BlockSpec((tm, D), lambda i: (i, 0))],
        out_specs=[pl.BlockSpec((SUBLANES, LANES), lambda i: (0, 0)), pl.BlockSpec((tm, D), lambda i: (i, 0)),
                   pl.BlockSpec((tm, D), lambda i: (i, 0)), pl.BlockSpec((1, D), lambda i: (0, 0))],
        out_shape=[jax.ShapeDtypeStruct((SUBLANES, LANES), F32), jax.ShapeDtypeStruct((T, D), F32),
                   jax.ShapeDtypeStruct((T, D), BF16), jax.ShapeDtypeStruct((1, D), F32)],
        compiler_params=_params("arbitrary"))(h, gain, target)


def _halo_specs(tm, T, width, col, tile=lambda i: i):
    r8 = tm // SUBLANES
    nb8 = T // SUBLANES
    return [pl.BlockSpec((tm, width), lambda i: (tile(i), col)),
            pl.BlockSpec((SUBLANES, width), lambda i: (jnp.maximum(tile(i) * r8 - 1, 0), col)),
            pl.BlockSpec((SUBLANES, width), lambda i: (jnp.minimum((tile(i) + 1) * r8, nb8 - 1), col))]


def _ext(cur, prev, nxt, has_prev, has_next):
    return jnp.concatenate([jnp.where(has_prev, prev, 0.0), cur, jnp.where(has_next, nxt, 0.0)], axis=0)


def _shifted(ext, offset, tm):
    n = ext.shape[0]
    sh = (-offset) % n
    r = ext if sh == 0 else pltpu.roll(ext, sh, 0)
    return r[SUBLANES:SUBLANES + tm]


def _rg_conv_fwd(proj, cw8, cb, name):
    T = proj.shape[0]
    tm = min(512, T)
    nT = T // tm

    def body(cur_ref, prev_ref, next_ref, w_ref, b_ref, o_ref):
        i = pl.program_id(0)
        ext = _ext(cur_ref[...], prev_ref[...], next_ref[...], i > 0, i < nT - 1)
        acc = jnp.broadcast_to(b_ref[...], (tm, RG_W))
        for k in range(4):
            acc = acc + w_ref[k:k + 1, :] * _shifted(ext, k - 2, tm)
        o_ref[...] = acc

    return pl.pallas_call(
        body, name=name, grid=(nT,),
        in_specs=_halo_specs(tm, T, RG_W, 0) + [pl.BlockSpec((SUBLANES, RG_W), lambda i: (0, 0)),
                                                pl.BlockSpec((1, RG_W), lambda i: (0, 0))],
        out_specs=pl.BlockSpec((tm, RG_W), lambda i: (i, 0)),
        out_shape=jax.ShapeDtypeStruct((T, RG_W), F32),
        compiler_params=_params("parallel"))(proj, proj, proj, cw8, cb)


def _rg_conv_bwd(dxc, proj, cw8, name):
    T = proj.shape[0]
    tm = min(512, T)
    nT = T // tm

    def body(a0, p0, n0, a1, p1, n1, xa, xp, xn, w_ref, dxa_ref, dw_ref, db_ref):
        i = pl.program_id(0)

        @pl.when(i == 0)
        def _():
            dw_ref[...] = jnp.zeros_like(dw_ref)
            db_ref[...] = jnp.zeros_like(db_ref)

        has_p, has_n = i > 0, i < nT - 1
        cur = a0[...] + a1[...]
        dext = _ext(cur, p0[...] + p1[...], n0[...] + n1[...], has_p, has_n)
        xext = _ext(xa[...], xp[...], xn[...], has_p, has_n)
        acc = jnp.zeros((tm, RG_W), F32)
        rows = []
        for k in range(4):
            acc = acc + w_ref[k:k + 1, :] * _shifted(dext, 2 - k, tm)
            rows.append(jnp.sum(cur * _shifted(xext, k - 2, tm), axis=0, keepdims=True))
        dxa_ref[...] = acc
        dw_ref[...] += jnp.concatenate(rows + [jnp.zeros((4, RG_W), F32)], axis=0)
        db_ref[...] += jnp.sum(cur, axis=0, keepdims=True)

    return pl.pallas_call(
        body, name=name, grid=(nT,),
        in_specs=(_halo_specs(tm, T, RG_W, 0) + _halo_specs(tm, T, RG_W, 0)
                  + _halo_specs(tm, T, RG_W, 0) + [pl.BlockSpec((SUBLANES, RG_W), lambda i: (0, 0))]),
        out_specs=[pl.BlockSpec((tm, RG_W), lambda i: (i, 0)), pl.BlockSpec((SUBLANES, RG_W), lambda i: (0, 0)),
                   pl.BlockSpec((1, RG_W), lambda i: (0, 0))],
        out_shape=[jax.ShapeDtypeStruct((T, RG_W), F32), jax.ShapeDtypeStruct((SUBLANES, RG_W), F32),
                   jax.ShapeDtypeStruct((1, RG_W), F32)],
        compiler_params=_params("arbitrary"))(dxc[0], dxc[0], dxc[0], dxc[1], dxc[1], dxc[1], proj, proj, proj, cw8)


def _local_scan(a, b, ascending):
    n = a.shape[0]
    pos = jnp.bitwise_and(lax.broadcasted_iota(jnp.int32, a.shape, 0), SUBLANES - 1)
    for s in (1, 2, 4):
        sh = s if ascending else n - s
        ok = (pos >= s) if ascending else (pos < SUBLANES - s)
        a_sh, b_sh = pltpu.roll(a, sh, 0), pltpu.roll(b, sh, 0)
        b = jnp.where(ok, a * b_sh + b, b)
        a = jnp.where(ok, a * a_sh, a)
    return a, b


def _group_scan(chains, a_sc, b_sc, carry, n_groups):
    def step(g, hs):
        new = []
        for (d, out_ref, asc), h in zip(chains, hs):
            r0 = pl.multiple_of((g if asc else n_groups - 1 - g) * SUBLANES, SUBLANES)
            out_ref[pl.ds(r0, SUBLANES), :] = a_sc[d, pl.ds(r0, SUBLANES), :] * h + b_sc[d, pl.ds(r0, SUBLANES), :]
            new.append(out_ref[pl.ds(r0 + (SUBLANES - 1 if asc else 0), 1), :])
        return tuple(new)

    hs = lax.fori_loop(0, n_groups, step, tuple(carry[d, 0:1, :] for d, _, _ in chains))
    for (d, _, _), h in zip(chains, hs):
        carry[d, 0:1, :] = h


def _rg_scan_fwd(xc, wbd, bias, lam, name):
    T = xc.shape[0]
    tm = min(512, T)
    nT = T // tm

    def body(xf_ref, xb_ref, w_ref, b_ref, lam_ref, hf_ref, hb_ref, a_sc, b_sc, carry):
        @pl.when(pl.program_id(0) == 0)
        def _():
            carry[...] = jnp.zeros_like(carry)

        for d, x_ref in enumerate((xf_ref, xb_ref)):
            a, u = _rg_gates(x_ref[...], w_ref[d], b_ref[d], lam_ref[d])
            a_sc[d], b_sc[d] = _local_scan(a, u, d == 0)
        _group_scan(((0, hf_ref, True), (1, hb_ref, False)), a_sc, b_sc, carry, tm // SUBLANES)

    full = lambda a: pl.BlockSpec(a.shape, lambda i: (0,) * len(a.shape))
    res = pl.pallas_call(
        body, name=name, grid=(nT,),
        in_specs=[pl.BlockSpec((tm, RG_W), lambda i: (i, 0)), pl.BlockSpec((tm, RG_W), lambda i: (nT - 1 - i, 0)),
                  full(wbd), full(bias), full(lam)],
        out_specs=[pl.BlockSpec((tm, RG_W), lambda i: (i, 0)), pl.BlockSpec((tm, RG_W), lambda i: (nT - 1 - i, 0))],
        out_shape=[jax.ShapeDtypeStruct((T, RG_W), F32)] * 2,
        scratch_shapes=[pltpu.VMEM((2, tm, RG_W), F32), pltpu.VMEM((2, tm, RG_W), F32),
                        pltpu.VMEM((2, SUBLANES, RG_W), F32)],
        compiler_params=_params("arbitrary"))(xc, xc, wbd, bias, lam)
    return res[0], res[1]


def _rg_scan_bwd(xc, wbd, bias, lam, hs, dho, name):
    T = xc.shape[0]
    tm = min(256, T)
    nT = T // tm
    tiles = (lambda i: nT - 1 - i, lambda i: i)

    def body(xf_ref, xb_ref, w_ref, b_ref, lam_ref, hfc, hfp, hfn, hbc, hbp, hbn, dof_ref, dob_ref,
             dxf_ref, dxb_ref, dw_ref, db_ref, dlam_ref, a_sc, b_sc, y_sc, carry):
        i = pl.program_id(0)

        @pl.when(i == 0)
        def _():
            carry[...] = jnp.zeros_like(carry)
            dw_ref[...] = jnp.zeros_like(dw_ref)
            db_ref[...] = jnp.zeros_like(db_ref)
            dlam_ref[...] = jnp.zeros_like(dlam_ref)

        vjps, entering = [], []
        for d, (x_ref, do_ref) in enumerate(((xf_ref, dof_ref), (xb_ref, dob_ref))):
            (a, _), vjp = jax.vjp(_rg_gates, x_ref[...], w_ref[d].astype(F32), b_ref[d], lam_ref[d])
            vjps.append(vjp)
            entering.append(carry[d, 0:1, :])
            a_sc[d], b_sc[d] = _local_scan(a, a * do_ref[...], d == 1)
        _group_scan(((0, y_sc.at[0], False), (1, y_sc.at[1], True)), a_sc, b_sc, carry, tm // SUBLANES)

        row = lax.broadcasted_iota(jnp.int32, (tm, RG_W), 0)
        for d, (do_ref, dx_ref, hc, hp, hn, ti) in enumerate(
                ((dof_ref, dxf_ref, hfc, hfp, hfn, nT - 1 - i), (dob_ref, dxb_ref, hbc, hbp, hbn, i))):
            y = y_sc[d]
            if d == 0:
                y_next = jnp.where(row == tm - 1, entering[d], pltpu.roll(y, tm - 1, 0))
            else:
                y_next = jnp.where(row == 0, entering[d], pltpu.roll(y, 1, 0))
            dtot = do_ref[...] + y_next
            ext = _ext(hc[...], hp[...], hn[...], ti > 0, ti < nT - 1)
            hprev = _shifted(ext, -1 if d == 0 else 1, tm)
            dxc, dw, db, dlam = vjps[d]((dtot * hprev, dtot))
            dx_ref[...] = dxc
            dw_ref[d] += dw
            db_ref[d] += db
            dlam_ref[d] += dlam

    full = lambda a: pl.BlockSpec(a.shape, lambda i: (0,) * len(a.shape))
    tok = lambda d: pl.BlockSpec((tm, RG_W), lambda i: (tiles[d](i), 0))
    acc_shapes = [jax.ShapeDtypeStruct((2, RG_W, 2 * RG_W), F32), jax.ShapeDtypeStruct((2, 1, 2 * RG_W), F32),
                  jax.ShapeDtypeStruct((2, 1, RG_W), F32)]
    res = pl.pallas_call(
        body, name=name, grid=(nT,),
        in_specs=([tok(0), tok(1), full(wbd), full(bias), full(lam)]
                  + _halo_specs(tm, T, RG_W, 0, tiles[0]) + _halo_specs(tm, T, RG_W, 0, tiles[1]) + [tok(0), tok(1)]),
        out_specs=[tok(0), tok(1)] + [full(s) for s in acc_shapes],
        out_shape=[jax.ShapeDtypeStruct((T, RG_W), F32)] * 2 + acc_shapes,
        scratch_shapes=[pltpu.VMEM((2, tm, RG_W), F32), pltpu.VMEM((2, tm, RG_W), F32),
                        pltpu.VMEM((2, tm, RG_W), F32), pltpu.VMEM((2, SUBLANES, RG_W), F32)],
        compiler_params=_params("arbitrary"))(xc, xc, wbd, bias, lam, hs[0], hs[0], hs[0], hs[1], hs[1], hs[1],
                                              dho, dho)
    return (res[0], res[1]), res[2], res[3], res[4]


def _chunk_rows(n_chunks, reverse):
    up, down = (lambda c: c), (lambda c: n_chunks - 1 - c)
    return (down, up) if reverse else (up, down)


STEP_CHUNKS = 4
STEP_ROWS = STEP_CHUNKS * CHUNK


def _sub_chunks(ascending):
    order = range(STEP_CHUNKS) if ascending else range(STEP_CHUNKS - 1, -1, -1)
    return [(s, slice(s * CHUNK, (s + 1) * CHUNK)) for s in order]


def _hg_fwd(proj, l0, l1, name):
    T = proj.shape[0]
    nC = T // CHUNK
    nS = nC // STEP_CHUNKS
    H, dk, dv = 4, 128, 128
    rows = _chunk_rows(nS, False)

    def body(qf, ff, vf, qb, fb, vb, l0_ref, l1_ref, of, ob, spf, spb, st):
        @pl.when(pl.program_id(0) == 0)
        def _():
            st[...] = jnp.zeros_like(st)

        for d, (q, f, v, o, sp) in enumerate(((qf, ff, vf, of, spf), (qb, fb, vb, ob, spb))):
            tri, tri_t, mref = _tri_consts(d)
            stp = tuple(st[d, h] for h in range(H))
            for s, r in _sub_chunks(d == 0):
                for h in range(H):
                    sp[s, h] = stp[h]
                o_val, stp = _hg_chunk(q[r, :], f[r, :], v[r, :], l0_ref[...], l1_ref[...], stp, tri, tri_t, mref)
                o[r, :] = o_val
            for h in range(H):
                st[d, h] = stp[h]

    tok = lambda d, col: pl.BlockSpec((STEP_ROWS, HG_W), lambda c: (rows[d](c), col))
    par = pl.BlockSpec((1, HG_W), lambda c: (0, 0))
    state = lambda d: pl.BlockSpec((STEP_CHUNKS, H, dv, dk), lambda c: (rows[d](c), 0, 0, 0))
    res = pl.pallas_call(
        body, name=name, grid=(nS,),
        in_specs=[tok(0, 2), tok(0, 3), tok(0, 5), tok(1, 2), tok(1, 4), tok(1, 5), par, par],
        out_specs=[tok(0, 0), tok(1, 0), state(0), state(1)],
        out_shape=[jax.ShapeDtypeStruct((T, H * dv), F32)] * 2 + [jax.ShapeDtypeStruct((nC, H, dv, dk), F32)] * 2,
        scratch_shapes=[pltpu.VMEM((2, H, dv, dk), F32)],
        compiler_params=_params("arbitrary"))(proj, proj, proj, proj, proj, proj, l0, l1)
    return (res[0], res[1]), (res[2], res[3])


def _hg_bwd(proj, l0, l1, sprev, do, name):
    T = proj.shape[0]
    nC = T // CHUNK
    nS = nC // STEP_CHUNKS
    H, dk, dv = 4, 128, 128
    rows = _chunk_rows(nS, True)

    def body(qf, ff, vf, qb, fb, vb, l0_ref, l1_ref, spf, spb, dof, dob,
             dqf, dff, dvf, dqb, dfb, dvb, dl0_ref, dl1_ref, dst):
        @pl.when(pl.program_id(0) == 0)
        def _():
            dst[...] = jnp.zeros_like(dst)
            dl0_ref[...] = jnp.zeros_like(dl0_ref)
            dl1_ref[...] = jnp.zeros_like(dl1_ref)

        for d, (q, f, v, sp, do_ref, dq_ref, df_ref, dv_ref) in enumerate(
                ((qf, ff, vf, spf, dof, dqf, dff, dvf), (qb, fb, vb, spb, dob, dqb, dfb, dvb))):
            tri, tri_t, mref = _tri_consts(d)
            fn = lambda q_, f_, v_, a0, a1, stp: _hg_chunk(q_, f_, v_, a0, a1, stp, tri, tri_t, mref)
            dstp = tuple(dst[d, h] for h in range(H))
            for s, r in _sub_chunks(d == 1):
                stp = tuple(sp[s, h] for h in range(H))
                _, vjp = jax.vjp(fn, q[r, :], f[r, :], v[r, :], l0_ref[...], l1_ref[...], stp)
                dq, df, dvv, dl0, dl1, dstp = vjp((do_ref[r, :], dstp))
                dq_ref[r, :] = dq.astype(BF16)
                df_ref[r, :] = df.astype(BF16)
                dv_ref[r, :] = dvv.astype(BF16)
                dl0_ref[d] += dl0
                dl1_ref[d] += dl1
            for h in range(H):
                dst[d, h] = dstp[h]

    tok = lambda d, col: pl.BlockSpec((STEP_ROWS, HG_W), lambda c: (rows[d](c), col))
    par = pl.BlockSpec((1, HG_W), lambda c: (0, 0))
    acc = pl.BlockSpec((2, 1, HG_W), lambda c: (0, 0, 0))
    state = lambda d: pl.BlockSpec((STEP_CHUNKS, H, dv, dk), lambda c: (rows[d](c), 0, 0, 0))
    res = pl.pallas_call(
        body, name=name, grid=(nS,),
        in_specs=[tok(0, 2), tok(0, 3), tok(0, 5), tok(1, 2), tok(1, 4), tok(1, 5), par, par,
                  state(0), state(1), tok(0, 0), tok(1, 0)],
        out_specs=[tok(0, 0)] * 3 + [tok(1, 0)] * 3 + [acc, acc],
        out_shape=[jax.ShapeDtypeStruct((T, HG_W), BF16)] * 6 + [jax.ShapeDtypeStruct((2, 1, HG_W), F32)] * 2,
        scratch_shapes=[pltpu.VMEM((2, H, dv, dk), F32)],
        compiler_params=_params("arbitrary"))(proj, proj, proj, proj, proj, proj, l0, l1, sprev[0], sprev[1], do, do)
    return (res[0], res[3]), (res[1], res[4]), (res[2], res[5]), res[6], res[7]


def _gate_logits(proj, wup, bg, name):
    T = proj.shape[0]
    tm = min(512, T)

    def body(lr_ref, w_ref, b_ref, z_ref, lrb_ref):
        lr = lr_ref[...].astype(BF16)
        lrb_ref[...] = lr
        for d in range(2):
            z_ref[d] = _dg(lr, w_ref[d], 1, 0) + b_ref[d]

    return pl.pallas_call(
        body, name=name, grid=(T // tm,),
        in_specs=[pl.BlockSpec((tm, LANES), lambda i: (i, 24)), pl.BlockSpec((2, LANES, 512), lambda i: (0, 0, 0)),
                  pl.BlockSpec((2, 1, 512), lambda i: (0, 0, 0))],
        out_specs=[pl.BlockSpec((2, tm, 512), lambda i: (0, i, 0)), pl.BlockSpec((tm, LANES), lambda i: (i, 0))],
        out_shape=[jax.ShapeDtypeStruct((2, T, 512), F32), jax.ShapeDtypeStruct((T, LANES), BF16)],
        compiler_params=_params("parallel"))(proj, wup, bg)


def _gate_logits_bwd(dz, wup, name):
    T = dz[0].shape[0]
    tm = min(512, T)

    def body(dzf_ref, dzb_ref, w_ref, dlr_ref, db_ref, dzb16_ref):
        @pl.when(pl.program_id(0) == 0)
        def _():
            db_ref[...] = jnp.zeros_like(db_ref)

        acc = jnp.zeros((tm, LANES), F32)
        for d, dz_ref in enumerate((dzf_ref, dzb_ref)):
            g = dz_ref[...]
            gb = g.astype(BF16)
            dzb16_ref[d] = gb
            acc = acc + _dg(gb, w_ref[d], 1, 1)
            db_ref[d] += jnp.sum(g, axis=0, keepdims=True)
        dlr_ref[...] = acc

    tok = pl.BlockSpec((tm, 512), lambda i: (i, 0))
    return pl.pallas_call(
        body, name=name, grid=(T // tm,),
        in_specs=[tok, tok, pl.BlockSpec((2, LANES, 512), lambda i: (0, 0, 0))],
        out_specs=[pl.BlockSpec((tm, LANES), lambda i: (i, 0)), pl.BlockSpec((2, 1, 512), lambda i: (0, 0, 0)),
                   pl.BlockSpec((2, tm, 512), lambda i: (0, i, 0))],
        out_shape=[jax.ShapeDtypeStruct((T, LANES), F32), jax.ShapeDtypeStruct((2, 1, 512), F32),
                   jax.ShapeDtypeStruct((2, T, 512), BF16)],
        compiler_params=_params("arbitrary"))(dz[0], dz[1], wup)


def _gla_fwd(proj, z, name):
    T = proj.shape[0]
    nC = T // CHUNK
    nS = nC // STEP_CHUNKS
    H, dk, dv = 4, 128, 256
    rows = _chunk_rows(nS, False)

    def body(qf, kf, vf, zf, qb, kb, vb, zb, of, ob, spf, spb, st):
        @pl.when(pl.program_id(0) == 0)
        def _():
            st[...] = jnp.zeros_like(st)

        for d, (q, k, v, z_ref, o, sp) in enumerate(((qf, kf, vf, zf, of, spf), (qb, kb, vb, zb, ob, spb))):
            tri, tri_t, mref = _tri_consts(d)
            stp = tuple(st[d, h] for h in range(H))
            for s, r in _sub_chunks(d == 0):
                for h in range(H):
                    sp[s, h] = stp[h]
                o_val, stp = _gla_chunk(q[r, :], k[r, :], v[r, :], z_ref[r, :], stp, tri, tri_t, mref)
                o[r, :] = o_val
            for h in range(H):
                st[d, h] = stp[h]

    tok = lambda d, w, col: pl.BlockSpec((STEP_ROWS, w), lambda c: (rows[d](c), col))
    gate = lambda d: pl.BlockSpec((None, STEP_ROWS, 512), lambda c: (d, rows[d](c), 0))
    state = lambda d: pl.BlockSpec((STEP_CHUNKS, H, dv, dk), lambda c: (rows[d](c), 0, 0, 0))
    res = pl.pallas_call(
        body, name=name, grid=(nS,),
        in_specs=[tok(0, 512, 0), tok(0, 512, 1), tok(0, 1024, 1), gate(0),
                  tok(1, 512, 0), tok(1, 512, 1), tok(1, 1024, 1), gate(1)],
        out_specs=[tok(0, H * dv, 0), tok(1, H * dv, 0), state(0), state(1)],
        out_shape=[jax.ShapeDtypeStruct((T, H * dv), F32)] * 2 + [jax.ShapeDtypeStruct((nC, H, dv, dk), F32)] * 2,
        scratch_shapes=[pltpu.VMEM((2, H, dv, dk), F32)],
        compiler_params=_params("arbitrary"))(proj, proj, proj, z, proj, proj, proj, z)
    return (res[0], res[1]), (res[2], res[3])


def _gla_bwd(proj, z, sprev, do, name):
    T = proj.shape[0]
    nC = T // CHUNK
    nS = nC // STEP_CHUNKS
    H, dk, dv = 4, 128, 256
    rows = _chunk_rows(nS, True)

    def body(qf, kf, vf, zf, qb, kb, vb, zb, spf, spb, dof, dob,
             dqf, dkf, dvf, dzf, dqb, dkb, dvb, dzb, dst):
        @pl.when(pl.program_id(0) == 0)
        def _():
            dst[...] = jnp.zeros_like(dst)

        for d, (q, k, v, z_ref, sp, do_ref, dq_ref, dk_ref, dv_ref, dz_ref) in enumerate(
                ((qf, kf, vf, zf, spf, dof, dqf, dkf, dvf, dzf), (qb, kb, vb, zb, spb, dob, dqb, dkb, dvb, dzb))):
            tri, tri_t, mref = _tri_consts(d)
            fn = lambda q_, k_, v_, z_, stp: _gla_chunk(q_, k_, v_, z_, stp, tri, tri_t, mref)
            dstp = tuple(dst[d, h] for h in range(H))
            for s, r in _sub_chunks(d == 1):
                stp = tuple(sp[s, h] for h in range(H))
                _, vjp = jax.vjp(fn, q[r, :], k[r, :], v[r, :], z_ref[r, :], stp)
                dq, dkk, dvv, dzz, dstp = vjp((do_ref[r, :], dstp))
                dq_ref[r, :] = dq.astype(BF16)
                dk_ref[r, :] = dkk.astype(BF16)
                dv_ref[r, :] = dvv.astype(BF16)
                dz_ref[r, :] = dzz
            for h in range(H):
                dst[d, h] = dstp[h]

    tok = lambda d, w, col: pl.BlockSpec((STEP_ROWS, w), lambda c: (rows[d](c), col))
    gate = lambda d: pl.BlockSpec((None, STEP_ROWS, 512), lambda c: (d, rows[d](c), 0))
    state = lambda d: pl.BlockSpec((STEP_CHUNKS, H, dv, dk), lambda c: (rows[d](c), 0, 0, 0))
    outs = lambda d: [tok(d, 512, 0), tok(d, 512, 0), tok(d, 1024, 0), tok(d, 512, 0)]
    shapes = [jax.ShapeDtypeStruct((T, 512), BF16), jax.ShapeDtypeStruct((T, 512), BF16),
              jax.ShapeDtypeStruct((T, 1024), BF16), jax.ShapeDtypeStruct((T, 512), F32)]
    res = pl.pallas_call(
        body, name=name, grid=(nS,),
        in_specs=[tok(0, 512, 0), tok(0, 512, 1), tok(0, 1024, 1), gate(0),
                  tok(1, 512, 0), tok(1, 512, 1), tok(1, 1024, 1), gate(1),
                  state(0), state(1), tok(0, H * dv, 0), tok(1, H * dv, 0)],
        out_specs=outs(0) + outs(1), out_shape=shapes + shapes,
        scratch_shapes=[pltpu.VMEM((2, H, dv, dk), F32)],
        compiler_params=_params("arbitrary"))(proj, proj, proj, z, proj, proj, proj, z, sprev[0], sprev[1], do, do)
    return (res[0], res[4]), (res[1], res[5]), (res[2], res[6]), (res[3], res[7])


def _l0_combine_fwd(hs, proj, o, gain, name):
    T = proj.shape[0]
    tm = min(512, T)

    def body(hf, hb, ga, of, ob, g, gn, out):
        out[...] = _l0_combine(hf[...], hb[...], ga[...], of[...], ob[...], g[...], gn[...]).astype(BF16)

    tok = pl.BlockSpec((tm, 512), lambda i: (i, 0))
    return pl.pallas_call(
        body, name=name, grid=(T // tm,),
        in_specs=[tok, tok, pl.BlockSpec((tm, 512), lambda i: (i, 1)), tok, tok,
                  pl.BlockSpec((tm, 512), lambda i: (i, 6)), pl.BlockSpec((1, 512), lambda i: (0, 0))],
        out_specs=pl.BlockSpec((tm, 1024), lambda i: (i, 0)),
        out_shape=jax.ShapeDtypeStruct((T, 1024), BF16),
        compiler_params=_params("parallel"))(hs[0], hs[1], proj, o[0], o[1], proj, gain)


def _l0_combine_bwd(hs, proj, o, gain, dh_b, w_out, name):
    T = proj.shape[0]
    tm = min(512, T)

    def body(hf, hb, ga, of, ob, g, gn, dhb_ref, w_ref, dho_ref, dga_ref, do_ref, dg_ref, dgn_ref):
        @pl.when(pl.program_id(0) == 0)
        def _():
            dgn_ref[...] = jnp.zeros_like(dgn_ref)

        _, vjp = jax.vjp(_l0_combine, hf[...], hb[...], ga[...], of[...], ob[...], g[...], gn[...])
        dhf, _, dga, dof, _, dg, dgn = vjp(_dg(dhb_ref[...], w_ref[...], 1, 1))
        dho_ref[...] = dhf
        dga_ref[...] = dga
        do_ref[...] = dof
        dg_ref[...] = dg
        dgn_ref[...] += dgn

    tok = lambda: pl.BlockSpec((tm, 512), lambda i: (i, 0))
    return pl.pallas_call(
        body, name=name, grid=(T // tm,),
        in_specs=[tok(), tok(), pl.BlockSpec((tm, 512), lambda i: (i, 1)), tok(), tok(),
                  pl.BlockSpec((tm, 512), lambda i: (i, 6)), pl.BlockSpec((1, 512), lambda i: (0, 0)),
                  pl.BlockSpec((tm, D_MODEL), lambda i: (i, 0)), pl.BlockSpec(w_out.shape, lambda i: (0, 0))],
        out_specs=[tok(), tok(), tok(), tok(), pl.BlockSpec((1, 512), lambda i: (0, 0))],
        out_shape=[jax.ShapeDtypeStruct((T, 512), F32)] * 4 + [jax.ShapeDtypeStruct((1, 512), F32)],
        compiler_params=_params("arbitrary"))(hs[0], hs[1], proj, o[0], o[1], proj, gain, dh_b, w_out)


def _l0_assemble(dxa, dga, dq, df, dv, dg, name):
    T = dxa.shape[0]
    tm = min(512, T)

    def body(xa, ga, q0, q1, f0, f1, v0, v1, g, out):
        both = lambda a, b: (a[...].astype(F32) + b[...].astype(F32)).astype(BF16)
        out[...] = jnp.concatenate([xa[...].astype(BF16), ga[...].astype(BF16), both(q0, q1), f0[...], f1[...],
                                    both(v0, v1), g[...].astype(BF16)], axis=1)

    tok = lambda: pl.BlockSpec((tm, 512), lambda i: (i, 0))
    return pl.pallas_call(
        body, name=name, grid=(T // tm,),
        in_specs=[tok() for _ in range(9)],
        out_specs=pl.BlockSpec((tm, AB_IN), lambda i: (i, 0)),
        out_shape=jax.ShapeDtypeStruct((T, AB_IN), BF16),
        compiler_params=_params("parallel"))(dxa, dga, dq[0], dq[1], df[0], df[1], dv[0], dv[1], dg)


def _l1_combine_fwd(o, proj, gain, name):
    T = proj.shape[0]
    tm = min(512, T)

    def body(of, ob, r, gn, out):
        out[...] = _l1_combine(of[...], ob[...], r[...], gn[...]).astype(BF16)

    tok = pl.BlockSpec((tm, 1024), lambda i: (i, 0))
    return pl.pallas_call(
        body, name=name, grid=(T // tm,),
        in_specs=[tok, tok, pl.BlockSpec((tm, 1024), lambda i: (i, 2)), pl.BlockSpec((1, 1024), lambda i: (0, 0))],
        out_specs=pl.BlockSpec((tm, 1024), lambda i: (i, 0)),
        out_shape=jax.ShapeDtypeStruct((T, 1024), BF16),
        compiler_params=_params("parallel"))(o[0], o[1], proj, gain)


def _l1_combine_bwd(o, proj, gain, dh_b, w_out, name):
    T = proj.shape[0]
    tm = min(512, T)

    def body(of, ob, r, gn, dhb_ref, w_ref, do_ref, dr_ref, dgn_ref):
        @pl.when(pl.program_id(0) == 0)
        def _():
            dgn_ref[...] = jnp.zeros_like(dgn_ref)

        _, vjp = jax.vjp(_l1_combine, of[...], ob[...], r[...], gn[...])
        dof, _, dr, dgn = vjp(_dg(dhb_ref[...], w_ref[...], 1, 1))
        do_ref[...] = dof
        dr_ref[...] = dr
        dgn_ref[...] += dgn

    tok = lambda: pl.BlockSpec((tm, 1024), lambda i: (i, 0))
    return pl.pallas_call(
        body, name=name, grid=(T // tm,),
        in_specs=[tok(), tok(), pl.BlockSpec((tm, 1024), lambda i: (i, 2)),
                  pl.BlockSpec((1, 1024), lambda i: (0, 0)), tok(), pl.BlockSpec(w_out.shape, lambda i: (0, 0))],
        out_specs=[tok(), tok(), pl.BlockSpec((1, 1024), lambda i: (0, 0))],
        out_shape=[jax.ShapeDtypeStruct((T, 1024), F32)] * 2 + [jax.ShapeDtypeStruct((1, 1024), F32)],
        compiler_params=_params("arbitrary"))(o[0], o[1], proj, gain, dh_b, w_out)


def _l1_assemble(dq, dk, dv, dr, dlr, name):
    T = dr.shape[0]
    tm = min(512, T)

    def body(q0, q1, k0, k1, v0, v1, r, a, out):
        both = lambda x, y: (x[...].astype(F32) + y[...].astype(F32)).astype(BF16)
        out[...] = jnp.concatenate([both(q0, q1), both(k0, k1), both(v0, v1), r[...].astype(BF16),
                                    a[...].astype(BF16)], axis=1)

    tok = lambda w: pl.BlockSpec((tm, w), lambda i: (i, 0))
    return pl.pallas_call(
        body, name=name, grid=(T // tm,),
        in_specs=[tok(512), tok(512), tok(512), tok(512), tok(1024), tok(1024), tok(1024), tok(LANES)],
        out_specs=pl.BlockSpec((tm, GLA_IN_PAD), lambda i: (i, 0)),
        out_shape=jax.ShapeDtypeStruct((T, GLA_IN_PAD), BF16),
        compiler_params=_params("parallel"))(dq[0], dq[1], dk[0], dk[1], dv[0], dv[1], dr, dlr)


HBM_SPEC = pl.BlockSpec(memory_space=pltpu.HBM)


def _place():
    x, y, c = lax.axis_index("x"), lax.axis_index("y"), lax.axis_index("c")
    return x, y, c


def _allgather_vmem(x_shard, name):
    m_per, n = x_shard.shape

    def body(x_ref, out_ref, send_sems, recv_sems, local_sem):
        x, y, c = _place()
        me, sibling = (x, y, c), (x, y, 1 - c)
        chips = [(1 - x, y), (x, 1 - y), (1 - x, 1 - y)]

        def rows(px, py, pc):
            return out_ref.at[pl.ds((4 * px + 2 * py + pc) * m_per, m_per), :]

        def copy(k, block, to, src=None):
            return pltpu.make_async_remote_copy(
                src_ref=rows(*block) if src is None else src, dst_ref=rows(*block),
                send_sem=send_sems.at[k], recv_sem=recv_sems.at[k], device_id=to, device_id_type=MESH)

        mine = pltpu.make_async_copy(x_ref, rows(*me), local_sem)
        mine.start()
        first = [copy(0, me, sibling, src=x_ref)]
        first += [copy(1 + j, me, (*chip, c), src=x_ref) for j, chip in enumerate(chips)]
        for cp in first:
            cp.start()
        passed = [copy(4 + j, (*chip, c), sibling) for j, chip in enumerate(chips)]
        for j, chip in enumerate(chips):
            copy(1 + j, (*chip, c), me).wait_recv()
            passed[j].start()
        copy(0, sibling, me).wait_recv()
        for j, chip in enumerate(chips):
            copy(4 + j, (*chip, 1 - c), me).wait_recv()
        for cp in first + passed:
            cp.wait_send()
        mine.wait()

    vm = pl.BlockSpec(memory_space=pltpu.VMEM)
    return pl.pallas_call(
        body, name=name, in_specs=[vm], out_specs=vm,
        out_shape=jax.ShapeDtypeStruct((N_DEV * m_per, n), x_shard.dtype),
        scratch_shapes=[pltpu.SemaphoreType.DMA((7,)), pltpu.SemaphoreType.DMA((7,)), pltpu.SemaphoreType.DMA],
        compiler_params=pltpu.CompilerParams(has_side_effects=True, vmem_limit_bytes=VMEM_LIMIT))(x_shard)


SEM_SPEC = pl.BlockSpec(memory_space=pltpu.SEMAPHORE)
DATAFLOW_EFFECT = pltpu.SideEffectType.DATAFLOW_SIDE_EFFECTING


def _copies(plan, srcs, lands, send_sems, recv_sems):
    x, y, c = _place()
    return [pltpu.make_async_remote_copy(src_ref=s, dst_ref=d, send_sem=send_sems.at[k], recv_sem=recv_sems.at[k],
                                         device_id=dev, device_id_type=MESH)
            for k, (s, d, dev) in enumerate(plan(srcs, lands, x, y, c))]


def _copies_start(plan, n_copies, srcs, lands, name):
    ns, nl = len(srcs), len(lands)

    def body(*refs):
        send_sems, recv_sems = refs[ns + nl], refs[ns + nl + 1]
        for cp in _copies(plan, refs[:ns], refs[ns:ns + nl], send_sems, recv_sems):
            cp.start()
        refs[-1][...] = jnp.zeros_like(refs[-1])

    arrays = list(srcs) + list(lands)
    res = pl.pallas_call(
        body, name=name,
        in_specs=[HBM_SPEC] * (ns + nl),
        out_specs=tuple([SEM_SPEC, SEM_SPEC] + [HBM_SPEC] * (ns + nl) + [pl.BlockSpec(memory_space=pltpu.VMEM)]),
        out_shape=tuple([pltpu.SemaphoreType.DMA((n_copies,)), pltpu.SemaphoreType.DMA((n_copies,))]
                        + [pltpu.HBM(a.shape, a.dtype) for a in arrays]
                        + [jax.ShapeDtypeStruct((SUBLANES, LANES), F32)]),
        input_output_aliases={i: 2 + i for i in range(ns + nl)},
        compiler_params=pltpu.CompilerParams(has_side_effects=DATAFLOW_EFFECT),
    )(*[pltpu.with_memory_space_constraint(a, pltpu.HBM) for a in arrays])
    return res[0], res[1], list(res[2:2 + ns]), list(res[2 + ns:2 + ns + nl]), res[-1]


def _copies_wait(plan, started, after, name):
    send_sems, recv_sems, srcs, lands, _ = started
    ns, nl = len(srcs), len(lands)

    def body(*refs):
        for cp in _copies(plan, refs[:ns], refs[ns:ns + nl], refs[ns + nl], refs[ns + nl + 1]):
            cp.wait_send()
            cp.wait_recv()

    arrays = list(srcs) + list(lands)
    res = pl.pallas_call(
        body, name=name,
        in_specs=[HBM_SPEC] * (ns + nl) + [SEM_SPEC, SEM_SPEC, pl.BlockSpec(memory_space=pl.ANY)],
        out_specs=tuple([HBM_SPEC] * (ns + nl)),
        out_shape=tuple(pltpu.HBM(a.shape, a.dtype) for a in arrays),
        input_output_aliases={i: i for i in range(ns + nl)},
        compiler_params=pltpu.CompilerParams(has_side_effects=DATAFLOW_EFFECT),
    )(*arrays, send_sems, recv_sems, after)
    return list(res[:ns]), list(res[ns:])


def _after(token, value):
    return value + token[0:1, 0:1].astype(value.dtype)


def _chips(x, y):
    return [(1 - x, y), (x, 1 - y), (1 - x, 1 - y)]


def _plan_gather_first(srcs, lands, x, y, c):
    me = 4 * x + 2 * y + c
    out = []
    for s, l in zip(srcs, lands):
        out.append((s, l.at[me], (x, y, 1 - c)))
        out += [(s, l.at[me], (*chip, c)) for chip in _chips(x, y)]
    return out


def _plan_gather_pass(srcs, lands, x, y, c):
    out = []
    for l in lands:
        for chip in _chips(x, y):
            slot = l.at[4 * chip[0] + 2 * chip[1] + c]
            out.append((slot, slot, (x, y, 1 - c)))
    return out


def _plan_grads_sibling(srcs, lands, x, y, c):
    return [(s.at[2 * q + (1 - c)], l.at[q], (x, y, 1 - c)) for s, l in zip(srcs, lands) for q in range(4)]


def _plan_grads_chips(srcs, lands, x, y, c):
    return [(s.at[2 * chip[0] + chip[1]], l.at[k], (*chip, c))
            for s, l in zip(srcs, lands) for k, chip in enumerate(_chips(x, y))]


def _landing(n_slots, like):
    return [lax.empty((n_slots,) + a.shape[1:], a.dtype) for a in like]


def _sum_slots(g, name):
    _, R, C = g.shape
    tr = min(256, R)
    assert R % tr == 0

    def body(g_ref, o_ref):
        acc = g_ref[0]
        for j in range(1, N_DEV):
            acc = acc + g_ref[j]
        o_ref[...] = acc

    return pl.pallas_call(
        body, name=name, grid=(R // tr,),
        in_specs=[pl.BlockSpec((N_DEV, tr, C), lambda i: (0, i, 0))],
        out_specs=pl.BlockSpec((tr, C), lambda i: (i, 0)),
        out_shape=jax.ShapeDtypeStruct((R, C), F32),
        compiler_params=_params("parallel"))(g)


def _chip_partial(g, r1, place, name):
    _, R, C = g.shape
    tr = min(1024, R)
    assert R % tr == 0

    def body(pl_ref, g_ref, r_ref, pb_ref, pm_ref):
        q = pl.program_id(1)
        s = g_ref[...] + r_ref[...]
        pb_ref[...] = s.astype(BF16)

        @pl.when(q == pl_ref[1])
        def _():
            pm_ref[...] = s

    grid_spec = pltpu.PrefetchScalarGridSpec(
        num_scalar_prefetch=1, grid=(R // tr, 4),
        in_specs=[pl.BlockSpec((None, tr, C), lambda r, q, p: (2 * q + p[0], r, 0)),
                  pl.BlockSpec((None, tr, C), lambda r, q, p: (q, r, 0))],
        out_specs=[pl.BlockSpec((None, tr, C), lambda r, q, p: (q, r, 0)),
                   pl.BlockSpec((tr, C), lambda r, q, p: (r, 0))])
    return pl.pallas_call(
        body, name=name, grid_spec=grid_spec,
        out_shape=[jax.ShapeDtypeStruct((4, R, C), BF16), jax.ShapeDtypeStruct((R, C), F32)],
        compiler_params=_params("parallel", "arbitrary"))(place, g, r1)


def _adamw_update(w, g, m, v, grad_ref, delta_ref, m_ref, v_ref):
    mn = ADAM_B1 * m + (1.0 - ADAM_B1) * g
    vn = ADAM_B2 * v + (1.0 - ADAM_B2) * jnp.square(g)
    m_hat = mn / (1.0 - ADAM_B1 ** ADAM_STEP)
    v_hat = vn / (1.0 - ADAM_B2 ** ADAM_STEP)
    grad_ref[...] = g
    delta_ref[...] = -ADAM_LR * (m_hat / (jnp.sqrt(v_hat) + ADAM_EPS) + ADAM_WD * w)
    m_ref[...] = mn
    v_ref[...] = vn


def _adamw_whole(w, g, m, v, name):
    def body(w_ref, g_ref, m_ref, v_ref, go, do, mo, vo):
        _adamw_update(w_ref[...], g_ref[...], m_ref[...], v_ref[...], go, do, mo, vo)

    vm = pl.BlockSpec(memory_space=pltpu.VMEM)
    return pl.pallas_call(body, name=name, in_specs=[vm] * 4, out_specs=[vm] * 4,
                          out_shape=[jax.ShapeDtypeStruct(w.shape, F32)] * 4)(w, g, m, v)


def _adamw(w, gparts, m, v, name):
    _, R, C = w.shape
    tr = min(256, R)
    assert R % tr == 0

    def body(w_ref, g0_ref, g3_ref, m_ref, v_ref, go, do, mo, vo):
        g = g0_ref[...]
        for k in range(3):
            g = g + g3_ref[k].astype(F32)
        _adamw_update(w_ref[...], g, m_ref[...], v_ref[...], go, do, mo, vo)

    blk = pl.BlockSpec((tr, C), lambda i: (i, 0))
    wblk = pl.BlockSpec((None, tr, C), lambda i: (0, i, 0))
    return pl.pallas_call(
        body, name=name, grid=(R // tr,),
        in_specs=[wblk, blk, pl.BlockSpec((3, tr, C), lambda i: (0, i, 0)), wblk, wblk], out_specs=[wblk] * 4,
        out_shape=[jax.ShapeDtypeStruct(w.shape, F32)] * 4,
        compiler_params=_params("parallel"))(w, gparts[0], gparts[1], m, v)


def _adamw_layers(w, parts, m, v, name):
    _, R, C = w.shape
    tr = min(256, R)
    assert R % tr == 0

    def body(w_ref, p0, r0, p1, r1, m_ref, v_ref, go, do, mo, vo):
        gs = []
        for p, r in ((p0, r0), (p1, r1)):
            g = p[...]
            for k in range(3):
                g = g + r[k].astype(F32)
            gs.append(g)
        g = jnp.where(pl.program_id(0) == 0, gs[0], gs[1])
        _adamw_update(w_ref[...], g, m_ref[...], v_ref[...], go, do, mo, vo)

    lay = pl.BlockSpec((None, tr, C), lambda l, i: (l, i, 0))
    one = pl.BlockSpec((tr, C), lambda l, i: (i, 0))
    three = pl.BlockSpec((3, tr, C), lambda l, i: (0, i, 0))
    return pl.pallas_call(
        body, name=name, grid=(2, R // tr), in_specs=[lay, one, three, one, three, lay, lay],
        out_specs=[lay] * 4, out_shape=[jax.ShapeDtypeStruct((2, R, C), F32)] * 4,
        compiler_params=_params("parallel", "parallel"))(w, parts[0][0], parts[0][1], parts[1][0], parts[1][1], m, v)


SMALL_SHARDED = ("rg_conv_w", "rg_b_a", "rg_b_x", "rg_lambda", "gla_w_gate_up", "gla_b_gate", "gla_norm")
SMALL_REPLICATED = ("norm_mix", "norm_mlp", "norm_final", "rg_conv_b", "rg_w_a", "rg_w_x", "hg_lb_logits", "hg_norm")
WEIGHT_NAMES = ("norm_mix", "norm_mlp", "norm_final", "mlp_w1", "mlp_w2", "ab_w_in", "ab_w_out", "rg_conv_w",
                "rg_conv_b", "rg_w_a", "rg_b_a", "rg_w_x", "rg_b_x", "rg_lambda", "hg_lb_logits", "hg_norm",
                "gla_w_in", "gla_w_out", "gla_w_gate_up", "gla_b_gate", "gla_norm")


def _rows128(a):
    return a.reshape(-1, LANES)


def _part_rows(a):
    return -(-(a.size // LANES) // SUBLANES) * SUBLANES


def _pack_rows(arrays, pad_to=SUBLANES):
    parts = [jnp.pad(_rows128(a), ((0, _part_rows(a) - a.size // LANES), (0, 0))) for a in arrays]
    total = sum(p.shape[0] for p in parts)
    extra = (-total) % pad_to
    if extra:
        parts.append(jnp.zeros((extra, LANES), parts[0].dtype))
    return jnp.concatenate(parts, axis=0)


def _unshard_last(g, shape_local):
    nd = len(shape_local)
    t = g.reshape((N_DEV,) + tuple(shape_local))
    t = jnp.moveaxis(t, 0, nd - 1)
    return t.reshape(tuple(shape_local[:-1]) + (N_DEV * shape_local[-1],))


def _block_diag(w):
    eye = jnp.eye(8, dtype=w.dtype)
    return (w[:, :, :, None, :] * eye[None, :, None, :, None]).reshape(2, RG_W, RG_W)


def _block_diag_extract(dw):
    t = dw.reshape(2, 8, 64, 8, 64)
    return jnp.moveaxis(jnp.diagonal(t, axis1=1, axis2=3), -1, 1)


def kernel(x, norm_mix, norm_mlp, norm_final, mlp_w1, mlp_w2, ab_w_in, ab_w_out, rg_conv_w, rg_conv_b, rg_w_a, rg_b_a, rg_w_x, rg_b_x, rg_lambda, hg_lb_logits, hg_norm, gla_w_in, gla_w_out, gla_w_gate_up, gla_b_gate, gla_norm, loss_target, m_norm_mix, m_norm_mlp, m_norm_final, m_mlp_w1, m_mlp_w2, m_ab_w_in, m_ab_w_out, m_rg_conv_w, m_rg_conv_b, m_rg_w_a, m_rg_b_a, m_rg_w_x, m_rg_b_x, m_rg_lambda, m_hg_lb_logits, m_hg_norm, m_gla_w_in, m_gla_w_out, m_gla_w_gate_up, m_gla_b_gate, m_gla_norm, v_norm_mix, v_norm_mlp, v_norm_final, v_mlp_w1, v_mlp_w2, v_ab_w_in, v_ab_w_out, v_rg_conv_w, v_rg_conv_b, v_rg_w_a, v_rg_b_a, v_rg_w_x, v_rg_b_x, v_rg_lambda, v_hg_lb_logits, v_hg_norm, v_gla_w_in, v_gla_w_out, v_gla_w_gate_up, v_gla_b_gate, v_gla_norm):
    w_loc = dict(norm_mix=norm_mix, norm_mlp=norm_mlp, norm_final=norm_final, mlp_w1=mlp_w1, mlp_w2=mlp_w2,
                 ab_w_in=ab_w_in, ab_w_out=ab_w_out, rg_conv_w=rg_conv_w, rg_conv_b=rg_conv_b, rg_w_a=rg_w_a,
                 rg_b_a=rg_b_a, rg_w_x=rg_w_x, rg_b_x=rg_b_x, rg_lambda=rg_lambda, hg_lb_logits=hg_lb_logits,
                 hg_norm=hg_norm, gla_w_in=gla_w_in, gla_w_out=gla_w_out, gla_w_gate_up=gla_w_gate_up,
                 gla_b_gate=gla_b_gate, gla_norm=gla_norm)
    m_loc = dict(norm_mix=m_norm_mix, norm_mlp=m_norm_mlp, norm_final=m_norm_final, mlp_w1=m_mlp_w1,
                 mlp_w2=m_mlp_w2, ab_w_in=m_ab_w_in, ab_w_out=m_ab_w_out, rg_conv_w=m_rg_conv_w,
                 rg_conv_b=m_rg_conv_b, rg_w_a=m_rg_w_a, rg_b_a=m_rg_b_a, rg_w_x=m_rg_w_x, rg_b_x=m_rg_b_x,
                 rg_lambda=m_rg_lambda, hg_lb_logits=m_hg_lb_logits, hg_norm=m_hg_norm, gla_w_in=m_gla_w_in,
                 gla_w_out=m_gla_w_out, gla_w_gate_up=m_gla_w_gate_up, gla_b_gate=m_gla_b_gate,
                 gla_norm=m_gla_norm)
    v_loc = dict(norm_mix=v_norm_mix, norm_mlp=v_norm_mlp, norm_final=v_norm_final, mlp_w1=v_mlp_w1,
                 mlp_w2=v_mlp_w2, ab_w_in=v_ab_w_in, ab_w_out=v_ab_w_out, rg_conv_w=v_rg_conv_w,
                 rg_conv_b=v_rg_conv_b, rg_w_a=v_rg_w_a, rg_b_a=v_rg_b_a, rg_w_x=v_rg_w_x, rg_b_x=v_rg_b_x,
                 rg_lambda=v_rg_lambda, hg_lb_logits=v_hg_lb_logits, hg_norm=v_hg_norm, gla_w_in=v_gla_w_in,
                 gla_w_out=v_gla_w_out, gla_w_gate_up=v_gla_w_gate_up, gla_b_gate=v_gla_b_gate,
                 gla_norm=v_gla_norm)

    T = x.shape[1]
    h0 = x.reshape(T, D_MODEL)
    target = loss_target.reshape(T, D_MODEL)
    ax, ay, ac = lax.axis_index("x"), lax.axis_index("y"), lax.axis_index("c")
    dev = 4 * ax + 2 * ay + ac
    place = jnp.stack([ac, 2 * ax + ay]).astype(jnp.int32)

    abin_shard = ab_w_in[0].astype(BF16)
    first_started = _copies_start(_plan_gather_first, 4, [abin_shard], _landing(N_DEV, [abin_shard[None]]),
                                  "ag_first_start")
    rest_shards = [mlp_w1[0].astype(BF16), mlp_w2[0].astype(BF16), gla_w_in[0].astype(BF16),
                   gla_w_out[0].astype(BF16), mlp_w1[1].astype(BF16), mlp_w2[1].astype(BF16),
                   _after(first_started[4], ab_w_out[0].astype(BF16))]
    ag_started = _copies_start(_plan_gather_first, 4 * len(rest_shards), rest_shards,
                               _landing(N_DEV, [s[None] for s in rest_shards]), "ag_rest_start")

    small_local = [w_loc[n] for n in SMALL_SHARDED]
    small_g = _allgather_vmem(_pack_rows(small_local, 8), "ag_small")
    small_g = small_g.reshape(N_DEV, -1, LANES)
    full = {}
    off = 0
    for n, a in zip(SMALL_SHARDED, small_local):
        full[n] = _unshard_last(small_g[:, off:off + a.size // LANES].reshape(N_DEV, a.size), a.shape)
        off += _part_rows(a)
    conv_w = full["rg_conv_w"][0]
    b_a, b_x, lam = full["rg_b_a"][0], full["rg_b_x"][0], full["rg_lambda"][0]
    w_up, b_gate, g_norm = full["gla_w_gate_up"][0], full["gla_b_gate"][0], full["gla_norm"]

    cw8 = jnp.pad(conv_w, ((0, 4), (0, 0)))
    wbd = jnp.concatenate([_block_diag(rg_w_a[0]), _block_diag(rg_w_x[0])], axis=2).astype(BF16)
    rg_bias = jnp.concatenate([b_a, b_x], axis=1).reshape(2, 1, 2 * RG_W)
    lam3 = lam.reshape(2, 1, RG_W)
    l0, l1 = hg_lb_logits[0:1], hg_lb_logits[1:2]
    wup_pad = jnp.zeros((2, LANES, 512), F32).at[0, 0:16].set(w_up[0]).at[1, 16:32].set(w_up[1])
    bg3 = b_gate.reshape(2, 1, 512)
    nmix0, nmix1 = norm_mix[0:1], norm_mix[1:2]
    nmlp0, nmlp1 = norm_mlp[0:1], norm_mlp[1:2]
    nfin = norm_final.reshape(1, D_MODEL)

    prepared = (ag_started[4] + cw8[:, 0:LANES] + wup_pad[0, 0:SUBLANES, 0:LANES] + rg_bias[0, :, 0:LANES]
                + wbd[0, 0:SUBLANES, 0:LANES].astype(F32) + lam3[0, :, 0:LANES] + bg3[0, :, 0:LANES])
    (abin_shard,), abin_l = _copies_wait(_plan_gather_first, first_started, prepared, "ag_first_wait")
    first_pass = _copies_start(_plan_gather_pass, 3, [], abin_l, "ag_first_pass_start")
    _, (abin_g,) = _copies_wait(_plan_gather_pass, first_pass, first_pass[4], "ag_first_pass_wait")
    abin_g = lax.dynamic_update_index_in_dim(abin_g, abin_shard, dev, 0)
    wab_in = jnp.transpose(abin_g, (1, 0, 2)).reshape(D_MODEL, AB_IN)
    proj0, y0 = _norm_matmul(h0, _after(ag_started[4], nmix0), wab_in, "l0_in_proj")
    xc = _rg_conv_fwd(proj0, cw8, rg_conv_b, "rg_conv")
    hs = _rg_scan_fwd(xc, wbd, rg_bias, lam3, "rg_scan")
    o_hg, s_hg = _hg_fwd(proj0, l0, l1, "hg_chunks")
    both_done = hs[0][0:SUBLANES, 0:LANES] + o_hg[0][0:SUBLANES, 0:LANES]
    rest_shards, rest_lands = _copies_wait(_plan_gather_first, ag_started, both_done, "ag_rest_wait")
    pass_started = _copies_start(_plan_gather_pass, 3 * len(rest_lands), [], rest_lands, "ag_pass_start")
    mixin0 = _l0_combine_fwd(hs, proj0, o_hg, _after(pass_started[4], hg_norm), "l0_combine")
    _, rest_g = _copies_wait(_plan_gather_pass, pass_started, mixin0, "ag_pass_wait")
    rest_g = [lax.dynamic_update_index_in_dim(g, s, dev, 0) for g, s in zip(rest_g, rest_shards)]
    wab_out = rest_g[6].reshape(D_MODEL, D_MODEL)
    h1 = _matmul_res(mixin0, wab_out, h0, "l0_out_proj")
    w1g = (rest_g[0], rest_g[4])
    w2f = (rest_g[1].reshape(D_FF, D_MODEL), rest_g[5].reshape(D_FF, D_MODEL))
    wgla_in = jnp.pad(jnp.transpose(rest_g[2], (1, 0, 2)).reshape(D_MODEL, GLA_IN),
                      ((0, 0), (0, GLA_IN_PAD - GLA_IN)))
    wgla_out = rest_g[3].reshape(D_MODEL, D_MODEL)
    h2, pre0, ym0 = _mlp_fwd(h1, nmlp0, w1g[0], w2f[0], "mlp0")
    proj1, y1 = _norm_matmul(h2, nmix1, wgla_in, "l1_in_proj")
    z_gate, lr_b = _gate_logits(proj1, wup_pad, bg3, "gla_gate_logits")
    o_gla, s_gla = _gla_fwd(proj1, z_gate, "gla_chunks")
    mixin1 = _l1_combine_fwd(o_gla, proj1, g_norm, "l1_combine")
    h3 = _matmul_res(mixin1, wgla_out, h2, "l1_out_proj")
    h4, pre1, ym1 = _mlp_fwd(h3, nmlp1, w1g[1], w2f[1], "mlp1")
    loss_blk, dh4, dh4b, d_nfin = _final_loss(h4, nfin, target, "final_loss")

    dh3, dh3b, dpre1, act1, d_nmlp1 = _mlp_bwd(dh4, dh4b, h3, nmlp1, pre1, w1g[1], w2f[1], "mlp1_bwd")
    g_w1_1 = _wgrad(ym1, dpre1, 512, "mlp1_dw1", sharded_cols=True)
    g_w2_1 = _wgrad(act1, dh4b, 512, "mlp1_dw2")
    g_gla_out = _wgrad(mixin1, dh3b, 512, "l1_out_dw")
    do_gla, dr, d_gnorm = _l1_combine_bwd(o_gla, proj1, g_norm, dh3b, wgla_out, "l1_combine_bwd")
    dq1, dk1, dv1, dz_gate = _gla_bwd(proj1, z_gate, s_gla, do_gla, "gla_chunks_bwd")
    dlr1, d_bg, dz_b = _gate_logits_bwd(dz_gate, wup_pad, "gla_gate_logits_bwd")
    d_wup = [_wgrad(lr_b, dz_b[d], 512, "gla_gate_dw%d" % d) for d in range(2)]
    dproj1 = _l1_assemble(dq1, dk1, dv1, dr, dlr1, "l1_assemble")
    dh2, dh2b, d_nmix1 = _dgrad_norm(dproj1, wgla_in, h2, nmix1, dh3, "l1_in_dgrad")
    g_gla_in = _wgrad(y1, dproj1, 640, "l1_in_dw")

    def reduce_start(grads, tag):
        return _copies_start(_plan_grads_sibling, 4 * len(grads), grads, _landing(4, grads), "rs_%s_d2d_start" % tag)

    def reduce_mid(started, after, tag):
        grads, got = _copies_wait(_plan_grads_sibling, started, after, "rs_%s_d2d_wait" % tag)
        parts = [_chip_partial(g, r, place, "rs_%s_partial%d" % (tag, a)) for a, (g, r) in enumerate(zip(grads, got))]
        pb = [p[0] for p in parts]
        return _copies_start(_plan_grads_chips, 3 * len(pb), pb, _landing(3, pb), "rs_%s_ici_start" % tag), \
            [p[1] for p in parts]

    def reduce_end(started, mine, after, tag):
        _, got = _copies_wait(_plan_grads_chips, started, after, "rs_%s_ici_wait" % tag)
        return list(zip(mine, got))

    slots_l1 = [g_w1_1, g_w2_1.reshape(N_DEV, 512, D_MODEL),
                jnp.transpose(g_gla_in[:, :GLA_IN].reshape(D_MODEL, N_DEV, GLA_IN // N_DEV), (1, 0, 2)),
                g_gla_out.reshape(N_DEV, 128, D_MODEL)]
    ra_d2d = reduce_start(slots_l1, "l1")

    dh1, dh1b, dpre0, act0, d_nmlp0 = _mlp_bwd(dh2, dh2b, h1, _after(ra_d2d[4], nmlp0), pre0, w1g[0], w2f[0],
                                               "mlp0_bwd")
    g_w1_0 = _wgrad(ym0, dpre0, 512, "mlp0_dw1", sharded_cols=True)
    g_w2_0 = _wgrad(act0, dh2b, 512, "mlp0_dw2")
    ra_ici, ra_mine = reduce_mid(ra_d2d, g_w2_0, "l1")
    g_ab_out = _wgrad(mixin0, dh1b, 512, "l0_out_dw")
    rb_d2d = reduce_start([g_w1_0, g_w2_0.reshape(N_DEV, 512, D_MODEL), g_ab_out.reshape(N_DEV, 128, D_MODEL)],
                          "mlp0")
    dho, dga, do_hg, dg_gate, d_hgnorm = _l0_combine_bwd(
        hs, proj0, o_hg, _after(rb_d2d[4], _after(ra_ici[4], hg_norm)), dh1b, wab_out, "l0_combine_bwd")
    dxc, d_wbd, d_rgb, d_lam = _rg_scan_bwd(xc, wbd, rg_bias, lam3, hs, dho, "rg_scan_bwd")
    dxa, d_cw8, d_cb = _rg_conv_bwd(dxc, proj0, cw8, "rg_conv_bwd")
    dq0, df0, dv0, d_l0, d_l1 = _hg_bwd(proj0, l0, l1, s_hg, do_hg, "hg_chunks_bwd")
    rb_ici, rb_mine = reduce_mid(rb_d2d, d_l0, "mlp0")
    dproj0 = _l0_assemble(dxa, dga, dq0, df0, dv0, dg_gate, "l0_assemble")
    dx, _, d_nmix0 = _dgrad_norm(dproj0, wab_in, h0, _after(rb_ici[4], nmix0), dh1, "l0_in_dgrad")

    d_wa = _block_diag_extract(d_wbd[:, :, :RG_W])[None]
    d_wx = _block_diag_extract(d_wbd[:, :, RG_W:])[None]
    small_full = {
        "norm_mix": jnp.concatenate([d_nmix0, d_nmix1], axis=0), "norm_mlp": jnp.concatenate([d_nmlp0, d_nmlp1], axis=0),
        "norm_final": d_nfin.reshape(D_MODEL), "rg_conv_b": d_cb, "rg_w_a": d_wa, "rg_w_x": d_wx,
        "hg_lb_logits": jnp.concatenate([d_l0[0] + d_l0[1], d_l1[0] + d_l1[1]], axis=0), "hg_norm": d_hgnorm,
        "rg_conv_w": d_cw8[0:4][None], "rg_b_a": d_rgb[:, 0, :RG_W][None], "rg_b_x": d_rgb[:, 0, RG_W:][None],
        "rg_lambda": d_lam[:, 0, :][None],
        "gla_w_gate_up": jnp.stack([d_wup[0][0:16], d_wup[1][16:32]])[None], "gla_b_gate": d_bg[:, 0, :][None],
        "gla_norm": d_gnorm}
    small_names = SMALL_REPLICATED + SMALL_SHARDED
    packed = _pack_rows([loss_blk] + [small_full[n] for n in small_names], 256)
    ar_first = _copies_start(_plan_gather_first, 4, [packed], _landing(N_DEV, [packed[None]]), "ar_small_start")

    g_ab_in = _wgrad(y0, dproj0, 512, "l0_in_dw", behind=ar_first[4])
    rc_d2d = reduce_start([jnp.transpose(g_ab_in.reshape(D_MODEL, N_DEV, AB_IN // N_DEV), (1, 0, 2))], "ab")
    (packed,), ar_lands = _copies_wait(_plan_gather_first, ar_first, rc_d2d[4], "ar_small_wait")
    ar_pass = _copies_start(_plan_gather_pass, 3, [], ar_lands, "ar_small_pass_start")
    rc_ici, rc_mine = reduce_mid(rc_d2d, ar_pass[4], "ab")
    _, (ar_gathered,) = _copies_wait(_plan_gather_pass, ar_pass, rc_ici[4], "ar_small_pass_wait")
    summed = _sum_slots(lax.dynamic_update_index_in_dim(ar_gathered, packed, dev, 0), "ar_small_sum")
    loss = summed[0, 0]

    pieces_l1 = reduce_end(ra_ici, ra_mine, rc_ici[4], "l1")
    res_gla_in = _adamw(gla_w_in, pieces_l1[2], m_gla_w_in, v_gla_w_in, "adamw_gla_in")
    res_gla_out = _adamw(gla_w_out, pieces_l1[3], m_gla_w_out, v_gla_w_out, "adamw_gla_out")
    pieces_mlp0 = reduce_end(rb_ici, rb_mine, res_gla_out[0], "mlp0")
    res_w1 = _adamw_layers(mlp_w1, (pieces_mlp0[0], pieces_l1[0]), m_mlp_w1, v_mlp_w1, "adamw_mlp_w1")
    res_w2 = _adamw_layers(mlp_w2, (pieces_mlp0[1], pieces_l1[1]), m_mlp_w2, v_mlp_w2, "adamw_mlp_w2")
    res = {"mlp_w1": tuple(res_w1), "mlp_w2": tuple(res_w2),
           "gla_w_in": tuple(res_gla_in), "gla_w_out": tuple(res_gla_out),
           "ab_w_out": tuple(_adamw(ab_w_out, pieces_mlp0[2], m_ab_w_out, v_ab_w_out, "adamw_ab_out"))}

    off = SUBLANES
    for n in small_names:
        a = small_full[n]
        gfull = summed[off:off + a.size // LANES].reshape(a.shape)
        off += _part_rows(a)
        local = w_loc[n].shape
        if n in SMALL_SHARDED:
            gfull = lax.dynamic_slice_in_dim(gfull, dev * local[-1], local[-1], axis=gfull.ndim - 1)
        flat = (-1, local[-1])
        outs = _adamw_whole(w_loc[n].reshape(flat), gfull.reshape(flat), m_loc[n].reshape(flat),
                            v_loc[n].reshape(flat), "adamw_" + n)
        res[n] = tuple(o.reshape(local) for o in outs)
    others_done = (res_w1[1][0, 0:SUBLANES, 0:LANES] + res_w2[1][0, 0:SUBLANES, 0:LANES]
                   + res_gla_in[1][0, 0:SUBLANES, 0:LANES])
    pieces_ab = reduce_end(rc_ici, rc_mine, others_done, "ab")
    res["ab_w_in"] = tuple(_adamw(ab_w_in, pieces_ab[0], m_ab_w_in, v_ab_w_in, "adamw_ab_in"))

    grad_x = dx.reshape(1, T, D_MODEL)
    out = [loss, grad_x]
    for k in range(4):
        out += [res[n][k] for n in WEIGHT_NAMES]
    return tuple(out)
```

```python
import jax
import jax.numpy as jnp
from jax import lax
from jax.experimental import pallas as pl
from jax.experimental.pallas import tpu as pltpu

F32, BF16 = jnp.float32, jnp.bfloat16
MESH = pl.DeviceIdType.MESH

D_MODEL = 1024
D_FF = 4096
RG_W = 512
HG_W = 512
CHUNK = 64
EPS = 1e-6
RG_C = 8.0
AB_IN = 3584
GLA_IN = 3104
GLA_IN_PAD = 3200
N_DEV = 8
LANES = 128
SUBLANES = 8
VMEM_LIMIT = 48 * 1024 * 1024

ADAM_LR, ADAM_B1, ADAM_B2, ADAM_EPS, ADAM_WD, ADAM_STEP = 0.001, 0.9, 0.999, 1e-08, 0.01, 10


def _params(*sem):
    return pltpu.CompilerParams(dimension_semantics=sem, vmem_limit_bytes=VMEM_LIMIT)


def _dg(a, b, ca, cb):
    return lax.dot_general(a.astype(BF16), b.astype(BF16), (((ca,), (cb,)), ((), ())),
                           preferred_element_type=F32)


@jax.custom_vjp
def _mm_nn(a, b):
    return _dg(a, b, 1, 0)


_mm_nn.defvjp(lambda a, b: (_dg(a, b, 1, 0), (a, b)),
              lambda res, g: (_dg(g, res[1], 1, 1), _dg(res[0], g, 0, 0)))


@jax.custom_vjp
def _mm_nt(a, b):
    return _dg(a, b, 1, 1)


_mm_nt.defvjp(lambda a, b: (_dg(a, b, 1, 1), (a, b)),
              lambda res, g: (_dg(g, res[1], 1, 0), _dg(g, res[0], 0, 0)))


@jax.custom_vjp
def _mm_tn(a, b):
    return _dg(a, b, 0, 0)


_mm_tn.defvjp(lambda a, b: (_dg(a, b, 0, 0), (a, b)),
              lambda res, g: (_dg(res[1], g, 1, 1), _dg(res[0], g, 1, 0)))


def _tri_dot(tri, x):
    hi = x.astype(BF16)
    lo = (x - hi.astype(F32)).astype(BF16)
    t = tri.astype(BF16)
    return jnp.dot(t, hi, preferred_element_type=F32) + jnp.dot(t, lo, preferred_element_type=F32)


@jax.custom_vjp
def _cum(tri, tri_t, x):
    return _tri_dot(tri, x)


_cum.defvjp(lambda tri, tri_t, x: (_tri_dot(tri, x), (tri, tri_t)),
            lambda res, g: (jnp.zeros_like(res[0]), jnp.zeros_like(res[1]), _tri_dot(res[1], g)))


@jax.custom_vjp
def _sig(x):
    return 1.0 / (1.0 + jnp.exp(-x))


_sig.defvjp(lambda x: (lambda s: (s, s))(1.0 / (1.0 + jnp.exp(-x))),
            lambda s, g: (g * s * (1.0 - s),))


def _gelu(x):
    return 0.5 * x * (1.0 + jnp.tanh(0.7978845608028654 * (x + 0.044715 * (x * x * x))))


def _softplus(z):
    return jnp.maximum(z, 0.0) + jnp.log(1.0 + jnp.exp(-jnp.abs(z)))


def _rms(x):
    return lax.rsqrt(jnp.mean(x * x, axis=-1, keepdims=True) + EPS)


def _rmsnorm_bwd(x, gain, dy):
    r = _rms(x)
    xh = x * r
    dgain = jnp.sum(dy * xh, axis=0, keepdims=True)
    dxh = dy * gain
    dx = r * (dxh - xh * jnp.mean(dxh * xh, axis=-1, keepdims=True))
    return dx, dgain


def _headnorm(o, gain, n_heads, hd):
    parts = []
    for h in range(n_heads):
        oh = o[:, h * hd:(h + 1) * hd]
        parts.append(oh * _rms(oh))
    return jnp.concatenate(parts, axis=1) * gain


def _tri_consts(d):
    row = lax.broadcasted_iota(jnp.int32, (CHUNK, CHUNK), 0)
    col = lax.broadcasted_iota(jnp.int32, (CHUNK, CHUNK), 1)
    ge = (row >= col).astype(F32)
    le = (row <= col).astype(F32)
    r1 = lax.broadcasted_iota(jnp.int32, (CHUNK, 1), 0)
    if d == 0:
        return ge, le, (r1 <= CHUNK // 2).astype(F32)
    return le, ge, (r1 >= CHUNK // 2 - 1).astype(F32)


def _chunk_core(qh, k, v, logf, st_prev, tri, tri_t, mref, n_heads, dk, dv):
    cum = _cum(tri, tri_t, logf)
    ref = jnp.sum(logf * mref, axis=0, keepdims=True)
    last = jnp.sum(logf, axis=0, keepdims=True)
    q_in = qh * jnp.exp(cum - ref)
    k_in = k * jnp.exp(ref - cum)
    k_st = k * jnp.exp(last - cum)
    q_dec = qh * jnp.exp(cum)
    decay = jnp.exp(last)
    outs, sts = [], []
    for h in range(n_heads):
        sk = slice(h * dk, (h + 1) * dk)
        sv = slice(h * dv, (h + 1) * dv)
        sc = _mm_nt(q_in[:, sk], k_in[:, sk]) * tri
        o = _mm_nn(sc, v[:, sv]) + _mm_nt(q_dec[:, sk], st_prev[h])
        sts.append(st_prev[h] * decay[:, sk] + _mm_tn(v[:, sv], k_st[:, sk]))
        outs.append(o)
    return jnp.concatenate(outs, axis=1), tuple(sts)


def _hg_chunk(q, f, v, l0, l1, st_prev, tri, tri_t, mref):
    lb = _sig(l0 - l1)
    sg = _sig(f)
    qh = q * _sig(q)
    logf = jnp.log(lb + (1.0 - lb) * sg)
    k = (1.0 - lb) * (1.0 - sg)
    return _chunk_core(qh, k, v, logf, st_prev, tri, tri_t, mref, 4, 128, 128)


def _gla_chunk(q, k, v, z, st_prev, tri, tri_t, mref):
    logf = (jnp.minimum(z, 0.0) - jnp.log(1.0 + jnp.exp(-jnp.abs(z)))) * (1.0 / 16.0)
    qh = q * (128.0 ** -0.5)
    return _chunk_core(qh, k, v, logf, st_prev, tri, tri_t, mref, 4, 128, 256)


def _rg_gates(xc, wbd, bias, lam):
    z = _mm_nn(xc, wbd) + bias
    r = _sig(z[:, :RG_W])
    i = _sig(z[:, RG_W:])
    log_a = -RG_C * r * _softplus(-lam)
    a = jnp.exp(log_a)
    x2 = 2.0 * log_a
    neg_expm1 = jnp.where(x2 > -1e-2, -(x2 + 0.5 * x2 * x2 + x2 * x2 * x2 * (1.0 / 6.0)), 1.0 - jnp.exp(x2))
    u = jnp.sqrt(neg_expm1) * (i * xc)
    return a, u


def _l0_combine(hf, hb, ga, of, ob, g, gain):
    ya = (hf + hb) * _gelu(ga)
    yb = _headnorm(of + ob, gain, 4, 128) * (g * _sig(g))
    return jnp.concatenate([ya, yb], axis=1)


def _l1_combine(of, ob, r, gain):
    return _headnorm(of + ob, gain, 4, 256) * (r * _sig(r))


def _norm_matmul(h, gain, w, name):
    T, D = h.shape
    N = w.shape[1]
    tm = min(512, T)

    def body(h_ref, g_ref, w_ref, o_ref, y_ref):
        x = h_ref[...]
        y = (x * _rms(x) * g_ref[...]).astype(BF16)
        y_ref[...] = y
        o_ref[...] = jnp.dot(y, w_ref[...], preferred_element_type=F32)

    return pl.pallas_call(
        body, name=name, grid=(T // tm,),
        in_specs=[pl.BlockSpec((tm, D), lambda i: (i, 0)), pl.BlockSpec((1, D), lambda i: (0, 0)),
                  pl.BlockSpec((D, N), lambda i: (0, 0))],
        out_specs=[pl.BlockSpec((tm, N), lambda i: (i, 0)), pl.BlockSpec((tm, D), lambda i: (i, 0))],
        out_shape=[jax.ShapeDtypeStruct((T, N), F32), jax.ShapeDtypeStruct((T, D), BF16)],
        compiler_params=_params("parallel"))(h, gain, w)


def _matmul_res(a, w, res, name):
    T, K = a.shape
    N = w.shape[1]
    tm = min(512, T)

    def body(a_ref, w_ref, r_ref, o_ref):
        o_ref[...] = r_ref[...] + jnp.dot(a_ref[...], w_ref[...], preferred_element_type=F32)

    return pl.pallas_call(
        body, name=name, grid=(T // tm,),
        in_specs=[pl.BlockSpec((tm, K), lambda i: (i, 0)), pl.BlockSpec((K, N), lambda i: (0, 0)),
                  pl.BlockSpec((tm, N), lambda i: (i, 0))],
        out_specs=pl.BlockSpec((tm, N), lambda i: (i, 0)),
        out_shape=jax.ShapeDtypeStruct((T, N), F32),
        compiler_params=_params("parallel"))(a, w, res)


def _dgrad_norm(dproj, w, h, gain, dres, name):
    T, N = dproj.shape
    D = w.shape[0]
    tm = min(512, T)

    def body(dp_ref, w_ref, h_ref, g_ref, dr_ref, dh_ref, dhb_ref, dg_ref):
        @pl.when(pl.program_id(0) == 0)
        def _():
            dg_ref[...] = jnp.zeros_like(dg_ref)

        dy = _dg(dp_ref[...], w_ref[...], 1, 1)
        dx, dgain = _rmsnorm_bwd(h_ref[...], g_ref[...], dy)
        dh = dr_ref[...] + dx
        dh_ref[...] = dh
        dhb_ref[...] = dh.astype(BF16)
        dg_ref[...] += dgain

    return pl.pallas_call(
        body, name=name, grid=(T // tm,),
        in_specs=[pl.BlockSpec((tm, N), lambda i: (i, 0)), pl.BlockSpec((D, N), lambda i: (0, 0)),
                  pl.BlockSpec((tm, D), lambda i: (i, 0)), pl.BlockSpec((1, D), lambda i: (0, 0)),
                  pl.BlockSpec((tm, D), lambda i: (i, 0))],
        out_specs=[pl.BlockSpec((tm, D), lambda i: (i, 0)), pl.BlockSpec((tm, D), lambda i: (i, 0)),
                   pl.BlockSpec((1, D), lambda i: (0, 0))],
        out_shape=[jax.ShapeDtypeStruct((T, D), F32), jax.ShapeDtypeStruct((T, D), BF16),
                   jax.ShapeDtypeStruct((1, D), F32)],
        compiler_params=_params("arbitrary"))(dproj, w, h, gain, dres)


def _wgrad(a, b, tn, name, sharded_cols=False, behind=None):
    T, K = a.shape
    N = b.shape[1]
    tk = min(1024, K)

    def body(a_ref, b_ref, *rest):
        o_ref, at_ref = rest[-2], rest[-1]

        @pl.when(pl.program_id(1) == 0)
        def _():
            at_ref[...] = a_ref[...].astype(BF16).T

        o_ref[...] = _dg(at_ref[...], b_ref[...], 1, 0)

    if sharded_cols:
        out_spec = pl.BlockSpec((None, tk, tn), lambda k, n: (n, k, 0))
        out_shape = jax.ShapeDtypeStruct((N // tn, K, tn), F32)
    else:
        out_spec = pl.BlockSpec((tk, tn), lambda k, n: (k, n))
        out_shape = jax.ShapeDtypeStruct((K, N), F32)
    in_specs = [pl.BlockSpec((T, tk), lambda k, n: (0, k)), pl.BlockSpec((T, tn), lambda k, n: (0, n))]
    args = [a, b]
    if behind is not None:
        in_specs.append(pl.BlockSpec((SUBLANES, LANES), lambda k, n: (0, 0)))
        args.append(behind)
    return pl.pallas_call(
        body, name=name, grid=(K // tk, N // tn), in_specs=in_specs, out_specs=out_spec, out_shape=out_shape,
        scratch_shapes=[pltpu.VMEM((tk, T), BF16)],
        compiler_params=_params("parallel", "arbitrary"))(*args)


def _resident(shape):
    return pl.BlockSpec(shape, lambda i: (0,) * len(shape), pipeline_mode=pl.Buffered(1))


def _mlp_fwd(h, gain, w1g, w2, name):
    T, D = h.shape
    nf, _, tf = w1g.shape
    tm = min(512, T)

    def body(h_ref, g_ref, w1_ref, w2_ref, o_ref, pre_ref, y_ref):
        x = h_ref[...]
        y = (x * _rms(x) * g_ref[...]).astype(BF16)
        y_ref[...] = y
        acc = x
        for j in range(nf):
            cols = slice(j * tf, (j + 1) * tf)
            pre = jnp.dot(y, w1_ref[j], preferred_element_type=F32)
            pre_ref[:, cols] = pre.astype(BF16)
            act = jnp.square(jnp.maximum(pre, 0.0)).astype(BF16)
            acc = acc + jnp.dot(act, w2_ref[cols, :], preferred_element_type=F32)
        o_ref[...] = acc

    return pl.pallas_call(
        body, name=name, grid=(T // tm,),
        in_specs=[pl.BlockSpec((tm, D), lambda i: (i, 0)), pl.BlockSpec((1, D), lambda i: (0, 0)),
                  _resident(w1g.shape), _resident(w2.shape)],
        out_specs=[pl.BlockSpec((tm, D), lambda i: (i, 0)), pl.BlockSpec((tm, nf * tf), lambda i: (i, 0)),
                   pl.BlockSpec((tm, D), lambda i: (i, 0))],
        out_shape=[jax.ShapeDtypeStruct((T, D), F32), jax.ShapeDtypeStruct((T, nf * tf), BF16),
                   jax.ShapeDtypeStruct((T, D), BF16)],
        compiler_params=_params("parallel"))(h, gain, w1g, w2)


def _mlp_bwd(dout, dout_b, h, gain, pre, w1g, w2, name):
    T, D = h.shape
    nf, _, tf = w1g.shape
    tm = min(256, T)

    def body(do_ref, dob_ref, h_ref, g_ref, pre_ref, w1_ref, w2_ref, dh_ref, dhb_ref, dpre_ref, act_ref, dg_ref):
        @pl.when(pl.program_id(0) == 0)
        def _():
            dg_ref[...] = jnp.zeros_like(dg_ref)

        dob = dob_ref[...]
        dy = None
        for j in range(nf):
            cols = slice(j * tf, (j + 1) * tf)
            rp = jnp.maximum(pre_ref[:, cols].astype(F32), 0.0)
            dpre = (_dg(dob, w2_ref[cols, :], 1, 1) * (2.0 * rp)).astype(BF16)
            dpre_ref[:, cols] = dpre
            act_ref[:, cols] = (rp * rp).astype(BF16)
            part = _dg(dpre, w1_ref[j], 1, 1)
            dy = part if dy is None else dy + part
        dx, dgain = _rmsnorm_bwd(h_ref[...], g_ref[...], dy)
        dh = do_ref[...] + dx
        dh_ref[...] = dh
        dhb_ref[...] = dh.astype(BF16)
        dg_ref[...] += dgain

    tok = lambda w: pl.BlockSpec((tm, w), lambda i: (i, 0))
    return pl.pallas_call(
        body, name=name, grid=(T // tm,),
        in_specs=[tok(D), tok(D), tok(D), pl.BlockSpec((1, D), lambda i: (0, 0)), tok(nf * tf),
                  _resident(w1g.shape), _resident(w2.shape)],
        out_specs=[tok(D), tok(D), tok(nf * tf), tok(nf * tf), pl.BlockSpec((1, D), lambda i: (0, 0))],
        out_shape=[jax.ShapeDtypeStruct((T, D), F32), jax.ShapeDtypeStruct((T, D), BF16),
                   jax.ShapeDtypeStruct((T, nf * tf), BF16),
                   jax.ShapeDtypeStruct((T, nf * tf), BF16), jax.ShapeDtypeStruct((1, D), F32)],
        compiler_params=_params("arbitrary"))(dout, dout_b, h, gain, pre, w1g, w2)


def _final_loss(h, gain, target, name):
    T, D = h.shape
    tm = min(512, T)

    def body(h_ref, g_ref, t_ref, l_ref, dh_ref, dhb_ref, dg_ref):
        @pl.when(pl.program_id(0) == 0)
        def _():
            l_ref[...] = jnp.zeros_like(l_ref)
            dg_ref[...] = jnp.zeros_like(dg_ref)

        x = h_ref[...]
        err = x * _rms(x) * g_ref[...] - t_ref[...]
        l_ref[...] += 0.5 * jnp.sum(jnp.mean(err * err, axis=-1, keepdims=True), axis=0, keepdims=True)
        dx, dgain = _rmsnorm_bwd(x, g_ref[...], err * (1.0 / D))
        dh_ref[...] = dx
        dhb_ref[...] = dx.astype(BF16)
        dg_ref[...] += dgain

    return pl.pallas_call(
        body, name=name, grid=(T // tm,),
        in_specs=[pl.BlockSpec((tm, D), lambda i: (i, 0)), pl.BlockSpec((1, D), lambda i: (0, 0)),
                  pl.BlockSpec((tm, D), lambda i: (i, 0))],
        out_specs=[pl.BlockSpec((SUBLANES, LANES), lambda i: (0, 0)), pl.BlockSpec((tm, D), lambda i: (i, 0)),
                   pl.BlockSpec((tm, D), lambda i: (i, 0)), pl.BlockSpec((1, D), lambda i: (0, 0))],
        out_shape=[jax.ShapeDtypeStruct((SUBLANES, LANES), F32), jax.ShapeDtypeStruct((T, D), F32),
                   jax.ShapeDtypeStruct((T, D), BF16), jax.ShapeDtypeStruct((1, D), F32)],
        compiler_params=_params("arbitrary"))(h, gain, target)


def _halo_specs(tm, T, width, col, tile=lambda i: i):
    r8 = tm // SUBLANES
    nb8 = T // SUBLANES
    return [pl.BlockSpec((tm, width), lambda i: (tile(i), col)),
            pl.BlockSpec((SUBLANES, width), lambda i: (jnp.maximum(tile(i) * r8 - 1, 0), col)),
            pl.BlockSpec((SUBLANES, width), lambda i: (jnp.minimum((tile(i) + 1) * r8, nb8 - 1), col))]


def _ext(cur, prev, nxt, has_prev, has_next):
    return jnp.concatenate([jnp.where(has_prev, prev, 0.0), cur, jnp.where(has_next, nxt, 0.0)], axis=0)


def _shifted(ext, offset, tm):
    n = ext.shape[0]
    sh = (-offset) % n
    r = ext if sh == 0 else pltpu.roll(ext, sh, 0)
    return r[SUBLANES:SUBLANES + tm]


def _rg_conv_fwd(proj, cw8, cb, name):
    T = proj.shape[0]
    tm = min(512, T)
    nT = T // tm

    def body(cur_ref, prev_ref, next_ref, w_ref, b_ref, o_ref):
        i = pl.program_id(0)
        ext = _ext(cur_ref[...], prev_ref[...], next_ref[...], i > 0, i < nT - 1)
        acc = jnp.broadcast_to(b_ref[...], (tm, RG_W))
        for k in range(4):
            acc = acc + w_ref[k:k + 1, :] * _shifted(ext, k - 2, tm)
        o_ref[...] = acc

    return pl.pallas_call(
        body, name=name, grid=(nT,),
        in_specs=_halo_specs(tm, T, RG_W, 0) + [pl.BlockSpec((SUBLANES, RG_W), lambda i: (0, 0)),
                                                pl.BlockSpec((1, RG_W), lambda i: (0, 0))],
        out_specs=pl.BlockSpec((tm, RG_W), lambda i: (i, 0)),
        out_shape=jax.ShapeDtypeStruct((T, RG_W), F32),
        compiler_params=_params("parallel"))(proj, proj, proj, cw8, cb)


def _rg_conv_bwd(dxc, proj, cw8, name):
    T = proj.shape[0]
    tm = min(512, T)
    nT = T // tm

    def body(a0, p0, n0, a1, p1, n1, xa, xp, xn, w_ref, dxa_ref, dw_ref, db_ref):
        i = pl.program_id(0)

        @pl.when(i == 0)
        def _():
            dw_ref[...] = jnp.zeros_like(dw_ref)
            db_ref[...] = jnp.zeros_like(db_ref)

        has_p, has_n = i > 0, i < nT - 1
        cur = a0[...] + a1[...]
        dext = _ext(cur, p0[...] + p1[...], n0[...] + n1[...], has_p, has_n)
        xext = _ext(xa[...], xp[...], xn[...], has_p, has_n)
        acc = jnp.zeros((tm, RG_W), F32)
        rows = []
        for k in range(4):
            acc = acc + w_ref[k:k + 1, :] * _shifted(dext, 2 - k, tm)
            rows.append(jnp.sum(cur * _shifted(xext, k - 2, tm), axis=0, keepdims=True))
        dxa_ref[...] = acc
        dw_ref[...] += jnp.concatenate(rows + [jnp.zeros((4, RG_W), F32)], axis=0)
        db_ref[...] += jnp.sum(cur, axis=0, keepdims=True)

    return pl.pallas_call(
        body, name=name, grid=(nT,),
        in_specs=(_halo_specs(tm, T, RG_W, 0) + _halo_specs(tm, T, RG_W, 0)
                  + _halo_specs(tm, T, RG_W, 0) + [pl.BlockSpec((SUBLANES, RG_W), lambda i: (0, 0))]),
        out_specs=[pl.BlockSpec((tm, RG_W), lambda i: (i, 0)), pl.BlockSpec((SUBLANES, RG_W), lambda i: (0, 0)),
                   pl.BlockSpec((1, RG_W), lambda i: (0, 0))],
        out_shape=[jax.ShapeDtypeStruct((T, RG_W), F32), jax.ShapeDtypeStruct((SUBLANES, RG_W), F32),
                   jax.ShapeDtypeStruct((1, RG_W), F32)],
        compiler_params=_params("arbitrary"))(dxc[0], dxc[0], dxc[0], dxc[1], dxc[1], dxc[1], proj, proj, proj, cw8)


def _local_scan(a, b, ascending):
    n = a.shape[0]
    pos = jnp.bitwise_and(lax.broadcasted_iota(jnp.int32, a.shape, 0), SUBLANES - 1)
    for s in (1, 2, 4):
        sh = s if ascending else n - s
        ok = (pos >= s) if ascending else (pos < SUBLANES - s)
        a_sh, b_sh = pltpu.roll(a, sh, 0), pltpu.roll(b, sh, 0)
        b = jnp.where(ok, a * b_sh + b, b)
        a = jnp.where(ok, a * a_sh, a)
    return a, b


def _group_scan(chains, a_sc, b_sc, carry, n_groups):
    def step(g, hs):
        new = []
        for (d, out_ref, asc), h in zip(chains, hs):
            r0 = pl.multiple_of((g if asc else n_groups - 1 - g) * SUBLANES, SUBLANES)
            out_ref[pl.ds(r0, SUBLANES), :] = a_sc[d, pl.ds(r0, SUBLANES), :] * h + b_sc[d, pl.ds(r0, SUBLANES), :]
            new.append(out_ref[pl.ds(r0 + (SUBLANES - 1 if asc else 0), 1), :])
        return tuple(new)

    hs = lax.fori_loop(0, n_groups, step, tuple(carry[d, 0:1, :] for d, _, _ in chains))
    for (d, _, _), h in zip(chains, hs):
        carry[d, 0:1, :] = h


def _rg_scan_fwd(xc, wbd, bias, lam, name):
    T = xc.shape[0]
    tm = min(512, T)
    nT = T // tm

    def body(xf_ref, xb_ref, w_ref, b_ref, lam_ref, hf_ref, hb_ref, a_sc, b_sc, carry):
        @pl.when(pl.program_id(0) == 0)
        def _():
            carry[...] = jnp.zeros_like(carry)

        for d, x_ref in enumerate((xf_ref, xb_ref)):
            a, u = _rg_gates(x_ref[...], w_ref[d], b_ref[d], lam_ref[d])
            a_sc[d], b_sc[d] = _local_scan(a, u, d == 0)
        _group_scan(((0, hf_ref, True), (1, hb_ref, False)), a_sc, b_sc, carry, tm // SUBLANES)

    full = lambda a: pl.BlockSpec(a.shape, lambda i: (0,) * len(a.shape))
    res = pl.pallas_call(
        body, name=name, grid=(nT,),
        in_specs=[pl.BlockSpec((tm, RG_W), lambda i: (i, 0)), pl.BlockSpec((tm, RG_W), lambda i: (nT - 1 - i, 0)),
                  full(wbd), full(bias), full(lam)],
        out_specs=[pl.BlockSpec((tm, RG_W), lambda i: (i, 0)), pl.BlockSpec((tm, RG_W), lambda i: (nT - 1 - i, 0))],
        out_shape=[jax.ShapeDtypeStruct((T, RG_W), F32)] * 2,
        scratch_shapes=[pltpu.VMEM((2, tm, RG_W), F32), pltpu.VMEM((2, tm, RG_W), F32),
                        pltpu.VMEM((2, SUBLANES, RG_W), F32)],
        compiler_params=_params("arbitrary"))(xc, xc, wbd, bias, lam)
    return res[0], res[1]


def _rg_scan_bwd(xc, wbd, bias, lam, hs, dho, name):
    T = xc.shape[0]
    tm = min(256, T)
    nT = T // tm
    tiles = (lambda i: nT - 1 - i, lambda i: i)

    def body(xf_ref, xb_ref, w_ref, b_ref, lam_ref, hfc, hfp, hfn, hbc, hbp, hbn, dof_ref, dob_ref,
             dxf_ref, dxb_ref, dw_ref, db_ref, dlam_ref, a_sc, b_sc, y_sc, carry):
        i = pl.program_id(0)

        @pl.when(i == 0)
        def _():
            carry[...] = jnp.zeros_like(carry)
            dw_ref[...] = jnp.zeros_like(dw_ref)
            db_ref[...] = jnp.zeros_like(db_ref)
            dlam_ref[...] = jnp.zeros_like(dlam_ref)

        vjps, entering = [], []
        for d, (x_ref, do_ref) in enumerate(((xf_ref, dof_ref), (xb_ref, dob_ref))):
            (a, _), vjp = jax.vjp(_rg_gates, x_ref[...], w_ref[d].astype(F32), b_ref[d], lam_ref[d])
            vjps.append(vjp)
            entering.append(carry[d, 0:1, :])
            a_sc[d], b_sc[d] = _local_scan(a, a * do_ref[...], d == 1)
        _group_scan(((0, y_sc.at[0], False), (1, y_sc.at[1], True)), a_sc, b_sc, carry, tm // SUBLANES)

        row = lax.broadcasted_iota(jnp.int32, (tm, RG_W), 0)
        for d, (do_ref, dx_ref, hc, hp, hn, ti) in enumerate(
                ((dof_ref, dxf_ref, hfc, hfp, hfn, nT - 1 - i), (dob_ref, dxb_ref, hbc, hbp, hbn, i))):
            y = y_sc[d]
            if d == 0:
                y_next = jnp.where(row == tm - 1, entering[d], pltpu.roll(y, tm - 1, 0))
            else:
                y_next = jnp.where(row == 0, entering[d], pltpu.roll(y, 1, 0))
            dtot = do_ref[...] + y_next
            ext = _ext(hc[...], hp[...], hn[...], ti > 0, ti < nT - 1)
            hprev = _shifted(ext, -1 if d == 0 else 1, tm)
            dxc, dw, db, dlam = vjps[d]((dtot * hprev, dtot))
            dx_ref[...] = dxc
            dw_ref[d] += dw
            db_ref[d] += db
            dlam_ref[d] += dlam

    full = lambda a: pl.BlockSpec(a.shape, lambda i: (0,) * len(a.shape))
    tok = lambda d: pl.BlockSpec((tm, RG_W), lambda i: (tiles[d](i), 0))
    acc_shapes = [jax.ShapeDtypeStruct((2, RG_W, 2 * RG_W), F32), jax.ShapeDtypeStruct((2, 1, 2 * RG_W), F32),
                  jax.ShapeDtypeStruct((2, 1, RG_W), F32)]
    res = pl.pallas_call(
        body, name=name, grid=(nT,),
        in_specs=([tok(0), tok(1), full(wbd), full(bias), full(lam)]
                  + _halo_specs(tm, T, RG_W, 0, tiles[0]) + _halo_specs(tm, T, RG_W, 0, tiles[1]) + [tok(0), tok(1)]),
        out_specs=[tok(0), tok(1)] + [full(s) for s in acc_shapes],
        out_shape=[jax.ShapeDtypeStruct((T, RG_W), F32)] * 2 + acc_shapes,
        scratch_shapes=[pltpu.VMEM((2, tm, RG_W), F32), pltpu.VMEM((2, tm, RG_W), F32),
                        pltpu.VMEM((2, tm, RG_W), F32), pltpu.VMEM((2, SUBLANES, RG_W), F32)],
        compiler_params=_params("arbitrary"))(xc, xc, wbd, bias, lam, hs[0], hs[0], hs[0], hs[1], hs[1], hs[1],
                                              dho, dho)
    return (res[0], res[1]), res[2], res[3], res[4]


def _chunk_rows(n_chunks, reverse):
    up, down = (lambda c: c), (lambda c: n_chunks - 1 - c)
    return (down, up) if reverse else (up, down)


STEP_CHUNKS = 4
STEP_ROWS = STEP_CHUNKS * CHUNK


def _sub_chunks(ascending):
    order = range(STEP_CHUNKS) if ascending else range(STEP_CHUNKS - 1, -1, -1)
    return [(s, slice(s * CHUNK, (s + 1) * CHUNK)) for s in order]


def _hg_fwd(proj, l0, l1, name):
    T = proj.shape[0]
    nC = T // CHUNK
    nS = nC // STEP_CHUNKS
    H, dk, dv = 4, 128, 128
    rows = _chunk_rows(nS, False)

    def body(qf, ff, vf, qb, fb, vb, l0_ref, l1_ref, of, ob, spf, spb, st):
        @pl.when(pl.program_id(0) == 0)
        def _():
            st[...] = jnp.zeros_like(st)

        for d, (q, f, v, o, sp) in enumerate(((qf, ff, vf, of, spf), (qb, fb, vb, ob, spb))):
            tri, tri_t, mref = _tri_consts(d)
            stp = tuple(st[d, h] for h in range(H))
            for s, r in _sub_chunks(d == 0):
                for h in range(H):
                    sp[s, h] = stp[h]
                o_val, stp = _hg_chunk(q[r, :], f[r, :], v[r, :], l0_ref[...], l1_ref[...], stp, tri, tri_t, mref)
                o[r, :] = o_val
            for h in range(H):
                st[d, h] = stp[h]

    tok = lambda d, col: pl.BlockSpec((STEP_ROWS, HG_W), lambda c: (rows[d](c), col))
    par = pl.BlockSpec((1, HG_W), lambda c: (0, 0))
    state = lambda d: pl.BlockSpec((STEP_CHUNKS, H, dv, dk), lambda c: (rows[d](c), 0, 0, 0))
    res = pl.pallas_call(
        body, name=name, grid=(nS,),
        in_specs=[tok(0, 2), tok(0, 3), tok(0, 5), tok(1, 2), tok(1, 4), tok(1, 5), par, par],
        out_specs=[tok(0, 0), tok(1, 0), state(0), state(1)],
        out_shape=[jax.ShapeDtypeStruct((T, H * dv), F32)] * 2 + [jax.ShapeDtypeStruct((nC, H, dv, dk), F32)] * 2,
        scratch_shapes=[pltpu.VMEM((2, H, dv, dk), F32)],
        compiler_params=_params("arbitrary"))(proj, proj, proj, proj, proj, proj, l0, l1)
    return (res[0], res[1]), (res[2], res[3])


def _hg_bwd(proj, l0, l1, sprev, do, name):
    T = proj.shape[0]
    nC = T // CHUNK
    nS = nC // STEP_CHUNKS
    H, dk, dv = 4, 128, 128
    rows = _chunk_rows(nS, True)

    def body(qf, ff, vf, qb, fb, vb, l0_ref, l1_ref, spf, spb, dof, dob,
             dqf, dff, dvf, dqb, dfb, dvb, dl0_ref, dl1_ref, dst):
        @pl.when(pl.program_id(0) == 0)
        def _():
            dst[...] = jnp.zeros_like(dst)
            dl0_ref[...] = jnp.zeros_like(dl0_ref)
            dl1_ref[...] = jnp.zeros_like(dl1_ref)

        for d, (q, f, v, sp, do_ref, dq_ref, df_ref, dv_ref) in enumerate(
                ((qf, ff, vf, spf, dof, dqf, dff, dvf), (qb, fb, vb, spb, dob, dqb, dfb, dvb))):
            tri, tri_t, mref = _tri_consts(d)
            fn = lambda q_, f_, v_, a0, a1, stp: _hg_chunk(q_, f_, v_, a0, a1, stp, tri, tri_t, mref)
            dstp = tuple(dst[d, h] for h in range(H))
            for s, r in _sub_chunks(d == 1):
                stp = tuple(sp[s, h] for h in range(H))
                _, vjp = jax.vjp(fn, q[r, :], f[r, :], v[r, :], l0_ref[...], l1_ref[...], stp)
                dq, df, dvv, dl0, dl1, dstp = vjp((do_ref[r, :], dstp))
                dq_ref[r, :] = dq.astype(BF16)
                df_ref[r, :] = df.astype(BF16)
                dv_ref[r, :] = dvv.astype(BF16)
                dl0_ref[d] += dl0
                dl1_ref[d] += dl1
            for h in range(H):
                dst[d, h] = dstp[h]

    tok = lambda d, col: pl.BlockSpec((STEP_ROWS, HG_W), lambda c: (rows[d](c), col))
    par = pl.BlockSpec((1, HG_W), lambda c: (0, 0))
    acc = pl.BlockSpec((2, 1, HG_W), lambda c: (0, 0, 0))
    state = lambda d: pl.BlockSpec((STEP_CHUNKS, H, dv, dk), lambda c: (rows[d](c), 0, 0, 0))
    res = pl.pallas_call(
        body, name=name, grid=(nS,),
        in_specs=[tok(0, 2), tok(0, 3), tok(0, 5), tok(1, 2), tok(1, 4), tok(1, 5), par, par,
                  state(0), state(1), tok(0, 0), tok(1, 0)],
        out_specs=[tok(0, 0)] * 3 + [tok(1, 0)] * 3 + [acc, acc],
        out_shape=[jax.ShapeDtypeStruct((T, HG_W), BF16)] * 6 + [jax.ShapeDtypeStruct((2, 1, HG_W), F32)] * 2,
        scratch_shapes=[pltpu.VMEM((2, H, dv, dk), F32)],
        compiler_params=_params("arbitrary"))(proj, proj, proj, proj, proj, proj, l0, l1, sprev[0], sprev[1], do, do)
    return (res[0], res[3]), (res[1], res[4]), (res[2], res[5]), res[6], res[7]


def _gate_logits(proj, wup, bg, name):
    T = proj.shape[0]
    tm = min(512, T)

    def body(lr_ref, w_ref, b_ref, z_ref, lrb_ref):
        lr = lr_ref[...].astype(BF16)
        lrb_ref[...] = lr
        for d in range(2):
            z_ref[d] = _dg(lr, w_ref[d], 1, 0) + b_ref[d]

    return pl.pallas_call(
        body, name=name, grid=(T // tm,),
        in_specs=[pl.BlockSpec((tm, LANES), lambda i: (i, 24)), pl.BlockSpec((2, LANES, 512), lambda i: (0, 0, 0)),
                  pl.BlockSpec((2, 1, 512), lambda i: (0, 0, 0))],
        out_specs=[pl.BlockSpec((2, tm, 512), lambda i: (0, i, 0)), pl.BlockSpec((tm, LANES), lambda i: (i, 0))],
        out_shape=[jax.ShapeDtypeStruct((2, T, 512), F32), jax.ShapeDtypeStruct((T, LANES), BF16)],
        compiler_params=_params("parallel"))(proj, wup, bg)


def _gate_logits_bwd(dz, wup, name):
    T = dz[0].shape[0]
    tm = min(512, T)

    def body(dzf_ref, dzb_ref, w_ref, dlr_ref, db_ref, dzb16_ref):
        @pl.when(pl.program_id(0) == 0)
        def _():
            db_ref[...] = jnp.zeros_like(db_ref)

        acc = jnp.zeros((tm, LANES), F32)
        for d, dz_ref in enumerate((dzf_ref, dzb_ref)):
            g = dz_ref[...]
            gb = g.astype(BF16)
            dzb16_ref[d] = gb
            acc = acc + _dg(gb, w_ref[d], 1, 1)
            db_ref[d] += jnp.sum(g, axis=0, keepdims=True)
        dlr_ref[...] = acc

    tok = pl.BlockSpec((tm, 512), lambda i: (i, 0))
    return pl.pallas_call(
        body, name=name, grid=(T // tm,),
        in_specs=[tok, tok, pl.BlockSpec((2, LANES, 512), lambda i: (0, 0, 0))],
        out_specs=[pl.BlockSpec((tm, LANES), lambda i: (i, 0)), pl.BlockSpec((2, 1, 512), lambda i: (0, 0, 0)),
                   pl.BlockSpec((2, tm, 512), lambda i: (0, i, 0))],
        out_shape=[jax.ShapeDtypeStruct((T, LANES), F32), jax.ShapeDtypeStruct((2, 1, 512), F32),
                   jax.ShapeDtypeStruct((2, T, 512), BF16)],
        compiler_params=_params("arbitrary"))(dz[0], dz[1], wup)


def _gla_fwd(proj, z, name):
    T = proj.shape[0]
    nC = T // CHUNK
    nS = nC // STEP_CHUNKS
    H, dk, dv = 4, 128, 256
    rows = _chunk_rows(nS, False)

    def body(qf, kf, vf, zf, qb, kb, vb, zb, of, ob, spf, spb, st):
        @pl.when(pl.program_id(0) == 0)
        def _():
            st[...] = jnp.zeros_like(st)

        for d, (q, k, v, z_ref, o, sp) in enumerate(((qf, kf, vf, zf, of, spf), (qb, kb, vb, zb, ob, spb))):
            tri, tri_t, mref = _tri_consts(d)
            stp = tuple(st[d, h] for h in range(H))
            for s, r in _sub_chunks(d == 0):
                for h in range(H):
                    sp[s, h] = stp[h]
                o_val, stp = _gla_chunk(q[r, :], k[r, :], v[r, :], z_ref[r, :], stp, tri, tri_t, mref)
                o[r, :] = o_val
            for h in range(H):
                st[d, h] = stp[h]

    tok = lambda d, w, col: pl.BlockSpec((STEP_ROWS, w), lambda c: (rows[d](c), col))
    gate = lambda d: pl.BlockSpec((None, STEP_ROWS, 512), lambda c: (d, rows[d](c), 0))
    state = lambda d: pl.BlockSpec((STEP_CHUNKS, H, dv, dk), lambda c: (rows[d](c), 0, 0, 0))
    res = pl.pallas_call(
        body, name=name, grid=(nS,),
        in_specs=[tok(0, 512, 0), tok(0, 512, 1), tok(0, 1024, 1), gate(0),
                  tok(1, 512, 0), tok(1, 512, 1), tok(1, 1024, 1), gate(1)],
        out_specs=[tok(0, H * dv, 0), tok(1, H * dv, 0), state(0), state(1)],
        out_shape=[jax.ShapeDtypeStruct((T, H * dv), F32)] * 2 + [jax.ShapeDtypeStruct((nC, H, dv, dk), F32)] * 2,
        scratch_shapes=[pltpu.VMEM((2, H, dv, dk), F32)],
        compiler_params=_params("arbitrary"))(proj, proj, proj, z, proj, proj, proj, z)
    return (res[0], res[1]), (res[2], res[3])


def _gla_bwd(proj, z, sprev, do, name):
    T = proj.shape[0]
    nC = T // CHUNK
    nS = nC // STEP_CHUNKS
    H, dk, dv = 4, 128, 256
    rows = _chunk_rows(nS, True)

    def body(qf, kf, vf, zf, qb, kb, vb, zb, spf, spb, dof, dob,
             dqf, dkf, dvf, dzf, dqb, dkb, dvb, dzb, dst):
        @pl.when(pl.program_id(0) == 0)
        def _():
            dst[...] = jnp.zeros_like(dst)

        for d, (q, k, v, z_ref, sp, do_ref, dq_ref, dk_ref, dv_ref, dz_ref) in enumerate(
                ((qf, kf, vf, zf, spf, dof, dqf, dkf, dvf, dzf), (qb, kb, vb, zb, spb, dob, dqb, dkb, dvb, dzb))):
            tri, tri_t, mref = _tri_consts(d)
            fn = lambda q_, k_, v_, z_, stp: _gla_chunk(q_, k_, v_, z_, stp, tri, tri_t, mref)
            dstp = tuple(dst[d, h] for h in range(H))
            for s, r in _sub_chunks(d == 1):
                stp = tuple(sp[s, h] for h in range(H))
                _, vjp = jax.vjp(fn, q[r, :], k[r, :], v[r, :], z_ref[r, :], stp)
                dq, dkk, dvv, dzz, dstp = vjp((do_ref[r, :], dstp))
                dq_ref[r, :] = dq.astype(BF16)
                dk_ref[r, :] = dkk.astype(BF16)
                dv_ref[r, :] = dvv.astype(BF16)
                dz_ref[r, :] = dzz
            for h in range(H):
                dst[d, h] = dstp[h]

    tok = lambda d, w, col: pl.BlockSpec((STEP_ROWS, w), lambda c: (rows[d](c), col))
    gate = lambda d: pl.BlockSpec((None, STEP_ROWS, 512), lambda c: (d, rows[d](c), 0))
    state = lambda d: pl.BlockSpec((STEP_CHUNKS, H, dv, dk), lambda c: (rows[d](c), 0, 0, 0))
    outs = lambda d: [tok(d, 512, 0), tok(d, 512, 0), tok(d, 1024, 0), tok(d, 512, 0)]
    shapes = [jax.ShapeDtypeStruct((T, 512), BF16), jax.ShapeDtypeStruct((T, 512), BF16),
              jax.ShapeDtypeStruct((T, 1024), BF16), jax.ShapeDtypeStruct((T, 512), F32)]
    res = pl.pallas_call(
        body, name=name, grid=(nS,),
        in_specs=[tok(0, 512, 0), tok(0, 512, 1), tok(0, 1024, 1), gate(0),
                  tok(1, 512, 0), tok(1, 512, 1), tok(1, 1024, 1), gate(1),
                  state(0), state(1), tok(0, H * dv, 0), tok(1, H * dv, 0)],
        out_specs=outs(0) + outs(1), out_shape=shapes + shapes,
        scratch_shapes=[pltpu.VMEM((2, H, dv, dk), F32)],
        compiler_params=_params("arbitrary"))(proj, proj, proj, z, proj, proj, proj, z, sprev[0], sprev[1], do, do)
    return (res[0], res[4]), (res[1], res[5]), (res[2], res[6]), (res[3], res[7])


def _l0_combine_fwd(hs, proj, o, gain, name):
    T = proj.shape[0]
    tm = min(512, T)

    def body(hf, hb, ga, of, ob, g, gn, out):
        out[...] = _l0_combine(hf[...], hb[...], ga[...], of[...], ob[...], g[...], gn[...]).astype(BF16)

    tok = pl.BlockSpec((tm, 512), lambda i: (i, 0))
    return pl.pallas_call(
        body, name=name, grid=(T // tm,),
        in_specs=[tok, tok, pl.BlockSpec((tm, 512), lambda i: (i, 1)), tok, tok,
                  pl.BlockSpec((tm, 512), lambda i: (i, 6)), pl.BlockSpec((1, 512), lambda i: (0, 0))],
        out_specs=pl.BlockSpec((tm, 1024), lambda i: (i, 0)),
        out_shape=jax.ShapeDtypeStruct((T, 1024), BF16),
        compiler_params=_params("parallel"))(hs[0], hs[1], proj, o[0], o[1], proj, gain)


def _l0_combine_bwd(hs, proj, o, gain, dh_b, w_out, name):
    T = proj.shape[0]
    tm = min(512, T)

    def body(hf, hb, ga, of, ob, g, gn, dhb_ref, w_ref, dho_ref, dga_ref, do_ref, dg_ref, dgn_ref):
        @pl.when(pl.program_id(0) == 0)
        def _():
            dgn_ref[...] = jnp.zeros_like(dgn_ref)

        _, vjp = jax.vjp(_l0_combine, hf[...], hb[...], ga[...], of[...], ob[...], g[...], gn[...])
        dhf, _, dga, dof, _, dg, dgn = vjp(_dg(dhb_ref[...], w_ref[...], 1, 1))
        dho_ref[...] = dhf
        dga_ref[...] = dga
        do_ref[...] = dof
        dg_ref[...] = dg
        dgn_ref[...] += dgn

    tok = lambda: pl.BlockSpec((tm, 512), lambda i: (i, 0))
    return pl.pallas_call(
        body, name=name, grid=(T // tm,),
        in_specs=[tok(), tok(), pl.BlockSpec((tm, 512), lambda i: (i, 1)), tok(), tok(),
                  pl.BlockSpec((tm, 512), lambda i: (i, 6)), pl.BlockSpec((1, 512), lambda i: (0, 0)),
                  pl.BlockSpec((tm, D_MODEL), lambda i: (i, 0)), pl.BlockSpec(w_out.shape, lambda i: (0, 0))],
        out_specs=[tok(), tok(), tok(), tok(), pl.BlockSpec((1, 512), lambda i: (0, 0))],
        out_shape=[jax.ShapeDtypeStruct((T, 512), F32)] * 4 + [jax.ShapeDtypeStruct((1, 512), F32)],
        compiler_params=_params("arbitrary"))(hs[0], hs[1], proj, o[0], o[1], proj, gain, dh_b, w_out)


def _l0_assemble(dxa, dga, dq, df, dv, dg, name):
    T = dxa.shape[0]
    tm = min(512, T)

    def body(xa, ga, q0, q1, f0, f1, v0, v1, g, out):
        both = lambda a, b: (a[...].astype(F32) + b[...].astype(F32)).astype(BF16)
        out[...] = jnp.concatenate([xa[...].astype(BF16), ga[...].astype(BF16), both(q0, q1), f0[...], f1[...],
                                    both(v0, v1), g[...].astype(BF16)], axis=1)

    tok = lambda: pl.BlockSpec((tm, 512), lambda i: (i, 0))
    return pl.pallas_call(
        body, name=name, grid=(T // tm,),
        in_specs=[tok() for _ in range(9)],
        out_specs=pl.BlockSpec((tm, AB_IN), lambda i: (i, 0)),
        out_shape=jax.ShapeDtypeStruct((T, AB_IN), BF16),
        compiler_params=_params("parallel"))(dxa, dga, dq[0], dq[1], df[0], df[1], dv[0], dv[1], dg)


def _l1_combine_fwd(o, proj, gain, name):
    T = proj.shape[0]
    tm = min(512, T)

    def body(of, ob, r, gn, out):
        out[...] = _l1_combine(of[...], ob[...], r[...], gn[...]).astype(BF16)

    tok = pl.BlockSpec((tm, 1024), lambda i: (i, 0))
    return pl.pallas_call(
        body, name=name, grid=(T // tm,),
        in_specs=[tok, tok, pl.BlockSpec((tm, 1024), lambda i: (i, 2)), pl.BlockSpec((1, 1024), lambda i: (0, 0))],
        out_specs=pl.BlockSpec((tm, 1024), lambda i: (i, 0)),
        out_shape=jax.ShapeDtypeStruct((T, 1024), BF16),
        compiler_params=_params("parallel"))(o[0], o[1], proj, gain)


def _l1_combine_bwd(o, proj, gain, dh_b, w_out, name):
    T = proj.shape[0]
    tm = min(512, T)

    def body(of, ob, r, gn, dhb_ref, w_ref, do_ref, dr_ref, dgn_ref):
        @pl.when(pl.program_id(0) == 0)
        def _():
            dgn_ref[...] = jnp.zeros_like(dgn_ref)

        _, vjp = jax.vjp(_l1_combine, of[...], ob[...], r[...], gn[...])
        dof, _, dr, dgn = vjp(_dg(dhb_ref[...], w_ref[...], 1, 1))
        do_ref[...] = dof
        dr_ref[...] = dr
        dgn_ref[...] += dgn

    tok = lambda: pl.BlockSpec((tm, 1024), lambda i: (i, 0))
    return pl.pallas_call(
        body, name=name, grid=(T // tm,),
        in_specs=[tok(), tok(), pl.BlockSpec((tm, 1024), lambda i: (i, 2)),
                  pl.BlockSpec((1, 1024), lambda i: (0, 0)), tok(), pl.BlockSpec(w_out.shape, lambda i: (0, 0))],
        out_specs=[tok(), tok(), pl.BlockSpec((1, 1024), lambda i: (0, 0))],
        out_shape=[jax.ShapeDtypeStruct((T, 1024), F32)] * 2 + [jax.ShapeDtypeStruct((1, 1024), F32)],
        compiler_params=_params("arbitrary"))(o[0], o[1], proj, gain, dh_b, w_out)


def _l1_assemble(dq, dk, dv, dr, dlr, name):
    T = dr.shape[0]
    tm = min(512, T)

    def body(q0, q1, k0, k1, v0, v1, r, a, out):
        both = lambda x, y: (x[...].astype(F32) + y[...].astype(F32)).astype(BF16)
        out[...] = jnp.concatenate([both(q0, q1), both(k0, k1), both(v0, v1), r[...].astype(BF16),
                                    a[...].astype(BF16)], axis=1)

    tok = lambda w: pl.BlockSpec((tm, w), lambda i: (i, 0))
    return pl.pallas_call(
        body, name=name, grid=(T // tm,),
        in_specs=[tok(512), tok(512), tok(512), tok(512), tok(1024), tok(1024), tok(1024), tok(LANES)],
        out_specs=pl.BlockSpec((tm, GLA_IN_PAD), lambda i: (i, 0)),
        out_shape=jax.ShapeDtypeStruct((T, GLA_IN_PAD), BF16),
        compiler_params=_params("parallel"))(dq[0], dq[1], dk[0], dk[1], dv[0], dv[1], dr, dlr)


HBM_SPEC = pl.BlockSpec(memory_space=pltpu.HBM)


def _place():
    x, y, c = lax.axis_index("x"), lax.axis_index("y"), lax.axis_index("c")
    return x, y, c


def _allgather_vmem(x_shard, name):
    m_per, n = x_shard.shape

    def body(x_ref, out_ref, send_sems, recv_sems, local_sem):
        x, y, c = _place()
        me, sibling = (x, y, c), (x, y, 1 - c)
        chips = [(1 - x, y), (x, 1 - y), (1 - x, 1 - y)]

        def rows(px, py, pc):
            return out_ref.at[pl.ds((4 * px + 2 * py + pc) * m_per, m_per), :]

        def copy(k, block, to, src=None):
            return pltpu.make_async_remote_copy(
                src_ref=rows(*block) if src is None else src, dst_ref=rows(*block),
                send_sem=send_sems.at[k], recv_sem=recv_sems.at[k], device_id=to, device_id_type=MESH)

        mine = pltpu.make_async_copy(x_ref, rows(*me), local_sem)
        mine.start()
        first = [copy(0, me, sibling, src=x_ref)]
        first += [copy(1 + j, me, (*chip, c), src=x_ref) for j, chip in enumerate(chips)]
        for cp in first:
            cp.start()
        passed = [copy(4 + j, (*chip, c), sibling) for j, chip in enumerate(chips)]
        for j, chip in enumerate(chips):
            copy(1 + j, (*chip, c), me).wait_recv()
            passed[j].start()
        copy(0, sibling, me).wait_recv()
        for j, chip in enumerate(chips):
            copy(4 + j, (*chip, 1 - c), me).wait_recv()
        for cp in first + passed:
            cp.wait_send()
        mine.wait()

    vm = pl.BlockSpec(memory_space=pltpu.VMEM)
    return pl.pallas_call(
        body, name=name, in_specs=[vm], out_specs=vm,
        out_shape=jax.ShapeDtypeStruct((N_DEV * m_per, n), x_shard.dtype),
        scratch_shapes=[pltpu.SemaphoreType.DMA((7,)), pltpu.SemaphoreType.DMA((7,)), pltpu.SemaphoreType.DMA],
        compiler_params=pltpu.CompilerParams(has_side_effects=True, vmem_limit_bytes=VMEM_LIMIT))(x_shard)


SEM_SPEC = pl.BlockSpec(memory_space=pltpu.SEMAPHORE)
DATAFLOW_EFFECT = pltpu.SideEffectType.DATAFLOW_SIDE_EFFECTING


def _copies(plan, srcs, lands, send_sems, recv_sems):
    x, y, c = _place()
    return [pltpu.make_async_remote_copy(src_ref=s, dst_ref=d, send_sem=send_sems.at[k], recv_sem=recv_sems.at[k],
                                         device_id=dev, device_id_type=MESH)
            for k, (s, d, dev) in enumerate(plan(srcs, lands, x, y, c))]


def _copies_start(plan, n_copies, srcs, lands, name):
    ns, nl = len(srcs), len(lands)

    def body(*refs):
        send_sems, recv_sems = refs[ns + nl], refs[ns + nl + 1]
        for cp in _copies(plan, refs[:ns], refs[ns:ns + nl], send_sems, recv_sems):
            cp.start()
        refs[-1][...] = jnp.zeros_like(refs[-1])

    arrays = list(srcs) + list(lands)
    res = pl.pallas_call(
        body, name=name,
        in_specs=[HBM_SPEC] * (ns + nl),
        out_specs=tuple([SEM_SPEC, SEM_SPEC] + [HBM_SPEC] * (ns + nl) + [pl.BlockSpec(memory_space=pltpu.VMEM)]),
        out_shape=tuple([pltpu.SemaphoreType.DMA((n_copies,)), pltpu.SemaphoreType.DMA((n_copies,))]
                        + [pltpu.HBM(a.shape, a.dtype) for a in arrays]
                        + [jax.ShapeDtypeStruct((SUBLANES, LANES), F32)]),
        input_output_aliases={i: 2 + i for i in range(ns + nl)},
        compiler_params=pltpu.CompilerParams(has_side_effects=DATAFLOW_EFFECT),
    )(*[pltpu.with_memory_space_constraint(a, pltpu.HBM) for a in arrays])
    return res[0], res[1], list(res[2:2 + ns]), list(res[2 + ns:2 + ns + nl]), res[-1]


def _copies_wait(plan, started, after, name):
    send_sems, recv_sems, srcs, lands, _ = started
    ns, nl = len(srcs), len(lands)

    def body(*refs):
        for cp in _copies(plan, refs[:ns], refs[ns:ns + nl], refs[ns + nl], refs[ns + nl + 1]):
            cp.wait_send()
            cp.wait_recv()

    arrays = list(srcs) + list(lands)
    res = pl.pallas_call(
        body, name=name,
        in_specs=[HBM_SPEC] * (ns + nl) + [SEM_SPEC, SEM_SPEC, pl.BlockSpec(memory_space=pl.ANY)],
        out_specs=tuple([HBM_SPEC] * (ns + nl)),
        out_shape=tuple(pltpu.HBM(a.shape, a.dtype) for a in arrays),
        input_output_aliases={i: i for i in range(ns + nl)},
        compiler_params=pltpu.CompilerParams(has_side_effects=DATAFLOW_EFFECT),
    )(*arrays, send_sems, recv_sems, after)
    return list(res[:ns]), list(res[ns:])


def _after(token, value):
    return value + token[0:1, 0:1].astype(value.dtype)


def _chips(x, y):
    return [(1 - x, y), (x, 1 - y), (1 - x, 1 - y)]


def _plan_gather_first(srcs, lands, x, y, c):
    me = 4 * x + 2 * y + c
    out = []
    for s, l in zip(srcs, lands):
        out.append((s, l.at[me], (x, y, 1 - c)))
        out += [(s, l.at[me], (*chip, c)) for chip in _chips(x, y)]
    return out


def _plan_gather_pass(srcs, lands, x, y, c):
    out = []
    for l in lands:
        for chip in _chips(x, y):
            slot = l.at[4 * chip[0] + 2 * chip[1] + c]
            out.append((slot, slot, (x, y, 1 - c)))
    return out


def _plan_grads_sibling(srcs, lands, x, y, c):
    return [(s.at[2 * q + (1 - c)], l.at[q], (x, y, 1 - c)) for s, l in zip(srcs, lands) for q in range(4)]


def _plan_grads_chips(srcs, lands, x, y, c):
    return [(s.at[2 * chip[0] + chip[1]], l.at[k], (*chip, c))
            for s, l in zip(srcs, lands) for k, chip in enumerate(_chips(x, y))]


def _landing(n_slots, like):
    return [lax.empty((n_slots,) + a.shape[1:], a.dtype) for a in like]


def _sum_slots(g, name):
    _, R, C = g.shape
    tr = min(256, R)
    assert R % tr == 0

    def body(g_ref, o_ref):
        acc = g_ref[0]
        for j in range(1, N_DEV):
            acc = acc + g_ref[j]
        o_ref[...] = acc

    return pl.pallas_call(
        body, name=name, grid=(R // tr,),
        in_specs=[pl.BlockSpec((N_DEV, tr, C), lambda i: (0, i, 0))],
        out_specs=pl.BlockSpec((tr, C), lambda i: (i, 0)),
        out_shape=jax.ShapeDtypeStruct((R, C), F32),
        compiler_params=_params("parallel"))(g)


def _chip_partial(g, r1, place, name):
    _, R, C = g.shape
    tr = min(1024, R)
    assert R % tr == 0

    def body(pl_ref, g_ref, r_ref, pb_ref, pm_ref):
        q = pl.program_id(1)
        s = g_ref[...] + r_ref[...]
        pb_ref[...] = s.astype(BF16)

        @pl.when(q == pl_ref[1])
        def _():
            pm_ref[...] = s

    grid_spec = pltpu.PrefetchScalarGridSpec(
        num_scalar_prefetch=1, grid=(R // tr, 4),
        in_specs=[pl.BlockSpec((None, tr, C), lambda r, q, p: (2 * q + p[0], r, 0)),
                  pl.BlockSpec((None, tr, C), lambda r, q, p: (q, r, 0))],
        out_specs=[pl.BlockSpec((None, tr, C), lambda r, q, p: (q, r, 0)),
                   pl.BlockSpec((tr, C), lambda r, q, p: (r, 0))])
    return pl.pallas_call(
        body, name=name, grid_spec=grid_spec,
        out_shape=[jax.ShapeDtypeStruct((4, R, C), BF16), jax.ShapeDtypeStruct((R, C), F32)],
        compiler_params=_params("parallel", "arbitrary"))(place, g, r1)


def _adamw_update(w, g, m, v, grad_ref, delta_ref, m_ref, v_ref):
    mn = ADAM_B1 * m + (1.0 - ADAM_B1) * g
    vn = ADAM_B2 * v + (1.0 - ADAM_B2) * jnp.square(g)
    m_hat = mn / (1.0 - ADAM_B1 ** ADAM_STEP)
    v_hat = vn / (1.0 - ADAM_B2 ** ADAM_STEP)
    grad_ref[...] = g
    delta_ref[...] = -ADAM_LR * (m_hat / (jnp.sqrt(v_hat) + ADAM_EPS) + ADAM_WD * w)
    m_ref[...] = mn
    v_ref[...] = vn


def _adamw_whole(w, g, m, v, name):
    def body(w_ref, g_ref, m_ref, v_ref, go, do, mo, vo):
        _adamw_update(w_ref[...], g_ref[...], m_ref[...], v_ref[...], go, do, mo, vo)

    vm = pl.BlockSpec(memory_space=pltpu.VMEM)
    return pl.pallas_call(body, name=name, in_specs=[vm] * 4, out_specs=[vm] * 4,
                          out_shape=[jax.ShapeDtypeStruct(w.shape, F32)] * 4)(w, g, m, v)


def _adamw(w, gparts, m, v, name):
    _, R, C = w.shape
    tr = min(256, R)
    assert R % tr == 0

    def body(w_ref, g0_ref, g3_ref, m_ref, v_ref, go, do, mo, vo):
        g = g0_ref[...]
        for k in range(3):
            g = g + g3_ref[k].astype(F32)
        _adamw_update(w_ref[...], g, m_ref[...], v_ref[...], go, do, mo, vo)

    blk = pl.BlockSpec((tr, C), lambda i: (i, 0))
    wblk = pl.BlockSpec((None, tr, C), lambda i: (0, i, 0))
    return pl.pallas_call(
        body, name=name, grid=(R // tr,),
        in_specs=[wblk, blk, pl.BlockSpec((3, tr, C), lambda i: (0, i, 0)), wblk, wblk], out_specs=[wblk] * 4,
        out_shape=[jax.ShapeDtypeStruct(w.shape, F32)] * 4,
        compiler_params=_params("parallel"))(w, gparts[0], gparts[1], m, v)


def _adamw_layers(w, parts, m, v, name):
    _, R, C = w.shape
    tr = min(256, R)
    assert R % tr == 0

    def body(w_ref, p0, r0, p1, r1, m_ref, v_ref, go, do, mo, vo):
        gs = []
        for p, r in ((p0, r0), (p1, r1)):
            g = p[...]
            for k in range(3):
                g = g + r[k].astype(F32)
            gs.append(g)
        g = jnp.where(pl.program_id(0) == 0, gs[0], gs[1])
        _adamw_update(w_ref[...], g, m_ref[...], v_ref[...], go, do, mo, vo)

    lay = pl.BlockSpec((None, tr, C), lambda l, i: (l, i, 0))
    one = pl.BlockSpec((tr, C), lambda l, i: (i, 0))
    three = pl.BlockSpec((3, tr, C), lambda l, i: (0, i, 0))
    return pl.pallas_call(
        body, name=name, grid=(2, R // tr), in_specs=[lay, one, three, one, three, lay, lay],
        out_specs=[lay] * 4, out_shape=[jax.ShapeDtypeStruct((2, R, C), F32)] * 4,
        compiler_params=_params("parallel", "parallel"))(w, parts[0][0], parts[0][1], parts[1][0], parts[1][1], m, v)


SMALL_SHARDED = ("rg_conv_w", "rg_b_a", "rg_b_x", "rg_lambda", "gla_w_gate_up", "gla_b_gate", "gla_norm")
SMALL_REPLICATED = ("norm_mix", "norm_mlp", "norm_final", "rg_conv_b", "rg_w_a", "rg_w_x", "hg_lb_logits", "hg_norm")
WEIGHT_NAMES = ("norm_mix", "norm_mlp", "norm_final", "mlp_w1", "mlp_w2", "ab_w_in", "ab_w_out", "rg_conv_w",
                "rg_conv_b", "rg_w_a", "rg_b_a", "rg_w_x", "rg_b_x", "rg_lambda", "hg_lb_logits", "hg_norm",
                "gla_w_in", "gla_w_out", "gla_w_gate_up", "gla_b_gate", "gla_norm")


def _rows128(a):
    return a.reshape(-1, LANES)


def _part_rows(a):
    return -(-(a.size // LANES) // SUBLANES) * SUBLANES


def _pack_rows(arrays, pad_to=SUBLANES):
    parts = [jnp.pad(_rows128(a), ((0, _part_rows(a) - a.size // LANES), (0, 0))) for a in arrays]
    total = sum(p.shape[0] for p in parts)
    extra = (-total) % pad_to
    if extra:
        parts.append(jnp.zeros((extra, LANES), parts[0].dtype))
    return jnp.concatenate(parts, axis=0)


def _unshard_last(g, shape_local):
    nd = len(shape_local)
    t = g.reshape((N_DEV,) + tuple(shape_local))
    t = jnp.moveaxis(t, 0, nd - 1)
    return t.reshape(tuple(shape_local[:-1]) + (N_DEV * shape_local[-1],))


def _block_diag(w):
    eye = jnp.eye(8, dtype=w.dtype)
    return (w[:, :, :, None, :] * eye[None, :, None, :, None]).reshape(2, RG_W, RG_W)


def _block_diag_extract(dw):
    t = dw.reshape(2, 8, 64, 8, 64)
    return jnp.moveaxis(jnp.diagonal(t, axis1=1, axis2=3), -1, 1)


def kernel(x, norm_mix, norm_mlp, norm_final, mlp_w1, mlp_w2, ab_w_in, ab_w_out, rg_conv_w, rg_conv_b, rg_w_a, rg_b_a, rg_w_x, rg_b_x, rg_lambda, hg_lb_logits, hg_norm, gla_w_in, gla_w_out, gla_w_gate_up, gla_b_gate, gla_norm, loss_target, m_norm_mix, m_norm_mlp, m_norm_final, m_mlp_w1, m_mlp_w2, m_ab_w_in, m_ab_w_out, m_rg_conv_w, m_rg_conv_b, m_rg_w_a, m_rg_b_a, m_rg_w_x, m_rg_b_x, m_rg_lambda, m_hg_lb_logits, m_hg_norm, m_gla_w_in, m_gla_w_out, m_gla_w_gate_up, m_gla_b_gate, m_gla_norm, v_norm_mix, v_norm_mlp, v_norm_final, v_mlp_w1, v_mlp_w2, v_ab_w_in, v_ab_w_out, v_rg_conv_w, v_rg_conv_b, v_rg_w_a, v_rg_b_a, v_rg_w_x, v_rg_b_x, v_rg_lambda, v_hg_lb_logits, v_hg_norm, v_gla_w_in, v_gla_w_out, v_gla_w_gate_up, v_gla_b_gate, v_gla_norm):
    w_loc = dict(norm_mix=norm_mix, norm_mlp=norm_mlp, norm_final=norm_final, mlp_w1=mlp_w1, mlp_w2=mlp_w2,
                 ab_w_in=ab_w_in, ab_w_out=ab_w_out, rg_conv_w=rg_conv_w, rg_conv_b=rg_conv_b, rg_w_a=rg_w_a,
                 rg_b_a=rg_b_a, rg_w_x=rg_w_x, rg_b_x=rg_b_x, rg_lambda=rg_lambda, hg_lb_logits=hg_lb_logits,
                 hg_norm=hg_norm, gla_w_in=gla_w_in, gla_w_out=gla_w_out, gla_w_gate_up=gla_w_gate_up,
                 gla_b_gate=gla_b_gate, gla_norm=gla_norm)
    m_loc = dict(norm_mix=m_norm_mix, norm_mlp=m_norm_mlp, norm_final=m_norm_final, mlp_w1=m_mlp_w1,
                 mlp_w2=m_mlp_w2, ab_w_in=m_ab_w_in, ab_w_out=m_ab_w_out, rg_conv_w=m_rg_conv_w,
                 rg_conv_b=m_rg_conv_b, rg_w_a=m_rg_w_a, rg_b_a=m_rg_b_a, rg_w_x=m_rg_w_x, rg_b_x=m_rg_b_x,
                 rg_lambda=m_rg_lambda, hg_lb_logits=m_hg_lb_logits, hg_norm=m_hg_norm, gla_w_in=m_gla_w_in,
                 gla_w_out=m_gla_w_out, gla_w_gate_up=m_gla_w_gate_up, gla_b_gate=m_gla_b_gate,
                 gla_norm=m_gla_norm)
    v_loc = dict(norm_mix=v_norm_mix, norm_mlp=v_norm_mlp, norm_final=v_norm_final, mlp_w1=v_mlp_w1,
                 mlp_w2=v_mlp_w2, ab_w_in=v_ab_w_in, ab_w_out=v_ab_w_out, rg_conv_w=v_rg_conv_w,
                 rg_conv_b=v_rg_conv_b, rg_w_a=v_rg_w_a, rg_b_a=v_rg_b_a, rg_w_x=v_rg_w_x, rg_b_x=v_rg_b_x,
                 rg_lambda=v_rg_lambda, hg_lb_logits=v_hg_lb_logits, hg_norm=v_hg_norm, gla_w_in=v_gla_w_in,
                 gla_w_out=v_gla_w_out, gla_w_gate_up=v_gla_w_gate_up, gla_b_gate=v_gla_b_gate,
                 gla_norm=v_gla_norm)

    T = x.shape[1]
    h0 = x.reshape(T, D_MODEL)
    target = loss_target.reshape(T, D_MODEL)
    ax, ay, ac = lax.axis_index("x"), lax.axis_index("y"), lax.axis_index("c")
    dev = 4 * ax + 2 * ay + ac
    place = jnp.stack([ac, 2 * ax + ay]).astype(jnp.int32)

    abin_shard = ab_w_in[0].astype(BF16)
    first_started = _copies_start(_plan_gather_first, 4, [abin_shard], _landing(N_DEV, [abin_shard[None]]),
                                  "ag_first_start")
    rest_shards = [mlp_w1[0].astype(BF16), mlp_w2[0].astype(BF16), gla_w_in[0].astype(BF16),
                   gla_w_out[0].astype(BF16), mlp_w1[1].astype(BF16), mlp_w2[1].astype(BF16),
                   _after(first_started[4], ab_w_out[0].astype(BF16))]
    ag_started = _copies_start(_plan_gather_first, 4 * len(rest_shards), rest_shards,
                               _landing(N_DEV, [s[None] for s in rest_shards]), "ag_rest_start")

    small_local = [w_loc[n] for n in SMALL_SHARDED]
    small_g = _allgather_vmem(_pack_rows(small_local, 8), "ag_small")
    small_g = small_g.reshape(N_DEV, -1, LANES)
    full = {}
    off = 0
    for n, a in zip(SMALL_SHARDED, small_local):
        full[n] = _unshard_last(small_g[:, off:off + a.size // LANES].reshape(N_DEV, a.size), a.shape)
        off += _part_rows(a)
    conv_w = full["rg_conv_w"][0]
    b_a, b_x, lam = full["rg_b_a"][0], full["rg_b_x"][0], full["rg_lambda"][0]
    w_up, b_gate, g_norm = full["gla_w_gate_up"][0], full["gla_b_gate"][0], full["gla_norm"]

    cw8 = jnp.pad(conv_w, ((0, 4), (0, 0)))
    wbd = jnp.concatenate([_block_diag(rg_w_a[0]), _block_diag(rg_w_x[0])], axis=2).astype(BF16)
    rg_bias = jnp.concatenate([b_a, b_x], axis=1).reshape(2, 1, 2 * RG_W)
    lam3 = lam.reshape(2, 1, RG_W)
    l0, l1 = hg_lb_logits[0:1], hg_lb_logits[1:2]
    wup_pad = jnp.zeros((2, LANES, 512), F32).at[0, 0:16].set(w_up[0]).at[1, 16:32].set(w_up[1])
    bg3 = b_gate.reshape(2, 1, 512)
    nmix0, nmix1 = norm_mix[0:1], norm_mix[1:2]
    nmlp0, nmlp1 = norm_mlp[0:1], norm_mlp[1:2]
    nfin = norm_final.reshape(1, D_MODEL)

    prepared = (ag_started[4] + cw8[:, 0:LANES] + wup_pad[0, 0:SUBLANES, 0:LANES] + rg_bias[0, :, 0:LANES]
                + wbd[0, 0:SUBLANES, 0:LANES].astype(F32) + lam3[0, :, 0:LANES] + bg3[0, :, 0:LANES])
    (abin_shard,), abin_l = _copies_wait(_plan_gather_first, first_started, prepared, "ag_first_wait")
    first_pass = _copies_start(_plan_gather_pass, 3, [], abin_l, "ag_first_pass_start")
    _, (abin_g,) = _copies_wait(_plan_gather_pass, first_pass, first_pass[4], "ag_first_pass_wait")
    abin_g = lax.dynamic_update_index_in_dim(abin_g, abin_shard, dev, 0)
    wab_in = jnp.transpose(abin_g, (1, 0, 2)).reshape(D_MODEL, AB_IN)
    proj0, y0 = _norm_matmul(h0, _after(ag_started[4], nmix0), wab_in, "l0_in_proj")
    xc = _rg_conv_fwd(proj0, cw8, rg_conv_b, "rg_conv")
    hs = _rg_scan_fwd(xc, wbd, rg_bias, lam3, "rg_scan")
    o_hg, s_hg = _hg_fwd(proj0, l0, l1, "hg_chunks")
    both_done = hs[0][0:SUBLANES, 0:LANES] + o_hg[0][0:SUBLANES, 0:LANES]
    rest_shards, rest_lands = _copies_wait(_plan_gather_first, ag_started, both_done, "ag_rest_wait")
    pass_started = _copies_start(_plan_gather_pass, 3 * len(rest_lands), [], rest_lands, "ag_pass_start")
    mixin0 = _l0_combine_fwd(hs, proj0, o_hg, _after(pass_started[4], hg_norm), "l0_combine")
    _, rest_g = _copies_wait(_plan_gather_pass, pass_started, mixin0, "ag_pass_wait")
    rest_g = [lax.dynamic_update_index_in_dim(g, s, dev, 0) for g, s in zip(rest_g, rest_shards)]
    wab_out = rest_g[6].reshape(D_MODEL, D_MODEL)
    h1 = _matmul_res(mixin0, wab_out, h0, "l0_out_proj")
    w1g = (rest_g[0], rest_g[4])
    w2f = (rest_g[1].reshape(D_FF, D_MODEL), rest_g[5].reshape(D_FF, D_MODEL))
    wgla_in = jnp.pad(jnp.transpose(rest_g[2], (1, 0, 2)).reshape(D_MODEL, GLA_IN),
                      ((0, 0), (0, GLA_IN_PAD - GLA_IN)))
    wgla_out = rest_g[3].reshape(D_MODEL, D_MODEL)
    h2, pre0, ym0 = _mlp_fwd(h1, nmlp0, w1g[0], w2f[0], "mlp0")
    proj1, y1 = _norm_matmul(h2, nmix1, wgla_in, "l1_in_proj")
    z_gate, lr_b = _gate_logits(proj1, wup_pad, bg3, "gla_gate_logits")
    o_gla, s_gla = _gla_fwd(proj1, z_gate, "gla_chunks")
    mixin1 = _l1_combine_fwd(o_gla, proj1, g_norm, "l1_combine")
    h3 = _matmul_res(mixin1, wgla_out, h2, "l1_out_proj")
    h4, pre1, ym1 = _mlp_fwd(h3, nmlp1, w1g[1], w2f[1], "mlp1")
    loss_blk, dh4, dh4b, d_nfin = _final_loss(h4, nfin, target, "final_loss")

    dh3, dh3b, dpre1, act1, d_nmlp1 = _mlp_bwd(dh4, dh4b, h3, nmlp1, pre1, w1g[1], w2f[1], "mlp1_bwd")
    g_w1_1 = _wgrad(ym1, dpre1, 512, "mlp1_dw1", sharded_cols=True)
    g_w2_1 = _wgrad(act1, dh4b, 512, "mlp1_dw2")
    g_gla_out = _wgrad(mixin1, dh3b, 512, "l1_out_dw")
    do_gla, dr, d_gnorm = _l1_combine_bwd(o_gla, proj1, g_norm, dh3b, wgla_out, "l1_combine_bwd")
    dq1, dk1, dv1, dz_gate = _gla_bwd(proj1, z_gate, s_gla, do_gla, "gla_chunks_bwd")
    dlr1, d_bg, dz_b = _gate_logits_bwd(dz_gate, wup_pad, "gla_gate_logits_bwd")
    d_wup = [_wgrad(lr_b, dz_b[d], 512, "gla_gate_dw%d" % d) for d in range(2)]
    dproj1 = _l1_assemble(dq1, dk1, dv1, dr, dlr1, "l1_assemble")
    dh2, dh2b, d_nmix1 = _dgrad_norm(dproj1, wgla_in, h2, nmix1, dh3, "l1_in_dgrad")
    g_gla_in = _wgrad(y1, dproj1, 640, "l1_in_dw")

    def reduce_start(grads, tag):
        return _copies_start(_plan_grads_sibling, 4 * len(grads), grads, _landing(4, grads), "rs_%s_d2d_start" % tag)

    def reduce_mid(started, after, tag):
        grads, got = _copies_wait(_plan_grads_sibling, started, after, "rs_%s_d2d_wait" % tag)
        parts = [_chip_partial(g, r, place, "rs_%s_partial%d" % (tag, a)) for a, (g, r) in enumerate(zip(grads, got))]
        pb = [p[0] for p in parts]
        return _copies_start(_plan_grads_chips, 3 * len(pb), pb, _landing(3, pb), "rs_%s_ici_start" % tag), \
            [p[1] for p in parts]

    def reduce_end(started, mine, after, tag):
        _, got = _copies_wait(_plan_grads_chips, started, after, "rs_%s_ici_wait" % tag)
        return list(zip(mine, got))

    slots_l1 = [g_w1_1, g_w2_1.reshape(N_DEV, 512, D_MODEL),
                jnp.transpose(g_gla_in[:, :GLA_IN].reshape(D_MODEL, N_DEV, GLA_IN // N_DEV), (1, 0, 2)),
                g_gla_out.reshape(N_DEV, 128, D_MODEL)]
    ra_d2d = reduce_start(slots_l1, "l1")

    dh1, dh1b, dpre0, act0, d_nmlp0 = _mlp_bwd(dh2, dh2b, h1, _after(ra_d2d[4], nmlp0), pre0, w1g[0], w2f[0],
                                               "mlp0_bwd")
    g_w1_0 = _wgrad(ym0, dpre0, 512, "mlp0_dw1", sharded_cols=True)
    g_w2_0 = _wgrad(act0, dh2b, 512, "mlp0_dw2")
    ra_ici, ra_mine = reduce_mid(ra_d2d, g_w2_0, "l1")
    g_ab_out = _wgrad(mixin0, dh1b, 512, "l0_out_dw")
    rb_d2d = reduce_start([g_w1_0, g_w2_0.reshape(N_DEV, 512, D_MODEL), g_ab_out.reshape(N_DEV, 128, D_MODEL)],
                          "mlp0")
    dho, dga, do_hg, dg_gate, d_hgnorm = _l0_combine_bwd(
        hs, proj0, o_hg, _after(rb_d2d[4], _after(ra_ici[4], hg_norm)), dh1b, wab_out, "l0_combine_bwd")
    dxc, d_wbd, d_rgb, d_lam = _rg_scan_bwd(xc, wbd, rg_bias, lam3, hs, dho, "rg_scan_bwd")
    dxa, d_cw8, d_cb = _rg_conv_bwd(dxc, proj0, cw8, "rg_conv_bwd")
    dq0, df0, dv0, d_l0, d_l1 = _hg_bwd(proj0, l0, l1, s_hg, do_hg, "hg_chunks_bwd")
    rb_ici, rb_mine = reduce_mid(rb_d2d, d_l0, "mlp0")
    dproj0 = _l0_assemble(dxa, dga, dq0, df0, dv0, dg_gate, "l0_assemble")
    dx, _, d_nmix0 = _dgrad_norm(dproj0, wab_in, h0, _after(rb_ici[4], nmix0), dh1, "l0_in_dgrad")

    d_wa = _block_diag_extract(d_wbd[:, :, :RG_W])[None]
    d_wx = _block_diag_extract(d_wbd[:, :, RG_W:])[None]
    small_full = {
        "norm_mix": jnp.concatenate([d_nmix0, d_nmix1], axis=0), "norm_mlp": jnp.concatenate([d_nmlp0, d_nmlp1], axis=0),
        "norm_final": d_nfin.reshape(D_MODEL), "rg_conv_b": d_cb, "rg_w_a": d_wa, "rg_w_x": d_wx,
        "hg_lb_logits": jnp.concatenate([d_l0[0] + d_l0[1], d_l1[0] + d_l1[1]], axis=0), "hg_norm": d_hgnorm,
        "rg_conv_w": d_cw8[0:4][None], "rg_b_a": d_rgb[:, 0, :RG_W][None], "rg_b_x": d_rgb[:, 0, RG_W:][None],
        "rg_lambda": d_lam[:, 0, :][None],
        "gla_w_gate_up": jnp.stack([d_wup[0][0:16], d_wup[1][16:32]])[None], "gla_b_gate": d_bg[:, 0, :][None],
        "gla_norm": d_gnorm}
    small_names = SMALL_REPLICATED + SMALL_SHARDED
    packed = _pack_rows([loss_blk] + [small_full[n] for n in small_names], 256)
    ar_first = _copies_start(_plan_gather_first, 4, [packed], _landing(N_DEV, [packed[None]]), "ar_small_start")

    g_ab_in = _wgrad(y0, dproj0, 512, "l0_in_dw", behind=ar_first[4])
    rc_d2d = reduce_start([jnp.transpose(g_ab_in.reshape(D_MODEL, N_DEV, AB_IN // N_DEV), (1, 0, 2))], "ab")
    (packed,), ar_lands = _copies_wait(_plan_gather_first, ar_first, rc_d2d[4], "ar_small_wait")
    ar_pass = _copies_start(_plan_gather_pass, 3, [], ar_lands, "ar_small_pass_start")
    rc_ici, rc_mine = reduce_mid(rc_d2d, ar_pass[4], "ab")
    _, (ar_gathered,) = _copies_wait(_plan_gather_pass, ar_pass, rc_ici[4], "ar_small_pass_wait")
    summed = _sum_slots(lax.dynamic_update_index_in_dim(ar_gathered, packed, dev, 0), "ar_small_sum")
    loss = summed[0, 0]

    pieces_l1 = reduce_end(ra_ici, ra_mine, rc_ici[4], "l1")
    res_gla_in = _adamw(gla_w_in, pieces_l1[2], m_gla_w_in, v_gla_w_in, "adamw_gla_in")
    res_gla_out = _adamw(gla_w_out, pieces_l1[3], m_gla_w_out, v_gla_w_out, "adamw_gla_out")
    pieces_mlp0 = reduce_end(rb_ici, rb_mine, res_gla_out[0], "mlp0")
    res_w1 = _adamw_layers(mlp_w1, (pieces_mlp0[0], pieces_l1[0]), m_mlp_w1, v_mlp_w1, "adamw_mlp_w1")
    res_w2 = _adamw_layers(mlp_w2, (pieces_mlp0[1], pieces_l1[1]), m_mlp_w2, v_mlp_w2, "adamw_mlp_w2")
    res = {"mlp_w1": tuple(res_w1), "mlp_w2": tuple(res_w2),
           "gla_w_in": tuple(res_gla_in), "gla_w_out": tuple(res_gla_out),
           "ab_w_out": tuple(_adamw(ab_w_out, pieces_mlp0[2], m_ab_w_out, v_ab_w_out, "adamw_ab_out"))}

    off = SUBLANES
    for n in small_names:
        a = small_full[n]
        gfull = summed[off:off + a.size // LANES].reshape(a.shape)
        off += _part_rows(a)
        local = w_loc[n].shape
        if n in SMALL_SHARDED:
            gfull = lax.dynamic_slice_in_dim(gfull, dev * local[-1], local[-1], axis=gfull.ndim - 1)
        flat = (-1, local[-1])
        outs = _adamw_whole(w_loc[n].reshape(flat), gfull.reshape(flat), m_loc[n].reshape(flat),
                            v_loc[n].reshape(flat), "adamw_" + n)
        res[n] = tuple(o.reshape(local) for o in outs)
    others_done = (res_w1[1][0, 0:SUBLANES, 0:LANES] + res_w2[1][0, 0:SUBLANES, 0:LANES]
                   + res_gla_in[1][0, 0:SUBLANES, 0:LANES])
    pieces_ab = reduce_end(rc_ici, rc_mine, others_done, "ab")
    res["ab_w_in"] = tuple(_adamw(ab_w_in, pieces_ab[0], m_ab_w_in, v_ab_w_in, "adamw_ab_in"))

    grad_x = dx.reshape(1, T, D_MODEL)
    out = [loss, grad_x]
    for k in range(4):
        out += [res[n][k] for n in WEIGHT_NAMES]
    return tuple(out)
```
